```python
import math
import jax, jax.numpy as jnp
from jax import lax
import numpy as np

D_MODEL = 1024
BATCH = 8
SEQ = 4096
DEPTH = 4

N_MIXERS = 2
SSM_EXPAND = 2
D_INNER = SSM_EXPAND * D_MODEL
HEAD_DIM = 64
N_SSM_HEADS = D_INNER // HEAD_DIM
N_GROUPS = 4
HEADS_PER_GROUP = N_SSM_HEADS // N_GROUPS
D_STATE = 128
SSM_CONV = 4
CONV_DIM = D_INNER + 2 * N_GROUPS * D_STATE
IN_PROJ_DIM = 2 * D_INNER + 2 * N_GROUPS * D_STATE + N_SSM_HEADS
CHUNK = 128
CF_KERNEL = 31
N_MEM = 256
XA_HEADS = 4
XA_HEAD_DIM = D_MODEL // XA_HEADS
D_FF = 2816
FFN_CONV = 3
EPS = 1e-6

kernel_name = "hybrid_ssd_conformer_memxattn_convffn"


def rmsnorm(x, g):
    xf = x.astype(jnp.float32)
    y = xf * lax.rsqrt(jnp.mean(xf * xf, axis=-1, keepdims=True) + EPS)
    return (y * g.astype(jnp.float32)).astype(x.dtype)


def layernorm(x, g, b):
    xf = x.astype(jnp.float32)
    mu = jnp.mean(xf, axis=-1, keepdims=True)
    var = jnp.mean(jnp.square(xf - mu), axis=-1, keepdims=True)
    y = (xf - mu) * lax.rsqrt(var + EPS)
    return (y * g.astype(jnp.float32) + b.astype(jnp.float32)).astype(x.dtype)


def causal_dwconv(x, w, b):
    K, C = w.shape
    y = lax.conv_general_dilated(
        x, w[:, None, :].astype(x.dtype), window_strides=(1,), padding=[(K - 1, 0)],
        dimension_numbers=("NWC", "WIO", "NWC"), feature_group_count=C)
    return y + b.astype(x.dtype)


def ssd_mixer(h, in_w, conv_w, conv_b, dt_bias, A_log, D_skip, norm_g, out_w):
    Bsz, L, _ = h.shape
    nc = L // CHUNK
    G, Hg, P, N, Q = N_GROUPS, HEADS_PER_GROUP, HEAD_DIM, D_STATE, CHUNK
    f32 = jnp.float32
    proj = h @ in_w
    z, xBC, dt_raw = jnp.split(proj, [D_INNER, D_INNER + CONV_DIM], axis=-1)
    xBC = jax.nn.silu(causal_dwconv(xBC, conv_w, conv_b))
    xs, Bm, Cm = jnp.split(xBC, [D_INNER, D_INNER + G * N], axis=-1)
    xs = xs.astype(f32).reshape(Bsz, nc, Q, G, Hg, P)
    Bm = Bm.astype(f32).reshape(Bsz, nc, Q, G, N)
    Cm = Cm.astype(f32).reshape(Bsz, nc, Q, G, N)
    dt = jax.nn.softplus(dt_raw.astype(f32) + dt_bias.astype(f32)).reshape(Bsz, nc, Q, G, Hg)
    A = -jnp.exp(A_log.astype(f32)).reshape(G, Hg)
    cs = jnp.cumsum(dt * A, axis=2)
    xdt = xs * dt[..., None]
    tri = jnp.tril(jnp.ones((Q, Q), dtype=bool))
    seg = cs[:, :, :, None] - cs[:, :, None, :]
    decay = jnp.exp(jnp.where(tri[None, None, :, :, None, None], seg, -jnp.inf))
    CB = jnp.einsum("bcign,bcjgn->bcijg", Cm, Bm)
    y_diag = jnp.einsum("bcijgh,bcjghp->bcighp", CB[..., None] * decay, xdt)
    decay_to_end = jnp.exp(cs[:, :, -1:] - cs)
    states = jnp.einsum("bcjgn,bcjgh,bcjghp->bcghpn", Bm, decay_to_end, xdt)
    chunk_decay = jnp.exp(cs[:, :, -1])

    def step(carry, inp):
        st, dec = inp
        return carry * dec[..., None, None] + st, carry

    h0 = jnp.zeros((Bsz, G, Hg, P, N), f32)
    _, prev = lax.scan(step, h0, (jnp.swapaxes(states, 0, 1), jnp.swapaxes(chunk_decay, 0, 1)))
    prev = jnp.swapaxes(prev, 0, 1)
    y_off = jnp.einsum("bcign,bcghpn,bcigh->bcighp", Cm, prev, jnp.exp(cs))
    y = y_diag + y_off + xs * D_skip.astype(f32).reshape(G, Hg)[:, :, None]
    y = y.reshape(Bsz, L, D_INNER).astype(h.dtype)
    y = rmsnorm(y * jax.nn.silu(z), norm_g)
    return y @ out_w


def conformer_conv(h, pw1_w, pw1_b, dw_w, dw_b, ln_g, ln_b, pw2_w, pw2_b):
    u = h @ pw1_w + pw1_b
    a, gt = jnp.split(u, 2, axis=-1)
    c = causal_dwconv(a * jax.nn.sigmoid(gt), dw_w, dw_b)
    c = jax.nn.silu(layernorm(c, ln_g, ln_b))
    return c @ pw2_w + pw2_b


def mem_xattn(h, mem, mem_g, q_w, kv_w, o_w):
    Bsz, L, _ = h.shape
    m = rmsnorm(mem, mem_g)
    q = (h @ q_w).reshape(Bsz, L, XA_HEADS, XA_HEAD_DIM)
    k, v = jnp.split(m @ kv_w, 2, axis=-1)
    k = k.reshape(Bsz, N_MEM, XA_HEADS, XA_HEAD_DIM)
    v = v.reshape(Bsz, N_MEM, XA_HEADS, XA_HEAD_DIM)
    s = jnp.einsum("blhd,bmhd->bhlm", q, k).astype(jnp.float32) * (XA_HEAD_DIM ** -0.5)
    p = jax.nn.softmax(s, axis=-1).astype(v.dtype)
    o = jnp.einsum("bhlm,bmhd->blhd", p, v).reshape(Bsz, L, D_MODEL)
    return o @ o_w


def conv_ffn(h, in_w, conv_w, conv_b, out_w):
    u = causal_dwconv(h @ in_w, conv_w, conv_b)
    g, v = jnp.split(u, 2, axis=-1)
    return (jax.nn.silu(g) * v) @ out_w


def _fwd_setup_inputs(seed: int = 0) -> dict:
    key = jax.random.key(seed)
    ks = jax.random.split(key, 32)
    nA = (DEPTH + 1) // 2
    nB = DEPTH // 2
    D = D_MODEL

    def w(k, shape, fan_in):
        return jax.random.normal(k, shape, jnp.float32) * (fan_in ** -0.5)

    def gain(k, shape):
        return 1.0 + 0.02 * jax.random.normal(k, shape, jnp.float32)

    def bias(k, shape):
        return 0.02 * jax.random.normal(k, shape, jnp.float32)

    log_dt = jax.random.uniform(ks[5], (nA, N_SSM_HEADS), jnp.float32, math.log(1e-3), math.log(1e-1))
    dt0 = jnp.exp(log_dt)
    return {
        "x": jax.random.normal(ks[0], (BATCH, SEQ, D), jnp.float32),
        "mem": jax.random.normal(ks[1], (BATCH, N_MEM, D), jnp.float32),
        "norm_g": gain(ks[2], (DEPTH, 6, D)),
        "ssm_in_w": w(ks[3], (nA, D, IN_PROJ_DIM), D),
        "ssm_conv_w": w(ks[4], (nA, SSM_CONV, CONV_DIM), SSM_CONV),
        "ssm_conv_b": bias(ks[6], (nA, CONV_DIM)),
        "ssm_dt_bias": dt0 + jnp.log(-jnp.expm1(-dt0)),
        "ssm_A_log": jnp.log(jax.random.uniform(ks[7], (nA, N_SSM_HEADS), jnp.float32, 1.0, 16.0)),
        "ssm_D": 1.0 + 0.1 * jax.random.normal(ks[8], (nA, N_SSM_HEADS), jnp.float32),
        "ssm_norm_g": gain(ks[9], (nA, D_INNER)),
        "ssm_out_w": w(ks[10], (nA, D_INNER, D), D_INNER),
        "cf_pw1_w": w(ks[11], (nB, D, 2 * D), D),
        "cf_pw1_b": bias(ks[12], (nB, 2 * D)),
        "cf_dw_w": w(ks[13], (nB, CF_KERNEL, D), CF_KERNEL),
        "cf_dw_b": bias(ks[14], (nB, D)),
        "cf_ln_g": gain(ks[15], (nB, D)),
        "cf_ln_b": bias(ks[16], (nB, D)),
        "cf_pw2_w": w(ks[17], (nB, D, D), D),
        "cf_pw2_b": bias(ks[18], (nB, D)),
        "xa_mem_g": gain(ks[19], (DEPTH, D)),
        "xa_q_w": w(ks[20], (DEPTH, D, D), D),
        "xa_kv_w": w(ks[21], (DEPTH, D, 2 * D), D),
        "xa_o_w": w(ks[22], (DEPTH, D, D), D),
        "ffn_in_w": w(ks[23], (DEPTH, D, 2 * D_FF), D),
        "ffn_conv_w": w(ks[24], (DEPTH, FFN_CONV, 2 * D_FF), FFN_CONV),
        "ffn_conv_b": bias(ks[25], (DEPTH, 2 * D_FF)),
        "ffn_out_w": w(ks[26], (DEPTH, D_FF, D), D_FF),
    }


def _fwd_reference(x, mem, norm_g,
              ssm_in_w, ssm_conv_w, ssm_conv_b, ssm_dt_bias, ssm_A_log, ssm_D, ssm_norm_g, ssm_out_w,
              cf_pw1_w, cf_pw1_b, cf_dw_w, cf_dw_b, cf_ln_g, cf_ln_b, cf_pw2_w, cf_pw2_b,
              xa_mem_g, xa_q_w, xa_kv_w, xa_o_w,
              ffn_in_w, ffn_conv_w, ffn_conv_b, ffn_out_w):
    for i in range(DEPTH):
        g = norm_g[i]
        j = i // N_MIXERS
        h = rmsnorm(x, g[0])
        if i % N_MIXERS == 0:
            mix = ssd_mixer(h, ssm_in_w[j], ssm_conv_w[j], ssm_conv_b[j], ssm_dt_bias[j],
                            ssm_A_log[j], ssm_D[j], ssm_norm_g[j], ssm_out_w[j])
        else:
            mix = conformer_conv(h, cf_pw1_w[j], cf_pw1_b[j], cf_dw_w[j], cf_dw_b[j],
                                 cf_ln_g[j], cf_ln_b[j], cf_pw2_w[j], cf_pw2_b[j])
        x = x + rmsnorm(mix, g[1])
        a = mem_xattn(rmsnorm(x, g[2]), mem, xa_mem_g[i], xa_q_w[i], xa_kv_w[i], xa_o_w[i])
        x = x + rmsnorm(a, g[3])
        f = conv_ffn(rmsnorm(x, g[4]), ffn_in_w[i], ffn_conv_w[i], ffn_conv_b[i], ffn_out_w[i])
        x = x + rmsnorm(f, g[5])
    return x


import jax as _jax
import jax.numpy as _jnp

TWIN_FORMAT = 'train_step'
FWD_PARAMS = ['x', 'mem', 'norm_g', 'ssm_in_w', 'ssm_conv_w', 'ssm_conv_b', 'ssm_dt_bias', 'ssm_A_log', 'ssm_D', 'ssm_norm_g', 'ssm_out_w', 'cf_pw1_w', 'cf_pw1_b', 'cf_dw_w', 'cf_dw_b', 'cf_ln_g', 'cf_ln_b', 'cf_pw2_w', 'cf_pw2_b', 'xa_mem_g', 'xa_q_w', 'xa_kv_w', 'xa_o_w', 'ffn_in_w', 'ffn_conv_w', 'ffn_conv_b', 'ffn_out_w']
TWIN_WEIGHTS = ['norm_g', 'ssm_in_w', 'ssm_conv_w', 'ssm_conv_b', 'ssm_dt_bias', 'ssm_A_log', 'ssm_D', 'ssm_norm_g', 'ssm_out_w', 'cf_pw1_w', 'cf_pw1_b', 'cf_dw_w', 'cf_dw_b', 'cf_ln_g', 'cf_ln_b', 'cf_pw2_w', 'cf_pw2_b', 'xa_mem_g', 'xa_q_w', 'xa_kv_w', 'xa_o_w', 'ffn_in_w', 'ffn_conv_w', 'ffn_conv_b', 'ffn_out_w']
TWIN_DIFF_INPUT = 'x'
TWIN_INPUTS = ['x', 'mem', 'norm_g', 'ssm_in_w', 'ssm_conv_w', 'ssm_conv_b', 'ssm_dt_bias', 'ssm_A_log', 'ssm_D', 'ssm_norm_g', 'ssm_out_w', 'cf_pw1_w', 'cf_pw1_b', 'cf_dw_w', 'cf_dw_b', 'cf_ln_g', 'cf_ln_b', 'cf_pw2_w', 'cf_pw2_b', 'xa_mem_g', 'xa_q_w', 'xa_kv_w', 'xa_o_w', 'ffn_in_w', 'ffn_conv_w', 'ffn_conv_b', 'ffn_out_w', 'loss_target', 'm_norm_g', 'm_ssm_in_w', 'm_ssm_conv_w', 'm_ssm_conv_b', 'm_ssm_dt_bias', 'm_ssm_A_log', 'm_ssm_D', 'm_ssm_norm_g', 'm_ssm_out_w', 'm_cf_pw1_w', 'm_cf_pw1_b', 'm_cf_dw_w', 'm_cf_dw_b', 'm_cf_ln_g', 'm_cf_ln_b', 'm_cf_pw2_w', 'm_cf_pw2_b', 'm_xa_mem_g', 'm_xa_q_w', 'm_xa_kv_w', 'm_xa_o_w', 'm_ffn_in_w', 'm_ffn_conv_w', 'm_ffn_conv_b', 'm_ffn_out_w', 'v_norm_g', 'v_ssm_in_w', 'v_ssm_conv_w', 'v_ssm_conv_b', 'v_ssm_dt_bias', 'v_ssm_A_log', 'v_ssm_D', 'v_ssm_norm_g', 'v_ssm_out_w', 'v_cf_pw1_w', 'v_cf_pw1_b', 'v_cf_dw_w', 'v_cf_dw_b', 'v_cf_ln_g', 'v_cf_ln_b', 'v_cf_pw2_w', 'v_cf_pw2_b', 'v_xa_mem_g', 'v_xa_q_w', 'v_xa_kv_w', 'v_xa_o_w', 'v_ffn_in_w', 'v_ffn_conv_w', 'v_ffn_conv_b', 'v_ffn_out_w']
TWIN_OUTPUTS = ['loss', 'grad_x', 'grad_norm_g', 'grad_ssm_in_w', 'grad_ssm_conv_w', 'grad_ssm_conv_b', 'grad_ssm_dt_bias', 'grad_ssm_A_log', 'grad_ssm_D', 'grad_ssm_norm_g', 'grad_ssm_out_w', 'grad_cf_pw1_w', 'grad_cf_pw1_b', 'grad_cf_dw_w', 'grad_cf_dw_b', 'grad_cf_ln_g', 'grad_cf_ln_b', 'grad_cf_pw2_w', 'grad_cf_pw2_b', 'grad_xa_mem_g', 'grad_xa_q_w', 'grad_xa_kv_w', 'grad_xa_o_w', 'grad_ffn_in_w', 'grad_ffn_conv_w', 'grad_ffn_conv_b', 'grad_ffn_out_w', 'delta_norm_g', 'delta_ssm_in_w', 'delta_ssm_conv_w', 'delta_ssm_conv_b', 'delta_ssm_dt_bias', 'delta_ssm_A_log', 'delta_ssm_D', 'delta_ssm_norm_g', 'delta_ssm_out_w', 'delta_cf_pw1_w', 'delta_cf_pw1_b', 'delta_cf_dw_w', 'delta_cf_dw_b', 'delta_cf_ln_g', 'delta_cf_ln_b', 'delta_cf_pw2_w', 'delta_cf_pw2_b', 'delta_xa_mem_g', 'delta_xa_q_w', 'delta_xa_kv_w', 'delta_xa_o_w', 'delta_ffn_in_w', 'delta_ffn_conv_w', 'delta_ffn_conv_b', 'delta_ffn_out_w', 'new_m_norm_g', 'new_m_ssm_in_w', 'new_m_ssm_conv_w', 'new_m_ssm_conv_b', 'new_m_ssm_dt_bias', 'new_m_ssm_A_log', 'new_m_ssm_D', 'new_m_ssm_norm_g', 'new_m_ssm_out_w', 'new_m_cf_pw1_w', 'new_m_cf_pw1_b', 'new_m_cf_dw_w', 'new_m_cf_dw_b', 'new_m_cf_ln_g', 'new_m_cf_ln_b', 'new_m_cf_pw2_w', 'new_m_cf_pw2_b', 'new_m_xa_mem_g', 'new_m_xa_q_w', 'new_m_xa_kv_w', 'new_m_xa_o_w', 'new_m_ffn_in_w', 'new_m_ffn_conv_w', 'new_m_ffn_conv_b', 'new_m_ffn_out_w', 'new_v_norm_g', 'new_v_ssm_in_w', 'new_v_ssm_conv_w', 'new_v_ssm_conv_b', 'new_v_ssm_dt_bias', 'new_v_ssm_A_log', 'new_v_ssm_D', 'new_v_ssm_norm_g', 'new_v_ssm_out_w', 'new_v_cf_pw1_w', 'new_v_cf_pw1_b', 'new_v_cf_dw_w', 'new_v_cf_dw_b', 'new_v_cf_ln_g', 'new_v_cf_ln_b', 'new_v_cf_pw2_w', 'new_v_cf_pw2_b', 'new_v_xa_mem_g', 'new_v_xa_q_w', 'new_v_xa_kv_w', 'new_v_xa_o_w', 'new_v_ffn_in_w', 'new_v_ffn_conv_w', 'new_v_ffn_conv_b', 'new_v_ffn_out_w']
TWIN_LEAF_KINDS = {'loss': 'loss', 'grad_x': 'grad_x', 'grad_norm_g': 'grad_w', 'grad_ssm_in_w': 'grad_w', 'grad_ssm_conv_w': 'grad_w', 'grad_ssm_conv_b': 'grad_w', 'grad_ssm_dt_bias': 'grad_w', 'grad_ssm_A_log': 'grad_w', 'grad_ssm_D': 'grad_w', 'grad_ssm_norm_g': 'grad_w', 'grad_ssm_out_w': 'grad_w', 'grad_cf_pw1_w': 'grad_w', 'grad_cf_pw1_b': 'grad_w', 'grad_cf_dw_w': 'grad_w', 'grad_cf_dw_b': 'grad_w', 'grad_cf_ln_g': 'grad_w', 'grad_cf_ln_b': 'grad_w', 'grad_cf_pw2_w': 'grad_w', 'grad_cf_pw2_b': 'grad_w', 'grad_xa_mem_g': 'grad_w', 'grad_xa_q_w': 'grad_w', 'grad_xa_kv_w': 'grad_w', 'grad_xa_o_w': 'grad_w', 'grad_ffn_in_w': 'grad_w', 'grad_ffn_conv_w': 'grad_w', 'grad_ffn_conv_b': 'grad_w', 'grad_ffn_out_w': 'grad_w', 'delta_norm_g': 'delta_w', 'delta_ssm_in_w': 'delta_w', 'delta_ssm_conv_w': 'delta_w', 'delta_ssm_conv_b': 'delta_w', 'delta_ssm_dt_bias': 'delta_w', 'delta_ssm_A_log': 'delta_w', 'delta_ssm_D': 'delta_w', 'delta_ssm_norm_g': 'delta_w', 'delta_ssm_out_w': 'delta_w', 'delta_cf_pw1_w': 'delta_w', 'delta_cf_pw1_b': 'delta_w', 'delta_cf_dw_w': 'delta_w', 'delta_cf_dw_b': 'delta_w', 'delta_cf_ln_g': 'delta_w', 'delta_cf_ln_b': 'delta_w', 'delta_cf_pw2_w': 'delta_w', 'delta_cf_pw2_b': 'delta_w', 'delta_xa_mem_g': 'delta_w', 'delta_xa_q_w': 'delta_w', 'delta_xa_kv_w': 'delta_w', 'delta_xa_o_w': 'delta_w', 'delta_ffn_in_w': 'delta_w', 'delta_ffn_conv_w': 'delta_w', 'delta_ffn_conv_b': 'delta_w', 'delta_ffn_out_w': 'delta_w', 'new_m_norm_g': 'new_m', 'new_m_ssm_in_w': 'new_m', 'new_m_ssm_conv_w': 'new_m', 'new_m_ssm_conv_b': 'new_m', 'new_m_ssm_dt_bias': 'new_m', 'new_m_ssm_A_log': 'new_m', 'new_m_ssm_D': 'new_m', 'new_m_ssm_norm_g': 'new_m', 'new_m_ssm_out_w': 'new_m', 'new_m_cf_pw1_w': 'new_m', 'new_m_cf_pw1_b': 'new_m', 'new_m_cf_dw_w': 'new_m', 'new_m_cf_dw_b': 'new_m', 'new_m_cf_ln_g': 'new_m', 'new_m_cf_ln_b': 'new_m', 'new_m_cf_pw2_w': 'new_m', 'new_m_cf_pw2_b': 'new_m', 'new_m_xa_mem_g': 'new_m', 'new_m_xa_q_w': 'new_m', 'new_m_xa_kv_w': 'new_m', 'new_m_xa_o_w': 'new_m', 'new_m_ffn_in_w': 'new_m', 'new_m_ffn_conv_w': 'new_m', 'new_m_ffn_conv_b': 'new_m', 'new_m_ffn_out_w': 'new_m', 'new_v_norm_g': 'new_v', 'new_v_ssm_in_w': 'new_v', 'new_v_ssm_conv_w': 'new_v', 'new_v_ssm_conv_b': 'new_v', 'new_v_ssm_dt_bias': 'new_v', 'new_v_ssm_A_log': 'new_v', 'new_v_ssm_D': 'new_v', 'new_v_ssm_norm_g': 'new_v', 'new_v_ssm_out_w': 'new_v', 'new_v_cf_pw1_w': 'new_v', 'new_v_cf_pw1_b': 'new_v', 'new_v_cf_dw_w': 'new_v', 'new_v_cf_dw_b': 'new_v', 'new_v_cf_ln_g': 'new_v', 'new_v_cf_ln_b': 'new_v', 'new_v_cf_pw2_w': 'new_v', 'new_v_cf_pw2_b': 'new_v', 'new_v_xa_mem_g': 'new_v', 'new_v_xa_q_w': 'new_v', 'new_v_xa_kv_w': 'new_v', 'new_v_xa_o_w': 'new_v', 'new_v_ffn_in_w': 'new_v', 'new_v_ffn_conv_w': 'new_v', 'new_v_ffn_conv_b': 'new_v', 'new_v_ffn_out_w': 'new_v'}


def _forward(args):
    return _fwd_reference(*[args[k] for k in FWD_PARAMS])


def _output_shape():
    out = _jax.eval_shape(lambda: _forward(_fwd_setup_inputs(0)))
    return out.shape, out.dtype

N_MICROBATCH = 1
ADAM_LR = 0.001
ADAM_B1 = 0.9
ADAM_B2 = 0.999
ADAM_EPS = 1e-08
ADAM_WD = 0.01
ADAM_STEP = 10
PER_EXAMPLE_BATCH_AXIS = {'x': 0, 'mem': 0, 'loss_target': 0}
SHARED_INPUTS = []
_WEIGHT_DTYPES = {'norm_g': _jnp.float32, 'ssm_in_w': _jnp.float32, 'ssm_conv_w': _jnp.float32, 'ssm_conv_b': _jnp.float32, 'ssm_dt_bias': _jnp.float32, 'ssm_A_log': _jnp.float32, 'ssm_D': _jnp.float32, 'ssm_norm_g': _jnp.float32, 'ssm_out_w': _jnp.float32, 'cf_pw1_w': _jnp.float32, 'cf_pw1_b': _jnp.float32, 'cf_dw_w': _jnp.float32, 'cf_dw_b': _jnp.float32, 'cf_ln_g': _jnp.float32, 'cf_ln_b': _jnp.float32, 'cf_pw2_w': _jnp.float32, 'cf_pw2_b': _jnp.float32, 'xa_mem_g': _jnp.float32, 'xa_q_w': _jnp.float32, 'xa_kv_w': _jnp.float32, 'xa_o_w': _jnp.float32, 'ffn_in_w': _jnp.float32, 'ffn_conv_w': _jnp.float32, 'ffn_conv_b': _jnp.float32, 'ffn_out_w': _jnp.float32}
MOMENT_SCALE = {'norm_g': 2.428932e+01, 'ssm_in_w': 2.025154e+00, 'ssm_conv_w': 3.020638e+00, 'ssm_conv_b': 8.824344e+00, 'ssm_dt_bias': 5.234118e+00, 'ssm_A_log': 2.870528e+01, 'ssm_D': 1.446590e+01, 'ssm_norm_g': 4.696339e+00, 'ssm_out_w': 7.067668e+00, 'cf_pw1_w': 5.139341e+00, 'cf_pw1_b': 1.826899e+01, 'cf_dw_w': 7.822332e+00, 'cf_dw_b': 4.394429e+01, 'cf_ln_g': 1.940732e+01, 'cf_ln_b': 2.590861e+01, 'cf_pw2_w': 1.314061e+01, 'cf_pw2_b': 5.227719e+01, 'xa_mem_g': 1.764334e+01, 'xa_q_w': 4.866671e+00, 'xa_kv_w': 1.200811e+01, 'xa_o_w': 1.644448e+01, 'ffn_in_w': 2.369047e+00, 'ffn_conv_w': 2.807876e+00, 'ffn_conv_b': 7.887134e+00, 'ffn_out_w': 4.792176e+00}


def _to_microbatches(a, axis):
    t = _jnp.moveaxis(a, axis, 0)
    t = t.reshape((N_MICROBATCH, t.shape[0] // N_MICROBATCH) + t.shape[1:])
    return _jnp.moveaxis(t, 1, axis + 1)


def setup_inputs(seed: int = 0) -> dict:
    inp = _fwd_setup_inputs(seed)
    key = _jax.random.fold_in(_jax.random.key(seed), 7919)
    shape, _ = _output_shape()
    out = dict(inp)
    out["loss_target"] = _jax.random.normal(_jax.random.fold_in(key, 0), shape, _jnp.float32)
    for i, name in enumerate(TWIN_WEIGHTS):
        w = inp[name].astype(_jnp.float32)
        if MOMENT_SCALE is None:
            s = _jnp.sqrt(_jnp.mean(_jnp.square(w)) + 1e-30)
        else:
            s = MOMENT_SCALE[name]
        km, kv = _jax.random.split(_jax.random.fold_in(key, i + 1))
        out[name] = w
        out["m_" + name] = s * _jax.random.normal(km, w.shape, _jnp.float32)
        out["v_" + name] = (s * s) * _jax.random.uniform(kv, w.shape, _jnp.float32, 0.5, 1.5)
    if N_MICROBATCH > 1:
        for name, axis in PER_EXAMPLE_BATCH_AXIS.items():
            out[name] = _to_microbatches(out[name], axis)
    return {'x': out['x'], 'mem': out['mem'], 'norm_g': out['norm_g'], 'ssm_in_w': out['ssm_in_w'], 'ssm_conv_w': out['ssm_conv_w'], 'ssm_conv_b': out['ssm_conv_b'], 'ssm_dt_bias': out['ssm_dt_bias'], 'ssm_A_log': out['ssm_A_log'], 'ssm_D': out['ssm_D'], 'ssm_norm_g': out['ssm_norm_g'], 'ssm_out_w': out['ssm_out_w'], 'cf_pw1_w': out['cf_pw1_w'], 'cf_pw1_b': out['cf_pw1_b'], 'cf_dw_w': out['cf_dw_w'], 'cf_dw_b': out['cf_dw_b'], 'cf_ln_g': out['cf_ln_g'], 'cf_ln_b': out['cf_ln_b'], 'cf_pw2_w': out['cf_pw2_w'], 'cf_pw2_b': out['cf_pw2_b'], 'xa_mem_g': out['xa_mem_g'], 'xa_q_w': out['xa_q_w'], 'xa_kv_w': out['xa_kv_w'], 'xa_o_w': out['xa_o_w'], 'ffn_in_w': out['ffn_in_w'], 'ffn_conv_w': out['ffn_conv_w'], 'ffn_conv_b': out['ffn_conv_b'], 'ffn_out_w': out['ffn_out_w'], 'loss_target': out['loss_target'], 'm_norm_g': out['m_norm_g'], 'm_ssm_in_w': out['m_ssm_in_w'], 'm_ssm_conv_w': out['m_ssm_conv_w'], 'm_ssm_conv_b': out['m_ssm_conv_b'], 'm_ssm_dt_bias': out['m_ssm_dt_bias'], 'm_ssm_A_log': out['m_ssm_A_log'], 'm_ssm_D': out['m_ssm_D'], 'm_ssm_norm_g': out['m_ssm_norm_g'], 'm_ssm_out_w': out['m_ssm_out_w'], 'm_cf_pw1_w': out['m_cf_pw1_w'], 'm_cf_pw1_b': out['m_cf_pw1_b'], 'm_cf_dw_w': out['m_cf_dw_w'], 'm_cf_dw_b': out['m_cf_dw_b'], 'm_cf_ln_g': out['m_cf_ln_g'], 'm_cf_ln_b': out['m_cf_ln_b'], 'm_cf_pw2_w': out['m_cf_pw2_w'], 'm_cf_pw2_b': out['m_cf_pw2_b'], 'm_xa_mem_g': out['m_xa_mem_g'], 'm_xa_q_w': out['m_xa_q_w'], 'm_xa_kv_w': out['m_xa_kv_w'], 'm_xa_o_w': out['m_xa_o_w'], 'm_ffn_in_w': out['m_ffn_in_w'], 'm_ffn_conv_w': out['m_ffn_conv_w'], 'm_ffn_conv_b': out['m_ffn_conv_b'], 'm_ffn_out_w': out['m_ffn_out_w'], 'v_norm_g': out['v_norm_g'], 'v_ssm_in_w': out['v_ssm_in_w'], 'v_ssm_conv_w': out['v_ssm_conv_w'], 'v_ssm_conv_b': out['v_ssm_conv_b'], 'v_ssm_dt_bias': out['v_ssm_dt_bias'], 'v_ssm_A_log': out['v_ssm_A_log'], 'v_ssm_D': out['v_ssm_D'], 'v_ssm_norm_g': out['v_ssm_norm_g'], 'v_ssm_out_w': out['v_ssm_out_w'], 'v_cf_pw1_w': out['v_cf_pw1_w'], 'v_cf_pw1_b': out['v_cf_pw1_b'], 'v_cf_dw_w': out['v_cf_dw_w'], 'v_cf_dw_b': out['v_cf_dw_b'], 'v_cf_ln_g': out['v_cf_ln_g'], 'v_cf_ln_b': out['v_cf_ln_b'], 'v_cf_pw2_w': out['v_cf_pw2_w'], 'v_cf_pw2_b': out['v_cf_pw2_b'], 'v_xa_mem_g': out['v_xa_mem_g'], 'v_xa_q_w': out['v_xa_q_w'], 'v_xa_kv_w': out['v_xa_kv_w'], 'v_xa_o_w': out['v_xa_o_w'], 'v_ffn_in_w': out['v_ffn_in_w'], 'v_ffn_conv_w': out['v_ffn_conv_w'], 'v_ffn_conv_b': out['v_ffn_conv_b'], 'v_ffn_out_w': out['v_ffn_out_w']}


def _loss(weights, diff, rest, loss_target):
    with _jax.named_scope("forward"):
        args = {**rest, TWIN_DIFF_INPUT: diff, **{k: w.astype(_WEIGHT_DTYPES[k]) for k, w in weights.items()}}
        y = _forward(args)
    with _jax.named_scope("loss_head"):
        err = _jnp.square(y.astype(_jnp.float32) - loss_target)
        return 0.5 * _jnp.sum(_jnp.mean(err, axis=-1)) if err.ndim else 0.5 * err


def _adamw(w, g, m, v):
    m = ADAM_B1 * m + (1.0 - ADAM_B1) * g
    v = ADAM_B2 * v + (1.0 - ADAM_B2) * _jnp.square(g)
    m_hat = m / (1.0 - ADAM_B1 ** ADAM_STEP)
    v_hat = v / (1.0 - ADAM_B2 ** ADAM_STEP)
    delta = -ADAM_LR * (m_hat / (_jnp.sqrt(v_hat) + ADAM_EPS) + ADAM_WD * w)
    return delta, m, v


def reference(x, mem, norm_g, ssm_in_w, ssm_conv_w, ssm_conv_b, ssm_dt_bias, ssm_A_log, ssm_D, ssm_norm_g, ssm_out_w, cf_pw1_w, cf_pw1_b, cf_dw_w, cf_dw_b, cf_ln_g, cf_ln_b, cf_pw2_w, cf_pw2_b, xa_mem_g, xa_q_w, xa_kv_w, xa_o_w, ffn_in_w, ffn_conv_w, ffn_conv_b, ffn_out_w, loss_target, m_norm_g, m_ssm_in_w, m_ssm_conv_w, m_ssm_conv_b, m_ssm_dt_bias, m_ssm_A_log, m_ssm_D, m_ssm_norm_g, m_ssm_out_w, m_cf_pw1_w, m_cf_pw1_b, m_cf_dw_w, m_cf_dw_b, m_cf_ln_g, m_cf_ln_b, m_cf_pw2_w, m_cf_pw2_b, m_xa_mem_g, m_xa_q_w, m_xa_kv_w, m_xa_o_w, m_ffn_in_w, m_ffn_conv_w, m_ffn_conv_b, m_ffn_out_w, v_norm_g, v_ssm_in_w, v_ssm_conv_w, v_ssm_conv_b, v_ssm_dt_bias, v_ssm_A_log, v_ssm_D, v_ssm_norm_g, v_ssm_out_w, v_cf_pw1_w, v_cf_pw1_b, v_cf_dw_w, v_cf_dw_b, v_cf_ln_g, v_cf_ln_b, v_cf_pw2_w, v_cf_pw2_b, v_xa_mem_g, v_xa_q_w, v_xa_kv_w, v_xa_o_w, v_ffn_in_w, v_ffn_conv_w, v_ffn_conv_b, v_ffn_out_w):
    given = dict(x=x, mem=mem, norm_g=norm_g, ssm_in_w=ssm_in_w, ssm_conv_w=ssm_conv_w, ssm_conv_b=ssm_conv_b, ssm_dt_bias=ssm_dt_bias, ssm_A_log=ssm_A_log, ssm_D=ssm_D, ssm_norm_g=ssm_norm_g, ssm_out_w=ssm_out_w, cf_pw1_w=cf_pw1_w, cf_pw1_b=cf_pw1_b, cf_dw_w=cf_dw_w, cf_dw_b=cf_dw_b, cf_ln_g=cf_ln_g, cf_ln_b=cf_ln_b, cf_pw2_w=cf_pw2_w, cf_pw2_b=cf_pw2_b, xa_mem_g=xa_mem_g, xa_q_w=xa_q_w, xa_kv_w=xa_kv_w, xa_o_w=xa_o_w, ffn_in_w=ffn_in_w, ffn_conv_w=ffn_conv_w, ffn_conv_b=ffn_conv_b, ffn_out_w=ffn_out_w, loss_target=loss_target, m_norm_g=m_norm_g, m_ssm_in_w=m_ssm_in_w, m_ssm_conv_w=m_ssm_conv_w, m_ssm_conv_b=m_ssm_conv_b, m_ssm_dt_bias=m_ssm_dt_bias, m_ssm_A_log=m_ssm_A_log, m_ssm_D=m_ssm_D, m_ssm_norm_g=m_ssm_norm_g, m_ssm_out_w=m_ssm_out_w, m_cf_pw1_w=m_cf_pw1_w, m_cf_pw1_b=m_cf_pw1_b, m_cf_dw_w=m_cf_dw_w, m_cf_dw_b=m_cf_dw_b, m_cf_ln_g=m_cf_ln_g, m_cf_ln_b=m_cf_ln_b, m_cf_pw2_w=m_cf_pw2_w, m_cf_pw2_b=m_cf_pw2_b, m_xa_mem_g=m_xa_mem_g, m_xa_q_w=m_xa_q_w, m_xa_kv_w=m_xa_kv_w, m_xa_o_w=m_xa_o_w, m_ffn_in_w=m_ffn_in_w, m_ffn_conv_w=m_ffn_conv_w, m_ffn_conv_b=m_ffn_conv_b, m_ffn_out_w=m_ffn_out_w, v_norm_g=v_norm_g, v_ssm_in_w=v_ssm_in_w, v_ssm_conv_w=v_ssm_conv_w, v_ssm_conv_b=v_ssm_conv_b, v_ssm_dt_bias=v_ssm_dt_bias, v_ssm_A_log=v_ssm_A_log, v_ssm_D=v_ssm_D, v_ssm_norm_g=v_ssm_norm_g, v_ssm_out_w=v_ssm_out_w, v_cf_pw1_w=v_cf_pw1_w, v_cf_pw1_b=v_cf_pw1_b, v_cf_dw_w=v_cf_dw_w, v_cf_dw_b=v_cf_dw_b, v_cf_ln_g=v_cf_ln_g, v_cf_ln_b=v_cf_ln_b, v_cf_pw2_w=v_cf_pw2_w, v_cf_pw2_b=v_cf_pw2_b, v_xa_mem_g=v_xa_mem_g, v_xa_q_w=v_xa_q_w, v_xa_kv_w=v_xa_kv_w, v_xa_o_w=v_xa_o_w, v_ffn_in_w=v_ffn_in_w, v_ffn_conv_w=v_ffn_conv_w, v_ffn_conv_b=v_ffn_conv_b, v_ffn_out_w=v_ffn_out_w)
    weights = {n: given[n] for n in TWIN_WEIGHTS}
    shared = {n: given[n] for n in SHARED_INPUTS}
    per_example = {n: given[n] for n in ['x', 'mem']}
    grad_fn = _jax.value_and_grad(_loss, argnums=(0, 1))

    def one_microbatch(ex, loss_target):
        ex = dict(ex)
        diff = ex.pop(TWIN_DIFF_INPUT)
        return grad_fn(weights, diff, {**shared, **ex}, loss_target)

    if N_MICROBATCH == 1:
        loss, (grad_w, grad_x) = one_microbatch(per_example, given["loss_target"])
    else:
        def body(carry, xs):
            loss_sum, grad_sum = carry
            l_k, (gw_k, gx_k) = one_microbatch(xs[0], xs[1])
            with _jax.named_scope("update"):
                return (loss_sum + l_k, _jax.tree.map(_jnp.add, grad_sum, gw_k)), gx_k

        init = (_jnp.zeros((), _jnp.float32), _jax.tree.map(_jnp.zeros_like, weights))
        (loss, grad_w), grad_x = _jax.lax.scan(body, init, (per_example, given["loss_target"]))
    with _jax.named_scope("update"):
        delta_w, new_m, new_v = {}, {}, {}
        for n in TWIN_WEIGHTS:
            delta_w[n], new_m[n], new_v[n] = _adamw(weights[n], grad_w[n], given["m_" + n], given["v_" + n])
    return (loss, grad_x, *[grad_w[n] for n in TWIN_WEIGHTS], *[delta_w[n] for n in TWIN_WEIGHTS],
            *[new_m[n] for n in TWIN_WEIGHTS], *[new_v[n] for n in TWIN_WEIGHTS])
```

```python
import functools

import jax
import jax.numpy as jnp
from jax import lax
from jax.experimental import pallas as pl
from jax.experimental.pallas import tpu as pltpu

F32 = jnp.float32
BF16 = jnp.bfloat16

D_MODEL = 1024
D_INNER = 2048
N_HEADS = 32
HEAD_DIM = 64
N_GROUPS = 4
D_STATE = 128
CHUNK = 128
CONV_DIM = 3072
SSM_K = 4
CF_K = 31
N_MEM = 256
XA_HEADS = 4
XA_HD = 256
D_FF = 2816
FFN_K = 3
EPS = 1e-6
DEPTH = 4
N_DEV = 8

ADAM_LR = 0.001
ADAM_B1 = 0.9
ADAM_B2 = 0.999
ADAM_EPS = 1e-08
ADAM_WD = 0.01
ADAM_STEP = 10

LANE = 128
VMEM_LIMIT = 56 * 1024 * 1024
NEG = -1e30
MESH = pl.DeviceIdType.MESH


def _cp(*sem):
    return pltpu.CompilerParams(dimension_semantics=sem if sem else None, vmem_limit_bytes=VMEM_LIMIT)


def _tile(n, cap):
    if n <= cap:
        return n
    best = 0
    for t in range(LANE, cap + 1, LANE):
        if n % t == 0:
            best = t
    assert best, (n, cap)
    return best


def _sig(x):
    return 1.0 / (1.0 + jnp.exp(-x))


def _split3(v):
    v0 = v.astype(BF16)
    r1 = v - v0.astype(F32)
    v1 = r1.astype(BF16)
    v2 = (r1 - v1.astype(F32)).astype(BF16)
    return v0, v1, v2


def _dot(a, b, ca=1, cb=0):
    return lax.dot_general(a, b, (((ca,), (cb,)), ((), ())), preferred_element_type=F32)


def _dot3(v, m, ca=1, cb=0):
    v0, v1, v2 = _split3(v)
    return _dot(v0, m, ca, cb) + _dot(v1, m, ca, cb) + _dot(v2, m, ca, cb)


def mm(a, b, *, ta=False, tb=False, bias=None, acc=None, out_dtype=F32, layer=None, into=None, name):
    if ta:
        K, M = a.shape
    else:
        M, K = a.shape
    if tb:
        N, K2 = b.shape[-2:]
    else:
        K2, N = b.shape[-2:]
    assert K == K2, (a.shape, b.shape, ta, tb)
    assert (b.ndim == 3) == (layer is not None)
    tm = _tile(M, 512)
    tn = _tile(N, 1536)
    tk = _tile(K, 2048)
    nk = K // tk
    has_bias, has_acc = bias is not None, acc is not None
    if into is not None:
        out_dtype = into[0].dtype
        assert into[0].shape[1:] == (M, N) and not has_acc

    def body(*refs):
        a_ref, b_ref = refs[0], refs[1]
        pos = 2
        bias_ref = acc_ref = None
        if has_bias:
            bias_ref = refs[pos]
            pos += 1
        if has_acc:
            acc_ref = refs[pos]
            pos += 1
        if into is not None:
            pos += 1
        o_ref = refs[pos]
        s_ref = refs[pos + 1] if nk > 1 else None
        p = _dot(a_ref[...].astype(BF16), b_ref[...].astype(BF16), 0 if ta else 1, 1 if tb else 0)

        def extras(v):
            if has_bias:
                v = v + bias_ref[...]
            if has_acc:
                v = v + acc_ref[...]
            return v

        if nk == 1:
            o_ref[...] = extras(p).astype(out_dtype)
        else:
            k = pl.program_id(2)

            @pl.when(k == 0)
            def _():
                s_ref[...] = extras(p)

            @pl.when(k > 0)
            def _():
                s_ref[...] += p

            @pl.when(k == nk - 1)
            def _():
                o_ref[...] = s_ref[...].astype(out_dtype)

    a_spec = pl.BlockSpec((tk, tm), lambda i, j, k: (k, i)) if ta else pl.BlockSpec((tm, tk), lambda i, j, k: (i, k))
    if layer is None:
        b_spec = pl.BlockSpec((tn, tk), lambda i, j, k: (j, k)) if tb else pl.BlockSpec((tk, tn), lambda i, j, k: (k, j))
    elif tb:
        b_spec = pl.BlockSpec((None, tn, tk), lambda i, j, k: (layer, j, k))
    else:
        b_spec = pl.BlockSpec((None, tk, tn), lambda i, j, k: (layer, k, j))
    in_specs, args = [a_spec, b_spec], [a, b]
    if has_bias:
        in_specs.append(pl.BlockSpec((1, tn), lambda i, j, k: (0, j)))
        args.append(bias.reshape(1, N).astype(F32))
    if has_acc:
        in_specs.append(pl.BlockSpec((tm, tn), lambda i, j, k: (i, j)))
        args.append(acc)
    if into is None:
        out_spec = pl.BlockSpec((tm, tn), lambda i, j, k: (i, j))
        out_shape = jax.ShapeDtypeStruct((M, N), out_dtype)
        aliases = {}
    else:
        buf, l = into
        in_specs.append(pl.BlockSpec(memory_space=pl.ANY))
        args.append(buf)
        out_spec = pl.BlockSpec((None, tm, tn), lambda i, j, k: (l, i, j))
        out_shape = jax.ShapeDtypeStruct(buf.shape, buf.dtype)
        aliases = {len(args) - 1: 0}
    return pl.pallas_call(
        body, name=name, grid=(M // tm, N // tn, nk),
        in_specs=in_specs, out_specs=out_spec, out_shape=out_shape, input_output_aliases=aliases,
        scratch_shapes=[pltpu.VMEM((tm, tn), F32)] if nk > 1 else [],
        compiler_params=_cp("parallel", "parallel", "arbitrary"),
    )(*args)


def colsum(x, name):
    L, C = x.shape
    tr = _tile(L, 512)
    tc = _tile(C, 1024)

    def body(x_ref, o_ref):
        @pl.when(pl.program_id(1) == 0)
        def _():
            o_ref[...] = jnp.zeros_like(o_ref)

        o_ref[...] += jnp.sum(x_ref[...].astype(F32), axis=0, keepdims=True)

    return pl.pallas_call(
        body, name=name, grid=(C // tc, L // tr),
        in_specs=[pl.BlockSpec((tr, tc), lambda j, i: (i, j))],
        out_specs=pl.BlockSpec((1, tc), lambda j, i: (0, j)),
        out_shape=jax.ShapeDtypeStruct((1, C), F32),
        compiler_params=_cp("parallel", "arbitrary"),
    )(x)


TR = 256


def _rows(L):
    return _tile(L, TR) if L % TR else TR


def _row_spec(tr, w):
    return pl.BlockSpec((tr, w), lambda i: (i, 0))


def _vec_spec(w):
    return pl.BlockSpec((1, w), lambda i: (0, 0))


def _rms(v):
    return lax.rsqrt(jnp.mean(v * v, axis=-1, keepdims=True) + EPS)


def norm_fwd(x, g, name):
    L, D = x.shape
    tr = min(TR, L)

    def body(x_ref, g_ref, h_ref):
        xv = x_ref[...]
        h_ref[...] = (xv * _rms(xv) * g_ref[...]).astype(BF16)

    return pl.pallas_call(
        body, name=name, grid=(L // tr,),
        in_specs=[_row_spec(tr, D), _vec_spec(D)], out_specs=_row_spec(tr, D),
        out_shape=jax.ShapeDtypeStruct((L, D), BF16), compiler_params=_cp("parallel"),
    )(x, g.reshape(1, D))


def bnd_fwd(x, f, gpost, gpre, name):
    L, D = x.shape
    tr = min(TR, L)

    def body(x_ref, f_ref, gp_ref, gn_ref, xo_ref, h_ref):
        fv = f_ref[...]
        xn = x_ref[...] + fv * _rms(fv) * gp_ref[...]
        xo_ref[...] = xn
        h_ref[...] = (xn * _rms(xn) * gn_ref[...]).astype(BF16)

    return pl.pallas_call(
        body, name=name, grid=(L // tr,),
        in_specs=[_row_spec(tr, D), _row_spec(tr, D), _vec_spec(D), _vec_spec(D)],
        out_specs=[_row_spec(tr, D), _row_spec(tr, D)],
        out_shape=[jax.ShapeDtypeStruct((L, D), F32), jax.ShapeDtypeStruct((L, D), BF16)],
        compiler_params=_cp("parallel"),
    )(x, f, gpost.reshape(1, D), gpre.reshape(1, D))


def final_fwd(x, f, gpost, target, name):
    L, D = x.shape
    tr = min(TR, L)
    n = L // tr

    def body(x_ref, f_ref, gp_ref, t_ref, dy_ref, loss_ref, acc_ref):
        i = pl.program_id(0)

        @pl.when(i == 0)
        def _():
            acc_ref[...] = jnp.zeros_like(acc_ref)

        fv = f_ref[...]
        e = x_ref[...] + fv * _rms(fv) * gp_ref[...] - t_ref[...]
        dy_ref[...] = e * (1.0 / D)
        acc_ref[...] += jnp.sum(e * e, axis=0, keepdims=True)

        @pl.when(i == n - 1)
        def _():
            loss_ref[...] = jnp.full((1, LANE), 0.5 / D, F32) * jnp.sum(acc_ref[...])

    return pl.pallas_call(
        body, name=name, grid=(n,),
        in_specs=[_row_spec(tr, D), _row_spec(tr, D), _vec_spec(D), _row_spec(tr, D)],
        out_specs=[_row_spec(tr, D), _vec_spec(LANE)],
        out_shape=[jax.ShapeDtypeStruct((L, D), F32), jax.ShapeDtypeStruct((1, LANE), F32)],
        scratch_shapes=[pltpu.VMEM((1, D), F32)],
        compiler_params=_cp("arbitrary"),
    )(x, f, gpost.reshape(1, D), target)


def _rms_bwd(v, g, dy):
    r = _rms(v)
    vn = v * r
    dg = jnp.sum(dy * vn, axis=0, keepdims=True)
    dvn = dy * g
    dv = r * (dvn - vn * jnp.mean(dvn * vn, axis=-1, keepdims=True))
    return dv, dg


def bnd_bwd(dxp, *, pre=None, post=None, name):
    L, D = dxp.shape
    tr = min(TR, L)
    has_pre, has_post = pre is not None, post is not None

    def body(*refs):
        pos = 0
        dxp_ref = refs[pos]; pos += 1
        if has_pre:
            x_ref, gpre_ref, dh_ref = refs[pos:pos + 3]; pos += 3
        if has_post:
            f_ref, gpost_ref = refs[pos:pos + 2]; pos += 2
        if has_pre:
            dx_ref, dgpre_ref = refs[pos:pos + 2]; pos += 2
        if has_post:
            df_ref, dgpost_ref = refs[pos:pos + 2]; pos += 2
        i = pl.program_id(0)
        dx = dxp_ref[...]
        if has_pre:
            d, dg = _rms_bwd(x_ref[...], gpre_ref[...], dh_ref[...])
            dx = dx + d
            dx_ref[...] = dx

            @pl.when(i == 0)
            def _():
                dgpre_ref[...] = jnp.zeros_like(dgpre_ref)

            dgpre_ref[...] += dg
        if has_post:
            d, dg = _rms_bwd(f_ref[...], gpost_ref[...], dx)
            df_ref[...] = d.astype(BF16)

            @pl.when(i == 0)
            def _():
                dgpost_ref[...] = jnp.zeros_like(dgpost_ref)

            dgpost_ref[...] += dg

    in_specs, args = [_row_spec(tr, D)], [dxp]
    out_specs, out_shape, names = [], [], []
    if has_pre:
        x, gpre, dh = pre
        in_specs += [_row_spec(tr, D), _vec_spec(D), _row_spec(tr, D)]
        args += [x, gpre.reshape(1, D), dh]
        out_specs += [_row_spec(tr, D), _vec_spec(D)]
        out_shape += [jax.ShapeDtypeStruct((L, D), F32), jax.ShapeDtypeStruct((1, D), F32)]
        names += ["dx", "dgpre"]
    if has_post:
        f, gpost = post
        in_specs += [_row_spec(tr, D), _vec_spec(D)]
        args += [f, gpost.reshape(1, D)]
        out_specs += [_row_spec(tr, D), _vec_spec(D)]
        out_shape += [jax.ShapeDtypeStruct((L, D), BF16), jax.ShapeDtypeStruct((1, D), F32)]
        names += ["df", "dgpost"]
    outs = pl.pallas_call(
        body, name=name, grid=(L // tr,), in_specs=in_specs, out_specs=out_specs, out_shape=out_shape,
        compiler_params=_cp("arbitrary"),
    )(*args)
    return dict(zip(names, outs))


def norm_dg(x, dy, name):
    L, D = x.shape
    tr = min(TR, L)

    def body(x_ref, dy_ref, o_ref):
        @pl.when(pl.program_id(0) == 0)
        def _():
            o_ref[...] = jnp.zeros_like(o_ref)

        xv = x_ref[...]
        o_ref[...] += jnp.sum(dy_ref[...] * xv * _rms(xv), axis=0, keepdims=True)

    return pl.pallas_call(
        body, name=name, grid=(L // tr,),
        in_specs=[_row_spec(tr, D), _row_spec(tr, D)], out_specs=_vec_spec(D),
        out_shape=jax.ShapeDtypeStruct((1, D), F32), compiler_params=_cp("arbitrary"),
    )(x, dy)


HALO = 32


def _prev_halo_spec(tr, tc, col):
    per = tr // HALO
    return pl.BlockSpec((HALO, tc), lambda *g: (jnp.maximum(g[-1] * per - 1, 0), col(*g)))


def _next_halo_spec(tr, tc, col, n_rows):
    per = tr // HALO
    last = n_rows // HALO - 1
    return pl.BlockSpec((HALO, tc), lambda *g: (jnp.minimum((g[-1] + 1) * per, last), col(*g)))


def _fill_prev(scr, halo_val, blk_val, i, tr):
    scr[pl.ds(0, HALO), :] = jnp.where(i == 0, 0.0, halo_val)
    scr[pl.ds(HALO, tr), :] = blk_val


def _conv(scr, w_ref, K, tr):
    acc = None
    for k in range(K):
        term = scr[pl.ds(HALO - (K - 1) + k, tr), :] * w_ref[k:k + 1, :]
        acc = term if acc is None else acc + term
    return acc


def _conv_dw(scr, d, o_ref, K, tr):
    for k in range(K):
        o_ref[k:k + 1, :] += jnp.sum(d * scr[pl.ds(HALO - (K - 1) + k, tr), :], axis=0, keepdims=True)


def dwconv_t(dy, w, K, name):
    L, C = dy.shape
    tr = min(TR, L)
    tc = _tile(C, 1024)
    n = L // tr

    def body(dy_ref, nx_ref, w_ref, o_ref, scr):
        i = pl.program_id(1)
        scr[pl.ds(0, tr), :] = dy_ref[...].astype(F32)
        scr[pl.ds(tr, HALO), :] = jnp.where(i == n - 1, 0.0, nx_ref[...].astype(F32))
        acc = None
        for k in range(K):
            term = scr[pl.ds(K - 1 - k, tr), :] * w_ref[k:k + 1, :]
            acc = term if acc is None else acc + term
        o_ref[...] = acc.astype(BF16)

    return pl.pallas_call(
        body, name=name, grid=(C // tc, n),
        in_specs=[pl.BlockSpec((tr, tc), lambda j, i: (i, j)),
                  _next_halo_spec(tr, tc, lambda j, i: j, L),
                  pl.BlockSpec((K, tc), lambda j, i: (0, j))],
        out_specs=pl.BlockSpec((tr, tc), lambda j, i: (i, j)),
        out_shape=jax.ShapeDtypeStruct((L, C), BF16),
        scratch_shapes=[pltpu.VMEM((tr + HALO, tc), F32)],
        compiler_params=_cp("parallel", "parallel"),
    )(dy, dy, w)


def ssm_conv_fwd(zx, w, b, *, col0, ncols, wcol0, out_dtype, name):
    L = zx.shape[0]
    tr = min(TR, L)
    tc = 1024
    cb, wb = col0 // tc, wcol0 // tc

    def body(x_ref, h_ref, w_ref, b_ref, o_ref, scr):
        i = pl.program_id(1)
        _fill_prev(scr, h_ref[...].astype(F32), x_ref[...].astype(F32), i, tr)
        pre = _conv(scr, w_ref, SSM_K, tr) + b_ref[...]
        o_ref[...] = (pre * _sig(pre)).astype(out_dtype)

    return pl.pallas_call(
        body, name=name, grid=(ncols // tc, L // tr),
        in_specs=[pl.BlockSpec((tr, tc), lambda j, i: (i, j + cb)),
                  _prev_halo_spec(tr, tc, lambda j, i: j + cb),
                  pl.BlockSpec((SSM_K, tc), lambda j, i: (0, j + wb)),
                  pl.BlockSpec((1, tc), lambda j, i: (0, j + wb))],
        out_specs=pl.BlockSpec((tr, tc), lambda j, i: (i, j)),
        out_shape=jax.ShapeDtypeStruct((L, ncols), out_dtype),
        scratch_shapes=[pltpu.VMEM((HALO + tr, tc), F32)],
        compiler_params=_cp("parallel", "parallel"),
    )(zx, zx, w, b)


def ssm_conv_bwd(zx, d, w, b, name):
    L = zx.shape[0]
    tr = min(TR, L)
    tc = 1024
    cb = D_INNER // tc

    def body(x_ref, h_ref, d_ref, w_ref, b_ref, dp_ref, dw_ref, db_ref, scr):
        i = pl.program_id(1)
        _fill_prev(scr, h_ref[...].astype(F32), x_ref[...].astype(F32), i, tr)
        pre = _conv(scr, w_ref, SSM_K, tr) + b_ref[...]
        s = _sig(pre)
        dp = d_ref[...] * s * (1.0 + pre * (1.0 - s))
        dp_ref[...] = dp.astype(BF16)

        @pl.when(i == 0)
        def _():
            dw_ref[...] = jnp.zeros_like(dw_ref)
            db_ref[...] = jnp.zeros_like(db_ref)

        _conv_dw(scr, dp, dw_ref, SSM_K, tr)
        db_ref[...] += jnp.sum(dp, axis=0, keepdims=True)

    return pl.pallas_call(
        body, name=name, grid=(CONV_DIM // tc, L // tr),
        in_specs=[pl.BlockSpec((tr, tc), lambda j, i: (i, j + cb)),
                  _prev_halo_spec(tr, tc, lambda j, i: j + cb),
                  pl.BlockSpec((tr, tc), lambda j, i: (i, j)),
                  pl.BlockSpec((SSM_K, tc), lambda j, i: (0, j)),
                  pl.BlockSpec((1, tc), lambda j, i: (0, j))],
        out_specs=[pl.BlockSpec((tr, tc), lambda j, i: (i, j)),
                   pl.BlockSpec((SSM_K, tc), lambda j, i: (0, j)),
                   pl.BlockSpec((1, tc), lambda j, i: (0, j))],
        out_shape=[jax.ShapeDtypeStruct((L, CONV_DIM), BF16),
                   jax.ShapeDtypeStruct((SSM_K, CONV_DIM), F32),
                   jax.ShapeDtypeStruct((1, CONV_DIM), F32)],
        scratch_shapes=[pltpu.VMEM((HALO + tr, tc), F32)],
        compiler_params=_cp("parallel", "arbitrary"),
    )(zx, zx, d, w, b)


FFN_TC = 1408


def ffn_act_fwd(u, w, b, name):
    L = u.shape[0]
    tr = min(TR, L)
    tc = FFN_TC
    nb = D_FF // tc

    def body(g_ref, hg_ref, v_ref, hv_ref, wg_ref, wv_ref, bg_ref, bv_ref, o_ref, sg, sv):
        i = pl.program_id(1)
        _fill_prev(sg, hg_ref[...].astype(F32), g_ref[...].astype(F32), i, tr)
        _fill_prev(sv, hv_ref[...].astype(F32), v_ref[...].astype(F32), i, tr)
        ug = _conv(sg, wg_ref, FFN_K, tr) + bg_ref[...]
        uv = _conv(sv, wv_ref, FFN_K, tr) + bv_ref[...]
        o_ref[...] = (ug * _sig(ug) * uv).astype(BF16)

    blk = lambda off: pl.BlockSpec((tr, tc), lambda j, i: (i, j + off))
    wsp = lambda off: pl.BlockSpec((FFN_K, tc), lambda j, i: (0, j + off))
    bsp = lambda off: pl.BlockSpec((1, tc), lambda j, i: (0, j + off))
    return pl.pallas_call(
        body, name=name, grid=(nb, L // tr),
        in_specs=[blk(0), _prev_halo_spec(tr, tc, lambda j, i: j),
                  blk(nb), _prev_halo_spec(tr, tc, lambda j, i: j + nb),
                  wsp(0), wsp(nb), bsp(0), bsp(nb)],
        out_specs=pl.BlockSpec((tr, tc), lambda j, i: (i, j)),
        out_shape=jax.ShapeDtypeStruct((L, D_FF), BF16),
        scratch_shapes=[pltpu.VMEM((HALO + tr, tc), F32), pltpu.VMEM((HALO + tr, tc), F32)],
        compiler_params=_cp("parallel", "parallel"),
    )(u, u, u, u, w, w, b, b)


def ffn_act_bwd(u, dact, w, b, name):
    L = u.shape[0]
    tr = min(TR, L)
    tc = FFN_TC
    nb = D_FF // tc

    def body(s_ref, hs_ref, o_ref, ho_ref, da_ref, ws_ref, wo_ref, bs_ref, bo_ref, d_ref, dw_ref, db_ref, ss, so):
        half = pl.program_id(0)
        i = pl.program_id(2)
        _fill_prev(ss, hs_ref[...].astype(F32), s_ref[...].astype(F32), i, tr)
        _fill_prev(so, ho_ref[...].astype(F32), o_ref[...].astype(F32), i, tr)
        us = _conv(ss, ws_ref, FFN_K, tr) + bs_ref[...]
        uo = _conv(so, wo_ref, FFN_K, tr) + bo_ref[...]
        da = da_ref[...].astype(F32)
        sg_s, sg_o = _sig(us), _sig(uo)
        d_gate = da * uo * sg_s * (1.0 + us * (1.0 - sg_s))
        d_val = da * uo * sg_o
        d = jnp.where(half == 0, d_gate, d_val)
        d_ref[...] = d.astype(BF16)

        @pl.when(i == 0)
        def _():
            dw_ref[...] = jnp.zeros_like(dw_ref)
            db_ref[...] = jnp.zeros_like(db_ref)

        _conv_dw(ss, d, dw_ref, FFN_K, tr)
        db_ref[...] += jnp.sum(d, axis=0, keepdims=True)

    me = lambda h, j, i: h * nb + j
    other = lambda h, j, i: (1 - h) * nb + j
    blk = lambda col: pl.BlockSpec((tr, tc), lambda h, j, i: (i, col(h, j, i)))
    wsp = lambda col: pl.BlockSpec((FFN_K, tc), lambda h, j, i: (0, col(h, j, i)))
    bsp = lambda col: pl.BlockSpec((1, tc), lambda h, j, i: (0, col(h, j, i)))
    return pl.pallas_call(
        body, name=name, grid=(2, nb, L // tr),
        in_specs=[blk(me), _prev_halo_spec(tr, tc, me), blk(other), _prev_halo_spec(tr, tc, other),
                  pl.BlockSpec((tr, tc), lambda h, j, i: (i, j)),
                  wsp(me), wsp(other), bsp(me), bsp(other)],
        out_specs=[blk(me), wsp(me), bsp(me)],
        out_shape=[jax.ShapeDtypeStruct((L, 2 * D_FF), BF16),
                   jax.ShapeDtypeStruct((FFN_K, 2 * D_FF), F32),
                   jax.ShapeDtypeStruct((1, 2 * D_FF), F32)],
        scratch_shapes=[pltpu.VMEM((HALO + tr, tc), F32), pltpu.VMEM((HALO + tr, tc), F32)],
        compiler_params=_cp("parallel", "parallel", "arbitrary"),
    )(u, u, u, u, dact, w, w, b, b)


def _ln_stats(c):
    mu = jnp.mean(c, axis=-1, keepdims=True)
    cc = c - mu
    rstd = lax.rsqrt(jnp.mean(cc * cc, axis=-1, keepdims=True) + EPS)
    return cc * rstd, rstd


def cf_fwd(u, dw_w, dw_b, ln_g, ln_b, name):
    L = u.shape[0]
    D = D_MODEL
    tr = min(TR, L)

    def body(a_ref, ha_ref, g_ref, hg_ref, w_ref, b_ref, lg_ref, lb_ref, c_ref, s_ref, scr):
        i = pl.program_id(0)
        glu_h = ha_ref[...].astype(F32) * _sig(hg_ref[...].astype(F32))
        glu = a_ref[...].astype(F32) * _sig(g_ref[...].astype(F32))
        _fill_prev(scr, glu_h, glu, i, tr)
        c = _conv(scr, w_ref, CF_K, tr) + b_ref[...]
        c_ref[...] = c
        xhat, _ = _ln_stats(c)
        ln = xhat * lg_ref[...] + lb_ref[...]
        s_ref[...] = (ln * _sig(ln)).astype(BF16)

    per = tr // HALO
    halo = lambda col: pl.BlockSpec((HALO, D), lambda i: (jnp.maximum(i * per - 1, 0), col))
    return pl.pallas_call(
        body, name=name, grid=(L // tr,),
        in_specs=[pl.BlockSpec((tr, D), lambda i: (i, 0)), halo(0),
                  pl.BlockSpec((tr, D), lambda i: (i, 1)), halo(1),
                  pl.BlockSpec((CF_K, D), lambda i: (0, 0)), _vec_spec(D), _vec_spec(D), _vec_spec(D)],
        out_specs=[_row_spec(tr, D), _row_spec(tr, D)],
        out_shape=[jax.ShapeDtypeStruct((L, D), F32), jax.ShapeDtypeStruct((L, D), BF16)],
        scratch_shapes=[pltpu.VMEM((HALO + tr, D), F32)],
        compiler_params=_cp("parallel"),
    )(u, u, u, u, dw_w, dw_b, ln_g, ln_b)


def cf_bwd_ln(c, ds, ln_g, ln_b, name):
    L, D = c.shape
    tr = min(TR, L)

    def body(c_ref, ds_ref, lg_ref, lb_ref, dc_ref, dg_ref, db_ref):
        xhat, rstd = _ln_stats(c_ref[...])
        ln = xhat * lg_ref[...] + lb_ref[...]
        sg = _sig(ln)
        dln = ds_ref[...].astype(F32) * sg * (1.0 + ln * (1.0 - sg))

        @pl.when(pl.program_id(0) == 0)
        def _():
            dg_ref[...] = jnp.zeros_like(dg_ref)
            db_ref[...] = jnp.zeros_like(db_ref)

        dg_ref[...] += jnp.sum(dln * xhat, axis=0, keepdims=True)
        db_ref[...] += jnp.sum(dln, axis=0, keepdims=True)
        dxh = dln * lg_ref[...]
        dc_ref[...] = rstd * (dxh - jnp.mean(dxh, axis=-1, keepdims=True)
                              - xhat * jnp.mean(dxh * xhat, axis=-1, keepdims=True))

    return pl.pallas_call(
        body, name=name, grid=(L // tr,),
        in_specs=[_row_spec(tr, D), _row_spec(tr, D), _vec_spec(D), _vec_spec(D)],
        out_specs=[_row_spec(tr, D), _vec_spec(D), _vec_spec(D)],
        out_shape=[jax.ShapeDtypeStruct((L, D), F32), jax.ShapeDtypeStruct((1, D), F32),
                   jax.ShapeDtypeStruct((1, D), F32)],
        compiler_params=_cp("arbitrary"),
    )(c, ds, ln_g, ln_b)


def cf_bwd_conv(u, dc, dw_w, name):
    L = u.shape[0]
    D = D_MODEL
    tr = min(TR, L)
    n = L // tr

    def body(a_ref, ha_ref, g_ref, hg_ref, dc_ref, nx_ref, w_ref, du_ref, dw_ref, db_ref, sx, sd):
        i = pl.program_id(0)
        a = a_ref[...].astype(F32)
        sg = _sig(g_ref[...].astype(F32))
        _fill_prev(sx, ha_ref[...].astype(F32) * _sig(hg_ref[...].astype(F32)), a * sg, i, tr)
        dcv = dc_ref[...]
        sd[pl.ds(0, tr), :] = dcv
        sd[pl.ds(tr, HALO), :] = jnp.where(i == n - 1, 0.0, nx_ref[...])
        dglu = None
        for k in range(CF_K):
            term = sd[pl.ds(CF_K - 1 - k, tr), :] * w_ref[k:k + 1, :]
            dglu = term if dglu is None else dglu + term
        du_ref[:, 0:D] = (dglu * sg).astype(BF16)
        du_ref[:, D:2 * D] = (dglu * a * sg * (1.0 - sg)).astype(BF16)

        @pl.when(i == 0)
        def _():
            dw_ref[...] = jnp.zeros_like(dw_ref)
            db_ref[...] = jnp.zeros_like(db_ref)

        _conv_dw(sx, dcv, dw_ref, CF_K, tr)
        db_ref[...] += jnp.sum(dcv, axis=0, keepdims=True)

    per = tr // HALO
    last = L // HALO - 1
    halo = lambda col: pl.BlockSpec((HALO, D), lambda i: (jnp.maximum(i * per - 1, 0), col))
    return pl.pallas_call(
        body, name=name, grid=(n,),
        in_specs=[pl.BlockSpec((tr, D), lambda i: (i, 0)), halo(0),
                  pl.BlockSpec((tr, D), lambda i: (i, 1)), halo(1),
                  _row_spec(tr, D),
                  pl.BlockSpec((HALO, D), lambda i: (jnp.minimum((i + 1) * per, last), 0)),
                  pl.BlockSpec((CF_K, D), lambda i: (0, 0))],
        out_specs=[pl.BlockSpec((tr, 2 * D), lambda i: (i, 0)),
                   pl.BlockSpec((CF_K, D), lambda i: (0, 0)), _vec_spec(D)],
        out_shape=[jax.ShapeDtypeStruct((L, 2 * D), BF16), jax.ShapeDtypeStruct((CF_K, D), F32),
                   jax.ShapeDtypeStruct((1, D), F32)],
        scratch_shapes=[pltpu.VMEM((HALO + tr, D), F32), pltpu.VMEM((tr + HALO, D), F32)],
        compiler_params=_cp("arbitrary"),
    )(u, u, u, u, dc, dc, dw_w)


XA_TR = 512
XA_SCALE = XA_HD ** -0.5


def _xa_probs(qh, kh):
    s = _dot(qh, kh, 1, 1) * XA_SCALE
    p = jnp.exp(s - jnp.max(s, axis=-1, keepdims=True))
    return p / jnp.sum(p, axis=-1, keepdims=True)


def attn_fwd(q, kv, name):
    L, D = q.shape
    tr = min(XA_TR, L)

    def body(q_ref, kv_ref, o_ref):
        for hd in range(XA_HEADS):
            c = slice(hd * XA_HD, (hd + 1) * XA_HD)
            p = _xa_probs(q_ref[:, c], kv_ref[:, c])
            vh = kv_ref[:, D + hd * XA_HD:D + (hd + 1) * XA_HD]
            o_ref[:, c] = _dot(p.astype(BF16), vh).astype(BF16)

    return pl.pallas_call(
        body, name=name, grid=(L // tr,),
        in_specs=[_row_spec(tr, D), pl.BlockSpec((N_MEM, 2 * D), lambda i: (0, 0))],
        out_specs=_row_spec(tr, D), out_shape=jax.ShapeDtypeStruct((L, D), BF16),
        compiler_params=_cp("parallel"),
    )(q, kv)


def attn_bwd(q, kv, do, name):
    L, D = q.shape
    tr = min(XA_TR, L)

    def body(q_ref, kv_ref, do_ref, dq_ref, dkv_ref):
        @pl.when(pl.program_id(0) == 0)
        def _():
            dkv_ref[...] = jnp.zeros_like(dkv_ref)

        for hd in range(XA_HEADS):
            c = slice(hd * XA_HD, (hd + 1) * XA_HD)
            cv = slice(D + hd * XA_HD, D + (hd + 1) * XA_HD)
            qh, kh, vh, doh = q_ref[:, c], kv_ref[:, c], kv_ref[:, cv], do_ref[:, c]
            p = _xa_probs(qh, kh)
            dp = _dot(doh, vh, 1, 1)
            dkv_ref[:, cv] += _dot(p.astype(BF16), doh, 0, 0)
            ds = (p * (dp - jnp.sum(dp * p, axis=-1, keepdims=True)) * XA_SCALE).astype(BF16)
            dq_ref[:, c] = _dot(ds, kh).astype(BF16)
            dkv_ref[:, c] += _dot(ds, qh, 0, 0)

    return pl.pallas_call(
        body, name=name, grid=(L // tr,),
        in_specs=[_row_spec(tr, D), pl.BlockSpec((N_MEM, 2 * D), lambda i: (0, 0)), _row_spec(tr, D)],
        out_specs=[_row_spec(tr, D), pl.BlockSpec((N_MEM, 2 * D), lambda i: (0, 0))],
        out_shape=[jax.ShapeDtypeStruct((L, D), BF16), jax.ShapeDtypeStruct((N_MEM, 2 * D), F32)],
        compiler_params=_cp("arbitrary"),
    )(q, kv, do)


N_PAIRS = N_HEADS // 2
PAIRS_PER_GROUP = N_PAIRS // N_GROUPS
GN = N_GROUPS * D_STATE


def _softplus(x):
    t = jnp.exp(-jnp.abs(x))
    return jnp.maximum(x, 0.0) + jnp.where(t < 1e-4, t * (1.0 - 0.5 * t), jnp.log(1.0 + t))


def _dot3b(m, v, ca=1, cb=0):
    v0, v1, v2 = _split3(v)
    return _dot(m, v0, ca, cb) + _dot(m, v1, ca, cb) + _dot(m, v2, ca, cb)


def ssd_consts():
    h = lax.broadcasted_iota(jnp.int32, (LANE, D_INNER), 0)
    c = lax.broadcasted_iota(jnp.int32, (LANE, D_INNER), 1)
    expand = (c // HEAD_DIM == h).astype(BF16)
    r = lax.broadcasted_iota(jnp.int32, (CHUNK, CHUNK), 0)
    k = lax.broadcasted_iota(jnp.int32, (CHUNK, CHUNK), 1)
    tri = (k <= r).astype(BF16)
    return expand, tri


def _ssd_common(dtr_ref, prm_ref, e_ref, tri_ref):
    lane = lax.broadcasted_iota(jnp.int32, (CHUNK, LANE), 1)
    valid = lane < N_HEADS
    A = -jnp.exp(prm_ref[1:2, :])
    pre = dtr_ref[...] + prm_ref[0:1, :]
    dt = jnp.where(valid, _softplus(pre), 0.0)
    cs = _dot3b(tri_ref[...], dt * A)
    E = e_ref[...]
    dt_x = _dot3(dt, E)
    cs_x = _dot3(cs, E)
    csl_x = cs_x[CHUNK - 1:CHUNK, :]
    return dict(valid=valid, A=A, pre=pre, dt=dt, cs=cs, csT=cs.T, dt_x=dt_x, ecs_x=jnp.exp(cs_x),
                dend_x=jnp.exp(csl_x - cs_x), cd_x=jnp.exp(csl_x), D_x=_dot3(prm_ref[...], E)[2:3, :])


def ssd_fwd(xs, bc, dtr, zx, prm, ng, name):
    L = xs.shape[0]
    nc = L // CHUNK
    expand, tri = ssd_consts()

    def body(xs_ref, bc_ref, dtr_ref, z_ref, prm_ref, ng_ref, e_ref, tri_ref, y_ref, yn_ref, st_ref, state):
        @pl.when(pl.program_id(0) == 0)
        def _():
            state[...] = jnp.zeros_like(state)

        q = _ssd_common(dtr_ref, prm_ref, e_ref, tri_ref)
        cs, csT = q["cs"], q["csT"]
        xs_v = xs_ref[...]
        X = xs_v * q["dt_x"]
        Xb = X.astype(BF16)
        Xd = (X * q["dend_x"]).astype(BF16)
        ii = lax.broadcasted_iota(jnp.int32, (CHUNK, CHUNK), 0)
        jj = lax.broadcasted_iota(jnp.int32, (CHUNK, CHUNK), 1)
        tril = jj <= ii
        first = jj < HEAD_DIM
        for g in range(N_GROUPS):
            Bg = bc_ref[:, g * D_STATE:(g + 1) * D_STATE]
            Cg = bc_ref[:, GN + g * D_STATE:GN + (g + 1) * D_STATE]
            S = _dot(Cg, Bg, 1, 1)
            for pr in range(PAIRS_PER_GROUP):
                pair = g * PAIRS_PER_GROUP + pr
                cols = slice(pair * LANE, (pair + 1) * LANE)
                Xp = Xb[:, cols]
                ys = []
                for h in (2 * pair, 2 * pair + 1):
                    seg = cs[:, h:h + 1] - csT[h:h + 1, :]
                    M = (S * jnp.exp(jnp.where(tril, seg, NEG))).astype(BF16)
                    ys.append(_dot(M, Xp))
                prevT = state[pair]
                st_ref[0, pair] = prevT
                yoff = _dot(Cg, prevT.astype(BF16)) * q["ecs_x"][:, cols]
                y_ref[:, cols] = jnp.where(first, ys[0], ys[1]) + yoff + xs_v[:, cols] * q["D_x"][:, cols]
                state[pair] = prevT * q["cd_x"][:, cols] + _dot(Bg, Xd[:, cols], 0, 0)
        z = z_ref[...].astype(F32)
        gt = y_ref[...] * z * _sig(z)
        yn_ref[...] = (gt * _rms(gt) * ng_ref[...]).astype(BF16)

    row = lambda w: pl.BlockSpec((CHUNK, w), lambda c: (c, 0))
    const = lambda a: pl.BlockSpec(a.shape, lambda c: (0,) * a.ndim)
    return pl.pallas_call(
        body, name=name, grid=(nc,),
        in_specs=[row(D_INNER), row(2 * GN), row(LANE), row(D_INNER), const(prm), const(ng), const(expand), const(tri)],
        out_specs=[row(D_INNER), row(D_INNER), pl.BlockSpec((1, N_PAIRS, D_STATE, LANE), lambda c: (c, 0, 0, 0))],
        out_shape=[jax.ShapeDtypeStruct((L, D_INNER), F32), jax.ShapeDtypeStruct((L, D_INNER), BF16),
                   jax.ShapeDtypeStruct((nc, N_PAIRS, D_STATE, LANE), F32)],
        scratch_shapes=[pltpu.VMEM((N_PAIRS, D_STATE, LANE), F32)],
        compiler_params=_cp("arbitrary"),
    )(xs, bc, dtr, zx, prm, ng, expand, tri)


def ssd_bwd(dyn, y, zx, xs, bc, dtr, st, prm, ng, name):
    L = xs.shape[0]
    nc = L // CHUNK
    expand, tri = ssd_consts()

    def body(dyn_ref, y_ref, z_ref, xs_ref, bc_ref, dtr_ref, st_ref, prm_ref, ng_ref, e_ref, tri_ref,
             dxbc_ref, dz_ref, ddtr_ref, dng_ref, dprm_ref, dstate, g_cs, g_q, dX, g_row):
        step = pl.program_id(0)

        @pl.when(step == 0)
        def _():
            dstate[...] = jnp.zeros_like(dstate)
            dng_ref[...] = jnp.zeros_like(dng_ref)
            dprm_ref[...] = jnp.zeros_like(dprm_ref)
            g_row[...] = jnp.zeros_like(g_row)

        q = _ssd_common(dtr_ref, prm_ref, e_ref, tri_ref)
        cs, csT, E = q["cs"], q["csT"], e_ref[...]
        xs_v = xs_ref[...]
        X = xs_v * q["dt_x"]
        Xb = X.astype(BF16)
        Xd_f = X * q["dend_x"]
        Xd = Xd_f.astype(BF16)

        yv = y_ref[...]
        z = z_ref[...].astype(F32)
        sz = _sig(z)
        silu = z * sz
        gt = yv * silu
        r = _rms(gt)
        gn = gt * r
        dyn_v = dyn_ref[...]
        dng_ref[...] += jnp.sum(dyn_v * gn, axis=0, keepdims=True)
        dgn = dyn_v * ng_ref[...]
        dgt = r * (dgn - gn * jnp.mean(dgn * gn, axis=-1, keepdims=True))
        dY = dgt * silu
        dz_ref[...] = (dgt * yv * sz * (1.0 + z * (1.0 - sz))).astype(BF16)
        dYb = dY.astype(BF16)
        g_row[1:2, :] += jnp.sum(dY * xs_v, axis=0, keepdims=True)

        ii = lax.broadcasted_iota(jnp.int32, (CHUNK, CHUNK), 0)
        jj = lax.broadcasted_iota(jnp.int32, (CHUNK, CHUNK), 1)
        tril = jj <= ii
        triu = jj >= ii
        first = jj < HEAD_DIM
        lane_row = lax.broadcasted_iota(jnp.int32, (1, LANE), 1)
        sub_col = lax.broadcasted_iota(jnp.int32, (CHUNK, 1), 0)
        dcs_col = jnp.zeros((CHUNK, LANE), F32)
        dcs_rowT = jnp.zeros((LANE, CHUNK), F32)
        for g in range(N_GROUPS):
            Bg = bc_ref[:, g * D_STATE:(g + 1) * D_STATE]
            Cg = bc_ref[:, GN + g * D_STATE:GN + (g + 1) * D_STATE]
            S = _dot(Cg, Bg, 1, 1)
            ST = _dot(Bg, Cg, 1, 1)
            dS = jnp.zeros((CHUNK, CHUNK), F32)
            dCg = jnp.zeros((CHUNK, D_STATE), F32)
            dBg = jnp.zeros((CHUNK, D_STATE), F32)
            for pr in range(PAIRS_PER_GROUP):
                pair = g * PAIRS_PER_GROUP + pr
                cols = slice(pair * LANE, (pair + 1) * LANE)
                Xp = Xb[:, cols]
                dYp_f = dY[:, cols]
                dYp = dYb[:, cols]
                prevT = st_ref[0, pair]
                prevTb = prevT.astype(BF16)
                dst = dstate[pair]
                dstb = dst.astype(BF16)
                ecs_p = q["ecs_x"][:, cols]
                g_cs[:, cols] = dYp_f * (_dot(Cg, prevTb) * ecs_p)
                dWb = (dYp_f * ecs_p).astype(BF16)
                dprev = dst * q["cd_x"][:, cols] + _dot(Cg, dWb, 0, 0)
                dCg = dCg + _dot(dWb, prevTb, 1, 1)
                g_row[0:1, cols] = jnp.sum(dst * prevT, axis=0, keepdims=True)
                dXp = None
                for hh, h in enumerate((2 * pair, 2 * pair + 1)):
                    mine = first if hh == 0 else jnp.logical_not(first)
                    seg = cs[:, h:h + 1] - csT[h:h + 1, :]
                    lam = jnp.exp(jnp.where(tril, seg, NEG))
                    dM = _dot(jnp.where(mine, dYp, jnp.zeros_like(dYp)), Xp, 1, 1)
                    dS = dS + dM * lam
                    Gm = dM * (S * lam)
                    dcs_col = dcs_col + jnp.sum(Gm, axis=1, keepdims=True) * (lane_row == h).astype(F32)
                    dcs_rowT = dcs_rowT + (sub_col == h).astype(F32) * jnp.sum(Gm, axis=0, keepdims=True)
                    MT = (ST * jnp.exp(jnp.where(triu, -seg, NEG))).astype(BF16)
                    t = _dot(MT, dYp)
                    dXp = t if dXp is None else jnp.where(first, dXp, t)
                dXd = _dot(Bg, dstb)
                dBg = dBg + _dot(Xd[:, cols], dstb, 1, 1)
                g_q[:, cols] = dXd * Xd_f[:, cols]
                dX[:, cols] = dXp + dXd * q["dend_x"][:, cols]
                dstate[pair] = dprev
            dSb = dS.astype(BF16)
            dxbc_ref[:, D_INNER + g * D_STATE:D_INNER + (g + 1) * D_STATE] = dBg + _dot(dSb, Cg, 0, 0)
            dxbc_ref[:, D_INNER + GN + g * D_STATE:D_INNER + GN + (g + 1) * D_STATE] = dCg + _dot(dSb, Bg)
        dXv = dX[...]
        dxbc_ref[:, 0:D_INNER] = q["D_x"] * dY + dXv * q["dt_x"]
        r_dt = _dot3(dXv * xs_v, E, 1, 1)
        r_cs = _dot3(g_cs[...], E, 1, 1)
        r_q = _dot3(g_q[...], E, 1, 1)
        r_row = _dot3(g_row[...], E, 1, 1)
        cd = jnp.exp(cs[CHUNK - 1:CHUNK, :])
        dcs_last = jnp.sum(r_q, axis=0, keepdims=True) + r_row[0:1, :] * cd
        dcs = r_cs - r_q + dcs_col - dcs_rowT.T + jnp.where(sub_col == CHUNK - 1, dcs_last, 0.0)
        da = _dot3b(tri_ref[...], dcs, 0, 0)
        dpre = jnp.where(q["valid"], (r_dt + da * q["A"]) * _sig(q["pre"]), 0.0)
        ddtr_ref[...] = dpre
        dprm_ref[0:1, :] += jnp.sum(dpre, axis=0, keepdims=True)
        dprm_ref[1:2, :] += jnp.sum(da * q["dt"], axis=0, keepdims=True) * q["A"]
        dprm_ref[2:3, :] = r_row[1:2, :]

    rev = lambda w: pl.BlockSpec((CHUNK, w), lambda c: (nc - 1 - c, 0))
    const = lambda a: pl.BlockSpec(a.shape, lambda c: (0,) * a.ndim)
    return pl.pallas_call(
        body, name=name, grid=(nc,),
        in_specs=[rev(D_INNER), rev(D_INNER), rev(D_INNER), rev(D_INNER), rev(2 * GN), rev(LANE),
                  pl.BlockSpec((1, N_PAIRS, D_STATE, LANE), lambda c: (nc - 1 - c, 0, 0, 0)),
                  const(prm), const(ng), const(expand), const(tri)],
        out_specs=[rev(CONV_DIM), rev(D_INNER), rev(LANE),
                   pl.BlockSpec((1, D_INNER), lambda c: (0, 0)), pl.BlockSpec((8, LANE), lambda c: (0, 0))],
        out_shape=[jax.ShapeDtypeStruct((L, CONV_DIM), F32), jax.ShapeDtypeStruct((L, D_INNER), BF16),
                   jax.ShapeDtypeStruct((L, LANE), F32), jax.ShapeDtypeStruct((1, D_INNER), F32),
                   jax.ShapeDtypeStruct((8, LANE), F32)],
        scratch_shapes=[pltpu.VMEM((N_PAIRS, D_STATE, LANE), F32), pltpu.VMEM((CHUNK, D_INNER), F32),
                        pltpu.VMEM((CHUNK, D_INNER), F32), pltpu.VMEM((CHUNK, D_INNER), F32),
                        pltpu.VMEM((8, D_INNER), F32)],
        compiler_params=_cp("arbitrary"),
    )(dyn, y, zx, xs, bc, dtr, st, prm, ng, expand, tri)


def _ssd_weights(W, j):
    w_in = W["ssm_in_w"][j]
    nzx = D_INNER + CONV_DIM
    wdt = jnp.pad(w_in[:, nzx:], ((0, 0), (0, LANE - N_HEADS)))
    prm = jnp.zeros((8, LANE), F32)
    prm = prm.at[0, :N_HEADS].set(W["ssm_dt_bias"][j]).at[1, :N_HEADS].set(W["ssm_A_log"][j])
    prm = prm.at[2, :N_HEADS].set(W["ssm_D"][j])
    return dict(wzx=w_in[:, :nzx], wz=w_in[:, :D_INNER], wx=w_in[:, D_INNER:nzx], wdt=wdt, cw=W["ssm_conv_w"][j],
                cb=W["ssm_conv_b"][j].reshape(1, CONV_DIM), prm=prm, ng=W["ssm_norm_g"][j].reshape(1, D_INNER))


def ssd_layer_fwd(h, W, j, tag):
    p = _ssd_weights(W, j)
    zx = mm(h, p["wzx"], out_dtype=BF16, name=f"{tag}_zx")
    dtr = mm(h, p["wdt"], name=f"{tag}_dt")
    xs = ssm_conv_fwd(zx, p["cw"], p["cb"], col0=D_INNER, ncols=D_INNER, wcol0=0, out_dtype=F32, name=f"{tag}_convx")
    bc = ssm_conv_fwd(zx, p["cw"], p["cb"], col0=2 * D_INNER, ncols=2 * GN, wcol0=D_INNER, out_dtype=BF16,
                      name=f"{tag}_convbc")
    y, yn, st = ssd_fwd(xs, bc, dtr, zx, p["prm"], p["ng"], name=f"{tag}_scan")
    f = mm(yn, W["ssm_out_w"], layer=j, name=f"{tag}_out")
    return f, dict(h=h, zx=zx, dtr=dtr, xs=xs, bc=bc, y=y, yn=yn, st=st, p=p)


def ssd_layer_bwd(df, ctx, W, GB, j, tag):
    p = ctx["p"]
    h = ctx["h"]
    dyn = mm(df, W["ssm_out_w"], layer=j, tb=True, name=f"{tag}_b_dyn")
    GB["ssm_out_w"] = mm(ctx["yn"], df, ta=True, into=(GB["ssm_out_w"], j), name=f"{tag}_b_gwo")
    dxbc, dz, ddtr, dng, dprm = ssd_bwd(dyn, ctx["y"], ctx["zx"], ctx["xs"], ctx["bc"], ctx["dtr"], ctx["st"],
                                        p["prm"], p["ng"], name=f"{tag}_b_scan")
    dpre, dcw, dcb = ssm_conv_bwd(ctx["zx"], dxbc, p["cw"], p["cb"], name=f"{tag}_b_conv")
    dxr = dwconv_t(dpre, p["cw"], SSM_K, name=f"{tag}_b_convt")
    dh = mm(dz, p["wz"], tb=True, name=f"{tag}_b_dh1")
    dh = mm(dxr, p["wx"], tb=True, acc=dh, name=f"{tag}_b_dh2")
    dh = mm(ddtr, p["wdt"], tb=True, acc=dh, name=f"{tag}_b_dh3")
    g_in = jnp.concatenate([mm(h, dz, ta=True, out_dtype=BF16, name=f"{tag}_b_gz"),
                            mm(h, dxr, ta=True, out_dtype=BF16, name=f"{tag}_b_gx"),
                            mm(h, ddtr, ta=True, out_dtype=BF16, name=f"{tag}_b_gdt")[:, :N_HEADS]], axis=1)
    return dh, dict(ssm_in_w=g_in, ssm_conv_w=dcw, ssm_conv_b=dcb[0], ssm_dt_bias=dprm[0, :N_HEADS],
                    ssm_A_log=dprm[1, :N_HEADS], ssm_D=dprm[2, :N_HEADS], ssm_norm_g=dng[0])


def cf_layer_fwd(h, W, j, tag):
    u = mm(h, W["cf_pw1_w"], layer=j, bias=W["cf_pw1_b"][j], out_dtype=BF16, name=f"{tag}_pw1")
    c, s = cf_fwd(u, W["cf_dw_w"][j], W["cf_dw_b"][j].reshape(1, -1), W["cf_ln_g"][j].reshape(1, -1),
                  W["cf_ln_b"][j].reshape(1, -1), name=f"{tag}_conv")
    f = mm(s, W["cf_pw2_w"], layer=j, bias=W["cf_pw2_b"][j], name=f"{tag}_pw2")
    return f, dict(h=h, u=u, c=c, s=s)


def cf_layer_bwd(df, ctx, W, GB, j, tag):
    h = ctx["h"]
    ds = mm(df, W["cf_pw2_w"], layer=j, tb=True, name=f"{tag}_b_ds")
    GB["cf_pw2_w"] = mm(ctx["s"], df, ta=True, into=(GB["cf_pw2_w"], j), name=f"{tag}_b_gpw2")
    g_b2 = colsum(df, name=f"{tag}_b_gb2")
    dc, dlg, dlb = cf_bwd_ln(ctx["c"], ds, W["cf_ln_g"][j].reshape(1, -1), W["cf_ln_b"][j].reshape(1, -1),
                             name=f"{tag}_b_ln")
    du, ddw, ddb = cf_bwd_conv(ctx["u"], dc, W["cf_dw_w"][j], name=f"{tag}_b_conv")
    dh = mm(du, W["cf_pw1_w"], layer=j, tb=True, name=f"{tag}_b_dh")
    GB["cf_pw1_w"] = mm(h, du, ta=True, into=(GB["cf_pw1_w"], j), name=f"{tag}_b_gpw1")
    g_b1 = colsum(du, name=f"{tag}_b_gb1")
    return dh, dict(cf_pw1_b=g_b1[0], cf_dw_w=ddw, cf_dw_b=ddb[0], cf_ln_g=dlg[0], cf_ln_b=dlb[0], cf_pw2_b=g_b2[0])


def xa_layer_fwd(h, mem, W, i, tag):
    m = norm_fwd(mem, W["xa_mem_g"][i], name=f"{tag}_memnorm")
    kv = mm(m, W["xa_kv_w"], layer=i, out_dtype=BF16, name=f"{tag}_kv")
    q = mm(h, W["xa_q_w"], layer=i, out_dtype=BF16, name=f"{tag}_q")
    o = attn_fwd(q, kv, name=f"{tag}_attn")
    f = mm(o, W["xa_o_w"], layer=i, name=f"{tag}_o")
    return f, dict(h=h, m=m, kv=kv, q=q, o=o)


def xa_layer_bwd(df, ctx, mem, W, GB, i, tag):
    h = ctx["h"]
    do = mm(df, W["xa_o_w"], layer=i, tb=True, out_dtype=BF16, name=f"{tag}_b_do")
    GB["xa_o_w"] = mm(ctx["o"], df, ta=True, into=(GB["xa_o_w"], i), name=f"{tag}_b_go")
    dq, dkv = attn_bwd(ctx["q"], ctx["kv"], do, name=f"{tag}_b_attn")
    dh = mm(dq, W["xa_q_w"], layer=i, tb=True, name=f"{tag}_b_dh")
    GB["xa_q_w"] = mm(h, dq, ta=True, into=(GB["xa_q_w"], i), name=f"{tag}_b_gq")
    GB["xa_kv_w"] = mm(ctx["m"], dkv, ta=True, into=(GB["xa_kv_w"], i), name=f"{tag}_b_gkv")
    dm = mm(dkv, W["xa_kv_w"], layer=i, tb=True, name=f"{tag}_b_dm")
    g_mg = norm_dg(mem, dm, name=f"{tag}_b_gmem")
    return dh, dict(xa_mem_g=g_mg[0])


def ffn_layer_fwd(h, W, i, tag):
    cw, cb = W["ffn_conv_w"][i], W["ffn_conv_b"][i].reshape(1, -1)
    u = mm(h, W["ffn_in_w"], layer=i, out_dtype=BF16, name=f"{tag}_in")
    act = ffn_act_fwd(u, cw, cb, name=f"{tag}_act")
    f = mm(act, W["ffn_out_w"], layer=i, name=f"{tag}_out")
    return f, dict(h=h, u=u, act=act)


def ffn_layer_bwd(df, ctx, W, GB, i, tag):
    cw, cb = W["ffn_conv_w"][i], W["ffn_conv_b"][i].reshape(1, -1)
    h = ctx["h"]
    dact = mm(df, W["ffn_out_w"], layer=i, tb=True, out_dtype=BF16, name=f"{tag}_b_dact")
    GB["ffn_out_w"] = mm(ctx["act"], df, ta=True, into=(GB["ffn_out_w"], i), name=f"{tag}_b_gout")
    duc, dcw, dcb = ffn_act_bwd(ctx["u"], dact, cw, cb, name=f"{tag}_b_act")
    du = dwconv_t(duc, cw, FFN_K, name=f"{tag}_b_convt")
    dh = mm(du, W["ffn_in_w"], layer=i, tb=True, name=f"{tag}_b_dh")
    GB["ffn_in_w"] = mm(h, du, ta=True, into=(GB["ffn_in_w"], i), name=f"{tag}_b_gin")
    return dh, dict(ffn_conv_w=dcw, ffn_conv_b=dcb[0])


def local_step(x, mem, target, W):
    subs = [(i, s) for i in range(DEPTH) for s in range(3)]
    ng = W["norm_g"]

    def fwd(i, s, h):
        tag = f"l{i}s{s}"
        if s == 0:
            return ssd_layer_fwd(h, W, i // 2, tag) if i % 2 == 0 else cf_layer_fwd(h, W, i // 2, tag)
        if s == 1:
            return xa_layer_fwd(h, mem, W, i, tag)
        return ffn_layer_fwd(h, W, i, tag)

    GB = {n: jnp.zeros(W[n].shape, BF16) for n in BIG if n != "ssm_in_w"}

    def bwd(i, s, df, ctx):
        tag = f"l{i}s{s}"
        if s == 0:
            return (ssd_layer_bwd if i % 2 == 0 else cf_layer_bwd)(df, ctx, W, GB, i // 2, tag)
        if s == 1:
            return xa_layer_bwd(df, ctx, mem, W, GB, i, tag)
        return ffn_layer_bwd(df, ctx, W, GB, i, tag)

    h = norm_fwd(x, ng[0, 0], name="norm0")
    saved = []
    dxp = loss = None
    for k, (i, s) in enumerate(subs):
        f, ctx = fwd(i, s, h)
        saved.append((x, f, ctx))
        if k + 1 < len(subs):
            ni, ns = subs[k + 1]
            x, h = bnd_fwd(x, f, ng[i, 2 * s + 1], ng[ni, 2 * ns], name=f"bnd{k}")
        else:
            dxp, loss = final_fwd(x, f, ng[i, 2 * s + 1], target, name="final")

    grads = {}

    def put(name, idx, val):
        grads.setdefault(name, {})[idx] = val

    i, s = subs[-1]
    top = bnd_bwd(dxp, post=(saved[-1][1], ng[i, 2 * s + 1]), name="bbnd_top")
    put("norm_g", (i, 2 * s + 1), top["dgpost"][0])
    df = top["df"]
    for k in range(len(subs) - 1, -1, -1):
        i, s = subs[k]
        xk, _, ctx = saved[k]
        dh, gw = bwd(i, s, df, ctx)
        for name, val in gw.items():
            put(name, i // 2 if name.startswith(("ssm_", "cf_")) else i, val)
        if k > 0:
            pi, ps = subs[k - 1]
            r = bnd_bwd(dxp, pre=(xk, ng[i, 2 * s], dh), post=(saved[k - 1][1], ng[pi, 2 * ps + 1]), name=f"bbnd{k}")
            put("norm_g", (pi, 2 * ps + 1), r["dgpost"][0])
            df = r["df"]
        else:
            r = bnd_bwd(dxp, pre=(xk, ng[i, 2 * s], dh), name="bbnd0")
        put("norm_g", (i, 2 * s), r["dgpre"][0])
        dxp = r["dx"]

    out = dict(GB)
    for name, d in grads.items():
        if name == "norm_g":
            out[name] = jnp.stack([jnp.stack([d[(i, t)] for t in range(6)]) for i in range(DEPTH)])
        else:
            out[name] = jnp.stack([d[j] for j in sorted(d)])
    return loss, dxp, out


ANY = pl.BlockSpec(memory_space=pl.ANY)


def _pos():
    return lax.axis_index("x"), lax.axis_index("y"), lax.axis_index("c")


def all_gather(shard, name):
    R, C = shard.shape

    def body(x_ref, out_ref, send_sems, recv_sems, local_sem):
        x, y, c = _pos()
        me, sibling = (x, y, c), (x, y, 1 - c)
        chips = [(1 - x, y), (x, 1 - y), (1 - x, 1 - y)]

        def slot(px, py, pc):
            return out_ref.at[4 * px + 2 * py + pc]

        def copy(k, block, to, src=None):
            return pltpu.make_async_remote_copy(
                src_ref=slot(*block) if src is None else src, dst_ref=slot(*block),
                send_sem=send_sems.at[k], recv_sem=recv_sems.at[k], device_id=to, device_id_type=MESH)

        mine = pltpu.make_async_copy(x_ref, slot(*me), local_sem)
        mine.start()
        first = [copy(0, me, sibling, src=x_ref)]
        first += [copy(1 + j, me, (*chip, c), src=x_ref) for j, chip in enumerate(chips)]
        for cp in first:
            cp.start()
        passed = [copy(4 + j, (*chip, c), sibling) for j, chip in enumerate(chips)]
        for j, chip in enumerate(chips):
            copy(1 + j, (*chip, c), me).wait_recv()
            passed[j].start()
        copy(0, sibling, me).wait_recv()
        for j, chip in enumerate(chips):
            copy(4 + j, (*chip, 1 - c), me).wait_recv()
        for cp in first + passed:
            cp.wait_send()
        mine.wait()

    return pl.pallas_call(
        body, name=name, out_shape=jax.ShapeDtypeStruct((N_DEV, R, C), shard.dtype),
        in_specs=[ANY], out_specs=ANY,
        scratch_shapes=[pltpu.SemaphoreType.DMA((7,)), pltpu.SemaphoreType.DMA((7,)), pltpu.SemaphoreType.DMA(())],
    )(shard)


def _win(ref, kind, k, a, b):
    if kind == "lead":
        return ref.at[k]
    if kind == "row":
        return ref.at[:, pl.ds(pl.multiple_of(k * a, 16), a), :]
    return ref.at[:, :, pl.ds(pl.multiple_of(k * b, LANE), b)]


def _full_shape(shard_shape, kind):
    n, a, b = shard_shape
    return {"lead": (N_DEV, n, a, b), "row": (n, N_DEV * a, b), "col": (n, a, N_DEV * b)}[kind]


def all_gather_big(shards, kinds, name):
    nw = len(shards)
    geo = [s.shape[1:] for s in shards]

    def body(*refs):
        x_refs, o_refs = refs[:nw], refs[nw:2 * nw]
        send_sems, recv_sems, local_sems = refs[2 * nw:]
        x, y, c = _pos()
        me, sibling = (x, y, c), (x, y, 1 - c)
        chips = [(1 - x, y), (x, 1 - y), (1 - x, 1 - y)]

        def slot(w, px, py, pc):
            return _win(o_refs[w], kinds[w], 4 * px + 2 * py + pc, *geo[w])

        def copy(w, k, block, to, src=None):
            return pltpu.make_async_remote_copy(
                src_ref=slot(w, *block) if src is None else src, dst_ref=slot(w, *block),
                send_sem=send_sems.at[7 * w + k], recv_sem=recv_sems.at[7 * w + k], device_id=to, device_id_type=MESH)

        mine = [pltpu.make_async_copy(x_refs[w], slot(w, *me), local_sems.at[w]) for w in range(nw)]
        for cp in mine:
            cp.start()
        first = []
        for w in range(nw):
            first.append(copy(w, 0, me, sibling, src=x_refs[w]))
            first += [copy(w, 1 + j, me, (*chip, c), src=x_refs[w]) for j, chip in enumerate(chips)]
        for cp in first:
            cp.start()
        passed = []
        for w in range(nw):
            for j, chip in enumerate(chips):
                copy(w, 1 + j, (*chip, c), me).wait_recv()
                cp = copy(w, 4 + j, (*chip, c), sibling)
                cp.start()
                passed.append(cp)
        for w in range(nw):
            copy(w, 0, sibling, me).wait_recv()
            for j, chip in enumerate(chips):
                copy(w, 4 + j, (*chip, 1 - c), me).wait_recv()
        for cp in first + passed:
            cp.wait_send()
        for cp in mine:
            cp.wait()

    return pl.pallas_call(
        body, name=name,
        out_shape=[jax.ShapeDtypeStruct(_full_shape(s.shape, k), s.dtype) for s, k in zip(shards, kinds)],
        in_specs=[ANY] * nw, out_specs=[ANY] * nw,
        scratch_shapes=[pltpu.SemaphoreType.DMA((7 * nw,)), pltpu.SemaphoreType.DMA((7 * nw,)),
                        pltpu.SemaphoreType.DMA((nw,))],
    )(*shards)


def rs_sibling(gs, kinds, geo, name):
    nw = len(gs)

    def body(*refs):
        g_refs, o_refs = refs[:nw], refs[nw:2 * nw]
        send_sems, recv_sems = refs[2 * nw:]
        x, y, c = _pos()
        cps = [pltpu.make_async_remote_copy(
            src_ref=_win(g_refs[w], kinds[w], 2 * j + 1 - c, *geo[w][1:]), dst_ref=o_refs[w].at[j],
            send_sem=send_sems.at[4 * w + j], recv_sem=recv_sems.at[4 * w + j],
            device_id=(x, y, 1 - c), device_id_type=MESH) for w in range(nw) for j in range(4)]
        for cp in cps:
            cp.start()
        for cp in cps:
            cp.wait()

    return pl.pallas_call(
        body, name=name, out_shape=[jax.ShapeDtypeStruct((4, *geo[w]), gs[w].dtype) for w in range(nw)],
        in_specs=[ANY] * nw, out_specs=[ANY] * nw,
        scratch_shapes=[pltpu.SemaphoreType.DMA((4 * nw,)), pltpu.SemaphoreType.DMA((4 * nw,))],
    )(*gs)


def rs_add(g, kind, got, name):
    _, n, a, b = got.shape
    ta = a if a <= 512 else 256
    per = a // ta
    core = lax.axis_index("c").astype(jnp.int32).reshape(1)

    def body(c_ref, g_ref, got_ref, o_ref):
        o_ref[...] = (g_ref[...].astype(F32) + got_ref[...].astype(F32)).astype(o_ref.dtype)

    if kind == "lead":
        g_spec = pl.BlockSpec((None, None, ta, b), lambda j, l, r, c_ref: (2 * j + c_ref[0], l, r, 0))
    elif kind == "row":
        g_spec = pl.BlockSpec((None, ta, b), lambda j, l, r, c_ref: (l, (2 * j + c_ref[0]) * per + r, 0))
    else:
        g_spec = pl.BlockSpec((None, ta, b), lambda j, l, r, c_ref: (l, r, 2 * j + c_ref[0]))
    blk = pl.BlockSpec((None, None, ta, b), lambda j, l, r, c_ref: (j, l, r, 0))
    return pl.pallas_call(
        body, name=name, out_shape=jax.ShapeDtypeStruct(got.shape, got.dtype),
        grid_spec=pltpu.PrefetchScalarGridSpec(num_scalar_prefetch=1, grid=(4, n, per),
                                               in_specs=[g_spec, blk], out_specs=blk),
        compiler_params=_cp("parallel", "parallel", "parallel"),
    )(core, g, got)


def rs_chips(s1s, name):
    nw = len(s1s)

    def body(*refs):
        s_refs, o_refs = refs[:nw], refs[nw:2 * nw]
        send_sems, recv_sems, local_sems = refs[2 * nw:]
        x, y, c = _pos()
        jme = 2 * x + y
        peers = [(1 - x, y), (x, 1 - y), (1 - x, 1 - y)]
        local = [pltpu.make_async_copy(s_refs[w].at[jme], o_refs[w].at[jme], local_sems.at[w]) for w in range(nw)]
        for cp in local:
            cp.start()

        def copy(w, k, src_slot, dst_slot, peer):
            return pltpu.make_async_remote_copy(
                src_ref=s_refs[w].at[src_slot], dst_ref=o_refs[w].at[dst_slot], send_sem=send_sems.at[3 * w + k],
                recv_sem=recv_sems.at[3 * w + k], device_id=(*peer, c), device_id_type=MESH)

        sends = [copy(w, k, 2 * px + py, jme, (px, py)) for w in range(nw) for k, (px, py) in enumerate(peers)]
        for cp in sends:
            cp.start()
        for w in range(nw):
            for k, (px, py) in enumerate(peers):
                copy(w, k, jme, 2 * px + py, (px, py)).wait_recv()
        for cp in sends:
            cp.wait_send()
        for cp in local:
            cp.wait()

    return pl.pallas_call(
        body, name=name, out_shape=[jax.ShapeDtypeStruct(s.shape, s.dtype) for s in s1s],
        in_specs=[ANY] * nw, out_specs=[ANY] * nw,
        scratch_shapes=[pltpu.SemaphoreType.DMA((3 * nw,)), pltpu.SemaphoreType.DMA((3 * nw,)),
                        pltpu.SemaphoreType.DMA((nw,))],
    )(*s1s)


def small_exchange(sh, rep, name):
    _, Rs, C = sh.shape
    Rr = rep.shape[0]

    def body(sh_ref, rep_ref, sh_out, rep_out, send_sems, recv_sems, local_sems):
        x, y, c = _pos()
        me = 4 * x + 2 * y + c
        l1 = pltpu.make_async_copy(sh_ref.at[me], sh_out.at[me], local_sems.at[0])
        l2 = pltpu.make_async_copy(rep_ref, rep_out.at[me], local_sems.at[1])
        l1.start()
        l2.start()

        def flip(v, bit):
            return 1 - v if bit else v

        sends, recvs = [], []
        for r in range(1, N_DEV):
            peer = (flip(x, r & 4), flip(y, r & 2), flip(c, r & 1))
            pid = 4 * peer[0] + 2 * peer[1] + peer[2]
            k = 2 * (r - 1)
            mk = lambda src, dst, kk: pltpu.make_async_remote_copy(
                src_ref=src, dst_ref=dst, send_sem=send_sems.at[kk], recv_sem=recv_sems.at[kk],
                device_id=peer, device_id_type=MESH)
            sends += [mk(sh_ref.at[pid], sh_out.at[me], k), mk(rep_ref, rep_out.at[me], k + 1)]
            recvs += [mk(sh_ref.at[me], sh_out.at[pid], k), mk(rep_ref, rep_out.at[pid], k + 1)]
        for cp in sends:
            cp.start()
        for cp in recvs:
            cp.wait_recv()
        for cp in sends:
            cp.wait_send()
        l1.wait()
        l2.wait()

    n = 2 * (N_DEV - 1)
    return pl.pallas_call(
        body, name=name,
        out_shape=[jax.ShapeDtypeStruct((N_DEV, Rs, C), sh.dtype), jax.ShapeDtypeStruct((N_DEV, *rep.shape), rep.dtype)],
        in_specs=[ANY, ANY], out_specs=[ANY, ANY],
        scratch_shapes=[pltpu.SemaphoreType.DMA((n,)), pltpu.SemaphoreType.DMA((n,)), pltpu.SemaphoreType.DMA((2,))],
    )(sh, rep)


def adam_slots(w, m, v, slots, name):
    S, n, a, b = slots.shape
    ta = max(t for t in range(16, min(a, 512) + 1, 8)
             if a % t == 0 and t * S * b * slots.dtype.itemsize <= 4 * 1024 * 1024)

    def body(w_ref, m_ref, v_ref, s_ref, g_ref, d_ref, m2_ref, v2_ref):
        gv = s_ref[0].astype(F32)
        for k in range(1, S):
            gv = gv + s_ref[k].astype(F32)
        m2 = ADAM_B1 * m_ref[...] + (1.0 - ADAM_B1) * gv
        v2 = ADAM_B2 * v_ref[...] + (1.0 - ADAM_B2) * (gv * gv)
        m_hat = m2 / (1.0 - ADAM_B1 ** ADAM_STEP)
        v_hat = v2 / (1.0 - ADAM_B2 ** ADAM_STEP)
        g_ref[...] = gv
        d_ref[...] = -ADAM_LR * (m_hat / (jnp.sqrt(v_hat) + ADAM_EPS) + ADAM_WD * w_ref[...])
        m2_ref[...] = m2
        v2_ref[...] = v2

    spec = pl.BlockSpec((None, ta, b), lambda l, r: (l, r, 0))
    return pl.pallas_call(
        body, name=name, grid=(n, a // ta),
        in_specs=[spec] * 3 + [pl.BlockSpec((S, None, ta, b), lambda l, r: (0, l, r, 0))], out_specs=[spec] * 4,
        out_shape=[jax.ShapeDtypeStruct((n, a, b), F32)] * 4, compiler_params=_cp("parallel", "parallel"),
    )(w, m, v, slots)


WEIGHTS = ["norm_g", "ssm_in_w", "ssm_conv_w", "ssm_conv_b", "ssm_dt_bias", "ssm_A_log", "ssm_D", "ssm_norm_g",
           "ssm_out_w", "cf_pw1_w", "cf_pw1_b", "cf_dw_w", "cf_dw_b", "cf_ln_g", "cf_ln_b", "cf_pw2_w", "cf_pw2_b",
           "xa_mem_g", "xa_q_w", "xa_kv_w", "xa_o_w", "ffn_in_w", "ffn_conv_w", "ffn_conv_b", "ffn_out_w"]
ARGS = ["x", "mem"] + WEIGHTS + ["loss_target"] + ["m_" + n for n in WEIGHTS] + ["v_" + n for n in WEIGHTS]
BIG = {"ssm_in_w": "col", "ssm_out_w": "row", "cf_pw1_w": "col", "cf_pw2_w": "row", "xa_q_w": "row",
       "xa_kv_w": "col", "xa_o_w": "row", "ffn_in_w": "col", "ffn_out_w": "row"}
SMALL = ["norm_g", "ssm_conv_w", "cf_pw1_b", "cf_dw_w", "cf_dw_b", "cf_ln_g", "cf_ln_b", "cf_pw2_b", "ffn_conv_w"]
REP = ["ssm_conv_b", "ssm_dt_bias", "ssm_A_log", "ssm_D", "ssm_norm_g", "xa_mem_g", "ffn_conv_b"]
SMALL_W = 768
REP_W = 5632


def _r8(n):
    return -(-n // 8) * 8


def _stack2d(arrs, wid):
    parts = []
    for a in arrs:
        r, c = a.shape[-2:]
        parts.append(jnp.pad(a, [(0, 0)] * (a.ndim - 2) + [(0, _r8(r) - r), (0, wid - c)]))
    return jnp.concatenate(parts, axis=-2)


def _unstack2d(buf, shapes2d):
    out, o = [], 0
    for r, c in shapes2d:
        out.append(buf[..., o:o + r, :c])
        o += _r8(r)
    return out


def _gathered_to_full(g):
    lead = g.shape[1:-1]
    return jnp.moveaxis(g, 0, -2).reshape(*lead, N_DEV * g.shape[-1])


def _full_to_slots(w):
    lead = w.shape[:-1]
    return jnp.moveaxis(w.reshape(*lead, N_DEV, w.shape[-1] // N_DEV), -2, 0)


def kernel(x, mem, norm_g, ssm_in_w, ssm_conv_w, ssm_conv_b, ssm_dt_bias, ssm_A_log, ssm_D, ssm_norm_g, ssm_out_w, cf_pw1_w, cf_pw1_b, cf_dw_w, cf_dw_b, cf_ln_g, cf_ln_b, cf_pw2_w, cf_pw2_b, xa_mem_g, xa_q_w, xa_kv_w, xa_o_w, ffn_in_w, ffn_conv_w, ffn_conv_b, ffn_out_w, loss_target, m_norm_g, m_ssm_in_w, m_ssm_conv_w, m_ssm_conv_b, m_ssm_dt_bias, m_ssm_A_log, m_ssm_D, m_ssm_norm_g, m_ssm_out_w, m_cf_pw1_w, m_cf_pw1_b, m_cf_dw_w, m_cf_dw_b, m_cf_ln_g, m_cf_ln_b, m_cf_pw2_w, m_cf_pw2_b, m_xa_mem_g, m_xa_q_w, m_xa_kv_w, m_xa_o_w, m_ffn_in_w, m_ffn_conv_w, m_ffn_conv_b, m_ffn_out_w, v_norm_g, v_ssm_in_w, v_ssm_conv_w, v_ssm_conv_b, v_ssm_dt_bias, v_ssm_A_log, v_ssm_D, v_ssm_norm_g, v_ssm_out_w, v_cf_pw1_w, v_cf_pw1_b, v_cf_dw_w, v_cf_dw_b, v_cf_ln_g, v_cf_ln_b, v_cf_pw2_w, v_cf_pw2_b, v_xa_mem_g, v_xa_q_w, v_xa_kv_w, v_xa_o_w, v_ffn_in_w, v_ffn_conv_w, v_ffn_conv_b, v_ffn_out_w):
    return _step(x, mem, norm_g, ssm_in_w, ssm_conv_w, ssm_conv_b, ssm_dt_bias, ssm_A_log, ssm_D, ssm_norm_g, ssm_out_w, cf_pw1_w, cf_pw1_b, cf_dw_w, cf_dw_b, cf_ln_g, cf_ln_b, cf_pw2_w, cf_pw2_b, xa_mem_g, xa_q_w, xa_kv_w, xa_o_w, ffn_in_w, ffn_conv_w, ffn_conv_b, ffn_out_w, loss_target, m_norm_g, m_ssm_in_w, m_ssm_conv_w, m_ssm_conv_b, m_ssm_dt_bias, m_ssm_A_log, m_ssm_D, m_ssm_norm_g, m_ssm_out_w, m_cf_pw1_w, m_cf_pw1_b, m_cf_dw_w, m_cf_dw_b, m_cf_ln_g, m_cf_ln_b, m_cf_pw2_w, m_cf_pw2_b, m_xa_mem_g, m_xa_q_w, m_xa_kv_w, m_xa_o_w, m_ffn_in_w, m_ffn_conv_w, m_ffn_conv_b, m_ffn_out_w, v_norm_g, v_ssm_in_w, v_ssm_conv_w, v_ssm_conv_b, v_ssm_dt_bias, v_ssm_A_log, v_ssm_D, v_ssm_norm_g, v_ssm_out_w, v_cf_pw1_w, v_cf_pw1_b, v_cf_dw_w, v_cf_dw_b, v_cf_ln_g, v_cf_ln_b, v_cf_pw2_w, v_cf_pw2_b, v_xa_mem_g, v_xa_q_w, v_xa_kv_w, v_xa_o_w, v_ffn_in_w, v_ffn_conv_w, v_ffn_conv_b, v_ffn_out_w)


def _step(*args):
    A = dict(zip(ARGS, args, strict=True))
    x, mem, target = A["x"][0], A["mem"][0], A["loss_target"][0]

    big = list(BIG)
    geo = [A[n].shape for n in big]
    kinds = ["row" if BIG[n] == "row" else ("col" if A[n].shape[-1] % LANE == 0 else "lead") for n in big]
    W = {n: A[n] for n in REP}
    for n, kind, g in zip(big, kinds, all_gather_big([A[n].astype(BF16) for n in big], kinds, name="gather_big")):
        W[n] = _gathered_to_full(g) if kind == "lead" else g

    small2d = [(A[n].size // A[n].shape[-1], A[n].shape[-1]) for n in SMALL]
    rep2d = [A[n].shape for n in REP] + [(1, 1)]
    stack_small = lambda pre: _stack2d([A[pre + n].reshape(rc) for n, rc in zip(SMALL, small2d)], SMALL_W)
    stack_rep = lambda pre: _stack2d([A[pre + n] for n in REP] + [jnp.zeros((1, 1), F32)], REP_W)
    small_g = all_gather(stack_small(""), name="gather_small")
    for n, g in zip(SMALL, _unstack2d(small_g, small2d)):
        W[n] = _gathered_to_full(g.reshape(N_DEV, *A[n].shape))

    loss, grad_x, G = local_step(x, mem, target, W)

    gs = [_full_to_slots(G[n]) if kind == "lead" else G[n] for n, kind in zip(big, kinds)]
    got1 = rs_sibling(gs, kinds, geo, name="rs_sibling")
    s1 = [rs_add(g, kind, got, name=f"rs_add_{n}") for n, g, kind, got in zip(big, gs, kinds, got1)]
    got2 = rs_chips(s1, name="rs_chips")

    sh = _stack2d([_full_to_slots(G[n]).reshape(N_DEV, *rc) for n, rc in zip(SMALL, small2d)], SMALL_W)
    rep = _stack2d([G[n] for n in REP] + [loss[:, :1]], REP_W)
    sh_got, rep_got = small_exchange(sh, rep, name="small_exchange")

    res = {}
    for n, slots in zip(big, got2):
        res[n] = adam_slots(A[n], A["m_" + n], A["v_" + n], slots, name=f"adam_{n}")
    for names, shapes2d, stack, slots, tag in ((SMALL, small2d, stack_small, sh_got, "small"),
                                               (REP, rep2d, stack_rep, rep_got, "rep")):
        outs4 = adam_slots(stack("")[None], stack("m_")[None], stack("v_")[None], slots[:, None], name=f"adam_{tag}")
        parts = [_unstack2d(o[0], shapes2d) for o in outs4]
        for k, n in enumerate(names):
            res[n] = tuple(q[k].reshape(A[n].shape) for q in parts)
        if tag == "rep":
            total_loss = parts[0][-1][0, 0]

    outs = [total_loss, grad_x[None]]
    for k in range(4):
        outs += [res[n][k] for n in WEIGHTS]
    return tuple(outs)
```

```python
import jax
import jax.numpy as jnp
from jax import lax
from jax.experimental import pallas as pl
from jax.experimental.pallas import tpu as pltpu

F32 = jnp.float32
BF16 = jnp.bfloat16

D_MODEL = 1024
D_INNER = 2048
N_HEADS = 32
HEAD_DIM = 64
N_GROUPS = 4
D_STATE = 128
CHUNK = 128
CONV_DIM = 3072
SSM_K = 4
CF_K = 31
N_MEM = 256
XA_HEADS = 4
XA_HD = 256
D_FF = 2816
FFN_K = 3
EPS = 1e-6
DEPTH = 4
N_DEV = 8

ADAM_LR = 0.001
ADAM_B1 = 0.9
ADAM_B2 = 0.999
ADAM_EPS = 1e-08
ADAM_WD = 0.01
ADAM_STEP = 10

LANE = 128
VMEM_LIMIT = 56 * 1024 * 1024
NEG = -1e30
MESH = pl.DeviceIdType.MESH


def _cp(*sem):
    return pltpu.CompilerParams(dimension_semantics=sem if sem else None, vmem_limit_bytes=VMEM_LIMIT)


def _tile(n, cap):
    if n <= cap:
        return n
    best = 0
    for t in range(LANE, cap + 1, LANE):
        if n % t == 0:
            best = t
    assert best, (n, cap)
    return best


def _sig(x):
    return 1.0 / (1.0 + jnp.exp(-x))


def _split3(v):
    v0 = v.astype(BF16)
    r1 = v - v0.astype(F32)
    v1 = r1.astype(BF16)
    v2 = (r1 - v1.astype(F32)).astype(BF16)
    return v0, v1, v2


def _dot(a, b, ca=1, cb=0):
    return lax.dot_general(a, b, (((ca,), (cb,)), ((), ())), preferred_element_type=F32)


def _dot3(v, m, ca=1, cb=0):
    v0, v1, v2 = _split3(v)
    return _dot(v0, m, ca, cb) + _dot(v1, m, ca, cb) + _dot(v2, m, ca, cb)


def mm(a, b, *, ta=False, tb=False, bias=None, acc=None, out_dtype=F32, a_idx=None, layer=None, b_k0=0, b_n=None,
       into=None, name):
    if ta:
        K, M = a.shape[-2:]
    else:
        M, K = a.shape[-2:]
    N = b_n if b_n is not None else (b.shape[-2] if tb else b.shape[-1])
    assert (b.ndim == 3) == (layer is not None) and (a.ndim == 3) == (a_idx is not None)
    tm = _tile(M, 1024)
    tn = _tile(N, 1536)
    tk = _tile(K, 2048)
    nk = K // tk
    assert b_k0 % tk == 0 and b_k0 + K <= (b.shape[-1] if tb else b.shape[-2])
    kb = b_k0 // tk
    has_bias, has_acc = bias is not None, acc is not None
    if into is not None:
        out_dtype = into[0].dtype
        assert into[0].shape[1] == M and into[2] % tn == 0 and into[2] + N <= into[0].shape[2] and not has_acc

    def body(*refs):
        a_ref, b_ref = refs[0], refs[1]
        pos = 2
        bias_ref = acc_ref = None
        if has_bias:
            bias_ref = refs[pos]
            pos += 1
        if has_acc:
            acc_ref = refs[pos]
            pos += 1
        if into is not None:
            pos += 1
        o_ref = refs[pos]
        s_ref = refs[pos + 1] if nk > 1 else None
        p = _dot(a_ref[...].astype(BF16), b_ref[...].astype(BF16), 0 if ta else 1, 1 if tb else 0)

        def extras(v):
            if has_bias:
                v = v + bias_ref[...]
            if has_acc:
                v = v + acc_ref[...]
            return v

        if nk == 1:
            o_ref[...] = extras(p).astype(out_dtype)
        else:
            k = pl.program_id(2)

            @pl.when(k == 0)
            def _():
                s_ref[...] = extras(p)

            @pl.when(k > 0)
            def _():
                s_ref[...] += p

            @pl.when(k == nk - 1)
            def _():
                o_ref[...] = s_ref[...].astype(out_dtype)

    lead_a = () if a_idx is None else (a_idx,)
    lead_b = () if layer is None else (layer,)
    sq = lambda lead: (None,) * len(lead)
    if ta:
        a_spec = pl.BlockSpec((*sq(lead_a), tk, tm), lambda i, j, k: (*lead_a, k, i))
    else:
        a_spec = pl.BlockSpec((*sq(lead_a), tm, tk), lambda i, j, k: (*lead_a, i, k))
    if tb:
        b_spec = pl.BlockSpec((*sq(lead_b), tn, tk), lambda i, j, k: (*lead_b, j, k + kb))
    else:
        b_spec = pl.BlockSpec((*sq(lead_b), tk, tn), lambda i, j, k: (*lead_b, k + kb, j))
    in_specs, args = [a_spec, b_spec], [a, b]
    if has_bias:
        in_specs.append(pl.BlockSpec((1, tn), lambda i, j, k: (0, j)))
        args.append(bias.reshape(1, N).astype(F32))
    if has_acc:
        in_specs.append(pl.BlockSpec((tm, tn), lambda i, j, k: (i, j)))
        args.append(acc)
    if into is None:
        out_spec = pl.BlockSpec((tm, tn), lambda i, j, k: (i, j))
        out_shape = jax.ShapeDtypeStruct((M, N), out_dtype)
        aliases = {}
    else:
        buf, l, col0 = into
        cb = col0 // tn
        in_specs.append(pl.BlockSpec(memory_space=pl.ANY))
        args.append(buf)
        out_spec = pl.BlockSpec((None, tm, tn), lambda i, j, k: (l, i, j + cb))
        out_shape = jax.ShapeDtypeStruct(buf.shape, buf.dtype)
        aliases = {len(args) - 1: 0}
    return pl.pallas_call(
        body, name=name, grid=(M // tm, N // tn, nk),
        in_specs=in_specs, out_specs=out_spec, out_shape=out_shape, input_output_aliases=aliases,
        scratch_shapes=[pltpu.VMEM((tm, tn), F32)] if nk > 1 else [],
        compiler_params=_cp("parallel", "parallel", "arbitrary"),
    )(*args)


def colsum(x, name):
    L, C = x.shape
    tr = _tile(L, 512)
    tc = _tile(C, 1024)

    def body(x_ref, o_ref):
        @pl.when(pl.program_id(1) == 0)
        def _():
            o_ref[...] = jnp.zeros_like(o_ref)

        o_ref[...] += jnp.sum(x_ref[...].astype(F32), axis=0, keepdims=True)

    return pl.pallas_call(
        body, name=name, grid=(C // tc, L // tr),
        in_specs=[pl.BlockSpec((tr, tc), lambda j, i: (i, j))],
        out_specs=pl.BlockSpec((1, tc), lambda j, i: (0, j)),
        out_shape=jax.ShapeDtypeStruct((1, C), F32),
        compiler_params=_cp("parallel", "arbitrary"),
    )(x)


TR = 256


def _row_spec(tr, w):
    return pl.BlockSpec((tr, w), lambda i: (i, 0))


def _vec_spec(w):
    return pl.BlockSpec((1, w), lambda i: (0, 0))


def _rms(v):
    return lax.rsqrt(jnp.mean(v * v, axis=-1, keepdims=True) + EPS)


def norm_fwd(x, g, name):
    L, D = x.shape
    tr = min(TR, L)

    def body(x_ref, g_ref, h_ref):
        xv = x_ref[...]
        h_ref[...] = (xv * _rms(xv) * g_ref[...]).astype(BF16)

    return pl.pallas_call(
        body, name=name, grid=(L // tr,),
        in_specs=[_row_spec(tr, D), _vec_spec(D)], out_specs=_row_spec(tr, D),
        out_shape=jax.ShapeDtypeStruct((L, D), BF16), compiler_params=_cp("parallel"),
    )(x, g.reshape(1, D))


def bnd_fwd(x, f, gpost, gpre, name):
    L, D = x.shape
    tr = min(TR, L)

    def body(x_ref, f_ref, gp_ref, gn_ref, xo_ref, h_ref):
        fv = f_ref[...]
        xn = x_ref[...] + fv * _rms(fv) * gp_ref[...]
        xo_ref[...] = xn
        h_ref[...] = (xn * _rms(xn) * gn_ref[...]).astype(BF16)

    return pl.pallas_call(
        body, name=name, grid=(L // tr,),
        in_specs=[_row_spec(tr, D), _row_spec(tr, D), _vec_spec(D), _vec_spec(D)],
        out_specs=[_row_spec(tr, D), _row_spec(tr, D)],
        out_shape=[jax.ShapeDtypeStruct((L, D), F32), jax.ShapeDtypeStruct((L, D), BF16)],
        compiler_params=_cp("parallel"),
    )(x, f, gpost.reshape(1, D), gpre.reshape(1, D))


def final_fwd(x, f, gpost, target, name):
    L, D = x.shape
    tr = min(TR, L)
    n = L // tr

    def body(x_ref, f_ref, gp_ref, t_ref, dy_ref, loss_ref, acc_ref):
        i = pl.program_id(0)

        @pl.when(i == 0)
        def _():
            acc_ref[...] = jnp.zeros_like(acc_ref)

        fv = f_ref[...]
        e = x_ref[...] + fv * _rms(fv) * gp_ref[...] - t_ref[...]
        dy_ref[...] = e * (1.0 / D)
        acc_ref[...] += jnp.sum(e * e, axis=0, keepdims=True)

        @pl.when(i == n - 1)
        def _():
            loss_ref[...] = jnp.full((1, LANE), 0.5 / D, F32) * jnp.sum(acc_ref[...])

    return pl.pallas_call(
        body, name=name, grid=(n,),
        in_specs=[_row_spec(tr, D), _row_spec(tr, D), _vec_spec(D), _row_spec(tr, D)],
        out_specs=[_row_spec(tr, D), _vec_spec(LANE)],
        out_shape=[jax.ShapeDtypeStruct((L, D), F32), jax.ShapeDtypeStruct((1, LANE), F32)],
        scratch_shapes=[pltpu.VMEM((1, D), F32)],
        compiler_params=_cp("arbitrary"),
    )(x, f, gpost.reshape(1, D), target)


def _rms_bwd(v, g, dy):
    r = _rms(v)
    vn = v * r
    dg = jnp.sum(dy * vn, axis=0, keepdims=True)
    dvn = dy * g
    dv = r * (dvn - vn * jnp.mean(dvn * vn, axis=-1, keepdims=True))
    return dv, dg


def bnd_bwd(dxp, *, pre=None, post=None, name):
    L, D = dxp.shape
    tr = min(TR, L)
    has_pre, has_post = pre is not None, post is not None

    def body(*refs):
        pos = 0
        dxp_ref = refs[pos]; pos += 1
        if has_pre:
            x_ref, gpre_ref, dh_ref = refs[pos:pos + 3]; pos += 3
        if has_post:
            f_ref, gpost_ref = refs[pos:pos + 2]; pos += 2
        if has_pre:
            dx_ref, dgpre_ref = refs[pos:pos + 2]; pos += 2
        if has_post:
            df_ref, dgpost_ref = refs[pos:pos + 2]; pos += 2
        i = pl.program_id(0)
        dx = dxp_ref[...]
        if has_pre:
            d, dg = _rms_bwd(x_ref[...], gpre_ref[...], dh_ref[...])
            dx = dx + d
            dx_ref[...] = dx

            @pl.when(i == 0)
            def _():
                dgpre_ref[...] = jnp.zeros_like(dgpre_ref)

            dgpre_ref[...] += dg
        if has_post:
            d, dg = _rms_bwd(f_ref[...], gpost_ref[...], dx)
            df_ref[...] = d.astype(BF16)

            @pl.when(i == 0)
            def _():
                dgpost_ref[...] = jnp.zeros_like(dgpost_ref)

            dgpost_ref[...] += dg

    in_specs, args = [_row_spec(tr, D)], [dxp]
    out_specs, out_shape, names = [], [], []
    if has_pre:
        x, gpre, dh = pre
        in_specs += [_row_spec(tr, D), _vec_spec(D), _row_spec(tr, D)]
        args += [x, gpre.reshape(1, D), dh]
        out_specs += [_row_spec(tr, D), _vec_spec(D)]
        out_shape += [jax.ShapeDtypeStruct((L, D), F32), jax.ShapeDtypeStruct((1, D), F32)]
        names += ["dx", "dgpre"]
    if has_post:
        f, gpost = post
        in_specs += [_row_spec(tr, D), _vec_spec(D)]
        args += [f, gpost.reshape(1, D)]
        out_specs += [_row_spec(tr, D), _vec_spec(D)]
        out_shape += [jax.ShapeDtypeStruct((L, D), BF16), jax.ShapeDtypeStruct((1, D), F32)]
        names += ["df", "dgpost"]
    outs = pl.pallas_call(
        body, name=name, grid=(L // tr,), in_specs=in_specs, out_specs=out_specs, out_shape=out_shape,
        compiler_params=_cp("arbitrary"),
    )(*args)
    return dict(zip(names, outs))


def norm_dg(x, dy, name):
    L, D = x.shape
    tr = min(TR, L)

    def body(x_ref, dy_ref, o_ref):
        @pl.when(pl.program_id(0) == 0)
        def _():
            o_ref[...] = jnp.zeros_like(o_ref)

        xv = x_ref[...]
        o_ref[...] += jnp.sum(dy_ref[...] * xv * _rms(xv), axis=0, keepdims=True)

    return pl.pallas_call(
        body, name=name, grid=(L // tr,),
        in_specs=[_row_spec(tr, D), _row_spec(tr, D)], out_specs=_vec_spec(D),
        out_shape=jax.ShapeDtypeStruct((1, D), F32), compiler_params=_cp("arbitrary"),
    )(x, dy)


HALO = 32


def _prev_halo_spec(tr, tc, col):
    per = tr // HALO
    return pl.BlockSpec((HALO, tc), lambda *g: (jnp.maximum(g[-1] * per - 1, 0), col(*g)))


def _fill_prev(scr, halo_val, blk_val, i, tr):
    scr[pl.ds(0, HALO), :] = jnp.where(i == 0, 0.0, halo_val)
    scr[pl.ds(HALO, tr), :] = blk_val


def _conv(scr, w_ref, K, tr):
    acc = None
    for k in range(K):
        term = scr[pl.ds(HALO - (K - 1) + k, tr), :] * w_ref[k:k + 1, :]
        acc = term if acc is None else acc + term
    return acc


def _conv_dw(scr, d, o_ref, K, tr):
    for k in range(K):
        o_ref[k:k + 1, :] += jnp.sum(d * scr[pl.ds(HALO - (K - 1) + k, tr), :], axis=0, keepdims=True)


def ssm_conv_fwd(zx, w, b, *, col0, ncols, wcol0, out_dtype, name):
    L = zx.shape[0]
    tr = min(TR, L)
    tc = 1024
    cb, wb = col0 // tc, wcol0 // tc

    def body(x_ref, h_ref, w_ref, b_ref, o_ref, p_ref, scr):
        i = pl.program_id(1)
        _fill_prev(scr, h_ref[...].astype(F32), x_ref[...].astype(F32), i, tr)
        pre = _conv(scr, w_ref, SSM_K, tr) + b_ref[...]
        p_ref[...] = pre.astype(BF16)
        o_ref[...] = (pre * _sig(pre)).astype(out_dtype)

    out = pl.BlockSpec((tr, tc), lambda j, i: (i, j))
    return pl.pallas_call(
        body, name=name, grid=(ncols // tc, L // tr),
        in_specs=[pl.BlockSpec((tr, tc), lambda j, i: (i, j + cb)),
                  _prev_halo_spec(tr, tc, lambda j, i: j + cb),
                  pl.BlockSpec((SSM_K, tc), lambda j, i: (0, j + wb)),
                  pl.BlockSpec((1, tc), lambda j, i: (0, j + wb))],
        out_specs=[out, out],
        out_shape=[jax.ShapeDtypeStruct((L, ncols), out_dtype), jax.ShapeDtypeStruct((L, ncols), BF16)],
        scratch_shapes=[pltpu.VMEM((HALO + tr, tc), F32)],
        compiler_params=_cp("parallel", "parallel"),
    )(zx, zx, w, b)


def ssm_conv_bwd(zx, pre, d, w, *, col0, dcol0, ncols, name):
    L = zx.shape[0]
    tr = min(TR, L)
    tc = 1024
    cb, db_ = col0 // tc, dcol0 // tc
    n = L // tr
    per = tr // HALO
    last = L // HALO - 1

    def body(x_ref, h_ref, p_ref, np_ref, d_ref, nd_ref, w_ref, dx_ref, dw_ref, db_ref, scr, sd):
        i = pl.program_id(1)
        _fill_prev(scr, h_ref[...].astype(F32), x_ref[...].astype(F32), i, tr)

        def dpre(p, dv):
            s = _sig(p)
            return dv * s * (1.0 + p * (1.0 - s))

        dp = dpre(p_ref[...].astype(F32), d_ref[...])
        sd[pl.ds(0, tr), :] = dp
        sd[pl.ds(tr, HALO), :] = jnp.where(i == n - 1, 0.0, dpre(np_ref[...].astype(F32), nd_ref[...]))
        acc = None
        for k in range(SSM_K):
            term = sd[pl.ds(SSM_K - 1 - k, tr), :] * w_ref[k:k + 1, :]
            acc = term if acc is None else acc + term
        dx_ref[...] = acc.astype(BF16)

        @pl.when(i == 0)
        def _():
            dw_ref[...] = jnp.zeros_like(dw_ref)
            db_ref[...] = jnp.zeros_like(db_ref)

        _conv_dw(scr, dp, dw_ref, SSM_K, tr)
        db_ref[...] += jnp.sum(dp, axis=0, keepdims=True)

    nxt = lambda i: jnp.minimum((i + 1) * per, last)
    return pl.pallas_call(
        body, name=name, grid=(ncols // tc, n),
        in_specs=[pl.BlockSpec((tr, tc), lambda j, i: (i, j + cb)),
                  _prev_halo_spec(tr, tc, lambda j, i: j + cb),
                  pl.BlockSpec((tr, tc), lambda j, i: (i, j)),
                  pl.BlockSpec((HALO, tc), lambda j, i: (nxt(i), j)),
                  pl.BlockSpec((tr, tc), lambda j, i: (i, j + db_)),
                  pl.BlockSpec((HALO, tc), lambda j, i: (nxt(i), j + db_)),
                  pl.BlockSpec((SSM_K, tc), lambda j, i: (0, j + db_))],
        out_specs=[pl.BlockSpec((tr, tc), lambda j, i: (i, j)),
                   pl.BlockSpec((SSM_K, tc), lambda j, i: (0, j)),
                   pl.BlockSpec((1, tc), lambda j, i: (0, j))],
        out_shape=[jax.ShapeDtypeStruct((L, ncols), BF16), jax.ShapeDtypeStruct((SSM_K, ncols), F32),
                   jax.ShapeDtypeStruct((1, ncols), F32)],
        scratch_shapes=[pltpu.VMEM((HALO + tr, tc), F32), pltpu.VMEM((tr + HALO, tc), F32)],
        compiler_params=_cp("parallel", "arbitrary"),
    )(zx, zx, pre, pre, d, d, w)


FFN_TC = 1408


def ffn_act_fwd(u, w, b, name):
    L = u.shape[0]
    tr = min(TR, L)
    tc = FFN_TC
    nb = D_FF // tc

    def body(g_ref, hg_ref, v_ref, hv_ref, wg_ref, wv_ref, bg_ref, bv_ref, o_ref, c_ref, sg, sv):
        i = pl.program_id(1)
        _fill_prev(sg, hg_ref[...].astype(F32), g_ref[...].astype(F32), i, tr)
        _fill_prev(sv, hv_ref[...].astype(F32), v_ref[...].astype(F32), i, tr)
        ug = _conv(sg, wg_ref, FFN_K, tr) + bg_ref[...]
        uv = _conv(sv, wv_ref, FFN_K, tr) + bv_ref[...]
        c_ref[0] = ug.astype(BF16)
        c_ref[1] = uv.astype(BF16)
        o_ref[...] = (ug * _sig(ug) * uv).astype(BF16)

    blk = lambda off: pl.BlockSpec((tr, tc), lambda j, i: (i, j + off))
    wsp = lambda off: pl.BlockSpec((FFN_K, tc), lambda j, i: (0, j + off))
    bsp = lambda off: pl.BlockSpec((1, tc), lambda j, i: (0, j + off))
    return pl.pallas_call(
        body, name=name, grid=(nb, L // tr),
        in_specs=[blk(0), _prev_halo_spec(tr, tc, lambda j, i: j),
                  blk(nb), _prev_halo_spec(tr, tc, lambda j, i: j + nb),
                  wsp(0), wsp(nb), bsp(0), bsp(nb)],
        out_specs=[pl.BlockSpec((tr, tc), lambda j, i: (i, j)), pl.BlockSpec((2, tr, tc), lambda j, i: (0, i, j))],
        out_shape=[jax.ShapeDtypeStruct((L, D_FF), BF16), jax.ShapeDtypeStruct((2, L, D_FF), BF16)],
        scratch_shapes=[pltpu.VMEM((HALO + tr, tc), F32), pltpu.VMEM((HALO + tr, tc), F32)],
        compiler_params=_cp("parallel", "parallel"),
    )(u, u, u, u, w, w, b, b)


def ffn_act_bwd(u, c, dact, w, name):
    L = u.shape[0]
    tr = min(TR, L)
    tc = FFN_TC
    nb = D_FF // tc
    n = L // tr
    per = tr // HALO
    last = L // HALO - 1

    def body(g_ref, hg_ref, v_ref, hv_ref, c_ref, nc_ref, da_ref, nda_ref, wg_ref, wv_ref,
             du_ref, dw_ref, db_ref, sg, sv, dg_s, dv_s):
        i = pl.program_id(1)
        _fill_prev(sg, hg_ref[...].astype(F32), g_ref[...].astype(F32), i, tr)
        _fill_prev(sv, hv_ref[...].astype(F32), v_ref[...].astype(F32), i, tr)

        def grads(cg, cv, da):
            s = _sig(cg)
            return da * cv * s * (1.0 + cg * (1.0 - s)), da * cg * s

        dg, dv = grads(c_ref[0].astype(F32), c_ref[1].astype(F32), da_ref[...].astype(F32))
        ndg, ndv = grads(nc_ref[0].astype(F32), nc_ref[1].astype(F32), nda_ref[...].astype(F32))
        at_end = i == n - 1
        for half, (scr, d, nd, x_scr, w_ref) in enumerate(((dg_s, dg, ndg, sg, wg_ref), (dv_s, dv, ndv, sv, wv_ref))):
            scr[pl.ds(0, tr), :] = d
            scr[pl.ds(tr, HALO), :] = jnp.where(at_end, 0.0, nd)
            acc = None
            for k in range(FFN_K):
                term = scr[pl.ds(FFN_K - 1 - k, tr), :] * w_ref[k:k + 1, :]
                acc = term if acc is None else acc + term
            du_ref[half] = acc.astype(BF16)

            @pl.when(i == 0)
            def _():
                dw_ref[half] = jnp.zeros((FFN_K, tc), F32)
                db_ref[half] = jnp.zeros((1, tc), F32)

            for k in range(FFN_K):
                dw_ref[half, k:k + 1, :] += jnp.sum(d * x_scr[pl.ds(HALO - (FFN_K - 1) + k, tr), :], axis=0, keepdims=True)
            db_ref[half] += jnp.sum(d, axis=0, keepdims=True)

    blk = lambda off: pl.BlockSpec((tr, tc), lambda j, i: (i, j + off))
    wsp = lambda off: pl.BlockSpec((FFN_K, tc), lambda j, i: (0, j + off))
    nxt = lambda i: jnp.minimum((i + 1) * per, last)
    return pl.pallas_call(
        body, name=name, grid=(nb, n),
        in_specs=[blk(0), _prev_halo_spec(tr, tc, lambda j, i: j),
                  blk(nb), _prev_halo_spec(tr, tc, lambda j, i: j + nb),
                  pl.BlockSpec((2, tr, tc), lambda j, i: (0, i, j)),
                  pl.BlockSpec((2, HALO, tc), lambda j, i: (0, nxt(i), j)),
                  pl.BlockSpec((tr, tc), lambda j, i: (i, j)),
                  pl.BlockSpec((HALO, tc), lambda j, i: (nxt(i), j)),
                  wsp(0), wsp(nb)],
        out_specs=[pl.BlockSpec((2, tr, tc), lambda j, i: (0, i, j)),
                   pl.BlockSpec((2, FFN_K, tc), lambda j, i: (0, 0, j)),
                   pl.BlockSpec((2, 1, tc), lambda j, i: (0, 0, j))],
        out_shape=[jax.ShapeDtypeStruct((2, L, D_FF), BF16), jax.ShapeDtypeStruct((2, FFN_K, D_FF), F32),
                   jax.ShapeDtypeStruct((2, 1, D_FF), F32)],
        scratch_shapes=[pltpu.VMEM((HALO + tr, tc), F32), pltpu.VMEM((HALO + tr, tc), F32),
                        pltpu.VMEM((tr + HALO, tc), F32), pltpu.VMEM((tr + HALO, tc), F32)],
        compiler_params=_cp("parallel", "arbitrary"),
    )(u, u, u, u, c, c, dact, dact, w, w)


def _ln_stats(c):
    mu = jnp.mean(c, axis=-1, keepdims=True)
    cc = c - mu
    rstd = lax.rsqrt(jnp.mean(cc * cc, axis=-1, keepdims=True) + EPS)
    return cc * rstd, rstd


def cf_fwd(u, dw_w, dw_b, ln_g, ln_b, name):
    L = u.shape[0]
    D = D_MODEL
    tr = min(TR, L)

    def body(a_ref, ha_ref, g_ref, hg_ref, w_ref, b_ref, lg_ref, lb_ref, c_ref, s_ref, scr):
        i = pl.program_id(0)
        glu_h = ha_ref[...].astype(F32) * _sig(hg_ref[...].astype(F32))
        glu = a_ref[...].astype(F32) * _sig(g_ref[...].astype(F32))
        _fill_prev(scr, glu_h, glu, i, tr)
        c = _conv(scr, w_ref, CF_K, tr) + b_ref[...]
        c_ref[...] = c
        xhat, _ = _ln_stats(c)
        ln = xhat * lg_ref[...] + lb_ref[...]
        s_ref[...] = (ln * _sig(ln)).astype(BF16)

    per = tr // HALO
    halo = lambda col: pl.BlockSpec((HALO, D), lambda i: (jnp.maximum(i * per - 1, 0), col))
    return pl.pallas_call(
        body, name=name, grid=(L // tr,),
        in_specs=[pl.BlockSpec((tr, D), lambda i: (i, 0)), halo(0),
                  pl.BlockSpec((tr, D), lambda i: (i, 1)), halo(1),
                  pl.BlockSpec((CF_K, D), lambda i: (0, 0)), _vec_spec(D), _vec_spec(D), _vec_spec(D)],
        out_specs=[_row_spec(tr, D), _row_spec(tr, D)],
        out_shape=[jax.ShapeDtypeStruct((L, D), F32), jax.ShapeDtypeStruct((L, D), BF16)],
        scratch_shapes=[pltpu.VMEM((HALO + tr, D), F32)],
        compiler_params=_cp("parallel"),
    )(u, u, u, u, dw_w, dw_b, ln_g, ln_b)


def cf_bwd_ln(c, ds, ln_g, ln_b, name):
    L, D = c.shape
    tr = min(TR, L)

    def body(c_ref, ds_ref, lg_ref, lb_ref, dc_ref, dg_ref, db_ref):
        xhat, rstd = _ln_stats(c_ref[...])
        ln = xhat * lg_ref[...] + lb_ref[...]
        sg = _sig(ln)
        dln = ds_ref[...].astype(F32) * sg * (1.0 + ln * (1.0 - sg))

        @pl.when(pl.program_id(0) == 0)
        def _():
            dg_ref[...] = jnp.zeros_like(dg_ref)
            db_ref[...] = jnp.zeros_like(db_ref)

        dg_ref[...] += jnp.sum(dln * xhat, axis=0, keepdims=True)
        db_ref[...] += jnp.sum(dln, axis=0, keepdims=True)
        dxh = dln * lg_ref[...]
        dc_ref[...] = rstd * (dxh - jnp.mean(dxh, axis=-1, keepdims=True)
                              - xhat * jnp.mean(dxh * xhat, axis=-1, keepdims=True))

    return pl.pallas_call(
        body, name=name, grid=(L // tr,),
        in_specs=[_row_spec(tr, D), _row_spec(tr, D), _vec_spec(D), _vec_spec(D)],
        out_specs=[_row_spec(tr, D), _vec_spec(D), _vec_spec(D)],
        out_shape=[jax.ShapeDtypeStruct((L, D), F32), jax.ShapeDtypeStruct((1, D), F32),
                   jax.ShapeDtypeStruct((1, D), F32)],
        compiler_params=_cp("arbitrary"),
    )(c, ds, ln_g, ln_b)


def cf_bwd_conv(u, dc, dw_w, name):
    L = u.shape[0]
    D = D_MODEL
    tr = min(TR, L)
    n = L // tr

    def body(a_ref, ha_ref, g_ref, hg_ref, dc_ref, nx_ref, w_ref, du_ref, dw_ref, db_ref, sx, sd):
        i = pl.program_id(0)
        a = a_ref[...].astype(F32)
        sg = _sig(g_ref[...].astype(F32))
        _fill_prev(sx, ha_ref[...].astype(F32) * _sig(hg_ref[...].astype(F32)), a * sg, i, tr)
        dcv = dc_ref[...]
        sd[pl.ds(0, tr), :] = dcv
        sd[pl.ds(tr, HALO), :] = jnp.where(i == n - 1, 0.0, nx_ref[...])
        dglu = None
        for k in range(CF_K):
            term = sd[pl.ds(CF_K - 1 - k, tr), :] * w_ref[k:k + 1, :]
            dglu = term if dglu is None else dglu + term
        du_ref[:, 0:D] = (dglu * sg).astype(BF16)
        du_ref[:, D:2 * D] = (dglu * a * sg * (1.0 - sg)).astype(BF16)

        @pl.when(i == 0)
        def _():
            dw_ref[...] = jnp.zeros_like(dw_ref)
            db_ref[...] = jnp.zeros_like(db_ref)

        _conv_dw(sx, dcv, dw_ref, CF_K, tr)
        db_ref[...] += jnp.sum(dcv, axis=0, keepdims=True)

    per = tr // HALO
    last = L // HALO - 1
    halo = lambda col: pl.BlockSpec((HALO, D), lambda i: (jnp.maximum(i * per - 1, 0), col))
    return pl.pallas_call(
        body, name=name, grid=(n,),
        in_specs=[pl.BlockSpec((tr, D), lambda i: (i, 0)), halo(0),
                  pl.BlockSpec((tr, D), lambda i: (i, 1)), halo(1),
                  _row_spec(tr, D),
                  pl.BlockSpec((HALO, D), lambda i: (jnp.minimum((i + 1) * per, last), 0)),
                  pl.BlockSpec((CF_K, D), lambda i: (0, 0))],
        out_specs=[pl.BlockSpec((tr, 2 * D), lambda i: (i, 0)),
                   pl.BlockSpec((CF_K, D), lambda i: (0, 0)), _vec_spec(D)],
        out_shape=[jax.ShapeDtypeStruct((L, 2 * D), BF16), jax.ShapeDtypeStruct((CF_K, D), F32),
                   jax.ShapeDtypeStruct((1, D), F32)],
        scratch_shapes=[pltpu.VMEM((HALO + tr, D), F32), pltpu.VMEM((tr + HALO, D), F32)],
        compiler_params=_cp("arbitrary"),
    )(u, u, u, u, dc, dc, dw_w)


XA_TR = 512
XA_SCALE = XA_HD ** -0.5


def _xa_probs(qh, kh):
    s = _dot(qh, kh, 1, 1) * XA_SCALE
    p = jnp.exp(s - jnp.max(s, axis=-1, keepdims=True))
    return p / jnp.sum(p, axis=-1, keepdims=True)


def attn_fwd(q, kv, name):
    L, D = q.shape
    tr = min(XA_TR, L)

    def body(q_ref, kv_ref, o_ref):
        for hd in range(XA_HEADS):
            c = slice(hd * XA_HD, (hd + 1) * XA_HD)
            p = _xa_probs(q_ref[:, c], kv_ref[:, c])
            vh = kv_ref[:, D + hd * XA_HD:D + (hd + 1) * XA_HD]
            o_ref[:, c] = _dot(p.astype(BF16), vh).astype(BF16)

    return pl.pallas_call(
        body, name=name, grid=(L // tr,),
        in_specs=[_row_spec(tr, D), pl.BlockSpec((N_MEM, 2 * D), lambda i: (0, 0))],
        out_specs=_row_spec(tr, D), out_shape=jax.ShapeDtypeStruct((L, D), BF16),
        compiler_params=_cp("parallel"),
    )(q, kv)


def attn_bwd(q, kv, do, name):
    L, D = q.shape
    tr = min(XA_TR, L)

    def body(q_ref, kv_ref, do_ref, dq_ref, dkv_ref):
        @pl.when(pl.program_id(0) == 0)
        def _():
            dkv_ref[...] = jnp.zeros_like(dkv_ref)

        for hd in range(XA_HEADS):
            c = slice(hd * XA_HD, (hd + 1) * XA_HD)
            cv = slice(D + hd * XA_HD, D + (hd + 1) * XA_HD)
            qh, kh, vh, doh = q_ref[:, c], kv_ref[:, c], kv_ref[:, cv], do_ref[:, c]
            p = _xa_probs(qh, kh)
            dp = _dot(doh, vh, 1, 1)
            dkv_ref[:, cv] += _dot(p.astype(BF16), doh, 0, 0)
            ds = (p * (dp - jnp.sum(dp * p, axis=-1, keepdims=True)) * XA_SCALE).astype(BF16)
            dq_ref[:, c] = _dot(ds, kh).astype(BF16)
            dkv_ref[:, c] += _dot(ds, qh, 0, 0)

    return pl.pallas_call(
        body, name=name, grid=(L // tr,),
        in_specs=[_row_spec(tr, D), pl.BlockSpec((N_MEM, 2 * D), lambda i: (0, 0)), _row_spec(tr, D)],
        out_specs=[_row_spec(tr, D), pl.BlockSpec((N_MEM, 2 * D), lambda i: (0, 0))],
        out_shape=[jax.ShapeDtypeStruct((L, D), BF16), jax.ShapeDtypeStruct((N_MEM, 2 * D), F32)],
        compiler_params=_cp("arbitrary"),
    )(q, kv, do)


N_PAIRS = N_HEADS // 2
PAIRS_PER_GROUP = N_PAIRS // N_GROUPS
GN = N_GROUPS * D_STATE


def _softplus(x):
    t = jnp.exp(-jnp.abs(x))
    return jnp.maximum(x, 0.0) + jnp.where(t < 1e-4, t * (1.0 - 0.5 * t), jnp.log(1.0 + t))


def _dot3b(m, v, ca=1, cb=0):
    v0, v1, v2 = _split3(v)
    return _dot(m, v0, ca, cb) + _dot(m, v1, ca, cb) + _dot(m, v2, ca, cb)


def ssd_consts():
    h = lax.broadcasted_iota(jnp.int32, (LANE, D_INNER), 0)
    c = lax.broadcasted_iota(jnp.int32, (LANE, D_INNER), 1)
    expand = (c // HEAD_DIM == h).astype(BF16)
    r = lax.broadcasted_iota(jnp.int32, (CHUNK, CHUNK), 0)
    k = lax.broadcasted_iota(jnp.int32, (CHUNK, CHUNK), 1)
    tri = (k <= r).astype(BF16)
    return expand, tri


def _ssd_common(dtr_ref, prm_ref, e_ref, tri_ref):
    lane = lax.broadcasted_iota(jnp.int32, (CHUNK, LANE), 1)
    valid = lane < N_HEADS
    A = -jnp.exp(prm_ref[1:2, :])
    pre = dtr_ref[...] + prm_ref[0:1, :]
    dt = jnp.where(valid, _softplus(pre), 0.0)
    cs = _dot3b(tri_ref[...], dt * A)
    E = e_ref[...]
    dt_x = _dot3(dt, E)
    cs_x = _dot3(cs, E)
    csl_x = cs_x[CHUNK - 1:CHUNK, :]
    return dict(valid=valid, A=A, pre=pre, dt=dt, cs=cs, csT=cs.T, dt_x=dt_x, ecs_x=jnp.exp(cs_x),
                dend_x=jnp.exp(csl_x - cs_x), cd_x=jnp.exp(csl_x), D_x=_dot3(prm_ref[...], E)[2:3, :])


def ssd_fwd(xs, bc, dtr, zx, prm, ng, name):
    L = xs.shape[0]
    nc = L // CHUNK
    expand, tri = ssd_consts()

    def body(xs_ref, bc_ref, dtr_ref, z_ref, prm_ref, ng_ref, e_ref, tri_ref, y_ref, yn_ref, st_ref, state):
        @pl.when(pl.program_id(0) == 0)
        def _():
            state[...] = jnp.zeros_like(state)

        q = _ssd_common(dtr_ref, prm_ref, e_ref, tri_ref)
        cs, csT = q["cs"], q["csT"]
        xs_v = xs_ref[...]
        X = xs_v * q["dt_x"]
        Xb = X.astype(BF16)
        Xd = (X * q["dend_x"]).astype(BF16)
        ii = lax.broadcasted_iota(jnp.int32, (CHUNK, CHUNK), 0)
        jj = lax.broadcasted_iota(jnp.int32, (CHUNK, CHUNK), 1)
        tril = jj <= ii
        first = jj < HEAD_DIM
        for g in range(N_GROUPS):
            Bg = bc_ref[:, g * D_STATE:(g + 1) * D_STATE]
            Cg = bc_ref[:, GN + g * D_STATE:GN + (g + 1) * D_STATE]
            S = _dot(Cg, Bg, 1, 1)
            for pr in range(PAIRS_PER_GROUP):
                pair = g * PAIRS_PER_GROUP + pr
                cols = slice(pair * LANE, (pair + 1) * LANE)
                Xp = Xb[:, cols]
                ys = []
                for h in (2 * pair, 2 * pair + 1):
                    seg = cs[:, h:h + 1] - csT[h:h + 1, :]
                    M = (S * jnp.exp(jnp.where(tril, seg, NEG))).astype(BF16)
                    ys.append(_dot(M, Xp))
                prevT = state[pair]
                st_ref[0, pair] = prevT
                yoff = _dot(Cg, prevT.astype(BF16)) * q["ecs_x"][:, cols]
                y_ref[:, cols] = jnp.where(first, ys[0], ys[1]) + yoff + xs_v[:, cols] * q["D_x"][:, cols]
                state[pair] = prevT * q["cd_x"][:, cols] + _dot(Bg, Xd[:, cols], 0, 0)
        z = z_ref[...].astype(F32)
        gt = y_ref[...] * z * _sig(z)
        yn_ref[...] = (gt * _rms(gt) * ng_ref[...]).astype(BF16)

    row = lambda w: pl.BlockSpec((CHUNK, w), lambda c: (c, 0))
    const = lambda a: pl.BlockSpec(a.shape, lambda c: (0,) * a.ndim)
    return pl.pallas_call(
        body, name=name, grid=(nc,),
        in_specs=[row(D_INNER), row(2 * GN), row(LANE), row(D_INNER), const(prm), const(ng), const(expand), const(tri)],
        out_specs=[row(D_INNER), row(D_INNER), pl.BlockSpec((1, N_PAIRS, D_STATE, LANE), lambda c: (c, 0, 0, 0))],
        out_shape=[jax.ShapeDtypeStruct((L, D_INNER), F32), jax.ShapeDtypeStruct((L, D_INNER), BF16),
                   jax.ShapeDtypeStruct((nc, N_PAIRS, D_STATE, LANE), F32)],
        scratch_shapes=[pltpu.VMEM((N_PAIRS, D_STATE, LANE), F32)],
        compiler_params=_cp("arbitrary"),
    )(xs, bc, dtr, zx, prm, ng, expand, tri)


def ssd_bwd(dyn, y, zx, xs, bc, dtr, st, prm, ng, name):
    L = xs.shape[0]
    nc = L // CHUNK
    expand, tri = ssd_consts()

    def body(dyn_ref, y_ref, z_ref, xs_ref, bc_ref, dtr_ref, st_ref, prm_ref, ng_ref, e_ref, tri_ref,
             dxbc_ref, dz_ref, ddtr_ref, dng_ref, dprm_ref, dstate, g_cs, g_q, dX, g_row):
        step = pl.program_id(0)

        @pl.when(step == 0)
        def _():
            dstate[...] = jnp.zeros_like(dstate)
            dng_ref[...] = jnp.zeros_like(dng_ref)
            dprm_ref[...] = jnp.zeros_like(dprm_ref)
            g_row[...] = jnp.zeros_like(g_row)

        q = _ssd_common(dtr_ref, prm_ref, e_ref, tri_ref)
        cs, csT, E = q["cs"], q["csT"], e_ref[...]
        xs_v = xs_ref[...]
        X = xs_v * q["dt_x"]
        Xb = X.astype(BF16)
        Xd_f = X * q["dend_x"]
        Xd = Xd_f.astype(BF16)

        yv = y_ref[...]
        z = z_ref[...].astype(F32)
        sz = _sig(z)
        silu = z * sz
        gt = yv * silu
        r = _rms(gt)
        gn = gt * r
        dyn_v = dyn_ref[...]
        dng_ref[...] += jnp.sum(dyn_v * gn, axis=0, keepdims=True)
        dgn = dyn_v * ng_ref[...]
        dgt = r * (dgn - gn * jnp.mean(dgn * gn, axis=-1, keepdims=True))
        dY = dgt * silu
        dz_ref[...] = (dgt * yv * sz * (1.0 + z * (1.0 - sz))).astype(BF16)
        dYb = dY.astype(BF16)
        g_row[1:2, :] += jnp.sum(dY * xs_v, axis=0, keepdims=True)

        ii = lax.broadcasted_iota(jnp.int32, (CHUNK, CHUNK), 0)
        jj = lax.broadcasted_iota(jnp.int32, (CHUNK, CHUNK), 1)
        tril = jj <= ii
        triu = jj >= ii
        first = jj < HEAD_DIM
        lane_row = lax.broadcasted_iota(jnp.int32, (1, LANE), 1)
        sub_col = lax.broadcasted_iota(jnp.int32, (CHUNK, 1), 0)
        dcs_col = jnp.zeros((CHUNK, LANE), F32)
        dcs_rowT = jnp.zeros((LANE, CHUNK), F32)
        for g in range(N_GROUPS):
            Bg = bc_ref[:, g * D_STATE:(g + 1) * D_STATE]
            Cg = bc_ref[:, GN + g * D_STATE:GN + (g + 1) * D_STATE]
            S = _dot(Cg, Bg, 1, 1)
            ST = _dot(Bg, Cg, 1, 1)
            dS = jnp.zeros((CHUNK, CHUNK), F32)
            dCg = jnp.zeros((CHUNK, D_STATE), F32)
            dBg = jnp.zeros((CHUNK, D_STATE), F32)
            for pr in range(PAIRS_PER_GROUP):
                pair = g * PAIRS_PER_GROUP + pr
                cols = slice(pair * LANE, (pair + 1) * LANE)
                Xp = Xb[:, cols]
                dYp_f = dY[:, cols]
                dYp = dYb[:, cols]
                prevT = st_ref[0, pair]
                prevTb = prevT.astype(BF16)
                dst = dstate[pair]
                dstb = dst.astype(BF16)
                ecs_p = q["ecs_x"][:, cols]
                g_cs[:, cols] = dYp_f * (_dot(Cg, prevTb) * ecs_p)
                dWb = (dYp_f * ecs_p).astype(BF16)
                dprev = dst * q["cd_x"][:, cols] + _dot(Cg, dWb, 0, 0)
                dCg = dCg + _dot(dWb, prevTb, 1, 1)
                g_row[0:1, cols] = jnp.sum(dst * prevT, axis=0, keepdims=True)
                dXp = None
                for hh, h in enumerate((2 * pair, 2 * pair + 1)):
                    mine = first if hh == 0 else jnp.logical_not(first)
                    seg = cs[:, h:h + 1] - csT[h:h + 1, :]
                    lam = jnp.exp(jnp.where(tril, seg, NEG))
                    dM = _dot(jnp.where(mine, dYp, jnp.zeros_like(dYp)), Xp, 1, 1)
                    dS = dS + dM * lam
                    Gm = dM * (S * lam)
                    dcs_col = dcs_col + jnp.sum(Gm, axis=1, keepdims=True) * (lane_row == h).astype(F32)
                    dcs_rowT = dcs_rowT + (sub_col == h).astype(F32) * jnp.sum(Gm, axis=0, keepdims=True)
                    MT = (ST * jnp.exp(jnp.where(triu, -seg, NEG))).astype(BF16)
                    t = _dot(MT, dYp)
                    dXp = t if dXp is None else jnp.where(first, dXp, t)
                dXd = _dot(Bg, dstb)
                dBg = dBg + _dot(Xd[:, cols], dstb, 1, 1)
                g_q[:, cols] = dXd * Xd_f[:, cols]
                dX[:, cols] = dXp + dXd * q["dend_x"][:, cols]
                dstate[pair] = dprev
            dSb = dS.astype(BF16)
            dxbc_ref[:, D_INNER + g * D_STATE:D_INNER + (g + 1) * D_STATE] = dBg + _dot(dSb, Cg, 0, 0)
            dxbc_ref[:, D_INNER + GN + g * D_STATE:D_INNER + GN + (g + 1) * D_STATE] = dCg + _dot(dSb, Bg)
        dXv = dX[...]
        dxbc_ref[:, 0:D_INNER] = q["D_x"] * dY + dXv * q["dt_x"]
        r_dt = _dot3(dXv * xs_v, E, 1, 1)
        r_cs = _dot3(g_cs[...], E, 1, 1)
        r_q = _dot3(g_q[...], E, 1, 1)
        r_row = _dot3(g_row[...], E, 1, 1)
        cd = jnp.exp(cs[CHUNK - 1:CHUNK, :])
        dcs_last = jnp.sum(r_q, axis=0, keepdims=True) + r_row[0:1, :] * cd
        dcs = r_cs - r_q + dcs_col - dcs_rowT.T + jnp.where(sub_col == CHUNK - 1, dcs_last, 0.0)
        da = _dot3b(tri_ref[...], dcs, 0, 0)
        dpre = jnp.where(q["valid"], (r_dt + da * q["A"]) * _sig(q["pre"]), 0.0)
        ddtr_ref[...] = dpre
        dprm_ref[0:1, :] += jnp.sum(dpre, axis=0, keepdims=True)
        dprm_ref[1:2, :] += jnp.sum(da * q["dt"], axis=0, keepdims=True) * q["A"]
        dprm_ref[2:3, :] = r_row[1:2, :]

    rev = lambda w: pl.BlockSpec((CHUNK, w), lambda c: (nc - 1 - c, 0))
    const = lambda a: pl.BlockSpec(a.shape, lambda c: (0,) * a.ndim)
    return pl.pallas_call(
        body, name=name, grid=(nc,),
        in_specs=[rev(D_INNER), rev(D_INNER), rev(D_INNER), rev(D_INNER), rev(2 * GN), rev(LANE),
                  pl.BlockSpec((1, N_PAIRS, D_STATE, LANE), lambda c: (nc - 1 - c, 0, 0, 0)),
                  const(prm), const(ng), const(expand), const(tri)],
        out_specs=[rev(CONV_DIM), rev(D_INNER), rev(LANE),
                   pl.BlockSpec((1, D_INNER), lambda c: (0, 0)), pl.BlockSpec((8, LANE), lambda c: (0, 0))],
        out_shape=[jax.ShapeDtypeStruct((L, CONV_DIM), F32), jax.ShapeDtypeStruct((L, D_INNER), BF16),
                   jax.ShapeDtypeStruct((L, LANE), F32), jax.ShapeDtypeStruct((1, D_INNER), F32),
                   jax.ShapeDtypeStruct((8, LANE), F32)],
        scratch_shapes=[pltpu.VMEM((N_PAIRS, D_STATE, LANE), F32), pltpu.VMEM((CHUNK, D_INNER), F32),
                        pltpu.VMEM((CHUNK, D_INNER), F32), pltpu.VMEM((CHUNK, D_INNER), F32),
                        pltpu.VMEM((8, D_INNER), F32)],
        compiler_params=_cp("arbitrary"),
    )(dyn, y, zx, xs, bc, dtr, st, prm, ng, expand, tri)


def _ssd_weights(W, j):
    w_in = W["ssm_in_w"][j]
    nzx = D_INNER + CONV_DIM
    wdt = jnp.pad(w_in[:, nzx:], ((0, 0), (0, LANE - N_HEADS)))
    prm = jnp.zeros((8, LANE), F32)
    prm = prm.at[0, :N_HEADS].set(W["ssm_dt_bias"][j]).at[1, :N_HEADS].set(W["ssm_A_log"][j])
    prm = prm.at[2, :N_HEADS].set(W["ssm_D"][j])
    return dict(wdt=wdt, cw=W["ssm_conv_w"][j], cb=W["ssm_conv_b"][j].reshape(1, CONV_DIM), prm=prm,
                ng=W["ssm_norm_g"][j].reshape(1, D_INNER))


def ssd_layer_fwd(h, W, j, tag):
    p = _ssd_weights(W, j)
    zx = mm(h, W["ssm_in_w"], layer=j, b_n=D_INNER + CONV_DIM, out_dtype=BF16, name=f"{tag}_zx")
    dtr = mm(h, p["wdt"], name=f"{tag}_dt")
    xs, pre_x = ssm_conv_fwd(zx, p["cw"], p["cb"], col0=D_INNER, ncols=D_INNER, wcol0=0, out_dtype=F32,
                             name=f"{tag}_convx")
    bc, pre_bc = ssm_conv_fwd(zx, p["cw"], p["cb"], col0=2 * D_INNER, ncols=2 * GN, wcol0=D_INNER, out_dtype=BF16,
                              name=f"{tag}_convbc")
    y, yn, st = ssd_fwd(xs, bc, dtr, zx, p["prm"], p["ng"], name=f"{tag}_scan")
    f = mm(yn, W["ssm_out_w"], layer=j, name=f"{tag}_out")
    return f, dict(h=h, zx=zx, dtr=dtr, xs=xs, bc=bc, pre_x=pre_x, pre_bc=pre_bc, y=y, yn=yn, st=st, p=p)


def ssd_layer_bwd(df, ctx, W, GB, j, tag):
    p = ctx["p"]
    h = ctx["h"]
    dyn = mm(df, W["ssm_out_w"], layer=j, tb=True, name=f"{tag}_b_dyn")
    GB["ssm_out_w"] = mm(ctx["yn"], df, ta=True, into=(GB["ssm_out_w"], j, 0), name=f"{tag}_b_gwo")
    dxbc, dz, ddtr, dng, dprm = ssd_bwd(dyn, ctx["y"], ctx["zx"], ctx["xs"], ctx["bc"], ctx["dtr"], ctx["st"],
                                        p["prm"], p["ng"], name=f"{tag}_b_scan")
    dx1, dcw1, dcb1 = ssm_conv_bwd(ctx["zx"], ctx["pre_x"], dxbc, p["cw"], col0=D_INNER, dcol0=0, ncols=D_INNER,
                                   name=f"{tag}_b_convx")
    dx2, dcw2, dcb2 = ssm_conv_bwd(ctx["zx"], ctx["pre_bc"], dxbc, p["cw"], col0=2 * D_INNER, dcol0=D_INNER,
                                   ncols=2 * GN, name=f"{tag}_b_convbc")
    dh = mm(dz, W["ssm_in_w"], layer=j, tb=True, b_k0=0, name=f"{tag}_b_dh1")
    dh = mm(dx1, W["ssm_in_w"], layer=j, tb=True, b_k0=D_INNER, acc=dh, name=f"{tag}_b_dh2")
    dh = mm(dx2, W["ssm_in_w"], layer=j, tb=True, b_k0=2 * D_INNER, acc=dh, name=f"{tag}_b_dh3")
    dh = mm(ddtr, p["wdt"], tb=True, acc=dh, name=f"{tag}_b_dh4")
    g_in = jnp.concatenate([mm(h, dz, ta=True, out_dtype=BF16, name=f"{tag}_b_gz"),
                            mm(h, dx1, ta=True, out_dtype=BF16, name=f"{tag}_b_gx"),
                            mm(h, dx2, ta=True, out_dtype=BF16, name=f"{tag}_b_gbc"),
                            mm(h, ddtr, ta=True, out_dtype=BF16, name=f"{tag}_b_gdt")[:, :N_HEADS]], axis=1)
    return dh, dict(ssm_in_w=g_in, ssm_conv_w=jnp.concatenate([dcw1, dcw2], axis=1),
                    ssm_conv_b=jnp.concatenate([dcb1, dcb2], axis=1)[0], ssm_dt_bias=dprm[0, :N_HEADS],
                    ssm_A_log=dprm[1, :N_HEADS], ssm_D=dprm[2, :N_HEADS], ssm_norm_g=dng[0])


def cf_layer_fwd(h, W, j, tag):
    u = mm(h, W["cf_pw1_w"], layer=j, bias=W["cf_pw1_b"][j], out_dtype=BF16, name=f"{tag}_pw1")
    c, s = cf_fwd(u, W["cf_dw_w"][j], W["cf_dw_b"][j].reshape(1, -1), W["cf_ln_g"][j].reshape(1, -1),
                  W["cf_ln_b"][j].reshape(1, -1), name=f"{tag}_conv")
    f = mm(s, W["cf_pw2_w"], layer=j, bias=W["cf_pw2_b"][j], name=f"{tag}_pw2")
    return f, dict(h=h, u=u, c=c, s=s)


def cf_layer_bwd(df, ctx, W, GB, j, tag):
    h = ctx["h"]
    ds = mm(df, W["cf_pw2_w"], layer=j, tb=True, name=f"{tag}_b_ds")
    GB["cf_pw2_w"] = mm(ctx["s"], df, ta=True, into=(GB["cf_pw2_w"], j, 0), name=f"{tag}_b_gpw2")
    g_b2 = colsum(df, name=f"{tag}_b_gb2")
    dc, dlg, dlb = cf_bwd_ln(ctx["c"], ds, W["cf_ln_g"][j].reshape(1, -1), W["cf_ln_b"][j].reshape(1, -1),
                             name=f"{tag}_b_ln")
    du, ddw, ddb = cf_bwd_conv(ctx["u"], dc, W["cf_dw_w"][j], name=f"{tag}_b_conv")
    dh = mm(du, W["cf_pw1_w"], layer=j, tb=True, name=f"{tag}_b_dh")
    GB["cf_pw1_w"] = mm(h, du, ta=True, into=(GB["cf_pw1_w"], j, 0), name=f"{tag}_b_gpw1")
    g_b1 = colsum(du, name=f"{tag}_b_gb1")
    return dh, dict(cf_pw1_b=g_b1[0], cf_dw_w=ddw, cf_dw_b=ddb[0], cf_ln_g=dlg[0], cf_ln_b=dlb[0], cf_pw2_b=g_b2[0])


def xa_layer_fwd(h, mem, W, i, tag):
    m = norm_fwd(mem, W["xa_mem_g"][i], name=f"{tag}_memnorm")
    kv = mm(m, W["xa_kv_w"], layer=i, out_dtype=BF16, name=f"{tag}_kv")
    q = mm(h, W["xa_q_w"], layer=i, out_dtype=BF16, name=f"{tag}_q")
    o = attn_fwd(q, kv, name=f"{tag}_attn")
    f = mm(o, W["xa_o_w"], layer=i, name=f"{tag}_o")
    return f, dict(h=h, m=m, kv=kv, q=q, o=o)


def xa_layer_bwd(df, ctx, mem, W, GB, i, tag):
    h = ctx["h"]
    do = mm(df, W["xa_o_w"], layer=i, tb=True, out_dtype=BF16, name=f"{tag}_b_do")
    GB["xa_o_w"] = mm(ctx["o"], df, ta=True, into=(GB["xa_o_w"], i, 0), name=f"{tag}_b_go")
    dq, dkv = attn_bwd(ctx["q"], ctx["kv"], do, name=f"{tag}_b_attn")
    dh = mm(dq, W["xa_q_w"], layer=i, tb=True, name=f"{tag}_b_dh")
    GB["xa_q_w"] = mm(h, dq, ta=True, into=(GB["xa_q_w"], i, 0), name=f"{tag}_b_gq")
    GB["xa_kv_w"] = mm(ctx["m"], dkv, ta=True, into=(GB["xa_kv_w"], i, 0), name=f"{tag}_b_gkv")
    dm = mm(dkv, W["xa_kv_w"], layer=i, tb=True, name=f"{tag}_b_dm")
    g_mg = norm_dg(mem, dm, name=f"{tag}_b_gmem")
    return dh, dict(xa_mem_g=g_mg[0])


def ffn_layer_fwd(h, W, i, tag):
    cw, cb = W["ffn_conv_w"][i], W["ffn_conv_b"][i].reshape(1, -1)
    u = mm(h, W["ffn_in_w"], layer=i, out_dtype=BF16, name=f"{tag}_in")
    act, c = ffn_act_fwd(u, cw, cb, name=f"{tag}_act")
    f = mm(act, W["ffn_out_w"], layer=i, name=f"{tag}_out")
    return f, dict(h=h, u=u, c=c, act=act)


def ffn_layer_bwd(df, ctx, W, GB, i, tag):
    h = ctx["h"]
    dact = mm(df, W["ffn_out_w"], layer=i, tb=True, out_dtype=BF16, name=f"{tag}_b_dact")
    GB["ffn_out_w"] = mm(ctx["act"], df, ta=True, into=(GB["ffn_out_w"], i, 0), name=f"{tag}_b_gout")
    du, dcw, dcb = ffn_act_bwd(ctx["u"], ctx["c"], dact, W["ffn_conv_w"][i], name=f"{tag}_b_act")
    dh = None
    for half in range(2):
        dh = mm(du, W["ffn_in_w"], a_idx=half, layer=i, tb=True, b_k0=half * D_FF, acc=dh, name=f"{tag}_b_dh{half}")
        GB["ffn_in_w"] = mm(h, du, ta=True, layer=half, into=(GB["ffn_in_w"], i, half * D_FF), name=f"{tag}_b_gin{half}")
    cat = lambda a: jnp.concatenate([a[0], a[1]], axis=-1)
    return dh, dict(ffn_conv_w=cat(dcw), ffn_conv_b=cat(dcb)[0])


def local_step(x, mem, target, W):
    subs = [(i, s) for i in range(DEPTH) for s in range(3)]
    ng = W["norm_g"]

    def fwd(i, s, h):
        tag = f"l{i}s{s}"
        if s == 0:
            return ssd_layer_fwd(h, W, i // 2, tag) if i % 2 == 0 else cf_layer_fwd(h, W, i // 2, tag)
        if s == 1:
            return xa_layer_fwd(h, mem, W, i, tag)
        return ffn_layer_fwd(h, W, i, tag)

    GB = {n: jnp.zeros(W[n].shape, BF16) for n in BIG if n != "ssm_in_w"}

    def bwd(i, s, df, ctx):
        tag = f"l{i}s{s}"
        if s == 0:
            return (ssd_layer_bwd if i % 2 == 0 else cf_layer_bwd)(df, ctx, W, GB, i // 2, tag)
        if s == 1:
            return xa_layer_bwd(df, ctx, mem, W, GB, i, tag)
        return ffn_layer_bwd(df, ctx, W, GB, i, tag)

    h = norm_fwd(x, ng[0, 0], name="norm0")
    saved = []
    dxp = loss = None
    for k, (i, s) in enumerate(subs):
        f, ctx = fwd(i, s, h)
        saved.append((x, f, ctx))
        if k + 1 < len(subs):
            ni, ns = subs[k + 1]
            x, h = bnd_fwd(x, f, ng[i, 2 * s + 1], ng[ni, 2 * ns], name=f"bnd{k}")
        else:
            dxp, loss = final_fwd(x, f, ng[i, 2 * s + 1], target, name="final")

    grads = {}

    def put(name, idx, val):
        grads.setdefault(name, {})[idx] = val

    i, s = subs[-1]
    top = bnd_bwd(dxp, post=(saved[-1][1], ng[i, 2 * s + 1]), name="bbnd_top")
    put("norm_g", (i, 2 * s + 1), top["dgpost"][0])
    df = top["df"]
    for k in range(len(subs) - 1, -1, -1):
        i, s = subs[k]
        xk, _, ctx = saved[k]
        dh, gw = bwd(i, s, df, ctx)
        for name, val in gw.items():
            put(name, i // 2 if name.startswith(("ssm_", "cf_")) else i, val)
        if k > 0:
            pi, ps = subs[k - 1]
            r = bnd_bwd(dxp, pre=(xk, ng[i, 2 * s], dh), post=(saved[k - 1][1], ng[pi, 2 * ps + 1]), name=f"bbnd{k}")
            put("norm_g", (pi, 2 * ps + 1), r["dgpost"][0])
            df = r["df"]
        else:
            r = bnd_bwd(dxp, pre=(xk, ng[i, 2 * s], dh), name="bbnd0")
        put("norm_g", (i, 2 * s), r["dgpre"][0])
        dxp = r["dx"]

    out = dict(GB)
    for name, d in grads.items():
        if name == "norm_g":
            out[name] = jnp.stack([jnp.stack([d[(i, t)] for t in range(6)]) for i in range(DEPTH)])
        else:
            out[name] = jnp.stack([d[j] for j in sorted(d)])
    return loss, dxp, out


ANY = pl.BlockSpec(memory_space=pl.ANY)


def _pos():
    return lax.axis_index("x"), lax.axis_index("y"), lax.axis_index("c")


def all_gather(shard, name):
    R, C = shard.shape

    def body(x_ref, out_ref, send_sems, recv_sems, local_sem):
        x, y, c = _pos()
        me, sibling = (x, y, c), (x, y, 1 - c)
        chips = [(1 - x, y), (x, 1 - y), (1 - x, 1 - y)]

        def slot(px, py, pc):
            return out_ref.at[4 * px + 2 * py + pc]

        def copy(k, block, to, src=None):
            return pltpu.make_async_remote_copy(
                src_ref=slot(*block) if src is None else src, dst_ref=slot(*block),
                send_sem=send_sems.at[k], recv_sem=recv_sems.at[k], device_id=to, device_id_type=MESH)

        mine = pltpu.make_async_copy(x_ref, slot(*me), local_sem)
        mine.start()
        first = [copy(0, me, sibling, src=x_ref)]
        first += [copy(1 + j, me, (*chip, c), src=x_ref) for j, chip in enumerate(chips)]
        for cp in first:
            cp.start()
        passed = [copy(4 + j, (*chip, c), sibling) for j, chip in enumerate(chips)]
        for j, chip in enumerate(chips):
            copy(1 + j, (*chip, c), me).wait_recv()
            passed[j].start()
        copy(0, sibling, me).wait_recv()
        for j, chip in enumerate(chips):
            copy(4 + j, (*chip, 1 - c), me).wait_recv()
        for cp in first + passed:
            cp.wait_send()
        mine.wait()

    return pl.pallas_call(
        body, name=name, out_shape=jax.ShapeDtypeStruct((N_DEV, R, C), shard.dtype),
        in_specs=[ANY], out_specs=ANY,
        scratch_shapes=[pltpu.SemaphoreType.DMA((7,)), pltpu.SemaphoreType.DMA((7,)), pltpu.SemaphoreType.DMA(())],
    )(shard)


def _win(ref, kind, k, a, b):
    if kind == "lead":
        return ref.at[k]
    if kind == "row":
        return ref.at[:, pl.ds(pl.multiple_of(k * a, 16), a), :]
    return ref.at[:, :, pl.ds(pl.multiple_of(k * b, LANE), b)]


def _full_shape(shard_shape, kind):
    n, a, b = shard_shape
    return {"lead": (N_DEV, n, a, b), "row": (n, N_DEV * a, b), "col": (n, a, N_DEV * b)}[kind]


def all_gather_big(shards, kinds, name):
    nw = len(shards)
    geo = [s.shape[1:] for s in shards]

    def body(*refs):
        x_refs, o_refs = refs[:nw], refs[nw:2 * nw]
        send_sems, recv_sems, local_sems = refs[2 * nw:]
        x, y, c = _pos()
        me, sibling = (x, y, c), (x, y, 1 - c)
        chips = [(1 - x, y), (x, 1 - y), (1 - x, 1 - y)]

        def slot(w, px, py, pc):
            return _win(o_refs[w], kinds[w], 4 * px + 2 * py + pc, *geo[w])

        def copy(w, k, block, to, src=None):
            return pltpu.make_async_remote_copy(
                src_ref=slot(w, *block) if src is None else src, dst_ref=slot(w, *block),
                send_sem=send_sems.at[7 * w + k], recv_sem=recv_sems.at[7 * w + k], device_id=to, device_id_type=MESH)

        mine = [pltpu.make_async_copy(x_refs[w], slot(w, *me), local_sems.at[w]) for w in range(nw)]
        for cp in mine:
            cp.start()
        first = []
        for w in range(nw):
            first.append(copy(w, 0, me, sibling, src=x_refs[w]))
            first += [copy(w, 1 + j, me, (*chip, c), src=x_refs[w]) for j, chip in enumerate(chips)]
        for cp in first:
            cp.start()
        passed = []
        for w in range(nw):
            for j, chip in enumerate(chips):
                copy(w, 1 + j, (*chip, c), me).wait_recv()
                cp = copy(w, 4 + j, (*chip, c), sibling)
                cp.start()
                passed.append(cp)
        for w in range(nw):
            copy(w, 0, sibling, me).wait_recv()
            for j, chip in enumerate(chips):
                copy(w, 4 + j, (*chip, 1 - c), me).wait_recv()
        for cp in first + passed:
            cp.wait_send()
        for cp in mine:
            cp.wait()

    return pl.pallas_call(
        body, name=name,
        out_shape=[jax.ShapeDtypeStruct(_full_shape(s.shape, k), s.dtype) for s, k in zip(shards, kinds)],
        in_specs=[ANY] * nw, out_specs=[ANY] * nw,
        scratch_shapes=[pltpu.SemaphoreType.DMA((7 * nw,)), pltpu.SemaphoreType.DMA((7 * nw,)),
                        pltpu.SemaphoreType.DMA((nw,))],
    )(*shards)


def rs_sibling(gs, kinds, geo, name):
    nw = len(gs)

    def body(*refs):
        g_refs, o_refs = refs[:nw], refs[nw:2 * nw]
        send_sems, recv_sems = refs[2 * nw:]
        x, y, c = _pos()
        cps = [pltpu.make_async_remote_copy(
            src_ref=_win(g_refs[w], kinds[w], 2 * j + 1 - c, *geo[w][1:]), dst_ref=o_refs[w].at[j],
            send_sem=send_sems.at[4 * w + j], recv_sem=recv_sems.at[4 * w + j],
            device_id=(x, y, 1 - c), device_id_type=MESH) for w in range(nw) for j in range(4)]
        for cp in cps:
            cp.start()
        for cp in cps:
            cp.wait()

    return pl.pallas_call(
        body, name=name, out_shape=[jax.ShapeDtypeStruct((4, *geo[w]), gs[w].dtype) for w in range(nw)],
        in_specs=[ANY] * nw, out_specs=[ANY] * nw,
        scratch_shapes=[pltpu.SemaphoreType.DMA((4 * nw,)), pltpu.SemaphoreType.DMA((4 * nw,))],
    )(*gs)


def rs_add(g, kind, got, name):
    _, n, a, b = got.shape
    ta = a if a <= 512 else 256
    per = a // ta
    core = lax.axis_index("c").astype(jnp.int32).reshape(1)

    def body(c_ref, g_ref, got_ref, o_ref):
        o_ref[...] = (g_ref[...].astype(F32) + got_ref[...].astype(F32)).astype(o_ref.dtype)

    if kind == "lead":
        g_spec = pl.BlockSpec((None, None, ta, b), lambda j, l, r, c_ref: (2 * j + c_ref[0], l, r, 0))
    elif kind == "row":
        g_spec = pl.BlockSpec((None, ta, b), lambda j, l, r, c_ref: (l, (2 * j + c_ref[0]) * per + r, 0))
    else:
        g_spec = pl.BlockSpec((None, ta, b), lambda j, l, r, c_ref: (l, r, 2 * j + c_ref[0]))
    blk = pl.BlockSpec((None, None, ta, b), lambda j, l, r, c_ref: (j, l, r, 0))
    return pl.pallas_call(
        body, name=name, out_shape=jax.ShapeDtypeStruct(got.shape, got.dtype),
        grid_spec=pltpu.PrefetchScalarGridSpec(num_scalar_prefetch=1, grid=(4, n, per),
                                               in_specs=[g_spec, blk], out_specs=blk),
        compiler_params=_cp("parallel", "parallel", "parallel"),
    )(core, g, got)


def rs_chips(s1s, name):
    nw = len(s1s)

    def body(*refs):
        s_refs, o_refs = refs[:nw], refs[nw:2 * nw]
        send_sems, recv_sems, local_sems = refs[2 * nw:]
        x, y, c = _pos()
        jme = 2 * x + y
        peers = [(1 - x, y), (x, 1 - y), (1 - x, 1 - y)]
        local = [pltpu.make_async_copy(s_refs[w].at[jme], o_refs[w].at[jme], local_sems.at[w]) for w in range(nw)]
        for cp in local:
            cp.start()

        def copy(w, k, src_slot, dst_slot, peer):
            return pltpu.make_async_remote_copy(
                src_ref=s_refs[w].at[src_slot], dst_ref=o_refs[w].at[dst_slot], send_sem=send_sems.at[3 * w + k],
                recv_sem=recv_sems.at[3 * w + k], device_id=(*peer, c), device_id_type=MESH)

        sends = [copy(w, k, 2 * px + py, jme, (px, py)) for w in range(nw) for k, (px, py) in enumerate(peers)]
        for cp in sends:
            cp.start()
        for w in range(nw):
            for k, (px, py) in enumerate(peers):
                copy(w, k, jme, 2 * px + py, (px, py)).wait_recv()
        for cp in sends:
            cp.wait_send()
        for cp in local:
            cp.wait()

    return pl.pallas_call(
        body, name=name, out_shape=[jax.ShapeDtypeStruct(s.shape, s.dtype) for s in s1s],
        in_specs=[ANY] * nw, out_specs=[ANY] * nw,
        scratch_shapes=[pltpu.SemaphoreType.DMA((3 * nw,)), pltpu.SemaphoreType.DMA((3 * nw,)),
                        pltpu.SemaphoreType.DMA((nw,))],
    )(*s1s)


def small_exchange(sh, rep, name):
    _, Rs, C = sh.shape
    Rr = rep.shape[0]

    def body(sh_ref, rep_ref, sh_out, rep_out, send_sems, recv_sems, local_sems):
        x, y, c = _pos()
        me = 4 * x + 2 * y + c
        l1 = pltpu.make_async_copy(sh_ref.at[me], sh_out.at[me], local_sems.at[0])
        l2 = pltpu.make_async_copy(rep_ref, rep_out.at[me], local_sems.at[1])
        l1.start()
        l2.start()

        def flip(v, bit):
            return 1 - v if bit else v

        sends, recvs = [], []
        for r in range(1, N_DEV):
            peer = (flip(x, r & 4), flip(y, r & 2), flip(c, r & 1))
            pid = 4 * peer[0] + 2 * peer[1] + peer[2]
            k = 2 * (r - 1)
            mk = lambda src, dst, kk: pltpu.make_async_remote_copy(
                src_ref=src, dst_ref=dst, send_sem=send_sems.at[kk], recv_sem=recv_sems.at[kk],
                device_id=peer, device_id_type=MESH)
            sends += [mk(sh_ref.at[pid], sh_out.at[me], k), mk(rep_ref, rep_out.at[me], k + 1)]
            recvs += [mk(sh_ref.at[me], sh_out.at[pid], k), mk(rep_ref, rep_out.at[pid], k + 1)]
        for cp in sends:
            cp.start()
        for cp in recvs:
            cp.wait_recv()
        for cp in sends:
            cp.wait_send()
        l1.wait()
        l2.wait()

    n = 2 * (N_DEV - 1)
    return pl.pallas_call(
        body, name=name,
        out_shape=[jax.ShapeDtypeStruct((N_DEV, Rs, C), sh.dtype), jax.ShapeDtypeStruct((N_DEV, *rep.shape), rep.dtype)],
        in_specs=[ANY, ANY], out_specs=[ANY, ANY],
        scratch_shapes=[pltpu.SemaphoreType.DMA((n,)), pltpu.SemaphoreType.DMA((n,)), pltpu.SemaphoreType.DMA((2,))],
    )(sh, rep)


def adam_slots(w, m, v, slots, name):
    S, n, a, b = slots.shape
    ta = max(t for t in range(16, min(a, 512) + 1, 8)
             if a % t == 0 and t * S * b * slots.dtype.itemsize <= 4 * 1024 * 1024)

    def body(w_ref, m_ref, v_ref, s_ref, g_ref, d_ref, m2_ref, v2_ref):
        gv = s_ref[0].astype(F32)
        for k in range(1, S):
            gv = gv + s_ref[k].astype(F32)
        m2 = ADAM_B1 * m_ref[...] + (1.0 - ADAM_B1) * gv
        v2 = ADAM_B2 * v_ref[...] + (1.0 - ADAM_B2) * (gv * gv)
        m_hat = m2 / (1.0 - ADAM_B1 ** ADAM_STEP)
        v_hat = v2 / (1.0 - ADAM_B2 ** ADAM_STEP)
        g_ref[...] = gv
        d_ref[...] = -ADAM_LR * (m_hat / (jnp.sqrt(v_hat) + ADAM_EPS) + ADAM_WD * w_ref[...])
        m2_ref[...] = m2
        v2_ref[...] = v2

    spec = pl.BlockSpec((None, ta, b), lambda l, r: (l, r, 0))
    return pl.pallas_call(
        body, name=name, grid=(n, a // ta),
        in_specs=[spec] * 3 + [pl.BlockSpec((S, None, ta, b), lambda l, r: (0, l, r, 0))], out_specs=[spec] * 4,
        out_shape=[jax.ShapeDtypeStruct((n, a, b), F32)] * 4, compiler_params=_cp("parallel", "parallel"),
    )(w, m, v, slots)


WEIGHTS = ["norm_g", "ssm_in_w", "ssm_conv_w", "ssm_conv_b", "ssm_dt_bias", "ssm_A_log", "ssm_D", "ssm_norm_g",
           "ssm_out_w", "cf_pw1_w", "cf_pw1_b", "cf_dw_w", "cf_dw_b", "cf_ln_g", "cf_ln_b", "cf_pw2_w", "cf_pw2_b",
           "xa_mem_g", "xa_q_w", "xa_kv_w", "xa_o_w", "ffn_in_w", "ffn_conv_w", "ffn_conv_b", "ffn_out_w"]
ARGS = ["x", "mem"] + WEIGHTS + ["loss_target"] + ["m_" + n for n in WEIGHTS] + ["v_" + n for n in WEIGHTS]
BIG = {"ssm_in_w": "col", "ssm_out_w": "row", "cf_pw1_w": "col", "cf_pw2_w": "row", "xa_q_w": "row",
       "xa_kv_w": "col", "xa_o_w": "row", "ffn_in_w": "col", "ffn_out_w": "row"}
SMALL = ["norm_g", "ssm_conv_w", "cf_pw1_b", "cf_dw_w", "cf_dw_b", "cf_ln_g", "cf_ln_b", "cf_pw2_b", "ffn_conv_w"]
REP = ["ssm_conv_b", "ssm_dt_bias", "ssm_A_log", "ssm_D", "ssm_norm_g", "xa_mem_g", "ffn_conv_b"]
SMALL_W = 768
REP_W = 5632


def _r8(n):
    return -(-n // 8) * 8


def _stack2d(arrs, wid):
    parts = []
    for a in arrs:
        r, c = a.shape[-2:]
        parts.append(jnp.pad(a, [(0, 0)] * (a.ndim - 2) + [(0, _r8(r) - r), (0, wid - c)]))
    return jnp.concatenate(parts, axis=-2)


def _unstack2d(buf, shapes2d):
    out, o = [], 0
    for r, c in shapes2d:
        out.append(buf[..., o:o + r, :c])
        o += _r8(r)
    return out


def _gathered_to_full(g):
    lead = g.shape[1:-1]
    return jnp.moveaxis(g, 0, -2).reshape(*lead, N_DEV * g.shape[-1])


def _full_to_slots(w):
    lead = w.shape[:-1]
    return jnp.moveaxis(w.reshape(*lead, N_DEV, w.shape[-1] // N_DEV), -2, 0)


def kernel(x, mem, norm_g, ssm_in_w, ssm_conv_w, ssm_conv_b, ssm_dt_bias, ssm_A_log, ssm_D, ssm_norm_g, ssm_out_w, cf_pw1_w, cf_pw1_b, cf_dw_w, cf_dw_b, cf_ln_g, cf_ln_b, cf_pw2_w, cf_pw2_b, xa_mem_g, xa_q_w, xa_kv_w, xa_o_w, ffn_in_w, ffn_conv_w, ffn_conv_b, ffn_out_w, loss_target, m_norm_g, m_ssm_in_w, m_ssm_conv_w, m_ssm_conv_b, m_ssm_dt_bias, m_ssm_A_log, m_ssm_D, m_ssm_norm_g, m_ssm_out_w, m_cf_pw1_w, m_cf_pw1_b, m_cf_dw_w, m_cf_dw_b, m_cf_ln_g, m_cf_ln_b, m_cf_pw2_w, m_cf_pw2_b, m_xa_mem_g, m_xa_q_w, m_xa_kv_w, m_xa_o_w, m_ffn_in_w, m_ffn_conv_w, m_ffn_conv_b, m_ffn_out_w, v_norm_g, v_ssm_in_w, v_ssm_conv_w, v_ssm_conv_b, v_ssm_dt_bias, v_ssm_A_log, v_ssm_D, v_ssm_norm_g, v_ssm_out_w, v_cf_pw1_w, v_cf_pw1_b, v_cf_dw_w, v_cf_dw_b, v_cf_ln_g, v_cf_ln_b, v_cf_pw2_w, v_cf_pw2_b, v_xa_mem_g, v_xa_q_w, v_xa_kv_w, v_xa_o_w, v_ffn_in_w, v_ffn_conv_w, v_ffn_conv_b, v_ffn_out_w):
    return _step(x, mem, norm_g, ssm_in_w, ssm_conv_w, ssm_conv_b, ssm_dt_bias, ssm_A_log, ssm_D, ssm_norm_g, ssm_out_w, cf_pw1_w, cf_pw1_b, cf_dw_w, cf_dw_b, cf_ln_g, cf_ln_b, cf_pw2_w, cf_pw2_b, xa_mem_g, xa_q_w, xa_kv_w, xa_o_w, ffn_in_w, ffn_conv_w, ffn_conv_b, ffn_out_w, loss_target, m_norm_g, m_ssm_in_w, m_ssm_conv_w, m_ssm_conv_b, m_ssm_dt_bias, m_ssm_A_log, m_ssm_D, m_ssm_norm_g, m_ssm_out_w, m_cf_pw1_w, m_cf_pw1_b, m_cf_dw_w, m_cf_dw_b, m_cf_ln_g, m_cf_ln_b, m_cf_pw2_w, m_cf_pw2_b, m_xa_mem_g, m_xa_q_w, m_xa_kv_w, m_xa_o_w, m_ffn_in_w, m_ffn_conv_w, m_ffn_conv_b, m_ffn_out_w, v_norm_g, v_ssm_in_w, v_ssm_conv_w, v_ssm_conv_b, v_ssm_dt_bias, v_ssm_A_log, v_ssm_D, v_ssm_norm_g, v_ssm_out_w, v_cf_pw1_w, v_cf_pw1_b, v_cf_dw_w, v_cf_dw_b, v_cf_ln_g, v_cf_ln_b, v_cf_pw2_w, v_cf_pw2_b, v_xa_mem_g, v_xa_q_w, v_xa_kv_w, v_xa_o_w, v_ffn_in_w, v_ffn_conv_w, v_ffn_conv_b, v_ffn_out_w)


def _step(*args):
    A = dict(zip(ARGS, args, strict=True))
    x, mem, target = A["x"][0], A["mem"][0], A["loss_target"][0]

    big = list(BIG)
    geo = [A[n].shape for n in big]
    kinds = ["row" if BIG[n] == "row" else ("col" if A[n].shape[-1] % LANE == 0 else "lead") for n in big]
    W = {n: A[n] for n in REP}
    for n, kind, g in zip(big, kinds, all_gather_big([A[n].astype(BF16) for n in big], kinds, name="gather_big")):
        W[n] = _gathered_to_full(g) if kind == "lead" else g

    small2d = [(A[n].size // A[n].shape[-1], A[n].shape[-1]) for n in SMALL]
    rep2d = [A[n].shape for n in REP] + [(1, 1)]
    stack_small = lambda pre: _stack2d([A[pre + n].reshape(rc) for n, rc in zip(SMALL, small2d)], SMALL_W)
    stack_rep = lambda pre: _stack2d([A[pre + n] for n in REP] + [jnp.zeros((1, 1), F32)], REP_W)
    small_g = all_gather(stack_small(""), name="gather_small")
    for n, g in zip(SMALL, _unstack2d(small_g, small2d)):
        W[n] = _gathered_to_full(g.reshape(N_DEV, *A[n].shape))

    loss, grad_x, G = local_step(x, mem, target, W)

    gs = [_full_to_slots(G[n]) if kind == "lead" else G[n] for n, kind in zip(big, kinds)]
    got1 = rs_sibling(gs, kinds, geo, name="rs_sibling")
    s1 = [rs_add(g, kind, got, name=f"rs_add_{n}") for n, g, kind, got in zip(big, gs, kinds, got1)]
    got2 = rs_chips(s1, name="rs_chips")

    sh = _stack2d([_full_to_slots(G[n]).reshape(N_DEV, *rc) for n, rc in zip(SMALL, small2d)], SMALL_W)
    rep = _stack2d([G[n] for n in REP] + [loss[:, :1]], REP_W)
    sh_got, rep_got = small_exchange(sh, rep, name="small_exchange")

    res = {}
    for n, slots in zip(big, got2):
        res[n] = adam_slots(A[n], A["m_" + n], A["v_" + n], slots, name=f"adam_{n}")
    for names, shapes2d, stack, slots, tag in ((SMALL, small2d, stack_small, sh_got, "small"),
                                               (REP, rep2d, stack_rep, rep_got, "rep")):
        outs4 = adam_slots(stack("")[None], stack("m_")[None], stack("v_")[None], slots[:, None], name=f"adam_{tag}")
        parts = [_unstack2d(o[0], shapes2d) for o in outs4]
        for k, n in enumerate(names):
            res[n] = tuple(q[k].reshape(A[n].shape) for q in parts)
        if tag == "rep":
            total_loss = parts[0][-1][0, 0]

    outs = [total_loss, grad_x[None]]
    for k in range(4):
        outs += [res[n][k] for n in WEIGHTS]
    return tuple(outs)
```

```python
import jax
import jax.numpy as jnp
from jax import lax
from jax.experimental import pallas as pl
from jax.experimental.pallas import tpu as pltpu

F32 = jnp.float32
BF16 = jnp.bfloat16

D_MODEL = 1024
D_INNER = 2048
N_HEADS = 32
HEAD_DIM = 64
N_GROUPS = 4
D_STATE = 128
CHUNK = 128
CONV_DIM = 3072
SSM_K = 4
CF_K = 31
N_MEM = 256
XA_HEADS = 4
XA_HD = 256
D_FF = 2816
FFN_K = 3
EPS = 1e-6
DEPTH = 4
N_DEV = 8

ADAM_LR = 0.001
ADAM_B1 = 0.9
ADAM_B2 = 0.999
ADAM_EPS = 1e-08
ADAM_WD = 0.01
ADAM_STEP = 10

LANE = 128
VMEM_LIMIT = 56 * 1024 * 1024
NEG = -1e30
MESH = pl.DeviceIdType.MESH


def _cp(*sem):
    return pltpu.CompilerParams(dimension_semantics=sem if sem else None, vmem_limit_bytes=VMEM_LIMIT)


def _tile(n, cap):
    if n <= cap:
        return n
    best = 0
    for t in range(LANE, cap + 1, LANE):
        if n % t == 0:
            best = t
    assert best, (n, cap)
    return best


def _sig(x):
    return 1.0 / (1.0 + jnp.exp(-x))


def _split3(v):
    v0 = v.astype(BF16)
    r1 = v - v0.astype(F32)
    v1 = r1.astype(BF16)
    v2 = (r1 - v1.astype(F32)).astype(BF16)
    return v0, v1, v2


def _dot(a, b, ca=1, cb=0):
    return lax.dot_general(a, b, (((ca,), (cb,)), ((), ())), preferred_element_type=F32)


def _dot3(v, m, ca=1, cb=0):
    v0, v1, v2 = _split3(v)
    return _dot(v0, m, ca, cb) + _dot(v1, m, ca, cb) + _dot(v2, m, ca, cb)


def mm(a, b, *, ta=False, tb=False, bias=None, acc=None, out_dtype=F32, a_idx=None, layer=None, b_k0=0, b_n=None,
       into=None, name):
    if isinstance(b, (list, tuple)):
        b, layer = b[layer], None
    if ta:
        K, M = a.shape[-2:]
    else:
        M, K = a.shape[-2:]
    N = b_n if b_n is not None else (b.shape[-2] if tb else b.shape[-1])
    assert (b.ndim == 3) == (layer is not None) and (a.ndim == 3) == (a_idx is not None)
    tm = _tile(M, 1024)
    tn = _tile(N, 1536)
    tk = _tile(K, 2048)
    nk = K // tk
    assert b_k0 % tk == 0 and b_k0 + K <= (b.shape[-1] if tb else b.shape[-2])
    kb = b_k0 // tk
    has_bias, has_acc = bias is not None, acc is not None
    if into is not None:
        out_dtype = into[0].dtype
        assert into[0].shape[1] == M and into[2] % tn == 0 and into[2] + N <= into[0].shape[2] and not has_acc

    def body(*refs):
        a_ref, b_ref = refs[0], refs[1]
        pos = 2
        bias_ref = acc_ref = None
        if has_bias:
            bias_ref = refs[pos]
            pos += 1
        if has_acc:
            acc_ref = refs[pos]
            pos += 1
        if into is not None:
            pos += 1
        o_ref = refs[pos]
        s_ref = refs[pos + 1] if nk > 1 else None
        p = _dot(a_ref[...].astype(BF16), b_ref[...].astype(BF16), 0 if ta else 1, 1 if tb else 0)

        def extras(v):
            if has_bias:
                v = v + bias_ref[...]
            if has_acc:
                v = v + acc_ref[...]
            return v

        if nk == 1:
            o_ref[...] = extras(p).astype(out_dtype)
        else:
            k = pl.program_id(2)

            @pl.when(k == 0)
            def _():
                s_ref[...] = extras(p)

            @pl.when(k > 0)
            def _():
                s_ref[...] += p

            @pl.when(k == nk - 1)
            def _():
                o_ref[...] = s_ref[...].astype(out_dtype)

    lead_a = () if a_idx is None else (a_idx,)
    lead_b = () if layer is None else (layer,)
    sq = lambda lead: (None,) * len(lead)
    if ta:
        a_spec = pl.BlockSpec((*sq(lead_a), tk, tm), lambda i, j, k: (*lead_a, k, i))
    else:
        a_spec = pl.BlockSpec((*sq(lead_a), tm, tk), lambda i, j, k: (*lead_a, i, k))
    if tb:
        b_spec = pl.BlockSpec((*sq(lead_b), tn, tk), lambda i, j, k: (*lead_b, j, k + kb))
    else:
        b_spec = pl.BlockSpec((*sq(lead_b), tk, tn), lambda i, j, k: (*lead_b, k + kb, j))
    in_specs, args = [a_spec, b_spec], [a, b]
    if has_bias:
        in_specs.append(pl.BlockSpec((1, tn), lambda i, j, k: (0, j)))
        args.append(bias.reshape(1, N).astype(F32))
    if has_acc:
        in_specs.append(pl.BlockSpec((tm, tn), lambda i, j, k: (i, j)))
        args.append(acc)
    if into is None:
        out_spec = pl.BlockSpec((tm, tn), lambda i, j, k: (i, j))
        out_shape = jax.ShapeDtypeStruct((M, N), out_dtype)
        aliases = {}
    else:
        buf, l, col0 = into
        cb = col0 // tn
        in_specs.append(pl.BlockSpec(memory_space=pl.ANY))
        args.append(buf)
        out_spec = pl.BlockSpec((None, tm, tn), lambda i, j, k: (l, i, j + cb))
        out_shape = jax.ShapeDtypeStruct(buf.shape, buf.dtype)
        aliases = {len(args) - 1: 0}
    return pl.pallas_call(
        body, name=name, grid=(M // tm, N // tn, nk),
        in_specs=in_specs, out_specs=out_spec, out_shape=out_shape, input_output_aliases=aliases,
        scratch_shapes=[pltpu.VMEM((tm, tn), F32)] if nk > 1 else [],
        compiler_params=_cp("parallel", "parallel", "arbitrary"),
    )(*args)


def colsum(x, name):
    L, C = x.shape
    tr = _tile(L, 512)
    tc = _tile(C, 1024)

    def body(x_ref, o_ref):
        @pl.when(pl.program_id(1) == 0)
        def _():
            o_ref[...] = jnp.zeros_like(o_ref)

        o_ref[...] += jnp.sum(x_ref[...].astype(F32), axis=0, keepdims=True)

    return pl.pallas_call(
        body, name=name, grid=(C // tc, L // tr),
        in_specs=[pl.BlockSpec((tr, tc), lambda j, i: (i, j))],
        out_specs=pl.BlockSpec((1, tc), lambda j, i: (0, j)),
        out_shape=jax.ShapeDtypeStruct((1, C), F32),
        compiler_params=_cp("parallel", "arbitrary"),
    )(x)


TR = 256


def _row_spec(tr, w):
    return pl.BlockSpec((tr, w), lambda i: (i, 0))


def _vec_spec(w):
    return pl.BlockSpec((1, w), lambda i: (0, 0))


def _rms(v):
    return lax.rsqrt(jnp.mean(v * v, axis=-1, keepdims=True) + EPS)


def norm_fwd(x, g, name):
    L, D = x.shape
    tr = min(TR, L)

    def body(x_ref, g_ref, h_ref):
        xv = x_ref[...]
        h_ref[...] = (xv * _rms(xv) * g_ref[...]).astype(BF16)

    return pl.pallas_call(
        body, name=name, grid=(L // tr,),
        in_specs=[_row_spec(tr, D), _vec_spec(D)], out_specs=_row_spec(tr, D),
        out_shape=jax.ShapeDtypeStruct((L, D), BF16), compiler_params=_cp("parallel"),
    )(x, g.reshape(1, D))


def bnd_fwd(x, f, gpost, gpre, name):
    L, D = x.shape
    tr = min(TR, L)

    def body(x_ref, f_ref, gp_ref, gn_ref, xo_ref, h_ref):
        fv = f_ref[...]
        xn = x_ref[...] + fv * _rms(fv) * gp_ref[...]
        xo_ref[...] = xn
        h_ref[...] = (xn * _rms(xn) * gn_ref[...]).astype(BF16)

    return pl.pallas_call(
        body, name=name, grid=(L // tr,),
        in_specs=[_row_spec(tr, D), _row_spec(tr, D), _vec_spec(D), _vec_spec(D)],
        out_specs=[_row_spec(tr, D), _row_spec(tr, D)],
        out_shape=[jax.ShapeDtypeStruct((L, D), F32), jax.ShapeDtypeStruct((L, D), BF16)],
        compiler_params=_cp("parallel"),
    )(x, f, gpost.reshape(1, D), gpre.reshape(1, D))


def final_fwd(x, f, gpost, target, name):
    L, D = x.shape
    tr = min(TR, L)
    n = L // tr

    def body(x_ref, f_ref, gp_ref, t_ref, dy_ref, loss_ref, acc_ref):
        i = pl.program_id(0)

        @pl.when(i == 0)
        def _():
            acc_ref[...] = jnp.zeros_like(acc_ref)

        fv = f_ref[...]
        e = x_ref[...] + fv * _rms(fv) * gp_ref[...] - t_ref[...]
        dy_ref[...] = e * (1.0 / D)
        acc_ref[...] += jnp.sum(e * e, axis=0, keepdims=True)

        @pl.when(i == n - 1)
        def _():
            loss_ref[...] = jnp.full((1, LANE), 0.5 / D, F32) * jnp.sum(acc_ref[...])

    return pl.pallas_call(
        body, name=name, grid=(n,),
        in_specs=[_row_spec(tr, D), _row_spec(tr, D), _vec_spec(D), _row_spec(tr, D)],
        out_specs=[_row_spec(tr, D), _vec_spec(LANE)],
        out_shape=[jax.ShapeDtypeStruct((L, D), F32), jax.ShapeDtypeStruct((1, LANE), F32)],
        scratch_shapes=[pltpu.VMEM((1, D), F32)],
        compiler_params=_cp("arbitrary"),
    )(x, f, gpost.reshape(1, D), target)


def _rms_bwd(v, g, dy):
    r = _rms(v)
    vn = v * r
    dg = jnp.sum(dy * vn, axis=0, keepdims=True)
    dvn = dy * g
    dv = r * (dvn - vn * jnp.mean(dvn * vn, axis=-1, keepdims=True))
    return dv, dg


def bnd_bwd(dxp, *, pre=None, post=None, name):
    L, D = dxp.shape
    tr = min(TR, L)
    has_pre, has_post = pre is not None, post is not None

    def body(*refs):
        pos = 0
        dxp_ref = refs[pos]; pos += 1
        if has_pre:
            x_ref, gpre_ref, dh_ref = refs[pos:pos + 3]; pos += 3
        if has_post:
            f_ref, gpost_ref = refs[pos:pos + 2]; pos += 2
        if has_pre:
            dx_ref, dgpre_ref = refs[pos:pos + 2]; pos += 2
        if has_post:
            df_ref, dgpost_ref = refs[pos:pos + 2]; pos += 2
        i = pl.program_id(0)
        dx = dxp_ref[...]
        if has_pre:
            d, dg = _rms_bwd(x_ref[...], gpre_ref[...], dh_ref[...])
            dx = dx + d
            dx_ref[...] = dx

            @pl.when(i == 0)
            def _():
                dgpre_ref[...] = jnp.zeros_like(dgpre_ref)

            dgpre_ref[...] += dg
        if has_post:
            d, dg = _rms_bwd(f_ref[...], gpost_ref[...], dx)
            df_ref[...] = d.astype(BF16)

            @pl.when(i == 0)
            def _():
                dgpost_ref[...] = jnp.zeros_like(dgpost_ref)

            dgpost_ref[...] += dg

    in_specs, args = [_row_spec(tr, D)], [dxp]
    out_specs, out_shape, names = [], [], []
    if has_pre:
        x, gpre, dh = pre
        in_specs += [_row_spec(tr, D), _vec_spec(D), _row_spec(tr, D)]
        args += [x, gpre.reshape(1, D), dh]
        out_specs += [_row_spec(tr, D), _vec_spec(D)]
        out_shape += [jax.ShapeDtypeStruct((L, D), F32), jax.ShapeDtypeStruct((1, D), F32)]
        names += ["dx", "dgpre"]
    if has_post:
        f, gpost = post
        in_specs += [_row_spec(tr, D), _vec_spec(D)]
        args += [f, gpost.reshape(1, D)]
        out_specs += [_row_spec(tr, D), _vec_spec(D)]
        out_shape += [jax.ShapeDtypeStruct((L, D), BF16), jax.ShapeDtypeStruct((1, D), F32)]
        names += ["df", "dgpost"]
    outs = pl.pallas_call(
        body, name=name, grid=(L // tr,), in_specs=in_specs, out_specs=out_specs, out_shape=out_shape,
        compiler_params=_cp("arbitrary"),
    )(*args)
    return dict(zip(names, outs))


def norm_dg(x, dy, name):
    L, D = x.shape
    tr = min(TR, L)

    def body(x_ref, dy_ref, o_ref):
        @pl.when(pl.program_id(0) == 0)
        def _():
            o_ref[...] = jnp.zeros_like(o_ref)

        xv = x_ref[...]
        o_ref[...] += jnp.sum(dy_ref[...] * xv * _rms(xv), axis=0, keepdims=True)

    return pl.pallas_call(
        body, name=name, grid=(L // tr,),
        in_specs=[_row_spec(tr, D), _row_spec(tr, D)], out_specs=_vec_spec(D),
        out_shape=jax.ShapeDtypeStruct((1, D), F32), compiler_params=_cp("arbitrary"),
    )(x, dy)


HALO = 32


def _prev_halo_spec(tr, tc, col):
    per = tr // HALO
    return pl.BlockSpec((HALO, tc), lambda *g: (jnp.maximum(g[-1] * per - 1, 0), col(*g)))


def _fill_prev(scr, halo_val, blk_val, i, tr):
    scr[pl.ds(0, HALO), :] = jnp.where(i == 0, 0.0, halo_val)
    scr[pl.ds(HALO, tr), :] = blk_val


def _conv(scr, w_ref, K, tr):
    acc = None
    for k in range(K):
        term = scr[pl.ds(HALO - (K - 1) + k, tr), :] * w_ref[k:k + 1, :]
        acc = term if acc is None else acc + term
    return acc


def _conv_dw(scr, d, o_ref, K, tr):
    for k in range(K):
        o_ref[k:k + 1, :] += jnp.sum(d * scr[pl.ds(HALO - (K - 1) + k, tr), :], axis=0, keepdims=True)


def ssm_conv_fwd(zx, w, b, *, col0, ncols, wcol0, out_dtype, name):
    L = zx.shape[0]
    tr = min(TR, L)
    tc = 1024
    cb, wb = col0 // tc, wcol0 // tc

    def body(x_ref, h_ref, w_ref, b_ref, o_ref, p_ref, scr):
        i = pl.program_id(1)
        _fill_prev(scr, h_ref[...].astype(F32), x_ref[...].astype(F32), i, tr)
        pre = _conv(scr, w_ref, SSM_K, tr) + b_ref[...]
        p_ref[...] = pre.astype(BF16)
        o_ref[...] = (pre * _sig(pre)).astype(out_dtype)

    out = pl.BlockSpec((tr, tc), lambda j, i: (i, j))
    return pl.pallas_call(
        body, name=name, grid=(ncols // tc, L // tr),
        in_specs=[pl.BlockSpec((tr, tc), lambda j, i: (i, j + cb)),
                  _prev_halo_spec(tr, tc, lambda j, i: j + cb),
                  pl.BlockSpec((SSM_K, tc), lambda j, i: (0, j + wb)),
                  pl.BlockSpec((1, tc), lambda j, i: (0, j + wb))],
        out_specs=[out, out],
        out_shape=[jax.ShapeDtypeStruct((L, ncols), out_dtype), jax.ShapeDtypeStruct((L, ncols), BF16)],
        scratch_shapes=[pltpu.VMEM((HALO + tr, tc), F32)],
        compiler_params=_cp("parallel", "parallel"),
    )(zx, zx, w, b)


def ssm_conv_bwd(zx, pre, d, w, *, col0, dcol0, ncols, name):
    L = zx.shape[0]
    tr = min(TR, L)
    tc = 1024
    cb, db_ = col0 // tc, dcol0 // tc
    n = L // tr
    per = tr // HALO
    last = L // HALO - 1

    def body(x_ref, h_ref, p_ref, np_ref, d_ref, nd_ref, w_ref, dx_ref, dw_ref, db_ref, scr, sd):
        i = pl.program_id(1)
        _fill_prev(scr, h_ref[...].astype(F32), x_ref[...].astype(F32), i, tr)

        def dpre(p, dv):
            s = _sig(p)
            return dv * s * (1.0 + p * (1.0 - s))

        dp = dpre(p_ref[...].astype(F32), d_ref[...])
        sd[pl.ds(0, tr), :] = dp
        sd[pl.ds(tr, HALO), :] = jnp.where(i == n - 1, 0.0, dpre(np_ref[...].astype(F32), nd_ref[...]))
        acc = None
        for k in range(SSM_K):
            term = sd[pl.ds(SSM_K - 1 - k, tr), :] * w_ref[k:k + 1, :]
            acc = term if acc is None else acc + term
        dx_ref[...] = acc.astype(BF16)

        @pl.when(i == 0)
        def _():
            dw_ref[...] = jnp.zeros_like(dw_ref)
            db_ref[...] = jnp.zeros_like(db_ref)

        _conv_dw(scr, dp, dw_ref, SSM_K, tr)
        db_ref[...] += jnp.sum(dp, axis=0, keepdims=True)

    nxt = lambda i: jnp.minimum((i + 1) * per, last)
    return pl.pallas_call(
        body, name=name, grid=(ncols // tc, n),
        in_specs=[pl.BlockSpec((tr, tc), lambda j, i: (i, j + cb)),
                  _prev_halo_spec(tr, tc, lambda j, i: j + cb),
                  pl.BlockSpec((tr, tc), lambda j, i: (i, j)),
                  pl.BlockSpec((HALO, tc), lambda j, i: (nxt(i), j)),
                  pl.BlockSpec((tr, tc), lambda j, i: (i, j + db_)),
                  pl.BlockSpec((HALO, tc), lambda j, i: (nxt(i), j + db_)),
                  pl.BlockSpec((SSM_K, tc), lambda j, i: (0, j + db_))],
        out_specs=[pl.BlockSpec((tr, tc), lambda j, i: (i, j)),
                   pl.BlockSpec((SSM_K, tc), lambda j, i: (0, j)),
                   pl.BlockSpec((1, tc), lambda j, i: (0, j))],
        out_shape=[jax.ShapeDtypeStruct((L, ncols), BF16), jax.ShapeDtypeStruct((SSM_K, ncols), F32),
                   jax.ShapeDtypeStruct((1, ncols), F32)],
        scratch_shapes=[pltpu.VMEM((HALO + tr, tc), F32), pltpu.VMEM((tr + HALO, tc), F32)],
        compiler_params=_cp("parallel", "arbitrary"),
    )(zx, zx, pre, pre, d, d, w)


FFN_TC = 1408


def ffn_act_fwd(u, w, b, name):
    L = u.shape[0]
    tr = min(TR, L)
    tc = FFN_TC
    nb = D_FF // tc

    def body(g_ref, hg_ref, v_ref, hv_ref, wg_ref, wv_ref, bg_ref, bv_ref, o_ref, c_ref, sg, sv):
        i = pl.program_id(1)
        _fill_prev(sg, hg_ref[...].astype(F32), g_ref[...].astype(F32), i, tr)
        _fill_prev(sv, hv_ref[...].astype(F32), v_ref[...].astype(F32), i, tr)
        ug = _conv(sg, wg_ref, FFN_K, tr) + bg_ref[...]
        uv = _conv(sv, wv_ref, FFN_K, tr) + bv_ref[...]
        c_ref[0] = ug.astype(BF16)
        c_ref[1] = uv.astype(BF16)
        o_ref[...] = (ug * _sig(ug) * uv).astype(BF16)

    blk = lambda off: pl.BlockSpec((tr, tc), lambda j, i: (i, j + off))
    wsp = lambda off: pl.BlockSpec((FFN_K, tc), lambda j, i: (0, j + off))
    bsp = lambda off: pl.BlockSpec((1, tc), lambda j, i: (0, j + off))
    return pl.pallas_call(
        body, name=name, grid=(nb, L // tr),
        in_specs=[blk(0), _prev_halo_spec(tr, tc, lambda j, i: j),
                  blk(nb), _prev_halo_spec(tr, tc, lambda j, i: j + nb),
                  wsp(0), wsp(nb), bsp(0), bsp(nb)],
        out_specs=[pl.BlockSpec((tr, tc), lambda j, i: (i, j)), pl.BlockSpec((2, tr, tc), lambda j, i: (0, i, j))],
        out_shape=[jax.ShapeDtypeStruct((L, D_FF), BF16), jax.ShapeDtypeStruct((2, L, D_FF), BF16)],
        scratch_shapes=[pltpu.VMEM((HALO + tr, tc), F32), pltpu.VMEM((HALO + tr, tc), F32)],
        compiler_params=_cp("parallel", "parallel"),
    )(u, u, u, u, w, w, b, b)


def ffn_act_bwd(u, c, dact, w, name):
    L = u.shape[0]
    tr = min(TR, L)
    tc = FFN_TC
    nb = D_FF // tc
    n = L // tr
    per = tr // HALO
    last = L // HALO - 1

    def body(g_ref, hg_ref, v_ref, hv_ref, c_ref, nc_ref, da_ref, nda_ref, wg_ref, wv_ref,
             du_ref, dw_ref, db_ref, sg, sv, dg_s, dv_s):
        i = pl.program_id(1)
        _fill_prev(sg, hg_ref[...].astype(F32), g_ref[...].astype(F32), i, tr)
        _fill_prev(sv, hv_ref[...].astype(F32), v_ref[...].astype(F32), i, tr)

        def grads(cg, cv, da):
            s = _sig(cg)
            return da * cv * s * (1.0 + cg * (1.0 - s)), da * cg * s

        dg, dv = grads(c_ref[0].astype(F32), c_ref[1].astype(F32), da_ref[...].astype(F32))
        ndg, ndv = grads(nc_ref[0].astype(F32), nc_ref[1].astype(F32), nda_ref[...].astype(F32))
        at_end = i == n - 1
        for half, (scr, d, nd, x_scr, w_ref) in enumerate(((dg_s, dg, ndg, sg, wg_ref), (dv_s, dv, ndv, sv, wv_ref))):
            scr[pl.ds(0, tr), :] = d
            scr[pl.ds(tr, HALO), :] = jnp.where(at_end, 0.0, nd)
            acc = None
            for k in range(FFN_K):
                term = scr[pl.ds(FFN_K - 1 - k, tr), :] * w_ref[k:k + 1, :]
                acc = term if acc is None else acc + term
            du_ref[half] = acc.astype(BF16)

            @pl.when(i == 0)
            def _():
                dw_ref[half] = jnp.zeros((FFN_K, tc), F32)
                db_ref[half] = jnp.zeros((1, tc), F32)

            for k in range(FFN_K):
                dw_ref[half, k:k + 1, :] += jnp.sum(d * x_scr[pl.ds(HALO - (FFN_K - 1) + k, tr), :], axis=0, keepdims=True)
            db_ref[half] += jnp.sum(d, axis=0, keepdims=True)

    blk = lambda off: pl.BlockSpec((tr, tc), lambda j, i: (i, j + off))
    wsp = lambda off: pl.BlockSpec((FFN_K, tc), lambda j, i: (0, j + off))
    nxt = lambda i: jnp.minimum((i + 1) * per, last)
    return pl.pallas_call(
        body, name=name, grid=(nb, n),
        in_specs=[blk(0), _prev_halo_spec(tr, tc, lambda j, i: j),
                  blk(nb), _prev_halo_spec(tr, tc, lambda j, i: j + nb),
                  pl.BlockSpec((2, tr, tc), lambda j, i: (0, i, j)),
                  pl.BlockSpec((2, HALO, tc), lambda j, i: (0, nxt(i), j)),
                  pl.BlockSpec((tr, tc), lambda j, i: (i, j)),
                  pl.BlockSpec((HALO, tc), lambda j, i: (nxt(i), j)),
                  wsp(0), wsp(nb)],
        out_specs=[pl.BlockSpec((2, tr, tc), lambda j, i: (0, i, j)),
                   pl.BlockSpec((2, FFN_K, tc), lambda j, i: (0, 0, j)),
                   pl.BlockSpec((2, 1, tc), lambda j, i: (0, 0, j))],
        out_shape=[jax.ShapeDtypeStruct((2, L, D_FF), BF16), jax.ShapeDtypeStruct((2, FFN_K, D_FF), F32),
                   jax.ShapeDtypeStruct((2, 1, D_FF), F32)],
        scratch_shapes=[pltpu.VMEM((HALO + tr, tc), F32), pltpu.VMEM((HALO + tr, tc), F32),
                        pltpu.VMEM((tr + HALO, tc), F32), pltpu.VMEM((tr + HALO, tc), F32)],
        compiler_params=_cp("parallel", "arbitrary"),
    )(u, u, u, u, c, c, dact, dact, w, w)


def _ln_stats(c):
    mu = jnp.mean(c, axis=-1, keepdims=True)
    cc = c - mu
    rstd = lax.rsqrt(jnp.mean(cc * cc, axis=-1, keepdims=True) + EPS)
    return cc * rstd, rstd


def cf_fwd(u, dw_w, dw_b, ln_g, ln_b, name):
    L = u.shape[0]
    D = D_MODEL
    tr = min(TR, L)

    def body(a_ref, ha_ref, g_ref, hg_ref, w_ref, b_ref, lg_ref, lb_ref, c_ref, s_ref, scr):
        i = pl.program_id(0)
        glu_h = ha_ref[...].astype(F32) * _sig(hg_ref[...].astype(F32))
        glu = a_ref[...].astype(F32) * _sig(g_ref[...].astype(F32))
        _fill_prev(scr, glu_h, glu, i, tr)
        c = _conv(scr, w_ref, CF_K, tr) + b_ref[...]
        c_ref[...] = c
        xhat, _ = _ln_stats(c)
        ln = xhat * lg_ref[...] + lb_ref[...]
        s_ref[...] = (ln * _sig(ln)).astype(BF16)

    per = tr // HALO
    halo = lambda col: pl.BlockSpec((HALO, D), lambda i: (jnp.maximum(i * per - 1, 0), col))
    return pl.pallas_call(
        body, name=name, grid=(L // tr,),
        in_specs=[pl.BlockSpec((tr, D), lambda i: (i, 0)), halo(0),
                  pl.BlockSpec((tr, D), lambda i: (i, 1)), halo(1),
                  pl.BlockSpec((CF_K, D), lambda i: (0, 0)), _vec_spec(D), _vec_spec(D), _vec_spec(D)],
        out_specs=[_row_spec(tr, D), _row_spec(tr, D)],
        out_shape=[jax.ShapeDtypeStruct((L, D), F32), jax.ShapeDtypeStruct((L, D), BF16)],
        scratch_shapes=[pltpu.VMEM((HALO + tr, D), F32)],
        compiler_params=_cp("parallel"),
    )(u, u, u, u, dw_w, dw_b, ln_g, ln_b)


def cf_bwd_ln(c, ds, ln_g, ln_b, name):
    L, D = c.shape
    tr = min(TR, L)

    def body(c_ref, ds_ref, lg_ref, lb_ref, dc_ref, dg_ref, db_ref):
        xhat, rstd = _ln_stats(c_ref[...])
        ln = xhat * lg_ref[...] + lb_ref[...]
        sg = _sig(ln)
        dln = ds_ref[...].astype(F32) * sg * (1.0 + ln * (1.0 - sg))

        @pl.when(pl.program_id(0) == 0)
        def _():
            dg_ref[...] = jnp.zeros_like(dg_ref)
            db_ref[...] = jnp.zeros_like(db_ref)

        dg_ref[...] += jnp.sum(dln * xhat, axis=0, keepdims=True)
        db_ref[...] += jnp.sum(dln, axis=0, keepdims=True)
        dxh = dln * lg_ref[...]
        dc_ref[...] = rstd * (dxh - jnp.mean(dxh, axis=-1, keepdims=True)
                              - xhat * jnp.mean(dxh * xhat, axis=-1, keepdims=True))

    return pl.pallas_call(
        body, name=name, grid=(L // tr,),
        in_specs=[_row_spec(tr, D), _row_spec(tr, D), _vec_spec(D), _vec_spec(D)],
        out_specs=[_row_spec(tr, D), _vec_spec(D), _vec_spec(D)],
        out_shape=[jax.ShapeDtypeStruct((L, D), F32), jax.ShapeDtypeStruct((1, D), F32),
                   jax.ShapeDtypeStruct((1, D), F32)],
        compiler_params=_cp("arbitrary"),
    )(c, ds, ln_g, ln_b)


def cf_bwd_conv(u, dc, dw_w, name):
    L = u.shape[0]
    D = D_MODEL
    tr = min(TR, L)
    n = L // tr

    def body(a_ref, ha_ref, g_ref, hg_ref, dc_ref, nx_ref, w_ref, du_ref, dw_ref, db_ref, sx, sd):
        i = pl.program_id(0)
        a = a_ref[...].astype(F32)
        sg = _sig(g_ref[...].astype(F32))
        _fill_prev(sx, ha_ref[...].astype(F32) * _sig(hg_ref[...].astype(F32)), a * sg, i, tr)
        dcv = dc_ref[...]
        sd[pl.ds(0, tr), :] = dcv
        sd[pl.ds(tr, HALO), :] = jnp.where(i == n - 1, 0.0, nx_ref[...])
        dglu = None
        for k in range(CF_K):
            term = sd[pl.ds(CF_K - 1 - k, tr), :] * w_ref[k:k + 1, :]
            dglu = term if dglu is None else dglu + term
        du_ref[:, 0:D] = (dglu * sg).astype(BF16)
        du_ref[:, D:2 * D] = (dglu * a * sg * (1.0 - sg)).astype(BF16)

        @pl.when(i == 0)
        def _():
            dw_ref[...] = jnp.zeros_like(dw_ref)
            db_ref[...] = jnp.zeros_like(db_ref)

        _conv_dw(sx, dcv, dw_ref, CF_K, tr)
        db_ref[...] += jnp.sum(dcv, axis=0, keepdims=True)

    per = tr // HALO
    last = L // HALO - 1
    halo = lambda col: pl.BlockSpec((HALO, D), lambda i: (jnp.maximum(i * per - 1, 0), col))
    return pl.pallas_call(
        body, name=name, grid=(n,),
        in_specs=[pl.BlockSpec((tr, D), lambda i: (i, 0)), halo(0),
                  pl.BlockSpec((tr, D), lambda i: (i, 1)), halo(1),
                  _row_spec(tr, D),
                  pl.BlockSpec((HALO, D), lambda i: (jnp.minimum((i + 1) * per, last), 0)),
                  pl.BlockSpec((CF_K, D), lambda i: (0, 0))],
        out_specs=[pl.BlockSpec((tr, 2 * D), lambda i: (i, 0)),
                   pl.BlockSpec((CF_K, D), lambda i: (0, 0)), _vec_spec(D)],
        out_shape=[jax.ShapeDtypeStruct((L, 2 * D), BF16), jax.ShapeDtypeStruct((CF_K, D), F32),
                   jax.ShapeDtypeStruct((1, D), F32)],
        scratch_shapes=[pltpu.VMEM((HALO + tr, D), F32), pltpu.VMEM((tr + HALO, D), F32)],
        compiler_params=_cp("arbitrary"),
    )(u, u, u, u, dc, dc, dw_w)


XA_TR = 512
XA_SCALE = XA_HD ** -0.5


def _xa_probs(qh, kh):
    s = _dot(qh, kh, 1, 1) * XA_SCALE
    p = jnp.exp(s - jnp.max(s, axis=-1, keepdims=True))
    return p / jnp.sum(p, axis=-1, keepdims=True)


def attn_fwd(q, kv, name):
    L, D = q.shape
    tr = min(XA_TR, L)

    def body(q_ref, kv_ref, o_ref):
        for hd in range(XA_HEADS):
            c = slice(hd * XA_HD, (hd + 1) * XA_HD)
            p = _xa_probs(q_ref[:, c], kv_ref[:, c])
            vh = kv_ref[:, D + hd * XA_HD:D + (hd + 1) * XA_HD]
            o_ref[:, c] = _dot(p.astype(BF16), vh).astype(BF16)

    return pl.pallas_call(
        body, name=name, grid=(L // tr,),
        in_specs=[_row_spec(tr, D), pl.BlockSpec((N_MEM, 2 * D), lambda i: (0, 0))],
        out_specs=_row_spec(tr, D), out_shape=jax.ShapeDtypeStruct((L, D), BF16),
        compiler_params=_cp("parallel"),
    )(q, kv)


def attn_bwd(q, kv, do, name):
    L, D = q.shape
    tr = min(XA_TR, L)

    def body(q_ref, kv_ref, do_ref, dq_ref, dkv_ref):
        @pl.when(pl.program_id(0) == 0)
        def _():
            dkv_ref[...] = jnp.zeros_like(dkv_ref)

        for hd in range(XA_HEADS):
            c = slice(hd * XA_HD, (hd + 1) * XA_HD)
            cv = slice(D + hd * XA_HD, D + (hd + 1) * XA_HD)
            qh, kh, vh, doh = q_ref[:, c], kv_ref[:, c], kv_ref[:, cv], do_ref[:, c]
            p = _xa_probs(qh, kh)
            dp = _dot(doh, vh, 1, 1)
            dkv_ref[:, cv] += _dot(p.astype(BF16), doh, 0, 0)
            ds = (p * (dp - jnp.sum(dp * p, axis=-1, keepdims=True)) * XA_SCALE).astype(BF16)
            dq_ref[:, c] = _dot(ds, kh).astype(BF16)
            dkv_ref[:, c] += _dot(ds, qh, 0, 0)

    return pl.pallas_call(
        body, name=name, grid=(L // tr,),
        in_specs=[_row_spec(tr, D), pl.BlockSpec((N_MEM, 2 * D), lambda i: (0, 0)), _row_spec(tr, D)],
        out_specs=[_row_spec(tr, D), pl.BlockSpec((N_MEM, 2 * D), lambda i: (0, 0))],
        out_shape=[jax.ShapeDtypeStruct((L, D), BF16), jax.ShapeDtypeStruct((N_MEM, 2 * D), F32)],
        compiler_params=_cp("arbitrary"),
    )(q, kv, do)


N_PAIRS = N_HEADS // 2
PAIRS_PER_GROUP = N_PAIRS // N_GROUPS
GN = N_GROUPS * D_STATE


def _softplus(x):
    t = jnp.exp(-jnp.abs(x))
    return jnp.maximum(x, 0.0) + jnp.where(t < 1e-4, t * (1.0 - 0.5 * t), jnp.log(1.0 + t))


def _dot3b(m, v, ca=1, cb=0):
    v0, v1, v2 = _split3(v)
    return _dot(m, v0, ca, cb) + _dot(m, v1, ca, cb) + _dot(m, v2, ca, cb)


def ssd_consts():
    h = lax.broadcasted_iota(jnp.int32, (LANE, D_INNER), 0)
    c = lax.broadcasted_iota(jnp.int32, (LANE, D_INNER), 1)
    expand = (c // HEAD_DIM == h).astype(BF16)
    r = lax.broadcasted_iota(jnp.int32, (CHUNK, CHUNK), 0)
    k = lax.broadcasted_iota(jnp.int32, (CHUNK, CHUNK), 1)
    tri = (k <= r).astype(BF16)
    return expand, tri


def _ssd_common(dtr_ref, prm_ref, e_ref, tri_ref):
    lane = lax.broadcasted_iota(jnp.int32, (CHUNK, LANE), 1)
    valid = lane < N_HEADS
    A = -jnp.exp(prm_ref[1:2, :])
    pre = dtr_ref[...] + prm_ref[0:1, :]
    dt = jnp.where(valid, _softplus(pre), 0.0)
    cs = _dot3b(tri_ref[...], dt * A)
    E = e_ref[...]
    dt_x = _dot3(dt, E)
    cs_x = _dot3(cs, E)
    csl_x = cs_x[CHUNK - 1:CHUNK, :]
    return dict(valid=valid, A=A, pre=pre, dt=dt, cs=cs, csT=cs.T, dt_x=dt_x, ecs_x=jnp.exp(cs_x),
                dend_x=jnp.exp(csl_x - cs_x), cd_x=jnp.exp(csl_x), D_x=_dot3(prm_ref[...], E)[2:3, :])


def ssd_fwd(xs, bc, dtr, zx, prm, ng, name):
    L = xs.shape[0]
    nc = L // CHUNK
    expand, tri = ssd_consts()

    def body(xs_ref, bc_ref, dtr_ref, z_ref, prm_ref, ng_ref, e_ref, tri_ref, y_ref, yn_ref, st_ref, state):
        @pl.when(pl.program_id(0) == 0)
        def _():
            state[...] = jnp.zeros_like(state)

        q = _ssd_common(dtr_ref, prm_ref, e_ref, tri_ref)
        cs, csT = q["cs"], q["csT"]
        xs_v = xs_ref[...]
        X = xs_v * q["dt_x"]
        Xb = X.astype(BF16)
        Xd = (X * q["dend_x"]).astype(BF16)
        ii = lax.broadcasted_iota(jnp.int32, (CHUNK, CHUNK), 0)
        jj = lax.broadcasted_iota(jnp.int32, (CHUNK, CHUNK), 1)
        tril = jj <= ii
        first = jj < HEAD_DIM
        for g in range(N_GROUPS):
            Bg = bc_ref[:, g * D_STATE:(g + 1) * D_STATE]
            Cg = bc_ref[:, GN + g * D_STATE:GN + (g + 1) * D_STATE]
            S = _dot(Cg, Bg, 1, 1)
            for pr in range(PAIRS_PER_GROUP):
                pair = g * PAIRS_PER_GROUP + pr
                cols = slice(pair * LANE, (pair + 1) * LANE)
                Xp = Xb[:, cols]
                ys = []
                for h in (2 * pair, 2 * pair + 1):
                    seg = cs[:, h:h + 1] - csT[h:h + 1, :]
                    M = (S * jnp.exp(jnp.where(tril, seg, NEG))).astype(BF16)
                    ys.append(_dot(M, Xp))
                prevT = state[pair]
                st_ref[0, pair] = prevT
                yoff = _dot(Cg, prevT.astype(BF16)) * q["ecs_x"][:, cols]
                y_ref[:, cols] = jnp.where(first, ys[0], ys[1]) + yoff + xs_v[:, cols] * q["D_x"][:, cols]
                state[pair] = prevT * q["cd_x"][:, cols] + _dot(Bg, Xd[:, cols], 0, 0)
        z = z_ref[...].astype(F32)
        gt = y_ref[...] * z * _sig(z)
        yn_ref[...] = (gt * _rms(gt) * ng_ref[...]).astype(BF16)

    row = lambda w: pl.BlockSpec((CHUNK, w), lambda c: (c, 0))
    const = lambda a: pl.BlockSpec(a.shape, lambda c: (0,) * a.ndim)
    return pl.pallas_call(
        body, name=name, grid=(nc,),
        in_specs=[row(D_INNER), row(2 * GN), row(LANE), row(D_INNER), const(prm), const(ng), const(expand), const(tri)],
        out_specs=[row(D_INNER), row(D_INNER), pl.BlockSpec((1, N_PAIRS, D_STATE, LANE), lambda c: (c, 0, 0, 0))],
        out_shape=[jax.ShapeDtypeStruct((L, D_INNER), F32), jax.ShapeDtypeStruct((L, D_INNER), BF16),
                   jax.ShapeDtypeStruct((nc, N_PAIRS, D_STATE, LANE), F32)],
        scratch_shapes=[pltpu.VMEM((N_PAIRS, D_STATE, LANE), F32)],
        compiler_params=_cp("arbitrary"),
    )(xs, bc, dtr, zx, prm, ng, expand, tri)


def ssd_bwd(dyn, y, zx, xs, bc, dtr, st, prm, ng, name):
    L = xs.shape[0]
    nc = L // CHUNK
    expand, tri = ssd_consts()

    def body(dyn_ref, y_ref, z_ref, xs_ref, bc_ref, dtr_ref, st_ref, prm_ref, ng_ref, e_ref, tri_ref,
             dxbc_ref, dz_ref, ddtr_ref, dng_ref, dprm_ref, dstate, g_cs, g_q, dX, g_row):
        step = pl.program_id(0)

        @pl.when(step == 0)
        def _():
            dstate[...] = jnp.zeros_like(dstate)
            dng_ref[...] = jnp.zeros_like(dng_ref)
            dprm_ref[...] = jnp.zeros_like(dprm_ref)
            g_row[...] = jnp.zeros_like(g_row)

        q = _ssd_common(dtr_ref, prm_ref, e_ref, tri_ref)
        cs, csT, E = q["cs"], q["csT"], e_ref[...]
        xs_v = xs_ref[...]
        X = xs_v * q["dt_x"]
        Xb = X.astype(BF16)
        Xd_f = X * q["dend_x"]
        Xd = Xd_f.astype(BF16)

        yv = y_ref[...]
        z = z_ref[...].astype(F32)
        sz = _sig(z)
        silu = z * sz
        gt = yv * silu
        r = _rms(gt)
        gn = gt * r
        dyn_v = dyn_ref[...]
        dng_ref[...] += jnp.sum(dyn_v * gn, axis=0, keepdims=True)
        dgn = dyn_v * ng_ref[...]
        dgt = r * (dgn - gn * jnp.mean(dgn * gn, axis=-1, keepdims=True))
        dY = dgt * silu
        dz_ref[...] = (dgt * yv * sz * (1.0 + z * (1.0 - sz))).astype(BF16)
        dYb = dY.astype(BF16)
        g_row[1:2, :] += jnp.sum(dY * xs_v, axis=0, keepdims=True)

        ii = lax.broadcasted_iota(jnp.int32, (CHUNK, CHUNK), 0)
        jj = lax.broadcasted_iota(jnp.int32, (CHUNK, CHUNK), 1)
        tril = jj <= ii
        triu = jj >= ii
        first = jj < HEAD_DIM
        lane_row = lax.broadcasted_iota(jnp.int32, (1, LANE), 1)
        sub_col = lax.broadcasted_iota(jnp.int32, (CHUNK, 1), 0)
        dcs_col = jnp.zeros((CHUNK, LANE), F32)
        dcs_rowT = jnp.zeros((LANE, CHUNK), F32)
        for g in range(N_GROUPS):
            Bg = bc_ref[:, g * D_STATE:(g + 1) * D_STATE]
            Cg = bc_ref[:, GN + g * D_STATE:GN + (g + 1) * D_STATE]
            S = _dot(Cg, Bg, 1, 1)
            ST = _dot(Bg, Cg, 1, 1)
            dS = jnp.zeros((CHUNK, CHUNK), F32)
            dCg = jnp.zeros((CHUNK, D_STATE), F32)
            dBg = jnp.zeros((CHUNK, D_STATE), F32)
            for pr in range(PAIRS_PER_GROUP):
                pair = g * PAIRS_PER_GROUP + pr
                cols = slice(pair * LANE, (pair + 1) * LANE)
                Xp = Xb[:, cols]
                dYp_f = dY[:, cols]
                dYp = dYb[:, cols]
                prevT = st_ref[0, pair]
                prevTb = prevT.astype(BF16)
                dst = dstate[pair]
                dstb = dst.astype(BF16)
                ecs_p = q["ecs_x"][:, cols]
                g_cs[:, cols] = dYp_f * (_dot(Cg, prevTb) * ecs_p)
                dWb = (dYp_f * ecs_p).astype(BF16)
                dprev = dst * q["cd_x"][:, cols] + _dot(Cg, dWb, 0, 0)
                dCg = dCg + _dot(dWb, prevTb, 1, 1)
                g_row[0:1, cols] = jnp.sum(dst * prevT, axis=0, keepdims=True)
                dXp = None
                for hh, h in enumerate((2 * pair, 2 * pair + 1)):
                    mine = first if hh == 0 else jnp.logical_not(first)
                    seg = cs[:, h:h + 1] - csT[h:h + 1, :]
                    lam = jnp.exp(jnp.where(tril, seg, NEG))
                    dM = _dot(jnp.where(mine, dYp, jnp.zeros_like(dYp)), Xp, 1, 1)
                    dS = dS + dM * lam
                    Gm = dM * (S * lam)
                    dcs_col = dcs_col + jnp.sum(Gm, axis=1, keepdims=True) * (lane_row == h).astype(F32)
                    dcs_rowT = dcs_rowT + (sub_col == h).astype(F32) * jnp.sum(Gm, axis=0, keepdims=True)
                    MT = (ST * jnp.exp(jnp.where(triu, -seg, NEG))).astype(BF16)
                    t = _dot(MT, dYp)
                    dXp = t if dXp is None else jnp.where(first, dXp, t)
                dXd = _dot(Bg, dstb)
                dBg = dBg + _dot(Xd[:, cols], dstb, 1, 1)
                g_q[:, cols] = dXd * Xd_f[:, cols]
                dX[:, cols] = dXp + dXd * q["dend_x"][:, cols]
                dstate[pair] = dprev
            dSb = dS.astype(BF16)
            dxbc_ref[:, D_INNER + g * D_STATE:D_INNER + (g + 1) * D_STATE] = dBg + _dot(dSb, Cg, 0, 0)
            dxbc_ref[:, D_INNER + GN + g * D_STATE:D_INNER + GN + (g + 1) * D_STATE] = dCg + _dot(dSb, Bg)
        dXv = dX[...]
        dxbc_ref[:, 0:D_INNER] = q["D_x"] * dY + dXv * q["dt_x"]
        r_dt = _dot3(dXv * xs_v, E, 1, 1)
        r_cs = _dot3(g_cs[...], E, 1, 1)
        r_q = _dot3(g_q[...], E, 1, 1)
        r_row = _dot3(g_row[...], E, 1, 1)
        cd = jnp.exp(cs[CHUNK - 1:CHUNK, :])
        dcs_last = jnp.sum(r_q, axis=0, keepdims=True) + r_row[0:1, :] * cd
        dcs = r_cs - r_q + dcs_col - dcs_rowT.T + jnp.where(sub_col == CHUNK - 1, dcs_last, 0.0)
        da = _dot3b(tri_ref[...], dcs, 0, 0)
        dpre = jnp.where(q["valid"], (r_dt + da * q["A"]) * _sig(q["pre"]), 0.0)
        ddtr_ref[...] = dpre
        dprm_ref[0:1, :] += jnp.sum(dpre, axis=0, keepdims=True)
        dprm_ref[1:2, :] += jnp.sum(da * q["dt"], axis=0, keepdims=True) * q["A"]
        dprm_ref[2:3, :] = r_row[1:2, :]

    rev = lambda w: pl.BlockSpec((CHUNK, w), lambda c: (nc - 1 - c, 0))
    const = lambda a: pl.BlockSpec(a.shape, lambda c: (0,) * a.ndim)
    return pl.pallas_call(
        body, name=name, grid=(nc,),
        in_specs=[rev(D_INNER), rev(D_INNER), rev(D_INNER), rev(D_INNER), rev(2 * GN), rev(LANE),
                  pl.BlockSpec((1, N_PAIRS, D_STATE, LANE), lambda c: (nc - 1 - c, 0, 0, 0)),
                  const(prm), const(ng), const(expand), const(tri)],
        out_specs=[rev(CONV_DIM), rev(D_INNER), rev(LANE),
                   pl.BlockSpec((1, D_INNER), lambda c: (0, 0)), pl.BlockSpec((8, LANE), lambda c: (0, 0))],
        out_shape=[jax.ShapeDtypeStruct((L, CONV_DIM), F32), jax.ShapeDtypeStruct((L, D_INNER), BF16),
                   jax.ShapeDtypeStruct((L, LANE), F32), jax.ShapeDtypeStruct((1, D_INNER), F32),
                   jax.ShapeDtypeStruct((8, LANE), F32)],
        scratch_shapes=[pltpu.VMEM((N_PAIRS, D_STATE, LANE), F32), pltpu.VMEM((CHUNK, D_INNER), F32),
                        pltpu.VMEM((CHUNK, D_INNER), F32), pltpu.VMEM((CHUNK, D_INNER), F32),
                        pltpu.VMEM((8, D_INNER), F32)],
        compiler_params=_cp("arbitrary"),
    )(dyn, y, zx, xs, bc, dtr, st, prm, ng, expand, tri)


def _ssd_weights(W, j):
    w_in = W["ssm_in_w"][j]
    nzx = D_INNER + CONV_DIM
    wdt = jnp.pad(w_in[:, nzx:], ((0, 0), (0, LANE - N_HEADS)))
    prm = jnp.zeros((8, LANE), F32)
    prm = prm.at[0, :N_HEADS].set(W["ssm_dt_bias"][j]).at[1, :N_HEADS].set(W["ssm_A_log"][j])
    prm = prm.at[2, :N_HEADS].set(W["ssm_D"][j])
    return dict(wdt=wdt, cw=W["ssm_conv_w"][j], cb=W["ssm_conv_b"][j].reshape(1, CONV_DIM), prm=prm,
                ng=W["ssm_norm_g"][j].reshape(1, D_INNER))


def ssd_layer_fwd(h, W, j, tag):
    p = _ssd_weights(W, j)
    zx = mm(h, W["ssm_in_w"], layer=j, b_n=D_INNER + CONV_DIM, out_dtype=BF16, name=f"{tag}_zx")
    dtr = mm(h, p["wdt"], name=f"{tag}_dt")
    xs, pre_x = ssm_conv_fwd(zx, p["cw"], p["cb"], col0=D_INNER, ncols=D_INNER, wcol0=0, out_dtype=F32,
                             name=f"{tag}_convx")
    bc, pre_bc = ssm_conv_fwd(zx, p["cw"], p["cb"], col0=2 * D_INNER, ncols=2 * GN, wcol0=D_INNER, out_dtype=BF16,
                              name=f"{tag}_convbc")
    y, yn, st = ssd_fwd(xs, bc, dtr, zx, p["prm"], p["ng"], name=f"{tag}_scan")
    f = mm(yn, W["ssm_out_w"], layer=j, name=f"{tag}_out")
    return f, dict(h=h, zx=zx, dtr=dtr, xs=xs, bc=bc, pre_x=pre_x, pre_bc=pre_bc, y=y, yn=yn, st=st, p=p)


def ssd_layer_bwd(df, ctx, W, GB, j, tag):
    p = ctx["p"]
    h = ctx["h"]
    dyn = mm(df, W["ssm_out_w"], layer=j, tb=True, name=f"{tag}_b_dyn")
    GB["ssm_out_w"] = mm(ctx["yn"], df, ta=True, into=(GB["ssm_out_w"], j, 0), name=f"{tag}_b_gwo")
    dxbc, dz, ddtr, dng, dprm = ssd_bwd(dyn, ctx["y"], ctx["zx"], ctx["xs"], ctx["bc"], ctx["dtr"], ctx["st"],
                                        p["prm"], p["ng"], name=f"{tag}_b_scan")
    dx1, dcw1, dcb1 = ssm_conv_bwd(ctx["zx"], ctx["pre_x"], dxbc, p["cw"], col0=D_INNER, dcol0=0, ncols=D_INNER,
                                   name=f"{tag}_b_convx")
    dx2, dcw2, dcb2 = ssm_conv_bwd(ctx["zx"], ctx["pre_bc"], dxbc, p["cw"], col0=2 * D_INNER, dcol0=D_INNER,
                                   ncols=2 * GN, name=f"{tag}_b_convbc")
    dh = mm(dz, W["ssm_in_w"], layer=j, tb=True, b_k0=0, name=f"{tag}_b_dh1")
    dh = mm(dx1, W["ssm_in_w"], layer=j, tb=True, b_k0=D_INNER, acc=dh, name=f"{tag}_b_dh2")
    dh = mm(dx2, W["ssm_in_w"], layer=j, tb=True, b_k0=2 * D_INNER, acc=dh, name=f"{tag}_b_dh3")
    dh = mm(ddtr, p["wdt"], tb=True, acc=dh, name=f"{tag}_b_dh4")
    g_in = jnp.concatenate([mm(h, dz, ta=True, out_dtype=BF16, name=f"{tag}_b_gz"),
                            mm(h, dx1, ta=True, out_dtype=BF16, name=f"{tag}_b_gx"),
                            mm(h, dx2, ta=True, out_dtype=BF16, name=f"{tag}_b_gbc"),
                            mm(h, ddtr, ta=True, out_dtype=BF16, name=f"{tag}_b_gdt")[:, :N_HEADS]], axis=1)
    return dh, dict(ssm_in_w=g_in, ssm_conv_w=jnp.concatenate([dcw1, dcw2], axis=1),
                    ssm_conv_b=jnp.concatenate([dcb1, dcb2], axis=1)[0], ssm_dt_bias=dprm[0, :N_HEADS],
                    ssm_A_log=dprm[1, :N_HEADS], ssm_D=dprm[2, :N_HEADS], ssm_norm_g=dng[0])


def cf_layer_fwd(h, W, j, tag):
    u = mm(h, W["cf_pw1_w"], layer=j, bias=W["cf_pw1_b"][j], out_dtype=BF16, name=f"{tag}_pw1")
    c, s = cf_fwd(u, W["cf_dw_w"][j], W["cf_dw_b"][j].reshape(1, -1), W["cf_ln_g"][j].reshape(1, -1),
                  W["cf_ln_b"][j].reshape(1, -1), name=f"{tag}_conv")
    f = mm(s, W["cf_pw2_w"], layer=j, bias=W["cf_pw2_b"][j], name=f"{tag}_pw2")
    return f, dict(h=h, u=u, c=c, s=s)


def cf_layer_bwd(df, ctx, W, GB, j, tag):
    h = ctx["h"]
    ds = mm(df, W["cf_pw2_w"], layer=j, tb=True, name=f"{tag}_b_ds")
    GB["cf_pw2_w"] = mm(ctx["s"], df, ta=True, into=(GB["cf_pw2_w"], j, 0), name=f"{tag}_b_gpw2")
    g_b2 = colsum(df, name=f"{tag}_b_gb2")
    dc, dlg, dlb = cf_bwd_ln(ctx["c"], ds, W["cf_ln_g"][j].reshape(1, -1), W["cf_ln_b"][j].reshape(1, -1),
                             name=f"{tag}_b_ln")
    du, ddw, ddb = cf_bwd_conv(ctx["u"], dc, W["cf_dw_w"][j], name=f"{tag}_b_conv")
    dh = mm(du, W["cf_pw1_w"], layer=j, tb=True, name=f"{tag}_b_dh")
    GB["cf_pw1_w"] = mm(h, du, ta=True, into=(GB["cf_pw1_w"], j, 0), name=f"{tag}_b_gpw1")
    g_b1 = colsum(du, name=f"{tag}_b_gb1")
    return dh, dict(cf_pw1_b=g_b1[0], cf_dw_w=ddw, cf_dw_b=ddb[0], cf_ln_g=dlg[0], cf_ln_b=dlb[0], cf_pw2_b=g_b2[0])


def xa_layer_fwd(h, mem, W, i, tag):
    m = norm_fwd(mem, W["xa_mem_g"][i], name=f"{tag}_memnorm")
    kv = mm(m, W["xa_kv_w"], layer=i, out_dtype=BF16, name=f"{tag}_kv")
    q = mm(h, W["xa_q_w"], layer=i, out_dtype=BF16, name=f"{tag}_q")
    o = attn_fwd(q, kv, name=f"{tag}_attn")
    f = mm(o, W["xa_o_w"], layer=i, name=f"{tag}_o")
    return f, dict(h=h, m=m, kv=kv, q=q, o=o)


def xa_layer_bwd(df, ctx, mem, W, GB, i, tag):
    h = ctx["h"]
    do = mm(df, W["xa_o_w"], layer=i, tb=True, out_dtype=BF16, name=f"{tag}_b_do")
    GB["xa_o_w"] = mm(ctx["o"], df, ta=True, into=(GB["xa_o_w"], i, 0), name=f"{tag}_b_go")
    dq, dkv = attn_bwd(ctx["q"], ctx["kv"], do, name=f"{tag}_b_attn")
    dh = mm(dq, W["xa_q_w"], layer=i, tb=True, name=f"{tag}_b_dh")
    GB["xa_q_w"] = mm(h, dq, ta=True, into=(GB["xa_q_w"], i, 0), name=f"{tag}_b_gq")
    GB["xa_kv_w"] = mm(ctx["m"], dkv, ta=True, into=(GB["xa_kv_w"], i, 0), name=f"{tag}_b_gkv")
    dm = mm(dkv, W["xa_kv_w"], layer=i, tb=True, name=f"{tag}_b_dm")
    g_mg = norm_dg(mem, dm, name=f"{tag}_b_gmem")
    return dh, dict(xa_mem_g=g_mg[0])


def ffn_layer_fwd(h, W, i, tag):
    cw, cb = W["ffn_conv_w"][i], W["ffn_conv_b"][i].reshape(1, -1)
    u = mm(h, W["ffn_in_w"], layer=i, out_dtype=BF16, name=f"{tag}_in")
    act, c = ffn_act_fwd(u, cw, cb, name=f"{tag}_act")
    f = mm(act, W["ffn_out_w"], layer=i, name=f"{tag}_out")
    return f, dict(h=h, u=u, c=c, act=act)


def ffn_layer_bwd(df, ctx, W, GB, i, tag):
    h = ctx["h"]
    dact = mm(df, W["ffn_out_w"], layer=i, tb=True, out_dtype=BF16, name=f"{tag}_b_dact")
    GB["ffn_out_w"] = mm(ctx["act"], df, ta=True, into=(GB["ffn_out_w"], i, 0), name=f"{tag}_b_gout")
    du, dcw, dcb = ffn_act_bwd(ctx["u"], ctx["c"], dact, W["ffn_conv_w"][i], name=f"{tag}_b_act")
    dh = None
    for half in range(2):
        dh = mm(du, W["ffn_in_w"], a_idx=half, layer=i, tb=True, b_k0=half * D_FF, acc=dh, name=f"{tag}_b_dh{half}")
        GB["ffn_in_w"] = mm(h, du, ta=True, layer=half, into=(GB["ffn_in_w"], i, half * D_FF), name=f"{tag}_b_gin{half}")
    cat = lambda a: jnp.concatenate([a[0], a[1]], axis=-1)
    return dh, dict(ffn_conv_w=cat(dcw), ffn_conv_b=cat(dcb)[0])


def _layer_weights(i):
    j = i // 2
    mixer = [("ssm_in_w", j), ("ssm_out_w", j)] if i % 2 == 0 else [("cf_pw1_w", j), ("cf_pw2_w", j)]
    return mixer + [(n, i) for n in ("xa_q_w", "xa_kv_w", "xa_o_w", "ffn_in_w", "ffn_out_w")]


def local_step(x, mem, target, W, fetch=None):
    subs = [(i, s) for i in range(DEPTH) for s in range(3)]
    ng = W["norm_g"]

    def fwd(i, s, h):
        tag = f"l{i}s{s}"
        if s == 0:
            return ssd_layer_fwd(h, W, i // 2, tag) if i % 2 == 0 else cf_layer_fwd(h, W, i // 2, tag)
        if s == 1:
            return xa_layer_fwd(h, mem, W, i, tag)
        return ffn_layer_fwd(h, W, i, tag)

    GB = {}

    def bwd(i, s, df, ctx):
        tag = f"l{i}s{s}"
        if s == 0:
            return (ssd_layer_bwd if i % 2 == 0 else cf_layer_bwd)(df, ctx, W, GB, i // 2, tag)
        if s == 1:
            return xa_layer_bwd(df, ctx, mem, W, GB, i, tag)
        return ffn_layer_bwd(df, ctx, W, GB, i, tag)

    h = norm_fwd(x, ng[0, 0], name="norm0")
    saved = []
    dxp = loss = None
    for k, (i, s) in enumerate(subs):
        if s == 0 and fetch is not None:
            fetch(i, x)
        f, ctx = fwd(i, s, h)
        saved.append((x, f, ctx))
        if k + 1 < len(subs):
            ni, ns = subs[k + 1]
            x, h = bnd_fwd(x, f, ng[i, 2 * s + 1], ng[ni, 2 * ns], name=f"bnd{k}")
        else:
            dxp, loss = final_fwd(x, f, ng[i, 2 * s + 1], target, name="final")

    for n in BIG:
        if n != "ssm_in_w":
            GB[n] = jnp.zeros((len(W[n]), *W[n][0].shape), BF16)
    grads = {}

    def put(name, idx, val):
        grads.setdefault(name, {})[idx] = val

    i, s = subs[-1]
    top = bnd_bwd(dxp, post=(saved[-1][1], ng[i, 2 * s + 1]), name="bbnd_top")
    put("norm_g", (i, 2 * s + 1), top["dgpost"][0])
    df = top["df"]
    for k in range(len(subs) - 1, -1, -1):
        i, s = subs[k]
        xk, _, ctx = saved[k]
        dh, gw = bwd(i, s, df, ctx)
        for name, val in gw.items():
            put(name, i // 2 if name.startswith(("ssm_", "cf_")) else i, val)
        if k > 0:
            pi, ps = subs[k - 1]
            r = bnd_bwd(dxp, pre=(xk, ng[i, 2 * s], dh), post=(saved[k - 1][1], ng[pi, 2 * ps + 1]), name=f"bbnd{k}")
            put("norm_g", (pi, 2 * ps + 1), r["dgpost"][0])
            df = r["df"]
        else:
            r = bnd_bwd(dxp, pre=(xk, ng[i, 2 * s], dh), name="bbnd0")
        put("norm_g", (i, 2 * s), r["dgpre"][0])
        dxp = r["dx"]

    out = dict(GB)
    for name, d in grads.items():
        if name == "norm_g":
            out[name] = jnp.stack([jnp.stack([d[(i, t)] for t in range(6)]) for i in range(DEPTH)])
        else:
            out[name] = jnp.stack([d[j] for j in sorted(d)])
    return loss, dxp, out


ANY = pl.BlockSpec(memory_space=pl.ANY)


def _pos():
    return lax.axis_index("x"), lax.axis_index("y"), lax.axis_index("c")


def all_gather(shard, name):
    R, C = shard.shape

    def body(x_ref, out_ref, send_sems, recv_sems, local_sem):
        x, y, c = _pos()
        me, sibling = (x, y, c), (x, y, 1 - c)
        chips = [(1 - x, y), (x, 1 - y), (1 - x, 1 - y)]

        def slot(px, py, pc):
            return out_ref.at[4 * px + 2 * py + pc]

        def copy(k, block, to, src=None):
            return pltpu.make_async_remote_copy(
                src_ref=slot(*block) if src is None else src, dst_ref=slot(*block),
                send_sem=send_sems.at[k], recv_sem=recv_sems.at[k], device_id=to, device_id_type=MESH)

        mine = pltpu.make_async_copy(x_ref, slot(*me), local_sem)
        mine.start()
        first = [copy(0, me, sibling, src=x_ref)]
        first += [copy(1 + j, me, (*chip, c), src=x_ref) for j, chip in enumerate(chips)]
        for cp in first:
            cp.start()
        passed = [copy(4 + j, (*chip, c), sibling) for j, chip in enumerate(chips)]
        for j, chip in enumerate(chips):
            copy(1 + j, (*chip, c), me).wait_recv()
            passed[j].start()
        copy(0, sibling, me).wait_recv()
        for j, chip in enumerate(chips):
            copy(4 + j, (*chip, 1 - c), me).wait_recv()
        for cp in first + passed:
            cp.wait_send()
        mine.wait()

    return pl.pallas_call(
        body, name=name, out_shape=jax.ShapeDtypeStruct((N_DEV, R, C), shard.dtype),
        in_specs=[ANY], out_specs=ANY,
        scratch_shapes=[pltpu.SemaphoreType.DMA((7,)), pltpu.SemaphoreType.DMA((7,)), pltpu.SemaphoreType.DMA(())],
    )(shard)


def _win(ref, kind, k, a, b):
    if kind == "lead":
        return ref.at[k]
    if kind == "row":
        return ref.at[:, pl.ds(pl.multiple_of(k * a, 16), a), :]
    return ref.at[:, :, pl.ds(pl.multiple_of(k * b, LANE), b)]


def _full_shape(shard_shape, kind):
    n, a, b = shard_shape
    return {"lead": (N_DEV, n, a, b), "row": (n, N_DEV * a, b), "col": (n, a, N_DEV * b)}[kind]


HBM = pl.BlockSpec(memory_space=pltpu.HBM)
SEMS = pl.BlockSpec(memory_space=pltpu.SEMAPHORE)
DATAFLOW = pltpu.SideEffectType.DATAFLOW_SIDE_EFFECTING
N_PEER = N_DEV - 1


def _in_hbm(a):
    return pltpu.with_memory_space_constraint(a, pltpu.HBM)


def _peer(x, y, c, r):
    return ((1 - x) if r & 4 else x, (1 - y) if r & 2 else y, (1 - c) if r & 1 else c)


def _win2(ref, kind, k, a, b):
    if kind == "lead":
        return ref.at[k]
    if kind == "row":
        return ref.at[pl.ds(pl.multiple_of(k * a, 16), a), :]
    return ref.at[:, pl.ds(pl.multiple_of(k * b, LANE), b)]


def _zone_shape(kind, a, b):
    return {"lead": (N_DEV, a, b), "row": (N_DEV * a, b), "col": (a, N_DEV * b)}[kind]


def gather_place(shards, items, name):
    ns = len(shards)

    def body(*refs):
        x_refs, o_refs, sems = refs[:ns], refs[ns:-1], refs[-1]
        x, y, c = _pos()
        me = 4 * x + 2 * y + c
        cps = [pltpu.make_async_copy(x_refs[w].at[l], _win2(o_refs[t], kind, me, a, b), sems.at[t])
               for t, (w, l, kind, a, b) in enumerate(items)]
        for cp in cps:
            cp.start()
        for cp in cps:
            cp.wait()

    return pl.pallas_call(
        body, name=name,
        out_shape=[jax.ShapeDtypeStruct(_zone_shape(kind, a, b), shards[w].dtype) for w, l, kind, a, b in items],
        in_specs=[ANY] * ns, out_specs=[ANY] * len(items),
        scratch_shapes=[pltpu.SemaphoreType.DMA((len(items),))],
    )(*shards)


def gather_start(shards, zones, items, name):
    ns, nz = len(shards), len(zones)

    def body(*refs):
        x_refs = refs[:ns]
        send_sems, recv_sems = refs[ns + nz], refs[ns + nz + 1]
        z_refs = refs[ns + nz + 2 + ns:ns + nz + 2 + ns + nz]
        token = refs[-1]
        x, y, c = _pos()
        me = 4 * x + 2 * y + c
        for t, (w, l, kind, a, b) in enumerate(items):
            for r in range(1, N_DEV):
                pltpu.make_async_remote_copy(
                    src_ref=x_refs[w].at[l], dst_ref=_win2(z_refs[t], kind, me, a, b),
                    send_sem=send_sems.at[N_PEER * t + r - 1], recv_sem=recv_sems.at[N_PEER * t + r - 1],
                    device_id=_peer(x, y, c, r), device_id_type=MESH).start()
        token[...] = jnp.zeros_like(token)

    n_sem = N_PEER * nz
    outs = pl.pallas_call(
        body, name=name,
        out_shape=(pltpu.SemaphoreType.DMA((n_sem,)), pltpu.SemaphoreType.DMA((n_sem,)),
                   *[pltpu.HBM(s.shape, s.dtype) for s in shards], *[pltpu.HBM(z.shape, z.dtype) for z in zones],
                   jax.ShapeDtypeStruct((8, LANE), F32)),
        in_specs=[HBM] * (ns + nz),
        out_specs=(SEMS, SEMS, *[HBM] * (ns + nz), pl.BlockSpec(memory_space=pltpu.VMEM)),
        input_output_aliases={i: 2 + i for i in range(ns + nz)},
        compiler_params=pltpu.CompilerParams(has_side_effects=DATAFLOW),
    )(*[_in_hbm(s) for s in shards], *[_in_hbm(z) for z in zones])
    return outs[0], outs[1], list(outs[2:2 + ns]), list(outs[2 + ns:2 + ns + nz]), outs[-1]


def gather_wait(zones, idx, items, send_sems, recv_sems, after, keep, name):
    nz, nk = len(zones), len(keep)

    def body(*refs):
        z_refs = refs[:nz]
        send_sems, recv_sems = refs[nz], refs[nz + 1]
        x, y, c = _pos()
        me = 4 * x + 2 * y + c
        for z_ref, t in zip(z_refs, idx):
            w, l, kind, a, b = items[t]
            for r in range(1, N_DEV):
                peer = _peer(x, y, c, r)
                cp = pltpu.make_async_remote_copy(
                    src_ref=_win2(z_ref, kind, me, a, b),
                    dst_ref=_win2(z_ref, kind, 4 * peer[0] + 2 * peer[1] + peer[2], a, b),
                    send_sem=send_sems.at[N_PEER * t + r - 1], recv_sem=recv_sems.at[N_PEER * t + r - 1],
                    device_id=peer, device_id_type=MESH)
                cp.wait_send()
                cp.wait_recv()

    outs = pl.pallas_call(
        body, name=name, out_shape=tuple(pltpu.HBM(z.shape, z.dtype) for z in zones),
        in_specs=[HBM] * nz + [SEMS, SEMS] + [pl.BlockSpec(memory_space=pl.ANY)] * (1 + nk),
        out_specs=tuple([HBM] * nz), input_output_aliases={i: i for i in range(nz)},
        compiler_params=pltpu.CompilerParams(has_side_effects=DATAFLOW),
    )(*zones, send_sems, recv_sems, after, *keep)
    return list(outs)


def rs_sibling(gs, kinds, geo, name):
    nw = len(gs)

    def body(*refs):
        g_refs, o_refs = refs[:nw], refs[nw:2 * nw]
        send_sems, recv_sems = refs[2 * nw:]
        x, y, c = _pos()
        cps = [pltpu.make_async_remote_copy(
            src_ref=_win(g_refs[w], kinds[w], 2 * j + 1 - c, *geo[w][1:]), dst_ref=o_refs[w].at[j],
            send_sem=send_sems.at[4 * w + j], recv_sem=recv_sems.at[4 * w + j],
            device_id=(x, y, 1 - c), device_id_type=MESH) for w in range(nw) for j in range(4)]
        for cp in cps:
            cp.start()
        for cp in cps:
            cp.wait()

    return pl.pallas_call(
        body, name=name, out_shape=[jax.ShapeDtypeStruct((4, *geo[w]), gs[w].dtype) for w in range(nw)],
        in_specs=[ANY] * nw, out_specs=[ANY] * nw,
        scratch_shapes=[pltpu.SemaphoreType.DMA((4 * nw,)), pltpu.SemaphoreType.DMA((4 * nw,))],
    )(*gs)


def rs_add(g, kind, got, name):
    _, n, a, b = got.shape
    ta = a if a <= 512 else 256
    per = a // ta
    core = lax.axis_index("c").astype(jnp.int32).reshape(1)

    def body(c_ref, g_ref, got_ref, o_ref):
        o_ref[...] = (g_ref[...].astype(F32) + got_ref[...].astype(F32)).astype(o_ref.dtype)

    if kind == "lead":
        g_spec = pl.BlockSpec((None, None, ta, b), lambda j, l, r, c_ref: (2 * j + c_ref[0], l, r, 0))
    elif kind == "row":
        g_spec = pl.BlockSpec((None, ta, b), lambda j, l, r, c_ref: (l, (2 * j + c_ref[0]) * per + r, 0))
    else:
        g_spec = pl.BlockSpec((None, ta, b), lambda j, l, r, c_ref: (l, r, 2 * j + c_ref[0]))
    blk = pl.BlockSpec((None, None, ta, b), lambda j, l, r, c_ref: (j, l, r, 0))
    return pl.pallas_call(
        body, name=name, out_shape=jax.ShapeDtypeStruct(got.shape, got.dtype),
        grid_spec=pltpu.PrefetchScalarGridSpec(num_scalar_prefetch=1, grid=(4, n, per),
                                               in_specs=[g_spec, blk], out_specs=blk),
        compiler_params=_cp("parallel", "parallel", "parallel"),
    )(core, g, got)


def rs_chips(s1s, name):
    nw = len(s1s)

    def body(*refs):
        s_refs, o_refs = refs[:nw], refs[nw:2 * nw]
        send_sems, recv_sems, local_sems = refs[2 * nw:]
        x, y, c = _pos()
        jme = 2 * x + y
        peers = [(1 - x, y), (x, 1 - y), (1 - x, 1 - y)]
        local = [pltpu.make_async_copy(s_refs[w].at[jme], o_refs[w].at[jme], local_sems.at[w]) for w in range(nw)]
        for cp in local:
            cp.start()

        def copy(w, k, src_slot, dst_slot, peer):
            return pltpu.make_async_remote_copy(
                src_ref=s_refs[w].at[src_slot], dst_ref=o_refs[w].at[dst_slot], send_sem=send_sems.at[3 * w + k],
                recv_sem=recv_sems.at[3 * w + k], device_id=(*peer, c), device_id_type=MESH)

        sends = [copy(w, k, 2 * px + py, jme, (px, py)) for w in range(nw) for k, (px, py) in enumerate(peers)]
        for cp in sends:
            cp.start()
        for w in range(nw):
            for k, (px, py) in enumerate(peers):
                copy(w, k, jme, 2 * px + py, (px, py)).wait_recv()
        for cp in sends:
            cp.wait_send()
        for cp in local:
            cp.wait()

    return pl.pallas_call(
        body, name=name, out_shape=[jax.ShapeDtypeStruct(s.shape, s.dtype) for s in s1s],
        in_specs=[ANY] * nw, out_specs=[ANY] * nw,
        scratch_shapes=[pltpu.SemaphoreType.DMA((3 * nw,)), pltpu.SemaphoreType.DMA((3 * nw,)),
                        pltpu.SemaphoreType.DMA((nw,))],
    )(*s1s)


def small_exchange(sh, rep, name):
    _, Rs, C = sh.shape
    Rr = rep.shape[0]

    def body(sh_ref, rep_ref, sh_out, rep_out, send_sems, recv_sems, local_sems):
        x, y, c = _pos()
        me = 4 * x + 2 * y + c
        l1 = pltpu.make_async_copy(sh_ref.at[me], sh_out.at[me], local_sems.at[0])
        l2 = pltpu.make_async_copy(rep_ref, rep_out.at[me], local_sems.at[1])
        l1.start()
        l2.start()

        def flip(v, bit):
            return 1 - v if bit else v

        sends, recvs = [], []
        for r in range(1, N_DEV):
            peer = (flip(x, r & 4), flip(y, r & 2), flip(c, r & 1))
            pid = 4 * peer[0] + 2 * peer[1] + peer[2]
            k = 2 * (r - 1)
            mk = lambda src, dst, kk: pltpu.make_async_remote_copy(
                src_ref=src, dst_ref=dst, send_sem=send_sems.at[kk], recv_sem=recv_sems.at[kk],
                device_id=peer, device_id_type=MESH)
            sends += [mk(sh_ref.at[pid], sh_out.at[me], k), mk(rep_ref, rep_out.at[me], k + 1)]
            recvs += [mk(sh_ref.at[me], sh_out.at[pid], k), mk(rep_ref, rep_out.at[pid], k + 1)]
        for cp in sends:
            cp.start()
        for cp in recvs:
            cp.wait_recv()
        for cp in sends:
            cp.wait_send()
        l1.wait()
        l2.wait()

    n = 2 * (N_DEV - 1)
    return pl.pallas_call(
        body, name=name,
        out_shape=[jax.ShapeDtypeStruct((N_DEV, Rs, C), sh.dtype), jax.ShapeDtypeStruct((N_DEV, *rep.shape), rep.dtype)],
        in_specs=[ANY, ANY], out_specs=[ANY, ANY],
        scratch_shapes=[pltpu.SemaphoreType.DMA((n,)), pltpu.SemaphoreType.DMA((n,)), pltpu.SemaphoreType.DMA((2,))],
    )(sh, rep)


def adam_slots(w, m, v, slots, name):
    S, n, a, b = slots.shape
    ta = max(t for t in range(16, min(a, 512) + 1, 8)
             if a % t == 0 and t * S * b * slots.dtype.itemsize <= 4 * 1024 * 1024)

    def body(w_ref, m_ref, v_ref, s_ref, g_ref, d_ref, m2_ref, v2_ref):
        gv = s_ref[0].astype(F32)
        for k in range(1, S):
            gv = gv + s_ref[k].astype(F32)
        m2 = ADAM_B1 * m_ref[...] + (1.0 - ADAM_B1) * gv
        v2 = ADAM_B2 * v_ref[...] + (1.0 - ADAM_B2) * (gv * gv)
        m_hat = m2 / (1.0 - ADAM_B1 ** ADAM_STEP)
        v_hat = v2 / (1.0 - ADAM_B2 ** ADAM_STEP)
        g_ref[...] = gv
        d_ref[...] = -ADAM_LR * (m_hat / (jnp.sqrt(v_hat) + ADAM_EPS) + ADAM_WD * w_ref[...])
        m2_ref[...] = m2
        v2_ref[...] = v2

    spec = pl.BlockSpec((None, ta, b), lambda l, r: (l, r, 0))
    return pl.pallas_call(
        body, name=name, grid=(n, a // ta),
        in_specs=[spec] * 3 + [pl.BlockSpec((S, None, ta, b), lambda l, r: (0, l, r, 0))], out_specs=[spec] * 4,
        out_shape=[jax.ShapeDtypeStruct((n, a, b), F32)] * 4, compiler_params=_cp("parallel", "parallel"),
    )(w, m, v, slots)


WEIGHTS = ["norm_g", "ssm_in_w", "ssm_conv_w", "ssm_conv_b", "ssm_dt_bias", "ssm_A_log", "ssm_D", "ssm_norm_g",
           "ssm_out_w", "cf_pw1_w", "cf_pw1_b", "cf_dw_w", "cf_dw_b", "cf_ln_g", "cf_ln_b", "cf_pw2_w", "cf_pw2_b",
           "xa_mem_g", "xa_q_w", "xa_kv_w", "xa_o_w", "ffn_in_w", "ffn_conv_w", "ffn_conv_b", "ffn_out_w"]
ARGS = ["x", "mem"] + WEIGHTS + ["loss_target"] + ["m_" + n for n in WEIGHTS] + ["v_" + n for n in WEIGHTS]
BIG = {"ssm_in_w": "col", "ssm_out_w": "row", "cf_pw1_w": "col", "cf_pw2_w": "row", "xa_q_w": "row",
       "xa_kv_w": "col", "xa_o_w": "row", "ffn_in_w": "col", "ffn_out_w": "row"}
SMALL = ["norm_g", "ssm_conv_w", "cf_pw1_b", "cf_dw_w", "cf_dw_b", "cf_ln_g", "cf_ln_b", "cf_pw2_b", "ffn_conv_w"]
REP = ["ssm_conv_b", "ssm_dt_bias", "ssm_A_log", "ssm_D", "ssm_norm_g", "xa_mem_g", "ffn_conv_b"]
SMALL_W = 768
REP_W = 5632


def _r8(n):
    return -(-n // 8) * 8


def _stack2d(arrs, wid):
    parts = []
    for a in arrs:
        r, c = a.shape[-2:]
        parts.append(jnp.pad(a, [(0, 0)] * (a.ndim - 2) + [(0, _r8(r) - r), (0, wid - c)]))
    return jnp.concatenate(parts, axis=-2)


def _unstack2d(buf, shapes2d):
    out, o = [], 0
    for r, c in shapes2d:
        out.append(buf[..., o:o + r, :c])
        o += _r8(r)
    return out


def _gathered_to_full(g):
    lead = g.shape[1:-1]
    return jnp.moveaxis(g, 0, -2).reshape(*lead, N_DEV * g.shape[-1])


def _full_to_slots(w):
    lead = w.shape[:-1]
    return jnp.moveaxis(w.reshape(*lead, N_DEV, w.shape[-1] // N_DEV), -2, 0)


def kernel(x, mem, norm_g, ssm_in_w, ssm_conv_w, ssm_conv_b, ssm_dt_bias, ssm_A_log, ssm_D, ssm_norm_g, ssm_out_w, cf_pw1_w, cf_pw1_b, cf_dw_w, cf_dw_b, cf_ln_g, cf_ln_b, cf_pw2_w, cf_pw2_b, xa_mem_g, xa_q_w, xa_kv_w, xa_o_w, ffn_in_w, ffn_conv_w, ffn_conv_b, ffn_out_w, loss_target, m_norm_g, m_ssm_in_w, m_ssm_conv_w, m_ssm_conv_b, m_ssm_dt_bias, m_ssm_A_log, m_ssm_D, m_ssm_norm_g, m_ssm_out_w, m_cf_pw1_w, m_cf_pw1_b, m_cf_dw_w, m_cf_dw_b, m_cf_ln_g, m_cf_ln_b, m_cf_pw2_w, m_cf_pw2_b, m_xa_mem_g, m_xa_q_w, m_xa_kv_w, m_xa_o_w, m_ffn_in_w, m_ffn_conv_w, m_ffn_conv_b, m_ffn_out_w, v_norm_g, v_ssm_in_w, v_ssm_conv_w, v_ssm_conv_b, v_ssm_dt_bias, v_ssm_A_log, v_ssm_D, v_ssm_norm_g, v_ssm_out_w, v_cf_pw1_w, v_cf_pw1_b, v_cf_dw_w, v_cf_dw_b, v_cf_ln_g, v_cf_ln_b, v_cf_pw2_w, v_cf_pw2_b, v_xa_mem_g, v_xa_q_w, v_xa_kv_w, v_xa_o_w, v_ffn_in_w, v_ffn_conv_w, v_ffn_conv_b, v_ffn_out_w):
    return _step(x, mem, norm_g, ssm_in_w, ssm_conv_w, ssm_conv_b, ssm_dt_bias, ssm_A_log, ssm_D, ssm_norm_g, ssm_out_w, cf_pw1_w, cf_pw1_b, cf_dw_w, cf_dw_b, cf_ln_g, cf_ln_b, cf_pw2_w, cf_pw2_b, xa_mem_g, xa_q_w, xa_kv_w, xa_o_w, ffn_in_w, ffn_conv_w, ffn_conv_b, ffn_out_w, loss_target, m_norm_g, m_ssm_in_w, m_ssm_conv_w, m_ssm_conv_b, m_ssm_dt_bias, m_ssm_A_log, m_ssm_D, m_ssm_norm_g, m_ssm_out_w, m_cf_pw1_w, m_cf_pw1_b, m_cf_dw_w, m_cf_dw_b, m_cf_ln_g, m_cf_ln_b, m_cf_pw2_w, m_cf_pw2_b, m_xa_mem_g, m_xa_q_w, m_xa_kv_w, m_xa_o_w, m_ffn_in_w, m_ffn_conv_w, m_ffn_conv_b, m_ffn_out_w, v_norm_g, v_ssm_in_w, v_ssm_conv_w, v_ssm_conv_b, v_ssm_dt_bias, v_ssm_A_log, v_ssm_D, v_ssm_norm_g, v_ssm_out_w, v_cf_pw1_w, v_cf_pw1_b, v_cf_dw_w, v_cf_dw_b, v_cf_ln_g, v_cf_ln_b, v_cf_pw2_w, v_cf_pw2_b, v_xa_mem_g, v_xa_q_w, v_xa_kv_w, v_xa_o_w, v_ffn_in_w, v_ffn_conv_w, v_ffn_conv_b, v_ffn_out_w)


def _step(*args):
    A = dict(zip(ARGS, args, strict=True))
    x, mem, target = A["x"][0], A["mem"][0], A["loss_target"][0]

    big = list(BIG)
    geo = [A[n].shape for n in big]
    kinds = ["row" if BIG[n] == "row" else ("col" if A[n].shape[-1] % LANE == 0 else "lead") for n in big]
    W = {n: A[n] for n in REP}
    shards = [A[n].astype(BF16) for n in big]
    items, layer_items = [], []
    for i in range(DEPTH):
        layer_items.append([])
        for n, l in _layer_weights(i):
            w = big.index(n)
            layer_items[i].append(len(items))
            items.append((w, l, kinds[w], *geo[w][1:]))
    zones = gather_place(shards, items, name="gather_place")
    send_sems, recv_sems, shards_thru, zones, token = gather_start(shards, zones, items, name="gather_start")
    for n in big:
        W[n] = [None] * A[n].shape[0]

    def fetch(i, x_in):
        ids = layer_items[i]
        got = gather_wait([zones[t] for t in ids], ids, items, send_sems, recv_sems, token if i == 0 else x_in,
                          shards_thru if i == DEPTH - 1 else [], name=f"gather_wait{i}")
        for t, z in zip(ids, got):
            w, l, kind = items[t][:3]
            W[big[w]][l] = _gathered_to_full(z) if kind == "lead" else z

    small2d = [(A[n].size // A[n].shape[-1], A[n].shape[-1]) for n in SMALL]
    rep2d = [A[n].shape for n in REP] + [(1, 1)]
    stack_small = lambda pre: _stack2d([A[pre + n].reshape(rc) for n, rc in zip(SMALL, small2d)], SMALL_W)
    stack_rep = lambda pre: _stack2d([A[pre + n] for n in REP] + [jnp.zeros((1, 1), F32)], REP_W)
    small_g = all_gather(stack_small(""), name="gather_small")
    for n, g in zip(SMALL, _unstack2d(small_g, small2d)):
        W[n] = _gathered_to_full(g.reshape(N_DEV, *A[n].shape))

    loss, grad_x, G = local_step(x, mem, target, W, fetch)

    gs = [_full_to_slots(G[n]) if kind == "lead" else G[n] for n, kind in zip(big, kinds)]
    got1 = rs_sibling(gs, kinds, geo, name="rs_sibling")
    s1 = [rs_add(g, kind, got, name=f"rs_add_{n}") for n, g, kind, got in zip(big, gs, kinds, got1)]
    got2 = rs_chips(s1, name="rs_chips")

    sh = _stack2d([_full_to_slots(G[n]).reshape(N_DEV, *rc) for n, rc in zip(SMALL, small2d)], SMALL_W)
    rep = _stack2d([G[n] for n in REP] + [loss[:, :1]], REP_W)
    sh_got, rep_got = small_exchange(sh, rep, name="small_exchange")

    res = {}
    for n, slots in zip(big, got2):
        res[n] = adam_slots(A[n], A["m_" + n], A["v_" + n], slots, name=f"adam_{n}")
    for names, shapes2d, stack, slots, tag in ((SMALL, small2d, stack_small, sh_got, "small"),
                                               (REP, rep2d, stack_rep, rep_got, "rep")):
        outs4 = adam_slots(stack("")[None], stack("m_")[None], stack("v_")[None], slots[:, None], name=f"adam_{tag}")
        parts = [_unstack2d(o[0], shapes2d) for o in outs4]
        for k, n in enumerate(names):
            res[n] = tuple(q[k].reshape(A[n].shape) for q in parts)
        if tag == "rep":
            total_loss = parts[0][-1][0, 0]

    outs = [total_loss, grad_x[None]]
    for k in range(4):
        outs += [res[n][k] for n in WEIGHTS]
    return tuple(outs)
```

```python
import jax
import jax.numpy as jnp
from jax import lax
from jax.experimental import pallas as pl
from jax.experimental.pallas import tpu as pltpu

F32 = jnp.float32
BF16 = jnp.bfloat16

D_MODEL = 1024
D_INNER = 2048
N_HEADS = 32
HEAD_DIM = 64
N_GROUPS = 4
D_STATE = 128
CHUNK = 128
CONV_DIM = 3072
SSM_K = 4
CF_K = 31
N_MEM = 256
XA_HEADS = 4
XA_HD = 256
D_FF = 2816
FFN_K = 3
EPS = 1e-6
DEPTH = 4
N_DEV = 8

ADAM_LR = 0.001
ADAM_B1 = 0.9
ADAM_B2 = 0.999
ADAM_EPS = 1e-08
ADAM_WD = 0.01
ADAM_STEP = 10

LANE = 128
VMEM_LIMIT = 56 * 1024 * 1024
NEG = -1e30
MESH = pl.DeviceIdType.MESH


def _cp(*sem):
    return pltpu.CompilerParams(dimension_semantics=sem if sem else None, vmem_limit_bytes=VMEM_LIMIT)


def _tile(n, cap):
    if n <= cap:
        return n
    best = 0
    for t in range(LANE, cap + 1, LANE):
        if n % t == 0:
            best = t
    assert best, (n, cap)
    return best


def _sig(x):
    return 1.0 / (1.0 + jnp.exp(-x))


def _split3(v):
    v0 = v.astype(BF16)
    r1 = v - v0.astype(F32)
    v1 = r1.astype(BF16)
    v2 = (r1 - v1.astype(F32)).astype(BF16)
    return v0, v1, v2


def _dot(a, b, ca=1, cb=0):
    return lax.dot_general(a, b, (((ca,), (cb,)), ((), ())), preferred_element_type=F32)


def _dot3(v, m, ca=1, cb=0):
    v0, v1, v2 = _split3(v)
    return _dot(v0, m, ca, cb) + _dot(v1, m, ca, cb) + _dot(v2, m, ca, cb)


def mm(a, b, *, ta=False, tb=False, bias=None, acc=None, out_dtype=F32, a_idx=None, layer=None, b_k0=0, b_n=None,
       into=None, name):
    if isinstance(b, (list, tuple)):
        b, layer = b[layer], None
    if ta:
        K, M = a.shape[-2:]
    else:
        M, K = a.shape[-2:]
    N = b_n if b_n is not None else (b.shape[-2] if tb else b.shape[-1])
    assert (b.ndim == 3) == (layer is not None) and (a.ndim == 3) == (a_idx is not None)
    tm = _tile(M, 1024)
    tn = _tile(N, 1536)
    tk = _tile(K, 2048)
    nk = K // tk
    assert b_k0 % tk == 0 and b_k0 + K <= (b.shape[-1] if tb else b.shape[-2])
    kb = b_k0 // tk
    has_bias, has_acc = bias is not None, acc is not None
    if into is not None:
        out_dtype = into[0].dtype
        assert into[0].shape[1] == M and into[2] % tn == 0 and into[2] + N <= into[0].shape[2] and not has_acc

    def body(*refs):
        a_ref, b_ref = refs[0], refs[1]
        pos = 2
        bias_ref = acc_ref = None
        if has_bias:
            bias_ref = refs[pos]
            pos += 1
        if has_acc:
            acc_ref = refs[pos]
            pos += 1
        if into is not None:
            pos += 1
        o_ref = refs[pos]
        s_ref = refs[pos + 1] if nk > 1 else None
        p = _dot(a_ref[...].astype(BF16), b_ref[...].astype(BF16), 0 if ta else 1, 1 if tb else 0)

        def extras(v):
            if has_bias:
                v = v + bias_ref[...]
            if has_acc:
                v = v + acc_ref[...]
            return v

        if nk == 1:
            o_ref[...] = extras(p).astype(out_dtype)
        else:
            k = pl.program_id(2)

            @pl.when(k == 0)
            def _():
                s_ref[...] = extras(p)

            @pl.when(k > 0)
            def _():
                s_ref[...] += p

            @pl.when(k == nk - 1)
            def _():
                o_ref[...] = s_ref[...].astype(out_dtype)

    lead_a = () if a_idx is None else (a_idx,)
    lead_b = () if layer is None else (layer,)
    sq = lambda lead: (None,) * len(lead)
    if ta:
        a_spec = pl.BlockSpec((*sq(lead_a), tk, tm), lambda i, j, k: (*lead_a, k, i))
    else:
        a_spec = pl.BlockSpec((*sq(lead_a), tm, tk), lambda i, j, k: (*lead_a, i, k))
    if tb:
        b_spec = pl.BlockSpec((*sq(lead_b), tn, tk), lambda i, j, k: (*lead_b, j, k + kb))
    else:
        b_spec = pl.BlockSpec((*sq(lead_b), tk, tn), lambda i, j, k: (*lead_b, k + kb, j))
    in_specs, args = [a_spec, b_spec], [a, b]
    if has_bias:
        in_specs.append(pl.BlockSpec((1, tn), lambda i, j, k: (0, j)))
        args.append(bias.reshape(1, N).astype(F32))
    if has_acc:
        in_specs.append(pl.BlockSpec((tm, tn), lambda i, j, k: (i, j)))
        args.append(acc)
    if into is None:
        out_spec = pl.BlockSpec((tm, tn), lambda i, j, k: (i, j))
        out_shape = jax.ShapeDtypeStruct((M, N), out_dtype)
        aliases = {}
    else:
        buf, l, col0 = into
        cb = col0 // tn
        in_specs.append(pl.BlockSpec(memory_space=pl.ANY))
        args.append(buf)
        out_spec = pl.BlockSpec((None, tm, tn), lambda i, j, k: (l, i, j + cb))
        out_shape = jax.ShapeDtypeStruct(buf.shape, buf.dtype)
        aliases = {len(args) - 1: 0}
    return pl.pallas_call(
        body, name=name, grid=(M // tm, N // tn, nk),
        in_specs=in_specs, out_specs=out_spec, out_shape=out_shape, input_output_aliases=aliases,
        scratch_shapes=[pltpu.VMEM((tm, tn), F32)] if nk > 1 else [],
        compiler_params=_cp("parallel", "parallel", "arbitrary"),
    )(*args)


def colsum(x, name):
    L, C = x.shape
    tr = _tile(L, 512)
    tc = _tile(C, 1024)

    def body(x_ref, o_ref):
        @pl.when(pl.program_id(1) == 0)
        def _():
            o_ref[...] = jnp.zeros_like(o_ref)

        o_ref[...] += jnp.sum(x_ref[...].astype(F32), axis=0, keepdims=True)

    return pl.pallas_call(
        body, name=name, grid=(C // tc, L // tr),
        in_specs=[pl.BlockSpec((tr, tc), lambda j, i: (i, j))],
        out_specs=pl.BlockSpec((1, tc), lambda j, i: (0, j)),
        out_shape=jax.ShapeDtypeStruct((1, C), F32),
        compiler_params=_cp("parallel", "arbitrary"),
    )(x)


TR = 256


def _row_spec(tr, w):
    return pl.BlockSpec((tr, w), lambda i: (i, 0))


def _vec_spec(w):
    return pl.BlockSpec((1, w), lambda i: (0, 0))


def _rms(v):
    return lax.rsqrt(jnp.mean(v * v, axis=-1, keepdims=True) + EPS)


def norm_fwd(x, g, name):
    L, D = x.shape
    tr = min(TR, L)

    def body(x_ref, g_ref, h_ref):
        xv = x_ref[...]
        h_ref[...] = (xv * _rms(xv) * g_ref[...]).astype(BF16)

    return pl.pallas_call(
        body, name=name, grid=(L // tr,),
        in_specs=[_row_spec(tr, D), _vec_spec(D)], out_specs=_row_spec(tr, D),
        out_shape=jax.ShapeDtypeStruct((L, D), BF16), compiler_params=_cp("parallel"),
    )(x, g.reshape(1, D))


def bnd_fwd(x, f, gpost, gpre, name):
    L, D = x.shape
    tr = min(TR, L)

    def body(x_ref, f_ref, gp_ref, gn_ref, xo_ref, h_ref):
        fv = f_ref[...]
        xn = x_ref[...] + fv * _rms(fv) * gp_ref[...]
        xo_ref[...] = xn
        h_ref[...] = (xn * _rms(xn) * gn_ref[...]).astype(BF16)

    return pl.pallas_call(
        body, name=name, grid=(L // tr,),
        in_specs=[_row_spec(tr, D), _row_spec(tr, D), _vec_spec(D), _vec_spec(D)],
        out_specs=[_row_spec(tr, D), _row_spec(tr, D)],
        out_shape=[jax.ShapeDtypeStruct((L, D), F32), jax.ShapeDtypeStruct((L, D), BF16)],
        compiler_params=_cp("parallel"),
    )(x, f, gpost.reshape(1, D), gpre.reshape(1, D))


def final_fwd(x, f, gpost, target, name):
    L, D = x.shape
    tr = min(TR, L)
    n = L // tr

    def body(x_ref, f_ref, gp_ref, t_ref, dy_ref, loss_ref, acc_ref):
        i = pl.program_id(0)

        @pl.when(i == 0)
        def _():
            acc_ref[...] = jnp.zeros_like(acc_ref)

        fv = f_ref[...]
        e = x_ref[...] + fv * _rms(fv) * gp_ref[...] - t_ref[...]
        dy_ref[...] = e * (1.0 / D)
        acc_ref[...] += jnp.sum(e * e, axis=0, keepdims=True)

        @pl.when(i == n - 1)
        def _():
            loss_ref[...] = jnp.full((1, LANE), 0.5 / D, F32) * jnp.sum(acc_ref[...])

    return pl.pallas_call(
        body, name=name, grid=(n,),
        in_specs=[_row_spec(tr, D), _row_spec(tr, D), _vec_spec(D), _row_spec(tr, D)],
        out_specs=[_row_spec(tr, D), _vec_spec(LANE)],
        out_shape=[jax.ShapeDtypeStruct((L, D), F32), jax.ShapeDtypeStruct((1, LANE), F32)],
        scratch_shapes=[pltpu.VMEM((1, D), F32)],
        compiler_params=_cp("arbitrary"),
    )(x, f, gpost.reshape(1, D), target)


def _rms_bwd(v, g, dy):
    r = _rms(v)
    vn = v * r
    dg = jnp.sum(dy * vn, axis=0, keepdims=True)
    dvn = dy * g
    dv = r * (dvn - vn * jnp.mean(dvn * vn, axis=-1, keepdims=True))
    return dv, dg


def bnd_bwd(dxp, *, pre=None, post=None, name):
    L, D = dxp.shape
    tr = min(TR, L)
    has_pre, has_post = pre is not None, post is not None

    def body(*refs):
        pos = 0
        dxp_ref = refs[pos]; pos += 1
        if has_pre:
            x_ref, gpre_ref, dh_ref = refs[pos:pos + 3]; pos += 3
        if has_post:
            f_ref, gpost_ref = refs[pos:pos + 2]; pos += 2
        if has_pre:
            dx_ref, dgpre_ref = refs[pos:pos + 2]; pos += 2
        if has_post:
            df_ref, dgpost_ref = refs[pos:pos + 2]; pos += 2
        i = pl.program_id(0)
        dx = dxp_ref[...]
        if has_pre:
            d, dg = _rms_bwd(x_ref[...], gpre_ref[...], dh_ref[...])
            dx = dx + d
            dx_ref[...] = dx

            @pl.when(i == 0)
            def _():
                dgpre_ref[...] = jnp.zeros_like(dgpre_ref)

            dgpre_ref[...] += dg
        if has_post:
            d, dg = _rms_bwd(f_ref[...], gpost_ref[...], dx)
            df_ref[...] = d.astype(BF16)

            @pl.when(i == 0)
            def _():
                dgpost_ref[...] = jnp.zeros_like(dgpost_ref)

            dgpost_ref[...] += dg

    in_specs, args = [_row_spec(tr, D)], [dxp]
    out_specs, out_shape, names = [], [], []
    if has_pre:
        x, gpre, dh = pre
        in_specs += [_row_spec(tr, D), _vec_spec(D), _row_spec(tr, D)]
        args += [x, gpre.reshape(1, D), dh]
        out_specs += [_row_spec(tr, D), _vec_spec(D)]
        out_shape += [jax.ShapeDtypeStruct((L, D), F32), jax.ShapeDtypeStruct((1, D), F32)]
        names += ["dx", "dgpre"]
    if has_post:
        f, gpost = post
        in_specs += [_row_spec(tr, D), _vec_spec(D)]
        args += [f, gpost.reshape(1, D)]
        out_specs += [_row_spec(tr, D), _vec_spec(D)]
        out_shape += [jax.ShapeDtypeStruct((L, D), BF16), jax.ShapeDtypeStruct((1, D), F32)]
        names += ["df", "dgpost"]
    outs = pl.pallas_call(
        body, name=name, grid=(L // tr,), in_specs=in_specs, out_specs=out_specs, out_shape=out_shape,
        compiler_params=_cp("arbitrary"),
    )(*args)
    return dict(zip(names, outs))


def norm_dg(x, dy, name):
    L, D = x.shape
    tr = min(TR, L)

    def body(x_ref, dy_ref, o_ref):
        @pl.when(pl.program_id(0) == 0)
        def _():
            o_ref[...] = jnp.zeros_like(o_ref)

        xv = x_ref[...]
        o_ref[...] += jnp.sum(dy_ref[...] * xv * _rms(xv), axis=0, keepdims=True)

    return pl.pallas_call(
        body, name=name, grid=(L // tr,),
        in_specs=[_row_spec(tr, D), _row_spec(tr, D)], out_specs=_vec_spec(D),
        out_shape=jax.ShapeDtypeStruct((1, D), F32), compiler_params=_cp("arbitrary"),
    )(x, dy)


HALO = 32


def _prev_halo_spec(tr, tc, col):
    per = tr // HALO
    return pl.BlockSpec((HALO, tc), lambda *g: (jnp.maximum(g[-1] * per - 1, 0), col(*g)))


def _fill_prev(scr, halo_val, blk_val, i, tr):
    scr[pl.ds(0, HALO), :] = jnp.where(i == 0, 0.0, halo_val)
    scr[pl.ds(HALO, tr), :] = blk_val


def _conv(scr, w_ref, K, tr):
    acc = None
    for k in range(K):
        term = scr[pl.ds(HALO - (K - 1) + k, tr), :] * w_ref[k:k + 1, :]
        acc = term if acc is None else acc + term
    return acc


def _conv_dw(scr, d, o_ref, K, tr):
    for k in range(K):
        o_ref[k:k + 1, :] += jnp.sum(d * scr[pl.ds(HALO - (K - 1) + k, tr), :], axis=0, keepdims=True)


def ssm_conv_fwd(zx, w, b, *, col0, ncols, wcol0, out_dtype, name):
    L = zx.shape[0]
    tr = min(TR, L)
    tc = 1024
    cb, wb = col0 // tc, wcol0 // tc

    def body(x_ref, h_ref, w_ref, b_ref, o_ref, p_ref, scr):
        i = pl.program_id(1)
        _fill_prev(scr, h_ref[...].astype(F32), x_ref[...].astype(F32), i, tr)
        pre = _conv(scr, w_ref, SSM_K, tr) + b_ref[...]
        p_ref[...] = pre.astype(BF16)
        o_ref[...] = (pre * _sig(pre)).astype(out_dtype)

    out = pl.BlockSpec((tr, tc), lambda j, i: (i, j))
    return pl.pallas_call(
        body, name=name, grid=(ncols // tc, L // tr),
        in_specs=[pl.BlockSpec((tr, tc), lambda j, i: (i, j + cb)),
                  _prev_halo_spec(tr, tc, lambda j, i: j + cb),
                  pl.BlockSpec((SSM_K, tc), lambda j, i: (0, j + wb)),
                  pl.BlockSpec((1, tc), lambda j, i: (0, j + wb))],
        out_specs=[out, out],
        out_shape=[jax.ShapeDtypeStruct((L, ncols), out_dtype), jax.ShapeDtypeStruct((L, ncols), BF16)],
        scratch_shapes=[pltpu.VMEM((HALO + tr, tc), F32)],
        compiler_params=_cp("parallel", "parallel"),
    )(zx, zx, w, b)


def ssm_conv_bwd(zx, pre, d, w, *, col0, dcol0, ncols, name):
    L = zx.shape[0]
    tr = min(TR, L)
    tc = 1024
    cb, db_ = col0 // tc, dcol0 // tc
    n = L // tr
    per = tr // HALO
    last = L // HALO - 1

    def body(x_ref, h_ref, p_ref, np_ref, d_ref, nd_ref, w_ref, dx_ref, dw_ref, db_ref, scr, sd):
        i = pl.program_id(1)
        _fill_prev(scr, h_ref[...].astype(F32), x_ref[...].astype(F32), i, tr)

        def dpre(p, dv):
            s = _sig(p)
            return dv * s * (1.0 + p * (1.0 - s))

        dp = dpre(p_ref[...].astype(F32), d_ref[...])
        sd[pl.ds(0, tr), :] = dp
        sd[pl.ds(tr, HALO), :] = jnp.where(i == n - 1, 0.0, dpre(np_ref[...].astype(F32), nd_ref[...]))
        acc = None
        for k in range(SSM_K):
            term = sd[pl.ds(SSM_K - 1 - k, tr), :] * w_ref[k:k + 1, :]
            acc = term if acc is None else acc + term
        dx_ref[...] = acc.astype(BF16)

        @pl.when(i == 0)
        def _():
            dw_ref[...] = jnp.zeros_like(dw_ref)
            db_ref[...] = jnp.zeros_like(db_ref)

        _conv_dw(scr, dp, dw_ref, SSM_K, tr)
        db_ref[...] += jnp.sum(dp, axis=0, keepdims=True)

    nxt = lambda i: jnp.minimum((i + 1) * per, last)
    return pl.pallas_call(
        body, name=name, grid=(ncols // tc, n),
        in_specs=[pl.BlockSpec((tr, tc), lambda j, i: (i, j + cb)),
                  _prev_halo_spec(tr, tc, lambda j, i: j + cb),
                  pl.BlockSpec((tr, tc), lambda j, i: (i, j)),
                  pl.BlockSpec((HALO, tc), lambda j, i: (nxt(i), j)),
                  pl.BlockSpec((tr, tc), lambda j, i: (i, j + db_)),
                  pl.BlockSpec((HALO, tc), lambda j, i: (nxt(i), j + db_)),
                  pl.BlockSpec((SSM_K, tc), lambda j, i: (0, j + db_))],
        out_specs=[pl.BlockSpec((tr, tc), lambda j, i: (i, j)),
                   pl.BlockSpec((SSM_K, tc), lambda j, i: (0, j)),
                   pl.BlockSpec((1, tc), lambda j, i: (0, j))],
        out_shape=[jax.ShapeDtypeStruct((L, ncols), BF16), jax.ShapeDtypeStruct((SSM_K, ncols), F32),
                   jax.ShapeDtypeStruct((1, ncols), F32)],
        scratch_shapes=[pltpu.VMEM((HALO + tr, tc), F32), pltpu.VMEM((tr + HALO, tc), F32)],
        compiler_params=_cp("parallel", "arbitrary"),
    )(zx, zx, pre, pre, d, d, w)


FFN_TC = 1408


def ffn_act_fwd(u, w, b, name):
    L = u.shape[0]
    tr = min(TR, L)
    tc = FFN_TC
    nb = D_FF // tc

    def body(g_ref, hg_ref, v_ref, hv_ref, wg_ref, wv_ref, bg_ref, bv_ref, o_ref, c_ref, sg, sv):
        i = pl.program_id(1)
        _fill_prev(sg, hg_ref[...].astype(F32), g_ref[...].astype(F32), i, tr)
        _fill_prev(sv, hv_ref[...].astype(F32), v_ref[...].astype(F32), i, tr)
        ug = _conv(sg, wg_ref, FFN_K, tr) + bg_ref[...]
        uv = _conv(sv, wv_ref, FFN_K, tr) + bv_ref[...]
        c_ref[0] = ug.astype(BF16)
        c_ref[1] = uv.astype(BF16)
        o_ref[...] = (ug * _sig(ug) * uv).astype(BF16)

    blk = lambda off: pl.BlockSpec((tr, tc), lambda j, i: (i, j + off))
    wsp = lambda off: pl.BlockSpec((FFN_K, tc), lambda j, i: (0, j + off))
    bsp = lambda off: pl.BlockSpec((1, tc), lambda j, i: (0, j + off))
    return pl.pallas_call(
        body, name=name, grid=(nb, L // tr),
        in_specs=[blk(0), _prev_halo_spec(tr, tc, lambda j, i: j),
                  blk(nb), _prev_halo_spec(tr, tc, lambda j, i: j + nb),
                  wsp(0), wsp(nb), bsp(0), bsp(nb)],
        out_specs=[pl.BlockSpec((tr, tc), lambda j, i: (i, j)), pl.BlockSpec((2, tr, tc), lambda j, i: (0, i, j))],
        out_shape=[jax.ShapeDtypeStruct((L, D_FF), BF16), jax.ShapeDtypeStruct((2, L, D_FF), BF16)],
        scratch_shapes=[pltpu.VMEM((HALO + tr, tc), F32), pltpu.VMEM((HALO + tr, tc), F32)],
        compiler_params=_cp("parallel", "parallel"),
    )(u, u, u, u, w, w, b, b)


def ffn_act_bwd(u, c, dact, w, name):
    L = u.shape[0]
    tr = min(TR, L)
    tc = FFN_TC
    nb = D_FF // tc
    n = L // tr
    per = tr // HALO
    last = L // HALO - 1

    def body(g_ref, hg_ref, v_ref, hv_ref, c_ref, nc_ref, da_ref, nda_ref, wg_ref, wv_ref,
             du_ref, dw_ref, db_ref, sg, sv, dg_s, dv_s):
        i = pl.program_id(1)
        _fill_prev(sg, hg_ref[...].astype(F32), g_ref[...].astype(F32), i, tr)
        _fill_prev(sv, hv_ref[...].astype(F32), v_ref[...].astype(F32), i, tr)

        def grads(cg, cv, da):
            s = _sig(cg)
            return da * cv * s * (1.0 + cg * (1.0 - s)), da * cg * s

        dg, dv = grads(c_ref[0].astype(F32), c_ref[1].astype(F32), da_ref[...].astype(F32))
        ndg, ndv = grads(nc_ref[0].astype(F32), nc_ref[1].astype(F32), nda_ref[...].astype(F32))
        at_end = i == n - 1
        for half, (scr, d, nd, x_scr, w_ref) in enumerate(((dg_s, dg, ndg, sg, wg_ref), (dv_s, dv, ndv, sv, wv_ref))):
            scr[pl.ds(0, tr), :] = d
            scr[pl.ds(tr, HALO), :] = jnp.where(at_end, 0.0, nd)
            acc = None
            for k in range(FFN_K):
                term = scr[pl.ds(FFN_K - 1 - k, tr), :] * w_ref[k:k + 1, :]
                acc = term if acc is None else acc + term
            du_ref[half] = acc.astype(BF16)

            @pl.when(i == 0)
            def _():
                dw_ref[half] = jnp.zeros((FFN_K, tc), F32)
                db_ref[half] = jnp.zeros((1, tc), F32)

            for k in range(FFN_K):
                dw_ref[half, k:k + 1, :] += jnp.sum(d * x_scr[pl.ds(HALO - (FFN_K - 1) + k, tr), :], axis=0, keepdims=True)
            db_ref[half] += jnp.sum(d, axis=0, keepdims=True)

    blk = lambda off: pl.BlockSpec((tr, tc), lambda j, i: (i, j + off))
    wsp = lambda off: pl.BlockSpec((FFN_K, tc), lambda j, i: (0, j + off))
    nxt = lambda i: jnp.minimum((i + 1) * per, last)
    return pl.pallas_call(
        body, name=name, grid=(nb, n),
        in_specs=[blk(0), _prev_halo_spec(tr, tc, lambda j, i: j),
                  blk(nb), _prev_halo_spec(tr, tc, lambda j, i: j + nb),
                  pl.BlockSpec((2, tr, tc), lambda j, i: (0, i, j)),
                  pl.BlockSpec((2, HALO, tc), lambda j, i: (0, nxt(i), j)),
                  pl.BlockSpec((tr, tc), lambda j, i: (i, j)),
                  pl.BlockSpec((HALO, tc), lambda j, i: (nxt(i), j)),
                  wsp(0), wsp(nb)],
        out_specs=[pl.BlockSpec((2, tr, tc), lambda j, i: (0, i, j)),
                   pl.BlockSpec((2, FFN_K, tc), lambda j, i: (0, 0, j)),
                   pl.BlockSpec((2, 1, tc), lambda j, i: (0, 0, j))],
        out_shape=[jax.ShapeDtypeStruct((2, L, D_FF), BF16), jax.ShapeDtypeStruct((2, FFN_K, D_FF), F32),
                   jax.ShapeDtypeStruct((2, 1, D_FF), F32)],
        scratch_shapes=[pltpu.VMEM((HALO + tr, tc), F32), pltpu.VMEM((HALO + tr, tc), F32),
                        pltpu.VMEM((tr + HALO, tc), F32), pltpu.VMEM((tr + HALO, tc), F32)],
        compiler_params=_cp("parallel", "arbitrary"),
    )(u, u, u, u, c, c, dact, dact, w, w)


def _ln_stats(c):
    mu = jnp.mean(c, axis=-1, keepdims=True)
    cc = c - mu
    rstd = lax.rsqrt(jnp.mean(cc * cc, axis=-1, keepdims=True) + EPS)
    return cc * rstd, rstd


def cf_fwd(u, dw_w, dw_b, ln_g, ln_b, name):
    L = u.shape[0]
    D = D_MODEL
    tr = min(TR, L)

    def body(a_ref, ha_ref, g_ref, hg_ref, w_ref, b_ref, lg_ref, lb_ref, c_ref, s_ref, scr):
        i = pl.program_id(0)
        glu_h = ha_ref[...].astype(F32) * _sig(hg_ref[...].astype(F32))
        glu = a_ref[...].astype(F32) * _sig(g_ref[...].astype(F32))
        _fill_prev(scr, glu_h, glu, i, tr)
        c = _conv(scr, w_ref, CF_K, tr) + b_ref[...]
        c_ref[...] = c
        xhat, _ = _ln_stats(c)
        ln = xhat * lg_ref[...] + lb_ref[...]
        s_ref[...] = (ln * _sig(ln)).astype(BF16)

    per = tr // HALO
    halo = lambda col: pl.BlockSpec((HALO, D), lambda i: (jnp.maximum(i * per - 1, 0), col))
    return pl.pallas_call(
        body, name=name, grid=(L // tr,),
        in_specs=[pl.BlockSpec((tr, D), lambda i: (i, 0)), halo(0),
                  pl.BlockSpec((tr, D), lambda i: (i, 1)), halo(1),
                  pl.BlockSpec((CF_K, D), lambda i: (0, 0)), _vec_spec(D), _vec_spec(D), _vec_spec(D)],
        out_specs=[_row_spec(tr, D), _row_spec(tr, D)],
        out_shape=[jax.ShapeDtypeStruct((L, D), F32), jax.ShapeDtypeStruct((L, D), BF16)],
        scratch_shapes=[pltpu.VMEM((HALO + tr, D), F32)],
        compiler_params=_cp("parallel"),
    )(u, u, u, u, dw_w, dw_b, ln_g, ln_b)


def cf_bwd_ln(c, ds, ln_g, ln_b, name):
    L, D = c.shape
    tr = min(TR, L)

    def body(c_ref, ds_ref, lg_ref, lb_ref, dc_ref, dg_ref, db_ref):
        xhat, rstd = _ln_stats(c_ref[...])
        ln = xhat * lg_ref[...] + lb_ref[...]
        sg = _sig(ln)
        dln = ds_ref[...].astype(F32) * sg * (1.0 + ln * (1.0 - sg))

        @pl.when(pl.program_id(0) == 0)
        def _():
            dg_ref[...] = jnp.zeros_like(dg_ref)
            db_ref[...] = jnp.zeros_like(db_ref)

        dg_ref[...] += jnp.sum(dln * xhat, axis=0, keepdims=True)
        db_ref[...] += jnp.sum(dln, axis=0, keepdims=True)
        dxh = dln * lg_ref[...]
        dc_ref[...] = rstd * (dxh - jnp.mean(dxh, axis=-1, keepdims=True)
                              - xhat * jnp.mean(dxh * xhat, axis=-1, keepdims=True))

    return pl.pallas_call(
        body, name=name, grid=(L // tr,),
        in_specs=[_row_spec(tr, D), _row_spec(tr, D), _vec_spec(D), _vec_spec(D)],
        out_specs=[_row_spec(tr, D), _vec_spec(D), _vec_spec(D)],
        out_shape=[jax.ShapeDtypeStruct((L, D), F32), jax.ShapeDtypeStruct((1, D), F32),
                   jax.ShapeDtypeStruct((1, D), F32)],
        compiler_params=_cp("arbitrary"),
    )(c, ds, ln_g, ln_b)


def cf_bwd_conv(u, dc, dw_w, name):
    L = u.shape[0]
    D = D_MODEL
    tr = min(TR, L)
    n = L // tr

    def body(a_ref, ha_ref, g_ref, hg_ref, dc_ref, nx_ref, w_ref, du_ref, dw_ref, db_ref, sx, sd):
        i = pl.program_id(0)
        a = a_ref[...].astype(F32)
        sg = _sig(g_ref[...].astype(F32))
        _fill_prev(sx, ha_ref[...].astype(F32) * _sig(hg_ref[...].astype(F32)), a * sg, i, tr)
        dcv = dc_ref[...]
        sd[pl.ds(0, tr), :] = dcv
        sd[pl.ds(tr, HALO), :] = jnp.where(i == n - 1, 0.0, nx_ref[...])
        dglu = None
        for k in range(CF_K):
            term = sd[pl.ds(CF_K - 1 - k, tr), :] * w_ref[k:k + 1, :]
            dglu = term if dglu is None else dglu + term
        du_ref[:, 0:D] = (dglu * sg).astype(BF16)
        du_ref[:, D:2 * D] = (dglu * a * sg * (1.0 - sg)).astype(BF16)

        @pl.when(i == 0)
        def _():
            dw_ref[...] = jnp.zeros_like(dw_ref)
            db_ref[...] = jnp.zeros_like(db_ref)

        _conv_dw(sx, dcv, dw_ref, CF_K, tr)
        db_ref[...] += jnp.sum(dcv, axis=0, keepdims=True)

    per = tr // HALO
    last = L // HALO - 1
    halo = lambda col: pl.BlockSpec((HALO, D), lambda i: (jnp.maximum(i * per - 1, 0), col))
    return pl.pallas_call(
        body, name=name, grid=(n,),
        in_specs=[pl.BlockSpec((tr, D), lambda i: (i, 0)), halo(0),
                  pl.BlockSpec((tr, D), lambda i: (i, 1)), halo(1),
                  _row_spec(tr, D),
                  pl.BlockSpec((HALO, D), lambda i: (jnp.minimum((i + 1) * per, last), 0)),
                  pl.BlockSpec((CF_K, D), lambda i: (0, 0))],
        out_specs=[pl.BlockSpec((tr, 2 * D), lambda i: (i, 0)),
                   pl.BlockSpec((CF_K, D), lambda i: (0, 0)), _vec_spec(D)],
        out_shape=[jax.ShapeDtypeStruct((L, 2 * D), BF16), jax.ShapeDtypeStruct((CF_K, D), F32),
                   jax.ShapeDtypeStruct((1, D), F32)],
        scratch_shapes=[pltpu.VMEM((HALO + tr, D), F32), pltpu.VMEM((tr + HALO, D), F32)],
        compiler_params=_cp("arbitrary"),
    )(u, u, u, u, dc, dc, dw_w)


XA_TR = 512
XA_SCALE = XA_HD ** -0.5


def _xa_probs(qh, kh):
    s = _dot(qh, kh, 1, 1) * XA_SCALE
    p = jnp.exp(s - jnp.max(s, axis=-1, keepdims=True))
    return p / jnp.sum(p, axis=-1, keepdims=True)


def attn_fwd(q, kv, name):
    L, D = q.shape
    tr = min(XA_TR, L)

    def body(q_ref, kv_ref, o_ref):
        for hd in range(XA_HEADS):
            c = slice(hd * XA_HD, (hd + 1) * XA_HD)
            p = _xa_probs(q_ref[:, c], kv_ref[:, c])
            vh = kv_ref[:, D + hd * XA_HD:D + (hd + 1) * XA_HD]
            o_ref[:, c] = _dot(p.astype(BF16), vh).astype(BF16)

    return pl.pallas_call(
        body, name=name, grid=(L // tr,),
        in_specs=[_row_spec(tr, D), pl.BlockSpec((N_MEM, 2 * D), lambda i: (0, 0))],
        out_specs=_row_spec(tr, D), out_shape=jax.ShapeDtypeStruct((L, D), BF16),
        compiler_params=_cp("parallel"),
    )(q, kv)


def attn_bwd(q, kv, do, name):
    L, D = q.shape
    tr = min(XA_TR, L)

    def body(q_ref, kv_ref, do_ref, dq_ref, dkv_ref):
        @pl.when(pl.program_id(0) == 0)
        def _():
            dkv_ref[...] = jnp.zeros_like(dkv_ref)

        for hd in range(XA_HEADS):
            c = slice(hd * XA_HD, (hd + 1) * XA_HD)
            cv = slice(D + hd * XA_HD, D + (hd + 1) * XA_HD)
            qh, kh, vh, doh = q_ref[:, c], kv_ref[:, c], kv_ref[:, cv], do_ref[:, c]
            p = _xa_probs(qh, kh)
            dp = _dot(doh, vh, 1, 1)
            dkv_ref[:, cv] += _dot(p.astype(BF16), doh, 0, 0)
            ds = (p * (dp - jnp.sum(dp * p, axis=-1, keepdims=True)) * XA_SCALE).astype(BF16)
            dq_ref[:, c] = _dot(ds, kh).astype(BF16)
            dkv_ref[:, c] += _dot(ds, qh, 0, 0)

    return pl.pallas_call(
        body, name=name, grid=(L // tr,),
        in_specs=[_row_spec(tr, D), pl.BlockSpec((N_MEM, 2 * D), lambda i: (0, 0)), _row_spec(tr, D)],
        out_specs=[_row_spec(tr, D), pl.BlockSpec((N_MEM, 2 * D), lambda i: (0, 0))],
        out_shape=[jax.ShapeDtypeStruct((L, D), BF16), jax.ShapeDtypeStruct((N_MEM, 2 * D), F32)],
        compiler_params=_cp("arbitrary"),
    )(q, kv, do)


N_PAIRS = N_HEADS // 2
PAIRS_PER_GROUP = N_PAIRS // N_GROUPS
GN = N_GROUPS * D_STATE


def _softplus(x):
    t = jnp.exp(-jnp.abs(x))
    return jnp.maximum(x, 0.0) + jnp.where(t < 1e-4, t * (1.0 - 0.5 * t), jnp.log(1.0 + t))


def _dot3b(m, v, ca=1, cb=0):
    v0, v1, v2 = _split3(v)
    return _dot(m, v0, ca, cb) + _dot(m, v1, ca, cb) + _dot(m, v2, ca, cb)


def ssd_consts():
    h = lax.broadcasted_iota(jnp.int32, (LANE, D_INNER), 0)
    c = lax.broadcasted_iota(jnp.int32, (LANE, D_INNER), 1)
    expand = (c // HEAD_DIM == h).astype(BF16)
    r = lax.broadcasted_iota(jnp.int32, (CHUNK, CHUNK), 0)
    k = lax.broadcasted_iota(jnp.int32, (CHUNK, CHUNK), 1)
    tri = (k <= r).astype(BF16)
    return expand, tri


def _ssd_common(dtr_ref, prm_ref, e_ref, tri_ref):
    lane = lax.broadcasted_iota(jnp.int32, (CHUNK, LANE), 1)
    valid = lane < N_HEADS
    A = -jnp.exp(prm_ref[1:2, :])
    pre = dtr_ref[...] + prm_ref[0:1, :]
    dt = jnp.where(valid, _softplus(pre), 0.0)
    cs = _dot3b(tri_ref[...], dt * A)
    E = e_ref[...]
    dt_x = _dot3(dt, E)
    cs_x = _dot3(cs, E)
    csl_x = cs_x[CHUNK - 1:CHUNK, :]
    return dict(valid=valid, A=A, pre=pre, dt=dt, cs=cs, csT=cs.T, dt_x=dt_x, ecs_x=jnp.exp(cs_x),
                dend_x=jnp.exp(csl_x - cs_x), cd_x=jnp.exp(csl_x), D_x=_dot3(prm_ref[...], E)[2:3, :])


def ssd_fwd(xs, bc, dtr, zx, prm, ng, name):
    L = xs.shape[0]
    nc = L // CHUNK
    expand, tri = ssd_consts()

    def body(xs_ref, bc_ref, dtr_ref, z_ref, prm_ref, ng_ref, e_ref, tri_ref, y_ref, yn_ref, st_ref, state):
        @pl.when(pl.program_id(0) == 0)
        def _():
            state[...] = jnp.zeros_like(state)

        q = _ssd_common(dtr_ref, prm_ref, e_ref, tri_ref)
        cs, csT = q["cs"], q["csT"]
        xs_v = xs_ref[...]
        X = xs_v * q["dt_x"]
        Xb = X.astype(BF16)
        Xd = (X * q["dend_x"]).astype(BF16)
        ii = lax.broadcasted_iota(jnp.int32, (CHUNK, CHUNK), 0)
        jj = lax.broadcasted_iota(jnp.int32, (CHUNK, CHUNK), 1)
        tril = jj <= ii
        first = jj < HEAD_DIM
        for g in range(N_GROUPS):
            Bg = bc_ref[:, g * D_STATE:(g + 1) * D_STATE]
            Cg = bc_ref[:, GN + g * D_STATE:GN + (g + 1) * D_STATE]
            S = _dot(Cg, Bg, 1, 1)
            for pr in range(PAIRS_PER_GROUP):
                pair = g * PAIRS_PER_GROUP + pr
                cols = slice(pair * LANE, (pair + 1) * LANE)
                Xp = Xb[:, cols]
                ys = []
                for h in (2 * pair, 2 * pair + 1):
                    seg = cs[:, h:h + 1] - csT[h:h + 1, :]
                    M = (S * jnp.exp(jnp.where(tril, seg, NEG))).astype(BF16)
                    ys.append(_dot(M, Xp))
                prevT = state[pair]
                st_ref[0, pair] = prevT
                yoff = _dot(Cg, prevT.astype(BF16)) * q["ecs_x"][:, cols]
                y_ref[:, cols] = jnp.where(first, ys[0], ys[1]) + yoff + xs_v[:, cols] * q["D_x"][:, cols]
                state[pair] = prevT * q["cd_x"][:, cols] + _dot(Bg, Xd[:, cols], 0, 0)
        z = z_ref[...].astype(F32)
        gt = y_ref[...] * z * _sig(z)
        yn_ref[...] = (gt * _rms(gt) * ng_ref[...]).astype(BF16)

    row = lambda w: pl.BlockSpec((CHUNK, w), lambda c: (c, 0))
    const = lambda a: pl.BlockSpec(a.shape, lambda c: (0,) * a.ndim)
    return pl.pallas_call(
        body, name=name, grid=(nc,),
        in_specs=[row(D_INNER), row(2 * GN), row(LANE), row(D_INNER), const(prm), const(ng), const(expand), const(tri)],
        out_specs=[row(D_INNER), row(D_INNER), pl.BlockSpec((1, N_PAIRS, D_STATE, LANE), lambda c: (c, 0, 0, 0))],
        out_shape=[jax.ShapeDtypeStruct((L, D_INNER), F32), jax.ShapeDtypeStruct((L, D_INNER), BF16),
                   jax.ShapeDtypeStruct((nc, N_PAIRS, D_STATE, LANE), F32)],
        scratch_shapes=[pltpu.VMEM((N_PAIRS, D_STATE, LANE), F32)],
        compiler_params=_cp("arbitrary"),
    )(xs, bc, dtr, zx, prm, ng, expand, tri)


def ssd_bwd(dyn, y, zx, xs, bc, dtr, st, prm, ng, name):
    L = xs.shape[0]
    nc = L // CHUNK
    expand, tri = ssd_consts()

    def body(dyn_ref, y_ref, z_ref, xs_ref, bc_ref, dtr_ref, st_ref, prm_ref, ng_ref, e_ref, tri_ref,
             dxbc_ref, dz_ref, ddtr_ref, dng_ref, dprm_ref, dstate, g_cs, g_q, dX, g_row):
        step = pl.program_id(0)

        @pl.when(step == 0)
        def _():
            dstate[...] = jnp.zeros_like(dstate)
            dng_ref[...] = jnp.zeros_like(dng_ref)
            dprm_ref[...] = jnp.zeros_like(dprm_ref)
            g_row[...] = jnp.zeros_like(g_row)

        q = _ssd_common(dtr_ref, prm_ref, e_ref, tri_ref)
        cs, csT, E = q["cs"], q["csT"], e_ref[...]
        xs_v = xs_ref[...]
        X = xs_v * q["dt_x"]
        Xb = X.astype(BF16)
        Xd_f = X * q["dend_x"]
        Xd = Xd_f.astype(BF16)

        yv = y_ref[...]
        z = z_ref[...].astype(F32)
        sz = _sig(z)
        silu = z * sz
        gt = yv * silu
        r = _rms(gt)
        gn = gt * r
        dyn_v = dyn_ref[...]
        dng_ref[...] += jnp.sum(dyn_v * gn, axis=0, keepdims=True)
        dgn = dyn_v * ng_ref[...]
        dgt = r * (dgn - gn * jnp.mean(dgn * gn, axis=-1, keepdims=True))
        dY = dgt * silu
        dz_ref[...] = (dgt * yv * sz * (1.0 + z * (1.0 - sz))).astype(BF16)
        dYb = dY.astype(BF16)
        g_row[1:2, :] += jnp.sum(dY * xs_v, axis=0, keepdims=True)

        ii = lax.broadcasted_iota(jnp.int32, (CHUNK, CHUNK), 0)
        jj = lax.broadcasted_iota(jnp.int32, (CHUNK, CHUNK), 1)
        tril = jj <= ii
        triu = jj >= ii
        first = jj < HEAD_DIM
        lane_row = lax.broadcasted_iota(jnp.int32, (1, LANE), 1)
        sub_col = lax.broadcasted_iota(jnp.int32, (CHUNK, 1), 0)
        dcs_col = jnp.zeros((CHUNK, LANE), F32)
        dcs_rowT = jnp.zeros((LANE, CHUNK), F32)
        for g in range(N_GROUPS):
            Bg = bc_ref[:, g * D_STATE:(g + 1) * D_STATE]
            Cg = bc_ref[:, GN + g * D_STATE:GN + (g + 1) * D_STATE]
            S = _dot(Cg, Bg, 1, 1)
            ST = _dot(Bg, Cg, 1, 1)
            dS = jnp.zeros((CHUNK, CHUNK), F32)
            dCg = jnp.zeros((CHUNK, D_STATE), F32)
            dBg = jnp.zeros((CHUNK, D_STATE), F32)
            for pr in range(PAIRS_PER_GROUP):
                pair = g * PAIRS_PER_GROUP + pr
                cols = slice(pair * LANE, (pair + 1) * LANE)
                Xp = Xb[:, cols]
                dYp_f = dY[:, cols]
                dYp = dYb[:, cols]
                prevT = st_ref[0, pair]
                prevTb = prevT.astype(BF16)
                dst = dstate[pair]
                dstb = dst.astype(BF16)
                ecs_p = q["ecs_x"][:, cols]
                g_cs[:, cols] = dYp_f * (_dot(Cg, prevTb) * ecs_p)
                dWb = (dYp_f * ecs_p).astype(BF16)
                dprev = dst * q["cd_x"][:, cols] + _dot(Cg, dWb, 0, 0)
                dCg = dCg + _dot(dWb, prevTb, 1, 1)
                g_row[0:1, cols] = jnp.sum(dst * prevT, axis=0, keepdims=True)
                dXp = None
                for hh, h in enumerate((2 * pair, 2 * pair + 1)):
                    mine = first if hh == 0 else jnp.logical_not(first)
                    seg = cs[:, h:h + 1] - csT[h:h + 1, :]
                    lam = jnp.exp(jnp.where(tril, seg, NEG))
                    dM = _dot(jnp.where(mine, dYp, jnp.zeros_like(dYp)), Xp, 1, 1)
                    dS = dS + dM * lam
                    Gm = dM * (S * lam)
                    dcs_col = dcs_col + jnp.sum(Gm, axis=1, keepdims=True) * (lane_row == h).astype(F32)
                    dcs_rowT = dcs_rowT + (sub_col == h).astype(F32) * jnp.sum(Gm, axis=0, keepdims=True)
                    MT = (ST * jnp.exp(jnp.where(triu, -seg, NEG))).astype(BF16)
                    t = _dot(MT, dYp)
                    dXp = t if dXp is None else jnp.where(first, dXp, t)
                dXd = _dot(Bg, dstb)
                dBg = dBg + _dot(Xd[:, cols], dstb, 1, 1)
                g_q[:, cols] = dXd * Xd_f[:, cols]
                dX[:, cols] = dXp + dXd * q["dend_x"][:, cols]
                dstate[pair] = dprev
            dSb = dS.astype(BF16)
            dxbc_ref[:, D_INNER + g * D_STATE:D_INNER + (g + 1) * D_STATE] = dBg + _dot(dSb, Cg, 0, 0)
            dxbc_ref[:, D_INNER + GN + g * D_STATE:D_INNER + GN + (g + 1) * D_STATE] = dCg + _dot(dSb, Bg)
        dXv = dX[...]
        dxbc_ref[:, 0:D_INNER] = q["D_x"] * dY + dXv * q["dt_x"]
        r_dt = _dot3(dXv * xs_v, E, 1, 1)
        r_cs = _dot3(g_cs[...], E, 1, 1)
        r_q = _dot3(g_q[...], E, 1, 1)
        r_row = _dot3(g_row[...], E, 1, 1)
        cd = jnp.exp(cs[CHUNK - 1:CHUNK, :])
        dcs_last = jnp.sum(r_q, axis=0, keepdims=True) + r_row[0:1, :] * cd
        dcs = r_cs - r_q + dcs_col - dcs_rowT.T + jnp.where(sub_col == CHUNK - 1, dcs_last, 0.0)
        da = _dot3b(tri_ref[...], dcs, 0, 0)
        dpre = jnp.where(q["valid"], (r_dt + da * q["A"]) * _sig(q["pre"]), 0.0)
        ddtr_ref[...] = dpre
        dprm_ref[0:1, :] += jnp.sum(dpre, axis=0, keepdims=True)
        dprm_ref[1:2, :] += jnp.sum(da * q["dt"], axis=0, keepdims=True) * q["A"]
        dprm_ref[2:3, :] = r_row[1:2, :]

    rev = lambda w: pl.BlockSpec((CHUNK, w), lambda c: (nc - 1 - c, 0))
    const = lambda a: pl.BlockSpec(a.shape, lambda c: (0,) * a.ndim)
    return pl.pallas_call(
        body, name=name, grid=(nc,),
        in_specs=[rev(D_INNER), rev(D_INNER), rev(D_INNER), rev(D_INNER), rev(2 * GN), rev(LANE),
                  pl.BlockSpec((1, N_PAIRS, D_STATE, LANE), lambda c: (nc - 1 - c, 0, 0, 0)),
                  const(prm), const(ng), const(expand), const(tri)],
        out_specs=[rev(CONV_DIM), rev(D_INNER), rev(LANE),
                   pl.BlockSpec((1, D_INNER), lambda c: (0, 0)), pl.BlockSpec((8, LANE), lambda c: (0, 0))],
        out_shape=[jax.ShapeDtypeStruct((L, CONV_DIM), F32), jax.ShapeDtypeStruct((L, D_INNER), BF16),
                   jax.ShapeDtypeStruct((L, LANE), F32), jax.ShapeDtypeStruct((1, D_INNER), F32),
                   jax.ShapeDtypeStruct((8, LANE), F32)],
        scratch_shapes=[pltpu.VMEM((N_PAIRS, D_STATE, LANE), F32), pltpu.VMEM((CHUNK, D_INNER), F32),
                        pltpu.VMEM((CHUNK, D_INNER), F32), pltpu.VMEM((CHUNK, D_INNER), F32),
                        pltpu.VMEM((8, D_INNER), F32)],
        compiler_params=_cp("arbitrary"),
    )(dyn, y, zx, xs, bc, dtr, st, prm, ng, expand, tri)


def _ssd_weights(W, j):
    w_in = W["ssm_in_w"][j]
    nzx = D_INNER + CONV_DIM
    wdt = jnp.pad(w_in[:, nzx:], ((0, 0), (0, LANE - N_HEADS)))
    prm = jnp.zeros((8, LANE), F32)
    prm = prm.at[0, :N_HEADS].set(W["ssm_dt_bias"][j]).at[1, :N_HEADS].set(W["ssm_A_log"][j])
    prm = prm.at[2, :N_HEADS].set(W["ssm_D"][j])
    return dict(wdt=wdt, cw=W["ssm_conv_w"][j], cb=W["ssm_conv_b"][j].reshape(1, CONV_DIM), prm=prm,
                ng=W["ssm_norm_g"][j].reshape(1, D_INNER))


def ssd_layer_fwd(h, W, j, tag):
    p = _ssd_weights(W, j)
    zx = mm(h, W["ssm_in_w"], layer=j, b_n=D_INNER + CONV_DIM, out_dtype=BF16, name=f"{tag}_zx")
    dtr = mm(h, p["wdt"], name=f"{tag}_dt")
    xs, pre_x = ssm_conv_fwd(zx, p["cw"], p["cb"], col0=D_INNER, ncols=D_INNER, wcol0=0, out_dtype=F32,
                             name=f"{tag}_convx")
    bc, pre_bc = ssm_conv_fwd(zx, p["cw"], p["cb"], col0=2 * D_INNER, ncols=2 * GN, wcol0=D_INNER, out_dtype=BF16,
                              name=f"{tag}_convbc")
    y, yn, st = ssd_fwd(xs, bc, dtr, zx, p["prm"], p["ng"], name=f"{tag}_scan")
    f = mm(yn, W["ssm_out_w"], layer=j, name=f"{tag}_out")
    return f, dict(h=h, zx=zx, dtr=dtr, xs=xs, bc=bc, pre_x=pre_x, pre_bc=pre_bc, y=y, yn=yn, st=st, p=p)


def ssd_layer_bwd(df, ctx, W, GB, j, tag):
    p = ctx["p"]
    h = ctx["h"]
    dyn = mm(df, W["ssm_out_w"], layer=j, tb=True, name=f"{tag}_b_dyn")
    GB["ssm_out_w"] = mm(ctx["yn"], df, ta=True, into=(GB["ssm_out_w"], j, 0), name=f"{tag}_b_gwo")
    dxbc, dz, ddtr, dng, dprm = ssd_bwd(dyn, ctx["y"], ctx["zx"], ctx["xs"], ctx["bc"], ctx["dtr"], ctx["st"],
                                        p["prm"], p["ng"], name=f"{tag}_b_scan")
    dx1, dcw1, dcb1 = ssm_conv_bwd(ctx["zx"], ctx["pre_x"], dxbc, p["cw"], col0=D_INNER, dcol0=0, ncols=D_INNER,
                                   name=f"{tag}_b_convx")
    dx2, dcw2, dcb2 = ssm_conv_bwd(ctx["zx"], ctx["pre_bc"], dxbc, p["cw"], col0=2 * D_INNER, dcol0=D_INNER,
                                   ncols=2 * GN, name=f"{tag}_b_convbc")
    dh = mm(dz, W["ssm_in_w"], layer=j, tb=True, b_k0=0, name=f"{tag}_b_dh1")
    dh = mm(dx1, W["ssm_in_w"], layer=j, tb=True, b_k0=D_INNER, acc=dh, name=f"{tag}_b_dh2")
    dh = mm(dx2, W["ssm_in_w"], layer=j, tb=True, b_k0=2 * D_INNER, acc=dh, name=f"{tag}_b_dh3")
    dh = mm(ddtr, p["wdt"], tb=True, acc=dh, name=f"{tag}_b_dh4")
    g_in = jnp.concatenate([mm(h, dz, ta=True, out_dtype=BF16, name=f"{tag}_b_gz"),
                            mm(h, dx1, ta=True, out_dtype=BF16, name=f"{tag}_b_gx"),
                            mm(h, dx2, ta=True, out_dtype=BF16, name=f"{tag}_b_gbc"),
                            mm(h, ddtr, ta=True, out_dtype=BF16, name=f"{tag}_b_gdt")[:, :N_HEADS]], axis=1)
    return dh, dict(ssm_in_w=g_in, ssm_conv_w=jnp.concatenate([dcw1, dcw2], axis=1),
                    ssm_conv_b=jnp.concatenate([dcb1, dcb2], axis=1)[0], ssm_dt_bias=dprm[0, :N_HEADS],
                    ssm_A_log=dprm[1, :N_HEADS], ssm_D=dprm[2, :N_HEADS], ssm_norm_g=dng[0])


def cf_layer_fwd(h, W, j, tag):
    u = mm(h, W["cf_pw1_w"], layer=j, bias=W["cf_pw1_b"][j], out_dtype=BF16, name=f"{tag}_pw1")
    c, s = cf_fwd(u, W["cf_dw_w"][j], W["cf_dw_b"][j].reshape(1, -1), W["cf_ln_g"][j].reshape(1, -1),
                  W["cf_ln_b"][j].reshape(1, -1), name=f"{tag}_conv")
    f = mm(s, W["cf_pw2_w"], layer=j, bias=W["cf_pw2_b"][j], name=f"{tag}_pw2")
    return f, dict(h=h, u=u, c=c, s=s)


def cf_layer_bwd(df, ctx, W, GB, j, tag):
    h = ctx["h"]
    ds = mm(df, W["cf_pw2_w"], layer=j, tb=True, name=f"{tag}_b_ds")
    GB["cf_pw2_w"] = mm(ctx["s"], df, ta=True, into=(GB["cf_pw2_w"], j, 0), name=f"{tag}_b_gpw2")
    g_b2 = colsum(df, name=f"{tag}_b_gb2")
    dc, dlg, dlb = cf_bwd_ln(ctx["c"], ds, W["cf_ln_g"][j].reshape(1, -1), W["cf_ln_b"][j].reshape(1, -1),
                             name=f"{tag}_b_ln")
    du, ddw, ddb = cf_bwd_conv(ctx["u"], dc, W["cf_dw_w"][j], name=f"{tag}_b_conv")
    dh = mm(du, W["cf_pw1_w"], layer=j, tb=True, name=f"{tag}_b_dh")
    GB["cf_pw1_w"] = mm(h, du, ta=True, into=(GB["cf_pw1_w"], j, 0), name=f"{tag}_b_gpw1")
    g_b1 = colsum(du, name=f"{tag}_b_gb1")
    return dh, dict(cf_pw1_b=g_b1[0], cf_dw_w=ddw, cf_dw_b=ddb[0], cf_ln_g=dlg[0], cf_ln_b=dlb[0], cf_pw2_b=g_b2[0])


def xa_layer_fwd(h, mem, W, i, tag):
    m = norm_fwd(mem, W["xa_mem_g"][i], name=f"{tag}_memnorm")
    kv = mm(m, W["xa_kv_w"], layer=i, out_dtype=BF16, name=f"{tag}_kv")
    q = mm(h, W["xa_q_w"], layer=i, out_dtype=BF16, name=f"{tag}_q")
    o = attn_fwd(q, kv, name=f"{tag}_attn")
    f = mm(o, W["xa_o_w"], layer=i, name=f"{tag}_o")
    return f, dict(h=h, m=m, kv=kv, q=q, o=o)


def xa_layer_bwd(df, ctx, mem, W, GB, i, tag):
    h = ctx["h"]
    do = mm(df, W["xa_o_w"], layer=i, tb=True, out_dtype=BF16, name=f"{tag}_b_do")
    GB["xa_o_w"] = mm(ctx["o"], df, ta=True, into=(GB["xa_o_w"], i, 0), name=f"{tag}_b_go")
    dq, dkv = attn_bwd(ctx["q"], ctx["kv"], do, name=f"{tag}_b_attn")
    dh = mm(dq, W["xa_q_w"], layer=i, tb=True, name=f"{tag}_b_dh")
    GB["xa_q_w"] = mm(h, dq, ta=True, into=(GB["xa_q_w"], i, 0), name=f"{tag}_b_gq")
    GB["xa_kv_w"] = mm(ctx["m"], dkv, ta=True, into=(GB["xa_kv_w"], i, 0), name=f"{tag}_b_gkv")
    dm = mm(dkv, W["xa_kv_w"], layer=i, tb=True, name=f"{tag}_b_dm")
    g_mg = norm_dg(mem, dm, name=f"{tag}_b_gmem")
    return dh, dict(xa_mem_g=g_mg[0])


def ffn_layer_fwd(h, W, i, tag):
    cw, cb = W["ffn_conv_w"][i], W["ffn_conv_b"][i].reshape(1, -1)
    u = mm(h, W["ffn_in_w"], layer=i, out_dtype=BF16, name=f"{tag}_in")
    act, c = ffn_act_fwd(u, cw, cb, name=f"{tag}_act")
    f = mm(act, W["ffn_out_w"], layer=i, name=f"{tag}_out")
    return f, dict(h=h, u=u, c=c, act=act)


def ffn_layer_bwd(df, ctx, W, GB, i, tag):
    h = ctx["h"]
    dact = mm(df, W["ffn_out_w"], layer=i, tb=True, out_dtype=BF16, name=f"{tag}_b_dact")
    GB["ffn_out_w"] = mm(ctx["act"], df, ta=True, into=(GB["ffn_out_w"], i, 0), name=f"{tag}_b_gout")
    du, dcw, dcb = ffn_act_bwd(ctx["u"], ctx["c"], dact, W["ffn_conv_w"][i], name=f"{tag}_b_act")
    dh = None
    for half in range(2):
        dh = mm(du, W["ffn_in_w"], a_idx=half, layer=i, tb=True, b_k0=half * D_FF, acc=dh, name=f"{tag}_b_dh{half}")
        GB["ffn_in_w"] = mm(h, du, ta=True, layer=half, into=(GB["ffn_in_w"], i, half * D_FF), name=f"{tag}_b_gin{half}")
    cat = lambda a: jnp.concatenate([a[0], a[1]], axis=-1)
    return dh, dict(ffn_conv_w=cat(dcw), ffn_conv_b=cat(dcb)[0])


def _layer_weights(i):
    j = i // 2
    mixer = [("ssm_in_w", j), ("ssm_out_w", j)] if i % 2 == 0 else [("cf_pw1_w", j), ("cf_pw2_w", j)]
    return mixer + [(n, i) for n in ("xa_q_w", "xa_kv_w", "xa_o_w", "ffn_in_w", "ffn_out_w")]


def local_step(x, mem, target, W, fetch=None):
    subs = [(i, s) for i in range(DEPTH) for s in range(3)]
    ng = W["norm_g"]

    def fwd(i, s, h):
        tag = f"l{i}s{s}"
        if s == 0:
            return ssd_layer_fwd(h, W, i // 2, tag) if i % 2 == 0 else cf_layer_fwd(h, W, i // 2, tag)
        if s == 1:
            return xa_layer_fwd(h, mem, W, i, tag)
        return ffn_layer_fwd(h, W, i, tag)

    GB = {}

    def bwd(i, s, df, ctx):
        tag = f"l{i}s{s}"
        if s == 0:
            return (ssd_layer_bwd if i % 2 == 0 else cf_layer_bwd)(df, ctx, W, GB, i // 2, tag)
        if s == 1:
            return xa_layer_bwd(df, ctx, mem, W, GB, i, tag)
        return ffn_layer_bwd(df, ctx, W, GB, i, tag)

    h = norm_fwd(x, ng[0, 0], name="norm0")
    saved = []
    dxp = loss = None
    for k, (i, s) in enumerate(subs):
        if s == 0 and fetch is not None:
            fetch(i, x)
        f, ctx = fwd(i, s, h)
        saved.append((x, f, ctx))
        if k + 1 < len(subs):
            ni, ns = subs[k + 1]
            x, h = bnd_fwd(x, f, ng[i, 2 * s + 1], ng[ni, 2 * ns], name=f"bnd{k}")
        else:
            dxp, loss = final_fwd(x, f, ng[i, 2 * s + 1], target, name="final")

    for n in BIG:
        if n != "ssm_in_w":
            GB[n] = jnp.zeros((len(W[n]), *W[n][0].shape), BF16)
    grads = {}

    def put(name, idx, val):
        grads.setdefault(name, {})[idx] = val

    i, s = subs[-1]
    top = bnd_bwd(dxp, post=(saved[-1][1], ng[i, 2 * s + 1]), name="bbnd_top")
    put("norm_g", (i, 2 * s + 1), top["dgpost"][0])
    df = top["df"]
    for k in range(len(subs) - 1, -1, -1):
        i, s = subs[k]
        xk, _, ctx = saved[k]
        dh, gw = bwd(i, s, df, ctx)
        for name, val in gw.items():
            put(name, i // 2 if name.startswith(("ssm_", "cf_")) else i, val)
        if k > 0:
            pi, ps = subs[k - 1]
            r = bnd_bwd(dxp, pre=(xk, ng[i, 2 * s], dh), post=(saved[k - 1][1], ng[pi, 2 * ps + 1]), name=f"bbnd{k}")
            put("norm_g", (pi, 2 * ps + 1), r["dgpost"][0])
            df = r["df"]
        else:
            r = bnd_bwd(dxp, pre=(xk, ng[i, 2 * s], dh), name="bbnd0")
        put("norm_g", (i, 2 * s), r["dgpre"][0])
        dxp = r["dx"]

    out = dict(GB)
    for name, d in grads.items():
        if name == "norm_g":
            out[name] = jnp.stack([jnp.stack([d[(i, t)] for t in range(6)]) for i in range(DEPTH)])
        else:
            out[name] = jnp.stack([d[j] for j in sorted(d)])
    return loss, dxp, out


ANY = pl.BlockSpec(memory_space=pl.ANY)


def _pos():
    return lax.axis_index("x"), lax.axis_index("y"), lax.axis_index("c")


def all_gather(shard, name):
    R, C = shard.shape

    def body(x_ref, out_ref, send_sems, recv_sems, local_sem):
        x, y, c = _pos()
        me, sibling = (x, y, c), (x, y, 1 - c)
        chips = [(1 - x, y), (x, 1 - y), (1 - x, 1 - y)]

        def slot(px, py, pc):
            return out_ref.at[4 * px + 2 * py + pc]

        def copy(k, block, to, src=None):
            return pltpu.make_async_remote_copy(
                src_ref=slot(*block) if src is None else src, dst_ref=slot(*block),
                send_sem=send_sems.at[k], recv_sem=recv_sems.at[k], device_id=to, device_id_type=MESH)

        mine = pltpu.make_async_copy(x_ref, slot(*me), local_sem)
        mine.start()
        first = [copy(0, me, sibling, src=x_ref)]
        first += [copy(1 + j, me, (*chip, c), src=x_ref) for j, chip in enumerate(chips)]
        for cp in first:
            cp.start()
        passed = [copy(4 + j, (*chip, c), sibling) for j, chip in enumerate(chips)]
        for j, chip in enumerate(chips):
            copy(1 + j, (*chip, c), me).wait_recv()
            passed[j].start()
        copy(0, sibling, me).wait_recv()
        for j, chip in enumerate(chips):
            copy(4 + j, (*chip, 1 - c), me).wait_recv()
        for cp in first + passed:
            cp.wait_send()
        mine.wait()

    return pl.pallas_call(
        body, name=name, out_shape=jax.ShapeDtypeStruct((N_DEV, R, C), shard.dtype),
        in_specs=[ANY], out_specs=ANY,
        scratch_shapes=[pltpu.SemaphoreType.DMA((7,)), pltpu.SemaphoreType.DMA((7,)), pltpu.SemaphoreType.DMA(())],
    )(shard)


def _win(ref, kind, k, a, b):
    if kind == "lead":
        return ref.at[k]
    if kind == "row":
        return ref.at[:, pl.ds(pl.multiple_of(k * a, 16), a), :]
    return ref.at[:, :, pl.ds(pl.multiple_of(k * b, LANE), b)]


def _full_shape(shard_shape, kind):
    n, a, b = shard_shape
    return {"lead": (N_DEV, n, a, b), "row": (n, N_DEV * a, b), "col": (n, a, N_DEV * b)}[kind]


HBM = pl.BlockSpec(memory_space=pltpu.HBM)
SEMS = pl.BlockSpec(memory_space=pltpu.SEMAPHORE)
DATAFLOW = pltpu.SideEffectType.DATAFLOW_SIDE_EFFECTING
N_PEER = N_DEV - 1


def _in_hbm(a):
    return pltpu.with_memory_space_constraint(a, pltpu.HBM)


def _peer(x, y, c, r):
    return ((1 - x) if r & 4 else x, (1 - y) if r & 2 else y, (1 - c) if r & 1 else c)


def _win2(ref, kind, k, a, b):
    if kind == "lead":
        return ref.at[k]
    if kind == "row":
        return ref.at[pl.ds(pl.multiple_of(k * a, 16), a), :]
    return ref.at[:, pl.ds(pl.multiple_of(k * b, LANE), b)]


def _zone_shape(kind, a, b):
    return {"lead": (N_DEV, a, b), "row": (N_DEV * a, b), "col": (a, N_DEV * b)}[kind]


def gather_start(shards, items, after, name):
    ns, nz, na = len(shards), len(items), len(after)
    zones = [lax.empty(_zone_shape(kind, a, b), shards[w].dtype) for w, l, kind, a, b in items]

    def body(*refs):
        x_refs = refs[:ns]
        send_sems, recv_sems, local_sems = refs[ns + nz + na:ns + nz + na + 3]
        z_refs = refs[ns + nz + na + 3 + ns:ns + nz + na + 3 + ns + nz]
        token = refs[-1]
        x, y, c = _pos()
        me = 4 * x + 2 * y + c
        for t, (w, l, kind, a, b) in enumerate(items):
            mine = _win2(z_refs[t], kind, me, a, b)
            pltpu.make_async_copy(x_refs[w].at[l], mine, local_sems.at[t]).start()
            for r in range(1, N_DEV):
                pltpu.make_async_remote_copy(
                    src_ref=x_refs[w].at[l], dst_ref=mine,
                    send_sem=send_sems.at[N_PEER * t + r - 1], recv_sem=recv_sems.at[N_PEER * t + r - 1],
                    device_id=_peer(x, y, c, r), device_id_type=MESH).start()
        token[...] = jnp.zeros_like(token)

    n_sem = N_PEER * nz
    outs = pl.pallas_call(
        body, name=name,
        out_shape=(pltpu.SemaphoreType.DMA((n_sem,)), pltpu.SemaphoreType.DMA((n_sem,)), pltpu.SemaphoreType.DMA((nz,)),
                   *[pltpu.HBM(s.shape, s.dtype) for s in shards], *[pltpu.HBM(z.shape, z.dtype) for z in zones],
                   jax.ShapeDtypeStruct((8, LANE), F32)),
        in_specs=[HBM] * (ns + nz) + [pl.BlockSpec(memory_space=pl.ANY)] * na,
        out_specs=(SEMS, SEMS, SEMS, *[HBM] * (ns + nz), pl.BlockSpec(memory_space=pltpu.VMEM)),
        input_output_aliases={i: 3 + i for i in range(ns + nz)},
        compiler_params=pltpu.CompilerParams(has_side_effects=DATAFLOW),
    )(*[_in_hbm(s) for s in shards], *[_in_hbm(z) for z in zones], *after)
    return outs[:3], list(outs[3:3 + ns]), list(outs[3 + ns:3 + ns + nz]), outs[-1]


def gather_wait(zones, idx, items, sems, after, keep, name):
    nz, nk = len(zones), len(keep)

    def body(*refs):
        z_refs = refs[:nz]
        send_sems, recv_sems, local_sems = refs[nz:nz + 3]
        x, y, c = _pos()
        me = 4 * x + 2 * y + c
        for z_ref, t in zip(z_refs, idx):
            w, l, kind, a, b = items[t]
            mine = _win2(z_ref, kind, me, a, b)
            pltpu.make_async_copy(mine, mine, local_sems.at[t]).wait()
            for r in range(1, N_DEV):
                peer = _peer(x, y, c, r)
                cp = pltpu.make_async_remote_copy(
                    src_ref=mine, dst_ref=_win2(z_ref, kind, 4 * peer[0] + 2 * peer[1] + peer[2], a, b),
                    send_sem=send_sems.at[N_PEER * t + r - 1], recv_sem=recv_sems.at[N_PEER * t + r - 1],
                    device_id=peer, device_id_type=MESH)
                cp.wait_send()
                cp.wait_recv()

    outs = pl.pallas_call(
        body, name=name, out_shape=tuple(pltpu.HBM(z.shape, z.dtype) for z in zones),
        in_specs=[HBM] * nz + [SEMS] * 3 + [pl.BlockSpec(memory_space=pl.ANY)] * (1 + nk),
        out_specs=tuple([HBM] * nz), input_output_aliases={i: i for i in range(nz)},
        compiler_params=pltpu.CompilerParams(has_side_effects=DATAFLOW),
    )(*zones, *sems, after, *keep)
    return list(outs)


def gather_now(shards, kinds, name):
    nw = len(shards)
    geo = [s.shape[1:] for s in shards]

    def body(*refs):
        x_refs, o_refs = refs[:nw], refs[nw:2 * nw]
        send_sems, recv_sems, local_sems = refs[2 * nw:]
        x, y, c = _pos()
        me, sibling = (x, y, c), (x, y, 1 - c)
        chips = [(1 - x, y), (x, 1 - y), (1 - x, 1 - y)]

        def slot(w, px, py, pc):
            return _win(o_refs[w], kinds[w], 4 * px + 2 * py + pc, *geo[w])

        def copy(w, k, block, to, src=None):
            return pltpu.make_async_remote_copy(
                src_ref=slot(w, *block) if src is None else src, dst_ref=slot(w, *block),
                send_sem=send_sems.at[7 * w + k], recv_sem=recv_sems.at[7 * w + k], device_id=to, device_id_type=MESH)

        mine = [pltpu.make_async_copy(x_refs[w], slot(w, *me), local_sems.at[w]) for w in range(nw)]
        for cp in mine:
            cp.start()
        first = []
        for w in range(nw):
            first.append(copy(w, 0, me, sibling, src=x_refs[w]))
            first += [copy(w, 1 + j, me, (*chip, c), src=x_refs[w]) for j, chip in enumerate(chips)]
        for cp in first:
            cp.start()
        passed = []
        for w in range(nw):
            for j, chip in enumerate(chips):
                copy(w, 1 + j, (*chip, c), me).wait_recv()
                cp = copy(w, 4 + j, (*chip, c), sibling)
                cp.start()
                passed.append(cp)
        for w in range(nw):
            copy(w, 0, sibling, me).wait_recv()
            for j, chip in enumerate(chips):
                copy(w, 4 + j, (*chip, 1 - c), me).wait_recv()
        for cp in first + passed:
            cp.wait_send()
        for cp in mine:
            cp.wait()

    return pl.pallas_call(
        body, name=name,
        out_shape=[jax.ShapeDtypeStruct(_full_shape(s.shape, k), s.dtype) for s, k in zip(shards, kinds)],
        in_specs=[ANY] * nw, out_specs=[ANY] * nw,
        scratch_shapes=[pltpu.SemaphoreType.DMA((7 * nw,)), pltpu.SemaphoreType.DMA((7 * nw,)),
                        pltpu.SemaphoreType.DMA((nw,))],
    )(*shards)


def rs_sibling(gs, kinds, geo, name):
    nw = len(gs)

    def body(*refs):
        g_refs, o_refs = refs[:nw], refs[nw:2 * nw]
        send_sems, recv_sems = refs[2 * nw:]
        x, y, c = _pos()
        cps = [pltpu.make_async_remote_copy(
            src_ref=_win(g_refs[w], kinds[w], 2 * j + 1 - c, *geo[w][1:]), dst_ref=o_refs[w].at[j],
            send_sem=send_sems.at[4 * w + j], recv_sem=recv_sems.at[4 * w + j],
            device_id=(x, y, 1 - c), device_id_type=MESH) for w in range(nw) for j in range(4)]
        for cp in cps:
            cp.start()
        for cp in cps:
            cp.wait()

    return pl.pallas_call(
        body, name=name, out_shape=[jax.ShapeDtypeStruct((4, *geo[w]), gs[w].dtype) for w in range(nw)],
        in_specs=[ANY] * nw, out_specs=[ANY] * nw,
        scratch_shapes=[pltpu.SemaphoreType.DMA((4 * nw,)), pltpu.SemaphoreType.DMA((4 * nw,))],
    )(*gs)


def rs_add(g, kind, got, name):
    _, n, a, b = got.shape
    ta = a if a <= 512 else 256
    per = a // ta
    core = lax.axis_index("c").astype(jnp.int32).reshape(1)

    def body(c_ref, g_ref, got_ref, o_ref):
        o_ref[...] = (g_ref[...].astype(F32) + got_ref[...].astype(F32)).astype(o_ref.dtype)

    if kind == "lead":
        g_spec = pl.BlockSpec((None, None, ta, b), lambda j, l, r, c_ref: (2 * j + c_ref[0], l, r, 0))
    elif kind == "row":
        g_spec = pl.BlockSpec((None, ta, b), lambda j, l, r, c_ref: (l, (2 * j + c_ref[0]) * per + r, 0))
    else:
        g_spec = pl.BlockSpec((None, ta, b), lambda j, l, r, c_ref: (l, r, 2 * j + c_ref[0]))
    blk = pl.BlockSpec((None, None, ta, b), lambda j, l, r, c_ref: (j, l, r, 0))
    return pl.pallas_call(
        body, name=name, out_shape=jax.ShapeDtypeStruct(got.shape, got.dtype),
        grid_spec=pltpu.PrefetchScalarGridSpec(num_scalar_prefetch=1, grid=(4, n, per),
                                               in_specs=[g_spec, blk], out_specs=blk),
        compiler_params=_cp("parallel", "parallel", "parallel"),
    )(core, g, got)


def rs_chips(s1s, name):
    nw = len(s1s)

    def body(*refs):
        s_refs, o_refs = refs[:nw], refs[nw:2 * nw]
        send_sems, recv_sems, local_sems = refs[2 * nw:]
        x, y, c = _pos()
        jme = 2 * x + y
        peers = [(1 - x, y), (x, 1 - y), (1 - x, 1 - y)]
        local = [pltpu.make_async_copy(s_refs[w].at[jme], o_refs[w].at[jme], local_sems.at[w]) for w in range(nw)]
        for cp in local:
            cp.start()

        def copy(w, k, src_slot, dst_slot, peer):
            return pltpu.make_async_remote_copy(
                src_ref=s_refs[w].at[src_slot], dst_ref=o_refs[w].at[dst_slot], send_sem=send_sems.at[3 * w + k],
                recv_sem=recv_sems.at[3 * w + k], device_id=(*peer, c), device_id_type=MESH)

        sends = [copy(w, k, 2 * px + py, jme, (px, py)) for w in range(nw) for k, (px, py) in enumerate(peers)]
        for cp in sends:
            cp.start()
        for w in range(nw):
            for k, (px, py) in enumerate(peers):
                copy(w, k, jme, 2 * px + py, (px, py)).wait_recv()
        for cp in sends:
            cp.wait_send()
        for cp in local:
            cp.wait()

    return pl.pallas_call(
        body, name=name, out_shape=[jax.ShapeDtypeStruct(s.shape, s.dtype) for s in s1s],
        in_specs=[ANY] * nw, out_specs=[ANY] * nw,
        scratch_shapes=[pltpu.SemaphoreType.DMA((3 * nw,)), pltpu.SemaphoreType.DMA((3 * nw,)),
                        pltpu.SemaphoreType.DMA((nw,))],
    )(*s1s)


def small_exchange(sh, rep, name):
    _, Rs, C = sh.shape
    Rr = rep.shape[0]

    def body(sh_ref, rep_ref, sh_out, rep_out, send_sems, recv_sems, local_sems):
        x, y, c = _pos()
        me = 4 * x + 2 * y + c
        l1 = pltpu.make_async_copy(sh_ref.at[me], sh_out.at[me], local_sems.at[0])
        l2 = pltpu.make_async_copy(rep_ref, rep_out.at[me], local_sems.at[1])
        l1.start()
        l2.start()

        def flip(v, bit):
            return 1 - v if bit else v

        sends, recvs = [], []
        for r in range(1, N_DEV):
            peer = (flip(x, r & 4), flip(y, r & 2), flip(c, r & 1))
            pid = 4 * peer[0] + 2 * peer[1] + peer[2]
            k = 2 * (r - 1)
            mk = lambda src, dst, kk: pltpu.make_async_remote_copy(
                src_ref=src, dst_ref=dst, send_sem=send_sems.at[kk], recv_sem=recv_sems.at[kk],
                device_id=peer, device_id_type=MESH)
            sends += [mk(sh_ref.at[pid], sh_out.at[me], k), mk(rep_ref, rep_out.at[me], k + 1)]
            recvs += [mk(sh_ref.at[me], sh_out.at[pid], k), mk(rep_ref, rep_out.at[pid], k + 1)]
        for cp in sends:
            cp.start()
        for cp in recvs:
            cp.wait_recv()
        for cp in sends:
            cp.wait_send()
        l1.wait()
        l2.wait()

    n = 2 * (N_DEV - 1)
    return pl.pallas_call(
        body, name=name,
        out_shape=[jax.ShapeDtypeStruct((N_DEV, Rs, C), sh.dtype), jax.ShapeDtypeStruct((N_DEV, *rep.shape), rep.dtype)],
        in_specs=[ANY, ANY], out_specs=[ANY, ANY],
        scratch_shapes=[pltpu.SemaphoreType.DMA((n,)), pltpu.SemaphoreType.DMA((n,)), pltpu.SemaphoreType.DMA((2,))],
    )(sh, rep)


def adam_slots(w, m, v, slots, name):
    S, n, a, b = slots.shape
    ta = max(t for t in range(16, min(a, 512) + 1, 8)
             if a % t == 0 and t * S * b * slots.dtype.itemsize <= 4 * 1024 * 1024)

    def body(w_ref, m_ref, v_ref, s_ref, g_ref, d_ref, m2_ref, v2_ref):
        gv = s_ref[0].astype(F32)
        for k in range(1, S):
            gv = gv + s_ref[k].astype(F32)
        m2 = ADAM_B1 * m_ref[...] + (1.0 - ADAM_B1) * gv
        v2 = ADAM_B2 * v_ref[...] + (1.0 - ADAM_B2) * (gv * gv)
        m_hat = m2 / (1.0 - ADAM_B1 ** ADAM_STEP)
        v_hat = v2 / (1.0 - ADAM_B2 ** ADAM_STEP)
        g_ref[...] = gv
        d_ref[...] = -ADAM_LR * (m_hat / (jnp.sqrt(v_hat) + ADAM_EPS) + ADAM_WD * w_ref[...])
        m2_ref[...] = m2
        v2_ref[...] = v2

    spec = pl.BlockSpec((None, ta, b), lambda l, r: (l, r, 0))
    return pl.pallas_call(
        body, name=name, grid=(n, a // ta),
        in_specs=[spec] * 3 + [pl.BlockSpec((S, None, ta, b), lambda l, r: (0, l, r, 0))], out_specs=[spec] * 4,
        out_shape=[jax.ShapeDtypeStruct((n, a, b), F32)] * 4, compiler_params=_cp("parallel", "parallel"),
    )(w, m, v, slots)


WEIGHTS = ["norm_g", "ssm_in_w", "ssm_conv_w", "ssm_conv_b", "ssm_dt_bias", "ssm_A_log", "ssm_D", "ssm_norm_g",
           "ssm_out_w", "cf_pw1_w", "cf_pw1_b", "cf_dw_w", "cf_dw_b", "cf_ln_g", "cf_ln_b", "cf_pw2_w", "cf_pw2_b",
           "xa_mem_g", "xa_q_w", "xa_kv_w", "xa_o_w", "ffn_in_w", "ffn_conv_w", "ffn_conv_b", "ffn_out_w"]
ARGS = ["x", "mem"] + WEIGHTS + ["loss_target"] + ["m_" + n for n in WEIGHTS] + ["v_" + n for n in WEIGHTS]
BIG = {"ssm_in_w": "col", "ssm_out_w": "row", "cf_pw1_w": "col", "cf_pw2_w": "row", "xa_q_w": "row",
       "xa_kv_w": "col", "xa_o_w": "row", "ffn_in_w": "col", "ffn_out_w": "row"}
SMALL = ["norm_g", "ssm_conv_w", "cf_pw1_b", "cf_dw_w", "cf_dw_b", "cf_ln_g", "cf_ln_b", "cf_pw2_b", "ffn_conv_w"]
REP = ["ssm_conv_b", "ssm_dt_bias", "ssm_A_log", "ssm_D", "ssm_norm_g", "xa_mem_g", "ffn_conv_b"]
SMALL_W = 768
REP_W = 5632


def _r8(n):
    return -(-n // 8) * 8


def _stack2d(arrs, wid):
    parts = []
    for a in arrs:
        r, c = a.shape[-2:]
        parts.append(jnp.pad(a, [(0, 0)] * (a.ndim - 2) + [(0, _r8(r) - r), (0, wid - c)]))
    return jnp.concatenate(parts, axis=-2)


def _unstack2d(buf, shapes2d):
    out, o = [], 0
    for r, c in shapes2d:
        out.append(buf[..., o:o + r, :c])
        o += _r8(r)
    return out


def _gathered_to_full(g):
    lead = g.shape[1:-1]
    return jnp.moveaxis(g, 0, -2).reshape(*lead, N_DEV * g.shape[-1])


def _full_to_slots(w):
    lead = w.shape[:-1]
    return jnp.moveaxis(w.reshape(*lead, N_DEV, w.shape[-1] // N_DEV), -2, 0)


def kernel(x, mem, norm_g, ssm_in_w, ssm_conv_w, ssm_conv_b, ssm_dt_bias, ssm_A_log, ssm_D, ssm_norm_g, ssm_out_w, cf_pw1_w, cf_pw1_b, cf_dw_w, cf_dw_b, cf_ln_g, cf_ln_b, cf_pw2_w, cf_pw2_b, xa_mem_g, xa_q_w, xa_kv_w, xa_o_w, ffn_in_w, ffn_conv_w, ffn_conv_b, ffn_out_w, loss_target, m_norm_g, m_ssm_in_w, m_ssm_conv_w, m_ssm_conv_b, m_ssm_dt_bias, m_ssm_A_log, m_ssm_D, m_ssm_norm_g, m_ssm_out_w, m_cf_pw1_w, m_cf_pw1_b, m_cf_dw_w, m_cf_dw_b, m_cf_ln_g, m_cf_ln_b, m_cf_pw2_w, m_cf_pw2_b, m_xa_mem_g, m_xa_q_w, m_xa_kv_w, m_xa_o_w, m_ffn_in_w, m_ffn_conv_w, m_ffn_conv_b, m_ffn_out_w, v_norm_g, v_ssm_in_w, v_ssm_conv_w, v_ssm_conv_b, v_ssm_dt_bias, v_ssm_A_log, v_ssm_D, v_ssm_norm_g, v_ssm_out_w, v_cf_pw1_w, v_cf_pw1_b, v_cf_dw_w, v_cf_dw_b, v_cf_ln_g, v_cf_ln_b, v_cf_pw2_w, v_cf_pw2_b, v_xa_mem_g, v_xa_q_w, v_xa_kv_w, v_xa_o_w, v_ffn_in_w, v_ffn_conv_w, v_ffn_conv_b, v_ffn_out_w):
    return _step(x, mem, norm_g, ssm_in_w, ssm_conv_w, ssm_conv_b, ssm_dt_bias, ssm_A_log, ssm_D, ssm_norm_g, ssm_out_w, cf_pw1_w, cf_pw1_b, cf_dw_w, cf_dw_b, cf_ln_g, cf_ln_b, cf_pw2_w, cf_pw2_b, xa_mem_g, xa_q_w, xa_kv_w, xa_o_w, ffn_in_w, ffn_conv_w, ffn_conv_b, ffn_out_w, loss_target, m_norm_g, m_ssm_in_w, m_ssm_conv_w, m_ssm_conv_b, m_ssm_dt_bias, m_ssm_A_log, m_ssm_D, m_ssm_norm_g, m_ssm_out_w, m_cf_pw1_w, m_cf_pw1_b, m_cf_dw_w, m_cf_dw_b, m_cf_ln_g, m_cf_ln_b, m_cf_pw2_w, m_cf_pw2_b, m_xa_mem_g, m_xa_q_w, m_xa_kv_w, m_xa_o_w, m_ffn_in_w, m_ffn_conv_w, m_ffn_conv_b, m_ffn_out_w, v_norm_g, v_ssm_in_w, v_ssm_conv_w, v_ssm_conv_b, v_ssm_dt_bias, v_ssm_A_log, v_ssm_D, v_ssm_norm_g, v_ssm_out_w, v_cf_pw1_w, v_cf_pw1_b, v_cf_dw_w, v_cf_dw_b, v_cf_ln_g, v_cf_ln_b, v_cf_pw2_w, v_cf_pw2_b, v_xa_mem_g, v_xa_q_w, v_xa_kv_w, v_xa_o_w, v_ffn_in_w, v_ffn_conv_w, v_ffn_conv_b, v_ffn_out_w)


def _step(*args):
    A = dict(zip(ARGS, args, strict=True))
    x, mem, target = A["x"][0], A["mem"][0], A["loss_target"][0]

    big = list(BIG)
    geo = [A[n].shape for n in big]
    kinds = ["row" if BIG[n] == "row" else ("col" if A[n].shape[-1] % LANE == 0 else "lead") for n in big]
    W = {n: A[n] for n in REP}
    small2d = [(A[n].size // A[n].shape[-1], A[n].shape[-1]) for n in SMALL]
    rep2d = [A[n].shape for n in REP] + [(1, 1)]
    stack_small = lambda pre: _stack2d([A[pre + n].reshape(rc) for n, rc in zip(SMALL, small2d)], SMALL_W)
    stack_rep = lambda pre: _stack2d([A[pre + n] for n in REP] + [jnp.zeros((1, 1), F32)], REP_W)
    small_g = all_gather(stack_small(""), name="gather_small")
    for n, g in zip(SMALL, _unstack2d(small_g, small2d)):
        W[n] = _gathered_to_full(g.reshape(N_DEV, *A[n].shape))

    shards = [A[n].astype(BF16) for n in big]
    for n in big:
        W[n] = [None] * A[n].shape[0]
    first = _layer_weights(0)
    got0 = gather_now([shards[big.index(n)][l:l + 1] for n, l in first], [kinds[big.index(n)] for n, l in first],
                      name="gather_layer0")
    for (n, l), g in zip(first, got0):
        W[n][l] = _gathered_to_full(g)[0] if kinds[big.index(n)] == "lead" else g[0]
    items, layer_items = [], [[]]
    for i in range(1, DEPTH):
        layer_items.append([])
        for n, l in _layer_weights(i):
            w = big.index(n)
            layer_items[i].append(len(items))
            items.append((w, l, kinds[w], *geo[w][1:]))
    sems, shards_thru, zones, token = gather_start(shards, items, [small_g, got0[0]], name="gather_start")
    mem = mem + token[0, 0]

    def fetch(i, x_in):
        ids = layer_items[i]
        if not ids:
            return
        got = gather_wait([zones[t] for t in ids], ids, items, sems, x_in, shards_thru if i == DEPTH - 1 else [],
                          name=f"gather_wait{i}")
        for t, z in zip(ids, got):
            w, l, kind = items[t][:3]
            W[big[w]][l] = _gathered_to_full(z) if kind == "lead" else z

    loss, grad_x, G = local_step(x, mem, target, W, fetch)

    gs = [_full_to_slots(G[n]) if kind == "lead" else G[n] for n, kind in zip(big, kinds)]
    got1 = rs_sibling(gs, kinds, geo, name="rs_sibling")
    s1 = [rs_add(g, kind, got, name=f"rs_add_{n}") for n, g, kind, got in zip(big, gs, kinds, got1)]
    got2 = rs_chips(s1, name="rs_chips")

    sh = _stack2d([_full_to_slots(G[n]).reshape(N_DEV, *rc) for n, rc in zip(SMALL, small2d)], SMALL_W)
    rep = _stack2d([G[n] for n in REP] + [loss[:, :1]], REP_W)
    sh_got, rep_got = small_exchange(sh, rep, name="small_exchange")

    res = {}
    for n, slots in zip(big, got2):
        res[n] = adam_slots(A[n], A["m_" + n], A["v_" + n], slots, name=f"adam_{n}")
    for names, shapes2d, stack, slots, tag in ((SMALL, small2d, stack_small, sh_got, "small"),
                                               (REP, rep2d, stack_rep, rep_got, "rep")):
        outs4 = adam_slots(stack("")[None], stack("m_")[None], stack("v_")[None], slots[:, None], name=f"adam_{tag}")
        parts = [_unstack2d(o[0], shapes2d) for o in outs4]
        for k, n in enumerate(names):
            res[n] = tuple(q[k].reshape(A[n].shape) for q in parts)
        if tag == "rep":
            total_loss = parts[0][-1][0, 0]

    outs = [total_loss, grad_x[None]]
    for k in range(4):
        outs += [res[n][k] for n in WEIGHTS]
    return tuple(outs)
```

```python
import jax
import jax.numpy as jnp
from jax import lax
from jax.experimental import pallas as pl
from jax.experimental.pallas import tpu as pltpu

F32 = jnp.float32
BF16 = jnp.bfloat16

D_MODEL = 1024
D_INNER = 2048
N_HEADS = 32
HEAD_DIM = 64
N_GROUPS = 4
D_STATE = 128
CHUNK = 128
CONV_DIM = 3072
SSM_K = 4
CF_K = 31
N_MEM = 256
XA_HEADS = 4
XA_HD = 256
D_FF = 2816
FFN_K = 3
EPS = 1e-6
DEPTH = 4
N_DEV = 8

ADAM_LR = 0.001
ADAM_B1 = 0.9
ADAM_B2 = 0.999
ADAM_EPS = 1e-08
ADAM_WD = 0.01
ADAM_STEP = 10

LANE = 128
VMEM_LIMIT = 56 * 1024 * 1024
NEG = -1e30
MESH = pl.DeviceIdType.MESH


def _cp(*sem):
    return pltpu.CompilerParams(dimension_semantics=sem if sem else None, vmem_limit_bytes=VMEM_LIMIT)


def _tile(n, cap):
    if n <= cap:
        return n
    best = 0
    for t in range(LANE, cap + 1, LANE):
        if n % t == 0:
            best = t
    assert best, (n, cap)
    return best


def _sig(x):
    return 1.0 / (1.0 + jnp.exp(-x))


def _split3(v):
    v0 = v.astype(BF16)
    r1 = v - v0.astype(F32)
    v1 = r1.astype(BF16)
    v2 = (r1 - v1.astype(F32)).astype(BF16)
    return v0, v1, v2


def _dot(a, b, ca=1, cb=0):
    return lax.dot_general(a, b, (((ca,), (cb,)), ((), ())), preferred_element_type=F32)


def _dot3(v, m, ca=1, cb=0):
    v0, v1, v2 = _split3(v)
    return _dot(v0, m, ca, cb) + _dot(v1, m, ca, cb) + _dot(v2, m, ca, cb)


def mm(a, b, *, ta=False, tb=False, bias=None, acc=None, out_dtype=F32, a_idx=None, layer=None, b_k0=0, b_n=None,
       into=None, name):
    if isinstance(b, (list, tuple)):
        b, layer = b[layer], None
    if ta:
        K, M = a.shape[-2:]
    else:
        M, K = a.shape[-2:]
    N = b_n if b_n is not None else (b.shape[-2] if tb else b.shape[-1])
    assert (b.ndim == 3) == (layer is not None) and (a.ndim == 3) == (a_idx is not None)
    tm = _tile(M, 1024)
    tn = _tile(N, 1536)
    tk = _tile(K, 2048)
    nk = K // tk
    assert b_k0 % tk == 0 and b_k0 + K <= (b.shape[-1] if tb else b.shape[-2])
    kb = b_k0 // tk
    has_bias, has_acc = bias is not None, acc is not None
    if into is not None:
        out_dtype = into[0].dtype
        assert into[0].shape[1] == M and into[2] % tn == 0 and into[2] + N <= into[0].shape[2] and not has_acc

    def body(*refs):
        a_ref, b_ref = refs[0], refs[1]
        pos = 2
        bias_ref = acc_ref = None
        if has_bias:
            bias_ref = refs[pos]
            pos += 1
        if has_acc:
            acc_ref = refs[pos]
            pos += 1
        if into is not None:
            pos += 1
        o_ref = refs[pos]
        s_ref = refs[pos + 1] if nk > 1 else None
        p = _dot(a_ref[...].astype(BF16), b_ref[...].astype(BF16), 0 if ta else 1, 1 if tb else 0)

        def extras(v):
            if has_bias:
                v = v + bias_ref[...]
            if has_acc:
                v = v + acc_ref[...]
            return v

        if nk == 1:
            o_ref[...] = extras(p).astype(out_dtype)
        else:
            k = pl.program_id(2)

            @pl.when(k == 0)
            def _():
                s_ref[...] = extras(p)

            @pl.when(k > 0)
            def _():
                s_ref[...] += p

            @pl.when(k == nk - 1)
            def _():
                o_ref[...] = s_ref[...].astype(out_dtype)

    lead_a = () if a_idx is None else (a_idx,)
    lead_b = () if layer is None else (layer,)
    sq = lambda lead: (None,) * len(lead)
    if ta:
        a_spec = pl.BlockSpec((*sq(lead_a), tk, tm), lambda i, j, k: (*lead_a, k, i))
    else:
        a_spec = pl.BlockSpec((*sq(lead_a), tm, tk), lambda i, j, k: (*lead_a, i, k))
    if tb:
        b_spec = pl.BlockSpec((*sq(lead_b), tn, tk), lambda i, j, k: (*lead_b, j, k + kb))
    else:
        b_spec = pl.BlockSpec((*sq(lead_b), tk, tn), lambda i, j, k: (*lead_b, k + kb, j))
    in_specs, args = [a_spec, b_spec], [a, b]
    if has_bias:
        in_specs.append(pl.BlockSpec((1, tn), lambda i, j, k: (0, j)))
        args.append(bias.reshape(1, N).astype(F32))
    if has_acc:
        in_specs.append(pl.BlockSpec((tm, tn), lambda i, j, k: (i, j)))
        args.append(acc)
    if into is None:
        out_spec = pl.BlockSpec((tm, tn), lambda i, j, k: (i, j))
        out_shape = jax.ShapeDtypeStruct((M, N), out_dtype)
        aliases = {}
    else:
        buf, l, col0 = into
        cb = col0 // tn
        in_specs.append(pl.BlockSpec(memory_space=pl.ANY))
        args.append(buf)
        out_spec = pl.BlockSpec((None, tm, tn), lambda i, j, k: (l, i, j + cb))
        out_shape = jax.ShapeDtypeStruct(buf.shape, buf.dtype)
        aliases = {len(args) - 1: 0}
    return pl.pallas_call(
        body, name=name, grid=(M // tm, N // tn, nk),
        in_specs=in_specs, out_specs=out_spec, out_shape=out_shape, input_output_aliases=aliases,
        scratch_shapes=[pltpu.VMEM((tm, tn), F32)] if nk > 1 else [],
        compiler_params=_cp("parallel", "parallel", "arbitrary"),
    )(*args)


def colsum(x, name):
    L, C = x.shape
    tr = _tile(L, 512)
    tc = _tile(C, 1024)

    def body(x_ref, o_ref):
        @pl.when(pl.program_id(1) == 0)
        def _():
            o_ref[...] = jnp.zeros_like(o_ref)

        o_ref[...] += jnp.sum(x_ref[...].astype(F32), axis=0, keepdims=True)

    return pl.pallas_call(
        body, name=name, grid=(C // tc, L // tr),
        in_specs=[pl.BlockSpec((tr, tc), lambda j, i: (i, j))],
        out_specs=pl.BlockSpec((1, tc), lambda j, i: (0, j)),
        out_shape=jax.ShapeDtypeStruct((1, C), F32),
        compiler_params=_cp("parallel", "arbitrary"),
    )(x)


TR = 256


def _row_spec(tr, w):
    return pl.BlockSpec((tr, w), lambda i: (i, 0))


def _vec_spec(w):
    return pl.BlockSpec((1, w), lambda i: (0, 0))


def _rms(v):
    return lax.rsqrt(jnp.mean(v * v, axis=-1, keepdims=True) + EPS)


def norm_fwd(x, g, name):
    L, D = x.shape
    tr = min(TR, L)

    def body(x_ref, g_ref, h_ref):
        xv = x_ref[...]
        h_ref[...] = (xv * _rms(xv) * g_ref[...]).astype(BF16)

    return pl.pallas_call(
        body, name=name, grid=(L // tr,),
        in_specs=[_row_spec(tr, D), _vec_spec(D)], out_specs=_row_spec(tr, D),
        out_shape=jax.ShapeDtypeStruct((L, D), BF16), compiler_params=_cp("parallel"),
    )(x, g.reshape(1, D))


def bnd_fwd(x, f, gpost, gpre, name):
    L, D = x.shape
    tr = min(TR, L)

    def body(x_ref, f_ref, gp_ref, gn_ref, xo_ref, h_ref):
        fv = f_ref[...]
        xn = x_ref[...] + fv * _rms(fv) * gp_ref[...]
        xo_ref[...] = xn
        h_ref[...] = (xn * _rms(xn) * gn_ref[...]).astype(BF16)

    return pl.pallas_call(
        body, name=name, grid=(L // tr,),
        in_specs=[_row_spec(tr, D), _row_spec(tr, D), _vec_spec(D), _vec_spec(D)],
        out_specs=[_row_spec(tr, D), _row_spec(tr, D)],
        out_shape=[jax.ShapeDtypeStruct((L, D), F32), jax.ShapeDtypeStruct((L, D), BF16)],
        compiler_params=_cp("parallel"),
    )(x, f, gpost.reshape(1, D), gpre.reshape(1, D))


def final_fwd(x, f, gpost, target, name):
    L, D = x.shape
    tr = min(TR, L)
    n = L // tr

    def body(x_ref, f_ref, gp_ref, t_ref, dy_ref, loss_ref, acc_ref):
        i = pl.program_id(0)

        @pl.when(i == 0)
        def _():
            acc_ref[...] = jnp.zeros_like(acc_ref)

        fv = f_ref[...]
        e = x_ref[...] + fv * _rms(fv) * gp_ref[...] - t_ref[...]
        dy_ref[...] = e * (1.0 / D)
        acc_ref[...] += jnp.sum(e * e, axis=0, keepdims=True)

        @pl.when(i == n - 1)
        def _():
            loss_ref[...] = jnp.full((1, LANE), 0.5 / D, F32) * jnp.sum(acc_ref[...])

    return pl.pallas_call(
        body, name=name, grid=(n,),
        in_specs=[_row_spec(tr, D), _row_spec(tr, D), _vec_spec(D), _row_spec(tr, D)],
        out_specs=[_row_spec(tr, D), _vec_spec(LANE)],
        out_shape=[jax.ShapeDtypeStruct((L, D), F32), jax.ShapeDtypeStruct((1, LANE), F32)],
        scratch_shapes=[pltpu.VMEM((1, D), F32)],
        compiler_params=_cp("arbitrary"),
    )(x, f, gpost.reshape(1, D), target)


def _rms_bwd(v, g, dy):
    r = _rms(v)
    vn = v * r
    dg = jnp.sum(dy * vn, axis=0, keepdims=True)
    dvn = dy * g
    dv = r * (dvn - vn * jnp.mean(dvn * vn, axis=-1, keepdims=True))
    return dv, dg


def bnd_bwd(dxp, *, pre=None, post=None, name):
    L, D = dxp.shape
    tr = min(TR, L)
    has_pre, has_post = pre is not None, post is not None

    def body(*refs):
        pos = 0
        dxp_ref = refs[pos]; pos += 1
        if has_pre:
            x_ref, gpre_ref, dh_ref = refs[pos:pos + 3]; pos += 3
        if has_post:
            f_ref, gpost_ref = refs[pos:pos + 2]; pos += 2
        if has_pre:
            dx_ref, dgpre_ref = refs[pos:pos + 2]; pos += 2
        if has_post:
            df_ref, dgpost_ref = refs[pos:pos + 2]; pos += 2
        i = pl.program_id(0)
        dx = dxp_ref[...]
        if has_pre:
            d, dg = _rms_bwd(x_ref[...], gpre_ref[...], dh_ref[...])
            dx = dx + d
            dx_ref[...] = dx

            @pl.when(i == 0)
            def _():
                dgpre_ref[...] = jnp.zeros_like(dgpre_ref)

            dgpre_ref[...] += dg
        if has_post:
            d, dg = _rms_bwd(f_ref[...], gpost_ref[...], dx)
            df_ref[...] = d.astype(BF16)

            @pl.when(i == 0)
            def _():
                dgpost_ref[...] = jnp.zeros_like(dgpost_ref)

            dgpost_ref[...] += dg

    in_specs, args = [_row_spec(tr, D)], [dxp]
    out_specs, out_shape, names = [], [], []
    if has_pre:
        x, gpre, dh = pre
        in_specs += [_row_spec(tr, D), _vec_spec(D), _row_spec(tr, D)]
        args += [x, gpre.reshape(1, D), dh]
        out_specs += [_row_spec(tr, D), _vec_spec(D)]
        out_shape += [jax.ShapeDtypeStruct((L, D), F32), jax.ShapeDtypeStruct((1, D), F32)]
        names += ["dx", "dgpre"]
    if has_post:
        f, gpost = post
        in_specs += [_row_spec(tr, D), _vec_spec(D)]
        args += [f, gpost.reshape(1, D)]
        out_specs += [_row_spec(tr, D), _vec_spec(D)]
        out_shape += [jax.ShapeDtypeStruct((L, D), BF16), jax.ShapeDtypeStruct((1, D), F32)]
        names += ["df", "dgpost"]
    outs = pl.pallas_call(
        body, name=name, grid=(L // tr,), in_specs=in_specs, out_specs=out_specs, out_shape=out_shape,
        compiler_params=_cp("arbitrary"),
    )(*args)
    return dict(zip(names, outs))


def norm_dg(x, dy, name):
    L, D = x.shape
    tr = min(TR, L)

    def body(x_ref, dy_ref, o_ref):
        @pl.when(pl.program_id(0) == 0)
        def _():
            o_ref[...] = jnp.zeros_like(o_ref)

        xv = x_ref[...]
        o_ref[...] += jnp.sum(dy_ref[...] * xv * _rms(xv), axis=0, keepdims=True)

    return pl.pallas_call(
        body, name=name, grid=(L // tr,),
        in_specs=[_row_spec(tr, D), _row_spec(tr, D)], out_specs=_vec_spec(D),
        out_shape=jax.ShapeDtypeStruct((1, D), F32), compiler_params=_cp("arbitrary"),
    )(x, dy)


HALO = 32


def _prev_halo_spec(tr, tc, col):
    per = tr // HALO
    return pl.BlockSpec((HALO, tc), lambda *g: (jnp.maximum(g[-1] * per - 1, 0), col(*g)))


def _fill_prev(scr, halo_val, blk_val, i, tr):
    scr[pl.ds(0, HALO), :] = jnp.where(i == 0, 0.0, halo_val)
    scr[pl.ds(HALO, tr), :] = blk_val


def _conv(scr, w_ref, K, tr):
    acc = None
    for k in range(K):
        term = scr[pl.ds(HALO - (K - 1) + k, tr), :] * w_ref[k:k + 1, :]
        acc = term if acc is None else acc + term
    return acc


def _conv_dw(scr, d, o_ref, K, tr):
    for k in range(K):
        o_ref[k:k + 1, :] += jnp.sum(d * scr[pl.ds(HALO - (K - 1) + k, tr), :], axis=0, keepdims=True)


def ssm_conv_fwd(zx, w, b, *, col0, ncols, wcol0, out_dtype, name):
    L = zx.shape[0]
    tr = min(TR, L)
    tc = 1024
    cb, wb = col0 // tc, wcol0 // tc

    def body(x_ref, h_ref, w_ref, b_ref, o_ref, p_ref, scr):
        i = pl.program_id(1)
        _fill_prev(scr, h_ref[...].astype(F32), x_ref[...].astype(F32), i, tr)
        pre = _conv(scr, w_ref, SSM_K, tr) + b_ref[...]
        p_ref[...] = pre.astype(BF16)
        o_ref[...] = (pre * _sig(pre)).astype(out_dtype)

    out = pl.BlockSpec((tr, tc), lambda j, i: (i, j))
    return pl.pallas_call(
        body, name=name, grid=(ncols // tc, L // tr),
        in_specs=[pl.BlockSpec((tr, tc), lambda j, i: (i, j + cb)),
                  _prev_halo_spec(tr, tc, lambda j, i: j + cb),
                  pl.BlockSpec((SSM_K, tc), lambda j, i: (0, j + wb)),
                  pl.BlockSpec((1, tc), lambda j, i: (0, j + wb))],
        out_specs=[out, out],
        out_shape=[jax.ShapeDtypeStruct((L, ncols), out_dtype), jax.ShapeDtypeStruct((L, ncols), BF16)],
        scratch_shapes=[pltpu.VMEM((HALO + tr, tc), F32)],
        compiler_params=_cp("parallel", "parallel"),
    )(zx, zx, w, b)


def ssm_conv_bwd(zx, pre, d, w, *, col0, dcol0, ncols, name):
    L = zx.shape[0]
    tr = min(TR, L)
    tc = 1024
    cb, db_ = col0 // tc, dcol0 // tc
    n = L // tr
    per = tr // HALO
    last = L // HALO - 1

    def body(x_ref, h_ref, p_ref, np_ref, d_ref, nd_ref, w_ref, dx_ref, dw_ref, db_ref, scr, sd):
        i = pl.program_id(1)
        _fill_prev(scr, h_ref[...].astype(F32), x_ref[...].astype(F32), i, tr)

        def dpre(p, dv):
            s = _sig(p)
            return dv * s * (1.0 + p * (1.0 - s))

        dp = dpre(p_ref[...].astype(F32), d_ref[...])
        sd[pl.ds(0, tr), :] = dp
        sd[pl.ds(tr, HALO), :] = jnp.where(i == n - 1, 0.0, dpre(np_ref[...].astype(F32), nd_ref[...]))
        acc = None
        for k in range(SSM_K):
            term = sd[pl.ds(SSM_K - 1 - k, tr), :] * w_ref[k:k + 1, :]
            acc = term if acc is None else acc + term
        dx_ref[...] = acc.astype(BF16)

        @pl.when(i == 0)
        def _():
            dw_ref[...] = jnp.zeros_like(dw_ref)
            db_ref[...] = jnp.zeros_like(db_ref)

        _conv_dw(scr, dp, dw_ref, SSM_K, tr)
        db_ref[...] += jnp.sum(dp, axis=0, keepdims=True)

    nxt = lambda i: jnp.minimum((i + 1) * per, last)
    return pl.pallas_call(
        body, name=name, grid=(ncols // tc, n),
        in_specs=[pl.BlockSpec((tr, tc), lambda j, i: (i, j + cb)),
                  _prev_halo_spec(tr, tc, lambda j, i: j + cb),
                  pl.BlockSpec((tr, tc), lambda j, i: (i, j)),
                  pl.BlockSpec((HALO, tc), lambda j, i: (nxt(i), j)),
                  pl.BlockSpec((tr, tc), lambda j, i: (i, j + db_)),
                  pl.BlockSpec((HALO, tc), lambda j, i: (nxt(i), j + db_)),
                  pl.BlockSpec((SSM_K, tc), lambda j, i: (0, j + db_))],
        out_specs=[pl.BlockSpec((tr, tc), lambda j, i: (i, j)),
                   pl.BlockSpec((SSM_K, tc), lambda j, i: (0, j)),
                   pl.BlockSpec((1, tc), lambda j, i: (0, j))],
        out_shape=[jax.ShapeDtypeStruct((L, ncols), BF16), jax.ShapeDtypeStruct((SSM_K, ncols), F32),
                   jax.ShapeDtypeStruct((1, ncols), F32)],
        scratch_shapes=[pltpu.VMEM((HALO + tr, tc), F32), pltpu.VMEM((tr + HALO, tc), F32)],
        compiler_params=_cp("parallel", "arbitrary"),
    )(zx, zx, pre, pre, d, d, w)


FFN_TC = 1408


def ffn_act_fwd(u, w, b, name):
    L = u.shape[0]
    tr = min(TR, L)
    tc = FFN_TC
    nb = D_FF // tc

    def body(g_ref, hg_ref, v_ref, hv_ref, wg_ref, wv_ref, bg_ref, bv_ref, o_ref, c_ref, sg, sv):
        i = pl.program_id(1)
        _fill_prev(sg, hg_ref[...].astype(F32), g_ref[...].astype(F32), i, tr)
        _fill_prev(sv, hv_ref[...].astype(F32), v_ref[...].astype(F32), i, tr)
        ug = _conv(sg, wg_ref, FFN_K, tr) + bg_ref[...]
        uv = _conv(sv, wv_ref, FFN_K, tr) + bv_ref[...]
        c_ref[0] = ug.astype(BF16)
        c_ref[1] = uv.astype(BF16)
        o_ref[...] = (ug * _sig(ug) * uv).astype(BF16)

    blk = lambda off: pl.BlockSpec((tr, tc), lambda j, i: (i, j + off))
    wsp = lambda off: pl.BlockSpec((FFN_K, tc), lambda j, i: (0, j + off))
    bsp = lambda off: pl.BlockSpec((1, tc), lambda j, i: (0, j + off))
    return pl.pallas_call(
        body, name=name, grid=(nb, L // tr),
        in_specs=[blk(0), _prev_halo_spec(tr, tc, lambda j, i: j),
                  blk(nb), _prev_halo_spec(tr, tc, lambda j, i: j + nb),
                  wsp(0), wsp(nb), bsp(0), bsp(nb)],
        out_specs=[pl.BlockSpec((tr, tc), lambda j, i: (i, j)), pl.BlockSpec((2, tr, tc), lambda j, i: (0, i, j))],
        out_shape=[jax.ShapeDtypeStruct((L, D_FF), BF16), jax.ShapeDtypeStruct((2, L, D_FF), BF16)],
        scratch_shapes=[pltpu.VMEM((HALO + tr, tc), F32), pltpu.VMEM((HALO + tr, tc), F32)],
        compiler_params=_cp("parallel", "parallel"),
    )(u, u, u, u, w, w, b, b)


def ffn_act_bwd(u, c, dact, w, name):
    L = u.shape[0]
    tr = min(TR, L)
    tc = FFN_TC
    nb = D_FF // tc
    n = L // tr
    per = tr // HALO
    last = L // HALO - 1

    def body(g_ref, hg_ref, v_ref, hv_ref, c_ref, nc_ref, da_ref, nda_ref, wg_ref, wv_ref,
             du_ref, dw_ref, db_ref, sg, sv, dg_s, dv_s):
        i = pl.program_id(1)
        _fill_prev(sg, hg_ref[...].astype(F32), g_ref[...].astype(F32), i, tr)
        _fill_prev(sv, hv_ref[...].astype(F32), v_ref[...].astype(F32), i, tr)

        def grads(cg, cv, da):
            s = _sig(cg)
            return da * cv * s * (1.0 + cg * (1.0 - s)), da * cg * s

        dg, dv = grads(c_ref[0].astype(F32), c_ref[1].astype(F32), da_ref[...].astype(F32))
        ndg, ndv = grads(nc_ref[0].astype(F32), nc_ref[1].astype(F32), nda_ref[...].astype(F32))
        at_end = i == n - 1
        for half, (scr, d, nd, x_scr, w_ref) in enumerate(((dg_s, dg, ndg, sg, wg_ref), (dv_s, dv, ndv, sv, wv_ref))):
            scr[pl.ds(0, tr), :] = d
            scr[pl.ds(tr, HALO), :] = jnp.where(at_end, 0.0, nd)
            acc = None
            for k in range(FFN_K):
                term = scr[pl.ds(FFN_K - 1 - k, tr), :] * w_ref[k:k + 1, :]
                acc = term if acc is None else acc + term
            du_ref[half] = acc.astype(BF16)

            @pl.when(i == 0)
            def _():
                dw_ref[half] = jnp.zeros((FFN_K, tc), F32)
                db_ref[half] = jnp.zeros((1, tc), F32)

            for k in range(FFN_K):
                dw_ref[half, k:k + 1, :] += jnp.sum(d * x_scr[pl.ds(HALO - (FFN_K - 1) + k, tr), :], axis=0, keepdims=True)
            db_ref[half] += jnp.sum(d, axis=0, keepdims=True)

    blk = lambda off: pl.BlockSpec((tr, tc), lambda j, i: (i, j + off))
    wsp = lambda off: pl.BlockSpec((FFN_K, tc), lambda j, i: (0, j + off))
    nxt = lambda i: jnp.minimum((i + 1) * per, last)
    return pl.pallas_call(
        body, name=name, grid=(nb, n),
        in_specs=[blk(0), _prev_halo_spec(tr, tc, lambda j, i: j),
                  blk(nb), _prev_halo_spec(tr, tc, lambda j, i: j + nb),
                  pl.BlockSpec((2, tr, tc), lambda j, i: (0, i, j)),
                  pl.BlockSpec((2, HALO, tc), lambda j, i: (0, nxt(i), j)),
                  pl.BlockSpec((tr, tc), lambda j, i: (i, j)),
                  pl.BlockSpec((HALO, tc), lambda j, i: (nxt(i), j)),
                  wsp(0), wsp(nb)],
        out_specs=[pl.BlockSpec((2, tr, tc), lambda j, i: (0, i, j)),
                   pl.BlockSpec((2, FFN_K, tc), lambda j, i: (0, 0, j)),
                   pl.BlockSpec((2, 1, tc), lambda j, i: (0, 0, j))],
        out_shape=[jax.ShapeDtypeStruct((2, L, D_FF), BF16), jax.ShapeDtypeStruct((2, FFN_K, D_FF), F32),
                   jax.ShapeDtypeStruct((2, 1, D_FF), F32)],
        scratch_shapes=[pltpu.VMEM((HALO + tr, tc), F32), pltpu.VMEM((HALO + tr, tc), F32),
                        pltpu.VMEM((tr + HALO, tc), F32), pltpu.VMEM((tr + HALO, tc), F32)],
        compiler_params=_cp("parallel", "arbitrary"),
    )(u, u, u, u, c, c, dact, dact, w, w)


def _ln_stats(c):
    mu = jnp.mean(c, axis=-1, keepdims=True)
    cc = c - mu
    rstd = lax.rsqrt(jnp.mean(cc * cc, axis=-1, keepdims=True) + EPS)
    return cc * rstd, rstd


def cf_fwd(u, dw_w, dw_b, ln_g, ln_b, name):
    L = u.shape[0]
    D = D_MODEL
    tr = min(TR, L)

    def body(a_ref, ha_ref, g_ref, hg_ref, w_ref, b_ref, lg_ref, lb_ref, c_ref, s_ref, scr):
        i = pl.program_id(0)
        glu_h = ha_ref[...].astype(F32) * _sig(hg_ref[...].astype(F32))
        glu = a_ref[...].astype(F32) * _sig(g_ref[...].astype(F32))
        _fill_prev(scr, glu_h, glu, i, tr)
        c = _conv(scr, w_ref, CF_K, tr) + b_ref[...]
        c_ref[...] = c
        xhat, _ = _ln_stats(c)
        ln = xhat * lg_ref[...] + lb_ref[...]
        s_ref[...] = (ln * _sig(ln)).astype(BF16)

    per = tr // HALO
    halo = lambda col: pl.BlockSpec((HALO, D), lambda i: (jnp.maximum(i * per - 1, 0), col))
    return pl.pallas_call(
        body, name=name, grid=(L // tr,),
        in_specs=[pl.BlockSpec((tr, D), lambda i: (i, 0)), halo(0),
                  pl.BlockSpec((tr, D), lambda i: (i, 1)), halo(1),
                  pl.BlockSpec((CF_K, D), lambda i: (0, 0)), _vec_spec(D), _vec_spec(D), _vec_spec(D)],
        out_specs=[_row_spec(tr, D), _row_spec(tr, D)],
        out_shape=[jax.ShapeDtypeStruct((L, D), F32), jax.ShapeDtypeStruct((L, D), BF16)],
        scratch_shapes=[pltpu.VMEM((HALO + tr, D), F32)],
        compiler_params=_cp("parallel"),
    )(u, u, u, u, dw_w, dw_b, ln_g, ln_b)


def cf_bwd_ln(c, ds, ln_g, ln_b, name):
    L, D = c.shape
    tr = min(TR, L)

    def body(c_ref, ds_ref, lg_ref, lb_ref, dc_ref, dg_ref, db_ref):
        xhat, rstd = _ln_stats(c_ref[...])
        ln = xhat * lg_ref[...] + lb_ref[...]
        sg = _sig(ln)
        dln = ds_ref[...].astype(F32) * sg * (1.0 + ln * (1.0 - sg))

        @pl.when(pl.program_id(0) == 0)
        def _():
            dg_ref[...] = jnp.zeros_like(dg_ref)
            db_ref[...] = jnp.zeros_like(db_ref)

        dg_ref[...] += jnp.sum(dln * xhat, axis=0, keepdims=True)
        db_ref[...] += jnp.sum(dln, axis=0, keepdims=True)
        dxh = dln * lg_ref[...]
        dc_ref[...] = rstd * (dxh - jnp.mean(dxh, axis=-1, keepdims=True)
                              - xhat * jnp.mean(dxh * xhat, axis=-1, keepdims=True))

    return pl.pallas_call(
        body, name=name, grid=(L // tr,),
        in_specs=[_row_spec(tr, D), _row_spec(tr, D), _vec_spec(D), _vec_spec(D)],
        out_specs=[_row_spec(tr, D), _vec_spec(D), _vec_spec(D)],
        out_shape=[jax.ShapeDtypeStruct((L, D), F32), jax.ShapeDtypeStruct((1, D), F32),
                   jax.ShapeDtypeStruct((1, D), F32)],
        compiler_params=_cp("arbitrary"),
    )(c, ds, ln_g, ln_b)


def cf_bwd_conv(u, dc, dw_w, name):
    L = u.shape[0]
    D = D_MODEL
    tr = min(TR, L)
    n = L // tr

    def body(a_ref, ha_ref, g_ref, hg_ref, dc_ref, nx_ref, w_ref, du_ref, dw_ref, db_ref, sx, sd):
        i = pl.program_id(0)
        a = a_ref[...].astype(F32)
        sg = _sig(g_ref[...].astype(F32))
        _fill_prev(sx, ha_ref[...].astype(F32) * _sig(hg_ref[...].astype(F32)), a * sg, i, tr)
        dcv = dc_ref[...]
        sd[pl.ds(0, tr), :] = dcv
        sd[pl.ds(tr, HALO), :] = jnp.where(i == n - 1, 0.0, nx_ref[...])
        dglu = None
        for k in range(CF_K):
            term = sd[pl.ds(CF_K - 1 - k, tr), :] * w_ref[k:k + 1, :]
            dglu = term if dglu is None else dglu + term
        du_ref[:, 0:D] = (dglu * sg).astype(BF16)
        du_ref[:, D:2 * D] = (dglu * a * sg * (1.0 - sg)).astype(BF16)

        @pl.when(i == 0)
        def _():
            dw_ref[...] = jnp.zeros_like(dw_ref)
            db_ref[...] = jnp.zeros_like(db_ref)

        _conv_dw(sx, dcv, dw_ref, CF_K, tr)
        db_ref[...] += jnp.sum(dcv, axis=0, keepdims=True)

    per = tr // HALO
    last = L // HALO - 1
    halo = lambda col: pl.BlockSpec((HALO, D), lambda i: (jnp.maximum(i * per - 1, 0), col))
    return pl.pallas_call(
        body, name=name, grid=(n,),
        in_specs=[pl.BlockSpec((tr, D), lambda i: (i, 0)), halo(0),
                  pl.BlockSpec((tr, D), lambda i: (i, 1)), halo(1),
                  _row_spec(tr, D),
                  pl.BlockSpec((HALO, D), lambda i: (jnp.minimum((i + 1) * per, last), 0)),
                  pl.BlockSpec((CF_K, D), lambda i: (0, 0))],
        out_specs=[pl.BlockSpec((tr, 2 * D), lambda i: (i, 0)),
                   pl.BlockSpec((CF_K, D), lambda i: (0, 0)), _vec_spec(D)],
        out_shape=[jax.ShapeDtypeStruct((L, 2 * D), BF16), jax.ShapeDtypeStruct((CF_K, D), F32),
                   jax.ShapeDtypeStruct((1, D), F32)],
        scratch_shapes=[pltpu.VMEM((HALO + tr, D), F32), pltpu.VMEM((tr + HALO, D), F32)],
        compiler_params=_cp("arbitrary"),
    )(u, u, u, u, dc, dc, dw_w)


XA_TR = 512
XA_SCALE = XA_HD ** -0.5


def _xa_probs(qh, kh):
    s = _dot(qh, kh, 1, 1) * XA_SCALE
    p = jnp.exp(s - jnp.max(s, axis=-1, keepdims=True))
    return p / jnp.sum(p, axis=-1, keepdims=True)


def attn_fwd(q, kv, name):
    L, D = q.shape
    tr = min(XA_TR, L)

    def body(q_ref, kv_ref, o_ref):
        for hd in range(XA_HEADS):
            c = slice(hd * XA_HD, (hd + 1) * XA_HD)
            p = _xa_probs(q_ref[:, c], kv_ref[:, c])
            vh = kv_ref[:, D + hd * XA_HD:D + (hd + 1) * XA_HD]
            o_ref[:, c] = _dot(p.astype(BF16), vh).astype(BF16)

    return pl.pallas_call(
        body, name=name, grid=(L // tr,),
        in_specs=[_row_spec(tr, D), pl.BlockSpec((N_MEM, 2 * D), lambda i: (0, 0))],
        out_specs=_row_spec(tr, D), out_shape=jax.ShapeDtypeStruct((L, D), BF16),
        compiler_params=_cp("parallel"),
    )(q, kv)


def attn_bwd(q, kv, do, name):
    L, D = q.shape
    tr = min(XA_TR, L)

    def body(q_ref, kv_ref, do_ref, dq_ref, dkv_ref):
        @pl.when(pl.program_id(0) == 0)
        def _():
            dkv_ref[...] = jnp.zeros_like(dkv_ref)

        for hd in range(XA_HEADS):
            c = slice(hd * XA_HD, (hd + 1) * XA_HD)
            cv = slice(D + hd * XA_HD, D + (hd + 1) * XA_HD)
            qh, kh, vh, doh = q_ref[:, c], kv_ref[:, c], kv_ref[:, cv], do_ref[:, c]
            p = _xa_probs(qh, kh)
            dp = _dot(doh, vh, 1, 1)
            dkv_ref[:, cv] += _dot(p.astype(BF16), doh, 0, 0)
            ds = (p * (dp - jnp.sum(dp * p, axis=-1, keepdims=True)) * XA_SCALE).astype(BF16)
            dq_ref[:, c] = _dot(ds, kh).astype(BF16)
            dkv_ref[:, c] += _dot(ds, qh, 0, 0)

    return pl.pallas_call(
        body, name=name, grid=(L // tr,),
        in_specs=[_row_spec(tr, D), pl.BlockSpec((N_MEM, 2 * D), lambda i: (0, 0)), _row_spec(tr, D)],
        out_specs=[_row_spec(tr, D), pl.BlockSpec((N_MEM, 2 * D), lambda i: (0, 0))],
        out_shape=[jax.ShapeDtypeStruct((L, D), BF16), jax.ShapeDtypeStruct((N_MEM, 2 * D), F32)],
        compiler_params=_cp("arbitrary"),
    )(q, kv, do)


N_PAIRS = N_HEADS // 2
PAIRS_PER_GROUP = N_PAIRS // N_GROUPS
GN = N_GROUPS * D_STATE


def _softplus(x):
    t = jnp.exp(-jnp.abs(x))
    return jnp.maximum(x, 0.0) + jnp.where(t < 1e-4, t * (1.0 - 0.5 * t), jnp.log(1.0 + t))


def _dot3b(m, v, ca=1, cb=0):
    v0, v1, v2 = _split3(v)
    return _dot(m, v0, ca, cb) + _dot(m, v1, ca, cb) + _dot(m, v2, ca, cb)


def ssd_consts():
    h = lax.broadcasted_iota(jnp.int32, (LANE, D_INNER), 0)
    c = lax.broadcasted_iota(jnp.int32, (LANE, D_INNER), 1)
    expand = (c // HEAD_DIM == h).astype(BF16)
    r = lax.broadcasted_iota(jnp.int32, (CHUNK, CHUNK), 0)
    k = lax.broadcasted_iota(jnp.int32, (CHUNK, CHUNK), 1)
    tri = (k <= r).astype(BF16)
    return expand, tri


def _ssd_common(dtr_ref, prm_ref, e_ref, tri_ref):
    lane = lax.broadcasted_iota(jnp.int32, (CHUNK, LANE), 1)
    valid = lane < N_HEADS
    A = -jnp.exp(prm_ref[1:2, :])
    pre = dtr_ref[...] + prm_ref[0:1, :]
    dt = jnp.where(valid, _softplus(pre), 0.0)
    cs = _dot3b(tri_ref[...], dt * A)
    E = e_ref[...]
    dt_x = _dot3(dt, E)
    cs_x = _dot3(cs, E)
    csl_x = cs_x[CHUNK - 1:CHUNK, :]
    return dict(valid=valid, A=A, pre=pre, dt=dt, cs=cs, csT=cs.T, dt_x=dt_x, ecs_x=jnp.exp(cs_x),
                dend_x=jnp.exp(csl_x - cs_x), cd_x=jnp.exp(csl_x), D_x=_dot3(prm_ref[...], E)[2:3, :])


def ssd_fwd(xs, bc, dtr, zx, prm, ng, name):
    L = xs.shape[0]
    nc = L // CHUNK
    expand, tri = ssd_consts()

    def body(xs_ref, bc_ref, dtr_ref, z_ref, prm_ref, ng_ref, e_ref, tri_ref, y_ref, yn_ref, st_ref, state):
        @pl.when(pl.program_id(0) == 0)
        def _():
            state[...] = jnp.zeros_like(state)

        q = _ssd_common(dtr_ref, prm_ref, e_ref, tri_ref)
        cs, csT = q["cs"], q["csT"]
        xs_v = xs_ref[...]
        X = xs_v * q["dt_x"]
        Xb = X.astype(BF16)
        Xd = (X * q["dend_x"]).astype(BF16)
        ii = lax.broadcasted_iota(jnp.int32, (CHUNK, CHUNK), 0)
        jj = lax.broadcasted_iota(jnp.int32, (CHUNK, CHUNK), 1)
        tril = jj <= ii
        first = jj < HEAD_DIM
        for g in range(N_GROUPS):
            Bg = bc_ref[:, g * D_STATE:(g + 1) * D_STATE]
            Cg = bc_ref[:, GN + g * D_STATE:GN + (g + 1) * D_STATE]
            S = _dot(Cg, Bg, 1, 1)
            for pr in range(PAIRS_PER_GROUP):
                pair = g * PAIRS_PER_GROUP + pr
                cols = slice(pair * LANE, (pair + 1) * LANE)
                Xp = Xb[:, cols]
                ys = []
                for h in (2 * pair, 2 * pair + 1):
                    seg = cs[:, h:h + 1] - csT[h:h + 1, :]
                    M = (S * jnp.exp(jnp.where(tril, seg, NEG))).astype(BF16)
                    ys.append(_dot(M, Xp))
                prevT = state[pair]
                st_ref[0, pair] = prevT
                yoff = _dot(Cg, prevT.astype(BF16)) * q["ecs_x"][:, cols]
                y_ref[:, cols] = jnp.where(first, ys[0], ys[1]) + yoff + xs_v[:, cols] * q["D_x"][:, cols]
                state[pair] = prevT * q["cd_x"][:, cols] + _dot(Bg, Xd[:, cols], 0, 0)
        z = z_ref[...].astype(F32)
        gt = y_ref[...] * z * _sig(z)
        yn_ref[...] = (gt * _rms(gt) * ng_ref[...]).astype(BF16)

    row = lambda w: pl.BlockSpec((CHUNK, w), lambda c: (c, 0))
    const = lambda a: pl.BlockSpec(a.shape, lambda c: (0,) * a.ndim)
    return pl.pallas_call(
        body, name=name, grid=(nc,),
        in_specs=[row(D_INNER), row(2 * GN), row(LANE), row(D_INNER), const(prm), const(ng), const(expand), const(tri)],
        out_specs=[row(D_INNER), row(D_INNER), pl.BlockSpec((1, N_PAIRS, D_STATE, LANE), lambda c: (c, 0, 0, 0))],
        out_shape=[jax.ShapeDtypeStruct((L, D_INNER), F32), jax.ShapeDtypeStruct((L, D_INNER), BF16),
                   jax.ShapeDtypeStruct((nc, N_PAIRS, D_STATE, LANE), F32)],
        scratch_shapes=[pltpu.VMEM((N_PAIRS, D_STATE, LANE), F32)],
        compiler_params=_cp("arbitrary"),
    )(xs, bc, dtr, zx, prm, ng, expand, tri)


def ssd_bwd(dyn, y, zx, xs, bc, dtr, st, prm, ng, name):
    L = xs.shape[0]
    nc = L // CHUNK
    expand, tri = ssd_consts()

    def body(dyn_ref, y_ref, z_ref, xs_ref, bc_ref, dtr_ref, st_ref, prm_ref, ng_ref, e_ref, tri_ref,
             dxbc_ref, dz_ref, ddtr_ref, dng_ref, dprm_ref, dstate, g_cs, g_q, dX, g_row):
        step = pl.program_id(0)

        @pl.when(step == 0)
        def _():
            dstate[...] = jnp.zeros_like(dstate)
            dng_ref[...] = jnp.zeros_like(dng_ref)
            dprm_ref[...] = jnp.zeros_like(dprm_ref)
            g_row[...] = jnp.zeros_like(g_row)

        q = _ssd_common(dtr_ref, prm_ref, e_ref, tri_ref)
        cs, csT, E = q["cs"], q["csT"], e_ref[...]
        xs_v = xs_ref[...]
        X = xs_v * q["dt_x"]
        Xb = X.astype(BF16)
        Xd_f = X * q["dend_x"]
        Xd = Xd_f.astype(BF16)

        yv = y_ref[...]
        z = z_ref[...].astype(F32)
        sz = _sig(z)
        silu = z * sz
        gt = yv * silu
        r = _rms(gt)
        gn = gt * r
        dyn_v = dyn_ref[...]
        dng_ref[...] += jnp.sum(dyn_v * gn, axis=0, keepdims=True)
        dgn = dyn_v * ng_ref[...]
        dgt = r * (dgn - gn * jnp.mean(dgn * gn, axis=-1, keepdims=True))
        dY = dgt * silu
        dz_ref[...] = (dgt * yv * sz * (1.0 + z * (1.0 - sz))).astype(BF16)
        dYb = dY.astype(BF16)
        g_row[1:2, :] += jnp.sum(dY * xs_v, axis=0, keepdims=True)

        ii = lax.broadcasted_iota(jnp.int32, (CHUNK, CHUNK), 0)
        jj = lax.broadcasted_iota(jnp.int32, (CHUNK, CHUNK), 1)
        tril = jj <= ii
        triu = jj >= ii
        first = jj < HEAD_DIM
        lane_row = lax.broadcasted_iota(jnp.int32, (1, LANE), 1)
        sub_col = lax.broadcasted_iota(jnp.int32, (CHUNK, 1), 0)
        dcs_col = jnp.zeros((CHUNK, LANE), F32)
        dcs_rowT = jnp.zeros((LANE, CHUNK), F32)
        for g in range(N_GROUPS):
            Bg = bc_ref[:, g * D_STATE:(g + 1) * D_STATE]
            Cg = bc_ref[:, GN + g * D_STATE:GN + (g + 1) * D_STATE]
            S = _dot(Cg, Bg, 1, 1)
            ST = _dot(Bg, Cg, 1, 1)
            dS = jnp.zeros((CHUNK, CHUNK), F32)
            dCg = jnp.zeros((CHUNK, D_STATE), F32)
            dBg = jnp.zeros((CHUNK, D_STATE), F32)
            for pr in range(PAIRS_PER_GROUP):
                pair = g * PAIRS_PER_GROUP + pr
                cols = slice(pair * LANE, (pair + 1) * LANE)
                Xp = Xb[:, cols]
                dYp_f = dY[:, cols]
                dYp = dYb[:, cols]
                prevT = st_ref[0, pair]
                prevTb = prevT.astype(BF16)
                dst = dstate[pair]
                dstb = dst.astype(BF16)
                ecs_p = q["ecs_x"][:, cols]
                g_cs[:, cols] = dYp_f * (_dot(Cg, prevTb) * ecs_p)
                dWb = (dYp_f * ecs_p).astype(BF16)
                dprev = dst * q["cd_x"][:, cols] + _dot(Cg, dWb, 0, 0)
                dCg = dCg + _dot(dWb, prevTb, 1, 1)
                g_row[0:1, cols] = jnp.sum(dst * prevT, axis=0, keepdims=True)
                dXp = None
                for hh, h in enumerate((2 * pair, 2 * pair + 1)):
                    mine = first if hh == 0 else jnp.logical_not(first)
                    seg = cs[:, h:h + 1] - csT[h:h + 1, :]
                    lam = jnp.exp(jnp.where(tril, seg, NEG))
                    dM = _dot(jnp.where(mine, dYp, jnp.zeros_like(dYp)), Xp, 1, 1)
                    dS = dS + dM * lam
                    Gm = dM * (S * lam)
                    dcs_col = dcs_col + jnp.sum(Gm, axis=1, keepdims=True) * (lane_row == h).astype(F32)
                    dcs_rowT = dcs_rowT + (sub_col == h).astype(F32) * jnp.sum(Gm, axis=0, keepdims=True)
                    MT = (ST * jnp.exp(jnp.where(triu, -seg, NEG))).astype(BF16)
                    t = _dot(MT, dYp)
                    dXp = t if dXp is None else jnp.where(first, dXp, t)
                dXd = _dot(Bg, dstb)
                dBg = dBg + _dot(Xd[:, cols], dstb, 1, 1)
                g_q[:, cols] = dXd * Xd_f[:, cols]
                dX[:, cols] = dXp + dXd * q["dend_x"][:, cols]
                dstate[pair] = dprev
            dSb = dS.astype(BF16)
            dxbc_ref[:, D_INNER + g * D_STATE:D_INNER + (g + 1) * D_STATE] = dBg + _dot(dSb, Cg, 0, 0)
            dxbc_ref[:, D_INNER + GN + g * D_STATE:D_INNER + GN + (g + 1) * D_STATE] = dCg + _dot(dSb, Bg)
        dXv = dX[...]
        dxbc_ref[:, 0:D_INNER] = q["D_x"] * dY + dXv * q["dt_x"]
        r_dt = _dot3(dXv * xs_v, E, 1, 1)
        r_cs = _dot3(g_cs[...], E, 1, 1)
        r_q = _dot3(g_q[...], E, 1, 1)
        r_row = _dot3(g_row[...], E, 1, 1)
        cd = jnp.exp(cs[CHUNK - 1:CHUNK, :])
        dcs_last = jnp.sum(r_q, axis=0, keepdims=True) + r_row[0:1, :] * cd
        dcs = r_cs - r_q + dcs_col - dcs_rowT.T + jnp.where(sub_col == CHUNK - 1, dcs_last, 0.0)
        da = _dot3b(tri_ref[...], dcs, 0, 0)
        dpre = jnp.where(q["valid"], (r_dt + da * q["A"]) * _sig(q["pre"]), 0.0)
        ddtr_ref[...] = dpre
        dprm_ref[0:1, :] += jnp.sum(dpre, axis=0, keepdims=True)
        dprm_ref[1:2, :] += jnp.sum(da * q["dt"], axis=0, keepdims=True) * q["A"]
        dprm_ref[2:3, :] = r_row[1:2, :]

    rev = lambda w: pl.BlockSpec((CHUNK, w), lambda c: (nc - 1 - c, 0))
    const = lambda a: pl.BlockSpec(a.shape, lambda c: (0,) * a.ndim)
    return pl.pallas_call(
        body, name=name, grid=(nc,),
        in_specs=[rev(D_INNER), rev(D_INNER), rev(D_INNER), rev(D_INNER), rev(2 * GN), rev(LANE),
                  pl.BlockSpec((1, N_PAIRS, D_STATE, LANE), lambda c: (nc - 1 - c, 0, 0, 0)),
                  const(prm), const(ng), const(expand), const(tri)],
        out_specs=[rev(CONV_DIM), rev(D_INNER), rev(LANE),
                   pl.BlockSpec((1, D_INNER), lambda c: (0, 0)), pl.BlockSpec((8, LANE), lambda c: (0, 0))],
        out_shape=[jax.ShapeDtypeStruct((L, CONV_DIM), F32), jax.ShapeDtypeStruct((L, D_INNER), BF16),
                   jax.ShapeDtypeStruct((L, LANE), F32), jax.ShapeDtypeStruct((1, D_INNER), F32),
                   jax.ShapeDtypeStruct((8, LANE), F32)],
        scratch_shapes=[pltpu.VMEM((N_PAIRS, D_STATE, LANE), F32), pltpu.VMEM((CHUNK, D_INNER), F32),
                        pltpu.VMEM((CHUNK, D_INNER), F32), pltpu.VMEM((CHUNK, D_INNER), F32),
                        pltpu.VMEM((8, D_INNER), F32)],
        compiler_params=_cp("arbitrary"),
    )(dyn, y, zx, xs, bc, dtr, st, prm, ng, expand, tri)


def _ssd_weights(W, j):
    w_in = W["ssm_in_w"][j]
    nzx = D_INNER + CONV_DIM
    wdt = jnp.pad(w_in[:, nzx:], ((0, 0), (0, LANE - N_HEADS)))
    prm = jnp.zeros((8, LANE), F32)
    prm = prm.at[0, :N_HEADS].set(W["ssm_dt_bias"][j]).at[1, :N_HEADS].set(W["ssm_A_log"][j])
    prm = prm.at[2, :N_HEADS].set(W["ssm_D"][j])
    return dict(wdt=wdt, cw=W["ssm_conv_w"][j], cb=W["ssm_conv_b"][j].reshape(1, CONV_DIM), prm=prm,
                ng=W["ssm_norm_g"][j].reshape(1, D_INNER))


def ssd_layer_fwd(h, W, j, tag):
    p = _ssd_weights(W, j)
    zx = mm(h, W["ssm_in_w"], layer=j, b_n=D_INNER + CONV_DIM, out_dtype=BF16, name=f"{tag}_zx")
    dtr = mm(h, p["wdt"], name=f"{tag}_dt")
    xs, pre_x = ssm_conv_fwd(zx, p["cw"], p["cb"], col0=D_INNER, ncols=D_INNER, wcol0=0, out_dtype=F32,
                             name=f"{tag}_convx")
    bc, pre_bc = ssm_conv_fwd(zx, p["cw"], p["cb"], col0=2 * D_INNER, ncols=2 * GN, wcol0=D_INNER, out_dtype=BF16,
                              name=f"{tag}_convbc")
    y, yn, st = ssd_fwd(xs, bc, dtr, zx, p["prm"], p["ng"], name=f"{tag}_scan")
    f = mm(yn, W["ssm_out_w"], layer=j, name=f"{tag}_out")
    return f, dict(h=h, zx=zx, dtr=dtr, xs=xs, bc=bc, pre_x=pre_x, pre_bc=pre_bc, y=y, yn=yn, st=st, p=p)


def ssd_layer_bwd(df, ctx, W, GB, j, tag):
    p = ctx["p"]
    h = ctx["h"]
    dyn = mm(df, W["ssm_out_w"], layer=j, tb=True, name=f"{tag}_b_dyn")
    GB["ssm_out_w"] = mm(ctx["yn"], df, ta=True, into=(GB["ssm_out_w"], j, 0), name=f"{tag}_b_gwo")
    dxbc, dz, ddtr, dng, dprm = ssd_bwd(dyn, ctx["y"], ctx["zx"], ctx["xs"], ctx["bc"], ctx["dtr"], ctx["st"],
                                        p["prm"], p["ng"], name=f"{tag}_b_scan")
    dx1, dcw1, dcb1 = ssm_conv_bwd(ctx["zx"], ctx["pre_x"], dxbc, p["cw"], col0=D_INNER, dcol0=0, ncols=D_INNER,
                                   name=f"{tag}_b_convx")
    dx2, dcw2, dcb2 = ssm_conv_bwd(ctx["zx"], ctx["pre_bc"], dxbc, p["cw"], col0=2 * D_INNER, dcol0=D_INNER,
                                   ncols=2 * GN, name=f"{tag}_b_convbc")
    dh = mm(dz, W["ssm_in_w"], layer=j, tb=True, b_k0=0, name=f"{tag}_b_dh1")
    dh = mm(dx1, W["ssm_in_w"], layer=j, tb=True, b_k0=D_INNER, acc=dh, name=f"{tag}_b_dh2")
    dh = mm(dx2, W["ssm_in_w"], layer=j, tb=True, b_k0=2 * D_INNER, acc=dh, name=f"{tag}_b_dh3")
    dh = mm(ddtr, p["wdt"], tb=True, acc=dh, name=f"{tag}_b_dh4")
    g_in = jnp.concatenate([mm(h, dz, ta=True, out_dtype=BF16, name=f"{tag}_b_gz"),
                            mm(h, dx1, ta=True, out_dtype=BF16, name=f"{tag}_b_gx"),
                            mm(h, dx2, ta=True, out_dtype=BF16, name=f"{tag}_b_gbc"),
                            mm(h, ddtr, ta=True, out_dtype=BF16, name=f"{tag}_b_gdt")[:, :N_HEADS]], axis=1)
    return dh, dict(ssm_in_w=g_in, ssm_conv_w=jnp.concatenate([dcw1, dcw2], axis=1),
                    ssm_conv_b=jnp.concatenate([dcb1, dcb2], axis=1)[0], ssm_dt_bias=dprm[0, :N_HEADS],
                    ssm_A_log=dprm[1, :N_HEADS], ssm_D=dprm[2, :N_HEADS], ssm_norm_g=dng[0])


def cf_layer_fwd(h, W, j, tag):
    u = mm(h, W["cf_pw1_w"], layer=j, bias=W["cf_pw1_b"][j], out_dtype=BF16, name=f"{tag}_pw1")
    c, s = cf_fwd(u, W["cf_dw_w"][j], W["cf_dw_b"][j].reshape(1, -1), W["cf_ln_g"][j].reshape(1, -1),
                  W["cf_ln_b"][j].reshape(1, -1), name=f"{tag}_conv")
    f = mm(s, W["cf_pw2_w"], layer=j, bias=W["cf_pw2_b"][j], name=f"{tag}_pw2")
    return f, dict(h=h, u=u, c=c, s=s)


def cf_layer_bwd(df, ctx, W, GB, j, tag):
    h = ctx["h"]
    ds = mm(df, W["cf_pw2_w"], layer=j, tb=True, name=f"{tag}_b_ds")
    GB["cf_pw2_w"] = mm(ctx["s"], df, ta=True, into=(GB["cf_pw2_w"], j, 0), name=f"{tag}_b_gpw2")
    g_b2 = colsum(df, name=f"{tag}_b_gb2")
    dc, dlg, dlb = cf_bwd_ln(ctx["c"], ds, W["cf_ln_g"][j].reshape(1, -1), W["cf_ln_b"][j].reshape(1, -1),
                             name=f"{tag}_b_ln")
    du, ddw, ddb = cf_bwd_conv(ctx["u"], dc, W["cf_dw_w"][j], name=f"{tag}_b_conv")
    dh = mm(du, W["cf_pw1_w"], layer=j, tb=True, name=f"{tag}_b_dh")
    GB["cf_pw1_w"] = mm(h, du, ta=True, into=(GB["cf_pw1_w"], j, 0), name=f"{tag}_b_gpw1")
    g_b1 = colsum(du, name=f"{tag}_b_gb1")
    return dh, dict(cf_pw1_b=g_b1[0], cf_dw_w=ddw, cf_dw_b=ddb[0], cf_ln_g=dlg[0], cf_ln_b=dlb[0], cf_pw2_b=g_b2[0])


def xa_layer_fwd(h, mem, W, i, tag):
    m = norm_fwd(mem, W["xa_mem_g"][i], name=f"{tag}_memnorm")
    kv = mm(m, W["xa_kv_w"], layer=i, out_dtype=BF16, name=f"{tag}_kv")
    q = mm(h, W["xa_q_w"], layer=i, out_dtype=BF16, name=f"{tag}_q")
    o = attn_fwd(q, kv, name=f"{tag}_attn")
    f = mm(o, W["xa_o_w"], layer=i, name=f"{tag}_o")
    return f, dict(h=h, m=m, kv=kv, q=q, o=o)


def xa_layer_bwd(df, ctx, mem, W, GB, i, tag):
    h = ctx["h"]
    do = mm(df, W["xa_o_w"], layer=i, tb=True, out_dtype=BF16, name=f"{tag}_b_do")
    GB["xa_o_w"] = mm(ctx["o"], df, ta=True, into=(GB["xa_o_w"], i, 0), name=f"{tag}_b_go")
    dq, dkv = attn_bwd(ctx["q"], ctx["kv"], do, name=f"{tag}_b_attn")
    dh = mm(dq, W["xa_q_w"], layer=i, tb=True, name=f"{tag}_b_dh")
    GB["xa_q_w"] = mm(h, dq, ta=True, into=(GB["xa_q_w"], i, 0), name=f"{tag}_b_gq")
    GB["xa_kv_w"] = mm(ctx["m"], dkv, ta=True, into=(GB["xa_kv_w"], i, 0), name=f"{tag}_b_gkv")
    dm = mm(dkv, W["xa_kv_w"], layer=i, tb=True, name=f"{tag}_b_dm")
    g_mg = norm_dg(mem, dm, name=f"{tag}_b_gmem")
    return dh, dict(xa_mem_g=g_mg[0])


def ffn_layer_fwd(h, W, i, tag):
    cw, cb = W["ffn_conv_w"][i], W["ffn_conv_b"][i].reshape(1, -1)
    u = mm(h, W["ffn_in_w"], layer=i, out_dtype=BF16, name=f"{tag}_in")
    act, c = ffn_act_fwd(u, cw, cb, name=f"{tag}_act")
    f = mm(act, W["ffn_out_w"], layer=i, name=f"{tag}_out")
    return f, dict(h=h, u=u, c=c, act=act)


def ffn_layer_bwd(df, ctx, W, GB, i, tag):
    h = ctx["h"]
    dact = mm(df, W["ffn_out_w"], layer=i, tb=True, out_dtype=BF16, name=f"{tag}_b_dact")
    GB["ffn_out_w"] = mm(ctx["act"], df, ta=True, into=(GB["ffn_out_w"], i, 0), name=f"{tag}_b_gout")
    du, dcw, dcb = ffn_act_bwd(ctx["u"], ctx["c"], dact, W["ffn_conv_w"][i], name=f"{tag}_b_act")
    dh = None
    for half in range(2):
        dh = mm(du, W["ffn_in_w"], a_idx=half, layer=i, tb=True, b_k0=half * D_FF, acc=dh, name=f"{tag}_b_dh{half}")
        GB["ffn_in_w"] = mm(h, du, ta=True, layer=half, into=(GB["ffn_in_w"], i, half * D_FF), name=f"{tag}_b_gin{half}")
    cat = lambda a: jnp.concatenate([a[0], a[1]], axis=-1)
    return dh, dict(ffn_conv_w=cat(dcw), ffn_conv_b=cat(dcb)[0])


def _layer_weights(i):
    j = i // 2
    mixer = [("ssm_in_w", j), ("ssm_out_w", j)] if i % 2 == 0 else [("cf_pw1_w", j), ("cf_pw2_w", j)]
    return mixer + [(n, i) for n in ("xa_q_w", "xa_kv_w", "xa_o_w", "ffn_in_w", "ffn_out_w")]


def local_step(x, mem, target, W, fetch=None, layer_done=None):
    subs = [(i, s) for i in range(DEPTH) for s in range(3)]
    ng = W["norm_g"]

    def fwd(i, s, h):
        tag = f"l{i}s{s}"
        if s == 0:
            return ssd_layer_fwd(h, W, i // 2, tag) if i % 2 == 0 else cf_layer_fwd(h, W, i // 2, tag)
        if s == 1:
            return xa_layer_fwd(h, mem, W, i, tag)
        return ffn_layer_fwd(h, W, i, tag)

    GB = {}

    def bwd(i, s, df, ctx):
        tag = f"l{i}s{s}"
        if s == 0:
            return (ssd_layer_bwd if i % 2 == 0 else cf_layer_bwd)(df, ctx, W, GB, i // 2, tag)
        if s == 1:
            return xa_layer_bwd(df, ctx, mem, W, GB, i, tag)
        return ffn_layer_bwd(df, ctx, W, GB, i, tag)

    h = norm_fwd(x, ng[0, 0], name="norm0")
    saved = []
    dxp = loss = None
    for k, (i, s) in enumerate(subs):
        if s == 0 and fetch is not None:
            fetch(i, x)
        f, ctx = fwd(i, s, h)
        saved.append((x, f, ctx))
        if k + 1 < len(subs):
            ni, ns = subs[k + 1]
            x, h = bnd_fwd(x, f, ng[i, 2 * s + 1], ng[ni, 2 * ns], name=f"bnd{k}")
        else:
            dxp, loss = final_fwd(x, f, ng[i, 2 * s + 1], target, name="final")

    for n in BIG:
        if n != "ssm_in_w":
            GB[n] = jnp.zeros((len(W[n]), *W[n][0].shape), BF16)
    grads = {}

    def put(name, idx, val):
        grads.setdefault(name, {})[idx] = val

    i, s = subs[-1]
    top = bnd_bwd(dxp, post=(saved[-1][1], ng[i, 2 * s + 1]), name="bbnd_top")
    put("norm_g", (i, 2 * s + 1), top["dgpost"][0])
    df = top["df"]
    for k in range(len(subs) - 1, -1, -1):
        i, s = subs[k]
        xk, _, ctx = saved[k]
        dh, gw = bwd(i, s, df, ctx)
        for name, val in gw.items():
            put(name, i // 2 if name.startswith(("ssm_", "cf_")) else i, val)
        if k > 0:
            pi, ps = subs[k - 1]
            r = bnd_bwd(dxp, pre=(xk, ng[i, 2 * s], dh), post=(saved[k - 1][1], ng[pi, 2 * ps + 1]), name=f"bbnd{k}")
            put("norm_g", (pi, 2 * ps + 1), r["dgpost"][0])
            df = r["df"]
        else:
            r = bnd_bwd(dxp, pre=(xk, ng[i, 2 * s], dh), name="bbnd0")
        put("norm_g", (i, 2 * s), r["dgpre"][0])
        dxp = r["dx"]
        if s == 0 and layer_done is not None:
            layer_done(i, GB, grads["ssm_in_w"].pop(i // 2) if i % 2 == 0 else None)

    out = {} if layer_done is not None else dict(GB)
    for name, d in grads.items():
        if name == "norm_g":
            out[name] = jnp.stack([jnp.stack([d[(i, t)] for t in range(6)]) for i in range(DEPTH)])
        elif d:
            out[name] = jnp.stack([d[j] for j in sorted(d)])
    return loss, dxp, out


ANY = pl.BlockSpec(memory_space=pl.ANY)


def _pos():
    return lax.axis_index("x"), lax.axis_index("y"), lax.axis_index("c")


def all_gather(shard, name):
    R, C = shard.shape

    def body(x_ref, out_ref, send_sems, recv_sems, local_sem):
        x, y, c = _pos()
        me, sibling = (x, y, c), (x, y, 1 - c)
        chips = [(1 - x, y), (x, 1 - y), (1 - x, 1 - y)]

        def slot(px, py, pc):
            return out_ref.at[4 * px + 2 * py + pc]

        def copy(k, block, to, src=None):
            return pltpu.make_async_remote_copy(
                src_ref=slot(*block) if src is None else src, dst_ref=slot(*block),
                send_sem=send_sems.at[k], recv_sem=recv_sems.at[k], device_id=to, device_id_type=MESH)

        mine = pltpu.make_async_copy(x_ref, slot(*me), local_sem)
        mine.start()
        first = [copy(0, me, sibling, src=x_ref)]
        first += [copy(1 + j, me, (*chip, c), src=x_ref) for j, chip in enumerate(chips)]
        for cp in first:
            cp.start()
        passed = [copy(4 + j, (*chip, c), sibling) for j, chip in enumerate(chips)]
        for j, chip in enumerate(chips):
            copy(1 + j, (*chip, c), me).wait_recv()
            passed[j].start()
        copy(0, sibling, me).wait_recv()
        for j, chip in enumerate(chips):
            copy(4 + j, (*chip, 1 - c), me).wait_recv()
        for cp in first + passed:
            cp.wait_send()
        mine.wait()

    return pl.pallas_call(
        body, name=name, out_shape=jax.ShapeDtypeStruct((N_DEV, R, C), shard.dtype),
        in_specs=[ANY], out_specs=ANY,
        scratch_shapes=[pltpu.SemaphoreType.DMA((7,)), pltpu.SemaphoreType.DMA((7,)), pltpu.SemaphoreType.DMA(())],
    )(shard)


def _win(ref, kind, k, a, b):
    if kind == "lead":
        return ref.at[k]
    if kind == "row":
        return ref.at[:, pl.ds(pl.multiple_of(k * a, 16), a), :]
    return ref.at[:, :, pl.ds(pl.multiple_of(k * b, LANE), b)]


def _full_shape(shard_shape, kind):
    n, a, b = shard_shape
    return {"lead": (N_DEV, n, a, b), "row": (n, N_DEV * a, b), "col": (n, a, N_DEV * b)}[kind]


HBM = pl.BlockSpec(memory_space=pltpu.HBM)
SEMS = pl.BlockSpec(memory_space=pltpu.SEMAPHORE)
DATAFLOW = pltpu.SideEffectType.DATAFLOW_SIDE_EFFECTING
N_PEER = N_DEV - 1


def _in_hbm(a):
    return pltpu.with_memory_space_constraint(a, pltpu.HBM)


def _peer(x, y, c, r):
    return ((1 - x) if r & 4 else x, (1 - y) if r & 2 else y, (1 - c) if r & 1 else c)


def _win2(ref, kind, k, a, b):
    if kind == "lead":
        return ref.at[k]
    if kind == "row":
        return ref.at[pl.ds(pl.multiple_of(k * a, 16), a), :]
    return ref.at[:, pl.ds(pl.multiple_of(k * b, LANE), b)]


def _zone_shape(kind, a, b):
    return {"lead": (N_DEV, a, b), "row": (N_DEV * a, b), "col": (a, N_DEV * b)}[kind]


def gather_start(shards, items, after, name):
    ns, nz, na = len(shards), len(items), len(after)
    zones = [lax.empty(_zone_shape(kind, a, b), shards[w].dtype) for w, l, kind, a, b in items]

    def body(*refs):
        x_refs = refs[:ns]
        send_sems, recv_sems, local_sems = refs[ns + nz + na:ns + nz + na + 3]
        z_refs = refs[ns + nz + na + 3 + ns:ns + nz + na + 3 + ns + nz]
        token = refs[-1]
        x, y, c = _pos()
        me = 4 * x + 2 * y + c
        for t, (w, l, kind, a, b) in enumerate(items):
            mine = _win2(z_refs[t], kind, me, a, b)
            pltpu.make_async_copy(x_refs[w].at[l], mine, local_sems.at[t]).start()
            for r in range(1, N_DEV):
                pltpu.make_async_remote_copy(
                    src_ref=x_refs[w].at[l], dst_ref=mine,
                    send_sem=send_sems.at[N_PEER * t + r - 1], recv_sem=recv_sems.at[N_PEER * t + r - 1],
                    device_id=_peer(x, y, c, r), device_id_type=MESH).start()
        token[...] = jnp.zeros_like(token)

    n_sem = N_PEER * nz
    outs = pl.pallas_call(
        body, name=name,
        out_shape=(pltpu.SemaphoreType.DMA((n_sem,)), pltpu.SemaphoreType.DMA((n_sem,)), pltpu.SemaphoreType.DMA((nz,)),
                   *[pltpu.HBM(s.shape, s.dtype) for s in shards], *[pltpu.HBM(z.shape, z.dtype) for z in zones],
                   jax.ShapeDtypeStruct((8, LANE), F32)),
        in_specs=[HBM] * (ns + nz) + [pl.BlockSpec(memory_space=pl.ANY)] * na,
        out_specs=(SEMS, SEMS, SEMS, *[HBM] * (ns + nz), pl.BlockSpec(memory_space=pltpu.VMEM)),
        input_output_aliases={i: 3 + i for i in range(ns + nz)},
        compiler_params=pltpu.CompilerParams(has_side_effects=DATAFLOW),
    )(*[_in_hbm(s) for s in shards], *[_in_hbm(z) for z in zones], *after)
    return outs[:3], list(outs[3:3 + ns]), list(outs[3 + ns:3 + ns + nz]), outs[-1]


def gather_wait(zones, idx, items, sems, after, keep, name):
    nz, nk = len(zones), len(keep)

    def body(*refs):
        z_refs = refs[:nz]
        send_sems, recv_sems, local_sems = refs[nz:nz + 3]
        x, y, c = _pos()
        me = 4 * x + 2 * y + c
        for z_ref, t in zip(z_refs, idx):
            w, l, kind, a, b = items[t]
            mine = _win2(z_ref, kind, me, a, b)
            pltpu.make_async_copy(mine, mine, local_sems.at[t]).wait()
            for r in range(1, N_DEV):
                peer = _peer(x, y, c, r)
                cp = pltpu.make_async_remote_copy(
                    src_ref=mine, dst_ref=_win2(z_ref, kind, 4 * peer[0] + 2 * peer[1] + peer[2], a, b),
                    send_sem=send_sems.at[N_PEER * t + r - 1], recv_sem=recv_sems.at[N_PEER * t + r - 1],
                    device_id=peer, device_id_type=MESH)
                cp.wait_send()
                cp.wait_recv()

    outs = pl.pallas_call(
        body, name=name, out_shape=tuple(pltpu.HBM(z.shape, z.dtype) for z in zones),
        in_specs=[HBM] * nz + [SEMS] * 3 + [pl.BlockSpec(memory_space=pl.ANY)] * (1 + nk),
        out_specs=tuple([HBM] * nz), input_output_aliases={i: i for i in range(nz)},
        compiler_params=pltpu.CompilerParams(has_side_effects=DATAFLOW),
    )(*zones, *sems, after, *keep)
    return list(outs)


def gather_now(shards, kinds, name):
    nw = len(shards)
    geo = [s.shape[1:] for s in shards]

    def body(*refs):
        x_refs, o_refs = refs[:nw], refs[nw:2 * nw]
        send_sems, recv_sems, local_sems = refs[2 * nw:]
        x, y, c = _pos()
        me, sibling = (x, y, c), (x, y, 1 - c)
        chips = [(1 - x, y), (x, 1 - y), (1 - x, 1 - y)]

        def slot(w, px, py, pc):
            return _win(o_refs[w], kinds[w], 4 * px + 2 * py + pc, *geo[w])

        def copy(w, k, block, to, src=None):
            return pltpu.make_async_remote_copy(
                src_ref=slot(w, *block) if src is None else src, dst_ref=slot(w, *block),
                send_sem=send_sems.at[7 * w + k], recv_sem=recv_sems.at[7 * w + k], device_id=to, device_id_type=MESH)

        mine = [pltpu.make_async_copy(x_refs[w], slot(w, *me), local_sems.at[w]) for w in range(nw)]
        for cp in mine:
            cp.start()
        first = []
        for w in range(nw):
            first.append(copy(w, 0, me, sibling, src=x_refs[w]))
            first += [copy(w, 1 + j, me, (*chip, c), src=x_refs[w]) for j, chip in enumerate(chips)]
        for cp in first:
            cp.start()
        passed = []
        for w in range(nw):
            for j, chip in enumerate(chips):
                copy(w, 1 + j, (*chip, c), me).wait_recv()
                cp = copy(w, 4 + j, (*chip, c), sibling)
                cp.start()
                passed.append(cp)
        for w in range(nw):
            copy(w, 0, sibling, me).wait_recv()
            for j, chip in enumerate(chips):
                copy(w, 4 + j, (*chip, 1 - c), me).wait_recv()
        for cp in first + passed:
            cp.wait_send()
        for cp in mine:
            cp.wait()

    return pl.pallas_call(
        body, name=name,
        out_shape=[jax.ShapeDtypeStruct(_full_shape(s.shape, k), s.dtype) for s, k in zip(shards, kinds)],
        in_specs=[ANY] * nw, out_specs=[ANY] * nw,
        scratch_shapes=[pltpu.SemaphoreType.DMA((7 * nw,)), pltpu.SemaphoreType.DMA((7 * nw,)),
                        pltpu.SemaphoreType.DMA((nw,))],
    )(*shards)


def _src_win(ref, l, kind, k, a, b):
    return _win2(ref if l is None else ref.at[l], kind, k, a, b)


def rs_start(srcs, items, name):
    ns, nz = len(srcs), len(items)
    zones = [lax.empty((N_PEER, a, b), srcs[w].dtype) for w, l, kind, a, b in items]

    def body(*refs):
        s_refs = refs[:ns]
        send_sems, recv_sems = refs[ns + nz], refs[ns + nz + 1]
        z_refs = refs[ns + nz + 2 + ns:]
        x, y, c = _pos()
        for t, (w, l, kind, a, b) in enumerate(items):
            for r in range(1, N_DEV):
                peer = _peer(x, y, c, r)
                pltpu.make_async_remote_copy(
                    src_ref=_src_win(s_refs[w], l, kind, 4 * peer[0] + 2 * peer[1] + peer[2], a, b),
                    dst_ref=z_refs[t].at[r - 1],
                    send_sem=send_sems.at[N_PEER * t + r - 1], recv_sem=recv_sems.at[N_PEER * t + r - 1],
                    device_id=peer, device_id_type=MESH).start()

    n_sem = N_PEER * nz
    outs = pl.pallas_call(
        body, name=name,
        out_shape=(pltpu.SemaphoreType.DMA((n_sem,)), pltpu.SemaphoreType.DMA((n_sem,)),
                   *[pltpu.HBM(s.shape, s.dtype) for s in srcs], *[pltpu.HBM(z.shape, z.dtype) for z in zones]),
        in_specs=[HBM] * (ns + nz), out_specs=(SEMS, SEMS, *[HBM] * (ns + nz)),
        input_output_aliases={i: 2 + i for i in range(ns + nz)},
        compiler_params=pltpu.CompilerParams(has_side_effects=DATAFLOW),
    )(*[_in_hbm(s) for s in srcs], *[_in_hbm(z) for z in zones])
    return outs[:2], list(outs[2:2 + ns]), list(outs[2 + ns:])


def rs_wait(zones, items, sems, after, keep, name):
    nz, nk = len(zones), len(keep)

    def body(*refs):
        z_refs = refs[:nz]
        send_sems, recv_sems = refs[nz], refs[nz + 1]
        x, y, c = _pos()
        for t, z_ref in enumerate(z_refs):
            for r in range(1, N_DEV):
                cp = pltpu.make_async_remote_copy(
                    src_ref=z_ref.at[r - 1], dst_ref=z_ref.at[r - 1],
                    send_sem=send_sems.at[N_PEER * t + r - 1], recv_sem=recv_sems.at[N_PEER * t + r - 1],
                    device_id=_peer(x, y, c, r), device_id_type=MESH)
                cp.wait_send()
                cp.wait_recv()

    outs = pl.pallas_call(
        body, name=name, out_shape=tuple(pltpu.HBM(z.shape, z.dtype) for z in zones),
        in_specs=[HBM] * nz + [SEMS] * 2 + [pl.BlockSpec(memory_space=pl.ANY)] * (1 + nk),
        out_specs=tuple([HBM] * nz), input_output_aliases={i: i for i in range(nz)},
        compiler_params=pltpu.CompilerParams(has_side_effects=DATAFLOW),
    )(*zones, *sems, after, *keep)
    return list(outs)


def adam_rs(w, m, v, l, own, kind, zone, outs, name):
    n, a, b = w.shape
    ta = max(t for t in range(16, min(a, 256) + 1, 16) if a % t == 0)
    per = a // ta
    me = (4 * lax.axis_index("x") + 2 * lax.axis_index("y") + lax.axis_index("c")).astype(jnp.int32).reshape(1)

    def body(me_ref, w_ref, m_ref, v_ref, own_ref, z_ref, i0, i1, i2, i3, g_ref, d_ref, m2_ref, v2_ref):
        gv = own_ref[...].astype(F32)
        for k in range(N_PEER):
            gv = gv + z_ref[k].astype(F32)
        m2 = ADAM_B1 * m_ref[...] + (1.0 - ADAM_B1) * gv
        v2 = ADAM_B2 * v_ref[...] + (1.0 - ADAM_B2) * (gv * gv)
        m_hat = m2 / (1.0 - ADAM_B1 ** ADAM_STEP)
        v_hat = v2 / (1.0 - ADAM_B2 ** ADAM_STEP)
        g_ref[...] = gv
        d_ref[...] = -ADAM_LR * (m_hat / (jnp.sqrt(v_hat) + ADAM_EPS) + ADAM_WD * w_ref[...])
        m2_ref[...] = m2
        v2_ref[...] = v2

    spec = pl.BlockSpec((None, ta, b), lambda r, me_ref: (l, r, 0))
    if kind == "lead":
        own_spec = pl.BlockSpec((None, ta, b), lambda r, me_ref: (me_ref[0], r, 0))
    elif kind == "row":
        own_spec = pl.BlockSpec((None, ta, b), lambda r, me_ref: (l, me_ref[0] * per + r, 0))
    else:
        own_spec = pl.BlockSpec((None, ta, b), lambda r, me_ref: (l, r, me_ref[0]))
    return pl.pallas_call(
        body, name=name, out_shape=[jax.ShapeDtypeStruct((n, a, b), F32)] * 4,
        grid_spec=pltpu.PrefetchScalarGridSpec(
            num_scalar_prefetch=1, grid=(per,),
            in_specs=[spec] * 3 + [own_spec, pl.BlockSpec((N_PEER, ta, b), lambda r, me_ref: (0, r, 0))] + [ANY] * 4,
            out_specs=[spec] * 4),
        input_output_aliases={6 + k: k for k in range(4)},
        compiler_params=_cp("parallel"),
    )(me, w, m, v, own, zone, *outs)


def small_exchange(sh, rep, name):
    _, Rs, C = sh.shape
    Rr = rep.shape[0]

    def body(sh_ref, rep_ref, sh_out, rep_out, send_sems, recv_sems, local_sems):
        x, y, c = _pos()
        me = 4 * x + 2 * y + c
        l1 = pltpu.make_async_copy(sh_ref.at[me], sh_out.at[me], local_sems.at[0])
        l2 = pltpu.make_async_copy(rep_ref, rep_out.at[me], local_sems.at[1])
        l1.start()
        l2.start()

        def flip(v, bit):
            return 1 - v if bit else v

        sends, recvs = [], []
        for r in range(1, N_DEV):
            peer = (flip(x, r & 4), flip(y, r & 2), flip(c, r & 1))
            pid = 4 * peer[0] + 2 * peer[1] + peer[2]
            k = 2 * (r - 1)
            mk = lambda src, dst, kk: pltpu.make_async_remote_copy(
                src_ref=src, dst_ref=dst, send_sem=send_sems.at[kk], recv_sem=recv_sems.at[kk],
                device_id=peer, device_id_type=MESH)
            sends += [mk(sh_ref.at[pid], sh_out.at[me], k), mk(rep_ref, rep_out.at[me], k + 1)]
            recvs += [mk(sh_ref.at[me], sh_out.at[pid], k), mk(rep_ref, rep_out.at[pid], k + 1)]
        for cp in sends:
            cp.start()
        for cp in recvs:
            cp.wait_recv()
        for cp in sends:
            cp.wait_send()
        l1.wait()
        l2.wait()

    n = 2 * (N_DEV - 1)
    return pl.pallas_call(
        body, name=name,
        out_shape=[jax.ShapeDtypeStruct((N_DEV, Rs, C), sh.dtype), jax.ShapeDtypeStruct((N_DEV, *rep.shape), rep.dtype)],
        in_specs=[ANY, ANY], out_specs=[ANY, ANY],
        scratch_shapes=[pltpu.SemaphoreType.DMA((n,)), pltpu.SemaphoreType.DMA((n,)), pltpu.SemaphoreType.DMA((2,))],
    )(sh, rep)


def adam_slots(w, m, v, slots, name):
    S, n, a, b = slots.shape
    ta = max(t for t in range(16, min(a, 512) + 1, 8)
             if a % t == 0 and t * S * b * slots.dtype.itemsize <= 4 * 1024 * 1024)

    def body(w_ref, m_ref, v_ref, s_ref, g_ref, d_ref, m2_ref, v2_ref):
        gv = s_ref[0].astype(F32)
        for k in range(1, S):
            gv = gv + s_ref[k].astype(F32)
        m2 = ADAM_B1 * m_ref[...] + (1.0 - ADAM_B1) * gv
        v2 = ADAM_B2 * v_ref[...] + (1.0 - ADAM_B2) * (gv * gv)
        m_hat = m2 / (1.0 - ADAM_B1 ** ADAM_STEP)
        v_hat = v2 / (1.0 - ADAM_B2 ** ADAM_STEP)
        g_ref[...] = gv
        d_ref[...] = -ADAM_LR * (m_hat / (jnp.sqrt(v_hat) + ADAM_EPS) + ADAM_WD * w_ref[...])
        m2_ref[...] = m2
        v2_ref[...] = v2

    spec = pl.BlockSpec((None, ta, b), lambda l, r: (l, r, 0))
    return pl.pallas_call(
        body, name=name, grid=(n, a // ta),
        in_specs=[spec] * 3 + [pl.BlockSpec((S, None, ta, b), lambda l, r: (0, l, r, 0))], out_specs=[spec] * 4,
        out_shape=[jax.ShapeDtypeStruct((n, a, b), F32)] * 4, compiler_params=_cp("parallel", "parallel"),
    )(w, m, v, slots)


WEIGHTS = ["norm_g", "ssm_in_w", "ssm_conv_w", "ssm_conv_b", "ssm_dt_bias", "ssm_A_log", "ssm_D", "ssm_norm_g",
           "ssm_out_w", "cf_pw1_w", "cf_pw1_b", "cf_dw_w", "cf_dw_b", "cf_ln_g", "cf_ln_b", "cf_pw2_w", "cf_pw2_b",
           "xa_mem_g", "xa_q_w", "xa_kv_w", "xa_o_w", "ffn_in_w", "ffn_conv_w", "ffn_conv_b", "ffn_out_w"]
ARGS = ["x", "mem"] + WEIGHTS + ["loss_target"] + ["m_" + n for n in WEIGHTS] + ["v_" + n for n in WEIGHTS]
BIG = {"ssm_in_w": "col", "ssm_out_w": "row", "cf_pw1_w": "col", "cf_pw2_w": "row", "xa_q_w": "row",
       "xa_kv_w": "col", "xa_o_w": "row", "ffn_in_w": "col", "ffn_out_w": "row"}
SMALL = ["norm_g", "ssm_conv_w", "cf_pw1_b", "cf_dw_w", "cf_dw_b", "cf_ln_g", "cf_ln_b", "cf_pw2_b", "ffn_conv_w"]
REP = ["ssm_conv_b", "ssm_dt_bias", "ssm_A_log", "ssm_D", "ssm_norm_g", "xa_mem_g", "ffn_conv_b"]
SMALL_W = 768
REP_W = 512


def _r8(n):
    return -(-n // 8) * 8


def _stack2d(arrs, wid):
    parts = []
    for a in arrs:
        r, c = a.shape[-2:]
        parts.append(jnp.pad(a, [(0, 0)] * (a.ndim - 2) + [(0, _r8(r) - r), (0, wid - c)]))
    return jnp.concatenate(parts, axis=-2)


def _unstack2d(buf, shapes2d):
    out, o = [], 0
    for r, c in shapes2d:
        out.append(buf[..., o:o + r, :c])
        o += _r8(r)
    return out


def _gathered_to_full(g):
    lead = g.shape[1:-1]
    return jnp.moveaxis(g, 0, -2).reshape(*lead, N_DEV * g.shape[-1])


def _full_to_slots(w):
    lead = w.shape[:-1]
    return jnp.moveaxis(w.reshape(*lead, N_DEV, w.shape[-1] // N_DEV), -2, 0)


def kernel(x, mem, norm_g, ssm_in_w, ssm_conv_w, ssm_conv_b, ssm_dt_bias, ssm_A_log, ssm_D, ssm_norm_g, ssm_out_w, cf_pw1_w, cf_pw1_b, cf_dw_w, cf_dw_b, cf_ln_g, cf_ln_b, cf_pw2_w, cf_pw2_b, xa_mem_g, xa_q_w, xa_kv_w, xa_o_w, ffn_in_w, ffn_conv_w, ffn_conv_b, ffn_out_w, loss_target, m_norm_g, m_ssm_in_w, m_ssm_conv_w, m_ssm_conv_b, m_ssm_dt_bias, m_ssm_A_log, m_ssm_D, m_ssm_norm_g, m_ssm_out_w, m_cf_pw1_w, m_cf_pw1_b, m_cf_dw_w, m_cf_dw_b, m_cf_ln_g, m_cf_ln_b, m_cf_pw2_w, m_cf_pw2_b, m_xa_mem_g, m_xa_q_w, m_xa_kv_w, m_xa_o_w, m_ffn_in_w, m_ffn_conv_w, m_ffn_conv_b, m_ffn_out_w, v_norm_g, v_ssm_in_w, v_ssm_conv_w, v_ssm_conv_b, v_ssm_dt_bias, v_ssm_A_log, v_ssm_D, v_ssm_norm_g, v_ssm_out_w, v_cf_pw1_w, v_cf_pw1_b, v_cf_dw_w, v_cf_dw_b, v_cf_ln_g, v_cf_ln_b, v_cf_pw2_w, v_cf_pw2_b, v_xa_mem_g, v_xa_q_w, v_xa_kv_w, v_xa_o_w, v_ffn_in_w, v_ffn_conv_w, v_ffn_conv_b, v_ffn_out_w):
    return _step(x, mem, norm_g, ssm_in_w, ssm_conv_w, ssm_conv_b, ssm_dt_bias, ssm_A_log, ssm_D, ssm_norm_g, ssm_out_w, cf_pw1_w, cf_pw1_b, cf_dw_w, cf_dw_b, cf_ln_g, cf_ln_b, cf_pw2_w, cf_pw2_b, xa_mem_g, xa_q_w, xa_kv_w, xa_o_w, ffn_in_w, ffn_conv_w, ffn_conv_b, ffn_out_w, loss_target, m_norm_g, m_ssm_in_w, m_ssm_conv_w, m_ssm_conv_b, m_ssm_dt_bias, m_ssm_A_log, m_ssm_D, m_ssm_norm_g, m_ssm_out_w, m_cf_pw1_w, m_cf_pw1_b, m_cf_dw_w, m_cf_dw_b, m_cf_ln_g, m_cf_ln_b, m_cf_pw2_w, m_cf_pw2_b, m_xa_mem_g, m_xa_q_w, m_xa_kv_w, m_xa_o_w, m_ffn_in_w, m_ffn_conv_w, m_ffn_conv_b, m_ffn_out_w, v_norm_g, v_ssm_in_w, v_ssm_conv_w, v_ssm_conv_b, v_ssm_dt_bias, v_ssm_A_log, v_ssm_D, v_ssm_norm_g, v_ssm_out_w, v_cf_pw1_w, v_cf_pw1_b, v_cf_dw_w, v_cf_dw_b, v_cf_ln_g, v_cf_ln_b, v_cf_pw2_w, v_cf_pw2_b, v_xa_mem_g, v_xa_q_w, v_xa_kv_w, v_xa_o_w, v_ffn_in_w, v_ffn_conv_w, v_ffn_conv_b, v_ffn_out_w)


def _step(*args):
    A = dict(zip(ARGS, args, strict=True))
    x, mem, target = A["x"][0], A["mem"][0], A["loss_target"][0]

    big = list(BIG)
    geo = [A[n].shape for n in big]
    kinds = ["row" if BIG[n] == "row" else ("col" if A[n].shape[-1] % LANE == 0 else "lead") for n in big]
    W = {n: A[n] for n in REP}
    small2d = [(A[n].size // A[n].shape[-1], A[n].shape[-1]) for n in SMALL]
    rep2d = [(A[n].size // REP_W, REP_W) if A[n].shape[-1] % REP_W == 0 else A[n].shape for n in REP] + [(1, 1)]
    stack_small = lambda pre: _stack2d([A[pre + n].reshape(rc) for n, rc in zip(SMALL, small2d)], SMALL_W)
    stack_rep = lambda pre: _stack2d([A[pre + n].reshape(rc) for n, rc in zip(REP, rep2d)] + [jnp.zeros((1, 1), F32)],
                                     REP_W)
    small_g = all_gather(stack_small(""), name="gather_small")
    for n, g in zip(SMALL, _unstack2d(small_g, small2d)):
        W[n] = _gathered_to_full(g.reshape(N_DEV, *A[n].shape))

    shards = [A[n].astype(BF16) for n in big]
    for n in big:
        W[n] = [None] * A[n].shape[0]
    first = _layer_weights(0)
    got0 = gather_now([shards[big.index(n)][l:l + 1] for n, l in first], [kinds[big.index(n)] for n, l in first],
                      name="gather_layer0")
    for (n, l), g in zip(first, got0):
        W[n][l] = _gathered_to_full(g)[0] if kinds[big.index(n)] == "lead" else g[0]
    items, layer_items = [], [[]]
    for i in range(1, DEPTH):
        layer_items.append([])
        for n, l in _layer_weights(i):
            w = big.index(n)
            layer_items[i].append(len(items))
            items.append((w, l, kinds[w], *geo[w][1:]))
    sems, shards_thru, zones, token = gather_start(shards, items, [small_g, got0[0]], name="gather_start")
    mem = mem + token[0, 0]

    def fetch(i, x_in):
        ids = layer_items[i]
        if not ids:
            return
        got = gather_wait([zones[t] for t in ids], ids, items, sems, x_in, shards_thru if i == DEPTH - 1 else [],
                          name=f"gather_wait{i}")
        for t, z in zip(ids, got):
            w, l, kind = items[t][:3]
            W[big[w]][l] = _gathered_to_full(z) if kind == "lead" else z

    sent = []
    final = {}

    def layer_done(i, GB, g_in):
        srcs, its = [], []
        for n, l in _layer_weights(i):
            w = big.index(n)
            if kinds[w] == "lead":
                srcs.append(_full_to_slots(g_in if n == "ssm_in_w" else GB[n][l]))
                its.append((len(srcs) - 1, None, "lead", *geo[w][1:], n, l))
            else:
                srcs.append(GB[n])
                its.append((len(srcs) - 1, l, kinds[w], *geo[w][1:], n, l))
        sems_i, thru, zones_i = rs_start(srcs, [it[:5] for it in its], name=f"rs_start{i}")
        for it, s in zip(its, thru):
            if it[2] != "lead":
                GB[it[5]] = s
        sent.append((its, sems_i, [s for it, s in zip(its, thru) if it[2] == "lead"], zones_i))
        final["GB"] = GB

    loss, grad_x, G = local_step(x, mem, target, W, fetch, layer_done)

    sh = _stack2d([_full_to_slots(G[n]).reshape(N_DEV, *rc) for n, rc in zip(SMALL, small2d)], SMALL_W)
    rep = _stack2d([G[n].reshape(rc) for n, rc in zip(REP, rep2d)] + [loss[:, :1]], REP_W)
    sh_got, rep_got = small_exchange(sh, rep, name="small_exchange")

    res = {}
    GBf = final["GB"]
    bufs = {n: [lax.empty(A[n].shape, F32) for _ in range(4)] for n in big}
    for i, (its, sems_i, lead_srcs, zones_i) in enumerate(sent):
        keep = lead_srcs + [GBf[it[5]] for it in its if it[2] != "lead"]
        zones_i = rs_wait(zones_i, [it[:5] for it in its], sems_i, sh_got, keep, name=f"rs_wait{i}")
        lead_it = iter(lead_srcs)
        for it, z in zip(its, zones_i):
            n, l = it[5], it[6]
            own = next(lead_it) if it[2] == "lead" else GBf[n]
            bufs[n] = adam_rs(A[n], A["m_" + n], A["v_" + n], l, own, it[2], z, bufs[n], name=f"adam_{n}{l}")
    for n in big:
        res[n] = tuple(bufs[n])
    for names, shapes2d, stack, slots, tag in ((SMALL, small2d, stack_small, sh_got, "small"),
                                               (REP, rep2d, stack_rep, rep_got, "rep")):
        outs4 = adam_slots(stack("")[None], stack("m_")[None], stack("v_")[None], slots[:, None], name=f"adam_{tag}")
        parts = [_unstack2d(o[0], shapes2d) for o in outs4]
        for k, n in enumerate(names):
            res[n] = tuple(q[k].reshape(A[n].shape) for q in parts)
        if tag == "rep":
            total_loss = parts[0][-1][0, 0]

    outs = [total_loss, grad_x[None]]
    for k in range(4):
        outs += [res[n][k] for n in WEIGHTS]
    return tuple(outs)
```

```python
import jax
import jax.numpy as jnp
from jax import lax
from jax.experimental import pallas as pl
from jax.experimental.pallas import tpu as pltpu

F32 = jnp.float32
BF16 = jnp.bfloat16

D_MODEL = 1024
D_INNER = 2048
N_HEADS = 32
HEAD_DIM = 64
N_GROUPS = 4
D_STATE = 128
CHUNK = 128
CONV_DIM = 3072
SSM_K = 4
CF_K = 31
N_MEM = 256
XA_HEADS = 4
XA_HD = 256
D_FF = 2816
FFN_K = 3
EPS = 1e-6
DEPTH = 4
N_DEV = 8

ADAM_LR = 0.001
ADAM_B1 = 0.9
ADAM_B2 = 0.999
ADAM_EPS = 1e-08
ADAM_WD = 0.01
ADAM_STEP = 10

LANE = 128
VMEM_LIMIT = 56 * 1024 * 1024
NEG = -1e30
MESH = pl.DeviceIdType.MESH


def _cp(*sem):
    return pltpu.CompilerParams(dimension_semantics=sem if sem else None, vmem_limit_bytes=VMEM_LIMIT)


def _tile(n, cap):
    if n <= cap:
        return n
    best = 0
    for t in range(LANE, cap + 1, LANE):
        if n % t == 0:
            best = t
    assert best, (n, cap)
    return best


def _sig(x):
    return 1.0 / (1.0 + jnp.exp(-x))


def _split3(v):
    v0 = v.astype(BF16)
    r1 = v - v0.astype(F32)
    v1 = r1.astype(BF16)
    v2 = (r1 - v1.astype(F32)).astype(BF16)
    return v0, v1, v2


def _dot(a, b, ca=1, cb=0):
    return lax.dot_general(a, b, (((ca,), (cb,)), ((), ())), preferred_element_type=F32)


def _dot3(v, m, ca=1, cb=0):
    v0, v1, v2 = _split3(v)
    return _dot(v0, m, ca, cb) + _dot(v1, m, ca, cb) + _dot(v2, m, ca, cb)


def mm(a, b, *, ta=False, tb=False, bias=None, acc=None, out_dtype=F32, a_idx=None, layer=None, b_k0=0, b_n=None,
       into=None, name):
    if isinstance(b, (list, tuple)):
        b, layer = b[layer], None
    if ta:
        K, M = a.shape[-2:]
    else:
        M, K = a.shape[-2:]
    N = b_n if b_n is not None else (b.shape[-2] if tb else b.shape[-1])
    assert (b.ndim == 3) == (layer is not None) and (a.ndim == 3) == (a_idx is not None)
    tm = _tile(M, 1024)
    tn = _tile(N, 1536)
    tk = _tile(K, 2048)
    nk = K // tk
    assert b_k0 % tk == 0 and b_k0 + K <= (b.shape[-1] if tb else b.shape[-2])
    kb = b_k0 // tk
    has_bias, has_acc = bias is not None, acc is not None
    if into is not None:
        out_dtype = into[0].dtype
        assert into[0].shape[1] == M and into[2] % tn == 0 and into[2] + N <= into[0].shape[2] and not has_acc

    def body(*refs):
        a_ref, b_ref = refs[0], refs[1]
        pos = 2
        bias_ref = acc_ref = None
        if has_bias:
            bias_ref = refs[pos]
            pos += 1
        if has_acc:
            acc_ref = refs[pos]
            pos += 1
        if into is not None:
            pos += 1
        o_ref = refs[pos]
        s_ref = refs[pos + 1] if nk > 1 else None
        p = _dot(a_ref[...].astype(BF16), b_ref[...].astype(BF16), 0 if ta else 1, 1 if tb else 0)

        def extras(v):
            if has_bias:
                v = v + bias_ref[...]
            if has_acc:
                v = v + acc_ref[...]
            return v

        if nk == 1:
            o_ref[...] = extras(p).astype(out_dtype)
        else:
            k = pl.program_id(2)

            @pl.when(k == 0)
            def _():
                s_ref[...] = extras(p)

            @pl.when(k > 0)
            def _():
                s_ref[...] += p

            @pl.when(k == nk - 1)
            def _():
                o_ref[...] = s_ref[...].astype(out_dtype)

    lead_a = () if a_idx is None else (a_idx,)
    lead_b = () if layer is None else (layer,)
    sq = lambda lead: (None,) * len(lead)
    if ta:
        a_spec = pl.BlockSpec((*sq(lead_a), tk, tm), lambda i, j, k: (*lead_a, k, i))
    else:
        a_spec = pl.BlockSpec((*sq(lead_a), tm, tk), lambda i, j, k: (*lead_a, i, k))
    if tb:
        b_spec = pl.BlockSpec((*sq(lead_b), tn, tk), lambda i, j, k: (*lead_b, j, k + kb))
    else:
        b_spec = pl.BlockSpec((*sq(lead_b), tk, tn), lambda i, j, k: (*lead_b, k + kb, j))
    in_specs, args = [a_spec, b_spec], [a, b]
    if has_bias:
        in_specs.append(pl.BlockSpec((1, tn), lambda i, j, k: (0, j)))
        args.append(bias.reshape(1, N).astype(F32))
    if has_acc:
        in_specs.append(pl.BlockSpec((tm, tn), lambda i, j, k: (i, j)))
        args.append(acc)
    if into is None:
        out_spec = pl.BlockSpec((tm, tn), lambda i, j, k: (i, j))
        out_shape = jax.ShapeDtypeStruct((M, N), out_dtype)
        aliases = {}
    else:
        buf, l, col0 = into
        cb = col0 // tn
        in_specs.append(pl.BlockSpec(memory_space=pl.ANY))
        args.append(buf)
        out_spec = pl.BlockSpec((None, tm, tn), lambda i, j, k: (l, i, j + cb))
        out_shape = jax.ShapeDtypeStruct(buf.shape, buf.dtype)
        aliases = {len(args) - 1: 0}
    return pl.pallas_call(
        body, name=name, grid=(M // tm, N // tn, nk),
        in_specs=in_specs, out_specs=out_spec, out_shape=out_shape, input_output_aliases=aliases,
        scratch_shapes=[pltpu.VMEM((tm, tn), F32)] if nk > 1 else [],
        compiler_params=_cp("parallel", "parallel", "arbitrary"),
    )(*args)


def colsum(x, name):
    L, C = x.shape
    tr = _tile(L, 512)
    tc = _tile(C, 1024)

    def body(x_ref, o_ref):
        @pl.when(pl.program_id(1) == 0)
        def _():
            o_ref[...] = jnp.zeros_like(o_ref)

        o_ref[...] += jnp.sum(x_ref[...].astype(F32), axis=0, keepdims=True)

    return pl.pallas_call(
        body, name=name, grid=(C // tc, L // tr),
        in_specs=[pl.BlockSpec((tr, tc), lambda j, i: (i, j))],
        out_specs=pl.BlockSpec((1, tc), lambda j, i: (0, j)),
        out_shape=jax.ShapeDtypeStruct((1, C), F32),
        compiler_params=_cp("parallel", "arbitrary"),
    )(x)


TR = 256


def _row_spec(tr, w):
    return pl.BlockSpec((tr, w), lambda i: (i, 0))


def _vec_spec(w):
    return pl.BlockSpec((1, w), lambda i: (0, 0))


def _rms(v):
    return lax.rsqrt(jnp.mean(v * v, axis=-1, keepdims=True) + EPS)


def norm_fwd(x, g, name):
    L, D = x.shape
    tr = min(TR, L)

    def body(x_ref, g_ref, h_ref):
        xv = x_ref[...]
        h_ref[...] = (xv * _rms(xv) * g_ref[...]).astype(BF16)

    return pl.pallas_call(
        body, name=name, grid=(L // tr,),
        in_specs=[_row_spec(tr, D), _vec_spec(D)], out_specs=_row_spec(tr, D),
        out_shape=jax.ShapeDtypeStruct((L, D), BF16), compiler_params=_cp("parallel"),
    )(x, g.reshape(1, D))


def bnd_fwd(x, f, gpost, gpre, name):
    L, D = x.shape
    tr = min(TR, L)

    def body(x_ref, f_ref, gp_ref, gn_ref, xo_ref, h_ref):
        fv = f_ref[...]
        xn = x_ref[...] + fv * _rms(fv) * gp_ref[...]
        xo_ref[...] = xn
        h_ref[...] = (xn * _rms(xn) * gn_ref[...]).astype(BF16)

    return pl.pallas_call(
        body, name=name, grid=(L // tr,),
        in_specs=[_row_spec(tr, D), _row_spec(tr, D), _vec_spec(D), _vec_spec(D)],
        out_specs=[_row_spec(tr, D), _row_spec(tr, D)],
        out_shape=[jax.ShapeDtypeStruct((L, D), F32), jax.ShapeDtypeStruct((L, D), BF16)],
        compiler_params=_cp("parallel"),
    )(x, f, gpost.reshape(1, D), gpre.reshape(1, D))


def final_fwd(x, f, gpost, target, name):
    L, D = x.shape
    tr = min(TR, L)
    n = L // tr

    def body(x_ref, f_ref, gp_ref, t_ref, dy_ref, loss_ref, acc_ref):
        i = pl.program_id(0)

        @pl.when(i == 0)
        def _():
            acc_ref[...] = jnp.zeros_like(acc_ref)

        fv = f_ref[...]
        e = x_ref[...] + fv * _rms(fv) * gp_ref[...] - t_ref[...]
        dy_ref[...] = e * (1.0 / D)
        acc_ref[...] += jnp.sum(e * e, axis=0, keepdims=True)

        @pl.when(i == n - 1)
        def _():
            loss_ref[...] = jnp.full((1, LANE), 0.5 / D, F32) * jnp.sum(acc_ref[...])

    return pl.pallas_call(
        body, name=name, grid=(n,),
        in_specs=[_row_spec(tr, D), _row_spec(tr, D), _vec_spec(D), _row_spec(tr, D)],
        out_specs=[_row_spec(tr, D), _vec_spec(LANE)],
        out_shape=[jax.ShapeDtypeStruct((L, D), F32), jax.ShapeDtypeStruct((1, LANE), F32)],
        scratch_shapes=[pltpu.VMEM((1, D), F32)],
        compiler_params=_cp("arbitrary"),
    )(x, f, gpost.reshape(1, D), target)


def _rms_bwd(v, g, dy):
    r = _rms(v)
    vn = v * r
    dg = jnp.sum(dy * vn, axis=0, keepdims=True)
    dvn = dy * g
    dv = r * (dvn - vn * jnp.mean(dvn * vn, axis=-1, keepdims=True))
    return dv, dg


def bnd_bwd(dxp, *, pre=None, post=None, dep=None, name):
    L, D = dxp.shape
    tr = min(TR, L)
    has_pre, has_post = pre is not None, post is not None

    def body(*refs):
        pos = 0
        dxp_ref = refs[pos]; pos += 1
        if has_pre:
            x_ref, gpre_ref, dh_ref = refs[pos:pos + 3]; pos += 3
        if has_post:
            f_ref, gpost_ref = refs[pos:pos + 2]; pos += 2
        if dep is not None:
            pos += 1
        if has_pre:
            dx_ref, dgpre_ref = refs[pos:pos + 2]; pos += 2
        if has_post:
            df_ref, dgpost_ref = refs[pos:pos + 2]; pos += 2
        i = pl.program_id(0)
        dx = dxp_ref[...]
        if has_pre:
            d, dg = _rms_bwd(x_ref[...], gpre_ref[...], dh_ref[...])
            dx = dx + d
            dx_ref[...] = dx

            @pl.when(i == 0)
            def _():
                dgpre_ref[...] = jnp.zeros_like(dgpre_ref)

            dgpre_ref[...] += dg
        if has_post:
            d, dg = _rms_bwd(f_ref[...], gpost_ref[...], dx)
            df_ref[...] = d.astype(BF16)

            @pl.when(i == 0)
            def _():
                dgpost_ref[...] = jnp.zeros_like(dgpost_ref)

            dgpost_ref[...] += dg

    in_specs, args = [_row_spec(tr, D)], [dxp]
    out_specs, out_shape, names = [], [], []
    if has_pre:
        x, gpre, dh = pre
        in_specs += [_row_spec(tr, D), _vec_spec(D), _row_spec(tr, D)]
        args += [x, gpre.reshape(1, D), dh]
        out_specs += [_row_spec(tr, D), _vec_spec(D)]
        out_shape += [jax.ShapeDtypeStruct((L, D), F32), jax.ShapeDtypeStruct((1, D), F32)]
        names += ["dx", "dgpre"]
    if has_post:
        f, gpost = post
        in_specs += [_row_spec(tr, D), _vec_spec(D)]
        args += [f, gpost.reshape(1, D)]
        out_specs += [_row_spec(tr, D), _vec_spec(D)]
        out_shape += [jax.ShapeDtypeStruct((L, D), BF16), jax.ShapeDtypeStruct((1, D), F32)]
        names += ["df", "dgpost"]
    if dep is not None:
        in_specs.append(pl.BlockSpec(memory_space=pl.ANY))
        args.append(dep)
    outs = pl.pallas_call(
        body, name=name, grid=(L // tr,), in_specs=in_specs, out_specs=out_specs, out_shape=out_shape,
        compiler_params=_cp("arbitrary"),
    )(*args)
    return dict(zip(names, outs))


def norm_dg(x, dy, name):
    L, D = x.shape
    tr = min(TR, L)

    def body(x_ref, dy_ref, o_ref):
        @pl.when(pl.program_id(0) == 0)
        def _():
            o_ref[...] = jnp.zeros_like(o_ref)

        xv = x_ref[...]
        o_ref[...] += jnp.sum(dy_ref[...] * xv * _rms(xv), axis=0, keepdims=True)

    return pl.pallas_call(
        body, name=name, grid=(L // tr,),
        in_specs=[_row_spec(tr, D), _row_spec(tr, D)], out_specs=_vec_spec(D),
        out_shape=jax.ShapeDtypeStruct((1, D), F32), compiler_params=_cp("arbitrary"),
    )(x, dy)


HALO = 32


def _prev_halo_spec(tr, tc, col):
    per = tr // HALO
    return pl.BlockSpec((HALO, tc), lambda *g: (jnp.maximum(g[-1] * per - 1, 0), col(*g)))


def _fill_prev(scr, halo_val, blk_val, i, tr):
    scr[pl.ds(0, HALO), :] = jnp.where(i == 0, 0.0, halo_val)
    scr[pl.ds(HALO, tr), :] = blk_val


def _conv(scr, w_ref, K, tr):
    acc = None
    for k in range(K):
        term = scr[pl.ds(HALO - (K - 1) + k, tr), :] * w_ref[k:k + 1, :]
        acc = term if acc is None else acc + term
    return acc


def _conv_dw(scr, d, o_ref, K, tr):
    for k in range(K):
        o_ref[k:k + 1, :] += jnp.sum(d * scr[pl.ds(HALO - (K - 1) + k, tr), :], axis=0, keepdims=True)


def ssm_conv_fwd(zx, w, b, *, col0, ncols, wcol0, out_dtype, name):
    L = zx.shape[0]
    tr = min(TR, L)
    tc = 1024
    cb, wb = col0 // tc, wcol0 // tc

    def body(x_ref, h_ref, w_ref, b_ref, o_ref, p_ref, scr):
        i = pl.program_id(1)
        _fill_prev(scr, h_ref[...].astype(F32), x_ref[...].astype(F32), i, tr)
        pre = _conv(scr, w_ref, SSM_K, tr) + b_ref[...]
        p_ref[...] = pre.astype(BF16)
        o_ref[...] = (pre * _sig(pre)).astype(out_dtype)

    out = pl.BlockSpec((tr, tc), lambda j, i: (i, j))
    return pl.pallas_call(
        body, name=name, grid=(ncols // tc, L // tr),
        in_specs=[pl.BlockSpec((tr, tc), lambda j, i: (i, j + cb)),
                  _prev_halo_spec(tr, tc, lambda j, i: j + cb),
                  pl.BlockSpec((SSM_K, tc), lambda j, i: (0, j + wb)),
                  pl.BlockSpec((1, tc), lambda j, i: (0, j + wb))],
        out_specs=[out, out],
        out_shape=[jax.ShapeDtypeStruct((L, ncols), out_dtype), jax.ShapeDtypeStruct((L, ncols), BF16)],
        scratch_shapes=[pltpu.VMEM((HALO + tr, tc), F32)],
        compiler_params=_cp("parallel", "parallel"),
    )(zx, zx, w, b)


def ssm_conv_bwd(zx, pre, d, w, *, col0, dcol0, ncols, name):
    L = zx.shape[0]
    tr = min(TR, L)
    tc = 1024
    cb, db_ = col0 // tc, dcol0 // tc
    n = L // tr
    per = tr // HALO
    last = L // HALO - 1

    def body(x_ref, h_ref, p_ref, np_ref, d_ref, nd_ref, w_ref, dx_ref, dw_ref, db_ref, scr, sd):
        i = pl.program_id(1)
        _fill_prev(scr, h_ref[...].astype(F32), x_ref[...].astype(F32), i, tr)

        def dpre(p, dv):
            s = _sig(p)
            return dv * s * (1.0 + p * (1.0 - s))

        dp = dpre(p_ref[...].astype(F32), d_ref[...])
        sd[pl.ds(0, tr), :] = dp
        sd[pl.ds(tr, HALO), :] = jnp.where(i == n - 1, 0.0, dpre(np_ref[...].astype(F32), nd_ref[...]))
        acc = None
        for k in range(SSM_K):
            term = sd[pl.ds(SSM_K - 1 - k, tr), :] * w_ref[k:k + 1, :]
            acc = term if acc is None else acc + term
        dx_ref[...] = acc.astype(BF16)

        @pl.when(i == 0)
        def _():
            dw_ref[...] = jnp.zeros_like(dw_ref)
            db_ref[...] = jnp.zeros_like(db_ref)

        _conv_dw(scr, dp, dw_ref, SSM_K, tr)
        db_ref[...] += jnp.sum(dp, axis=0, keepdims=True)

    nxt = lambda i: jnp.minimum((i + 1) * per, last)
    return pl.pallas_call(
        body, name=name, grid=(ncols // tc, n),
        in_specs=[pl.BlockSpec((tr, tc), lambda j, i: (i, j + cb)),
                  _prev_halo_spec(tr, tc, lambda j, i: j + cb),
                  pl.BlockSpec((tr, tc), lambda j, i: (i, j)),
                  pl.BlockSpec((HALO, tc), lambda j, i: (nxt(i), j)),
                  pl.BlockSpec((tr, tc), lambda j, i: (i, j + db_)),
                  pl.BlockSpec((HALO, tc), lambda j, i: (nxt(i), j + db_)),
                  pl.BlockSpec((SSM_K, tc), lambda j, i: (0, j + db_))],
        out_specs=[pl.BlockSpec((tr, tc), lambda j, i: (i, j)),
                   pl.BlockSpec((SSM_K, tc), lambda j, i: (0, j)),
                   pl.BlockSpec((1, tc), lambda j, i: (0, j))],
        out_shape=[jax.ShapeDtypeStruct((L, ncols), BF16), jax.ShapeDtypeStruct((SSM_K, ncols), F32),
                   jax.ShapeDtypeStruct((1, ncols), F32)],
        scratch_shapes=[pltpu.VMEM((HALO + tr, tc), F32), pltpu.VMEM((tr + HALO, tc), F32)],
        compiler_params=_cp("parallel", "arbitrary"),
    )(zx, zx, pre, pre, d, d, w)


FFN_TC = 1408


def ffn_act_fwd(u, w, b, name):
    L = u.shape[0]
    tr = min(TR, L)
    tc = FFN_TC
    nb = D_FF // tc

    def body(g_ref, hg_ref, v_ref, hv_ref, wg_ref, wv_ref, bg_ref, bv_ref, o_ref, c_ref, sg, sv):
        i = pl.program_id(1)
        _fill_prev(sg, hg_ref[...].astype(F32), g_ref[...].astype(F32), i, tr)
        _fill_prev(sv, hv_ref[...].astype(F32), v_ref[...].astype(F32), i, tr)
        ug = _conv(sg, wg_ref, FFN_K, tr) + bg_ref[...]
        uv = _conv(sv, wv_ref, FFN_K, tr) + bv_ref[...]
        c_ref[0] = ug.astype(BF16)
        c_ref[1] = uv.astype(BF16)
        o_ref[...] = (ug * _sig(ug) * uv).astype(BF16)

    blk = lambda off: pl.BlockSpec((tr, tc), lambda j, i: (i, j + off))
    wsp = lambda off: pl.BlockSpec((FFN_K, tc), lambda j, i: (0, j + off))
    bsp = lambda off: pl.BlockSpec((1, tc), lambda j, i: (0, j + off))
    return pl.pallas_call(
        body, name=name, grid=(nb, L // tr),
        in_specs=[blk(0), _prev_halo_spec(tr, tc, lambda j, i: j),
                  blk(nb), _prev_halo_spec(tr, tc, lambda j, i: j + nb),
                  wsp(0), wsp(nb), bsp(0), bsp(nb)],
        out_specs=[pl.BlockSpec((tr, tc), lambda j, i: (i, j)), pl.BlockSpec((2, tr, tc), lambda j, i: (0, i, j))],
        out_shape=[jax.ShapeDtypeStruct((L, D_FF), BF16), jax.ShapeDtypeStruct((2, L, D_FF), BF16)],
        scratch_shapes=[pltpu.VMEM((HALO + tr, tc), F32), pltpu.VMEM((HALO + tr, tc), F32)],
        compiler_params=_cp("parallel", "parallel"),
    )(u, u, u, u, w, w, b, b)


def ffn_act_bwd(u, c, dact, w, name):
    L = u.shape[0]
    tr = min(TR, L)
    tc = FFN_TC
    nb = D_FF // tc
    n = L // tr
    per = tr // HALO
    last = L // HALO - 1

    def body(g_ref, hg_ref, v_ref, hv_ref, c_ref, nc_ref, da_ref, nda_ref, wg_ref, wv_ref,
             du_ref, dw_ref, db_ref, sg, sv, dg_s, dv_s):
        i = pl.program_id(1)
        _fill_prev(sg, hg_ref[...].astype(F32), g_ref[...].astype(F32), i, tr)
        _fill_prev(sv, hv_ref[...].astype(F32), v_ref[...].astype(F32), i, tr)

        def grads(cg, cv, da):
            s = _sig(cg)
            return da * cv * s * (1.0 + cg * (1.0 - s)), da * cg * s

        dg, dv = grads(c_ref[0].astype(F32), c_ref[1].astype(F32), da_ref[...].astype(F32))
        ndg, ndv = grads(nc_ref[0].astype(F32), nc_ref[1].astype(F32), nda_ref[...].astype(F32))
        at_end = i == n - 1
        for half, (scr, d, nd, x_scr, w_ref) in enumerate(((dg_s, dg, ndg, sg, wg_ref), (dv_s, dv, ndv, sv, wv_ref))):
            scr[pl.ds(0, tr), :] = d
            scr[pl.ds(tr, HALO), :] = jnp.where(at_end, 0.0, nd)
            acc = None
            for k in range(FFN_K):
                term = scr[pl.ds(FFN_K - 1 - k, tr), :] * w_ref[k:k + 1, :]
                acc = term if acc is None else acc + term
            du_ref[half] = acc.astype(BF16)

            @pl.when(i == 0)
            def _():
                dw_ref[half] = jnp.zeros((FFN_K, tc), F32)
                db_ref[half] = jnp.zeros((1, tc), F32)

            for k in range(FFN_K):
                dw_ref[half, k:k + 1, :] += jnp.sum(d * x_scr[pl.ds(HALO - (FFN_K - 1) + k, tr), :], axis=0, keepdims=True)
            db_ref[half] += jnp.sum(d, axis=0, keepdims=True)

    blk = lambda off: pl.BlockSpec((tr, tc), lambda j, i: (i, j + off))
    wsp = lambda off: pl.BlockSpec((FFN_K, tc), lambda j, i: (0, j + off))
    nxt = lambda i: jnp.minimum((i + 1) * per, last)
    return pl.pallas_call(
        body, name=name, grid=(nb, n),
        in_specs=[blk(0), _prev_halo_spec(tr, tc, lambda j, i: j),
                  blk(nb), _prev_halo_spec(tr, tc, lambda j, i: j + nb),
                  pl.BlockSpec((2, tr, tc), lambda j, i: (0, i, j)),
                  pl.BlockSpec((2, HALO, tc), lambda j, i: (0, nxt(i), j)),
                  pl.BlockSpec((tr, tc), lambda j, i: (i, j)),
                  pl.BlockSpec((HALO, tc), lambda j, i: (nxt(i), j)),
                  wsp(0), wsp(nb)],
        out_specs=[pl.BlockSpec((2, tr, tc), lambda j, i: (0, i, j)),
                   pl.BlockSpec((2, FFN_K, tc), lambda j, i: (0, 0, j)),
                   pl.BlockSpec((2, 1, tc), lambda j, i: (0, 0, j))],
        out_shape=[jax.ShapeDtypeStruct((2, L, D_FF), BF16), jax.ShapeDtypeStruct((2, FFN_K, D_FF), F32),
                   jax.ShapeDtypeStruct((2, 1, D_FF), F32)],
        scratch_shapes=[pltpu.VMEM((HALO + tr, tc), F32), pltpu.VMEM((HALO + tr, tc), F32),
                        pltpu.VMEM((tr + HALO, tc), F32), pltpu.VMEM((tr + HALO, tc), F32)],
        compiler_params=_cp("parallel", "arbitrary"),
    )(u, u, u, u, c, c, dact, dact, w, w)


def _ln_stats(c):
    mu = jnp.mean(c, axis=-1, keepdims=True)
    cc = c - mu
    rstd = lax.rsqrt(jnp.mean(cc * cc, axis=-1, keepdims=True) + EPS)
    return cc * rstd, rstd


def cf_fwd(u, dw_w, dw_b, ln_g, ln_b, name):
    L = u.shape[0]
    D = D_MODEL
    tr = min(TR, L)

    def body(a_ref, ha_ref, g_ref, hg_ref, w_ref, b_ref, lg_ref, lb_ref, c_ref, s_ref, scr):
        i = pl.program_id(0)
        glu_h = ha_ref[...].astype(F32) * _sig(hg_ref[...].astype(F32))
        glu = a_ref[...].astype(F32) * _sig(g_ref[...].astype(F32))
        _fill_prev(scr, glu_h, glu, i, tr)
        c = _conv(scr, w_ref, CF_K, tr) + b_ref[...]
        c_ref[...] = c
        xhat, _ = _ln_stats(c)
        ln = xhat * lg_ref[...] + lb_ref[...]
        s_ref[...] = (ln * _sig(ln)).astype(BF16)

    per = tr // HALO
    halo = lambda col: pl.BlockSpec((HALO, D), lambda i: (jnp.maximum(i * per - 1, 0), col))
    return pl.pallas_call(
        body, name=name, grid=(L // tr,),
        in_specs=[pl.BlockSpec((tr, D), lambda i: (i, 0)), halo(0),
                  pl.BlockSpec((tr, D), lambda i: (i, 1)), halo(1),
                  pl.BlockSpec((CF_K, D), lambda i: (0, 0)), _vec_spec(D), _vec_spec(D), _vec_spec(D)],
        out_specs=[_row_spec(tr, D), _row_spec(tr, D)],
        out_shape=[jax.ShapeDtypeStruct((L, D), F32), jax.ShapeDtypeStruct((L, D), BF16)],
        scratch_shapes=[pltpu.VMEM((HALO + tr, D), F32)],
        compiler_params=_cp("parallel"),
    )(u, u, u, u, dw_w, dw_b, ln_g, ln_b)


def cf_bwd_ln(c, ds, ln_g, ln_b, name):
    L, D = c.shape
    tr = min(TR, L)

    def body(c_ref, ds_ref, lg_ref, lb_ref, dc_ref, dg_ref, db_ref):
        xhat, rstd = _ln_stats(c_ref[...])
        ln = xhat * lg_ref[...] + lb_ref[...]
        sg = _sig(ln)
        dln = ds_ref[...].astype(F32) * sg * (1.0 + ln * (1.0 - sg))

        @pl.when(pl.program_id(0) == 0)
        def _():
            dg_ref[...] = jnp.zeros_like(dg_ref)
            db_ref[...] = jnp.zeros_like(db_ref)

        dg_ref[...] += jnp.sum(dln * xhat, axis=0, keepdims=True)
        db_ref[...] += jnp.sum(dln, axis=0, keepdims=True)
        dxh = dln * lg_ref[...]
        dc_ref[...] = rstd * (dxh - jnp.mean(dxh, axis=-1, keepdims=True)
                              - xhat * jnp.mean(dxh * xhat, axis=-1, keepdims=True))

    return pl.pallas_call(
        body, name=name, grid=(L // tr,),
        in_specs=[_row_spec(tr, D), _row_spec(tr, D), _vec_spec(D), _vec_spec(D)],
        out_specs=[_row_spec(tr, D), _vec_spec(D), _vec_spec(D)],
        out_shape=[jax.ShapeDtypeStruct((L, D), F32), jax.ShapeDtypeStruct((1, D), F32),
                   jax.ShapeDtypeStruct((1, D), F32)],
        compiler_params=_cp("arbitrary"),
    )(c, ds, ln_g, ln_b)


def cf_bwd_conv(u, dc, dw_w, name):
    L = u.shape[0]
    D = D_MODEL
    tr = min(TR, L)
    n = L // tr

    def body(a_ref, ha_ref, g_ref, hg_ref, dc_ref, nx_ref, w_ref, du_ref, dw_ref, db_ref, sx, sd):
        i = pl.program_id(0)
        a = a_ref[...].astype(F32)
        sg = _sig(g_ref[...].astype(F32))
        _fill_prev(sx, ha_ref[...].astype(F32) * _sig(hg_ref[...].astype(F32)), a * sg, i, tr)
        dcv = dc_ref[...]
        sd[pl.ds(0, tr), :] = dcv
        sd[pl.ds(tr, HALO), :] = jnp.where(i == n - 1, 0.0, nx_ref[...])
        dglu = None
        for k in range(CF_K):
            term = sd[pl.ds(CF_K - 1 - k, tr), :] * w_ref[k:k + 1, :]
            dglu = term if dglu is None else dglu + term
        du_ref[:, 0:D] = (dglu * sg).astype(BF16)
        du_ref[:, D:2 * D] = (dglu * a * sg * (1.0 - sg)).astype(BF16)

        @pl.when(i == 0)
        def _():
            dw_ref[...] = jnp.zeros_like(dw_ref)
            db_ref[...] = jnp.zeros_like(db_ref)

        _conv_dw(sx, dcv, dw_ref, CF_K, tr)
        db_ref[...] += jnp.sum(dcv, axis=0, keepdims=True)

    per = tr // HALO
    last = L // HALO - 1
    halo = lambda col: pl.BlockSpec((HALO, D), lambda i: (jnp.maximum(i * per - 1, 0), col))
    return pl.pallas_call(
        body, name=name, grid=(n,),
        in_specs=[pl.BlockSpec((tr, D), lambda i: (i, 0)), halo(0),
                  pl.BlockSpec((tr, D), lambda i: (i, 1)), halo(1),
                  _row_spec(tr, D),
                  pl.BlockSpec((HALO, D), lambda i: (jnp.minimum((i + 1) * per, last), 0)),
                  pl.BlockSpec((CF_K, D), lambda i: (0, 0))],
        out_specs=[pl.BlockSpec((tr, 2 * D), lambda i: (i, 0)),
                   pl.BlockSpec((CF_K, D), lambda i: (0, 0)), _vec_spec(D)],
        out_shape=[jax.ShapeDtypeStruct((L, 2 * D), BF16), jax.ShapeDtypeStruct((CF_K, D), F32),
                   jax.ShapeDtypeStruct((1, D), F32)],
        scratch_shapes=[pltpu.VMEM((HALO + tr, D), F32), pltpu.VMEM((tr + HALO, D), F32)],
        compiler_params=_cp("arbitrary"),
    )(u, u, u, u, dc, dc, dw_w)


XA_TR = 512
XA_SCALE = XA_HD ** -0.5


def _xa_probs(qh, kh):
    s = _dot(qh, kh, 1, 1) * XA_SCALE
    p = jnp.exp(s - jnp.max(s, axis=-1, keepdims=True))
    return p / jnp.sum(p, axis=-1, keepdims=True)


def attn_fwd(q, kv, name):
    L, D = q.shape
    tr = min(XA_TR, L)

    def body(q_ref, kv_ref, o_ref):
        for hd in range(XA_HEADS):
            c = slice(hd * XA_HD, (hd + 1) * XA_HD)
            p = _xa_probs(q_ref[:, c], kv_ref[:, c])
            vh = kv_ref[:, D + hd * XA_HD:D + (hd + 1) * XA_HD]
            o_ref[:, c] = _dot(p.astype(BF16), vh).astype(BF16)

    return pl.pallas_call(
        body, name=name, grid=(L // tr,),
        in_specs=[_row_spec(tr, D), pl.BlockSpec((N_MEM, 2 * D), lambda i: (0, 0))],
        out_specs=_row_spec(tr, D), out_shape=jax.ShapeDtypeStruct((L, D), BF16),
        compiler_params=_cp("parallel"),
    )(q, kv)


def attn_bwd(q, kv, do, name):
    L, D = q.shape
    tr = min(XA_TR, L)

    def body(q_ref, kv_ref, do_ref, dq_ref, dkv_ref):
        @pl.when(pl.program_id(0) == 0)
        def _():
            dkv_ref[...] = jnp.zeros_like(dkv_ref)

        for hd in range(XA_HEADS):
            c = slice(hd * XA_HD, (hd + 1) * XA_HD)
            cv = slice(D + hd * XA_HD, D + (hd + 1) * XA_HD)
            qh, kh, vh, doh = q_ref[:, c], kv_ref[:, c], kv_ref[:, cv], do_ref[:, c]
            p = _xa_probs(qh, kh)
            dp = _dot(doh, vh, 1, 1)
            dkv_ref[:, cv] += _dot(p.astype(BF16), doh, 0, 0)
            ds = (p * (dp - jnp.sum(dp * p, axis=-1, keepdims=True)) * XA_SCALE).astype(BF16)
            dq_ref[:, c] = _dot(ds, kh).astype(BF16)
            dkv_ref[:, c] += _dot(ds, qh, 0, 0)

    return pl.pallas_call(
        body, name=name, grid=(L // tr,),
        in_specs=[_row_spec(tr, D), pl.BlockSpec((N_MEM, 2 * D), lambda i: (0, 0)), _row_spec(tr, D)],
        out_specs=[_row_spec(tr, D), pl.BlockSpec((N_MEM, 2 * D), lambda i: (0, 0))],
        out_shape=[jax.ShapeDtypeStruct((L, D), BF16), jax.ShapeDtypeStruct((N_MEM, 2 * D), F32)],
        compiler_params=_cp("arbitrary"),
    )(q, kv, do)


N_PAIRS = N_HEADS // 2
PAIRS_PER_GROUP = N_PAIRS // N_GROUPS
GN = N_GROUPS * D_STATE


def _softplus(x):
    t = jnp.exp(-jnp.abs(x))
    return jnp.maximum(x, 0.0) + jnp.where(t < 1e-4, t * (1.0 - 0.5 * t), jnp.log(1.0 + t))


def _dot3b(m, v, ca=1, cb=0):
    v0, v1, v2 = _split3(v)
    return _dot(m, v0, ca, cb) + _dot(m, v1, ca, cb) + _dot(m, v2, ca, cb)


def ssd_consts():
    h = lax.broadcasted_iota(jnp.int32, (LANE, D_INNER), 0)
    c = lax.broadcasted_iota(jnp.int32, (LANE, D_INNER), 1)
    expand = (c // HEAD_DIM == h).astype(BF16)
    r = lax.broadcasted_iota(jnp.int32, (CHUNK, CHUNK), 0)
    k = lax.broadcasted_iota(jnp.int32, (CHUNK, CHUNK), 1)
    tri = (k <= r).astype(BF16)
    return expand, tri


def _ssd_common(dtr_ref, prm_ref, e_ref, tri_ref):
    lane = lax.broadcasted_iota(jnp.int32, (CHUNK, LANE), 1)
    valid = lane < N_HEADS
    A = -jnp.exp(prm_ref[1:2, :])
    pre = dtr_ref[...] + prm_ref[0:1, :]
    dt = jnp.where(valid, _softplus(pre), 0.0)
    cs = _dot3b(tri_ref[...], dt * A)
    E = e_ref[...]
    dt_x = _dot3(dt, E)
    cs_x = _dot3(cs, E)
    csl_x = cs_x[CHUNK - 1:CHUNK, :]
    return dict(valid=valid, A=A, pre=pre, dt=dt, cs=cs, csT=cs.T, dt_x=dt_x, ecs_x=jnp.exp(cs_x),
                dend_x=jnp.exp(csl_x - cs_x), cd_x=jnp.exp(csl_x), D_x=_dot3(prm_ref[...], E)[2:3, :])


def ssd_fwd(xs, bc, dtr, zx, prm, ng, name):
    L = xs.shape[0]
    nc = L // CHUNK
    expand, tri = ssd_consts()

    def body(xs_ref, bc_ref, dtr_ref, z_ref, prm_ref, ng_ref, e_ref, tri_ref, y_ref, yn_ref, st_ref, state):
        @pl.when(pl.program_id(0) == 0)
        def _():
            state[...] = jnp.zeros_like(state)

        q = _ssd_common(dtr_ref, prm_ref, e_ref, tri_ref)
        cs, csT = q["cs"], q["csT"]
        xs_v = xs_ref[...]
        X = xs_v * q["dt_x"]
        Xb = X.astype(BF16)
        Xd = (X * q["dend_x"]).astype(BF16)
        ii = lax.broadcasted_iota(jnp.int32, (CHUNK, CHUNK), 0)
        jj = lax.broadcasted_iota(jnp.int32, (CHUNK, CHUNK), 1)
        tril = jj <= ii
        first = jj < HEAD_DIM
        for g in range(N_GROUPS):
            Bg = bc_ref[:, g * D_STATE:(g + 1) * D_STATE]
            Cg = bc_ref[:, GN + g * D_STATE:GN + (g + 1) * D_STATE]
            S = _dot(Cg, Bg, 1, 1)
            for pr in range(PAIRS_PER_GROUP):
                pair = g * PAIRS_PER_GROUP + pr
                cols = slice(pair * LANE, (pair + 1) * LANE)
                Xp = Xb[:, cols]
                ys = []
                for h in (2 * pair, 2 * pair + 1):
                    seg = cs[:, h:h + 1] - csT[h:h + 1, :]
                    M = (S * jnp.exp(jnp.where(tril, seg, NEG))).astype(BF16)
                    ys.append(_dot(M, Xp))
                prevT = state[pair]
                st_ref[0, pair] = prevT
                yoff = _dot(Cg, prevT.astype(BF16)) * q["ecs_x"][:, cols]
                y_ref[:, cols] = jnp.where(first, ys[0], ys[1]) + yoff + xs_v[:, cols] * q["D_x"][:, cols]
                state[pair] = prevT * q["cd_x"][:, cols] + _dot(Bg, Xd[:, cols], 0, 0)
        z = z_ref[...].astype(F32)
        gt = y_ref[...] * z * _sig(z)
        yn_ref[...] = (gt * _rms(gt) * ng_ref[...]).astype(BF16)

    row = lambda w: pl.BlockSpec((CHUNK, w), lambda c: (c, 0))
    const = lambda a: pl.BlockSpec(a.shape, lambda c: (0,) * a.ndim)
    return pl.pallas_call(
        body, name=name, grid=(nc,),
        in_specs=[row(D_INNER), row(2 * GN), row(LANE), row(D_INNER), const(prm), const(ng), const(expand), const(tri)],
        out_specs=[row(D_INNER), row(D_INNER), pl.BlockSpec((1, N_PAIRS, D_STATE, LANE), lambda c: (c, 0, 0, 0))],
        out_shape=[jax.ShapeDtypeStruct((L, D_INNER), F32), jax.ShapeDtypeStruct((L, D_INNER), BF16),
                   jax.ShapeDtypeStruct((nc, N_PAIRS, D_STATE, LANE), F32)],
        scratch_shapes=[pltpu.VMEM((N_PAIRS, D_STATE, LANE), F32)],
        compiler_params=_cp("arbitrary"),
    )(xs, bc, dtr, zx, prm, ng, expand, tri)


def ssd_bwd(dyn, y, zx, xs, bc, dtr, st, prm, ng, name):
    L = xs.shape[0]
    nc = L // CHUNK
    expand, tri = ssd_consts()

    def body(dyn_ref, y_ref, z_ref, xs_ref, bc_ref, dtr_ref, st_ref, prm_ref, ng_ref, e_ref, tri_ref,
             dxbc_ref, dz_ref, ddtr_ref, dng_ref, dprm_ref, dstate, g_cs, g_q, dX, g_row):
        step = pl.program_id(0)

        @pl.when(step == 0)
        def _():
            dstate[...] = jnp.zeros_like(dstate)
            dng_ref[...] = jnp.zeros_like(dng_ref)
            dprm_ref[...] = jnp.zeros_like(dprm_ref)
            g_row[...] = jnp.zeros_like(g_row)

        q = _ssd_common(dtr_ref, prm_ref, e_ref, tri_ref)
        cs, csT, E = q["cs"], q["csT"], e_ref[...]
        xs_v = xs_ref[...]
        X = xs_v * q["dt_x"]
        Xb = X.astype(BF16)
        Xd_f = X * q["dend_x"]
        Xd = Xd_f.astype(BF16)

        yv = y_ref[...]
        z = z_ref[...].astype(F32)
        sz = _sig(z)
        silu = z * sz
        gt = yv * silu
        r = _rms(gt)
        gn = gt * r
        dyn_v = dyn_ref[...]
        dng_ref[...] += jnp.sum(dyn_v * gn, axis=0, keepdims=True)
        dgn = dyn_v * ng_ref[...]
        dgt = r * (dgn - gn * jnp.mean(dgn * gn, axis=-1, keepdims=True))
        dY = dgt * silu
        dz_ref[...] = (dgt * yv * sz * (1.0 + z * (1.0 - sz))).astype(BF16)
        dYb = dY.astype(BF16)
        g_row[1:2, :] += jnp.sum(dY * xs_v, axis=0, keepdims=True)

        ii = lax.broadcasted_iota(jnp.int32, (CHUNK, CHUNK), 0)
        jj = lax.broadcasted_iota(jnp.int32, (CHUNK, CHUNK), 1)
        tril = jj <= ii
        triu = jj >= ii
        first = jj < HEAD_DIM
        lane_row = lax.broadcasted_iota(jnp.int32, (1, LANE), 1)
        sub_col = lax.broadcasted_iota(jnp.int32, (CHUNK, 1), 0)
        dcs_col = jnp.zeros((CHUNK, LANE), F32)
        dcs_rowT = jnp.zeros((LANE, CHUNK), F32)
        for g in range(N_GROUPS):
            Bg = bc_ref[:, g * D_STATE:(g + 1) * D_STATE]
            Cg = bc_ref[:, GN + g * D_STATE:GN + (g + 1) * D_STATE]
            S = _dot(Cg, Bg, 1, 1)
            ST = _dot(Bg, Cg, 1, 1)
            dS = jnp.zeros((CHUNK, CHUNK), F32)
            dCg = jnp.zeros((CHUNK, D_STATE), F32)
            dBg = jnp.zeros((CHUNK, D_STATE), F32)
            for pr in range(PAIRS_PER_GROUP):
                pair = g * PAIRS_PER_GROUP + pr
                cols = slice(pair * LANE, (pair + 1) * LANE)
                Xp = Xb[:, cols]
                dYp_f = dY[:, cols]
                dYp = dYb[:, cols]
                prevT = st_ref[0, pair]
                prevTb = prevT.astype(BF16)
                dst = dstate[pair]
                dstb = dst.astype(BF16)
                ecs_p = q["ecs_x"][:, cols]
                g_cs[:, cols] = dYp_f * (_dot(Cg, prevTb) * ecs_p)
                dWb = (dYp_f * ecs_p).astype(BF16)
                dprev = dst * q["cd_x"][:, cols] + _dot(Cg, dWb, 0, 0)
                dCg = dCg + _dot(dWb, prevTb, 1, 1)
                g_row[0:1, cols] = jnp.sum(dst * prevT, axis=0, keepdims=True)
                dXp = None
                for hh, h in enumerate((2 * pair, 2 * pair + 1)):
                    mine = first if hh == 0 else jnp.logical_not(first)
                    seg = cs[:, h:h + 1] - csT[h:h + 1, :]
                    lam = jnp.exp(jnp.where(tril, seg, NEG))
                    dM = _dot(jnp.where(mine, dYp, jnp.zeros_like(dYp)), Xp, 1, 1)
                    dS = dS + dM * lam
                    Gm = dM * (S * lam)
                    dcs_col = dcs_col + jnp.sum(Gm, axis=1, keepdims=True) * (lane_row == h).astype(F32)
                    dcs_rowT = dcs_rowT + (sub_col == h).astype(F32) * jnp.sum(Gm, axis=0, keepdims=True)
                    MT = (ST * jnp.exp(jnp.where(triu, -seg, NEG))).astype(BF16)
                    t = _dot(MT, dYp)
                    dXp = t if dXp is None else jnp.where(first, dXp, t)
                dXd = _dot(Bg, dstb)
                dBg = dBg + _dot(Xd[:, cols], dstb, 1, 1)
                g_q[:, cols] = dXd * Xd_f[:, cols]
                dX[:, cols] = dXp + dXd * q["dend_x"][:, cols]
                dstate[pair] = dprev
            dSb = dS.astype(BF16)
            dxbc_ref[:, D_INNER + g * D_STATE:D_INNER + (g + 1) * D_STATE] = dBg + _dot(dSb, Cg, 0, 0)
            dxbc_ref[:, D_INNER + GN + g * D_STATE:D_INNER + GN + (g + 1) * D_STATE] = dCg + _dot(dSb, Bg)
        dXv = dX[...]
        dxbc_ref[:, 0:D_INNER] = q["D_x"] * dY + dXv * q["dt_x"]
        r_dt = _dot3(dXv * xs_v, E, 1, 1)
        r_cs = _dot3(g_cs[...], E, 1, 1)
        r_q = _dot3(g_q[...], E, 1, 1)
        r_row = _dot3(g_row[...], E, 1, 1)
        cd = jnp.exp(cs[CHUNK - 1:CHUNK, :])
        dcs_last = jnp.sum(r_q, axis=0, keepdims=True) + r_row[0:1, :] * cd
        dcs = r_cs - r_q + dcs_col - dcs_rowT.T + jnp.where(sub_col == CHUNK - 1, dcs_last, 0.0)
        da = _dot3b(tri_ref[...], dcs, 0, 0)
        dpre = jnp.where(q["valid"], (r_dt + da * q["A"]) * _sig(q["pre"]), 0.0)
        ddtr_ref[...] = dpre
        dprm_ref[0:1, :] += jnp.sum(dpre, axis=0, keepdims=True)
        dprm_ref[1:2, :] += jnp.sum(da * q["dt"], axis=0, keepdims=True) * q["A"]
        dprm_ref[2:3, :] = r_row[1:2, :]

    rev = lambda w: pl.BlockSpec((CHUNK, w), lambda c: (nc - 1 - c, 0))
    const = lambda a: pl.BlockSpec(a.shape, lambda c: (0,) * a.ndim)
    return pl.pallas_call(
        body, name=name, grid=(nc,),
        in_specs=[rev(D_INNER), rev(D_INNER), rev(D_INNER), rev(D_INNER), rev(2 * GN), rev(LANE),
                  pl.BlockSpec((1, N_PAIRS, D_STATE, LANE), lambda c: (nc - 1 - c, 0, 0, 0)),
                  const(prm), const(ng), const(expand), const(tri)],
        out_specs=[rev(CONV_DIM), rev(D_INNER), rev(LANE),
                   pl.BlockSpec((1, D_INNER), lambda c: (0, 0)), pl.BlockSpec((8, LANE), lambda c: (0, 0))],
        out_shape=[jax.ShapeDtypeStruct((L, CONV_DIM), F32), jax.ShapeDtypeStruct((L, D_INNER), BF16),
                   jax.ShapeDtypeStruct((L, LANE), F32), jax.ShapeDtypeStruct((1, D_INNER), F32),
                   jax.ShapeDtypeStruct((8, LANE), F32)],
        scratch_shapes=[pltpu.VMEM((N_PAIRS, D_STATE, LANE), F32), pltpu.VMEM((CHUNK, D_INNER), F32),
                        pltpu.VMEM((CHUNK, D_INNER), F32), pltpu.VMEM((CHUNK, D_INNER), F32),
                        pltpu.VMEM((8, D_INNER), F32)],
        compiler_params=_cp("arbitrary"),
    )(dyn, y, zx, xs, bc, dtr, st, prm, ng, expand, tri)


def _ssd_weights(W, j):
    w_in = W["ssm_in_w"][j]
    nzx = D_INNER + CONV_DIM
    wdt = jnp.pad(w_in[:, nzx:], ((0, 0), (0, LANE - N_HEADS)))
    prm = jnp.zeros((8, LANE), F32)
    prm = prm.at[0, :N_HEADS].set(W["ssm_dt_bias"][j]).at[1, :N_HEADS].set(W["ssm_A_log"][j])
    prm = prm.at[2, :N_HEADS].set(W["ssm_D"][j])
    return dict(wdt=wdt, cw=W["ssm_conv_w"][j], cb=W["ssm_conv_b"][j].reshape(1, CONV_DIM), prm=prm,
                ng=W["ssm_norm_g"][j].reshape(1, D_INNER))


def ssd_layer_fwd(h, W, j, tag):
    p = _ssd_weights(W, j)
    zx = mm(h, W["ssm_in_w"], layer=j, b_n=D_INNER + CONV_DIM, out_dtype=BF16, name=f"{tag}_zx")
    dtr = mm(h, p["wdt"], name=f"{tag}_dt")
    xs, pre_x = ssm_conv_fwd(zx, p["cw"], p["cb"], col0=D_INNER, ncols=D_INNER, wcol0=0, out_dtype=F32,
                             name=f"{tag}_convx")
    bc, pre_bc = ssm_conv_fwd(zx, p["cw"], p["cb"], col0=2 * D_INNER, ncols=2 * GN, wcol0=D_INNER, out_dtype=BF16,
                              name=f"{tag}_convbc")
    y, yn, st = ssd_fwd(xs, bc, dtr, zx, p["prm"], p["ng"], name=f"{tag}_scan")
    f = mm(yn, W["ssm_out_w"], layer=j, name=f"{tag}_out")
    return f, dict(h=h, zx=zx, dtr=dtr, xs=xs, bc=bc, pre_x=pre_x, pre_bc=pre_bc, y=y, yn=yn, st=st, p=p)


def ssd_layer_bwd(df, ctx, W, GB, j, tag):
    p = ctx["p"]
    h = ctx["h"]
    dyn = mm(df, W["ssm_out_w"], layer=j, tb=True, name=f"{tag}_b_dyn")
    GB["ssm_out_w"] = mm(ctx["yn"], df, ta=True, into=(GB["ssm_out_w"], j, 0), name=f"{tag}_b_gwo")
    dxbc, dz, ddtr, dng, dprm = ssd_bwd(dyn, ctx["y"], ctx["zx"], ctx["xs"], ctx["bc"], ctx["dtr"], ctx["st"],
                                        p["prm"], p["ng"], name=f"{tag}_b_scan")
    dx1, dcw1, dcb1 = ssm_conv_bwd(ctx["zx"], ctx["pre_x"], dxbc, p["cw"], col0=D_INNER, dcol0=0, ncols=D_INNER,
                                   name=f"{tag}_b_convx")
    dx2, dcw2, dcb2 = ssm_conv_bwd(ctx["zx"], ctx["pre_bc"], dxbc, p["cw"], col0=2 * D_INNER, dcol0=D_INNER,
                                   ncols=2 * GN, name=f"{tag}_b_convbc")
    dh = mm(dz, W["ssm_in_w"], layer=j, tb=True, b_k0=0, name=f"{tag}_b_dh1")
    dh = mm(dx1, W["ssm_in_w"], layer=j, tb=True, b_k0=D_INNER, acc=dh, name=f"{tag}_b_dh2")
    dh = mm(dx2, W["ssm_in_w"], layer=j, tb=True, b_k0=2 * D_INNER, acc=dh, name=f"{tag}_b_dh3")
    dh = mm(ddtr, p["wdt"], tb=True, acc=dh, name=f"{tag}_b_dh4")
    g_in = jnp.concatenate([mm(h, dz, ta=True, out_dtype=BF16, name=f"{tag}_b_gz"),
                            mm(h, dx1, ta=True, out_dtype=BF16, name=f"{tag}_b_gx"),
                            mm(h, dx2, ta=True, out_dtype=BF16, name=f"{tag}_b_gbc"),
                            mm(h, ddtr, ta=True, out_dtype=BF16, name=f"{tag}_b_gdt")[:, :N_HEADS]], axis=1)
    return dh, dict(ssm_in_w=g_in, ssm_conv_w=jnp.concatenate([dcw1, dcw2], axis=1),
                    ssm_conv_b=jnp.concatenate([dcb1, dcb2], axis=1)[0], ssm_dt_bias=dprm[0, :N_HEADS],
                    ssm_A_log=dprm[1, :N_HEADS], ssm_D=dprm[2, :N_HEADS], ssm_norm_g=dng[0])


def cf_layer_fwd(h, W, j, tag):
    u = mm(h, W["cf_pw1_w"], layer=j, bias=W["cf_pw1_b"][j], out_dtype=BF16, name=f"{tag}_pw1")
    c, s = cf_fwd(u, W["cf_dw_w"][j], W["cf_dw_b"][j].reshape(1, -1), W["cf_ln_g"][j].reshape(1, -1),
                  W["cf_ln_b"][j].reshape(1, -1), name=f"{tag}_conv")
    f = mm(s, W["cf_pw2_w"], layer=j, bias=W["cf_pw2_b"][j], name=f"{tag}_pw2")
    return f, dict(h=h, u=u, c=c, s=s)


def cf_layer_bwd(df, ctx, W, GB, j, tag):
    h = ctx["h"]
    ds = mm(df, W["cf_pw2_w"], layer=j, tb=True, name=f"{tag}_b_ds")
    GB["cf_pw2_w"] = mm(ctx["s"], df, ta=True, into=(GB["cf_pw2_w"], j, 0), name=f"{tag}_b_gpw2")
    g_b2 = colsum(df, name=f"{tag}_b_gb2")
    dc, dlg, dlb = cf_bwd_ln(ctx["c"], ds, W["cf_ln_g"][j].reshape(1, -1), W["cf_ln_b"][j].reshape(1, -1),
                             name=f"{tag}_b_ln")
    du, ddw, ddb = cf_bwd_conv(ctx["u"], dc, W["cf_dw_w"][j], name=f"{tag}_b_conv")
    dh = mm(du, W["cf_pw1_w"], layer=j, tb=True, name=f"{tag}_b_dh")
    GB["cf_pw1_w"] = mm(h, du, ta=True, into=(GB["cf_pw1_w"], j, 0), name=f"{tag}_b_gpw1")
    g_b1 = colsum(du, name=f"{tag}_b_gb1")
    return dh, dict(cf_pw1_b=g_b1[0], cf_dw_w=ddw, cf_dw_b=ddb[0], cf_ln_g=dlg[0], cf_ln_b=dlb[0], cf_pw2_b=g_b2[0])


def xa_layer_fwd(h, mem, W, i, tag):
    m = norm_fwd(mem, W["xa_mem_g"][i], name=f"{tag}_memnorm")
    kv = mm(m, W["xa_kv_w"], layer=i, out_dtype=BF16, name=f"{tag}_kv")
    q = mm(h, W["xa_q_w"], layer=i, out_dtype=BF16, name=f"{tag}_q")
    o = attn_fwd(q, kv, name=f"{tag}_attn")
    f = mm(o, W["xa_o_w"], layer=i, name=f"{tag}_o")
    return f, dict(h=h, m=m, kv=kv, q=q, o=o)


def xa_layer_bwd(df, ctx, mem, W, GB, i, tag):
    h = ctx["h"]
    do = mm(df, W["xa_o_w"], layer=i, tb=True, out_dtype=BF16, name=f"{tag}_b_do")
    GB["xa_o_w"] = mm(ctx["o"], df, ta=True, into=(GB["xa_o_w"], i, 0), name=f"{tag}_b_go")
    dq, dkv = attn_bwd(ctx["q"], ctx["kv"], do, name=f"{tag}_b_attn")
    dh = mm(dq, W["xa_q_w"], layer=i, tb=True, name=f"{tag}_b_dh")
    GB["xa_q_w"] = mm(h, dq, ta=True, into=(GB["xa_q_w"], i, 0), name=f"{tag}_b_gq")
    GB["xa_kv_w"] = mm(ctx["m"], dkv, ta=True, into=(GB["xa_kv_w"], i, 0), name=f"{tag}_b_gkv")
    dm = mm(dkv, W["xa_kv_w"], layer=i, tb=True, name=f"{tag}_b_dm")
    g_mg = norm_dg(mem, dm, name=f"{tag}_b_gmem")
    return dh, dict(xa_mem_g=g_mg[0])


def ffn_layer_fwd(h, W, i, tag):
    cw, cb = W["ffn_conv_w"][i], W["ffn_conv_b"][i].reshape(1, -1)
    u = mm(h, W["ffn_in_w"], layer=i, out_dtype=BF16, name=f"{tag}_in")
    act, c = ffn_act_fwd(u, cw, cb, name=f"{tag}_act")
    f = mm(act, W["ffn_out_w"], layer=i, name=f"{tag}_out")
    return f, dict(h=h, u=u, c=c, act=act)


def ffn_layer_bwd(df, ctx, W, GB, i, tag):
    h = ctx["h"]
    dact = mm(df, W["ffn_out_w"], layer=i, tb=True, out_dtype=BF16, name=f"{tag}_b_dact")
    GB["ffn_out_w"] = mm(ctx["act"], df, ta=True, into=(GB["ffn_out_w"], i, 0), name=f"{tag}_b_gout")
    du, dcw, dcb = ffn_act_bwd(ctx["u"], ctx["c"], dact, W["ffn_conv_w"][i], name=f"{tag}_b_act")
    dh = None
    for half in range(2):
        dh = mm(du, W["ffn_in_w"], a_idx=half, layer=i, tb=True, b_k0=half * D_FF, acc=dh, name=f"{tag}_b_dh{half}")
        GB["ffn_in_w"] = mm(h, du, ta=True, layer=half, into=(GB["ffn_in_w"], i, half * D_FF), name=f"{tag}_b_gin{half}")
    cat = lambda a: jnp.concatenate([a[0], a[1]], axis=-1)
    return dh, dict(ffn_conv_w=cat(dcw), ffn_conv_b=cat(dcb)[0])


def _layer_weights(i):
    j = i // 2
    mixer = [("ssm_in_w", j), ("ssm_out_w", j)] if i % 2 == 0 else [("cf_pw1_w", j), ("cf_pw2_w", j)]
    return mixer + [(n, i) for n in ("xa_q_w", "xa_kv_w", "xa_o_w", "ffn_in_w", "ffn_out_w")]


def local_step(x, mem, target, W, fetch=None, layer_done=None):
    subs = [(i, s) for i in range(DEPTH) for s in range(3)]
    ng = W["norm_g"]

    def fwd(i, s, h):
        tag = f"l{i}s{s}"
        if s == 0:
            return ssd_layer_fwd(h, W, i // 2, tag) if i % 2 == 0 else cf_layer_fwd(h, W, i // 2, tag)
        if s == 1:
            return xa_layer_fwd(h, mem, W, i, tag)
        return ffn_layer_fwd(h, W, i, tag)

    GB = {}

    def bwd(i, s, df, ctx):
        tag = f"l{i}s{s}"
        if s == 0:
            return (ssd_layer_bwd if i % 2 == 0 else cf_layer_bwd)(df, ctx, W, GB, i // 2, tag)
        if s == 1:
            return xa_layer_bwd(df, ctx, mem, W, GB, i, tag)
        return ffn_layer_bwd(df, ctx, W, GB, i, tag)

    h = norm_fwd(x, ng[0, 0], name="norm0")
    saved = []
    dxp = loss = None
    for k, (i, s) in enumerate(subs):
        if s == 0 and fetch is not None:
            fetch(i, x)
        f, ctx = fwd(i, s, h)
        saved.append((x, f, ctx))
        if k + 1 < len(subs):
            ni, ns = subs[k + 1]
            x, h = bnd_fwd(x, f, ng[i, 2 * s + 1], ng[ni, 2 * ns], name=f"bnd{k}")
        else:
            dxp, loss = final_fwd(x, f, ng[i, 2 * s + 1], target, name="final")

    for n in BIG:
        if n != "ssm_in_w":
            GB[n] = jnp.zeros((len(W[n]), *W[n][0].shape), BF16)
    grads = {}

    def put(name, idx, val):
        grads.setdefault(name, {})[idx] = val

    i, s = subs[-1]
    top = bnd_bwd(dxp, post=(saved[-1][1], ng[i, 2 * s + 1]), name="bbnd_top")
    put("norm_g", (i, 2 * s + 1), top["dgpost"][0])
    df = top["df"]
    for k in range(len(subs) - 1, -1, -1):
        i, s = subs[k]
        xk, _, ctx = saved[k]
        dh, gw = bwd(i, s, df, ctx)
        for name, val in gw.items():
            put(name, i // 2 if name.startswith(("ssm_", "cf_")) else i, val)
        dep = None
        if s == 0 and layer_done is not None:
            dep = layer_done(i, GB, grads["ssm_in_w"].pop(i // 2) if i % 2 == 0 else None)
        if k > 0:
            pi, ps = subs[k - 1]
            r = bnd_bwd(dxp, pre=(xk, ng[i, 2 * s], dh), post=(saved[k - 1][1], ng[pi, 2 * ps + 1]), dep=dep,
                        name=f"bbnd{k}")
            put("norm_g", (pi, 2 * ps + 1), r["dgpost"][0])
            df = r["df"]
        else:
            r = bnd_bwd(dxp, pre=(xk, ng[i, 2 * s], dh), dep=dep, name="bbnd0")
        put("norm_g", (i, 2 * s), r["dgpre"][0])
        dxp = r["dx"]

    out = {} if layer_done is not None else dict(GB)
    for name, d in grads.items():
        if name == "norm_g":
            out[name] = jnp.stack([jnp.stack([d[(i, t)] for t in range(6)]) for i in range(DEPTH)])
        elif d:
            out[name] = jnp.stack([d[j] for j in sorted(d)])
    return loss, dxp, out


ANY = pl.BlockSpec(memory_space=pl.ANY)


def _pos():
    return lax.axis_index("x"), lax.axis_index("y"), lax.axis_index("c")


def all_gather(shard, name):
    R, C = shard.shape

    def body(x_ref, out_ref, send_sems, recv_sems, local_sem):
        x, y, c = _pos()
        me, sibling = (x, y, c), (x, y, 1 - c)
        chips = [(1 - x, y), (x, 1 - y), (1 - x, 1 - y)]

        def slot(px, py, pc):
            return out_ref.at[4 * px + 2 * py + pc]

        def copy(k, block, to, src=None):
            return pltpu.make_async_remote_copy(
                src_ref=slot(*block) if src is None else src, dst_ref=slot(*block),
                send_sem=send_sems.at[k], recv_sem=recv_sems.at[k], device_id=to, device_id_type=MESH)

        mine = pltpu.make_async_copy(x_ref, slot(*me), local_sem)
        mine.start()
        first = [copy(0, me, sibling, src=x_ref)]
        first += [copy(1 + j, me, (*chip, c), src=x_ref) for j, chip in enumerate(chips)]
        for cp in first:
            cp.start()
        passed = [copy(4 + j, (*chip, c), sibling) for j, chip in enumerate(chips)]
        for j, chip in enumerate(chips):
            copy(1 + j, (*chip, c), me).wait_recv()
            passed[j].start()
        copy(0, sibling, me).wait_recv()
        for j, chip in enumerate(chips):
            copy(4 + j, (*chip, 1 - c), me).wait_recv()
        for cp in first + passed:
            cp.wait_send()
        mine.wait()

    return pl.pallas_call(
        body, name=name, out_shape=jax.ShapeDtypeStruct((N_DEV, R, C), shard.dtype),
        in_specs=[ANY], out_specs=ANY,
        scratch_shapes=[pltpu.SemaphoreType.DMA((7,)), pltpu.SemaphoreType.DMA((7,)), pltpu.SemaphoreType.DMA(())],
    )(shard)


def _win(ref, kind, k, a, b):
    if kind == "lead":
        return ref.at[k]
    if kind == "row":
        return ref.at[:, pl.ds(pl.multiple_of(k * a, 16), a), :]
    return ref.at[:, :, pl.ds(pl.multiple_of(k * b, LANE), b)]


def _full_shape(shard_shape, kind):
    n, a, b = shard_shape
    return {"lead": (N_DEV, n, a, b), "row": (n, N_DEV * a, b), "col": (n, a, N_DEV * b)}[kind]


HBM = pl.BlockSpec(memory_space=pltpu.HBM)
SEMS = pl.BlockSpec(memory_space=pltpu.SEMAPHORE)
DATAFLOW = pltpu.SideEffectType.DATAFLOW_SIDE_EFFECTING
N_PEER = N_DEV - 1


def _in_hbm(a):
    return pltpu.with_memory_space_constraint(a, pltpu.HBM)


def _peer(x, y, c, r):
    return ((1 - x) if r & 4 else x, (1 - y) if r & 2 else y, (1 - c) if r & 1 else c)


def _win2(ref, kind, k, a, b):
    if kind == "lead":
        return ref.at[k]
    if kind == "row":
        return ref.at[pl.ds(pl.multiple_of(k * a, 16), a), :]
    return ref.at[:, pl.ds(pl.multiple_of(k * b, LANE), b)]


def _zone_shape(kind, a, b):
    return {"lead": (N_DEV, a, b), "row": (N_DEV * a, b), "col": (a, N_DEV * b)}[kind]


def gather_start(shards, items, after, name):
    ns, nz, na = len(shards), len(items), len(after)
    zones = [lax.empty(_zone_shape(kind, a, b), shards[w].dtype) for w, l, kind, a, b in items]

    def body(*refs):
        x_refs = refs[:ns]
        send_sems, recv_sems, local_sems = refs[ns + nz + na:ns + nz + na + 3]
        z_refs = refs[ns + nz + na + 3 + ns:ns + nz + na + 3 + ns + nz]
        token = refs[-1]
        x, y, c = _pos()
        me = 4 * x + 2 * y + c
        for t, (w, l, kind, a, b) in enumerate(items):
            mine = _win2(z_refs[t], kind, me, a, b)
            pltpu.make_async_copy(x_refs[w].at[l], mine, local_sems.at[t]).start()
            for r in range(1, N_DEV):
                pltpu.make_async_remote_copy(
                    src_ref=x_refs[w].at[l], dst_ref=mine,
                    send_sem=send_sems.at[N_PEER * t + r - 1], recv_sem=recv_sems.at[N_PEER * t + r - 1],
                    device_id=_peer(x, y, c, r), device_id_type=MESH).start()
        token[...] = jnp.zeros_like(token)

    n_sem = N_PEER * nz
    outs = pl.pallas_call(
        body, name=name,
        out_shape=(pltpu.SemaphoreType.DMA((n_sem,)), pltpu.SemaphoreType.DMA((n_sem,)), pltpu.SemaphoreType.DMA((nz,)),
                   *[pltpu.HBM(s.shape, s.dtype) for s in shards], *[pltpu.HBM(z.shape, z.dtype) for z in zones],
                   jax.ShapeDtypeStruct((8, LANE), F32)),
        in_specs=[HBM] * (ns + nz) + [pl.BlockSpec(memory_space=pl.ANY)] * na,
        out_specs=(SEMS, SEMS, SEMS, *[HBM] * (ns + nz), pl.BlockSpec(memory_space=pltpu.VMEM)),
        input_output_aliases={i: 3 + i for i in range(ns + nz)},
        compiler_params=pltpu.CompilerParams(has_side_effects=DATAFLOW),
    )(*[_in_hbm(s) for s in shards], *[_in_hbm(z) for z in zones], *after)
    return outs[:3], list(outs[3:3 + ns]), list(outs[3 + ns:3 + ns + nz]), outs[-1]


def gather_wait(zones, idx, items, sems, after, keep, name):
    nz, nk = len(zones), len(keep)

    def body(*refs):
        z_refs = refs[:nz]
        send_sems, recv_sems, local_sems = refs[nz:nz + 3]
        x, y, c = _pos()
        me = 4 * x + 2 * y + c
        for z_ref, t in zip(z_refs, idx):
            w, l, kind, a, b = items[t]
            mine = _win2(z_ref, kind, me, a, b)
            pltpu.make_async_copy(mine, mine, local_sems.at[t]).wait()
            for r in range(1, N_DEV):
                peer = _peer(x, y, c, r)
                cp = pltpu.make_async_remote_copy(
                    src_ref=mine, dst_ref=_win2(z_ref, kind, 4 * peer[0] + 2 * peer[1] + peer[2], a, b),
                    send_sem=send_sems.at[N_PEER * t + r - 1], recv_sem=recv_sems.at[N_PEER * t + r - 1],
                    device_id=peer, device_id_type=MESH)
                cp.wait_send()
                cp.wait_recv()

    outs = pl.pallas_call(
        body, name=name, out_shape=tuple(pltpu.HBM(z.shape, z.dtype) for z in zones),
        in_specs=[HBM] * nz + [SEMS] * 3 + [pl.BlockSpec(memory_space=pl.ANY)] * (1 + nk),
        out_specs=tuple([HBM] * nz), input_output_aliases={i: i for i in range(nz)},
        compiler_params=pltpu.CompilerParams(has_side_effects=DATAFLOW),
    )(*zones, *sems, after, *keep)
    return list(outs)


def gather_now(shards, kinds, name):
    nw = len(shards)
    geo = [s.shape[1:] for s in shards]

    def body(*refs):
        x_refs, o_refs = refs[:nw], refs[nw:2 * nw]
        send_sems, recv_sems, local_sems = refs[2 * nw:]
        x, y, c = _pos()
        me, sibling = (x, y, c), (x, y, 1 - c)
        chips = [(1 - x, y), (x, 1 - y), (1 - x, 1 - y)]

        def slot(w, px, py, pc):
            return _win(o_refs[w], kinds[w], 4 * px + 2 * py + pc, *geo[w])

        def copy(w, k, block, to, src=None):
            return pltpu.make_async_remote_copy(
                src_ref=slot(w, *block) if src is None else src, dst_ref=slot(w, *block),
                send_sem=send_sems.at[7 * w + k], recv_sem=recv_sems.at[7 * w + k], device_id=to, device_id_type=MESH)

        mine = [pltpu.make_async_copy(x_refs[w], slot(w, *me), local_sems.at[w]) for w in range(nw)]
        for cp in mine:
            cp.start()
        first = []
        for w in range(nw):
            first.append(copy(w, 0, me, sibling, src=x_refs[w]))
            first += [copy(w, 1 + j, me, (*chip, c), src=x_refs[w]) for j, chip in enumerate(chips)]
        for cp in first:
            cp.start()
        passed = []
        for w in range(nw):
            for j, chip in enumerate(chips):
                copy(w, 1 + j, (*chip, c), me).wait_recv()
                cp = copy(w, 4 + j, (*chip, c), sibling)
                cp.start()
                passed.append(cp)
        for w in range(nw):
            copy(w, 0, sibling, me).wait_recv()
            for j, chip in enumerate(chips):
                copy(w, 4 + j, (*chip, 1 - c), me).wait_recv()
        for cp in first + passed:
            cp.wait_send()
        for cp in mine:
            cp.wait()

    return pl.pallas_call(
        body, name=name,
        out_shape=[jax.ShapeDtypeStruct(_full_shape(s.shape, k), s.dtype) for s, k in zip(shards, kinds)],
        in_specs=[ANY] * nw, out_specs=[ANY] * nw,
        scratch_shapes=[pltpu.SemaphoreType.DMA((7 * nw,)), pltpu.SemaphoreType.DMA((7 * nw,)),
                        pltpu.SemaphoreType.DMA((nw,))],
    )(*shards)


def _src_win(ref, l, kind, k, a, b):
    return _win2(ref if l is None else ref.at[l], kind, k, a, b)


def rs_start(srcs, items, name):
    ns, nz = len(srcs), len(items)
    zones = [lax.empty((N_PEER, a, b), srcs[w].dtype) for w, l, kind, a, b in items]

    def body(*refs):
        s_refs = refs[:ns]
        send_sems, recv_sems = refs[ns + nz], refs[ns + nz + 1]
        z_refs = refs[ns + nz + 2 + ns:ns + nz + 2 + ns + nz]
        token = refs[-1]
        x, y, c = _pos()
        for t, (w, l, kind, a, b) in enumerate(items):
            for r in range(1, N_DEV):
                peer = _peer(x, y, c, r)
                pltpu.make_async_remote_copy(
                    src_ref=_src_win(s_refs[w], l, kind, 4 * peer[0] + 2 * peer[1] + peer[2], a, b),
                    dst_ref=z_refs[t].at[r - 1],
                    send_sem=send_sems.at[N_PEER * t + r - 1], recv_sem=recv_sems.at[N_PEER * t + r - 1],
                    device_id=peer, device_id_type=MESH).start()
        token[...] = jnp.zeros_like(token)

    n_sem = N_PEER * nz
    outs = pl.pallas_call(
        body, name=name,
        out_shape=(pltpu.SemaphoreType.DMA((n_sem,)), pltpu.SemaphoreType.DMA((n_sem,)),
                   *[pltpu.HBM(s.shape, s.dtype) for s in srcs], *[pltpu.HBM(z.shape, z.dtype) for z in zones],
                   jax.ShapeDtypeStruct((8, LANE), F32)),
        in_specs=[HBM] * (ns + nz), out_specs=(SEMS, SEMS, *[HBM] * (ns + nz), pl.BlockSpec(memory_space=pltpu.VMEM)),
        input_output_aliases={i: 2 + i for i in range(ns + nz)},
        compiler_params=pltpu.CompilerParams(has_side_effects=DATAFLOW),
    )(*[_in_hbm(s) for s in srcs], *[_in_hbm(z) for z in zones])
    return outs[:2], list(outs[2:2 + ns]), list(outs[2 + ns:2 + ns + nz]), outs[-1]


def rs_wait(zones, items, sems, after, keep, name):
    nz, nk = len(zones), len(keep)

    def body(*refs):
        z_refs = refs[:nz]
        send_sems, recv_sems = refs[nz], refs[nz + 1]
        x, y, c = _pos()
        for t, z_ref in enumerate(z_refs):
            for r in range(1, N_DEV):
                cp = pltpu.make_async_remote_copy(
                    src_ref=z_ref.at[r - 1], dst_ref=z_ref.at[r - 1],
                    send_sem=send_sems.at[N_PEER * t + r - 1], recv_sem=recv_sems.at[N_PEER * t + r - 1],
                    device_id=_peer(x, y, c, r), device_id_type=MESH)
                cp.wait_send()
                cp.wait_recv()

    outs = pl.pallas_call(
        body, name=name, out_shape=tuple(pltpu.HBM(z.shape, z.dtype) for z in zones),
        in_specs=[HBM] * nz + [SEMS] * 2 + [pl.BlockSpec(memory_space=pl.ANY)] * (1 + nk),
        out_specs=tuple([HBM] * nz), input_output_aliases={i: i for i in range(nz)},
        compiler_params=pltpu.CompilerParams(has_side_effects=DATAFLOW),
    )(*zones, *sems, after, *keep)
    return list(outs)


def adam_rs(w, m, v, l, own, kind, zone, outs, name):
    n, a, b = w.shape
    ta = max(t for t in range(16, min(a, 256) + 1, 16) if a % t == 0)
    per = a // ta
    me = (4 * lax.axis_index("x") + 2 * lax.axis_index("y") + lax.axis_index("c")).astype(jnp.int32).reshape(1)

    def body(me_ref, w_ref, m_ref, v_ref, own_ref, z_ref, i0, i1, i2, i3, g_ref, d_ref, m2_ref, v2_ref):
        gv = own_ref[...].astype(F32)
        for k in range(N_PEER):
            gv = gv + z_ref[k].astype(F32)
        m2 = ADAM_B1 * m_ref[...] + (1.0 - ADAM_B1) * gv
        v2 = ADAM_B2 * v_ref[...] + (1.0 - ADAM_B2) * (gv * gv)
        m_hat = m2 / (1.0 - ADAM_B1 ** ADAM_STEP)
        v_hat = v2 / (1.0 - ADAM_B2 ** ADAM_STEP)
        g_ref[...] = gv
        d_ref[...] = -ADAM_LR * (m_hat / (jnp.sqrt(v_hat) + ADAM_EPS) + ADAM_WD * w_ref[...])
        m2_ref[...] = m2
        v2_ref[...] = v2

    spec = pl.BlockSpec((None, ta, b), lambda r, me_ref: (l, r, 0))
    if kind == "lead":
        own_spec = pl.BlockSpec((None, ta, b), lambda r, me_ref: (me_ref[0], r, 0))
    elif kind == "row":
        own_spec = pl.BlockSpec((None, ta, b), lambda r, me_ref: (l, me_ref[0] * per + r, 0))
    else:
        own_spec = pl.BlockSpec((None, ta, b), lambda r, me_ref: (l, r, me_ref[0]))
    return pl.pallas_call(
        body, name=name, out_shape=[jax.ShapeDtypeStruct((n, a, b), F32)] * 4,
        grid_spec=pltpu.PrefetchScalarGridSpec(
            num_scalar_prefetch=1, grid=(per,),
            in_specs=[spec] * 3 + [own_spec, pl.BlockSpec((N_PEER, ta, b), lambda r, me_ref: (0, r, 0))] + [ANY] * 4,
            out_specs=[spec] * 4),
        input_output_aliases={6 + k: k for k in range(4)},
        compiler_params=_cp("parallel"),
    )(me, w, m, v, own, zone, *outs)


def small_exchange(sh, rep, name):
    _, Rs, C = sh.shape
    Rr = rep.shape[0]

    def body(sh_ref, rep_ref, sh_out, rep_out, send_sems, recv_sems, local_sems):
        x, y, c = _pos()
        me = 4 * x + 2 * y + c
        l1 = pltpu.make_async_copy(sh_ref.at[me], sh_out.at[me], local_sems.at[0])
        l2 = pltpu.make_async_copy(rep_ref, rep_out.at[me], local_sems.at[1])
        l1.start()
        l2.start()

        def flip(v, bit):
            return 1 - v if bit else v

        sends, recvs = [], []
        for r in range(1, N_DEV):
            peer = (flip(x, r & 4), flip(y, r & 2), flip(c, r & 1))
            pid = 4 * peer[0] + 2 * peer[1] + peer[2]
            k = 2 * (r - 1)
            mk = lambda src, dst, kk: pltpu.make_async_remote_copy(
                src_ref=src, dst_ref=dst, send_sem=send_sems.at[kk], recv_sem=recv_sems.at[kk],
                device_id=peer, device_id_type=MESH)
            sends += [mk(sh_ref.at[pid], sh_out.at[me], k), mk(rep_ref, rep_out.at[me], k + 1)]
            recvs += [mk(sh_ref.at[me], sh_out.at[pid], k), mk(rep_ref, rep_out.at[pid], k + 1)]
        for cp in sends:
            cp.start()
        for cp in recvs:
            cp.wait_recv()
        for cp in sends:
            cp.wait_send()
        l1.wait()
        l2.wait()

    n = 2 * (N_DEV - 1)
    return pl.pallas_call(
        body, name=name,
        out_shape=[jax.ShapeDtypeStruct((N_DEV, Rs, C), sh.dtype), jax.ShapeDtypeStruct((N_DEV, *rep.shape), rep.dtype)],
        in_specs=[ANY, ANY], out_specs=[ANY, ANY],
        scratch_shapes=[pltpu.SemaphoreType.DMA((n,)), pltpu.SemaphoreType.DMA((n,)), pltpu.SemaphoreType.DMA((2,))],
    )(sh, rep)


def adam_slots(w, m, v, slots, name):
    S, n, a, b = slots.shape
    ta = max(t for t in range(16, min(a, 512) + 1, 8)
             if a % t == 0 and t * S * b * slots.dtype.itemsize <= 4 * 1024 * 1024)

    def body(w_ref, m_ref, v_ref, s_ref, g_ref, d_ref, m2_ref, v2_ref):
        gv = s_ref[0].astype(F32)
        for k in range(1, S):
            gv = gv + s_ref[k].astype(F32)
        m2 = ADAM_B1 * m_ref[...] + (1.0 - ADAM_B1) * gv
        v2 = ADAM_B2 * v_ref[...] + (1.0 - ADAM_B2) * (gv * gv)
        m_hat = m2 / (1.0 - ADAM_B1 ** ADAM_STEP)
        v_hat = v2 / (1.0 - ADAM_B2 ** ADAM_STEP)
        g_ref[...] = gv
        d_ref[...] = -ADAM_LR * (m_hat / (jnp.sqrt(v_hat) + ADAM_EPS) + ADAM_WD * w_ref[...])
        m2_ref[...] = m2
        v2_ref[...] = v2

    spec = pl.BlockSpec((None, ta, b), lambda l, r: (l, r, 0))
    return pl.pallas_call(
        body, name=name, grid=(n, a // ta),
        in_specs=[spec] * 3 + [pl.BlockSpec((S, None, ta, b), lambda l, r: (0, l, r, 0))], out_specs=[spec] * 4,
        out_shape=[jax.ShapeDtypeStruct((n, a, b), F32)] * 4, compiler_params=_cp("parallel", "parallel"),
    )(w, m, v, slots)


WEIGHTS = ["norm_g", "ssm_in_w", "ssm_conv_w", "ssm_conv_b", "ssm_dt_bias", "ssm_A_log", "ssm_D", "ssm_norm_g",
           "ssm_out_w", "cf_pw1_w", "cf_pw1_b", "cf_dw_w", "cf_dw_b", "cf_ln_g", "cf_ln_b", "cf_pw2_w", "cf_pw2_b",
           "xa_mem_g", "xa_q_w", "xa_kv_w", "xa_o_w", "ffn_in_w", "ffn_conv_w", "ffn_conv_b", "ffn_out_w"]
ARGS = ["x", "mem"] + WEIGHTS + ["loss_target"] + ["m_" + n for n in WEIGHTS] + ["v_" + n for n in WEIGHTS]
BIG = {"ssm_in_w": "col", "ssm_out_w": "row", "cf_pw1_w": "col", "cf_pw2_w": "row", "xa_q_w": "row",
       "xa_kv_w": "col", "xa_o_w": "row", "ffn_in_w": "col", "ffn_out_w": "row"}
SMALL = ["norm_g", "ssm_conv_w", "cf_pw1_b", "cf_dw_w", "cf_dw_b", "cf_ln_g", "cf_ln_b", "cf_pw2_b", "ffn_conv_w"]
REP = ["ssm_conv_b", "ssm_dt_bias", "ssm_A_log", "ssm_D", "ssm_norm_g", "xa_mem_g", "ffn_conv_b"]
SMALL_W = 768
REP_W = 512


def _r8(n):
    return -(-n // 8) * 8


def _stack2d(arrs, wid):
    parts = []
    for a in arrs:
        r, c = a.shape[-2:]
        parts.append(jnp.pad(a, [(0, 0)] * (a.ndim - 2) + [(0, _r8(r) - r), (0, wid - c)]))
    return jnp.concatenate(parts, axis=-2)


def _unstack2d(buf, shapes2d):
    out, o = [], 0
    for r, c in shapes2d:
        out.append(buf[..., o:o + r, :c])
        o += _r8(r)
    return out


def _gathered_to_full(g):
    lead = g.shape[1:-1]
    return jnp.moveaxis(g, 0, -2).reshape(*lead, N_DEV * g.shape[-1])


def _full_to_slots(w):
    lead = w.shape[:-1]
    return jnp.moveaxis(w.reshape(*lead, N_DEV, w.shape[-1] // N_DEV), -2, 0)


def kernel(x, mem, norm_g, ssm_in_w, ssm_conv_w, ssm_conv_b, ssm_dt_bias, ssm_A_log, ssm_D, ssm_norm_g, ssm_out_w, cf_pw1_w, cf_pw1_b, cf_dw_w, cf_dw_b, cf_ln_g, cf_ln_b, cf_pw2_w, cf_pw2_b, xa_mem_g, xa_q_w, xa_kv_w, xa_o_w, ffn_in_w, ffn_conv_w, ffn_conv_b, ffn_out_w, loss_target, m_norm_g, m_ssm_in_w, m_ssm_conv_w, m_ssm_conv_b, m_ssm_dt_bias, m_ssm_A_log, m_ssm_D, m_ssm_norm_g, m_ssm_out_w, m_cf_pw1_w, m_cf_pw1_b, m_cf_dw_w, m_cf_dw_b, m_cf_ln_g, m_cf_ln_b, m_cf_pw2_w, m_cf_pw2_b, m_xa_mem_g, m_xa_q_w, m_xa_kv_w, m_xa_o_w, m_ffn_in_w, m_ffn_conv_w, m_ffn_conv_b, m_ffn_out_w, v_norm_g, v_ssm_in_w, v_ssm_conv_w, v_ssm_conv_b, v_ssm_dt_bias, v_ssm_A_log, v_ssm_D, v_ssm_norm_g, v_ssm_out_w, v_cf_pw1_w, v_cf_pw1_b, v_cf_dw_w, v_cf_dw_b, v_cf_ln_g, v_cf_ln_b, v_cf_pw2_w, v_cf_pw2_b, v_xa_mem_g, v_xa_q_w, v_xa_kv_w, v_xa_o_w, v_ffn_in_w, v_ffn_conv_w, v_ffn_conv_b, v_ffn_out_w):
    return _step(x, mem, norm_g, ssm_in_w, ssm_conv_w, ssm_conv_b, ssm_dt_bias, ssm_A_log, ssm_D, ssm_norm_g, ssm_out_w, cf_pw1_w, cf_pw1_b, cf_dw_w, cf_dw_b, cf_ln_g, cf_ln_b, cf_pw2_w, cf_pw2_b, xa_mem_g, xa_q_w, xa_kv_w, xa_o_w, ffn_in_w, ffn_conv_w, ffn_conv_b, ffn_out_w, loss_target, m_norm_g, m_ssm_in_w, m_ssm_conv_w, m_ssm_conv_b, m_ssm_dt_bias, m_ssm_A_log, m_ssm_D, m_ssm_norm_g, m_ssm_out_w, m_cf_pw1_w, m_cf_pw1_b, m_cf_dw_w, m_cf_dw_b, m_cf_ln_g, m_cf_ln_b, m_cf_pw2_w, m_cf_pw2_b, m_xa_mem_g, m_xa_q_w, m_xa_kv_w, m_xa_o_w, m_ffn_in_w, m_ffn_conv_w, m_ffn_conv_b, m_ffn_out_w, v_norm_g, v_ssm_in_w, v_ssm_conv_w, v_ssm_conv_b, v_ssm_dt_bias, v_ssm_A_log, v_ssm_D, v_ssm_norm_g, v_ssm_out_w, v_cf_pw1_w, v_cf_pw1_b, v_cf_dw_w, v_cf_dw_b, v_cf_ln_g, v_cf_ln_b, v_cf_pw2_w, v_cf_pw2_b, v_xa_mem_g, v_xa_q_w, v_xa_kv_w, v_xa_o_w, v_ffn_in_w, v_ffn_conv_w, v_ffn_conv_b, v_ffn_out_w)


def _step(*args):
    A = dict(zip(ARGS, args, strict=True))
    x, mem, target = A["x"][0], A["mem"][0], A["loss_target"][0]

    big = list(BIG)
    geo = [A[n].shape for n in big]
    kinds = ["row" if BIG[n] == "row" else ("col" if A[n].shape[-1] % LANE == 0 else "lead") for n in big]
    W = {n: A[n] for n in REP}
    small2d = [(A[n].size // A[n].shape[-1], A[n].shape[-1]) for n in SMALL]
    rep2d = [(A[n].size // REP_W, REP_W) if A[n].shape[-1] % REP_W == 0 else A[n].shape for n in REP] + [(1, 1)]
    stack_small = lambda pre: _stack2d([A[pre + n].reshape(rc) for n, rc in zip(SMALL, small2d)], SMALL_W)
    stack_rep = lambda pre: _stack2d([A[pre + n].reshape(rc) for n, rc in zip(REP, rep2d)] + [jnp.zeros((1, 1), F32)],
                                     REP_W)
    small_g = all_gather(stack_small(""), name="gather_small")
    for n, g in zip(SMALL, _unstack2d(small_g, small2d)):
        W[n] = _gathered_to_full(g.reshape(N_DEV, *A[n].shape))

    shards = [A[n].astype(BF16) for n in big]
    for n in big:
        W[n] = [None] * A[n].shape[0]
    first = _layer_weights(0)
    got0 = gather_now([shards[big.index(n)][l:l + 1] for n, l in first], [kinds[big.index(n)] for n, l in first],
                      name="gather_layer0")
    for (n, l), g in zip(first, got0):
        W[n][l] = _gathered_to_full(g)[0] if kinds[big.index(n)] == "lead" else g[0]
    items, layer_items = [], [[]]
    for i in range(1, DEPTH):
        layer_items.append([])
        for n, l in _layer_weights(i):
            w = big.index(n)
            layer_items[i].append(len(items))
            items.append((w, l, kinds[w], *geo[w][1:]))
    sems, shards_thru, zones, token = gather_start(shards, items, [small_g, got0[0]], name="gather_start")
    mem = mem + token[0, 0]

    def fetch(i, x_in):
        ids = layer_items[i]
        if not ids:
            return
        got = gather_wait([zones[t] for t in ids], ids, items, sems, x_in, shards_thru if i == DEPTH - 1 else [],
                          name=f"gather_wait{i}")
        for t, z in zip(ids, got):
            w, l, kind = items[t][:3]
            W[big[w]][l] = _gathered_to_full(z) if kind == "lead" else z

    sent = []
    final = {}

    def layer_done(i, GB, g_in):
        srcs, its = [], []
        for n, l in _layer_weights(i):
            w = big.index(n)
            if kinds[w] == "lead":
                srcs.append(_full_to_slots(g_in if n == "ssm_in_w" else GB[n][l]))
                its.append((len(srcs) - 1, None, "lead", *geo[w][1:], n, l))
            else:
                srcs.append(GB[n])
                its.append((len(srcs) - 1, l, kinds[w], *geo[w][1:], n, l))
        sems_i, thru, zones_i, token_i = rs_start(srcs, [it[:5] for it in its], name=f"rs_start{i}")
        for it, s in zip(its, thru):
            if it[2] != "lead":
                GB[it[5]] = s
        sent.append((its, sems_i, [s for it, s in zip(its, thru) if it[2] == "lead"], zones_i))
        final["GB"] = GB
        return token_i

    loss, grad_x, G = local_step(x, mem, target, W, fetch, layer_done)

    sh = _stack2d([_full_to_slots(G[n]).reshape(N_DEV, *rc) for n, rc in zip(SMALL, small2d)], SMALL_W)
    rep = _stack2d([G[n].reshape(rc) for n, rc in zip(REP, rep2d)] + [loss[:, :1]], REP_W)
    sh_got, rep_got = small_exchange(sh, rep, name="small_exchange")

    res = {}
    GBf = final["GB"]
    bufs = {n: [lax.empty(A[n].shape, F32) for _ in range(4)] for n in big}
    for i, (its, sems_i, lead_srcs, zones_i) in enumerate(sent):
        keep = lead_srcs + [GBf[it[5]] for it in its if it[2] != "lead"]
        zones_i = rs_wait(zones_i, [it[:5] for it in its], sems_i, sh_got, keep, name=f"rs_wait{i}")
        lead_it = iter(lead_srcs)
        for it, z in zip(its, zones_i):
            n, l = it[5], it[6]
            own = next(lead_it) if it[2] == "lead" else GBf[n]
            bufs[n] = adam_rs(A[n], A["m_" + n], A["v_" + n], l, own, it[2], z, bufs[n], name=f"adam_{n}{l}")
    for n in big:
        res[n] = tuple(bufs[n])
    for names, shapes2d, stack, slots, tag in ((SMALL, small2d, stack_small, sh_got, "small"),
                                               (REP, rep2d, stack_rep, rep_got, "rep")):
        outs4 = adam_slots(stack("")[None], stack("m_")[None], stack("v_")[None], slots[:, None], name=f"adam_{tag}")
        parts = [_unstack2d(o[0], shapes2d) for o in outs4]
        for k, n in enumerate(names):
            res[n] = tuple(q[k].reshape(A[n].shape) for q in parts)
        if tag == "rep":
            total_loss = parts[0][-1][0, 0]

    outs = [total_loss, grad_x[None]]
    for k in range(4):
        outs += [res[n][k] for n in WEIGHTS]
    return tuple(outs)
```

```python
import jax
import jax.numpy as jnp
from jax import lax
from jax.experimental import pallas as pl
from jax.experimental.pallas import tpu as pltpu

F32 = jnp.float32
BF16 = jnp.bfloat16

D_MODEL = 1024
D_INNER = 2048
N_HEADS = 32
HEAD_DIM = 64
N_GROUPS = 4
D_STATE = 128
CHUNK = 128
CONV_DIM = 3072
SSM_K = 4
CF_K = 31
N_MEM = 256
XA_HEADS = 4
XA_HD = 256
D_FF = 2816
FFN_K = 3
EPS = 1e-6
DEPTH = 4
N_DEV = 8

ADAM_LR = 0.001
ADAM_B1 = 0.9
ADAM_B2 = 0.999
ADAM_EPS = 1e-08
ADAM_WD = 0.01
ADAM_STEP = 10

LANE = 128
VMEM_LIMIT = 56 * 1024 * 1024
NEG = -1e30
MESH = pl.DeviceIdType.MESH


def _cp(*sem):
    return pltpu.CompilerParams(dimension_semantics=sem if sem else None, vmem_limit_bytes=VMEM_LIMIT)


def _tile(n, cap):
    if n <= cap:
        return n
    best = 0
    for t in range(LANE, cap + 1, LANE):
        if n % t == 0:
            best = t
    assert best, (n, cap)
    return best


def _sig(x):
    return 1.0 / (1.0 + jnp.exp(-x))


def _split3(v):
    v0 = v.astype(BF16)
    r1 = v - v0.astype(F32)
    v1 = r1.astype(BF16)
    v2 = (r1 - v1.astype(F32)).astype(BF16)
    return v0, v1, v2


def _dot(a, b, ca=1, cb=0):
    return lax.dot_general(a, b, (((ca,), (cb,)), ((), ())), preferred_element_type=F32)


def _dot3(v, m, ca=1, cb=0):
    v0, v1, v2 = _split3(v)
    return _dot(v0, m, ca, cb) + _dot(v1, m, ca, cb) + _dot(v2, m, ca, cb)


def mm(a, b, *, ta=False, tb=False, bias=None, acc=None, out_dtype=F32, a_idx=None, layer=None, b_k0=0, b_n=None,
       into=None, name):
    if isinstance(b, (list, tuple)):
        b, layer = b[layer], None
    if ta:
        K, M = a.shape[-2:]
    else:
        M, K = a.shape[-2:]
    N = b_n if b_n is not None else (b.shape[-2] if tb else b.shape[-1])
    assert (b.ndim == 3) == (layer is not None) and (a.ndim == 3) == (a_idx is not None)
    tm = _tile(M, 1024)
    tn = _tile(N, 1536)
    tk = _tile(K, 2048)
    nk = K // tk
    assert b_k0 % tk == 0 and b_k0 + K <= (b.shape[-1] if tb else b.shape[-2])
    kb = b_k0 // tk
    has_bias, has_acc = bias is not None, acc is not None
    if into is not None:
        out_dtype = into[0].dtype
        assert into[0].shape[1] == M and into[2] % tn == 0 and into[2] + N <= into[0].shape[2] and not has_acc

    def body(*refs):
        a_ref, b_ref = refs[0], refs[1]
        pos = 2
        bias_ref = acc_ref = None
        if has_bias:
            bias_ref = refs[pos]
            pos += 1
        if has_acc:
            acc_ref = refs[pos]
            pos += 1
        if into is not None:
            pos += 1
        o_ref = refs[pos]
        s_ref = refs[pos + 1] if nk > 1 else None
        p = _dot(a_ref[...].astype(BF16), b_ref[...].astype(BF16), 0 if ta else 1, 1 if tb else 0)

        def extras(v):
            if has_bias:
                v = v + bias_ref[...]
            if has_acc:
                v = v + acc_ref[...]
            return v

        if nk == 1:
            o_ref[...] = extras(p).astype(out_dtype)
        else:
            k = pl.program_id(2)

            @pl.when(k == 0)
            def _():
                s_ref[...] = extras(p)

            @pl.when(k > 0)
            def _():
                s_ref[...] += p

            @pl.when(k == nk - 1)
            def _():
                o_ref[...] = s_ref[...].astype(out_dtype)

    lead_a = () if a_idx is None else (a_idx,)
    lead_b = () if layer is None else (layer,)
    sq = lambda lead: (None,) * len(lead)
    if ta:
        a_spec = pl.BlockSpec((*sq(lead_a), tk, tm), lambda i, j, k: (*lead_a, k, i))
    else:
        a_spec = pl.BlockSpec((*sq(lead_a), tm, tk), lambda i, j, k: (*lead_a, i, k))
    if tb:
        b_spec = pl.BlockSpec((*sq(lead_b), tn, tk), lambda i, j, k: (*lead_b, j, k + kb))
    else:
        b_spec = pl.BlockSpec((*sq(lead_b), tk, tn), lambda i, j, k: (*lead_b, k + kb, j))
    in_specs, args = [a_spec, b_spec], [a, b]
    if has_bias:
        in_specs.append(pl.BlockSpec((1, tn), lambda i, j, k: (0, j)))
        args.append(bias.reshape(1, N).astype(F32))
    if has_acc:
        in_specs.append(pl.BlockSpec((tm, tn), lambda i, j, k: (i, j)))
        args.append(acc)
    if into is None:
        out_spec = pl.BlockSpec((tm, tn), lambda i, j, k: (i, j))
        out_shape = jax.ShapeDtypeStruct((M, N), out_dtype)
        aliases = {}
    else:
        buf, l, col0 = into
        cb = col0 // tn
        in_specs.append(pl.BlockSpec(memory_space=pl.ANY))
        args.append(buf)
        out_spec = pl.BlockSpec((None, tm, tn), lambda i, j, k: (l, i, j + cb))
        out_shape = jax.ShapeDtypeStruct(buf.shape, buf.dtype)
        aliases = {len(args) - 1: 0}
    return pl.pallas_call(
        body, name=name, grid=(M // tm, N // tn, nk),
        in_specs=in_specs, out_specs=out_spec, out_shape=out_shape, input_output_aliases=aliases,
        scratch_shapes=[pltpu.VMEM((tm, tn), F32)] if nk > 1 else [],
        compiler_params=_cp("parallel", "parallel", "arbitrary"),
    )(*args)


def colsum(x, name):
    L, C = x.shape
    tr = _tile(L, 512)
    tc = _tile(C, 1024)

    def body(x_ref, o_ref):
        @pl.when(pl.program_id(1) == 0)
        def _():
            o_ref[...] = jnp.zeros_like(o_ref)

        o_ref[...] += jnp.sum(x_ref[...].astype(F32), axis=0, keepdims=True)

    return pl.pallas_call(
        body, name=name, grid=(C // tc, L // tr),
        in_specs=[pl.BlockSpec((tr, tc), lambda j, i: (i, j))],
        out_specs=pl.BlockSpec((1, tc), lambda j, i: (0, j)),
        out_shape=jax.ShapeDtypeStruct((1, C), F32),
        compiler_params=_cp("parallel", "arbitrary"),
    )(x)


TR = 256


def _row_spec(tr, w):
    return pl.BlockSpec((tr, w), lambda i: (i, 0))


def _vec_spec(w):
    return pl.BlockSpec((1, w), lambda i: (0, 0))


def _rms(v):
    return lax.rsqrt(jnp.mean(v * v, axis=-1, keepdims=True) + EPS)


def norm_fwd(x, g, name):
    L, D = x.shape
    tr = min(TR, L)

    def body(x_ref, g_ref, h_ref):
        xv = x_ref[...]
        h_ref[...] = (xv * _rms(xv) * g_ref[...]).astype(BF16)

    return pl.pallas_call(
        body, name=name, grid=(L // tr,),
        in_specs=[_row_spec(tr, D), _vec_spec(D)], out_specs=_row_spec(tr, D),
        out_shape=jax.ShapeDtypeStruct((L, D), BF16), compiler_params=_cp("parallel"),
    )(x, g.reshape(1, D))


def bnd_fwd(x, f, gpost, gpre, name):
    L, D = x.shape
    tr = min(TR, L)

    def body(x_ref, f_ref, gp_ref, gn_ref, xo_ref, h_ref):
        fv = f_ref[...]
        xn = x_ref[...] + fv * _rms(fv) * gp_ref[...]
        xo_ref[...] = xn
        h_ref[...] = (xn * _rms(xn) * gn_ref[...]).astype(BF16)

    return pl.pallas_call(
        body, name=name, grid=(L // tr,),
        in_specs=[_row_spec(tr, D), _row_spec(tr, D), _vec_spec(D), _vec_spec(D)],
        out_specs=[_row_spec(tr, D), _row_spec(tr, D)],
        out_shape=[jax.ShapeDtypeStruct((L, D), F32), jax.ShapeDtypeStruct((L, D), BF16)],
        compiler_params=_cp("parallel"),
    )(x, f, gpost.reshape(1, D), gpre.reshape(1, D))


def final_fwd(x, f, gpost, target, name):
    L, D = x.shape
    tr = min(TR, L)
    n = L // tr

    def body(x_ref, f_ref, gp_ref, t_ref, dy_ref, loss_ref, acc_ref):
        i = pl.program_id(0)

        @pl.when(i == 0)
        def _():
            acc_ref[...] = jnp.zeros_like(acc_ref)

        fv = f_ref[...]
        e = x_ref[...] + fv * _rms(fv) * gp_ref[...] - t_ref[...]
        dy_ref[...] = e * (1.0 / D)
        acc_ref[...] += jnp.sum(e * e, axis=0, keepdims=True)

        @pl.when(i == n - 1)
        def _():
            loss_ref[...] = jnp.full((1, LANE), 0.5 / D, F32) * jnp.sum(acc_ref[...])

    return pl.pallas_call(
        body, name=name, grid=(n,),
        in_specs=[_row_spec(tr, D), _row_spec(tr, D), _vec_spec(D), _row_spec(tr, D)],
        out_specs=[_row_spec(tr, D), _vec_spec(LANE)],
        out_shape=[jax.ShapeDtypeStruct((L, D), F32), jax.ShapeDtypeStruct((1, LANE), F32)],
        scratch_shapes=[pltpu.VMEM((1, D), F32)],
        compiler_params=_cp("arbitrary"),
    )(x, f, gpost.reshape(1, D), target)


def _rms_bwd(v, g, dy):
    r = _rms(v)
    vn = v * r
    dg = jnp.sum(dy * vn, axis=0, keepdims=True)
    dvn = dy * g
    dv = r * (dvn - vn * jnp.mean(dvn * vn, axis=-1, keepdims=True))
    return dv, dg


def bnd_bwd(dxp, *, pre=None, post=None, dep=None, name):
    L, D = dxp.shape
    tr = min(TR, L)
    has_pre, has_post = pre is not None, post is not None

    def body(*refs):
        pos = 0
        dxp_ref = refs[pos]; pos += 1
        if has_pre:
            x_ref, gpre_ref, dh_ref = refs[pos:pos + 3]; pos += 3
        if has_post:
            f_ref, gpost_ref = refs[pos:pos + 2]; pos += 2
        if dep is not None:
            pos += 1
        if has_pre:
            dx_ref, dgpre_ref = refs[pos:pos + 2]; pos += 2
        if has_post:
            df_ref, dgpost_ref = refs[pos:pos + 2]; pos += 2
        i = pl.program_id(0)
        dx = dxp_ref[...]
        if has_pre:
            d, dg = _rms_bwd(x_ref[...], gpre_ref[...], dh_ref[...])
            dx = dx + d
            dx_ref[...] = dx

            @pl.when(i == 0)
            def _():
                dgpre_ref[...] = jnp.zeros_like(dgpre_ref)

            dgpre_ref[...] += dg
        if has_post:
            d, dg = _rms_bwd(f_ref[...], gpost_ref[...], dx)
            df_ref[...] = d.astype(BF16)

            @pl.when(i == 0)
            def _():
                dgpost_ref[...] = jnp.zeros_like(dgpost_ref)

            dgpost_ref[...] += dg

    in_specs, args = [_row_spec(tr, D)], [dxp]
    out_specs, out_shape, names = [], [], []
    if has_pre:
        x, gpre, dh = pre
        in_specs += [_row_spec(tr, D), _vec_spec(D), _row_spec(tr, D)]
        args += [x, gpre.reshape(1, D), dh]
        out_specs += [_row_spec(tr, D), _vec_spec(D)]
        out_shape += [jax.ShapeDtypeStruct((L, D), F32), jax.ShapeDtypeStruct((1, D), F32)]
        names += ["dx", "dgpre"]
    if has_post:
        f, gpost = post
        in_specs += [_row_spec(tr, D), _vec_spec(D)]
        args += [f, gpost.reshape(1, D)]
        out_specs += [_row_spec(tr, D), _vec_spec(D)]
        out_shape += [jax.ShapeDtypeStruct((L, D), BF16), jax.ShapeDtypeStruct((1, D), F32)]
        names += ["df", "dgpost"]
    if dep is not None:
        in_specs.append(pl.BlockSpec(memory_space=pl.ANY))
        args.append(dep)
    outs = pl.pallas_call(
        body, name=name, grid=(L // tr,), in_specs=in_specs, out_specs=out_specs, out_shape=out_shape,
        compiler_params=_cp("arbitrary"),
    )(*args)
    return dict(zip(names, outs))


def norm_dg(x, dy, name):
    L, D = x.shape
    tr = min(TR, L)

    def body(x_ref, dy_ref, o_ref):
        @pl.when(pl.program_id(0) == 0)
        def _():
            o_ref[...] = jnp.zeros_like(o_ref)

        xv = x_ref[...]
        o_ref[...] += jnp.sum(dy_ref[...] * xv * _rms(xv), axis=0, keepdims=True)

    return pl.pallas_call(
        body, name=name, grid=(L // tr,),
        in_specs=[_row_spec(tr, D), _row_spec(tr, D)], out_specs=_vec_spec(D),
        out_shape=jax.ShapeDtypeStruct((1, D), F32), compiler_params=_cp("arbitrary"),
    )(x, dy)


HALO = 32


def _prev_halo_spec(tr, tc, col):
    per = tr // HALO
    return pl.BlockSpec((HALO, tc), lambda *g: (jnp.maximum(g[-1] * per - 1, 0), col(*g)))


def _fill_prev(scr, halo_val, blk_val, i, tr):
    scr[pl.ds(0, HALO), :] = jnp.where(i == 0, 0.0, halo_val)
    scr[pl.ds(HALO, tr), :] = blk_val


def _conv(scr, w_ref, K, tr):
    acc = None
    for k in range(K):
        term = scr[pl.ds(HALO - (K - 1) + k, tr), :] * w_ref[k:k + 1, :]
        acc = term if acc is None else acc + term
    return acc


def _conv_dw(scr, d, o_ref, K, tr):
    for k in range(K):
        o_ref[k:k + 1, :] += jnp.sum(d * scr[pl.ds(HALO - (K - 1) + k, tr), :], axis=0, keepdims=True)


def ssm_conv_fwd(zx, w, b, *, col0, ncols, wcol0, out_dtype, name):
    L = zx.shape[0]
    tr = min(TR, L)
    tc = 1024
    cb, wb = col0 // tc, wcol0 // tc

    def body(x_ref, h_ref, w_ref, b_ref, o_ref, p_ref, scr):
        i = pl.program_id(1)
        _fill_prev(scr, h_ref[...].astype(F32), x_ref[...].astype(F32), i, tr)
        pre = _conv(scr, w_ref, SSM_K, tr) + b_ref[...]
        p_ref[...] = pre.astype(BF16)
        o_ref[...] = (pre * _sig(pre)).astype(out_dtype)

    out = pl.BlockSpec((tr, tc), lambda j, i: (i, j))
    return pl.pallas_call(
        body, name=name, grid=(ncols // tc, L // tr),
        in_specs=[pl.BlockSpec((tr, tc), lambda j, i: (i, j + cb)),
                  _prev_halo_spec(tr, tc, lambda j, i: j + cb),
                  pl.BlockSpec((SSM_K, tc), lambda j, i: (0, j + wb)),
                  pl.BlockSpec((1, tc), lambda j, i: (0, j + wb))],
        out_specs=[out, out],
        out_shape=[jax.ShapeDtypeStruct((L, ncols), out_dtype), jax.ShapeDtypeStruct((L, ncols), BF16)],
        scratch_shapes=[pltpu.VMEM((HALO + tr, tc), F32)],
        compiler_params=_cp("parallel", "parallel"),
    )(zx, zx, w, b)


def ssm_conv_bwd(zx, pre, d, w, *, col0, dcol0, ncols, name):
    L = zx.shape[0]
    tr = min(TR, L)
    tc = 1024
    cb, db_ = col0 // tc, dcol0 // tc
    n = L // tr
    per = tr // HALO
    last = L // HALO - 1

    def body(x_ref, h_ref, p_ref, np_ref, d_ref, nd_ref, w_ref, dx_ref, dw_ref, db_ref, scr, sd):
        i = pl.program_id(1)
        _fill_prev(scr, h_ref[...].astype(F32), x_ref[...].astype(F32), i, tr)

        def dpre(p, dv):
            s = _sig(p)
            return dv * s * (1.0 + p * (1.0 - s))

        dp = dpre(p_ref[...].astype(F32), d_ref[...])
        sd[pl.ds(0, tr), :] = dp
        sd[pl.ds(tr, HALO), :] = jnp.where(i == n - 1, 0.0, dpre(np_ref[...].astype(F32), nd_ref[...]))
        acc = None
        for k in range(SSM_K):
            term = sd[pl.ds(SSM_K - 1 - k, tr), :] * w_ref[k:k + 1, :]
            acc = term if acc is None else acc + term
        dx_ref[...] = acc.astype(BF16)

        @pl.when(i == 0)
        def _():
            dw_ref[...] = jnp.zeros_like(dw_ref)
            db_ref[...] = jnp.zeros_like(db_ref)

        _conv_dw(scr, dp, dw_ref, SSM_K, tr)
        db_ref[...] += jnp.sum(dp, axis=0, keepdims=True)

    nxt = lambda i: jnp.minimum((i + 1) * per, last)
    return pl.pallas_call(
        body, name=name, grid=(ncols // tc, n),
        in_specs=[pl.BlockSpec((tr, tc), lambda j, i: (i, j + cb)),
                  _prev_halo_spec(tr, tc, lambda j, i: j + cb),
                  pl.BlockSpec((tr, tc), lambda j, i: (i, j)),
                  pl.BlockSpec((HALO, tc), lambda j, i: (nxt(i), j)),
                  pl.BlockSpec((tr, tc), lambda j, i: (i, j + db_)),
                  pl.BlockSpec((HALO, tc), lambda j, i: (nxt(i), j + db_)),
                  pl.BlockSpec((SSM_K, tc), lambda j, i: (0, j + db_))],
        out_specs=[pl.BlockSpec((tr, tc), lambda j, i: (i, j)),
                   pl.BlockSpec((SSM_K, tc), lambda j, i: (0, j)),
                   pl.BlockSpec((1, tc), lambda j, i: (0, j))],
        out_shape=[jax.ShapeDtypeStruct((L, ncols), BF16), jax.ShapeDtypeStruct((SSM_K, ncols), F32),
                   jax.ShapeDtypeStruct((1, ncols), F32)],
        scratch_shapes=[pltpu.VMEM((HALO + tr, tc), F32), pltpu.VMEM((tr + HALO, tc), F32)],
        compiler_params=_cp("parallel", "arbitrary"),
    )(zx, zx, pre, pre, d, d, w)


FFN_TC = 1408


def ffn_act_fwd(u, w, b, name):
    L = u.shape[0]
    tr = min(TR, L)
    tc = FFN_TC
    nb = D_FF // tc

    def body(g_ref, hg_ref, v_ref, hv_ref, wg_ref, wv_ref, bg_ref, bv_ref, o_ref, c_ref, sg, sv):
        i = pl.program_id(1)
        _fill_prev(sg, hg_ref[...].astype(F32), g_ref[...].astype(F32), i, tr)
        _fill_prev(sv, hv_ref[...].astype(F32), v_ref[...].astype(F32), i, tr)
        ug = _conv(sg, wg_ref, FFN_K, tr) + bg_ref[...]
        uv = _conv(sv, wv_ref, FFN_K, tr) + bv_ref[...]
        c_ref[0] = ug.astype(BF16)
        c_ref[1] = uv.astype(BF16)
        o_ref[...] = (ug * _sig(ug) * uv).astype(BF16)

    blk = lambda off: pl.BlockSpec((tr, tc), lambda j, i: (i, j + off))
    wsp = lambda off: pl.BlockSpec((FFN_K, tc), lambda j, i: (0, j + off))
    bsp = lambda off: pl.BlockSpec((1, tc), lambda j, i: (0, j + off))
    return pl.pallas_call(
        body, name=name, grid=(nb, L // tr),
        in_specs=[blk(0), _prev_halo_spec(tr, tc, lambda j, i: j),
                  blk(nb), _prev_halo_spec(tr, tc, lambda j, i: j + nb),
                  wsp(0), wsp(nb), bsp(0), bsp(nb)],
        out_specs=[pl.BlockSpec((tr, tc), lambda j, i: (i, j)), pl.BlockSpec((2, tr, tc), lambda j, i: (0, i, j))],
        out_shape=[jax.ShapeDtypeStruct((L, D_FF), BF16), jax.ShapeDtypeStruct((2, L, D_FF), BF16)],
        scratch_shapes=[pltpu.VMEM((HALO + tr, tc), F32), pltpu.VMEM((HALO + tr, tc), F32)],
        compiler_params=_cp("parallel", "parallel"),
    )(u, u, u, u, w, w, b, b)


def ffn_act_bwd(u, c, dact, w, name):
    L = u.shape[0]
    tr = min(TR, L)
    tc = FFN_TC
    nb = D_FF // tc
    n = L // tr
    per = tr // HALO
    last = L // HALO - 1

    def body(g_ref, hg_ref, v_ref, hv_ref, c_ref, nc_ref, da_ref, nda_ref, wg_ref, wv_ref,
             du_ref, dw_ref, db_ref, sg, sv, dg_s, dv_s):
        i = pl.program_id(1)
        _fill_prev(sg, hg_ref[...].astype(F32), g_ref[...].astype(F32), i, tr)
        _fill_prev(sv, hv_ref[...].astype(F32), v_ref[...].astype(F32), i, tr)

        def grads(cg, cv, da):
            s = _sig(cg)
            return da * cv * s * (1.0 + cg * (1.0 - s)), da * cg * s

        dg, dv = grads(c_ref[0].astype(F32), c_ref[1].astype(F32), da_ref[...].astype(F32))
        ndg, ndv = grads(nc_ref[0].astype(F32), nc_ref[1].astype(F32), nda_ref[...].astype(F32))
        at_end = i == n - 1
        for half, (scr, d, nd, x_scr, w_ref) in enumerate(((dg_s, dg, ndg, sg, wg_ref), (dv_s, dv, ndv, sv, wv_ref))):
            scr[pl.ds(0, tr), :] = d
            scr[pl.ds(tr, HALO), :] = jnp.where(at_end, 0.0, nd)
            acc = None
            for k in range(FFN_K):
                term = scr[pl.ds(FFN_K - 1 - k, tr), :] * w_ref[k:k + 1, :]
                acc = term if acc is None else acc + term
            du_ref[half] = acc.astype(BF16)

            @pl.when(i == 0)
            def _():
                dw_ref[half] = jnp.zeros((FFN_K, tc), F32)
                db_ref[half] = jnp.zeros((1, tc), F32)

            for k in range(FFN_K):
                dw_ref[half, k:k + 1, :] += jnp.sum(d * x_scr[pl.ds(HALO - (FFN_K - 1) + k, tr), :], axis=0, keepdims=True)
            db_ref[half] += jnp.sum(d, axis=0, keepdims=True)

    blk = lambda off: pl.BlockSpec((tr, tc), lambda j, i: (i, j + off))
    wsp = lambda off: pl.BlockSpec((FFN_K, tc), lambda j, i: (0, j + off))
    nxt = lambda i: jnp.minimum((i + 1) * per, last)
    return pl.pallas_call(
        body, name=name, grid=(nb, n),
        in_specs=[blk(0), _prev_halo_spec(tr, tc, lambda j, i: j),
                  blk(nb), _prev_halo_spec(tr, tc, lambda j, i: j + nb),
                  pl.BlockSpec((2, tr, tc), lambda j, i: (0, i, j)),
                  pl.BlockSpec((2, HALO, tc), lambda j, i: (0, nxt(i), j)),
                  pl.BlockSpec((tr, tc), lambda j, i: (i, j)),
                  pl.BlockSpec((HALO, tc), lambda j, i: (nxt(i), j)),
                  wsp(0), wsp(nb)],
        out_specs=[pl.BlockSpec((2, tr, tc), lambda j, i: (0, i, j)),
                   pl.BlockSpec((2, FFN_K, tc), lambda j, i: (0, 0, j)),
                   pl.BlockSpec((2, 1, tc), lambda j, i: (0, 0, j))],
        out_shape=[jax.ShapeDtypeStruct((2, L, D_FF), BF16), jax.ShapeDtypeStruct((2, FFN_K, D_FF), F32),
                   jax.ShapeDtypeStruct((2, 1, D_FF), F32)],
        scratch_shapes=[pltpu.VMEM((HALO + tr, tc), F32), pltpu.VMEM((HALO + tr, tc), F32),
                        pltpu.VMEM((tr + HALO, tc), F32), pltpu.VMEM((tr + HALO, tc), F32)],
        compiler_params=_cp("parallel", "arbitrary"),
    )(u, u, u, u, c, c, dact, dact, w, w)


def _ln_stats(c):
    mu = jnp.mean(c, axis=-1, keepdims=True)
    cc = c - mu
    rstd = lax.rsqrt(jnp.mean(cc * cc, axis=-1, keepdims=True) + EPS)
    return cc * rstd, rstd


def cf_fwd(u, dw_w, dw_b, ln_g, ln_b, name):
    L = u.shape[0]
    D = D_MODEL
    tr = min(TR, L)

    def body(a_ref, ha_ref, g_ref, hg_ref, w_ref, b_ref, lg_ref, lb_ref, c_ref, s_ref, scr):
        i = pl.program_id(0)
        glu_h = ha_ref[...].astype(F32) * _sig(hg_ref[...].astype(F32))
        glu = a_ref[...].astype(F32) * _sig(g_ref[...].astype(F32))
        _fill_prev(scr, glu_h, glu, i, tr)
        c = _conv(scr, w_ref, CF_K, tr) + b_ref[...]
        c_ref[...] = c
        xhat, _ = _ln_stats(c)
        ln = xhat * lg_ref[...] + lb_ref[...]
        s_ref[...] = (ln * _sig(ln)).astype(BF16)

    per = tr // HALO
    halo = lambda col: pl.BlockSpec((HALO, D), lambda i: (jnp.maximum(i * per - 1, 0), col))
    return pl.pallas_call(
        body, name=name, grid=(L // tr,),
        in_specs=[pl.BlockSpec((tr, D), lambda i: (i, 0)), halo(0),
                  pl.BlockSpec((tr, D), lambda i: (i, 1)), halo(1),
                  pl.BlockSpec((CF_K, D), lambda i: (0, 0)), _vec_spec(D), _vec_spec(D), _vec_spec(D)],
        out_specs=[_row_spec(tr, D), _row_spec(tr, D)],
        out_shape=[jax.ShapeDtypeStruct((L, D), F32), jax.ShapeDtypeStruct((L, D), BF16)],
        scratch_shapes=[pltpu.VMEM((HALO + tr, D), F32)],
        compiler_params=_cp("parallel"),
    )(u, u, u, u, dw_w, dw_b, ln_g, ln_b)


def cf_bwd_ln(c, ds, ln_g, ln_b, name):
    L, D = c.shape
    tr = min(TR, L)

    def body(c_ref, ds_ref, lg_ref, lb_ref, dc_ref, dg_ref, db_ref):
        xhat, rstd = _ln_stats(c_ref[...])
        ln = xhat * lg_ref[...] + lb_ref[...]
        sg = _sig(ln)
        dln = ds_ref[...].astype(F32) * sg * (1.0 + ln * (1.0 - sg))

        @pl.when(pl.program_id(0) == 0)
        def _():
            dg_ref[...] = jnp.zeros_like(dg_ref)
            db_ref[...] = jnp.zeros_like(db_ref)

        dg_ref[...] += jnp.sum(dln * xhat, axis=0, keepdims=True)
        db_ref[...] += jnp.sum(dln, axis=0, keepdims=True)
        dxh = dln * lg_ref[...]
        dc_ref[...] = rstd * (dxh - jnp.mean(dxh, axis=-1, keepdims=True)
                              - xhat * jnp.mean(dxh * xhat, axis=-1, keepdims=True))

    return pl.pallas_call(
        body, name=name, grid=(L // tr,),
        in_specs=[_row_spec(tr, D), _row_spec(tr, D), _vec_spec(D), _vec_spec(D)],
        out_specs=[_row_spec(tr, D), _vec_spec(D), _vec_spec(D)],
        out_shape=[jax.ShapeDtypeStruct((L, D), F32), jax.ShapeDtypeStruct((1, D), F32),
                   jax.ShapeDtypeStruct((1, D), F32)],
        compiler_params=_cp("arbitrary"),
    )(c, ds, ln_g, ln_b)


def cf_bwd_conv(u, dc, dw_w, name):
    L = u.shape[0]
    D = D_MODEL
    tr = min(TR, L)
    n = L // tr

    def body(a_ref, ha_ref, g_ref, hg_ref, dc_ref, nx_ref, w_ref, du_ref, dw_ref, db_ref, sx, sd):
        i = pl.program_id(0)
        a = a_ref[...].astype(F32)
        sg = _sig(g_ref[...].astype(F32))
        _fill_prev(sx, ha_ref[...].astype(F32) * _sig(hg_ref[...].astype(F32)), a * sg, i, tr)
        dcv = dc_ref[...]
        sd[pl.ds(0, tr), :] = dcv
        sd[pl.ds(tr, HALO), :] = jnp.where(i == n - 1, 0.0, nx_ref[...])
        dglu = None
        for k in range(CF_K):
            term = sd[pl.ds(CF_K - 1 - k, tr), :] * w_ref[k:k + 1, :]
            dglu = term if dglu is None else dglu + term
        du_ref[:, 0:D] = (dglu * sg).astype(BF16)
        du_ref[:, D:2 * D] = (dglu * a * sg * (1.0 - sg)).astype(BF16)

        @pl.when(i == 0)
        def _():
            dw_ref[...] = jnp.zeros_like(dw_ref)
            db_ref[...] = jnp.zeros_like(db_ref)

        _conv_dw(sx, dcv, dw_ref, CF_K, tr)
        db_ref[...] += jnp.sum(dcv, axis=0, keepdims=True)

    per = tr // HALO
    last = L // HALO - 1
    halo = lambda col: pl.BlockSpec((HALO, D), lambda i: (jnp.maximum(i * per - 1, 0), col))
    return pl.pallas_call(
        body, name=name, grid=(n,),
        in_specs=[pl.BlockSpec((tr, D), lambda i: (i, 0)), halo(0),
                  pl.BlockSpec((tr, D), lambda i: (i, 1)), halo(1),
                  _row_spec(tr, D),
                  pl.BlockSpec((HALO, D), lambda i: (jnp.minimum((i + 1) * per, last), 0)),
                  pl.BlockSpec((CF_K, D), lambda i: (0, 0))],
        out_specs=[pl.BlockSpec((tr, 2 * D), lambda i: (i, 0)),
                   pl.BlockSpec((CF_K, D), lambda i: (0, 0)), _vec_spec(D)],
        out_shape=[jax.ShapeDtypeStruct((L, 2 * D), BF16), jax.ShapeDtypeStruct((CF_K, D), F32),
                   jax.ShapeDtypeStruct((1, D), F32)],
        scratch_shapes=[pltpu.VMEM((HALO + tr, D), F32), pltpu.VMEM((tr + HALO, D), F32)],
        compiler_params=_cp("arbitrary"),
    )(u, u, u, u, dc, dc, dw_w)


XA_TR = 512
XA_SCALE = XA_HD ** -0.5


def _xa_probs(qh, kh):
    s = _dot(qh, kh, 1, 1) * XA_SCALE
    p = jnp.exp(s - jnp.max(s, axis=-1, keepdims=True))
    return p / jnp.sum(p, axis=-1, keepdims=True)


def attn_fwd(q, kv, name):
    L, D = q.shape
    tr = min(XA_TR, L)

    def body(q_ref, kv_ref, o_ref):
        for hd in range(XA_HEADS):
            c = slice(hd * XA_HD, (hd + 1) * XA_HD)
            p = _xa_probs(q_ref[:, c], kv_ref[:, c])
            vh = kv_ref[:, D + hd * XA_HD:D + (hd + 1) * XA_HD]
            o_ref[:, c] = _dot(p.astype(BF16), vh).astype(BF16)

    return pl.pallas_call(
        body, name=name, grid=(L // tr,),
        in_specs=[_row_spec(tr, D), pl.BlockSpec((N_MEM, 2 * D), lambda i: (0, 0))],
        out_specs=_row_spec(tr, D), out_shape=jax.ShapeDtypeStruct((L, D), BF16),
        compiler_params=_cp("parallel"),
    )(q, kv)


def attn_bwd(q, kv, do, name):
    L, D = q.shape
    tr = min(XA_TR, L)

    def body(q_ref, kv_ref, do_ref, dq_ref, dkv_ref):
        @pl.when(pl.program_id(0) == 0)
        def _():
            dkv_ref[...] = jnp.zeros_like(dkv_ref)

        for hd in range(XA_HEADS):
            c = slice(hd * XA_HD, (hd + 1) * XA_HD)
            cv = slice(D + hd * XA_HD, D + (hd + 1) * XA_HD)
            qh, kh, vh, doh = q_ref[:, c], kv_ref[:, c], kv_ref[:, cv], do_ref[:, c]
            p = _xa_probs(qh, kh)
            dp = _dot(doh, vh, 1, 1)
            dkv_ref[:, cv] += _dot(p.astype(BF16), doh, 0, 0)
            ds = (p * (dp - jnp.sum(dp * p, axis=-1, keepdims=True)) * XA_SCALE).astype(BF16)
            dq_ref[:, c] = _dot(ds, kh).astype(BF16)
            dkv_ref[:, c] += _dot(ds, qh, 0, 0)

    return pl.pallas_call(
        body, name=name, grid=(L // tr,),
        in_specs=[_row_spec(tr, D), pl.BlockSpec((N_MEM, 2 * D), lambda i: (0, 0)), _row_spec(tr, D)],
        out_specs=[_row_spec(tr, D), pl.BlockSpec((N_MEM, 2 * D), lambda i: (0, 0))],
        out_shape=[jax.ShapeDtypeStruct((L, D), BF16), jax.ShapeDtypeStruct((N_MEM, 2 * D), F32)],
        compiler_params=_cp("arbitrary"),
    )(q, kv, do)


N_PAIRS = N_HEADS // 2
PAIRS_PER_GROUP = N_PAIRS // N_GROUPS
GN = N_GROUPS * D_STATE


def _softplus(x):
    t = jnp.exp(-jnp.abs(x))
    return jnp.maximum(x, 0.0) + jnp.where(t < 1e-4, t * (1.0 - 0.5 * t), jnp.log(1.0 + t))


def _dot3b(m, v, ca=1, cb=0):
    v0, v1, v2 = _split3(v)
    return _dot(m, v0, ca, cb) + _dot(m, v1, ca, cb) + _dot(m, v2, ca, cb)


def ssd_consts():
    h = lax.broadcasted_iota(jnp.int32, (LANE, D_INNER), 0)
    c = lax.broadcasted_iota(jnp.int32, (LANE, D_INNER), 1)
    expand = (c // HEAD_DIM == h).astype(BF16)
    r = lax.broadcasted_iota(jnp.int32, (CHUNK, CHUNK), 0)
    k = lax.broadcasted_iota(jnp.int32, (CHUNK, CHUNK), 1)
    tri = (k <= r).astype(BF16)
    return expand, tri


def _ssd_common(dtr_ref, prm_ref, e_ref, tri_ref):
    lane = lax.broadcasted_iota(jnp.int32, (CHUNK, LANE), 1)
    valid = lane < N_HEADS
    A = -jnp.exp(prm_ref[1:2, :])
    pre = dtr_ref[...] + prm_ref[0:1, :]
    dt = jnp.where(valid, _softplus(pre), 0.0)
    cs = _dot3b(tri_ref[...], dt * A)
    E = e_ref[...]
    dt_x = _dot3(dt, E)
    cs_x = _dot3(cs, E)
    csl_x = cs_x[CHUNK - 1:CHUNK, :]
    return dict(valid=valid, A=A, pre=pre, dt=dt, cs=cs, csT=cs.T, dt_x=dt_x, ecs_x=jnp.exp(cs_x),
                dend_x=jnp.exp(csl_x - cs_x), cd_x=jnp.exp(csl_x), D_x=_dot3(prm_ref[...], E)[2:3, :])


def ssd_fwd(xs, bc, dtr, zx, prm, ng, name):
    L = xs.shape[0]
    nc = L // CHUNK
    expand, tri = ssd_consts()

    def body(xs_ref, bc_ref, dtr_ref, z_ref, prm_ref, ng_ref, e_ref, tri_ref, y_ref, yn_ref, st_ref, state):
        @pl.when(pl.program_id(0) == 0)
        def _():
            state[...] = jnp.zeros_like(state)

        q = _ssd_common(dtr_ref, prm_ref, e_ref, tri_ref)
        cs, csT = q["cs"], q["csT"]
        xs_v = xs_ref[...]
        X = xs_v * q["dt_x"]
        Xb = X.astype(BF16)
        Xd = (X * q["dend_x"]).astype(BF16)
        ii = lax.broadcasted_iota(jnp.int32, (CHUNK, CHUNK), 0)
        jj = lax.broadcasted_iota(jnp.int32, (CHUNK, CHUNK), 1)
        tril = jj <= ii
        first = jj < HEAD_DIM
        for g in range(N_GROUPS):
            Bg = bc_ref[:, g * D_STATE:(g + 1) * D_STATE]
            Cg = bc_ref[:, GN + g * D_STATE:GN + (g + 1) * D_STATE]
            S = _dot(Cg, Bg, 1, 1)
            for pr in range(PAIRS_PER_GROUP):
                pair = g * PAIRS_PER_GROUP + pr
                cols = slice(pair * LANE, (pair + 1) * LANE)
                Xp = Xb[:, cols]
                ys = []
                for h in (2 * pair, 2 * pair + 1):
                    seg = cs[:, h:h + 1] - csT[h:h + 1, :]
                    M = (S * jnp.exp(jnp.where(tril, seg, NEG))).astype(BF16)
                    ys.append(_dot(M, Xp))
                prevT = state[pair]
                st_ref[0, pair] = prevT
                yoff = _dot(Cg, prevT.astype(BF16)) * q["ecs_x"][:, cols]
                y_ref[:, cols] = jnp.where(first, ys[0], ys[1]) + yoff + xs_v[:, cols] * q["D_x"][:, cols]
                state[pair] = prevT * q["cd_x"][:, cols] + _dot(Bg, Xd[:, cols], 0, 0)
        z = z_ref[...].astype(F32)
        gt = y_ref[...] * z * _sig(z)
        yn_ref[...] = (gt * _rms(gt) * ng_ref[...]).astype(BF16)

    row = lambda w: pl.BlockSpec((CHUNK, w), lambda c: (c, 0))
    const = lambda a: pl.BlockSpec(a.shape, lambda c: (0,) * a.ndim)
    return pl.pallas_call(
        body, name=name, grid=(nc,),
        in_specs=[row(D_INNER), row(2 * GN), row(LANE), row(D_INNER), const(prm), const(ng), const(expand), const(tri)],
        out_specs=[row(D_INNER), row(D_INNER), pl.BlockSpec((1, N_PAIRS, D_STATE, LANE), lambda c: (c, 0, 0, 0))],
        out_shape=[jax.ShapeDtypeStruct((L, D_INNER), F32), jax.ShapeDtypeStruct((L, D_INNER), BF16),
                   jax.ShapeDtypeStruct((nc, N_PAIRS, D_STATE, LANE), F32)],
        scratch_shapes=[pltpu.VMEM((N_PAIRS, D_STATE, LANE), F32)],
        compiler_params=_cp("arbitrary"),
    )(xs, bc, dtr, zx, prm, ng, expand, tri)


def ssd_bwd(dyn, y, zx, xs, bc, dtr, st, prm, ng, name):
    L = xs.shape[0]
    nc = L // CHUNK
    expand, tri = ssd_consts()

    def body(dyn_ref, y_ref, z_ref, xs_ref, bc_ref, dtr_ref, st_ref, prm_ref, ng_ref, e_ref, tri_ref,
             dxbc_ref, dz_ref, ddtr_ref, dng_ref, dprm_ref, dstate, g_cs, g_q, dX, g_row):
        step = pl.program_id(0)

        @pl.when(step == 0)
        def _():
            dstate[...] = jnp.zeros_like(dstate)
            dng_ref[...] = jnp.zeros_like(dng_ref)
            dprm_ref[...] = jnp.zeros_like(dprm_ref)
            g_row[...] = jnp.zeros_like(g_row)

        q = _ssd_common(dtr_ref, prm_ref, e_ref, tri_ref)
        cs, csT, E = q["cs"], q["csT"], e_ref[...]
        xs_v = xs_ref[...]
        X = xs_v * q["dt_x"]
        Xb = X.astype(BF16)
        Xd_f = X * q["dend_x"]
        Xd = Xd_f.astype(BF16)

        yv = y_ref[...]
        z = z_ref[...].astype(F32)
        sz = _sig(z)
        silu = z * sz
        gt = yv * silu
        r = _rms(gt)
        gn = gt * r
        dyn_v = dyn_ref[...]
        dng_ref[...] += jnp.sum(dyn_v * gn, axis=0, keepdims=True)
        dgn = dyn_v * ng_ref[...]
        dgt = r * (dgn - gn * jnp.mean(dgn * gn, axis=-1, keepdims=True))
        dY = dgt * silu
        dz_ref[...] = (dgt * yv * sz * (1.0 + z * (1.0 - sz))).astype(BF16)
        dYb = dY.astype(BF16)
        g_row[1:2, :] += jnp.sum(dY * xs_v, axis=0, keepdims=True)

        ii = lax.broadcasted_iota(jnp.int32, (CHUNK, CHUNK), 0)
        jj = lax.broadcasted_iota(jnp.int32, (CHUNK, CHUNK), 1)
        tril = jj <= ii
        triu = jj >= ii
        first = jj < HEAD_DIM
        lane_row = lax.broadcasted_iota(jnp.int32, (1, LANE), 1)
        sub_col = lax.broadcasted_iota(jnp.int32, (CHUNK, 1), 0)
        dcs_col = jnp.zeros((CHUNK, LANE), F32)
        dcs_rowT = jnp.zeros((LANE, CHUNK), F32)
        for g in range(N_GROUPS):
            Bg = bc_ref[:, g * D_STATE:(g + 1) * D_STATE]
            Cg = bc_ref[:, GN + g * D_STATE:GN + (g + 1) * D_STATE]
            S = _dot(Cg, Bg, 1, 1)
            ST = _dot(Bg, Cg, 1, 1)
            dS = jnp.zeros((CHUNK, CHUNK), F32)
            dCg = jnp.zeros((CHUNK, D_STATE), F32)
            dBg = jnp.zeros((CHUNK, D_STATE), F32)
            for pr in range(PAIRS_PER_GROUP):
                pair = g * PAIRS_PER_GROUP + pr
                cols = slice(pair * LANE, (pair + 1) * LANE)
                Xp = Xb[:, cols]
                dYp_f = dY[:, cols]
                dYp = dYb[:, cols]
                prevT = st_ref[0, pair]
                prevTb = prevT.astype(BF16)
                dst = dstate[pair]
                dstb = dst.astype(BF16)
                ecs_p = q["ecs_x"][:, cols]
                g_cs[:, cols] = dYp_f * (_dot(Cg, prevTb) * ecs_p)
                dWb = (dYp_f * ecs_p).astype(BF16)
                dprev = dst * q["cd_x"][:, cols] + _dot(Cg, dWb, 0, 0)
                dCg = dCg + _dot(dWb, prevTb, 1, 1)
                g_row[0:1, cols] = jnp.sum(dst * prevT, axis=0, keepdims=True)
                dXp = None
                for hh, h in enumerate((2 * pair, 2 * pair + 1)):
                    mine = first if hh == 0 else jnp.logical_not(first)
                    seg = cs[:, h:h + 1] - csT[h:h + 1, :]
                    lam = jnp.exp(jnp.where(tril, seg, NEG))
                    dM = _dot(jnp.where(mine, dYp, jnp.zeros_like(dYp)), Xp, 1, 1)
                    dS = dS + dM * lam
                    Gm = dM * (S * lam)
                    dcs_col = dcs_col + jnp.sum(Gm, axis=1, keepdims=True) * (lane_row == h).astype(F32)
                    dcs_rowT = dcs_rowT + (sub_col == h).astype(F32) * jnp.sum(Gm, axis=0, keepdims=True)
                    MT = (ST * jnp.exp(jnp.where(triu, -seg, NEG))).astype(BF16)
                    t = _dot(MT, dYp)
                    dXp = t if dXp is None else jnp.where(first, dXp, t)
                dXd = _dot(Bg, dstb)
                dBg = dBg + _dot(Xd[:, cols], dstb, 1, 1)
                g_q[:, cols] = dXd * Xd_f[:, cols]
                dX[:, cols] = dXp + dXd * q["dend_x"][:, cols]
                dstate[pair] = dprev
            dSb = dS.astype(BF16)
            dxbc_ref[:, D_INNER + g * D_STATE:D_INNER + (g + 1) * D_STATE] = dBg + _dot(dSb, Cg, 0, 0)
            dxbc_ref[:, D_INNER + GN + g * D_STATE:D_INNER + GN + (g + 1) * D_STATE] = dCg + _dot(dSb, Bg)
        dXv = dX[...]
        dxbc_ref[:, 0:D_INNER] = q["D_x"] * dY + dXv * q["dt_x"]
        r_dt = _dot3(dXv * xs_v, E, 1, 1)
        r_cs = _dot3(g_cs[...], E, 1, 1)
        r_q = _dot3(g_q[...], E, 1, 1)
        r_row = _dot3(g_row[...], E, 1, 1)
        cd = jnp.exp(cs[CHUNK - 1:CHUNK, :])
        dcs_last = jnp.sum(r_q, axis=0, keepdims=True) + r_row[0:1, :] * cd
        dcs = r_cs - r_q + dcs_col - dcs_rowT.T + jnp.where(sub_col == CHUNK - 1, dcs_last, 0.0)
        da = _dot3b(tri_ref[...], dcs, 0, 0)
        dpre = jnp.where(q["valid"], (r_dt + da * q["A"]) * _sig(q["pre"]), 0.0)
        ddtr_ref[...] = dpre
        dprm_ref[0:1, :] += jnp.sum(dpre, axis=0, keepdims=True)
        dprm_ref[1:2, :] += jnp.sum(da * q["dt"], axis=0, keepdims=True) * q["A"]
        dprm_ref[2:3, :] = r_row[1:2, :]

    rev = lambda w: pl.BlockSpec((CHUNK, w), lambda c: (nc - 1 - c, 0))
    const = lambda a: pl.BlockSpec(a.shape, lambda c: (0,) * a.ndim)
    return pl.pallas_call(
        body, name=name, grid=(nc,),
        in_specs=[rev(D_INNER), rev(D_INNER), rev(D_INNER), rev(D_INNER), rev(2 * GN), rev(LANE),
                  pl.BlockSpec((1, N_PAIRS, D_STATE, LANE), lambda c: (nc - 1 - c, 0, 0, 0)),
                  const(prm), const(ng), const(expand), const(tri)],
        out_specs=[rev(CONV_DIM), rev(D_INNER), rev(LANE),
                   pl.BlockSpec((1, D_INNER), lambda c: (0, 0)), pl.BlockSpec((8, LANE), lambda c: (0, 0))],
        out_shape=[jax.ShapeDtypeStruct((L, CONV_DIM), F32), jax.ShapeDtypeStruct((L, D_INNER), BF16),
                   jax.ShapeDtypeStruct((L, LANE), F32), jax.ShapeDtypeStruct((1, D_INNER), F32),
                   jax.ShapeDtypeStruct((8, LANE), F32)],
        scratch_shapes=[pltpu.VMEM((N_PAIRS, D_STATE, LANE), F32), pltpu.VMEM((CHUNK, D_INNER), F32),
                        pltpu.VMEM((CHUNK, D_INNER), F32), pltpu.VMEM((CHUNK, D_INNER), F32),
                        pltpu.VMEM((8, D_INNER), F32)],
        compiler_params=_cp("arbitrary"),
    )(dyn, y, zx, xs, bc, dtr, st, prm, ng, expand, tri)


def _ssd_weights(W, j):
    w_in = W["ssm_in_w"][j]
    nzx = D_INNER + CONV_DIM
    wdt = jnp.pad(w_in[:, nzx:], ((0, 0), (0, LANE - N_HEADS)))
    prm = jnp.zeros((8, LANE), F32)
    prm = prm.at[0, :N_HEADS].set(W["ssm_dt_bias"][j]).at[1, :N_HEADS].set(W["ssm_A_log"][j])
    prm = prm.at[2, :N_HEADS].set(W["ssm_D"][j])
    return dict(wdt=wdt, cw=W["ssm_conv_w"][j], cb=W["ssm_conv_b"][j].reshape(1, CONV_DIM), prm=prm,
                ng=W["ssm_norm_g"][j].reshape(1, D_INNER))


def ssd_layer_fwd(h, W, j, tag):
    p = _ssd_weights(W, j)
    zx = mm(h, W["ssm_in_w"], layer=j, b_n=D_INNER + CONV_DIM, out_dtype=BF16, name=f"{tag}_zx")
    dtr = mm(h, p["wdt"], name=f"{tag}_dt")
    xs, pre_x = ssm_conv_fwd(zx, p["cw"], p["cb"], col0=D_INNER, ncols=D_INNER, wcol0=0, out_dtype=F32,
                             name=f"{tag}_convx")
    bc, pre_bc = ssm_conv_fwd(zx, p["cw"], p["cb"], col0=2 * D_INNER, ncols=2 * GN, wcol0=D_INNER, out_dtype=BF16,
                              name=f"{tag}_convbc")
    y, yn, st = ssd_fwd(xs, bc, dtr, zx, p["prm"], p["ng"], name=f"{tag}_scan")
    f = mm(yn, W["ssm_out_w"], layer=j, name=f"{tag}_out")
    return f, dict(h=h, zx=zx, dtr=dtr, xs=xs, bc=bc, pre_x=pre_x, pre_bc=pre_bc, y=y, yn=yn, st=st, p=p)


def ssd_layer_bwd(df, ctx, W, GB, j, tag):
    p = ctx["p"]
    h = ctx["h"]
    dyn = mm(df, W["ssm_out_w"], layer=j, tb=True, name=f"{tag}_b_dyn")
    GB["ssm_out_w"] = mm(ctx["yn"], df, ta=True, into=(GB["ssm_out_w"], j, 0), name=f"{tag}_b_gwo")
    dxbc, dz, ddtr, dng, dprm = ssd_bwd(dyn, ctx["y"], ctx["zx"], ctx["xs"], ctx["bc"], ctx["dtr"], ctx["st"],
                                        p["prm"], p["ng"], name=f"{tag}_b_scan")
    dx1, dcw1, dcb1 = ssm_conv_bwd(ctx["zx"], ctx["pre_x"], dxbc, p["cw"], col0=D_INNER, dcol0=0, ncols=D_INNER,
                                   name=f"{tag}_b_convx")
    dx2, dcw2, dcb2 = ssm_conv_bwd(ctx["zx"], ctx["pre_bc"], dxbc, p["cw"], col0=2 * D_INNER, dcol0=D_INNER,
                                   ncols=2 * GN, name=f"{tag}_b_convbc")
    dh = mm(dz, W["ssm_in_w"], layer=j, tb=True, b_k0=0, name=f"{tag}_b_dh1")
    dh = mm(dx1, W["ssm_in_w"], layer=j, tb=True, b_k0=D_INNER, acc=dh, name=f"{tag}_b_dh2")
    dh = mm(dx2, W["ssm_in_w"], layer=j, tb=True, b_k0=2 * D_INNER, acc=dh, name=f"{tag}_b_dh3")
    dh = mm(ddtr, p["wdt"], tb=True, acc=dh, name=f"{tag}_b_dh4")
    g_in = jnp.concatenate([mm(h, dz, ta=True, out_dtype=BF16, name=f"{tag}_b_gz"),
                            mm(h, dx1, ta=True, out_dtype=BF16, name=f"{tag}_b_gx"),
                            mm(h, dx2, ta=True, out_dtype=BF16, name=f"{tag}_b_gbc"),
                            mm(h, ddtr, ta=True, out_dtype=BF16, name=f"{tag}_b_gdt")[:, :N_HEADS]], axis=1)
    return dh, dict(ssm_in_w=g_in, ssm_conv_w=jnp.concatenate([dcw1, dcw2], axis=1),
                    ssm_conv_b=jnp.concatenate([dcb1, dcb2], axis=1)[0], ssm_dt_bias=dprm[0, :N_HEADS],
                    ssm_A_log=dprm[1, :N_HEADS], ssm_D=dprm[2, :N_HEADS], ssm_norm_g=dng[0])


def cf_layer_fwd(h, W, j, tag):
    u = mm(h, W["cf_pw1_w"], layer=j, bias=W["cf_pw1_b"][j], out_dtype=BF16, name=f"{tag}_pw1")
    c, s = cf_fwd(u, W["cf_dw_w"][j], W["cf_dw_b"][j].reshape(1, -1), W["cf_ln_g"][j].reshape(1, -1),
                  W["cf_ln_b"][j].reshape(1, -1), name=f"{tag}_conv")
    f = mm(s, W["cf_pw2_w"], layer=j, bias=W["cf_pw2_b"][j], name=f"{tag}_pw2")
    return f, dict(h=h, u=u, c=c, s=s)


def cf_layer_bwd(df, ctx, W, GB, j, tag):
    h = ctx["h"]
    ds = mm(df, W["cf_pw2_w"], layer=j, tb=True, name=f"{tag}_b_ds")
    GB["cf_pw2_w"] = mm(ctx["s"], df, ta=True, into=(GB["cf_pw2_w"], j, 0), name=f"{tag}_b_gpw2")
    g_b2 = colsum(df, name=f"{tag}_b_gb2")
    dc, dlg, dlb = cf_bwd_ln(ctx["c"], ds, W["cf_ln_g"][j].reshape(1, -1), W["cf_ln_b"][j].reshape(1, -1),
                             name=f"{tag}_b_ln")
    du, ddw, ddb = cf_bwd_conv(ctx["u"], dc, W["cf_dw_w"][j], name=f"{tag}_b_conv")
    dh = mm(du, W["cf_pw1_w"], layer=j, tb=True, name=f"{tag}_b_dh")
    GB["cf_pw1_w"] = mm(h, du, ta=True, into=(GB["cf_pw1_w"], j, 0), name=f"{tag}_b_gpw1")
    g_b1 = colsum(du, name=f"{tag}_b_gb1")
    return dh, dict(cf_pw1_b=g_b1[0], cf_dw_w=ddw, cf_dw_b=ddb[0], cf_ln_g=dlg[0], cf_ln_b=dlb[0], cf_pw2_b=g_b2[0])


def xa_layer_fwd(h, mem, W, i, tag):
    m = norm_fwd(mem, W["xa_mem_g"][i], name=f"{tag}_memnorm")
    kv = mm(m, W["xa_kv_w"], layer=i, out_dtype=BF16, name=f"{tag}_kv")
    q = mm(h, W["xa_q_w"], layer=i, out_dtype=BF16, name=f"{tag}_q")
    o = attn_fwd(q, kv, name=f"{tag}_attn")
    f = mm(o, W["xa_o_w"], layer=i, name=f"{tag}_o")
    return f, dict(h=h, m=m, kv=kv, q=q, o=o)


def xa_layer_bwd(df, ctx, mem, W, GB, i, tag):
    h = ctx["h"]
    do = mm(df, W["xa_o_w"], layer=i, tb=True, out_dtype=BF16, name=f"{tag}_b_do")
    GB["xa_o_w"] = mm(ctx["o"], df, ta=True, into=(GB["xa_o_w"], i, 0), name=f"{tag}_b_go")
    dq, dkv = attn_bwd(ctx["q"], ctx["kv"], do, name=f"{tag}_b_attn")
    dh = mm(dq, W["xa_q_w"], layer=i, tb=True, name=f"{tag}_b_dh")
    GB["xa_q_w"] = mm(h, dq, ta=True, into=(GB["xa_q_w"], i, 0), name=f"{tag}_b_gq")
    GB["xa_kv_w"] = mm(ctx["m"], dkv, ta=True, into=(GB["xa_kv_w"], i, 0), name=f"{tag}_b_gkv")
    dm = mm(dkv, W["xa_kv_w"], layer=i, tb=True, name=f"{tag}_b_dm")
    g_mg = norm_dg(mem, dm, name=f"{tag}_b_gmem")
    return dh, dict(xa_mem_g=g_mg[0])


def ffn_layer_fwd(h, W, i, tag):
    cw, cb = W["ffn_conv_w"][i], W["ffn_conv_b"][i].reshape(1, -1)
    u = mm(h, W["ffn_in_w"], layer=i, out_dtype=BF16, name=f"{tag}_in")
    act, c = ffn_act_fwd(u, cw, cb, name=f"{tag}_act")
    f = mm(act, W["ffn_out_w"], layer=i, name=f"{tag}_out")
    return f, dict(h=h, u=u, c=c, act=act)


def ffn_layer_bwd(df, ctx, W, GB, i, tag):
    h = ctx["h"]
    dact = mm(df, W["ffn_out_w"], layer=i, tb=True, out_dtype=BF16, name=f"{tag}_b_dact")
    GB["ffn_out_w"] = mm(ctx["act"], df, ta=True, into=(GB["ffn_out_w"], i, 0), name=f"{tag}_b_gout")
    du, dcw, dcb = ffn_act_bwd(ctx["u"], ctx["c"], dact, W["ffn_conv_w"][i], name=f"{tag}_b_act")
    dh = None
    for half in range(2):
        dh = mm(du, W["ffn_in_w"], a_idx=half, layer=i, tb=True, b_k0=half * D_FF, acc=dh, name=f"{tag}_b_dh{half}")
        GB["ffn_in_w"] = mm(h, du, ta=True, layer=half, into=(GB["ffn_in_w"], i, half * D_FF), name=f"{tag}_b_gin{half}")
    cat = lambda a: jnp.concatenate([a[0], a[1]], axis=-1)
    return dh, dict(ffn_conv_w=cat(dcw), ffn_conv_b=cat(dcb)[0])


def _sublayer_weights(i, s):
    if s == 0:
        return [("ssm_in_w", i // 2), ("ssm_out_w", i // 2)] if i % 2 == 0 else [("cf_pw1_w", i // 2), ("cf_pw2_w", i // 2)]
    return [(n, i) for n in (("xa_q_w", "xa_kv_w", "xa_o_w") if s == 1 else ("ffn_in_w", "ffn_out_w"))]


def local_step(x, mem, target, W, fetch=None, layer_done=None):
    subs = [(i, s) for i in range(DEPTH) for s in range(3)]
    ng = W["norm_g"]

    def fwd(i, s, h):
        tag = f"l{i}s{s}"
        if s == 0:
            return ssd_layer_fwd(h, W, i // 2, tag) if i % 2 == 0 else cf_layer_fwd(h, W, i // 2, tag)
        if s == 1:
            return xa_layer_fwd(h, mem, W, i, tag)
        return ffn_layer_fwd(h, W, i, tag)

    GB = {}

    def bwd(i, s, df, ctx):
        tag = f"l{i}s{s}"
        if s == 0:
            return (ssd_layer_bwd if i % 2 == 0 else cf_layer_bwd)(df, ctx, W, GB, i // 2, tag)
        if s == 1:
            return xa_layer_bwd(df, ctx, mem, W, GB, i, tag)
        return ffn_layer_bwd(df, ctx, W, GB, i, tag)

    h = norm_fwd(x, ng[0, 0], name="norm0")
    saved = []
    dxp = loss = None
    for k, (i, s) in enumerate(subs):
        if fetch is not None:
            fetch(i, s, x)
        f, ctx = fwd(i, s, h)
        saved.append((x, f, ctx))
        if k + 1 < len(subs):
            ni, ns = subs[k + 1]
            x, h = bnd_fwd(x, f, ng[i, 2 * s + 1], ng[ni, 2 * ns], name=f"bnd{k}")
        else:
            dxp, loss = final_fwd(x, f, ng[i, 2 * s + 1], target, name="final")

    for n in BIG:
        if n != "ssm_in_w":
            GB[n] = jnp.zeros((len(W[n]), *W[n][0].shape), BF16)
    grads = {}

    def put(name, idx, val):
        grads.setdefault(name, {})[idx] = val

    i, s = subs[-1]
    top = bnd_bwd(dxp, post=(saved[-1][1], ng[i, 2 * s + 1]), name="bbnd_top")
    put("norm_g", (i, 2 * s + 1), top["dgpost"][0])
    df = top["df"]
    for k in range(len(subs) - 1, -1, -1):
        i, s = subs[k]
        xk, _, ctx = saved[k]
        dh, gw = bwd(i, s, df, ctx)
        for name, val in gw.items():
            put(name, i // 2 if name.startswith(("ssm_", "cf_")) else i, val)
        dep = None
        if layer_done is not None:
            dep = layer_done(i, s, GB, grads["ssm_in_w"].pop(i // 2) if (s == 0 and i % 2 == 0) else None)
        if k > 0:
            pi, ps = subs[k - 1]
            r = bnd_bwd(dxp, pre=(xk, ng[i, 2 * s], dh), post=(saved[k - 1][1], ng[pi, 2 * ps + 1]), dep=dep,
                        name=f"bbnd{k}")
            put("norm_g", (pi, 2 * ps + 1), r["dgpost"][0])
            df = r["df"]
        else:
            r = bnd_bwd(dxp, pre=(xk, ng[i, 2 * s], dh), dep=dep, name="bbnd0")
        put("norm_g", (i, 2 * s), r["dgpre"][0])
        dxp = r["dx"]

    out = {} if layer_done is not None else dict(GB)
    for name, d in grads.items():
        if name == "norm_g":
            out[name] = jnp.stack([jnp.stack([d[(i, t)] for t in range(6)]) for i in range(DEPTH)])
        elif d:
            out[name] = jnp.stack([d[j] for j in sorted(d)])
    return loss, dxp, out


ANY = pl.BlockSpec(memory_space=pl.ANY)


def _pos():
    return lax.axis_index("x"), lax.axis_index("y"), lax.axis_index("c")


def all_gather(shard, name):
    R, C = shard.shape

    def body(x_ref, out_ref, send_sems, recv_sems, local_sem):
        x, y, c = _pos()
        me, sibling = (x, y, c), (x, y, 1 - c)
        chips = [(1 - x, y), (x, 1 - y), (1 - x, 1 - y)]

        def slot(px, py, pc):
            return out_ref.at[4 * px + 2 * py + pc]

        def copy(k, block, to, src=None):
            return pltpu.make_async_remote_copy(
                src_ref=slot(*block) if src is None else src, dst_ref=slot(*block),
                send_sem=send_sems.at[k], recv_sem=recv_sems.at[k], device_id=to, device_id_type=MESH)

        mine = pltpu.make_async_copy(x_ref, slot(*me), local_sem)
        mine.start()
        first = [copy(0, me, sibling, src=x_ref)]
        first += [copy(1 + j, me, (*chip, c), src=x_ref) for j, chip in enumerate(chips)]
        for cp in first:
            cp.start()
        passed = [copy(4 + j, (*chip, c), sibling) for j, chip in enumerate(chips)]
        for j, chip in enumerate(chips):
            copy(1 + j, (*chip, c), me).wait_recv()
            passed[j].start()
        copy(0, sibling, me).wait_recv()
        for j, chip in enumerate(chips):
            copy(4 + j, (*chip, 1 - c), me).wait_recv()
        for cp in first + passed:
            cp.wait_send()
        mine.wait()

    return pl.pallas_call(
        body, name=name, out_shape=jax.ShapeDtypeStruct((N_DEV, R, C), shard.dtype),
        in_specs=[ANY], out_specs=ANY,
        scratch_shapes=[pltpu.SemaphoreType.DMA((7,)), pltpu.SemaphoreType.DMA((7,)), pltpu.SemaphoreType.DMA(())],
    )(shard)


def _win(ref, kind, k, a, b):
    if kind == "lead":
        return ref.at[k]
    if kind == "row":
        return ref.at[:, pl.ds(pl.multiple_of(k * a, 16), a), :]
    return ref.at[:, :, pl.ds(pl.multiple_of(k * b, LANE), b)]


def _full_shape(shard_shape, kind):
    n, a, b = shard_shape
    return {"lead": (N_DEV, n, a, b), "row": (n, N_DEV * a, b), "col": (n, a, N_DEV * b)}[kind]


HBM = pl.BlockSpec(memory_space=pltpu.HBM)
SEMS = pl.BlockSpec(memory_space=pltpu.SEMAPHORE)
DATAFLOW = pltpu.SideEffectType.DATAFLOW_SIDE_EFFECTING
N_PEER = N_DEV - 1


def _in_hbm(a):
    return pltpu.with_memory_space_constraint(a, pltpu.HBM)


def _peer(x, y, c, r):
    return ((1 - x) if r & 4 else x, (1 - y) if r & 2 else y, (1 - c) if r & 1 else c)


def _win2(ref, kind, k, a, b):
    if kind == "lead":
        return ref.at[k]
    if kind == "row":
        return ref.at[pl.ds(pl.multiple_of(k * a, 16), a), :]
    return ref.at[:, pl.ds(pl.multiple_of(k * b, LANE), b)]


def _zone_shape(kind, a, b):
    return {"lead": (N_DEV, a, b), "row": (N_DEV * a, b), "col": (a, N_DEV * b)}[kind]


def gather_start(shards, items, after, name):
    ns, nz, na = len(shards), len(items), len(after)
    zones = [lax.empty(_zone_shape(kind, a, b), shards[w].dtype) for w, l, kind, a, b in items]

    def body(*refs):
        x_refs = refs[:ns]
        send_sems, recv_sems, local_sems = refs[ns + nz + na:ns + nz + na + 3]
        z_refs = refs[ns + nz + na + 3 + ns:ns + nz + na + 3 + ns + nz]
        token = refs[-1]
        x, y, c = _pos()
        me = 4 * x + 2 * y + c
        for t, (w, l, kind, a, b) in enumerate(items):
            mine = _win2(z_refs[t], kind, me, a, b)
            pltpu.make_async_copy(x_refs[w].at[l], mine, local_sems.at[t]).start()
            for r in range(1, N_DEV):
                pltpu.make_async_remote_copy(
                    src_ref=x_refs[w].at[l], dst_ref=mine,
                    send_sem=send_sems.at[N_PEER * t + r - 1], recv_sem=recv_sems.at[N_PEER * t + r - 1],
                    device_id=_peer(x, y, c, r), device_id_type=MESH).start()
        token[...] = jnp.zeros_like(token)

    n_sem = N_PEER * nz
    outs = pl.pallas_call(
        body, name=name,
        out_shape=(pltpu.SemaphoreType.DMA((n_sem,)), pltpu.SemaphoreType.DMA((n_sem,)), pltpu.SemaphoreType.DMA((nz,)),
                   *[pltpu.HBM(s.shape, s.dtype) for s in shards], *[pltpu.HBM(z.shape, z.dtype) for z in zones],
                   jax.ShapeDtypeStruct((8, LANE), F32)),
        in_specs=[HBM] * (ns + nz) + [pl.BlockSpec(memory_space=pl.ANY)] * na,
        out_specs=(SEMS, SEMS, SEMS, *[HBM] * (ns + nz), pl.BlockSpec(memory_space=pltpu.VMEM)),
        input_output_aliases={i: 3 + i for i in range(ns + nz)},
        compiler_params=pltpu.CompilerParams(has_side_effects=DATAFLOW),
    )(*[_in_hbm(s) for s in shards], *[_in_hbm(z) for z in zones], *after)
    return outs[:3], list(outs[3:3 + ns]), list(outs[3 + ns:3 + ns + nz]), outs[-1]


def gather_wait(zones, idx, items, sems, after, keep, name):
    nz, nk = len(zones), len(keep)

    def body(*refs):
        z_refs = refs[:nz]
        send_sems, recv_sems, local_sems = refs[nz:nz + 3]
        x, y, c = _pos()
        me = 4 * x + 2 * y + c
        for z_ref, t in zip(z_refs, idx):
            w, l, kind, a, b = items[t]
            mine = _win2(z_ref, kind, me, a, b)
            pltpu.make_async_copy(mine, mine, local_sems.at[t]).wait()
            for r in range(1, N_DEV):
                peer = _peer(x, y, c, r)
                cp = pltpu.make_async_remote_copy(
                    src_ref=mine, dst_ref=_win2(z_ref, kind, 4 * peer[0] + 2 * peer[1] + peer[2], a, b),
                    send_sem=send_sems.at[N_PEER * t + r - 1], recv_sem=recv_sems.at[N_PEER * t + r - 1],
                    device_id=peer, device_id_type=MESH)
                cp.wait_send()
                cp.wait_recv()

    outs = pl.pallas_call(
        body, name=name, out_shape=tuple(pltpu.HBM(z.shape, z.dtype) for z in zones),
        in_specs=[HBM] * nz + [SEMS] * 3 + [pl.BlockSpec(memory_space=pl.ANY)] * (1 + nk),
        out_specs=tuple([HBM] * nz), input_output_aliases={i: i for i in range(nz)},
        compiler_params=pltpu.CompilerParams(has_side_effects=DATAFLOW),
    )(*zones, *sems, after, *keep)
    return list(outs)


def gather_now(shards, kinds, name):
    nw = len(shards)
    geo = [s.shape[1:] for s in shards]

    def body(*refs):
        x_refs, o_refs = refs[:nw], refs[nw:2 * nw]
        send_sems, recv_sems, local_sems = refs[2 * nw:]
        x, y, c = _pos()
        me, sibling = (x, y, c), (x, y, 1 - c)
        chips = [(1 - x, y), (x, 1 - y), (1 - x, 1 - y)]

        def slot(w, px, py, pc):
            return _win(o_refs[w], kinds[w], 4 * px + 2 * py + pc, *geo[w])

        def copy(w, k, block, to, src=None):
            return pltpu.make_async_remote_copy(
                src_ref=slot(w, *block) if src is None else src, dst_ref=slot(w, *block),
                send_sem=send_sems.at[7 * w + k], recv_sem=recv_sems.at[7 * w + k], device_id=to, device_id_type=MESH)

        mine = [pltpu.make_async_copy(x_refs[w], slot(w, *me), local_sems.at[w]) for w in range(nw)]
        for cp in mine:
            cp.start()
        first = []
        for w in range(nw):
            first.append(copy(w, 0, me, sibling, src=x_refs[w]))
            first += [copy(w, 1 + j, me, (*chip, c), src=x_refs[w]) for j, chip in enumerate(chips)]
        for cp in first:
            cp.start()
        passed = []
        for w in range(nw):
            for j, chip in enumerate(chips):
                copy(w, 1 + j, (*chip, c), me).wait_recv()
                cp = copy(w, 4 + j, (*chip, c), sibling)
                cp.start()
                passed.append(cp)
        for w in range(nw):
            copy(w, 0, sibling, me).wait_recv()
            for j, chip in enumerate(chips):
                copy(w, 4 + j, (*chip, 1 - c), me).wait_recv()
        for cp in first + passed:
            cp.wait_send()
        for cp in mine:
            cp.wait()

    return pl.pallas_call(
        body, name=name,
        out_shape=[jax.ShapeDtypeStruct(_full_shape(s.shape, k), s.dtype) for s, k in zip(shards, kinds)],
        in_specs=[ANY] * nw, out_specs=[ANY] * nw,
        scratch_shapes=[pltpu.SemaphoreType.DMA((7 * nw,)), pltpu.SemaphoreType.DMA((7 * nw,)),
                        pltpu.SemaphoreType.DMA((nw,))],
    )(*shards)


def _src_win(ref, l, kind, k, a, b):
    return _win2(ref if l is None else ref.at[l], kind, k, a, b)


def rs_start(srcs, items, name):
    ns, nz = len(srcs), len(items)
    zones = [lax.empty((N_PEER, a, b), srcs[w].dtype) for w, l, kind, a, b in items]

    def body(*refs):
        s_refs = refs[:ns]
        send_sems, recv_sems = refs[ns + nz], refs[ns + nz + 1]
        z_refs = refs[ns + nz + 2 + ns:ns + nz + 2 + ns + nz]
        token = refs[-1]
        x, y, c = _pos()
        for t, (w, l, kind, a, b) in enumerate(items):
            for r in range(1, N_DEV):
                peer = _peer(x, y, c, r)
                pltpu.make_async_remote_copy(
                    src_ref=_src_win(s_refs[w], l, kind, 4 * peer[0] + 2 * peer[1] + peer[2], a, b),
                    dst_ref=z_refs[t].at[r - 1],
                    send_sem=send_sems.at[N_PEER * t + r - 1], recv_sem=recv_sems.at[N_PEER * t + r - 1],
                    device_id=peer, device_id_type=MESH).start()
        token[...] = jnp.zeros_like(token)

    n_sem = N_PEER * nz
    outs = pl.pallas_call(
        body, name=name,
        out_shape=(pltpu.SemaphoreType.DMA((n_sem,)), pltpu.SemaphoreType.DMA((n_sem,)),
                   *[pltpu.HBM(s.shape, s.dtype) for s in srcs], *[pltpu.HBM(z.shape, z.dtype) for z in zones],
                   jax.ShapeDtypeStruct((8, LANE), F32)),
        in_specs=[HBM] * (ns + nz), out_specs=(SEMS, SEMS, *[HBM] * (ns + nz), pl.BlockSpec(memory_space=pltpu.VMEM)),
        input_output_aliases={i: 2 + i for i in range(ns + nz)},
        compiler_params=pltpu.CompilerParams(has_side_effects=DATAFLOW),
    )(*[_in_hbm(s) for s in srcs], *[_in_hbm(z) for z in zones])
    return outs[:2], list(outs[2:2 + ns]), list(outs[2 + ns:2 + ns + nz]), outs[-1]


def rs_wait(zones, items, sems, after, keep, name):
    nz, nk = len(zones), len(keep)

    def body(*refs):
        z_refs = refs[:nz]
        send_sems, recv_sems = refs[nz], refs[nz + 1]
        x, y, c = _pos()
        for t, z_ref in enumerate(z_refs):
            for r in range(1, N_DEV):
                cp = pltpu.make_async_remote_copy(
                    src_ref=z_ref.at[r - 1], dst_ref=z_ref.at[r - 1],
                    send_sem=send_sems.at[N_PEER * t + r - 1], recv_sem=recv_sems.at[N_PEER * t + r - 1],
                    device_id=_peer(x, y, c, r), device_id_type=MESH)
                cp.wait_send()
                cp.wait_recv()

    outs = pl.pallas_call(
        body, name=name, out_shape=tuple(pltpu.HBM(z.shape, z.dtype) for z in zones),
        in_specs=[HBM] * nz + [SEMS] * 2 + [pl.BlockSpec(memory_space=pl.ANY)] * (1 + nk),
        out_specs=tuple([HBM] * nz), input_output_aliases={i: i for i in range(nz)},
        compiler_params=pltpu.CompilerParams(has_side_effects=DATAFLOW),
    )(*zones, *sems, after, *keep)
    return list(outs)


def adam_rs(w, m, v, l, own, kind, zone, outs, name):
    n, a, b = w.shape
    ta = max(t for t in range(16, min(a, 256) + 1, 16) if a % t == 0)
    per = a // ta
    me = (4 * lax.axis_index("x") + 2 * lax.axis_index("y") + lax.axis_index("c")).astype(jnp.int32).reshape(1)

    def body(me_ref, w_ref, m_ref, v_ref, own_ref, z_ref, i0, i1, i2, i3, g_ref, d_ref, m2_ref, v2_ref):
        gv = own_ref[...].astype(F32)
        for k in range(N_PEER):
            gv = gv + z_ref[k].astype(F32)
        m2 = ADAM_B1 * m_ref[...] + (1.0 - ADAM_B1) * gv
        v2 = ADAM_B2 * v_ref[...] + (1.0 - ADAM_B2) * (gv * gv)
        m_hat = m2 / (1.0 - ADAM_B1 ** ADAM_STEP)
        v_hat = v2 / (1.0 - ADAM_B2 ** ADAM_STEP)
        g_ref[...] = gv
        d_ref[...] = -ADAM_LR * (m_hat / (jnp.sqrt(v_hat) + ADAM_EPS) + ADAM_WD * w_ref[...])
        m2_ref[...] = m2
        v2_ref[...] = v2

    spec = pl.BlockSpec((None, ta, b), lambda r, me_ref: (l, r, 0))
    if kind == "lead":
        own_spec = pl.BlockSpec((None, ta, b), lambda r, me_ref: (me_ref[0], r, 0))
    elif kind == "row":
        own_spec = pl.BlockSpec((None, ta, b), lambda r, me_ref: (l, me_ref[0] * per + r, 0))
    else:
        own_spec = pl.BlockSpec((None, ta, b), lambda r, me_ref: (l, r, me_ref[0]))
    return pl.pallas_call(
        body, name=name, out_shape=[jax.ShapeDtypeStruct((n, a, b), F32)] * 4,
        grid_spec=pltpu.PrefetchScalarGridSpec(
            num_scalar_prefetch=1, grid=(per,),
            in_specs=[spec] * 3 + [own_spec, pl.BlockSpec((N_PEER, ta, b), lambda r, me_ref: (0, r, 0))] + [ANY] * 4,
            out_specs=[spec] * 4),
        input_output_aliases={6 + k: k for k in range(4)},
        compiler_params=_cp("parallel"),
    )(me, w, m, v, own, zone, *outs)


def small_exchange(sh, rep, name):
    _, Rs, C = sh.shape
    Rr = rep.shape[0]

    def body(sh_ref, rep_ref, sh_out, rep_out, send_sems, recv_sems, local_sems):
        x, y, c = _pos()
        me = 4 * x + 2 * y + c
        l1 = pltpu.make_async_copy(sh_ref.at[me], sh_out.at[me], local_sems.at[0])
        l2 = pltpu.make_async_copy(rep_ref, rep_out.at[me], local_sems.at[1])
        l1.start()
        l2.start()

        def flip(v, bit):
            return 1 - v if bit else v

        sends, recvs = [], []
        for r in range(1, N_DEV):
            peer = (flip(x, r & 4), flip(y, r & 2), flip(c, r & 1))
            pid = 4 * peer[0] + 2 * peer[1] + peer[2]
            k = 2 * (r - 1)
            mk = lambda src, dst, kk: pltpu.make_async_remote_copy(
                src_ref=src, dst_ref=dst, send_sem=send_sems.at[kk], recv_sem=recv_sems.at[kk],
                device_id=peer, device_id_type=MESH)
            sends += [mk(sh_ref.at[pid], sh_out.at[me], k), mk(rep_ref, rep_out.at[me], k + 1)]
            recvs += [mk(sh_ref.at[me], sh_out.at[pid], k), mk(rep_ref, rep_out.at[pid], k + 1)]
        for cp in sends:
            cp.start()
        for cp in recvs:
            cp.wait_recv()
        for cp in sends:
            cp.wait_send()
        l1.wait()
        l2.wait()

    n = 2 * (N_DEV - 1)
    return pl.pallas_call(
        body, name=name,
        out_shape=[jax.ShapeDtypeStruct((N_DEV, Rs, C), sh.dtype), jax.ShapeDtypeStruct((N_DEV, *rep.shape), rep.dtype)],
        in_specs=[ANY, ANY], out_specs=[ANY, ANY],
        scratch_shapes=[pltpu.SemaphoreType.DMA((n,)), pltpu.SemaphoreType.DMA((n,)), pltpu.SemaphoreType.DMA((2,))],
    )(sh, rep)


def adam_slots(w, m, v, slots, name):
    S, n, a, b = slots.shape
    ta = max(t for t in range(16, min(a, 512) + 1, 8)
             if a % t == 0 and t * S * b * slots.dtype.itemsize <= 4 * 1024 * 1024)

    def body(w_ref, m_ref, v_ref, s_ref, g_ref, d_ref, m2_ref, v2_ref):
        gv = s_ref[0].astype(F32)
        for k in range(1, S):
            gv = gv + s_ref[k].astype(F32)
        m2 = ADAM_B1 * m_ref[...] + (1.0 - ADAM_B1) * gv
        v2 = ADAM_B2 * v_ref[...] + (1.0 - ADAM_B2) * (gv * gv)
        m_hat = m2 / (1.0 - ADAM_B1 ** ADAM_STEP)
        v_hat = v2 / (1.0 - ADAM_B2 ** ADAM_STEP)
        g_ref[...] = gv
        d_ref[...] = -ADAM_LR * (m_hat / (jnp.sqrt(v_hat) + ADAM_EPS) + ADAM_WD * w_ref[...])
        m2_ref[...] = m2
        v2_ref[...] = v2

    spec = pl.BlockSpec((None, ta, b), lambda l, r: (l, r, 0))
    return pl.pallas_call(
        body, name=name, grid=(n, a // ta),
        in_specs=[spec] * 3 + [pl.BlockSpec((S, None, ta, b), lambda l, r: (0, l, r, 0))], out_specs=[spec] * 4,
        out_shape=[jax.ShapeDtypeStruct((n, a, b), F32)] * 4, compiler_params=_cp("parallel", "parallel"),
    )(w, m, v, slots)


WEIGHTS = ["norm_g", "ssm_in_w", "ssm_conv_w", "ssm_conv_b", "ssm_dt_bias", "ssm_A_log", "ssm_D", "ssm_norm_g",
           "ssm_out_w", "cf_pw1_w", "cf_pw1_b", "cf_dw_w", "cf_dw_b", "cf_ln_g", "cf_ln_b", "cf_pw2_w", "cf_pw2_b",
           "xa_mem_g", "xa_q_w", "xa_kv_w", "xa_o_w", "ffn_in_w", "ffn_conv_w", "ffn_conv_b", "ffn_out_w"]
ARGS = ["x", "mem"] + WEIGHTS + ["loss_target"] + ["m_" + n for n in WEIGHTS] + ["v_" + n for n in WEIGHTS]
BIG = {"ssm_in_w": "col", "ssm_out_w": "row", "cf_pw1_w": "col", "cf_pw2_w": "row", "xa_q_w": "row",
       "xa_kv_w": "col", "xa_o_w": "row", "ffn_in_w": "col", "ffn_out_w": "row"}
SMALL = ["norm_g", "ssm_conv_w", "cf_pw1_b", "cf_dw_w", "cf_dw_b", "cf_ln_g", "cf_ln_b", "cf_pw2_b", "ffn_conv_w"]
REP = ["ssm_conv_b", "ssm_dt_bias", "ssm_A_log", "ssm_D", "ssm_norm_g", "xa_mem_g", "ffn_conv_b"]
SMALL_W = 768
REP_W = 512


def _r8(n):
    return -(-n // 8) * 8


def _stack2d(arrs, wid):
    parts = []
    for a in arrs:
        r, c = a.shape[-2:]
        parts.append(jnp.pad(a, [(0, 0)] * (a.ndim - 2) + [(0, _r8(r) - r), (0, wid - c)]))
    return jnp.concatenate(parts, axis=-2)


def _unstack2d(buf, shapes2d):
    out, o = [], 0
    for r, c in shapes2d:
        out.append(buf[..., o:o + r, :c])
        o += _r8(r)
    return out


def _gathered_to_full(g):
    lead = g.shape[1:-1]
    return jnp.moveaxis(g, 0, -2).reshape(*lead, N_DEV * g.shape[-1])


def _full_to_slots(w):
    lead = w.shape[:-1]
    return jnp.moveaxis(w.reshape(*lead, N_DEV, w.shape[-1] // N_DEV), -2, 0)


def kernel(x, mem, norm_g, ssm_in_w, ssm_conv_w, ssm_conv_b, ssm_dt_bias, ssm_A_log, ssm_D, ssm_norm_g, ssm_out_w, cf_pw1_w, cf_pw1_b, cf_dw_w, cf_dw_b, cf_ln_g, cf_ln_b, cf_pw2_w, cf_pw2_b, xa_mem_g, xa_q_w, xa_kv_w, xa_o_w, ffn_in_w, ffn_conv_w, ffn_conv_b, ffn_out_w, loss_target, m_norm_g, m_ssm_in_w, m_ssm_conv_w, m_ssm_conv_b, m_ssm_dt_bias, m_ssm_A_log, m_ssm_D, m_ssm_norm_g, m_ssm_out_w, m_cf_pw1_w, m_cf_pw1_b, m_cf_dw_w, m_cf_dw_b, m_cf_ln_g, m_cf_ln_b, m_cf_pw2_w, m_cf_pw2_b, m_xa_mem_g, m_xa_q_w, m_xa_kv_w, m_xa_o_w, m_ffn_in_w, m_ffn_conv_w, m_ffn_conv_b, m_ffn_out_w, v_norm_g, v_ssm_in_w, v_ssm_conv_w, v_ssm_conv_b, v_ssm_dt_bias, v_ssm_A_log, v_ssm_D, v_ssm_norm_g, v_ssm_out_w, v_cf_pw1_w, v_cf_pw1_b, v_cf_dw_w, v_cf_dw_b, v_cf_ln_g, v_cf_ln_b, v_cf_pw2_w, v_cf_pw2_b, v_xa_mem_g, v_xa_q_w, v_xa_kv_w, v_xa_o_w, v_ffn_in_w, v_ffn_conv_w, v_ffn_conv_b, v_ffn_out_w):
    return _step(x, mem, norm_g, ssm_in_w, ssm_conv_w, ssm_conv_b, ssm_dt_bias, ssm_A_log, ssm_D, ssm_norm_g, ssm_out_w, cf_pw1_w, cf_pw1_b, cf_dw_w, cf_dw_b, cf_ln_g, cf_ln_b, cf_pw2_w, cf_pw2_b, xa_mem_g, xa_q_w, xa_kv_w, xa_o_w, ffn_in_w, ffn_conv_w, ffn_conv_b, ffn_out_w, loss_target, m_norm_g, m_ssm_in_w, m_ssm_conv_w, m_ssm_conv_b, m_ssm_dt_bias, m_ssm_A_log, m_ssm_D, m_ssm_norm_g, m_ssm_out_w, m_cf_pw1_w, m_cf_pw1_b, m_cf_dw_w, m_cf_dw_b, m_cf_ln_g, m_cf_ln_b, m_cf_pw2_w, m_cf_pw2_b, m_xa_mem_g, m_xa_q_w, m_xa_kv_w, m_xa_o_w, m_ffn_in_w, m_ffn_conv_w, m_ffn_conv_b, m_ffn_out_w, v_norm_g, v_ssm_in_w, v_ssm_conv_w, v_ssm_conv_b, v_ssm_dt_bias, v_ssm_A_log, v_ssm_D, v_ssm_norm_g, v_ssm_out_w, v_cf_pw1_w, v_cf_pw1_b, v_cf_dw_w, v_cf_dw_b, v_cf_ln_g, v_cf_ln_b, v_cf_pw2_w, v_cf_pw2_b, v_xa_mem_g, v_xa_q_w, v_xa_kv_w, v_xa_o_w, v_ffn_in_w, v_ffn_conv_w, v_ffn_conv_b, v_ffn_out_w)


def _step(*args):
    A = dict(zip(ARGS, args, strict=True))
    x, mem, target = A["x"][0], A["mem"][0], A["loss_target"][0]

    big = list(BIG)
    geo = [A[n].shape for n in big]
    kinds = ["row" if BIG[n] == "row" else ("col" if A[n].shape[-1] % LANE == 0 else "lead") for n in big]
    W = {n: A[n] for n in REP}
    small2d = [(A[n].size // A[n].shape[-1], A[n].shape[-1]) for n in SMALL]
    rep2d = [(A[n].size // REP_W, REP_W) if A[n].shape[-1] % REP_W == 0 else A[n].shape for n in REP] + [(1, 1)]
    stack_small = lambda pre: _stack2d([A[pre + n].reshape(rc) for n, rc in zip(SMALL, small2d)], SMALL_W)
    stack_rep = lambda pre: _stack2d([A[pre + n].reshape(rc) for n, rc in zip(REP, rep2d)] + [jnp.zeros((1, 1), F32)],
                                     REP_W)
    small_g = all_gather(stack_small(""), name="gather_small")
    for n, g in zip(SMALL, _unstack2d(small_g, small2d)):
        W[n] = _gathered_to_full(g.reshape(N_DEV, *A[n].shape))

    shards = [A[n].astype(BF16) for n in big]
    for n in big:
        W[n] = [None] * A[n].shape[0]
    first = _sublayer_weights(0, 0)
    got0 = gather_now([shards[big.index(n)][l:l + 1] for n, l in first], [kinds[big.index(n)] for n, l in first],
                      name="gather_first")
    for (n, l), g in zip(first, got0):
        W[n][l] = _gathered_to_full(g)[0] if kinds[big.index(n)] == "lead" else g[0]
    items, sub_items = [], {}
    for i in range(DEPTH):
        for s in range(3):
            sub_items[i, s] = []
            for n, l in _sublayer_weights(i, s) if (i, s) != (0, 0) else []:
                w = big.index(n)
                sub_items[i, s].append(len(items))
                items.append((w, l, kinds[w], *geo[w][1:]))
    sems, shards_thru, zones, token = gather_start(shards, items, [small_g, got0[0]], name="gather_start")
    x = x + token[0, 0]

    def fetch(i, s, x_in):
        ids = sub_items[i, s]
        if not ids:
            return
        got = gather_wait([zones[t] for t in ids], ids, items, sems, x_in, shards_thru if (i, s) == (DEPTH - 1, 2) else [],
                          name=f"gather_wait{i}{s}")
        for t, z in zip(ids, got):
            w, l, kind = items[t][:3]
            W[big[w]][l] = _gathered_to_full(z) if kind == "lead" else z

    sent = []
    final = {}

    def layer_done(i, s, GB, g_in):
        srcs, its = [], []
        for n, l in _sublayer_weights(i, s):
            w = big.index(n)
            if kinds[w] == "lead":
                srcs.append(_full_to_slots(g_in if n == "ssm_in_w" else GB[n][l]))
                its.append((len(srcs) - 1, None, "lead", *geo[w][1:], n, l))
            else:
                srcs.append(GB[n])
                its.append((len(srcs) - 1, l, kinds[w], *geo[w][1:], n, l))
        sems_i, thru, zones_i, token_i = rs_start(srcs, [it[:5] for it in its], name=f"rs_start{i}{s}")
        for it, s in zip(its, thru):
            if it[2] != "lead":
                GB[it[5]] = s
        sent.append((its, sems_i, [s for it, s in zip(its, thru) if it[2] == "lead"], zones_i))
        final["GB"] = GB
        return token_i

    loss, grad_x, G = local_step(x, mem, target, W, fetch, layer_done)

    sh = _stack2d([_full_to_slots(G[n]).reshape(N_DEV, *rc) for n, rc in zip(SMALL, small2d)], SMALL_W)
    rep = _stack2d([G[n].reshape(rc) for n, rc in zip(REP, rep2d)] + [loss[:, :1]], REP_W)
    sh_got, rep_got = small_exchange(sh, rep, name="small_exchange")

    res = {}
    GBf = final["GB"]
    bufs = {n: [lax.empty(A[n].shape, F32) for _ in range(4)] for n in big}
    for i, (its, sems_i, lead_srcs, zones_i) in enumerate(sent):
        keep = lead_srcs + [GBf[it[5]] for it in its if it[2] != "lead"]
        zones_i = rs_wait(zones_i, [it[:5] for it in its], sems_i, sh_got, keep, name=f"rs_wait{i}")
        lead_it = iter(lead_srcs)
        for it, z in zip(its, zones_i):
            n, l = it[5], it[6]
            own = next(lead_it) if it[2] == "lead" else GBf[n]
            bufs[n] = adam_rs(A[n], A["m_" + n], A["v_" + n], l, own, it[2], z, bufs[n], name=f"adam_{n}{l}")
    for n in big:
        res[n] = tuple(bufs[n])
    for names, shapes2d, stack, slots, tag in ((SMALL, small2d, stack_small, sh_got, "small"),
                                               (REP, rep2d, stack_rep, rep_got, "rep")):
        outs4 = adam_slots(stack("")[None], stack("m_")[None], stack("v_")[None], slots[:, None], name=f"adam_{tag}")
        parts = [_unstack2d(o[0], shapes2d) for o in outs4]
        for k, n in enumerate(names):
            res[n] = tuple(q[k].reshape(A[n].shape) for q in parts)
        if tag == "rep":
            total_loss = parts[0][-1][0, 0]

    outs = [total_loss, grad_x[None]]
    for k in range(4):
        outs += [res[n][k] for n in WEIGHTS]
    return tuple(outs)
```

```python
import jax
import jax.numpy as jnp
from jax import lax
from jax.experimental import pallas as pl
from jax.experimental.pallas import tpu as pltpu

F32 = jnp.float32
BF16 = jnp.bfloat16

D_MODEL = 1024
D_INNER = 2048
N_HEADS = 32
HEAD_DIM = 64
N_GROUPS = 4
D_STATE = 128
CHUNK = 128
CONV_DIM = 3072
SSM_K = 4
CF_K = 31
N_MEM = 256
XA_HEADS = 4
XA_HD = 256
D_FF = 2816
FFN_K = 3
EPS = 1e-6
DEPTH = 4
N_DEV = 8

ADAM_LR = 0.001
ADAM_B1 = 0.9
ADAM_B2 = 0.999
ADAM_EPS = 1e-08
ADAM_WD = 0.01
ADAM_STEP = 10

LANE = 128
VMEM_LIMIT = 56 * 1024 * 1024
NEG = -1e30
MESH = pl.DeviceIdType.MESH


def _cp(*sem):
    return pltpu.CompilerParams(dimension_semantics=sem if sem else None, vmem_limit_bytes=VMEM_LIMIT)


def _tile(n, cap):
    if n <= cap:
        return n
    best = 0
    for t in range(LANE, cap + 1, LANE):
        if n % t == 0:
            best = t
    assert best, (n, cap)
    return best


def _sig(x):
    return 1.0 / (1.0 + jnp.exp(-x))


def _split3(v):
    v0 = v.astype(BF16)
    r1 = v - v0.astype(F32)
    v1 = r1.astype(BF16)
    v2 = (r1 - v1.astype(F32)).astype(BF16)
    return v0, v1, v2


def _dot(a, b, ca=1, cb=0):
    return lax.dot_general(a, b, (((ca,), (cb,)), ((), ())), preferred_element_type=F32)


def _dot3(v, m, ca=1, cb=0):
    v0, v1, v2 = _split3(v)
    return _dot(v0, m, ca, cb) + _dot(v1, m, ca, cb) + _dot(v2, m, ca, cb)


def mm(a, b, *, ta=False, tb=False, bias=None, acc=None, out_dtype=F32, a_idx=None, layer=None, b_k0=0, b_n=None,
       into=None, name):
    if isinstance(b, (list, tuple)):
        b, layer = b[layer], None
    if ta:
        K, M = a.shape[-2:]
    else:
        M, K = a.shape[-2:]
    N = b_n if b_n is not None else (b.shape[-2] if tb else b.shape[-1])
    assert (b.ndim == 3) == (layer is not None) and (a.ndim == 3) == (a_idx is not None)
    tm = _tile(M, 1024)
    tn = _tile(N, 1536)
    tk = _tile(K, 2048)
    nk = K // tk
    assert b_k0 % tk == 0 and b_k0 + K <= (b.shape[-1] if tb else b.shape[-2])
    kb = b_k0 // tk
    has_bias, has_acc = bias is not None, acc is not None
    if into is not None:
        out_dtype = into[0].dtype
        assert into[0].shape[1] == M and into[2] % tn == 0 and into[2] + N <= into[0].shape[2] and not has_acc

    def body(*refs):
        a_ref, b_ref = refs[0], refs[1]
        pos = 2
        bias_ref = acc_ref = None
        if has_bias:
            bias_ref = refs[pos]
            pos += 1
        if has_acc:
            acc_ref = refs[pos]
            pos += 1
        if into is not None:
            pos += 1
        o_ref = refs[pos]
        s_ref = refs[pos + 1] if nk > 1 else None
        p = _dot(a_ref[...].astype(BF16), b_ref[...].astype(BF16), 0 if ta else 1, 1 if tb else 0)

        def extras(v):
            if has_bias:
                v = v + bias_ref[...]
            if has_acc:
                v = v + acc_ref[...]
            return v

        if nk == 1:
            o_ref[...] = extras(p).astype(out_dtype)
        else:
            k = pl.program_id(2)

            @pl.when(k == 0)
            def _():
                s_ref[...] = extras(p)

            @pl.when(k > 0)
            def _():
                s_ref[...] += p

            @pl.when(k == nk - 1)
            def _():
                o_ref[...] = s_ref[...].astype(out_dtype)

    lead_a = () if a_idx is None else (a_idx,)
    lead_b = () if layer is None else (layer,)
    sq = lambda lead: (None,) * len(lead)
    if ta:
        a_spec = pl.BlockSpec((*sq(lead_a), tk, tm), lambda i, j, k: (*lead_a, k, i))
    else:
        a_spec = pl.BlockSpec((*sq(lead_a), tm, tk), lambda i, j, k: (*lead_a, i, k))
    if tb:
        b_spec = pl.BlockSpec((*sq(lead_b), tn, tk), lambda i, j, k: (*lead_b, j, k + kb))
    else:
        b_spec = pl.BlockSpec((*sq(lead_b), tk, tn), lambda i, j, k: (*lead_b, k + kb, j))
    in_specs, args = [a_spec, b_spec], [a, b]
    if has_bias:
        in_specs.append(pl.BlockSpec((1, tn), lambda i, j, k: (0, j)))
        args.append(bias.reshape(1, N).astype(F32))
    if has_acc:
        in_specs.append(pl.BlockSpec((tm, tn), lambda i, j, k: (i, j)))
        args.append(acc)
    if into is None:
        out_spec = pl.BlockSpec((tm, tn), lambda i, j, k: (i, j))
        out_shape = jax.ShapeDtypeStruct((M, N), out_dtype)
        aliases = {}
    else:
        buf, l, col0 = into
        cb = col0 // tn
        in_specs.append(pl.BlockSpec(memory_space=pl.ANY))
        args.append(buf)
        out_spec = pl.BlockSpec((None, tm, tn), lambda i, j, k: (l, i, j + cb))
        out_shape = jax.ShapeDtypeStruct(buf.shape, buf.dtype)
        aliases = {len(args) - 1: 0}
    return pl.pallas_call(
        body, name=name, grid=(M // tm, N // tn, nk),
        in_specs=in_specs, out_specs=out_spec, out_shape=out_shape, input_output_aliases=aliases,
        scratch_shapes=[pltpu.VMEM((tm, tn), F32)] if nk > 1 else [],
        compiler_params=_cp("parallel", "parallel", "arbitrary"),
    )(*args)


def colsum(x, name):
    L, C = x.shape
    tr = _tile(L, 512)
    tc = _tile(C, 1024)

    def body(x_ref, o_ref):
        @pl.when(pl.program_id(1) == 0)
        def _():
            o_ref[...] = jnp.zeros_like(o_ref)

        o_ref[...] += jnp.sum(x_ref[...].astype(F32), axis=0, keepdims=True)

    return pl.pallas_call(
        body, name=name, grid=(C // tc, L // tr),
        in_specs=[pl.BlockSpec((tr, tc), lambda j, i: (i, j))],
        out_specs=pl.BlockSpec((1, tc), lambda j, i: (0, j)),
        out_shape=jax.ShapeDtypeStruct((1, C), F32),
        compiler_params=_cp("parallel", "arbitrary"),
    )(x)


TR = 256


def _row_spec(tr, w):
    return pl.BlockSpec((tr, w), lambda i: (i, 0))


def _vec_spec(w):
    return pl.BlockSpec((1, w), lambda i: (0, 0))


def _rms(v):
    return lax.rsqrt(jnp.mean(v * v, axis=-1, keepdims=True) + EPS)


def norm_fwd(x, g, name):
    L, D = x.shape
    tr = min(TR, L)

    def body(x_ref, g_ref, h_ref):
        xv = x_ref[...]
        h_ref[...] = (xv * _rms(xv) * g_ref[...]).astype(BF16)

    return pl.pallas_call(
        body, name=name, grid=(L // tr,),
        in_specs=[_row_spec(tr, D), _vec_spec(D)], out_specs=_row_spec(tr, D),
        out_shape=jax.ShapeDtypeStruct((L, D), BF16), compiler_params=_cp("parallel"),
    )(x, g.reshape(1, D))


def bnd_fwd(x, f, gpost, gpre, name):
    L, D = x.shape
    tr = min(TR, L)

    def body(x_ref, f_ref, gp_ref, gn_ref, xo_ref, h_ref):
        fv = f_ref[...].astype(F32)
        xn = x_ref[...] + fv * _rms(fv) * gp_ref[...]
        xo_ref[...] = xn
        h_ref[...] = (xn * _rms(xn) * gn_ref[...]).astype(BF16)

    return pl.pallas_call(
        body, name=name, grid=(L // tr,),
        in_specs=[_row_spec(tr, D), _row_spec(tr, D), _vec_spec(D), _vec_spec(D)],
        out_specs=[_row_spec(tr, D), _row_spec(tr, D)],
        out_shape=[jax.ShapeDtypeStruct((L, D), F32), jax.ShapeDtypeStruct((L, D), BF16)],
        compiler_params=_cp("parallel"),
    )(x, f, gpost.reshape(1, D), gpre.reshape(1, D))


def final_fwd(x, f, gpost, target, name):
    L, D = x.shape
    tr = min(TR, L)
    n = L // tr

    def body(x_ref, f_ref, gp_ref, t_ref, dy_ref, loss_ref, acc_ref):
        i = pl.program_id(0)

        @pl.when(i == 0)
        def _():
            acc_ref[...] = jnp.zeros_like(acc_ref)

        fv = f_ref[...].astype(F32)
        e = x_ref[...] + fv * _rms(fv) * gp_ref[...] - t_ref[...]
        dy_ref[...] = e * (1.0 / D)
        acc_ref[...] += jnp.sum(e * e, axis=0, keepdims=True)

        @pl.when(i == n - 1)
        def _():
            loss_ref[...] = jnp.full((1, LANE), 0.5 / D, F32) * jnp.sum(acc_ref[...])

    return pl.pallas_call(
        body, name=name, grid=(n,),
        in_specs=[_row_spec(tr, D), _row_spec(tr, D), _vec_spec(D), _row_spec(tr, D)],
        out_specs=[_row_spec(tr, D), _vec_spec(LANE)],
        out_shape=[jax.ShapeDtypeStruct((L, D), F32), jax.ShapeDtypeStruct((1, LANE), F32)],
        scratch_shapes=[pltpu.VMEM((1, D), F32)],
        compiler_params=_cp("arbitrary"),
    )(x, f, gpost.reshape(1, D), target)


def _rms_bwd(v, g, dy):
    r = _rms(v)
    vn = v * r
    dg = jnp.sum(dy * vn, axis=0, keepdims=True)
    dvn = dy * g
    dv = r * (dvn - vn * jnp.mean(dvn * vn, axis=-1, keepdims=True))
    return dv, dg


def bnd_bwd(dxp, *, pre=None, post=None, dep=None, name):
    L, D = dxp.shape
    tr = min(TR, L)
    has_pre, has_post = pre is not None, post is not None

    def body(*refs):
        pos = 0
        dxp_ref = refs[pos]; pos += 1
        if has_pre:
            x_ref, gpre_ref, dh_ref = refs[pos:pos + 3]; pos += 3
        if has_post:
            f_ref, gpost_ref = refs[pos:pos + 2]; pos += 2
        if dep is not None:
            pos += 1
        if has_pre:
            dx_ref, dgpre_ref = refs[pos:pos + 2]; pos += 2
        if has_post:
            df_ref, dgpost_ref = refs[pos:pos + 2]; pos += 2
        i = pl.program_id(0)
        dx = dxp_ref[...]
        if has_pre:
            d, dg = _rms_bwd(x_ref[...], gpre_ref[...], dh_ref[...])
            dx = dx + d
            dx_ref[...] = dx

            @pl.when(i == 0)
            def _():
                dgpre_ref[...] = jnp.zeros_like(dgpre_ref)

            dgpre_ref[...] += dg
        if has_post:
            d, dg = _rms_bwd(f_ref[...].astype(F32), gpost_ref[...], dx)
            df_ref[...] = d.astype(BF16)

            @pl.when(i == 0)
            def _():
                dgpost_ref[...] = jnp.zeros_like(dgpost_ref)

            dgpost_ref[...] += dg

    in_specs, args = [_row_spec(tr, D)], [dxp]
    out_specs, out_shape, names = [], [], []
    if has_pre:
        x, gpre, dh = pre
        in_specs += [_row_spec(tr, D), _vec_spec(D), _row_spec(tr, D)]
        args += [x, gpre.reshape(1, D), dh]
        out_specs += [_row_spec(tr, D), _vec_spec(D)]
        out_shape += [jax.ShapeDtypeStruct((L, D), F32), jax.ShapeDtypeStruct((1, D), F32)]
        names += ["dx", "dgpre"]
    if has_post:
        f, gpost = post
        in_specs += [_row_spec(tr, D), _vec_spec(D)]
        args += [f, gpost.reshape(1, D)]
        out_specs += [_row_spec(tr, D), _vec_spec(D)]
        out_shape += [jax.ShapeDtypeStruct((L, D), BF16), jax.ShapeDtypeStruct((1, D), F32)]
        names += ["df", "dgpost"]
    if dep is not None:
        in_specs.append(pl.BlockSpec(memory_space=pl.ANY))
        args.append(dep)
    outs = pl.pallas_call(
        body, name=name, grid=(L // tr,), in_specs=in_specs, out_specs=out_specs, out_shape=out_shape,
        compiler_params=_cp("arbitrary"),
    )(*args)
    return dict(zip(names, outs))


def norm_dg(x, dy, name):
    L, D = x.shape
    tr = min(TR, L)

    def body(x_ref, dy_ref, o_ref):
        @pl.when(pl.program_id(0) == 0)
        def _():
            o_ref[...] = jnp.zeros_like(o_ref)

        xv = x_ref[...]
        o_ref[...] += jnp.sum(dy_ref[...] * xv * _rms(xv), axis=0, keepdims=True)

    return pl.pallas_call(
        body, name=name, grid=(L // tr,),
        in_specs=[_row_spec(tr, D), _row_spec(tr, D)], out_specs=_vec_spec(D),
        out_shape=jax.ShapeDtypeStruct((1, D), F32), compiler_params=_cp("arbitrary"),
    )(x, dy)


HALO = 32


def _prev_halo_spec(tr, tc, col):
    per = tr // HALO
    return pl.BlockSpec((HALO, tc), lambda *g: (jnp.maximum(g[-1] * per - 1, 0), col(*g)))


def _fill_prev(scr, halo_val, blk_val, i, tr):
    scr[pl.ds(0, HALO), :] = jnp.where(i == 0, 0.0, halo_val)
    scr[pl.ds(HALO, tr), :] = blk_val


def _conv(scr, w_ref, K, tr):
    acc = None
    for k in range(K):
        term = scr[pl.ds(HALO - (K - 1) + k, tr), :] * w_ref[k:k + 1, :]
        acc = term if acc is None else acc + term
    return acc


def _shift_copies(scr, sh, rows):
    n = rows - 8
    for r in range(1, 8):
        sh[r - 1, pl.ds(0, n), :] = scr[pl.ds(r, n), :]


def _tap(scr, sh, off, tr):
    q, r = divmod(off, 8)
    return scr[pl.ds(off, tr), :] if r == 0 else sh[r - 1, pl.ds(8 * q, tr), :]


def _conv_dw(scr, d, o_ref, K, tr):
    for k in range(K):
        o_ref[k:k + 1, :] += jnp.sum(d * scr[pl.ds(HALO - (K - 1) + k, tr), :], axis=0, keepdims=True)


def ssm_conv_fwd(zx, w, b, *, col0, ncols, wcol0, out_dtype, name):
    L = zx.shape[0]
    tr = min(TR, L)
    tc = 1024
    cb, wb = col0 // tc, wcol0 // tc

    def body(x_ref, h_ref, w_ref, b_ref, o_ref, p_ref, scr):
        i = pl.program_id(1)
        _fill_prev(scr, h_ref[...].astype(F32), x_ref[...].astype(F32), i, tr)
        pre = _conv(scr, w_ref, SSM_K, tr) + b_ref[...]
        p_ref[...] = pre.astype(BF16)
        o_ref[...] = (pre * _sig(pre)).astype(out_dtype)

    out = pl.BlockSpec((tr, tc), lambda j, i: (i, j))
    return pl.pallas_call(
        body, name=name, grid=(ncols // tc, L // tr),
        in_specs=[pl.BlockSpec((tr, tc), lambda j, i: (i, j + cb)),
                  _prev_halo_spec(tr, tc, lambda j, i: j + cb),
                  pl.BlockSpec((SSM_K, tc), lambda j, i: (0, j + wb)),
                  pl.BlockSpec((1, tc), lambda j, i: (0, j + wb))],
        out_specs=[out, out],
        out_shape=[jax.ShapeDtypeStruct((L, ncols), out_dtype), jax.ShapeDtypeStruct((L, ncols), BF16)],
        scratch_shapes=[pltpu.VMEM((HALO + tr, tc), F32)],
        compiler_params=_cp("parallel", "parallel"),
    )(zx, zx, w, b)


def ssm_conv_bwd(zx, pre, d, w, *, col0, dcol0, ncols, name):
    L = zx.shape[0]
    tr = min(TR, L)
    tc = 1024
    cb, db_ = col0 // tc, dcol0 // tc
    n = L // tr
    per = tr // HALO
    last = L // HALO - 1

    def body(x_ref, h_ref, p_ref, np_ref, d_ref, nd_ref, w_ref, dx_ref, dw_ref, db_ref, scr, sd):
        i = pl.program_id(1)
        _fill_prev(scr, h_ref[...].astype(F32), x_ref[...].astype(F32), i, tr)

        def dpre(p, dv):
            s = _sig(p)
            return dv * s * (1.0 + p * (1.0 - s))

        dp = dpre(p_ref[...].astype(F32), d_ref[...])
        sd[pl.ds(0, tr), :] = dp
        sd[pl.ds(tr, HALO), :] = jnp.where(i == n - 1, 0.0, dpre(np_ref[...].astype(F32), nd_ref[...]))
        acc = None
        for k in range(SSM_K):
            term = sd[pl.ds(SSM_K - 1 - k, tr), :] * w_ref[k:k + 1, :]
            acc = term if acc is None else acc + term
        dx_ref[...] = acc.astype(BF16)

        @pl.when(i == 0)
        def _():
            dw_ref[...] = jnp.zeros_like(dw_ref)
            db_ref[...] = jnp.zeros_like(db_ref)

        _conv_dw(scr, dp, dw_ref, SSM_K, tr)
        db_ref[...] += jnp.sum(dp, axis=0, keepdims=True)

    nxt = lambda i: jnp.minimum((i + 1) * per, last)
    return pl.pallas_call(
        body, name=name, grid=(ncols // tc, n),
        in_specs=[pl.BlockSpec((tr, tc), lambda j, i: (i, j + cb)),
                  _prev_halo_spec(tr, tc, lambda j, i: j + cb),
                  pl.BlockSpec((tr, tc), lambda j, i: (i, j)),
                  pl.BlockSpec((HALO, tc), lambda j, i: (nxt(i), j)),
                  pl.BlockSpec((tr, tc), lambda j, i: (i, j + db_)),
                  pl.BlockSpec((HALO, tc), lambda j, i: (nxt(i), j + db_)),
                  pl.BlockSpec((SSM_K, tc), lambda j, i: (0, j + db_))],
        out_specs=[pl.BlockSpec((tr, tc), lambda j, i: (i, j)),
                   pl.BlockSpec((SSM_K, tc), lambda j, i: (0, j)),
                   pl.BlockSpec((1, tc), lambda j, i: (0, j))],
        out_shape=[jax.ShapeDtypeStruct((L, ncols), BF16), jax.ShapeDtypeStruct((SSM_K, ncols), F32),
                   jax.ShapeDtypeStruct((1, ncols), F32)],
        scratch_shapes=[pltpu.VMEM((HALO + tr, tc), F32), pltpu.VMEM((tr + HALO, tc), F32)],
        compiler_params=_cp("parallel", "arbitrary"),
    )(zx, zx, pre, pre, d, d, w)


FFN_TC = 1408


def ffn_act_fwd(u, w, b, name):
    L = u.shape[0]
    tr = min(TR, L)
    tc = FFN_TC
    nb = D_FF // tc

    def body(g_ref, hg_ref, v_ref, hv_ref, wg_ref, wv_ref, bg_ref, bv_ref, o_ref, c_ref, sg, sv):
        i = pl.program_id(1)
        _fill_prev(sg, hg_ref[...].astype(F32), g_ref[...].astype(F32), i, tr)
        _fill_prev(sv, hv_ref[...].astype(F32), v_ref[...].astype(F32), i, tr)
        ug = _conv(sg, wg_ref, FFN_K, tr) + bg_ref[...]
        uv = _conv(sv, wv_ref, FFN_K, tr) + bv_ref[...]
        c_ref[0] = ug.astype(BF16)
        c_ref[1] = uv.astype(BF16)
        o_ref[...] = (ug * _sig(ug) * uv).astype(BF16)

    blk = lambda off: pl.BlockSpec((tr, tc), lambda j, i: (i, j + off))
    wsp = lambda off: pl.BlockSpec((FFN_K, tc), lambda j, i: (0, j + off))
    bsp = lambda off: pl.BlockSpec((1, tc), lambda j, i: (0, j + off))
    return pl.pallas_call(
        body, name=name, grid=(nb, L // tr),
        in_specs=[blk(0), _prev_halo_spec(tr, tc, lambda j, i: j),
                  blk(nb), _prev_halo_spec(tr, tc, lambda j, i: j + nb),
                  wsp(0), wsp(nb), bsp(0), bsp(nb)],
        out_specs=[pl.BlockSpec((tr, tc), lambda j, i: (i, j)), pl.BlockSpec((2, tr, tc), lambda j, i: (0, i, j))],
        out_shape=[jax.ShapeDtypeStruct((L, D_FF), BF16), jax.ShapeDtypeStruct((2, L, D_FF), BF16)],
        scratch_shapes=[pltpu.VMEM((HALO + tr, tc), F32), pltpu.VMEM((HALO + tr, tc), F32)],
        compiler_params=_cp("parallel", "parallel"),
    )(u, u, u, u, w, w, b, b)


def ffn_act_bwd(u, c, dact, w, name):
    L = u.shape[0]
    tr = min(TR, L)
    tc = FFN_TC
    nb = D_FF // tc
    n = L // tr
    per = tr // HALO
    last = L // HALO - 1

    def body(g_ref, hg_ref, v_ref, hv_ref, c_ref, nc_ref, da_ref, nda_ref, wg_ref, wv_ref,
             du_ref, dw_ref, db_ref, sg, sv, dg_s, dv_s):
        i = pl.program_id(1)
        _fill_prev(sg, hg_ref[...].astype(F32), g_ref[...].astype(F32), i, tr)
        _fill_prev(sv, hv_ref[...].astype(F32), v_ref[...].astype(F32), i, tr)

        def grads(cg, cv, da):
            s = _sig(cg)
            return da * cv * s * (1.0 + cg * (1.0 - s)), da * cg * s

        dg, dv = grads(c_ref[0].astype(F32), c_ref[1].astype(F32), da_ref[...].astype(F32))
        ndg, ndv = grads(nc_ref[0].astype(F32), nc_ref[1].astype(F32), nda_ref[...].astype(F32))
        at_end = i == n - 1
        for half, (scr, d, nd, x_scr, w_ref) in enumerate(((dg_s, dg, ndg, sg, wg_ref), (dv_s, dv, ndv, sv, wv_ref))):
            scr[pl.ds(0, tr), :] = d
            scr[pl.ds(tr, HALO), :] = jnp.where(at_end, 0.0, nd)
            acc = None
            for k in range(FFN_K):
                term = scr[pl.ds(FFN_K - 1 - k, tr), :] * w_ref[k:k + 1, :]
                acc = term if acc is None else acc + term
            du_ref[half] = acc.astype(BF16)

            @pl.when(i == 0)
            def _():
                dw_ref[half] = jnp.zeros((FFN_K, tc), F32)
                db_ref[half] = jnp.zeros((1, tc), F32)

            for k in range(FFN_K):
                dw_ref[half, k:k + 1, :] += jnp.sum(d * x_scr[pl.ds(HALO - (FFN_K - 1) + k, tr), :], axis=0, keepdims=True)
            db_ref[half] += jnp.sum(d, axis=0, keepdims=True)

    blk = lambda off: pl.BlockSpec((tr, tc), lambda j, i: (i, j + off))
    wsp = lambda off: pl.BlockSpec((FFN_K, tc), lambda j, i: (0, j + off))
    nxt = lambda i: jnp.minimum((i + 1) * per, last)
    return pl.pallas_call(
        body, name=name, grid=(nb, n),
        in_specs=[blk(0), _prev_halo_spec(tr, tc, lambda j, i: j),
                  blk(nb), _prev_halo_spec(tr, tc, lambda j, i: j + nb),
                  pl.BlockSpec((2, tr, tc), lambda j, i: (0, i, j)),
                  pl.BlockSpec((2, HALO, tc), lambda j, i: (0, nxt(i), j)),
                  pl.BlockSpec((tr, tc), lambda j, i: (i, j)),
                  pl.BlockSpec((HALO, tc), lambda j, i: (nxt(i), j)),
                  wsp(0), wsp(nb)],
        out_specs=[pl.BlockSpec((2, tr, tc), lambda j, i: (0, i, j)),
                   pl.BlockSpec((2, FFN_K, tc), lambda j, i: (0, 0, j)),
                   pl.BlockSpec((2, 1, tc), lambda j, i: (0, 0, j))],
        out_shape=[jax.ShapeDtypeStruct((2, L, D_FF), BF16), jax.ShapeDtypeStruct((2, FFN_K, D_FF), F32),
                   jax.ShapeDtypeStruct((2, 1, D_FF), F32)],
        scratch_shapes=[pltpu.VMEM((HALO + tr, tc), F32), pltpu.VMEM((HALO + tr, tc), F32),
                        pltpu.VMEM((tr + HALO, tc), F32), pltpu.VMEM((tr + HALO, tc), F32)],
        compiler_params=_cp("parallel", "arbitrary"),
    )(u, u, u, u, c, c, dact, dact, w, w)


def _ln_stats(c):
    mu = jnp.mean(c, axis=-1, keepdims=True)
    cc = c - mu
    rstd = lax.rsqrt(jnp.mean(cc * cc, axis=-1, keepdims=True) + EPS)
    return cc * rstd, rstd


def cf_fwd(u, dw_w, dw_b, ln_g, ln_b, name):
    L = u.shape[0]
    D = D_MODEL
    tr = min(TR, L)

    def body(a_ref, ha_ref, g_ref, hg_ref, w_ref, b_ref, lg_ref, lb_ref, c_ref, s_ref, scr, sh):
        i = pl.program_id(0)
        glu_h = ha_ref[...].astype(F32) * _sig(hg_ref[...].astype(F32))
        glu = a_ref[...].astype(F32) * _sig(g_ref[...].astype(F32))
        _fill_prev(scr, glu_h, glu, i, tr)
        _shift_copies(scr, sh, HALO + tr)
        c = b_ref[...]
        for k in range(CF_K):
            c = c + _tap(scr, sh, HALO - (CF_K - 1) + k, tr) * w_ref[k:k + 1, :]
        c_ref[...] = c
        xhat, _ = _ln_stats(c)
        ln = xhat * lg_ref[...] + lb_ref[...]
        s_ref[...] = (ln * _sig(ln)).astype(BF16)

    per = tr // HALO
    halo = lambda col: pl.BlockSpec((HALO, D), lambda i: (jnp.maximum(i * per - 1, 0), col))
    return pl.pallas_call(
        body, name=name, grid=(L // tr,),
        in_specs=[pl.BlockSpec((tr, D), lambda i: (i, 0)), halo(0),
                  pl.BlockSpec((tr, D), lambda i: (i, 1)), halo(1),
                  pl.BlockSpec((CF_K, D), lambda i: (0, 0)), _vec_spec(D), _vec_spec(D), _vec_spec(D)],
        out_specs=[_row_spec(tr, D), _row_spec(tr, D)],
        out_shape=[jax.ShapeDtypeStruct((L, D), F32), jax.ShapeDtypeStruct((L, D), BF16)],
        scratch_shapes=[pltpu.VMEM((HALO + tr, D), F32), pltpu.VMEM((7, HALO + tr, D), F32)],
        compiler_params=_cp("parallel"),
    )(u, u, u, u, dw_w, dw_b, ln_g, ln_b)


def cf_bwd_ln(c, ds, ln_g, ln_b, name):
    L, D = c.shape
    tr = min(TR, L)

    def body(c_ref, ds_ref, lg_ref, lb_ref, dc_ref, dg_ref, db_ref):
        xhat, rstd = _ln_stats(c_ref[...])
        ln = xhat * lg_ref[...] + lb_ref[...]
        sg = _sig(ln)
        dln = ds_ref[...].astype(F32) * sg * (1.0 + ln * (1.0 - sg))

        @pl.when(pl.program_id(0) == 0)
        def _():
            dg_ref[...] = jnp.zeros_like(dg_ref)
            db_ref[...] = jnp.zeros_like(db_ref)

        dg_ref[...] += jnp.sum(dln * xhat, axis=0, keepdims=True)
        db_ref[...] += jnp.sum(dln, axis=0, keepdims=True)
        dxh = dln * lg_ref[...]
        dc_ref[...] = rstd * (dxh - jnp.mean(dxh, axis=-1, keepdims=True)
                              - xhat * jnp.mean(dxh * xhat, axis=-1, keepdims=True))

    return pl.pallas_call(
        body, name=name, grid=(L // tr,),
        in_specs=[_row_spec(tr, D), _row_spec(tr, D), _vec_spec(D), _vec_spec(D)],
        out_specs=[_row_spec(tr, D), _vec_spec(D), _vec_spec(D)],
        out_shape=[jax.ShapeDtypeStruct((L, D), F32), jax.ShapeDtypeStruct((1, D), F32),
                   jax.ShapeDtypeStruct((1, D), F32)],
        compiler_params=_cp("arbitrary"),
    )(c, ds, ln_g, ln_b)


def cf_bwd_conv(u, dc, dw_w, name):
    L = u.shape[0]
    D = D_MODEL
    tr = min(TR, L)
    n = L // tr

    def body(a_ref, ha_ref, g_ref, hg_ref, dc_ref, nx_ref, w_ref, du_ref, dw_ref, db_ref, sx, sd, shx, shd):
        i = pl.program_id(0)
        a = a_ref[...].astype(F32)
        sg = _sig(g_ref[...].astype(F32))
        _fill_prev(sx, ha_ref[...].astype(F32) * _sig(hg_ref[...].astype(F32)), a * sg, i, tr)
        dcv = dc_ref[...]
        sd[pl.ds(0, tr), :] = dcv
        sd[pl.ds(tr, HALO), :] = jnp.where(i == n - 1, 0.0, nx_ref[...])
        _shift_copies(sx, shx, HALO + tr)
        _shift_copies(sd, shd, tr + HALO)
        dglu = None
        for k in range(CF_K):
            term = _tap(sd, shd, CF_K - 1 - k, tr) * w_ref[k:k + 1, :]
            dglu = term if dglu is None else dglu + term
        du_ref[:, 0:D] = (dglu * sg).astype(BF16)
        du_ref[:, D:2 * D] = (dglu * a * sg * (1.0 - sg)).astype(BF16)

        @pl.when(i == 0)
        def _():
            dw_ref[...] = jnp.zeros_like(dw_ref)
            db_ref[...] = jnp.zeros_like(db_ref)

        for k in range(CF_K):
            dw_ref[k:k + 1, :] += jnp.sum(dcv * _tap(sx, shx, HALO - (CF_K - 1) + k, tr), axis=0, keepdims=True)
        db_ref[...] += jnp.sum(dcv, axis=0, keepdims=True)

    per = tr // HALO
    last = L // HALO - 1
    halo = lambda col: pl.BlockSpec((HALO, D), lambda i: (jnp.maximum(i * per - 1, 0), col))
    return pl.pallas_call(
        body, name=name, grid=(n,),
        in_specs=[pl.BlockSpec((tr, D), lambda i: (i, 0)), halo(0),
                  pl.BlockSpec((tr, D), lambda i: (i, 1)), halo(1),
                  _row_spec(tr, D),
                  pl.BlockSpec((HALO, D), lambda i: (jnp.minimum((i + 1) * per, last), 0)),
                  pl.BlockSpec((CF_K, D), lambda i: (0, 0))],
        out_specs=[pl.BlockSpec((tr, 2 * D), lambda i: (i, 0)),
                   pl.BlockSpec((CF_K, D), lambda i: (0, 0)), _vec_spec(D)],
        out_shape=[jax.ShapeDtypeStruct((L, 2 * D), BF16), jax.ShapeDtypeStruct((CF_K, D), F32),
                   jax.ShapeDtypeStruct((1, D), F32)],
        scratch_shapes=[pltpu.VMEM((HALO + tr, D), F32), pltpu.VMEM((tr + HALO, D), F32),
                        pltpu.VMEM((7, HALO + tr, D), F32), pltpu.VMEM((7, HALO + tr, D), F32)],
        compiler_params=_cp("arbitrary"),
    )(u, u, u, u, dc, dc, dw_w)


XA_TR = 512
XA_SCALE = XA_HD ** -0.5


def _xa_probs(qh, kh):
    s = _dot(qh, kh, 1, 1) * XA_SCALE
    p = jnp.exp(s - jnp.max(s, axis=-1, keepdims=True))
    return p / jnp.sum(p, axis=-1, keepdims=True)


def attn_fwd(q, kv, name):
    L, D = q.shape
    tr = min(XA_TR, L)

    def body(q_ref, kv_ref, o_ref):
        for hd in range(XA_HEADS):
            c = slice(hd * XA_HD, (hd + 1) * XA_HD)
            p = _xa_probs(q_ref[:, c], kv_ref[:, c])
            vh = kv_ref[:, D + hd * XA_HD:D + (hd + 1) * XA_HD]
            o_ref[:, c] = _dot(p.astype(BF16), vh).astype(BF16)

    return pl.pallas_call(
        body, name=name, grid=(L // tr,),
        in_specs=[_row_spec(tr, D), pl.BlockSpec((N_MEM, 2 * D), lambda i: (0, 0))],
        out_specs=_row_spec(tr, D), out_shape=jax.ShapeDtypeStruct((L, D), BF16),
        compiler_params=_cp("parallel"),
    )(q, kv)


def attn_bwd(q, kv, do, name):
    L, D = q.shape
    tr = min(XA_TR, L)

    def body(q_ref, kv_ref, do_ref, dq_ref, dkv_ref):
        @pl.when(pl.program_id(0) == 0)
        def _():
            dkv_ref[...] = jnp.zeros_like(dkv_ref)

        for hd in range(XA_HEADS):
            c = slice(hd * XA_HD, (hd + 1) * XA_HD)
            cv = slice(D + hd * XA_HD, D + (hd + 1) * XA_HD)
            qh, kh, vh, doh = q_ref[:, c], kv_ref[:, c], kv_ref[:, cv], do_ref[:, c]
            p = _xa_probs(qh, kh)
            dp = _dot(doh, vh, 1, 1)
            dkv_ref[:, cv] += _dot(p.astype(BF16), doh, 0, 0)
            ds = (p * (dp - jnp.sum(dp * p, axis=-1, keepdims=True)) * XA_SCALE).astype(BF16)
            dq_ref[:, c] = _dot(ds, kh).astype(BF16)
            dkv_ref[:, c] += _dot(ds, qh, 0, 0)

    return pl.pallas_call(
        body, name=name, grid=(L // tr,),
        in_specs=[_row_spec(tr, D), pl.BlockSpec((N_MEM, 2 * D), lambda i: (0, 0)), _row_spec(tr, D)],
        out_specs=[_row_spec(tr, D), pl.BlockSpec((N_MEM, 2 * D), lambda i: (0, 0))],
        out_shape=[jax.ShapeDtypeStruct((L, D), BF16), jax.ShapeDtypeStruct((N_MEM, 2 * D), F32)],
        compiler_params=_cp("arbitrary"),
    )(q, kv, do)


N_PAIRS = N_HEADS // 2
PAIRS_PER_GROUP = N_PAIRS // N_GROUPS
GN = N_GROUPS * D_STATE


def _softplus(x):
    t = jnp.exp(-jnp.abs(x))
    return jnp.maximum(x, 0.0) + jnp.where(t < 1e-4, t * (1.0 - 0.5 * t), jnp.log(1.0 + t))


def _dot3b(m, v, ca=1, cb=0):
    v0, v1, v2 = _split3(v)
    return _dot(m, v0, ca, cb) + _dot(m, v1, ca, cb) + _dot(m, v2, ca, cb)


def ssd_consts():
    h = lax.broadcasted_iota(jnp.int32, (LANE, D_INNER), 0)
    c = lax.broadcasted_iota(jnp.int32, (LANE, D_INNER), 1)
    expand = (c // HEAD_DIM == h).astype(BF16)
    r = lax.broadcasted_iota(jnp.int32, (CHUNK, CHUNK), 0)
    k = lax.broadcasted_iota(jnp.int32, (CHUNK, CHUNK), 1)
    tri = (k <= r).astype(BF16)
    return expand, tri


def _ssd_common(dtr_ref, prm_ref, e_ref, tri_ref):
    lane = lax.broadcasted_iota(jnp.int32, (CHUNK, LANE), 1)
    valid = lane < N_HEADS
    A = -jnp.exp(prm_ref[1:2, :])
    pre = dtr_ref[...] + prm_ref[0:1, :]
    dt = jnp.where(valid, _softplus(pre), 0.0)
    cs = _dot3b(tri_ref[...], dt * A)
    E = e_ref[...]
    dt_x = _dot3(dt, E)
    cs_x = _dot3(cs, E)
    csl_x = cs_x[CHUNK - 1:CHUNK, :]
    return dict(valid=valid, A=A, pre=pre, dt=dt, cs=cs, csT=cs.T, dt_x=dt_x, ecs_x=jnp.exp(cs_x),
                dend_x=jnp.exp(csl_x - cs_x), cd_x=jnp.exp(csl_x), D_x=_dot3(prm_ref[...], E)[2:3, :])


def ssd_fwd(xs, bc, dtr, zx, prm, ng, name):
    L = xs.shape[0]
    nc = L // CHUNK
    expand, tri = ssd_consts()

    def body(xs_ref, bc_ref, dtr_ref, z_ref, prm_ref, ng_ref, e_ref, tri_ref, y_ref, yn_ref, st_ref, state):
        @pl.when(pl.program_id(0) == 0)
        def _():
            state[...] = jnp.zeros_like(state)

        q = _ssd_common(dtr_ref, prm_ref, e_ref, tri_ref)
        cs, csT = q["cs"], q["csT"]
        xs_v = xs_ref[...]
        X = xs_v * q["dt_x"]
        Xb = X.astype(BF16)
        Xd = (X * q["dend_x"]).astype(BF16)
        ii = lax.broadcasted_iota(jnp.int32, (CHUNK, CHUNK), 0)
        jj = lax.broadcasted_iota(jnp.int32, (CHUNK, CHUNK), 1)
        tril = jj <= ii
        first = jj < HEAD_DIM
        for g in range(N_GROUPS):
            Bg = bc_ref[:, g * D_STATE:(g + 1) * D_STATE]
            Cg = bc_ref[:, GN + g * D_STATE:GN + (g + 1) * D_STATE]
            S = _dot(Cg, Bg, 1, 1)
            for pr in range(PAIRS_PER_GROUP):
                pair = g * PAIRS_PER_GROUP + pr
                cols = slice(pair * LANE, (pair + 1) * LANE)
                Xp = Xb[:, cols]
                ys = []
                for h in (2 * pair, 2 * pair + 1):
                    seg = cs[:, h:h + 1] - csT[h:h + 1, :]
                    M = (S * jnp.exp(jnp.where(tril, seg, NEG))).astype(BF16)
                    ys.append(_dot(M, Xp))
                prevT = state[pair]
                st_ref[0, pair] = prevT
                yoff = _dot(Cg, prevT.astype(BF16)) * q["ecs_x"][:, cols]
                y_ref[:, cols] = jnp.where(first, ys[0], ys[1]) + yoff + xs_v[:, cols] * q["D_x"][:, cols]
                state[pair] = prevT * q["cd_x"][:, cols] + _dot(Bg, Xd[:, cols], 0, 0)
        z = z_ref[...].astype(F32)
        gt = y_ref[...] * z * _sig(z)
        yn_ref[...] = (gt * _rms(gt) * ng_ref[...]).astype(BF16)

    row = lambda w: pl.BlockSpec((CHUNK, w), lambda c: (c, 0))
    const = lambda a: pl.BlockSpec(a.shape, lambda c: (0,) * a.ndim)
    return pl.pallas_call(
        body, name=name, grid=(nc,),
        in_specs=[row(D_INNER), row(2 * GN), row(LANE), row(D_INNER), const(prm), const(ng), const(expand), const(tri)],
        out_specs=[row(D_INNER), row(D_INNER), pl.BlockSpec((1, N_PAIRS, D_STATE, LANE), lambda c: (c, 0, 0, 0))],
        out_shape=[jax.ShapeDtypeStruct((L, D_INNER), F32), jax.ShapeDtypeStruct((L, D_INNER), BF16),
                   jax.ShapeDtypeStruct((nc, N_PAIRS, D_STATE, LANE), F32)],
        scratch_shapes=[pltpu.VMEM((N_PAIRS, D_STATE, LANE), F32)],
        compiler_params=_cp("arbitrary"),
    )(xs, bc, dtr, zx, prm, ng, expand, tri)


def ssd_bwd(dyn, y, zx, xs, bc, dtr, st, prm, ng, name):
    L = xs.shape[0]
    nc = L // CHUNK
    expand, tri = ssd_consts()

    def body(dyn_ref, y_ref, z_ref, xs_ref, bc_ref, dtr_ref, st_ref, prm_ref, ng_ref, e_ref, tri_ref,
             dxbc_ref, dz_ref, ddtr_ref, dng_ref, dprm_ref, dstate, g_cs, g_q, dX, g_row):
        step = pl.program_id(0)

        @pl.when(step == 0)
        def _():
            dstate[...] = jnp.zeros_like(dstate)
            dng_ref[...] = jnp.zeros_like(dng_ref)
            dprm_ref[...] = jnp.zeros_like(dprm_ref)
            g_row[...] = jnp.zeros_like(g_row)

        q = _ssd_common(dtr_ref, prm_ref, e_ref, tri_ref)
        cs, csT, E = q["cs"], q["csT"], e_ref[...]
        xs_v = xs_ref[...]
        X = xs_v * q["dt_x"]
        Xb = X.astype(BF16)
        Xd_f = X * q["dend_x"]
        Xd = Xd_f.astype(BF16)

        yv = y_ref[...]
        z = z_ref[...].astype(F32)
        sz = _sig(z)
        silu = z * sz
        gt = yv * silu
        r = _rms(gt)
        gn = gt * r
        dyn_v = dyn_ref[...]
        dng_ref[...] += jnp.sum(dyn_v * gn, axis=0, keepdims=True)
        dgn = dyn_v * ng_ref[...]
        dgt = r * (dgn - gn * jnp.mean(dgn * gn, axis=-1, keepdims=True))
        dY = dgt * silu
        dz_ref[...] = (dgt * yv * sz * (1.0 + z * (1.0 - sz))).astype(BF16)
        dYb = dY.astype(BF16)
        g_row[1:2, :] += jnp.sum(dY * xs_v, axis=0, keepdims=True)

        ii = lax.broadcasted_iota(jnp.int32, (CHUNK, CHUNK), 0)
        jj = lax.broadcasted_iota(jnp.int32, (CHUNK, CHUNK), 1)
        tril = jj <= ii
        triu = jj >= ii
        first = jj < HEAD_DIM
        lane_row = lax.broadcasted_iota(jnp.int32, (1, LANE), 1)
        sub_col = lax.broadcasted_iota(jnp.int32, (CHUNK, 1), 0)
        dcs_col = jnp.zeros((CHUNK, LANE), F32)
        dcs_rowT = jnp.zeros((LANE, CHUNK), F32)
        for g in range(N_GROUPS):
            Bg = bc_ref[:, g * D_STATE:(g + 1) * D_STATE]
            Cg = bc_ref[:, GN + g * D_STATE:GN + (g + 1) * D_STATE]
            S = _dot(Cg, Bg, 1, 1)
            ST = _dot(Bg, Cg, 1, 1)
            dS = jnp.zeros((CHUNK, CHUNK), F32)
            dCg = jnp.zeros((CHUNK, D_STATE), F32)
            dBg = jnp.zeros((CHUNK, D_STATE), F32)
            for pr in range(PAIRS_PER_GROUP):
                pair = g * PAIRS_PER_GROUP + pr
                cols = slice(pair * LANE, (pair + 1) * LANE)
                Xp = Xb[:, cols]
                dYp_f = dY[:, cols]
                dYp = dYb[:, cols]
                prevT = st_ref[0, pair]
                prevTb = prevT.astype(BF16)
                dst = dstate[pair]
                dstb = dst.astype(BF16)
                ecs_p = q["ecs_x"][:, cols]
                g_cs[:, cols] = dYp_f * (_dot(Cg, prevTb) * ecs_p)
                dWb = (dYp_f * ecs_p).astype(BF16)
                dprev = dst * q["cd_x"][:, cols] + _dot(Cg, dWb, 0, 0)
                dCg = dCg + _dot(dWb, prevTb, 1, 1)
                g_row[0:1, cols] = jnp.sum(dst * prevT, axis=0, keepdims=True)
                dXp = None
                for hh, h in enumerate((2 * pair, 2 * pair + 1)):
                    mine = first if hh == 0 else jnp.logical_not(first)
                    seg = cs[:, h:h + 1] - csT[h:h + 1, :]
                    lam = jnp.exp(jnp.where(tril, seg, NEG))
                    dM = _dot(jnp.where(mine, dYp, jnp.zeros_like(dYp)), Xp, 1, 1)
                    dS = dS + dM * lam
                    Gm = dM * (S * lam)
                    dcs_col = dcs_col + jnp.sum(Gm, axis=1, keepdims=True) * (lane_row == h).astype(F32)
                    dcs_rowT = dcs_rowT + (sub_col == h).astype(F32) * jnp.sum(Gm, axis=0, keepdims=True)
                    MT = (ST * jnp.exp(jnp.where(triu, -seg, NEG))).astype(BF16)
                    t = _dot(MT, dYp)
                    dXp = t if dXp is None else jnp.where(first, dXp, t)
                dXd = _dot(Bg, dstb)
                dBg = dBg + _dot(Xd[:, cols], dstb, 1, 1)
                g_q[:, cols] = dXd * Xd_f[:, cols]
                dX[:, cols] = dXp + dXd * q["dend_x"][:, cols]
                dstate[pair] = dprev
            dSb = dS.astype(BF16)
            dxbc_ref[:, D_INNER + g * D_STATE:D_INNER + (g + 1) * D_STATE] = dBg + _dot(dSb, Cg, 0, 0)
            dxbc_ref[:, D_INNER + GN + g * D_STATE:D_INNER + GN + (g + 1) * D_STATE] = dCg + _dot(dSb, Bg)
        dXv = dX[...]
        dxbc_ref[:, 0:D_INNER] = q["D_x"] * dY + dXv * q["dt_x"]
        r_dt = _dot3(dXv * xs_v, E, 1, 1)
        r_cs = _dot3(g_cs[...], E, 1, 1)
        r_q = _dot3(g_q[...], E, 1, 1)
        r_row = _dot3(g_row[...], E, 1, 1)
        cd = jnp.exp(cs[CHUNK - 1:CHUNK, :])
        dcs_last = jnp.sum(r_q, axis=0, keepdims=True) + r_row[0:1, :] * cd
        dcs = r_cs - r_q + dcs_col - dcs_rowT.T + jnp.where(sub_col == CHUNK - 1, dcs_last, 0.0)
        da = _dot3b(tri_ref[...], dcs, 0, 0)
        dpre = jnp.where(q["valid"], (r_dt + da * q["A"]) * _sig(q["pre"]), 0.0)
        ddtr_ref[...] = dpre
        dprm_ref[0:1, :] += jnp.sum(dpre, axis=0, keepdims=True)
        dprm_ref[1:2, :] += jnp.sum(da * q["dt"], axis=0, keepdims=True) * q["A"]
        dprm_ref[2:3, :] = r_row[1:2, :]

    rev = lambda w: pl.BlockSpec((CHUNK, w), lambda c: (nc - 1 - c, 0))
    const = lambda a: pl.BlockSpec(a.shape, lambda c: (0,) * a.ndim)
    return pl.pallas_call(
        body, name=name, grid=(nc,),
        in_specs=[rev(D_INNER), rev(D_INNER), rev(D_INNER), rev(D_INNER), rev(2 * GN), rev(LANE),
                  pl.BlockSpec((1, N_PAIRS, D_STATE, LANE), lambda c: (nc - 1 - c, 0, 0, 0)),
                  const(prm), const(ng), const(expand), const(tri)],
        out_specs=[rev(CONV_DIM), rev(D_INNER), rev(LANE),
                   pl.BlockSpec((1, D_INNER), lambda c: (0, 0)), pl.BlockSpec((8, LANE), lambda c: (0, 0))],
        out_shape=[jax.ShapeDtypeStruct((L, CONV_DIM), F32), jax.ShapeDtypeStruct((L, D_INNER), BF16),
                   jax.ShapeDtypeStruct((L, LANE), F32), jax.ShapeDtypeStruct((1, D_INNER), F32),
                   jax.ShapeDtypeStruct((8, LANE), F32)],
        scratch_shapes=[pltpu.VMEM((N_PAIRS, D_STATE, LANE), F32), pltpu.VMEM((CHUNK, D_INNER), F32),
                        pltpu.VMEM((CHUNK, D_INNER), F32), pltpu.VMEM((CHUNK, D_INNER), F32),
                        pltpu.VMEM((8, D_INNER), F32)],
        compiler_params=_cp("arbitrary"),
    )(dyn, y, zx, xs, bc, dtr, st, prm, ng, expand, tri)


def _ssd_weights(W, j):
    w_in = W["ssm_in_w"][j]
    nzx = D_INNER + CONV_DIM
    wdt = jnp.pad(w_in[:, nzx:], ((0, 0), (0, LANE - N_HEADS)))
    prm = jnp.zeros((8, LANE), F32)
    prm = prm.at[0, :N_HEADS].set(W["ssm_dt_bias"][j]).at[1, :N_HEADS].set(W["ssm_A_log"][j])
    prm = prm.at[2, :N_HEADS].set(W["ssm_D"][j])
    return dict(wdt=wdt, cw=W["ssm_conv_w"][j], cb=W["ssm_conv_b"][j].reshape(1, CONV_DIM), prm=prm,
                ng=W["ssm_norm_g"][j].reshape(1, D_INNER))


def ssd_layer_fwd(h, W, j, tag):
    p = _ssd_weights(W, j)
    zx = mm(h, W["ssm_in_w"], layer=j, b_n=D_INNER + CONV_DIM, out_dtype=BF16, name=f"{tag}_zx")
    dtr = mm(h, p["wdt"], name=f"{tag}_dt")
    xs, pre_x = ssm_conv_fwd(zx, p["cw"], p["cb"], col0=D_INNER, ncols=D_INNER, wcol0=0, out_dtype=F32,
                             name=f"{tag}_convx")
    bc, pre_bc = ssm_conv_fwd(zx, p["cw"], p["cb"], col0=2 * D_INNER, ncols=2 * GN, wcol0=D_INNER, out_dtype=BF16,
                              name=f"{tag}_convbc")
    y, yn, st = ssd_fwd(xs, bc, dtr, zx, p["prm"], p["ng"], name=f"{tag}_scan")
    f = mm(yn, W["ssm_out_w"], layer=j, out_dtype=BF16, name=f"{tag}_out")
    return f, dict(h=h, zx=zx, dtr=dtr, xs=xs, bc=bc, pre_x=pre_x, pre_bc=pre_bc, y=y, yn=yn, st=st, p=p)


def ssd_layer_bwd(df, ctx, W, GB, j, tag):
    p = ctx["p"]
    h = ctx["h"]
    dyn = mm(df, W["ssm_out_w"], layer=j, tb=True, name=f"{tag}_b_dyn")
    GB["ssm_out_w"] = mm(ctx["yn"], df, ta=True, into=(GB["ssm_out_w"], j, 0), name=f"{tag}_b_gwo")
    dxbc, dz, ddtr, dng, dprm = ssd_bwd(dyn, ctx["y"], ctx["zx"], ctx["xs"], ctx["bc"], ctx["dtr"], ctx["st"],
                                        p["prm"], p["ng"], name=f"{tag}_b_scan")
    dx1, dcw1, dcb1 = ssm_conv_bwd(ctx["zx"], ctx["pre_x"], dxbc, p["cw"], col0=D_INNER, dcol0=0, ncols=D_INNER,
                                   name=f"{tag}_b_convx")
    dx2, dcw2, dcb2 = ssm_conv_bwd(ctx["zx"], ctx["pre_bc"], dxbc, p["cw"], col0=2 * D_INNER, dcol0=D_INNER,
                                   ncols=2 * GN, name=f"{tag}_b_convbc")
    dh = mm(dz, W["ssm_in_w"], layer=j, tb=True, b_k0=0, name=f"{tag}_b_dh1")
    dh = mm(dx1, W["ssm_in_w"], layer=j, tb=True, b_k0=D_INNER, acc=dh, name=f"{tag}_b_dh2")
    dh = mm(dx2, W["ssm_in_w"], layer=j, tb=True, b_k0=2 * D_INNER, acc=dh, name=f"{tag}_b_dh3")
    dh = mm(ddtr, p["wdt"], tb=True, acc=dh, name=f"{tag}_b_dh4")
    g_in = jnp.concatenate([mm(h, dz, ta=True, out_dtype=BF16, name=f"{tag}_b_gz"),
                            mm(h, dx1, ta=True, out_dtype=BF16, name=f"{tag}_b_gx"),
                            mm(h, dx2, ta=True, out_dtype=BF16, name=f"{tag}_b_gbc"),
                            mm(h, ddtr, ta=True, out_dtype=BF16, name=f"{tag}_b_gdt")[:, :N_HEADS]], axis=1)
    return dh, dict(ssm_in_w=g_in, ssm_conv_w=jnp.concatenate([dcw1, dcw2], axis=1),
                    ssm_conv_b=jnp.concatenate([dcb1, dcb2], axis=1)[0], ssm_dt_bias=dprm[0, :N_HEADS],
                    ssm_A_log=dprm[1, :N_HEADS], ssm_D=dprm[2, :N_HEADS], ssm_norm_g=dng[0])


def cf_layer_fwd(h, W, j, tag):
    u = mm(h, W["cf_pw1_w"], layer=j, bias=W["cf_pw1_b"][j], out_dtype=BF16, name=f"{tag}_pw1")
    c, s = cf_fwd(u, W["cf_dw_w"][j], W["cf_dw_b"][j].reshape(1, -1), W["cf_ln_g"][j].reshape(1, -1),
                  W["cf_ln_b"][j].reshape(1, -1), name=f"{tag}_conv")
    f = mm(s, W["cf_pw2_w"], layer=j, bias=W["cf_pw2_b"][j], out_dtype=BF16, name=f"{tag}_pw2")
    return f, dict(h=h, u=u, c=c, s=s)


def cf_layer_bwd(df, ctx, W, GB, j, tag):
    h = ctx["h"]
    ds = mm(df, W["cf_pw2_w"], layer=j, tb=True, name=f"{tag}_b_ds")
    GB["cf_pw2_w"] = mm(ctx["s"], df, ta=True, into=(GB["cf_pw2_w"], j, 0), name=f"{tag}_b_gpw2")
    g_b2 = colsum(df, name=f"{tag}_b_gb2")
    dc, dlg, dlb = cf_bwd_ln(ctx["c"], ds, W["cf_ln_g"][j].reshape(1, -1), W["cf_ln_b"][j].reshape(1, -1),
                             name=f"{tag}_b_ln")
    du, ddw, ddb = cf_bwd_conv(ctx["u"], dc, W["cf_dw_w"][j], name=f"{tag}_b_conv")
    dh = mm(du, W["cf_pw1_w"], layer=j, tb=True, name=f"{tag}_b_dh")
    GB["cf_pw1_w"] = mm(h, du, ta=True, into=(GB["cf_pw1_w"], j, 0), name=f"{tag}_b_gpw1")
    g_b1 = colsum(du, name=f"{tag}_b_gb1")
    return dh, dict(cf_pw1_b=g_b1[0], cf_dw_w=ddw, cf_dw_b=ddb[0], cf_ln_g=dlg[0], cf_ln_b=dlb[0], cf_pw2_b=g_b2[0])


def xa_layer_fwd(h, mem, W, i, tag):
    m = norm_fwd(mem, W["xa_mem_g"][i], name=f"{tag}_memnorm")
    kv = mm(m, W["xa_kv_w"], layer=i, out_dtype=BF16, name=f"{tag}_kv")
    q = mm(h, W["xa_q_w"], layer=i, out_dtype=BF16, name=f"{tag}_q")
    o = attn_fwd(q, kv, name=f"{tag}_attn")
    f = mm(o, W["xa_o_w"], layer=i, out_dtype=BF16, name=f"{tag}_o")
    return f, dict(h=h, m=m, kv=kv, q=q, o=o)


def xa_layer_bwd(df, ctx, mem, W, GB, i, tag):
    h = ctx["h"]
    do = mm(df, W["xa_o_w"], layer=i, tb=True, out_dtype=BF16, name=f"{tag}_b_do")
    GB["xa_o_w"] = mm(ctx["o"], df, ta=True, into=(GB["xa_o_w"], i, 0), name=f"{tag}_b_go")
    dq, dkv = attn_bwd(ctx["q"], ctx["kv"], do, name=f"{tag}_b_attn")
    dh = mm(dq, W["xa_q_w"], layer=i, tb=True, name=f"{tag}_b_dh")
    GB["xa_q_w"] = mm(h, dq, ta=True, into=(GB["xa_q_w"], i, 0), name=f"{tag}_b_gq")
    GB["xa_kv_w"] = mm(ctx["m"], dkv, ta=True, into=(GB["xa_kv_w"], i, 0), name=f"{tag}_b_gkv")
    dm = mm(dkv, W["xa_kv_w"], layer=i, tb=True, name=f"{tag}_b_dm")
    g_mg = norm_dg(mem, dm, name=f"{tag}_b_gmem")
    return dh, dict(xa_mem_g=g_mg[0])


def ffn_layer_fwd(h, W, i, tag):
    cw, cb = W["ffn_conv_w"][i], W["ffn_conv_b"][i].reshape(1, -1)
    u = mm(h, W["ffn_in_w"], layer=i, out_dtype=BF16, name=f"{tag}_in")
    act, c = ffn_act_fwd(u, cw, cb, name=f"{tag}_act")
    f = mm(act, W["ffn_out_w"], layer=i, out_dtype=BF16, name=f"{tag}_out")
    return f, dict(h=h, u=u, c=c, act=act)


def ffn_layer_bwd(df, ctx, W, GB, i, tag):
    h = ctx["h"]
    dact = mm(df, W["ffn_out_w"], layer=i, tb=True, out_dtype=BF16, name=f"{tag}_b_dact")
    GB["ffn_out_w"] = mm(ctx["act"], df, ta=True, into=(GB["ffn_out_w"], i, 0), name=f"{tag}_b_gout")
    du, dcw, dcb = ffn_act_bwd(ctx["u"], ctx["c"], dact, W["ffn_conv_w"][i], name=f"{tag}_b_act")
    dh = None
    for half in range(2):
        dh = mm(du, W["ffn_in_w"], a_idx=half, layer=i, tb=True, b_k0=half * D_FF, acc=dh, name=f"{tag}_b_dh{half}")
        GB["ffn_in_w"] = mm(h, du, ta=True, layer=half, into=(GB["ffn_in_w"], i, half * D_FF), name=f"{tag}_b_gin{half}")
    cat = lambda a: jnp.concatenate([a[0], a[1]], axis=-1)
    return dh, dict(ffn_conv_w=cat(dcw), ffn_conv_b=cat(dcb)[0])


def _sublayer_weights(i, s):
    if s == 0:
        return [("ssm_in_w", i // 2), ("ssm_out_w", i // 2)] if i % 2 == 0 else [("cf_pw1_w", i // 2), ("cf_pw2_w", i // 2)]
    return [(n, i) for n in (("xa_q_w", "xa_kv_w", "xa_o_w") if s == 1 else ("ffn_in_w", "ffn_out_w"))]


def local_step(x, mem, target, W, fetch=None, layer_done=None):
    subs = [(i, s) for i in range(DEPTH) for s in range(3)]
    ng = W["norm_g"]

    def fwd(i, s, h):
        tag = f"l{i}s{s}"
        if s == 0:
            return ssd_layer_fwd(h, W, i // 2, tag) if i % 2 == 0 else cf_layer_fwd(h, W, i // 2, tag)
        if s == 1:
            return xa_layer_fwd(h, mem, W, i, tag)
        return ffn_layer_fwd(h, W, i, tag)

    GB = {}

    def bwd(i, s, df, ctx):
        tag = f"l{i}s{s}"
        if s == 0:
            return (ssd_layer_bwd if i % 2 == 0 else cf_layer_bwd)(df, ctx, W, GB, i // 2, tag)
        if s == 1:
            return xa_layer_bwd(df, ctx, mem, W, GB, i, tag)
        return ffn_layer_bwd(df, ctx, W, GB, i, tag)

    h = norm_fwd(x, ng[0, 0], name="norm0")
    saved = []
    dxp = loss = None
    for k, (i, s) in enumerate(subs):
        if fetch is not None:
            fetch(i, s, x)
        f, ctx = fwd(i, s, h)
        saved.append((x, f, ctx))
        if k + 1 < len(subs):
            ni, ns = subs[k + 1]
            x, h = bnd_fwd(x, f, ng[i, 2 * s + 1], ng[ni, 2 * ns], name=f"bnd{k}")
        else:
            dxp, loss = final_fwd(x, f, ng[i, 2 * s + 1], target, name="final")

    for n in BIG:
        if n != "ssm_in_w":
            GB[n] = jnp.zeros((len(W[n]), *W[n][0].shape), BF16)
    grads = {}

    def put(name, idx, val):
        grads.setdefault(name, {})[idx] = val

    i, s = subs[-1]
    top = bnd_bwd(dxp, post=(saved[-1][1], ng[i, 2 * s + 1]), name="bbnd_top")
    put("norm_g", (i, 2 * s + 1), top["dgpost"][0])
    df = top["df"]
    for k in range(len(subs) - 1, -1, -1):
        i, s = subs[k]
        xk, _, ctx = saved[k]
        dh, gw = bwd(i, s, df, ctx)
        for name, val in gw.items():
            put(name, i // 2 if name.startswith(("ssm_", "cf_")) else i, val)
        dep = None
        if layer_done is not None:
            dep = layer_done(i, s, GB, grads["ssm_in_w"].pop(i // 2) if (s == 0 and i % 2 == 0) else None)
        if k > 0:
            pi, ps = subs[k - 1]
            r = bnd_bwd(dxp, pre=(xk, ng[i, 2 * s], dh), post=(saved[k - 1][1], ng[pi, 2 * ps + 1]), dep=dep,
                        name=f"bbnd{k}")
            put("norm_g", (pi, 2 * ps + 1), r["dgpost"][0])
            df = r["df"]
        else:
            r = bnd_bwd(dxp, pre=(xk, ng[i, 2 * s], dh), dep=dep, name="bbnd0")
        put("norm_g", (i, 2 * s), r["dgpre"][0])
        dxp = r["dx"]

    out = {} if layer_done is not None else dict(GB)
    for name, d in grads.items():
        if name == "norm_g":
            out[name] = jnp.stack([jnp.stack([d[(i, t)] for t in range(6)]) for i in range(DEPTH)])
        elif d:
            out[name] = jnp.stack([d[j] for j in sorted(d)])
    return loss, dxp, out


ANY = pl.BlockSpec(memory_space=pl.ANY)


def _pos():
    return lax.axis_index("x"), lax.axis_index("y"), lax.axis_index("c")


def all_gather(shard, name):
    R, C = shard.shape

    def body(x_ref, out_ref, send_sems, recv_sems, local_sem):
        x, y, c = _pos()
        me, sibling = (x, y, c), (x, y, 1 - c)
        chips = [(1 - x, y), (x, 1 - y), (1 - x, 1 - y)]

        def slot(px, py, pc):
            return out_ref.at[4 * px + 2 * py + pc]

        def copy(k, block, to, src=None):
            return pltpu.make_async_remote_copy(
                src_ref=slot(*block) if src is None else src, dst_ref=slot(*block),
                send_sem=send_sems.at[k], recv_sem=recv_sems.at[k], device_id=to, device_id_type=MESH)

        mine = pltpu.make_async_copy(x_ref, slot(*me), local_sem)
        mine.start()
        first = [copy(0, me, sibling, src=x_ref)]
        first += [copy(1 + j, me, (*chip, c), src=x_ref) for j, chip in enumerate(chips)]
        for cp in first:
            cp.start()
        passed = [copy(4 + j, (*chip, c), sibling) for j, chip in enumerate(chips)]
        for j, chip in enumerate(chips):
            copy(1 + j, (*chip, c), me).wait_recv()
            passed[j].start()
        copy(0, sibling, me).wait_recv()
        for j, chip in enumerate(chips):
            copy(4 + j, (*chip, 1 - c), me).wait_recv()
        for cp in first + passed:
            cp.wait_send()
        mine.wait()

    return pl.pallas_call(
        body, name=name, out_shape=jax.ShapeDtypeStruct((N_DEV, R, C), shard.dtype),
        in_specs=[ANY], out_specs=ANY,
        scratch_shapes=[pltpu.SemaphoreType.DMA((7,)), pltpu.SemaphoreType.DMA((7,)), pltpu.SemaphoreType.DMA(())],
    )(shard)


def _win(ref, kind, k, a, b):
    if kind == "lead":
        return ref.at[k]
    if kind == "row":
        return ref.at[:, pl.ds(pl.multiple_of(k * a, 16), a), :]
    return ref.at[:, :, pl.ds(pl.multiple_of(k * b, LANE), b)]


def _full_shape(shard_shape, kind):
    n, a, b = shard_shape
    return {"lead": (N_DEV, n, a, b), "row": (n, N_DEV * a, b), "col": (n, a, N_DEV * b)}[kind]


HBM = pl.BlockSpec(memory_space=pltpu.HBM)
SEMS = pl.BlockSpec(memory_space=pltpu.SEMAPHORE)
DATAFLOW = pltpu.SideEffectType.DATAFLOW_SIDE_EFFECTING
N_PEER = N_DEV - 1


def _in_hbm(a):
    return pltpu.with_memory_space_constraint(a, pltpu.HBM)


def _peer(x, y, c, r):
    return ((1 - x) if r & 4 else x, (1 - y) if r & 2 else y, (1 - c) if r & 1 else c)


def _win2(ref, kind, k, a, b):
    if kind == "lead":
        return ref.at[k]
    if kind == "row":
        return ref.at[pl.ds(pl.multiple_of(k * a, 16), a), :]
    return ref.at[:, pl.ds(pl.multiple_of(k * b, LANE), b)]


def _zone_shape(kind, a, b):
    return {"lead": (N_DEV, a, b), "row": (N_DEV * a, b), "col": (a, N_DEV * b)}[kind]


def gather_start(shards, items, after, name):
    ns, nz, na = len(shards), len(items), len(after)
    zones = [lax.empty(_zone_shape(kind, a, b), shards[w].dtype) for w, l, kind, a, b in items]

    def body(*refs):
        x_refs = refs[:ns]
        send_sems, recv_sems, local_sems = refs[ns + nz + na:ns + nz + na + 3]
        z_refs = refs[ns + nz + na + 3 + ns:ns + nz + na + 3 + ns + nz]
        token = refs[-1]
        x, y, c = _pos()
        me = 4 * x + 2 * y + c
        for t, (w, l, kind, a, b) in enumerate(items):
            mine = _win2(z_refs[t], kind, me, a, b)
            pltpu.make_async_copy(x_refs[w].at[l], mine, local_sems.at[t]).start()
            for r in range(1, N_DEV):
                pltpu.make_async_remote_copy(
                    src_ref=x_refs[w].at[l], dst_ref=mine,
                    send_sem=send_sems.at[N_PEER * t + r - 1], recv_sem=recv_sems.at[N_PEER * t + r - 1],
                    device_id=_peer(x, y, c, r), device_id_type=MESH).start()
        token[...] = jnp.zeros_like(token)

    n_sem = N_PEER * nz
    outs = pl.pallas_call(
        body, name=name,
        out_shape=(pltpu.SemaphoreType.DMA((n_sem,)), pltpu.SemaphoreType.DMA((n_sem,)), pltpu.SemaphoreType.DMA((nz,)),
                   *[pltpu.HBM(s.shape, s.dtype) for s in shards], *[pltpu.HBM(z.shape, z.dtype) for z in zones],
                   jax.ShapeDtypeStruct((8, LANE), F32)),
        in_specs=[HBM] * (ns + nz) + [pl.BlockSpec(memory_space=pl.ANY)] * na,
        out_specs=(SEMS, SEMS, SEMS, *[HBM] * (ns + nz), pl.BlockSpec(memory_space=pltpu.VMEM)),
        input_output_aliases={i: 3 + i for i in range(ns + nz)},
        compiler_params=pltpu.CompilerParams(has_side_effects=DATAFLOW),
    )(*[_in_hbm(s) for s in shards], *[_in_hbm(z) for z in zones], *after)
    return outs[:3], list(outs[3:3 + ns]), list(outs[3 + ns:3 + ns + nz]), outs[-1]


def gather_wait(zones, idx, items, sems, after, keep, name):
    nz, nk = len(zones), len(keep)

    def body(*refs):
        z_refs = refs[:nz]
        send_sems, recv_sems, local_sems = refs[nz:nz + 3]
        x, y, c = _pos()
        me = 4 * x + 2 * y + c
        for z_ref, t in zip(z_refs, idx):
            w, l, kind, a, b = items[t]
            mine = _win2(z_ref, kind, me, a, b)
            pltpu.make_async_copy(mine, mine, local_sems.at[t]).wait()
            for r in range(1, N_DEV):
                peer = _peer(x, y, c, r)
                cp = pltpu.make_async_remote_copy(
                    src_ref=mine, dst_ref=_win2(z_ref, kind, 4 * peer[0] + 2 * peer[1] + peer[2], a, b),
                    send_sem=send_sems.at[N_PEER * t + r - 1], recv_sem=recv_sems.at[N_PEER * t + r - 1],
                    device_id=peer, device_id_type=MESH)
                cp.wait_send()
                cp.wait_recv()

    outs = pl.pallas_call(
        body, name=name, out_shape=tuple(pltpu.HBM(z.shape, z.dtype) for z in zones),
        in_specs=[HBM] * nz + [SEMS] * 3 + [pl.BlockSpec(memory_space=pl.ANY)] * (1 + nk),
        out_specs=tuple([HBM] * nz), input_output_aliases={i: i for i in range(nz)},
        compiler_params=pltpu.CompilerParams(has_side_effects=DATAFLOW),
    )(*zones, *sems, after, *keep)
    return list(outs)


def gather_now(shards, kinds, name):
    nw = len(shards)
    geo = [s.shape[1:] for s in shards]

    def body(*refs):
        x_refs, o_refs = refs[:nw], refs[nw:2 * nw]
        send_sems, recv_sems, local_sems = refs[2 * nw:]
        x, y, c = _pos()
        me, sibling = (x, y, c), (x, y, 1 - c)
        chips = [(1 - x, y), (x, 1 - y), (1 - x, 1 - y)]

        def slot(w, px, py, pc):
            return _win(o_refs[w], kinds[w], 4 * px + 2 * py + pc, *geo[w])

        def copy(w, k, block, to, src=None):
            return pltpu.make_async_remote_copy(
                src_ref=slot(w, *block) if src is None else src, dst_ref=slot(w, *block),
                send_sem=send_sems.at[7 * w + k], recv_sem=recv_sems.at[7 * w + k], device_id=to, device_id_type=MESH)

        mine = [pltpu.make_async_copy(x_refs[w], slot(w, *me), local_sems.at[w]) for w in range(nw)]
        for cp in mine:
            cp.start()
        first = []
        for w in range(nw):
            first.append(copy(w, 0, me, sibling, src=x_refs[w]))
            first += [copy(w, 1 + j, me, (*chip, c), src=x_refs[w]) for j, chip in enumerate(chips)]
        for cp in first:
            cp.start()
        passed = []
        for w in range(nw):
            for j, chip in enumerate(chips):
                copy(w, 1 + j, (*chip, c), me).wait_recv()
                cp = copy(w, 4 + j, (*chip, c), sibling)
                cp.start()
                passed.append(cp)
        for w in range(nw):
            copy(w, 0, sibling, me).wait_recv()
            for j, chip in enumerate(chips):
                copy(w, 4 + j, (*chip, 1 - c), me).wait_recv()
        for cp in first + passed:
            cp.wait_send()
        for cp in mine:
            cp.wait()

    return pl.pallas_call(
        body, name=name,
        out_shape=[jax.ShapeDtypeStruct(_full_shape(s.shape, k), s.dtype) for s, k in zip(shards, kinds)],
        in_specs=[ANY] * nw, out_specs=[ANY] * nw,
        scratch_shapes=[pltpu.SemaphoreType.DMA((7 * nw,)), pltpu.SemaphoreType.DMA((7 * nw,)),
                        pltpu.SemaphoreType.DMA((nw,))],
    )(*shards)


def _src_win(ref, l, kind, k, a, b):
    return _win2(ref if l is None else ref.at[l], kind, k, a, b)


def rs_start(srcs, items, name):
    ns, nz = len(srcs), len(items)
    zones = [lax.empty((N_PEER, a, b), srcs[w].dtype) for w, l, kind, a, b in items]

    def body(*refs):
        s_refs = refs[:ns]
        send_sems, recv_sems = refs[ns + nz], refs[ns + nz + 1]
        z_refs = refs[ns + nz + 2 + ns:ns + nz + 2 + ns + nz]
        token = refs[-1]
        x, y, c = _pos()
        for t, (w, l, kind, a, b) in enumerate(items):
            for r in range(1, N_DEV):
                peer = _peer(x, y, c, r)
                pltpu.make_async_remote_copy(
                    src_ref=_src_win(s_refs[w], l, kind, 4 * peer[0] + 2 * peer[1] + peer[2], a, b),
                    dst_ref=z_refs[t].at[r - 1],
                    send_sem=send_sems.at[N_PEER * t + r - 1], recv_sem=recv_sems.at[N_PEER * t + r - 1],
                    device_id=peer, device_id_type=MESH).start()
        token[...] = jnp.zeros_like(token)

    n_sem = N_PEER * nz
    outs = pl.pallas_call(
        body, name=name,
        out_shape=(pltpu.SemaphoreType.DMA((n_sem,)), pltpu.SemaphoreType.DMA((n_sem,)),
                   *[pltpu.HBM(s.shape, s.dtype) for s in srcs], *[pltpu.HBM(z.shape, z.dtype) for z in zones],
                   jax.ShapeDtypeStruct((8, LANE), F32)),
        in_specs=[HBM] * (ns + nz), out_specs=(SEMS, SEMS, *[HBM] * (ns + nz), pl.BlockSpec(memory_space=pltpu.VMEM)),
        input_output_aliases={i: 2 + i for i in range(ns + nz)},
        compiler_params=pltpu.CompilerParams(has_side_effects=DATAFLOW),
    )(*[_in_hbm(s) for s in srcs], *[_in_hbm(z) for z in zones])
    return outs[:2], list(outs[2:2 + ns]), list(outs[2 + ns:2 + ns + nz]), outs[-1]


def rs_wait(zones, items, sems, after, keep, name):
    nz, nk = len(zones), len(keep)

    def body(*refs):
        z_refs = refs[:nz]
        send_sems, recv_sems = refs[nz], refs[nz + 1]
        x, y, c = _pos()
        for t, z_ref in enumerate(z_refs):
            for r in range(1, N_DEV):
                cp = pltpu.make_async_remote_copy(
                    src_ref=z_ref.at[r - 1], dst_ref=z_ref.at[r - 1],
                    send_sem=send_sems.at[N_PEER * t + r - 1], recv_sem=recv_sems.at[N_PEER * t + r - 1],
                    device_id=_peer(x, y, c, r), device_id_type=MESH)
                cp.wait_send()
                cp.wait_recv()

    outs = pl.pallas_call(
        body, name=name, out_shape=tuple(pltpu.HBM(z.shape, z.dtype) for z in zones),
        in_specs=[HBM] * nz + [SEMS] * 2 + [pl.BlockSpec(memory_space=pl.ANY)] * (1 + nk),
        out_specs=tuple([HBM] * nz), input_output_aliases={i: i for i in range(nz)},
        compiler_params=pltpu.CompilerParams(has_side_effects=DATAFLOW),
    )(*zones, *sems, after, *keep)
    return list(outs)


def adam_rs(w, m, v, l, own, kind, zone, outs, name):
    n, a, b = w.shape
    ta = max(t for t in range(16, min(a, 256) + 1, 16) if a % t == 0)
    per = a // ta
    me = (4 * lax.axis_index("x") + 2 * lax.axis_index("y") + lax.axis_index("c")).astype(jnp.int32).reshape(1)

    def body(me_ref, w_ref, m_ref, v_ref, own_ref, z_ref, i0, i1, i2, i3, g_ref, d_ref, m2_ref, v2_ref):
        gv = own_ref[...].astype(F32)
        for k in range(N_PEER):
            gv = gv + z_ref[k].astype(F32)
        m2 = ADAM_B1 * m_ref[...] + (1.0 - ADAM_B1) * gv
        v2 = ADAM_B2 * v_ref[...] + (1.0 - ADAM_B2) * (gv * gv)
        m_hat = m2 / (1.0 - ADAM_B1 ** ADAM_STEP)
        v_hat = v2 / (1.0 - ADAM_B2 ** ADAM_STEP)
        g_ref[...] = gv
        d_ref[...] = -ADAM_LR * (m_hat / (jnp.sqrt(v_hat) + ADAM_EPS) + ADAM_WD * w_ref[...])
        m2_ref[...] = m2
        v2_ref[...] = v2

    spec = pl.BlockSpec((None, ta, b), lambda r, me_ref: (l, r, 0))
    if kind == "lead":
        own_spec = pl.BlockSpec((None, ta, b), lambda r, me_ref: (me_ref[0], r, 0))
    elif kind == "row":
        own_spec = pl.BlockSpec((None, ta, b), lambda r, me_ref: (l, me_ref[0] * per + r, 0))
    else:
        own_spec = pl.BlockSpec((None, ta, b), lambda r, me_ref: (l, r, me_ref[0]))
    return pl.pallas_call(
        body, name=name, out_shape=[jax.ShapeDtypeStruct((n, a, b), F32)] * 4,
        grid_spec=pltpu.PrefetchScalarGridSpec(
            num_scalar_prefetch=1, grid=(per,),
            in_specs=[spec] * 3 + [own_spec, pl.BlockSpec((N_PEER, ta, b), lambda r, me_ref: (0, r, 0))] + [ANY] * 4,
            out_specs=[spec] * 4),
        input_output_aliases={6 + k: k for k in range(4)},
        compiler_params=_cp("parallel"),
    )(me, w, m, v, own, zone, *outs)


def small_exchange(sh, rep, name):
    _, Rs, C = sh.shape
    Rr = rep.shape[0]

    def body(sh_ref, rep_ref, sh_out, rep_out, send_sems, recv_sems, local_sems):
        x, y, c = _pos()
        me = 4 * x + 2 * y + c
        l1 = pltpu.make_async_copy(sh_ref.at[me], sh_out.at[me], local_sems.at[0])
        l2 = pltpu.make_async_copy(rep_ref, rep_out.at[me], local_sems.at[1])
        l1.start()
        l2.start()

        def flip(v, bit):
            return 1 - v if bit else v

        sends, recvs = [], []
        for r in range(1, N_DEV):
            peer = (flip(x, r & 4), flip(y, r & 2), flip(c, r & 1))
            pid = 4 * peer[0] + 2 * peer[1] + peer[2]
            k = 2 * (r - 1)
            mk = lambda src, dst, kk: pltpu.make_async_remote_copy(
                src_ref=src, dst_ref=dst, send_sem=send_sems.at[kk], recv_sem=recv_sems.at[kk],
                device_id=peer, device_id_type=MESH)
            sends += [mk(sh_ref.at[pid], sh_out.at[me], k), mk(rep_ref, rep_out.at[me], k + 1)]
            recvs += [mk(sh_ref.at[me], sh_out.at[pid], k), mk(rep_ref, rep_out.at[pid], k + 1)]
        for cp in sends:
            cp.start()
        for cp in recvs:
            cp.wait_recv()
        for cp in sends:
            cp.wait_send()
        l1.wait()
        l2.wait()

    n = 2 * (N_DEV - 1)
    return pl.pallas_call(
        body, name=name,
        out_shape=[jax.ShapeDtypeStruct((N_DEV, Rs, C), sh.dtype), jax.ShapeDtypeStruct((N_DEV, *rep.shape), rep.dtype)],
        in_specs=[ANY, ANY], out_specs=[ANY, ANY],
        scratch_shapes=[pltpu.SemaphoreType.DMA((n,)), pltpu.SemaphoreType.DMA((n,)), pltpu.SemaphoreType.DMA((2,))],
    )(sh, rep)


def adam_slots(w, m, v, slots, name):
    S, n, a, b = slots.shape
    ta = max(t for t in range(16, min(a, 512) + 1, 8)
             if a % t == 0 and t * S * b * slots.dtype.itemsize <= 4 * 1024 * 1024)

    def body(w_ref, m_ref, v_ref, s_ref, g_ref, d_ref, m2_ref, v2_ref):
        gv = s_ref[0].astype(F32)
        for k in range(1, S):
            gv = gv + s_ref[k].astype(F32)
        m2 = ADAM_B1 * m_ref[...] + (1.0 - ADAM_B1) * gv
        v2 = ADAM_B2 * v_ref[...] + (1.0 - ADAM_B2) * (gv * gv)
        m_hat = m2 / (1.0 - ADAM_B1 ** ADAM_STEP)
        v_hat = v2 / (1.0 - ADAM_B2 ** ADAM_STEP)
        g_ref[...] = gv
        d_ref[...] = -ADAM_LR * (m_hat / (jnp.sqrt(v_hat) + ADAM_EPS) + ADAM_WD * w_ref[...])
        m2_ref[...] = m2
        v2_ref[...] = v2

    spec = pl.BlockSpec((None, ta, b), lambda l, r: (l, r, 0))
    return pl.pallas_call(
        body, name=name, grid=(n, a // ta),
        in_specs=[spec] * 3 + [pl.BlockSpec((S, None, ta, b), lambda l, r: (0, l, r, 0))], out_specs=[spec] * 4,
        out_shape=[jax.ShapeDtypeStruct((n, a, b), F32)] * 4, compiler_params=_cp("parallel", "parallel"),
    )(w, m, v, slots)


WEIGHTS = ["norm_g", "ssm_in_w", "ssm_conv_w", "ssm_conv_b", "ssm_dt_bias", "ssm_A_log", "ssm_D", "ssm_norm_g",
           "ssm_out_w", "cf_pw1_w", "cf_pw1_b", "cf_dw_w", "cf_dw_b", "cf_ln_g", "cf_ln_b", "cf_pw2_w", "cf_pw2_b",
           "xa_mem_g", "xa_q_w", "xa_kv_w", "xa_o_w", "ffn_in_w", "ffn_conv_w", "ffn_conv_b", "ffn_out_w"]
ARGS = ["x", "mem"] + WEIGHTS + ["loss_target"] + ["m_" + n for n in WEIGHTS] + ["v_" + n for n in WEIGHTS]
BIG = {"ssm_in_w": "col", "ssm_out_w": "row", "cf_pw1_w": "col", "cf_pw2_w": "row", "xa_q_w": "row",
       "xa_kv_w": "col", "xa_o_w": "row", "ffn_in_w": "col", "ffn_out_w": "row"}
SMALL = ["norm_g", "ssm_conv_w", "cf_pw1_b", "cf_dw_w", "cf_dw_b", "cf_ln_g", "cf_ln_b", "cf_pw2_b", "ffn_conv_w"]
REP = ["ssm_conv_b", "ssm_dt_bias", "ssm_A_log", "ssm_D", "ssm_norm_g", "xa_mem_g", "ffn_conv_b"]
SMALL_W = 768
REP_W = 512


def _r8(n):
    return -(-n // 8) * 8


def _stack2d(arrs, wid):
    parts = []
    for a in arrs:
        r, c = a.shape[-2:]
        parts.append(jnp.pad(a, [(0, 0)] * (a.ndim - 2) + [(0, _r8(r) - r), (0, wid - c)]))
    return jnp.concatenate(parts, axis=-2)


def _unstack2d(buf, shapes2d):
    out, o = [], 0
    for r, c in shapes2d:
        out.append(buf[..., o:o + r, :c])
        o += _r8(r)
    return out


def _gathered_to_full(g):
    lead = g.shape[1:-1]
    return jnp.moveaxis(g, 0, -2).reshape(*lead, N_DEV * g.shape[-1])


def _full_to_slots(w):
    lead = w.shape[:-1]
    return jnp.moveaxis(w.reshape(*lead, N_DEV, w.shape[-1] // N_DEV), -2, 0)


def kernel(x, mem, norm_g, ssm_in_w, ssm_conv_w, ssm_conv_b, ssm_dt_bias, ssm_A_log, ssm_D, ssm_norm_g, ssm_out_w, cf_pw1_w, cf_pw1_b, cf_dw_w, cf_dw_b, cf_ln_g, cf_ln_b, cf_pw2_w, cf_pw2_b, xa_mem_g, xa_q_w, xa_kv_w, xa_o_w, ffn_in_w, ffn_conv_w, ffn_conv_b, ffn_out_w, loss_target, m_norm_g, m_ssm_in_w, m_ssm_conv_w, m_ssm_conv_b, m_ssm_dt_bias, m_ssm_A_log, m_ssm_D, m_ssm_norm_g, m_ssm_out_w, m_cf_pw1_w, m_cf_pw1_b, m_cf_dw_w, m_cf_dw_b, m_cf_ln_g, m_cf_ln_b, m_cf_pw2_w, m_cf_pw2_b, m_xa_mem_g, m_xa_q_w, m_xa_kv_w, m_xa_o_w, m_ffn_in_w, m_ffn_conv_w, m_ffn_conv_b, m_ffn_out_w, v_norm_g, v_ssm_in_w, v_ssm_conv_w, v_ssm_conv_b, v_ssm_dt_bias, v_ssm_A_log, v_ssm_D, v_ssm_norm_g, v_ssm_out_w, v_cf_pw1_w, v_cf_pw1_b, v_cf_dw_w, v_cf_dw_b, v_cf_ln_g, v_cf_ln_b, v_cf_pw2_w, v_cf_pw2_b, v_xa_mem_g, v_xa_q_w, v_xa_kv_w, v_xa_o_w, v_ffn_in_w, v_ffn_conv_w, v_ffn_conv_b, v_ffn_out_w):
    return _step(x, mem, norm_g, ssm_in_w, ssm_conv_w, ssm_conv_b, ssm_dt_bias, ssm_A_log, ssm_D, ssm_norm_g, ssm_out_w, cf_pw1_w, cf_pw1_b, cf_dw_w, cf_dw_b, cf_ln_g, cf_ln_b, cf_pw2_w, cf_pw2_b, xa_mem_g, xa_q_w, xa_kv_w, xa_o_w, ffn_in_w, ffn_conv_w, ffn_conv_b, ffn_out_w, loss_target, m_norm_g, m_ssm_in_w, m_ssm_conv_w, m_ssm_conv_b, m_ssm_dt_bias, m_ssm_A_log, m_ssm_D, m_ssm_norm_g, m_ssm_out_w, m_cf_pw1_w, m_cf_pw1_b, m_cf_dw_w, m_cf_dw_b, m_cf_ln_g, m_cf_ln_b, m_cf_pw2_w, m_cf_pw2_b, m_xa_mem_g, m_xa_q_w, m_xa_kv_w, m_xa_o_w, m_ffn_in_w, m_ffn_conv_w, m_ffn_conv_b, m_ffn_out_w, v_norm_g, v_ssm_in_w, v_ssm_conv_w, v_ssm_conv_b, v_ssm_dt_bias, v_ssm_A_log, v_ssm_D, v_ssm_norm_g, v_ssm_out_w, v_cf_pw1_w, v_cf_pw1_b, v_cf_dw_w, v_cf_dw_b, v_cf_ln_g, v_cf_ln_b, v_cf_pw2_w, v_cf_pw2_b, v_xa_mem_g, v_xa_q_w, v_xa_kv_w, v_xa_o_w, v_ffn_in_w, v_ffn_conv_w, v_ffn_conv_b, v_ffn_out_w)


def _step(*args):
    A = dict(zip(ARGS, args, strict=True))
    x, mem, target = A["x"][0], A["mem"][0], A["loss_target"][0]

    big = list(BIG)
    geo = [A[n].shape for n in big]
    kinds = ["row" if BIG[n] == "row" else ("col" if A[n].shape[-1] % LANE == 0 else "lead") for n in big]
    W = {n: A[n] for n in REP}
    small2d = [(A[n].size // A[n].shape[-1], A[n].shape[-1]) for n in SMALL]
    rep2d = [(A[n].size // REP_W, REP_W) if A[n].shape[-1] % REP_W == 0 else A[n].shape for n in REP] + [(1, 1)]
    stack_small = lambda pre: _stack2d([A[pre + n].reshape(rc) for n, rc in zip(SMALL, small2d)], SMALL_W)
    stack_rep = lambda pre: _stack2d([A[pre + n].reshape(rc) for n, rc in zip(REP, rep2d)] + [jnp.zeros((1, 1), F32)],
                                     REP_W)
    small_g = all_gather(stack_small(""), name="gather_small")
    for n, g in zip(SMALL, _unstack2d(small_g, small2d)):
        W[n] = _gathered_to_full(g.reshape(N_DEV, *A[n].shape))

    shards = [A[n].astype(BF16) for n in big]
    for n in big:
        W[n] = [None] * A[n].shape[0]
    first = _sublayer_weights(0, 0)
    got0 = gather_now([shards[big.index(n)][l:l + 1] for n, l in first], [kinds[big.index(n)] for n, l in first],
                      name="gather_first")
    for (n, l), g in zip(first, got0):
        W[n][l] = _gathered_to_full(g)[0] if kinds[big.index(n)] == "lead" else g[0]
    items, sub_items = [], {}
    for i in range(DEPTH):
        for s in range(3):
            sub_items[i, s] = []
            for n, l in _sublayer_weights(i, s) if (i, s) != (0, 0) else []:
                w = big.index(n)
                sub_items[i, s].append(len(items))
                items.append((w, l, kinds[w], *geo[w][1:]))
    sems, shards_thru, zones, token = gather_start(shards, items, [small_g, got0[0]], name="gather_start")
    x = x + token[0, 0]

    def fetch(i, s, x_in):
        ids = sub_items[i, s]
        if not ids:
            return
        got = gather_wait([zones[t] for t in ids], ids, items, sems, x_in, shards_thru if (i, s) == (DEPTH - 1, 2) else [],
                          name=f"gather_wait{i}{s}")
        for t, z in zip(ids, got):
            w, l, kind = items[t][:3]
            W[big[w]][l] = _gathered_to_full(z) if kind == "lead" else z

    sent = []
    final = {}

    def layer_done(i, s, GB, g_in):
        srcs, its = [], []
        for n, l in _sublayer_weights(i, s):
            w = big.index(n)
            if kinds[w] == "lead":
                srcs.append(_full_to_slots(g_in if n == "ssm_in_w" else GB[n][l]))
                its.append((len(srcs) - 1, None, "lead", *geo[w][1:], n, l))
            else:
                srcs.append(GB[n])
                its.append((len(srcs) - 1, l, kinds[w], *geo[w][1:], n, l))
        sems_i, thru, zones_i, token_i = rs_start(srcs, [it[:5] for it in its], name=f"rs_start{i}{s}")
        for it, s in zip(its, thru):
            if it[2] != "lead":
                GB[it[5]] = s
        sent.append((its, sems_i, [s for it, s in zip(its, thru) if it[2] == "lead"], zones_i))
        final["GB"] = GB
        return token_i

    loss, grad_x, G = local_step(x, mem, target, W, fetch, layer_done)

    sh = _stack2d([_full_to_slots(G[n]).reshape(N_DEV, *rc) for n, rc in zip(SMALL, small2d)], SMALL_W)
    rep = _stack2d([G[n].reshape(rc) for n, rc in zip(REP, rep2d)] + [loss[:, :1]], REP_W)
    sh_got, rep_got = small_exchange(sh, rep, name="small_exchange")

    res = {}
    GBf = final["GB"]
    bufs = {n: [lax.empty(A[n].shape, F32) for _ in range(4)] for n in big}
    for i, (its, sems_i, lead_srcs, zones_i) in enumerate(sent):
        keep = lead_srcs + [GBf[it[5]] for it in its if it[2] != "lead"]
        zones_i = rs_wait(zones_i, [it[:5] for it in its], sems_i, sh_got, keep, name=f"rs_wait{i}")
        lead_it = iter(lead_srcs)
        for it, z in zip(its, zones_i):
            n, l = it[5], it[6]
            own = next(lead_it) if it[2] == "lead" else GBf[n]
            bufs[n] = adam_rs(A[n], A["m_" + n], A["v_" + n], l, own, it[2], z, bufs[n], name=f"adam_{n}{l}")
    for n in big:
        res[n] = tuple(bufs[n])
    for names, shapes2d, stack, slots, tag in ((SMALL, small2d, stack_small, sh_got, "small"),
                                               (REP, rep2d, stack_rep, rep_got, "rep")):
        outs4 = adam_slots(stack("")[None], stack("m_")[None], stack("v_")[None], slots[:, None], name=f"adam_{tag}")
        parts = [_unstack2d(o[0], shapes2d) for o in outs4]
        for k, n in enumerate(names):
            res[n] = tuple(q[k].reshape(A[n].shape) for q in parts)
        if tag == "rep":
            total_loss = parts[0][-1][0, 0]

    outs = [total_loss, grad_x[None]]
    for k in range(4):
        outs += [res[n][k] for n in WEIGHTS]
    return tuple(outs)
```

```python
import jax
import jax.numpy as jnp
from jax import lax
from jax.experimental import pallas as pl
from jax.experimental.pallas import tpu as pltpu

F32 = jnp.float32
BF16 = jnp.bfloat16

D_MODEL = 1024
D_INNER = 2048
N_HEADS = 32
HEAD_DIM = 64
N_GROUPS = 4
D_STATE = 128
CHUNK = 128
CONV_DIM = 3072
SSM_K = 4
CF_K = 31
N_MEM = 256
XA_HEADS = 4
XA_HD = 256
D_FF = 2816
FFN_K = 3
EPS = 1e-6
DEPTH = 4
N_DEV = 8

ADAM_LR = 0.001
ADAM_B1 = 0.9
ADAM_B2 = 0.999
ADAM_EPS = 1e-08
ADAM_WD = 0.01
ADAM_STEP = 10

LANE = 128
VMEM_LIMIT = 56 * 1024 * 1024
NEG = -1e30
MESH = pl.DeviceIdType.MESH


def _cp(*sem):
    return pltpu.CompilerParams(dimension_semantics=sem if sem else None, vmem_limit_bytes=VMEM_LIMIT)


def _tile(n, cap):
    if n <= cap:
        return n
    best = 0
    for t in range(LANE, cap + 1, LANE):
        if n % t == 0:
            best = t
    assert best, (n, cap)
    return best


def _sig(x):
    return 1.0 / (1.0 + jnp.exp(-x))


def _split3(v):
    v0 = v.astype(BF16)
    r1 = v - v0.astype(F32)
    v1 = r1.astype(BF16)
    v2 = (r1 - v1.astype(F32)).astype(BF16)
    return v0, v1, v2


def _dot(a, b, ca=1, cb=0):
    return lax.dot_general(a, b, (((ca,), (cb,)), ((), ())), preferred_element_type=F32)


def _dot3(v, m, ca=1, cb=0):
    v0, v1, v2 = _split3(v)
    return _dot(v0, m, ca, cb) + _dot(v1, m, ca, cb) + _dot(v2, m, ca, cb)


def mm(a, b, *, ta=False, tb=False, bias=None, acc=None, out_dtype=F32, a_idx=None, layer=None, b_k0=0, b_n=None,
       into=None, name):
    if isinstance(b, (list, tuple)):
        b, layer = b[layer], None
    if ta:
        K, M = a.shape[-2:]
    else:
        M, K = a.shape[-2:]
    N = b_n if b_n is not None else (b.shape[-2] if tb else b.shape[-1])
    assert (b.ndim == 3) == (layer is not None) and (a.ndim == 3) == (a_idx is not None)
    tm = _tile(M, 1024)
    tn = _tile(N, 1536)
    tk = _tile(K, 2048)
    nk = K // tk
    assert b_k0 % tk == 0 and b_k0 + K <= (b.shape[-1] if tb else b.shape[-2])
    kb = b_k0 // tk
    has_bias, has_acc = bias is not None, acc is not None
    if into is not None:
        out_dtype = into[0].dtype
        assert into[0].shape[1] == M and into[2] % tn == 0 and into[2] + N <= into[0].shape[2] and not has_acc

    def body(*refs):
        a_ref, b_ref = refs[0], refs[1]
        pos = 2
        bias_ref = acc_ref = None
        if has_bias:
            bias_ref = refs[pos]
            pos += 1
        if has_acc:
            acc_ref = refs[pos]
            pos += 1
        if into is not None:
            pos += 1
        o_ref = refs[pos]
        s_ref = refs[pos + 1] if nk > 1 else None
        p = _dot(a_ref[...].astype(BF16), b_ref[...].astype(BF16), 0 if ta else 1, 1 if tb else 0)

        def extras(v):
            if has_bias:
                v = v + bias_ref[...]
            if has_acc:
                v = v + acc_ref[...]
            return v

        if nk == 1:
            o_ref[...] = extras(p).astype(out_dtype)
        else:
            k = pl.program_id(2)

            @pl.when(k == 0)
            def _():
                s_ref[...] = extras(p)

            @pl.when(k > 0)
            def _():
                s_ref[...] += p

            @pl.when(k == nk - 1)
            def _():
                o_ref[...] = s_ref[...].astype(out_dtype)

    lead_a = () if a_idx is None else (a_idx,)
    lead_b = () if layer is None else (layer,)
    sq = lambda lead: (None,) * len(lead)
    if ta:
        a_spec = pl.BlockSpec((*sq(lead_a), tk, tm), lambda i, j, k: (*lead_a, k, i))
    else:
        a_spec = pl.BlockSpec((*sq(lead_a), tm, tk), lambda i, j, k: (*lead_a, i, k))
    if tb:
        b_spec = pl.BlockSpec((*sq(lead_b), tn, tk), lambda i, j, k: (*lead_b, j, k + kb))
    else:
        b_spec = pl.BlockSpec((*sq(lead_b), tk, tn), lambda i, j, k: (*lead_b, k + kb, j))
    in_specs, args = [a_spec, b_spec], [a, b]
    if has_bias:
        in_specs.append(pl.BlockSpec((1, tn), lambda i, j, k: (0, j)))
        args.append(bias.reshape(1, N).astype(F32))
    if has_acc:
        in_specs.append(pl.BlockSpec((tm, tn), lambda i, j, k: (i, j)))
        args.append(acc)
    if into is None:
        out_spec = pl.BlockSpec((tm, tn), lambda i, j, k: (i, j))
        out_shape = jax.ShapeDtypeStruct((M, N), out_dtype)
        aliases = {}
    else:
        buf, l, col0 = into
        cb = col0 // tn
        in_specs.append(pl.BlockSpec(memory_space=pl.ANY))
        args.append(buf)
        out_spec = pl.BlockSpec((None, tm, tn), lambda i, j, k: (l, i, j + cb))
        out_shape = jax.ShapeDtypeStruct(buf.shape, buf.dtype)
        aliases = {len(args) - 1: 0}
    return pl.pallas_call(
        body, name=name, grid=(M // tm, N // tn, nk),
        in_specs=in_specs, out_specs=out_spec, out_shape=out_shape, input_output_aliases=aliases,
        scratch_shapes=[pltpu.VMEM((tm, tn), F32)] if nk > 1 else [],
        compiler_params=_cp("parallel", "parallel", "arbitrary"),
    )(*args)


def colsum(x, name):
    L, C = x.shape
    tr = _tile(L, 512)
    tc = _tile(C, 1024)

    def body(x_ref, o_ref):
        @pl.when(pl.program_id(1) == 0)
        def _():
            o_ref[...] = jnp.zeros_like(o_ref)

        o_ref[...] += jnp.sum(x_ref[...].astype(F32), axis=0, keepdims=True)

    return pl.pallas_call(
        body, name=name, grid=(C // tc, L // tr),
        in_specs=[pl.BlockSpec((tr, tc), lambda j, i: (i, j))],
        out_specs=pl.BlockSpec((1, tc), lambda j, i: (0, j)),
        out_shape=jax.ShapeDtypeStruct((1, C), F32),
        compiler_params=_cp("parallel", "arbitrary"),
    )(x)


TR = 256


def _row_spec(tr, w):
    return pl.BlockSpec((tr, w), lambda i: (i, 0))


def _vec_spec(w):
    return pl.BlockSpec((1, w), lambda i: (0, 0))


def _rms(v):
    return lax.rsqrt(jnp.mean(v * v, axis=-1, keepdims=True) + EPS)


def norm_fwd(x, g, name):
    L, D = x.shape
    tr = min(TR, L)

    def body(x_ref, g_ref, h_ref):
        xv = x_ref[...]
        h_ref[...] = (xv * _rms(xv) * g_ref[...]).astype(BF16)

    return pl.pallas_call(
        body, name=name, grid=(L // tr,),
        in_specs=[_row_spec(tr, D), _vec_spec(D)], out_specs=_row_spec(tr, D),
        out_shape=jax.ShapeDtypeStruct((L, D), BF16), compiler_params=_cp("parallel"),
    )(x, g.reshape(1, D))


def bnd_fwd(x, f, gpost, gpre, name):
    L, D = x.shape
    tr = min(TR, L)

    def body(x_ref, f_ref, gp_ref, gn_ref, xo_ref, h_ref):
        fv = f_ref[...].astype(F32)
        xn = x_ref[...] + fv * _rms(fv) * gp_ref[...]
        xo_ref[...] = xn
        h_ref[...] = (xn * _rms(xn) * gn_ref[...]).astype(BF16)

    return pl.pallas_call(
        body, name=name, grid=(L // tr,),
        in_specs=[_row_spec(tr, D), _row_spec(tr, D), _vec_spec(D), _vec_spec(D)],
        out_specs=[_row_spec(tr, D), _row_spec(tr, D)],
        out_shape=[jax.ShapeDtypeStruct((L, D), F32), jax.ShapeDtypeStruct((L, D), BF16)],
        compiler_params=_cp("parallel"),
    )(x, f, gpost.reshape(1, D), gpre.reshape(1, D))


def final_fwd(x, f, gpost, target, name):
    L, D = x.shape
    tr = min(TR, L)
    n = L // tr

    def body(x_ref, f_ref, gp_ref, t_ref, dy_ref, loss_ref, acc_ref):
        i = pl.program_id(0)

        @pl.when(i == 0)
        def _():
            acc_ref[...] = jnp.zeros_like(acc_ref)

        fv = f_ref[...].astype(F32)
        e = x_ref[...] + fv * _rms(fv) * gp_ref[...] - t_ref[...]
        dy_ref[...] = e * (1.0 / D)
        acc_ref[...] += jnp.sum(e * e, axis=0, keepdims=True)

        @pl.when(i == n - 1)
        def _():
            loss_ref[...] = jnp.full((1, LANE), 0.5 / D, F32) * jnp.sum(acc_ref[...])

    return pl.pallas_call(
        body, name=name, grid=(n,),
        in_specs=[_row_spec(tr, D), _row_spec(tr, D), _vec_spec(D), _row_spec(tr, D)],
        out_specs=[_row_spec(tr, D), _vec_spec(LANE)],
        out_shape=[jax.ShapeDtypeStruct((L, D), F32), jax.ShapeDtypeStruct((1, LANE), F32)],
        scratch_shapes=[pltpu.VMEM((1, D), F32)],
        compiler_params=_cp("arbitrary"),
    )(x, f, gpost.reshape(1, D), target)


def _rms_bwd(v, g, dy):
    r = _rms(v)
    vn = v * r
    dg = jnp.sum(dy * vn, axis=0, keepdims=True)
    dvn = dy * g
    dv = r * (dvn - vn * jnp.mean(dvn * vn, axis=-1, keepdims=True))
    return dv, dg


def bnd_bwd(dxp, *, pre=None, post=None, dep=None, name):
    L, D = dxp.shape
    tr = min(TR, L)
    has_pre, has_post = pre is not None, post is not None

    def body(*refs):
        pos = 0
        dxp_ref = refs[pos]; pos += 1
        if has_pre:
            x_ref, gpre_ref, dh_ref = refs[pos:pos + 3]; pos += 3
        if has_post:
            f_ref, gpost_ref = refs[pos:pos + 2]; pos += 2
        if dep is not None:
            pos += 1
        if has_pre:
            dx_ref, dgpre_ref = refs[pos:pos + 2]; pos += 2
        if has_post:
            df_ref, dgpost_ref = refs[pos:pos + 2]; pos += 2
        i = pl.program_id(0)
        dx = dxp_ref[...]
        if has_pre:
            d, dg = _rms_bwd(x_ref[...], gpre_ref[...], dh_ref[...].astype(F32))
            dx = dx + d
            dx_ref[...] = dx

            @pl.when(i == 0)
            def _():
                dgpre_ref[...] = jnp.zeros_like(dgpre_ref)

            dgpre_ref[...] += dg
        if has_post:
            d, dg = _rms_bwd(f_ref[...].astype(F32), gpost_ref[...], dx)
            df_ref[...] = d.astype(BF16)

            @pl.when(i == 0)
            def _():
                dgpost_ref[...] = jnp.zeros_like(dgpost_ref)

            dgpost_ref[...] += dg

    in_specs, args = [_row_spec(tr, D)], [dxp]
    out_specs, out_shape, names = [], [], []
    if has_pre:
        x, gpre, dh = pre
        in_specs += [_row_spec(tr, D), _vec_spec(D), _row_spec(tr, D)]
        args += [x, gpre.reshape(1, D), dh]
        out_specs += [_row_spec(tr, D), _vec_spec(D)]
        out_shape += [jax.ShapeDtypeStruct((L, D), F32), jax.ShapeDtypeStruct((1, D), F32)]
        names += ["dx", "dgpre"]
    if has_post:
        f, gpost = post
        in_specs += [_row_spec(tr, D), _vec_spec(D)]
        args += [f, gpost.reshape(1, D)]
        out_specs += [_row_spec(tr, D), _vec_spec(D)]
        out_shape += [jax.ShapeDtypeStruct((L, D), BF16), jax.ShapeDtypeStruct((1, D), F32)]
        names += ["df", "dgpost"]
    if dep is not None:
        in_specs.append(pl.BlockSpec(memory_space=pl.ANY))
        args.append(dep)
    outs = pl.pallas_call(
        body, name=name, grid=(L // tr,), in_specs=in_specs, out_specs=out_specs, out_shape=out_shape,
        compiler_params=_cp("arbitrary"),
    )(*args)
    return dict(zip(names, outs))


def norm_dg(x, dy, name):
    L, D = x.shape
    tr = min(TR, L)

    def body(x_ref, dy_ref, o_ref):
        @pl.when(pl.program_id(0) == 0)
        def _():
            o_ref[...] = jnp.zeros_like(o_ref)

        xv = x_ref[...]
        o_ref[...] += jnp.sum(dy_ref[...] * xv * _rms(xv), axis=0, keepdims=True)

    return pl.pallas_call(
        body, name=name, grid=(L // tr,),
        in_specs=[_row_spec(tr, D), _row_spec(tr, D)], out_specs=_vec_spec(D),
        out_shape=jax.ShapeDtypeStruct((1, D), F32), compiler_params=_cp("arbitrary"),
    )(x, dy)


HALO = 32


def _prev_halo_spec(tr, tc, col):
    per = tr // HALO
    return pl.BlockSpec((HALO, tc), lambda *g: (jnp.maximum(g[-1] * per - 1, 0), col(*g)))


def _fill_prev(scr, halo_val, blk_val, i, tr):
    scr[pl.ds(0, HALO), :] = jnp.where(i == 0, 0.0, halo_val)
    scr[pl.ds(HALO, tr), :] = blk_val


def _conv(scr, w_ref, K, tr):
    acc = None
    for k in range(K):
        term = scr[pl.ds(HALO - (K - 1) + k, tr), :] * w_ref[k:k + 1, :]
        acc = term if acc is None else acc + term
    return acc


def _shift_copies(scr, sh, rows):
    n = rows - 8
    for r in range(1, 8):
        sh[r - 1, pl.ds(0, n), :] = scr[pl.ds(r, n), :]


def _tap(scr, sh, off, tr):
    q, r = divmod(off, 8)
    return scr[pl.ds(off, tr), :] if r == 0 else sh[r - 1, pl.ds(8 * q, tr), :]


def ssm_conv_fwd(zx, w, b, *, col0, ncols, wcol0, out_dtype, name):
    L = zx.shape[0]
    tr = min(TR, L)
    tc = 1024
    cb, wb = col0 // tc, wcol0 // tc

    def body(x_ref, h_ref, w_ref, b_ref, o_ref, p_ref, scr):
        i = pl.program_id(1)
        _fill_prev(scr, h_ref[...].astype(F32), x_ref[...].astype(F32), i, tr)
        pre = _conv(scr, w_ref, SSM_K, tr) + b_ref[...]
        p_ref[...] = pre.astype(BF16)
        o_ref[...] = (pre * _sig(pre)).astype(out_dtype)

    out = pl.BlockSpec((tr, tc), lambda j, i: (i, j))
    return pl.pallas_call(
        body, name=name, grid=(ncols // tc, L // tr),
        in_specs=[pl.BlockSpec((tr, tc), lambda j, i: (i, j + cb)),
                  _prev_halo_spec(tr, tc, lambda j, i: j + cb),
                  pl.BlockSpec((SSM_K, tc), lambda j, i: (0, j + wb)),
                  pl.BlockSpec((1, tc), lambda j, i: (0, j + wb))],
        out_specs=[out, out],
        out_shape=[jax.ShapeDtypeStruct((L, ncols), out_dtype), jax.ShapeDtypeStruct((L, ncols), BF16)],
        scratch_shapes=[pltpu.VMEM((HALO + tr, tc), F32)],
        compiler_params=_cp("parallel", "parallel"),
    )(zx, zx, w, b)


def ssm_conv_bwd(zx, pre, d, w, *, col0, dcol0, ncols, name):
    L = zx.shape[0]
    tr = min(TR, L)
    tc = 1024
    cb, db_ = col0 // tc, dcol0 // tc
    n = L // tr
    per = tr // HALO
    last = L // HALO - 1

    def body(x_ref, p_ref, np_ref, d_ref, nd_ref, w_ref, dx_ref, dw_ref, db_ref, sd):
        i = pl.program_id(1)

        def dpre(p, dv):
            s = _sig(p)
            return dv * s * (1.0 + p * (1.0 - s))

        dp = dpre(p_ref[...].astype(F32), d_ref[...])
        sd[pl.ds(0, tr), :] = dp
        sd[pl.ds(tr, HALO), :] = jnp.where(i == n - 1, 0.0, dpre(np_ref[...].astype(F32), nd_ref[...]))

        @pl.when(i == 0)
        def _():
            dw_ref[...] = jnp.zeros_like(dw_ref)
            db_ref[...] = jnp.zeros_like(db_ref)

        xv = x_ref[...].astype(F32)
        acc = None
        for k in range(SSM_K):
            tk = sd[pl.ds(SSM_K - 1 - k, tr), :]
            term = tk * w_ref[k:k + 1, :]
            acc = term if acc is None else acc + term
            dw_ref[k:k + 1, :] += jnp.sum(xv * tk, axis=0, keepdims=True)
        dx_ref[...] = acc.astype(BF16)
        db_ref[...] += jnp.sum(dp, axis=0, keepdims=True)

    nxt = lambda i: jnp.minimum((i + 1) * per, last)
    return pl.pallas_call(
        body, name=name, grid=(ncols // tc, n),
        in_specs=[pl.BlockSpec((tr, tc), lambda j, i: (i, j + cb)),
                  pl.BlockSpec((tr, tc), lambda j, i: (i, j)),
                  pl.BlockSpec((HALO, tc), lambda j, i: (nxt(i), j)),
                  pl.BlockSpec((tr, tc), lambda j, i: (i, j + db_)),
                  pl.BlockSpec((HALO, tc), lambda j, i: (nxt(i), j + db_)),
                  pl.BlockSpec((SSM_K, tc), lambda j, i: (0, j + db_))],
        out_specs=[pl.BlockSpec((tr, tc), lambda j, i: (i, j)),
                   pl.BlockSpec((SSM_K, tc), lambda j, i: (0, j)),
                   pl.BlockSpec((1, tc), lambda j, i: (0, j))],
        out_shape=[jax.ShapeDtypeStruct((L, ncols), BF16), jax.ShapeDtypeStruct((SSM_K, ncols), F32),
                   jax.ShapeDtypeStruct((1, ncols), F32)],
        scratch_shapes=[pltpu.VMEM((tr + HALO, tc), F32)],
        compiler_params=_cp("parallel", "arbitrary"),
    )(zx, pre, pre, d, d, w)


FFN_TC = 1408


def ffn_act_fwd(u, w, b, name):
    L = u.shape[0]
    tr = min(TR, L)
    tc = FFN_TC
    nb = D_FF // tc

    def body(g_ref, hg_ref, v_ref, hv_ref, wg_ref, wv_ref, bg_ref, bv_ref, o_ref, c_ref, sg, sv):
        i = pl.program_id(1)
        _fill_prev(sg, hg_ref[...].astype(F32), g_ref[...].astype(F32), i, tr)
        _fill_prev(sv, hv_ref[...].astype(F32), v_ref[...].astype(F32), i, tr)
        ug = _conv(sg, wg_ref, FFN_K, tr) + bg_ref[...]
        uv = _conv(sv, wv_ref, FFN_K, tr) + bv_ref[...]
        c_ref[0] = ug.astype(BF16)
        c_ref[1] = uv.astype(BF16)
        o_ref[...] = (ug * _sig(ug) * uv).astype(BF16)

    blk = lambda off: pl.BlockSpec((tr, tc), lambda j, i: (i, j + off))
    wsp = lambda off: pl.BlockSpec((FFN_K, tc), lambda j, i: (0, j + off))
    bsp = lambda off: pl.BlockSpec((1, tc), lambda j, i: (0, j + off))
    return pl.pallas_call(
        body, name=name, grid=(nb, L // tr),
        in_specs=[blk(0), _prev_halo_spec(tr, tc, lambda j, i: j),
                  blk(nb), _prev_halo_spec(tr, tc, lambda j, i: j + nb),
                  wsp(0), wsp(nb), bsp(0), bsp(nb)],
        out_specs=[pl.BlockSpec((tr, tc), lambda j, i: (i, j)), pl.BlockSpec((2, tr, tc), lambda j, i: (0, i, j))],
        out_shape=[jax.ShapeDtypeStruct((L, D_FF), BF16), jax.ShapeDtypeStruct((2, L, D_FF), BF16)],
        scratch_shapes=[pltpu.VMEM((HALO + tr, tc), F32), pltpu.VMEM((HALO + tr, tc), F32)],
        compiler_params=_cp("parallel", "parallel"),
    )(u, u, u, u, w, w, b, b)


def ffn_act_bwd(u, c, dact, w, name):
    L = u.shape[0]
    tr = min(TR, L)
    tc = FFN_TC
    nb = D_FF // tc
    n = L // tr
    per = tr // HALO
    last = L // HALO - 1

    def body(g_ref, v_ref, c_ref, nc_ref, da_ref, nda_ref, wg_ref, wv_ref, du_ref, dw_ref, db_ref, dg_s, dv_s):
        i = pl.program_id(1)

        def grads(cg, cv, da):
            s = _sig(cg)
            return da * cv * s * (1.0 + cg * (1.0 - s)), da * cg * s

        dg, dv = grads(c_ref[0].astype(F32), c_ref[1].astype(F32), da_ref[...].astype(F32))
        ndg, ndv = grads(nc_ref[0].astype(F32), nc_ref[1].astype(F32), nda_ref[...].astype(F32))
        at_end = i == n - 1
        for half, (scr, d, nd, x_ref, w_ref) in enumerate(((dg_s, dg, ndg, g_ref, wg_ref), (dv_s, dv, ndv, v_ref, wv_ref))):
            scr[pl.ds(0, tr), :] = d
            scr[pl.ds(tr, HALO), :] = jnp.where(at_end, 0.0, nd)

            @pl.when(i == 0)
            def _():
                dw_ref[half] = jnp.zeros((FFN_K, tc), F32)
                db_ref[half] = jnp.zeros((1, tc), F32)

            xv = x_ref[...].astype(F32)
            acc = None
            for k in range(FFN_K):
                tk = scr[pl.ds(FFN_K - 1 - k, tr), :]
                term = tk * w_ref[k:k + 1, :]
                acc = term if acc is None else acc + term
                dw_ref[half, k:k + 1, :] += jnp.sum(xv * tk, axis=0, keepdims=True)
            du_ref[half] = acc.astype(BF16)
            db_ref[half] += jnp.sum(d, axis=0, keepdims=True)

    blk = lambda off: pl.BlockSpec((tr, tc), lambda j, i: (i, j + off))
    wsp = lambda off: pl.BlockSpec((FFN_K, tc), lambda j, i: (0, j + off))
    nxt = lambda i: jnp.minimum((i + 1) * per, last)
    return pl.pallas_call(
        body, name=name, grid=(nb, n),
        in_specs=[blk(0), blk(nb),
                  pl.BlockSpec((2, tr, tc), lambda j, i: (0, i, j)),
                  pl.BlockSpec((2, HALO, tc), lambda j, i: (0, nxt(i), j)),
                  pl.BlockSpec((tr, tc), lambda j, i: (i, j)),
                  pl.BlockSpec((HALO, tc), lambda j, i: (nxt(i), j)),
                  wsp(0), wsp(nb)],
        out_specs=[pl.BlockSpec((2, tr, tc), lambda j, i: (0, i, j)),
                   pl.BlockSpec((2, FFN_K, tc), lambda j, i: (0, 0, j)),
                   pl.BlockSpec((2, 1, tc), lambda j, i: (0, 0, j))],
        out_shape=[jax.ShapeDtypeStruct((2, L, D_FF), BF16), jax.ShapeDtypeStruct((2, FFN_K, D_FF), F32),
                   jax.ShapeDtypeStruct((2, 1, D_FF), F32)],
        scratch_shapes=[pltpu.VMEM((tr + HALO, tc), F32), pltpu.VMEM((tr + HALO, tc), F32)],
        compiler_params=_cp("parallel", "arbitrary"),
    )(u, u, c, c, dact, dact, w, w)


def _ln_stats(c):
    mu = jnp.mean(c, axis=-1, keepdims=True)
    cc = c - mu
    rstd = lax.rsqrt(jnp.mean(cc * cc, axis=-1, keepdims=True) + EPS)
    return cc * rstd, rstd


def cf_fwd(u, dw_w, dw_b, ln_g, ln_b, name):
    L = u.shape[0]
    D = D_MODEL
    tr = min(TR, L)

    def body(a_ref, ha_ref, g_ref, hg_ref, w_ref, b_ref, lg_ref, lb_ref, c_ref, s_ref, scr, sh):
        i = pl.program_id(0)
        glu_h = ha_ref[...].astype(F32) * _sig(hg_ref[...].astype(F32))
        glu = a_ref[...].astype(F32) * _sig(g_ref[...].astype(F32))
        _fill_prev(scr, glu_h, glu, i, tr)
        _shift_copies(scr, sh, HALO + tr)
        c = b_ref[...]
        for k in range(CF_K):
            c = c + _tap(scr, sh, HALO - (CF_K - 1) + k, tr) * w_ref[k:k + 1, :]
        c_ref[...] = c
        xhat, _ = _ln_stats(c)
        ln = xhat * lg_ref[...] + lb_ref[...]
        s_ref[...] = (ln * _sig(ln)).astype(BF16)

    per = tr // HALO
    halo = lambda col: pl.BlockSpec((HALO, D), lambda i: (jnp.maximum(i * per - 1, 0), col))
    return pl.pallas_call(
        body, name=name, grid=(L // tr,),
        in_specs=[pl.BlockSpec((tr, D), lambda i: (i, 0)), halo(0),
                  pl.BlockSpec((tr, D), lambda i: (i, 1)), halo(1),
                  pl.BlockSpec((CF_K, D), lambda i: (0, 0)), _vec_spec(D), _vec_spec(D), _vec_spec(D)],
        out_specs=[_row_spec(tr, D), _row_spec(tr, D)],
        out_shape=[jax.ShapeDtypeStruct((L, D), F32), jax.ShapeDtypeStruct((L, D), BF16)],
        scratch_shapes=[pltpu.VMEM((HALO + tr, D), F32), pltpu.VMEM((7, HALO + tr, D), F32)],
        compiler_params=_cp("parallel"),
    )(u, u, u, u, dw_w, dw_b, ln_g, ln_b)


def cf_bwd_ln(c, ds, ln_g, ln_b, name):
    L, D = c.shape
    tr = min(TR, L)

    def body(c_ref, ds_ref, lg_ref, lb_ref, dc_ref, dg_ref, db_ref):
        xhat, rstd = _ln_stats(c_ref[...])
        ln = xhat * lg_ref[...] + lb_ref[...]
        sg = _sig(ln)
        dln = ds_ref[...].astype(F32) * sg * (1.0 + ln * (1.0 - sg))

        @pl.when(pl.program_id(0) == 0)
        def _():
            dg_ref[...] = jnp.zeros_like(dg_ref)
            db_ref[...] = jnp.zeros_like(db_ref)

        dg_ref[...] += jnp.sum(dln * xhat, axis=0, keepdims=True)
        db_ref[...] += jnp.sum(dln, axis=0, keepdims=True)
        dxh = dln * lg_ref[...]
        dc_ref[...] = rstd * (dxh - jnp.mean(dxh, axis=-1, keepdims=True)
                              - xhat * jnp.mean(dxh * xhat, axis=-1, keepdims=True))

    return pl.pallas_call(
        body, name=name, grid=(L // tr,),
        in_specs=[_row_spec(tr, D), _row_spec(tr, D), _vec_spec(D), _vec_spec(D)],
        out_specs=[_row_spec(tr, D), _vec_spec(D), _vec_spec(D)],
        out_shape=[jax.ShapeDtypeStruct((L, D), F32), jax.ShapeDtypeStruct((1, D), F32),
                   jax.ShapeDtypeStruct((1, D), F32)],
        compiler_params=_cp("arbitrary"),
    )(c, ds, ln_g, ln_b)


def cf_bwd_conv(u, dc, dw_w, name):
    L = u.shape[0]
    D = D_MODEL
    tr = min(TR, L)
    n = L // tr

    def body(a_ref, g_ref, dc_ref, nx_ref, w_ref, du_ref, dw_ref, db_ref, sd, shd):
        i = pl.program_id(0)
        a = a_ref[...].astype(F32)
        sg = _sig(g_ref[...].astype(F32))
        glu = a * sg
        dcv = dc_ref[...]
        sd[pl.ds(0, tr), :] = dcv
        sd[pl.ds(tr, HALO), :] = jnp.where(i == n - 1, 0.0, nx_ref[...])
        _shift_copies(sd, shd, tr + HALO)

        @pl.when(i == 0)
        def _():
            dw_ref[...] = jnp.zeros_like(dw_ref)
            db_ref[...] = jnp.zeros_like(db_ref)

        dglu = None
        for k in range(CF_K):
            tk = _tap(sd, shd, CF_K - 1 - k, tr)
            term = tk * w_ref[k:k + 1, :]
            dglu = term if dglu is None else dglu + term
            dw_ref[k:k + 1, :] += jnp.sum(glu * tk, axis=0, keepdims=True)
        du_ref[:, 0:D] = (dglu * sg).astype(BF16)
        du_ref[:, D:2 * D] = (dglu * a * sg * (1.0 - sg)).astype(BF16)
        db_ref[...] += jnp.sum(dcv, axis=0, keepdims=True)

    per = tr // HALO
    last = L // HALO - 1
    return pl.pallas_call(
        body, name=name, grid=(n,),
        in_specs=[pl.BlockSpec((tr, D), lambda i: (i, 0)),
                  pl.BlockSpec((tr, D), lambda i: (i, 1)),
                  _row_spec(tr, D),
                  pl.BlockSpec((HALO, D), lambda i: (jnp.minimum((i + 1) * per, last), 0)),
                  pl.BlockSpec((CF_K, D), lambda i: (0, 0))],
        out_specs=[pl.BlockSpec((tr, 2 * D), lambda i: (i, 0)),
                   pl.BlockSpec((CF_K, D), lambda i: (0, 0)), _vec_spec(D)],
        out_shape=[jax.ShapeDtypeStruct((L, 2 * D), BF16), jax.ShapeDtypeStruct((CF_K, D), F32),
                   jax.ShapeDtypeStruct((1, D), F32)],
        scratch_shapes=[pltpu.VMEM((tr + HALO, D), F32), pltpu.VMEM((7, tr + HALO, D), F32)],
        compiler_params=_cp("arbitrary"),
    )(u, u, dc, dc, dw_w)


XA_TR = 512
XA_SCALE = XA_HD ** -0.5


def _xa_probs(qh, kh):
    s = _dot(qh, kh, 1, 1) * XA_SCALE
    p = jnp.exp(s - jnp.max(s, axis=-1, keepdims=True))
    return p / jnp.sum(p, axis=-1, keepdims=True)


def attn_fwd(q, kv, name):
    L, D = q.shape
    tr = min(XA_TR, L)

    def body(q_ref, kv_ref, o_ref):
        for hd in range(XA_HEADS):
            c = slice(hd * XA_HD, (hd + 1) * XA_HD)
            p = _xa_probs(q_ref[:, c], kv_ref[:, c])
            vh = kv_ref[:, D + hd * XA_HD:D + (hd + 1) * XA_HD]
            o_ref[:, c] = _dot(p.astype(BF16), vh).astype(BF16)

    return pl.pallas_call(
        body, name=name, grid=(L // tr,),
        in_specs=[_row_spec(tr, D), pl.BlockSpec((N_MEM, 2 * D), lambda i: (0, 0))],
        out_specs=_row_spec(tr, D), out_shape=jax.ShapeDtypeStruct((L, D), BF16),
        compiler_params=_cp("parallel"),
    )(q, kv)


def attn_bwd(q, kv, do, name):
    L, D = q.shape
    tr = min(XA_TR, L)

    def body(q_ref, kv_ref, do_ref, dq_ref, dkv_ref):
        @pl.when(pl.program_id(0) == 0)
        def _():
            dkv_ref[...] = jnp.zeros_like(dkv_ref)

        for hd in range(XA_HEADS):
            c = slice(hd * XA_HD, (hd + 1) * XA_HD)
            cv = slice(D + hd * XA_HD, D + (hd + 1) * XA_HD)
            qh, kh, vh, doh = q_ref[:, c], kv_ref[:, c], kv_ref[:, cv], do_ref[:, c]
            p = _xa_probs(qh, kh)
            dp = _dot(doh, vh, 1, 1)
            dkv_ref[:, cv] += _dot(p.astype(BF16), doh, 0, 0)
            ds = (p * (dp - jnp.sum(dp * p, axis=-1, keepdims=True)) * XA_SCALE).astype(BF16)
            dq_ref[:, c] = _dot(ds, kh).astype(BF16)
            dkv_ref[:, c] += _dot(ds, qh, 0, 0)

    return pl.pallas_call(
        body, name=name, grid=(L // tr,),
        in_specs=[_row_spec(tr, D), pl.BlockSpec((N_MEM, 2 * D), lambda i: (0, 0)), _row_spec(tr, D)],
        out_specs=[_row_spec(tr, D), pl.BlockSpec((N_MEM, 2 * D), lambda i: (0, 0))],
        out_shape=[jax.ShapeDtypeStruct((L, D), BF16), jax.ShapeDtypeStruct((N_MEM, 2 * D), F32)],
        compiler_params=_cp("arbitrary"),
    )(q, kv, do)


N_PAIRS = N_HEADS // 2
PAIRS_PER_GROUP = N_PAIRS // N_GROUPS
GN = N_GROUPS * D_STATE


def _softplus(x):
    t = jnp.exp(-jnp.abs(x))
    return jnp.maximum(x, 0.0) + jnp.where(t < 1e-4, t * (1.0 - 0.5 * t), jnp.log(1.0 + t))


def _dot3b(m, v, ca=1, cb=0):
    v0, v1, v2 = _split3(v)
    return _dot(m, v0, ca, cb) + _dot(m, v1, ca, cb) + _dot(m, v2, ca, cb)


def ssd_consts():
    h = lax.broadcasted_iota(jnp.int32, (LANE, D_INNER), 0)
    c = lax.broadcasted_iota(jnp.int32, (LANE, D_INNER), 1)
    expand = (c // HEAD_DIM == h).astype(BF16)
    r = lax.broadcasted_iota(jnp.int32, (CHUNK, CHUNK), 0)
    k = lax.broadcasted_iota(jnp.int32, (CHUNK, CHUNK), 1)
    tri = (k <= r).astype(BF16)
    return expand, tri


def _ssd_common(dtr_ref, prm_ref, e_ref, tri_ref):
    lane = lax.broadcasted_iota(jnp.int32, (CHUNK, LANE), 1)
    valid = lane < N_HEADS
    A = -jnp.exp(prm_ref[1:2, :])
    pre = dtr_ref[...] + prm_ref[0:1, :]
    dt = jnp.where(valid, _softplus(pre), 0.0)
    cs = _dot3b(tri_ref[...], dt * A)
    E = e_ref[...]
    dt_x = _dot3(dt, E)
    cs_x = _dot3(cs, E)
    csl_x = cs_x[CHUNK - 1:CHUNK, :]
    return dict(valid=valid, A=A, pre=pre, dt=dt, cs=cs, csT=cs.T, dt_x=dt_x, ecs_x=jnp.exp(cs_x),
                dend_x=jnp.exp(csl_x - cs_x), cd_x=jnp.exp(csl_x), D_x=_dot3(prm_ref[...], E)[2:3, :])


def ssd_fwd(xs, bc, dtr, zx, prm, ng, name):
    L = xs.shape[0]
    nc = L // CHUNK
    expand, tri = ssd_consts()

    def body(xs_ref, bc_ref, dtr_ref, z_ref, prm_ref, ng_ref, e_ref, tri_ref, y_ref, yn_ref, st_ref, state):
        @pl.when(pl.program_id(0) == 0)
        def _():
            state[...] = jnp.zeros_like(state)

        q = _ssd_common(dtr_ref, prm_ref, e_ref, tri_ref)
        cs, csT = q["cs"], q["csT"]
        xs_v = xs_ref[...]
        X = xs_v * q["dt_x"]
        Xb = X.astype(BF16)
        Xd = (X * q["dend_x"]).astype(BF16)
        ii = lax.broadcasted_iota(jnp.int32, (CHUNK, CHUNK), 0)
        jj = lax.broadcasted_iota(jnp.int32, (CHUNK, CHUNK), 1)
        tril = jj <= ii
        first = jj < HEAD_DIM
        for g in range(N_GROUPS):
            Bg = bc_ref[:, g * D_STATE:(g + 1) * D_STATE]
            Cg = bc_ref[:, GN + g * D_STATE:GN + (g + 1) * D_STATE]
            S = _dot(Cg, Bg, 1, 1)
            for pr in range(PAIRS_PER_GROUP):
                pair = g * PAIRS_PER_GROUP + pr
                cols = slice(pair * LANE, (pair + 1) * LANE)
                Xp = Xb[:, cols]
                ys = []
                for h in (2 * pair, 2 * pair + 1):
                    seg = cs[:, h:h + 1] - csT[h:h + 1, :]
                    M = (S * jnp.exp(jnp.where(tril, seg, NEG))).astype(BF16)
                    ys.append(_dot(M, Xp))
                prevT = state[pair]
                st_ref[0, pair] = prevT
                yoff = _dot(Cg, prevT.astype(BF16)) * q["ecs_x"][:, cols]
                y_ref[:, cols] = jnp.where(first, ys[0], ys[1]) + yoff + xs_v[:, cols] * q["D_x"][:, cols]
                state[pair] = prevT * q["cd_x"][:, cols] + _dot(Bg, Xd[:, cols], 0, 0)
        z = z_ref[...].astype(F32)
        gt = y_ref[...] * z * _sig(z)
        yn_ref[...] = (gt * _rms(gt) * ng_ref[...]).astype(BF16)

    row = lambda w: pl.BlockSpec((CHUNK, w), lambda c: (c, 0))
    const = lambda a: pl.BlockSpec(a.shape, lambda c: (0,) * a.ndim)
    return pl.pallas_call(
        body, name=name, grid=(nc,),
        in_specs=[row(D_INNER), row(2 * GN), row(LANE), row(D_INNER), const(prm), const(ng), const(expand), const(tri)],
        out_specs=[row(D_INNER), row(D_INNER), pl.BlockSpec((1, N_PAIRS, D_STATE, LANE), lambda c: (c, 0, 0, 0))],
        out_shape=[jax.ShapeDtypeStruct((L, D_INNER), F32), jax.ShapeDtypeStruct((L, D_INNER), BF16),
                   jax.ShapeDtypeStruct((nc, N_PAIRS, D_STATE, LANE), F32)],
        scratch_shapes=[pltpu.VMEM((N_PAIRS, D_STATE, LANE), F32)],
        compiler_params=_cp("arbitrary"),
    )(xs, bc, dtr, zx, prm, ng, expand, tri)


def ssd_bwd(dyn, y, zx, xs, bc, dtr, st, prm, ng, name):
    L = xs.shape[0]
    nc = L // CHUNK
    expand, tri = ssd_consts()

    def body(dyn_ref, y_ref, z_ref, xs_ref, bc_ref, dtr_ref, st_ref, prm_ref, ng_ref, e_ref, tri_ref,
             dxbc_ref, dz_ref, ddtr_ref, dng_ref, dprm_ref, dstate, g_cs, g_q, dX, g_row):
        step = pl.program_id(0)

        @pl.when(step == 0)
        def _():
            dstate[...] = jnp.zeros_like(dstate)
            dng_ref[...] = jnp.zeros_like(dng_ref)
            dprm_ref[...] = jnp.zeros_like(dprm_ref)
            g_row[...] = jnp.zeros_like(g_row)

        q = _ssd_common(dtr_ref, prm_ref, e_ref, tri_ref)
        cs, csT, E = q["cs"], q["csT"], e_ref[...]
        xs_v = xs_ref[...]
        X = xs_v * q["dt_x"]
        Xb = X.astype(BF16)
        Xd_f = X * q["dend_x"]
        Xd = Xd_f.astype(BF16)

        yv = y_ref[...]
        z = z_ref[...].astype(F32)
        sz = _sig(z)
        silu = z * sz
        gt = yv * silu
        r = _rms(gt)
        gn = gt * r
        dyn_v = dyn_ref[...]
        dng_ref[...] += jnp.sum(dyn_v * gn, axis=0, keepdims=True)
        dgn = dyn_v * ng_ref[...]
        dgt = r * (dgn - gn * jnp.mean(dgn * gn, axis=-1, keepdims=True))
        dY = dgt * silu
        dz_ref[...] = (dgt * yv * sz * (1.0 + z * (1.0 - sz))).astype(BF16)
        dYb = dY.astype(BF16)
        g_row[1:2, :] += jnp.sum(dY * xs_v, axis=0, keepdims=True)

        ii = lax.broadcasted_iota(jnp.int32, (CHUNK, CHUNK), 0)
        jj = lax.broadcasted_iota(jnp.int32, (CHUNK, CHUNK), 1)
        tril = jj <= ii
        triu = jj >= ii
        first = jj < HEAD_DIM
        lane_row = lax.broadcasted_iota(jnp.int32, (1, LANE), 1)
        sub_col = lax.broadcasted_iota(jnp.int32, (CHUNK, 1), 0)
        dcs_col = jnp.zeros((CHUNK, LANE), F32)
        dcs_rowT = jnp.zeros((LANE, CHUNK), F32)
        for g in range(N_GROUPS):
            Bg = bc_ref[:, g * D_STATE:(g + 1) * D_STATE]
            Cg = bc_ref[:, GN + g * D_STATE:GN + (g + 1) * D_STATE]
            S = _dot(Cg, Bg, 1, 1)
            ST = _dot(Bg, Cg, 1, 1)
            dS = jnp.zeros((CHUNK, CHUNK), F32)
            dCg = jnp.zeros((CHUNK, D_STATE), F32)
            dBg = jnp.zeros((CHUNK, D_STATE), F32)
            for pr in range(PAIRS_PER_GROUP):
                pair = g * PAIRS_PER_GROUP + pr
                cols = slice(pair * LANE, (pair + 1) * LANE)
                Xp = Xb[:, cols]
                dYp_f = dY[:, cols]
                dYp = dYb[:, cols]
                prevT = st_ref[0, pair]
                prevTb = prevT.astype(BF16)
                dst = dstate[pair]
                dstb = dst.astype(BF16)
                ecs_p = q["ecs_x"][:, cols]
                g_cs[:, cols] = dYp_f * (_dot(Cg, prevTb) * ecs_p)
                dWb = (dYp_f * ecs_p).astype(BF16)
                dprev = dst * q["cd_x"][:, cols] + _dot(Cg, dWb, 0, 0)
                dCg = dCg + _dot(dWb, prevTb, 1, 1)
                g_row[0:1, cols] = jnp.sum(dst * prevT, axis=0, keepdims=True)
                dXp = None
                for hh, h in enumerate((2 * pair, 2 * pair + 1)):
                    mine = first if hh == 0 else jnp.logical_not(first)
                    seg = cs[:, h:h + 1] - csT[h:h + 1, :]
                    lam = jnp.exp(jnp.where(tril, seg, NEG))
                    dM = _dot(jnp.where(mine, dYp, jnp.zeros_like(dYp)), Xp, 1, 1)
                    dS = dS + dM * lam
                    Gm = dM * (S * lam)
                    dcs_col = dcs_col + jnp.sum(Gm, axis=1, keepdims=True) * (lane_row == h).astype(F32)
                    dcs_rowT = dcs_rowT + (sub_col == h).astype(F32) * jnp.sum(Gm, axis=0, keepdims=True)
                    MT = (ST * jnp.exp(jnp.where(triu, -seg, NEG))).astype(BF16)
                    t = _dot(MT, dYp)
                    dXp = t if dXp is None else jnp.where(first, dXp, t)
                dXd = _dot(Bg, dstb)
                dBg = dBg + _dot(Xd[:, cols], dstb, 1, 1)
                g_q[:, cols] = dXd * Xd_f[:, cols]
                dX[:, cols] = dXp + dXd * q["dend_x"][:, cols]
                dstate[pair] = dprev
            dSb = dS.astype(BF16)
            dxbc_ref[:, D_INNER + g * D_STATE:D_INNER + (g + 1) * D_STATE] = dBg + _dot(dSb, Cg, 0, 0)
            dxbc_ref[:, D_INNER + GN + g * D_STATE:D_INNER + GN + (g + 1) * D_STATE] = dCg + _dot(dSb, Bg)
        dXv = dX[...]
        dxbc_ref[:, 0:D_INNER] = q["D_x"] * dY + dXv * q["dt_x"]
        r_dt = _dot3(dXv * xs_v, E, 1, 1)
        r_cs = _dot3(g_cs[...], E, 1, 1)
        r_q = _dot3(g_q[...], E, 1, 1)
        r_row = _dot3(g_row[...], E, 1, 1)
        cd = jnp.exp(cs[CHUNK - 1:CHUNK, :])
        dcs_last = jnp.sum(r_q, axis=0, keepdims=True) + r_row[0:1, :] * cd
        dcs = r_cs - r_q + dcs_col - dcs_rowT.T + jnp.where(sub_col == CHUNK - 1, dcs_last, 0.0)
        da = _dot3b(tri_ref[...], dcs, 0, 0)
        dpre = jnp.where(q["valid"], (r_dt + da * q["A"]) * _sig(q["pre"]), 0.0)
        ddtr_ref[...] = dpre
        dprm_ref[0:1, :] += jnp.sum(dpre, axis=0, keepdims=True)
        dprm_ref[1:2, :] += jnp.sum(da * q["dt"], axis=0, keepdims=True) * q["A"]
        dprm_ref[2:3, :] = r_row[1:2, :]

    rev = lambda w: pl.BlockSpec((CHUNK, w), lambda c: (nc - 1 - c, 0))
    const = lambda a: pl.BlockSpec(a.shape, lambda c: (0,) * a.ndim)
    return pl.pallas_call(
        body, name=name, grid=(nc,),
        in_specs=[rev(D_INNER), rev(D_INNER), rev(D_INNER), rev(D_INNER), rev(2 * GN), rev(LANE),
                  pl.BlockSpec((1, N_PAIRS, D_STATE, LANE), lambda c: (nc - 1 - c, 0, 0, 0)),
                  const(prm), const(ng), const(expand), const(tri)],
        out_specs=[rev(CONV_DIM), rev(D_INNER), rev(LANE),
                   pl.BlockSpec((1, D_INNER), lambda c: (0, 0)), pl.BlockSpec((8, LANE), lambda c: (0, 0))],
        out_shape=[jax.ShapeDtypeStruct((L, CONV_DIM), F32), jax.ShapeDtypeStruct((L, D_INNER), BF16),
                   jax.ShapeDtypeStruct((L, LANE), F32), jax.ShapeDtypeStruct((1, D_INNER), F32),
                   jax.ShapeDtypeStruct((8, LANE), F32)],
        scratch_shapes=[pltpu.VMEM((N_PAIRS, D_STATE, LANE), F32), pltpu.VMEM((CHUNK, D_INNER), F32),
                        pltpu.VMEM((CHUNK, D_INNER), F32), pltpu.VMEM((CHUNK, D_INNER), F32),
                        pltpu.VMEM((8, D_INNER), F32)],
        compiler_params=_cp("arbitrary"),
    )(dyn, y, zx, xs, bc, dtr, st, prm, ng, expand, tri)


def _ssd_weights(W, j):
    w_in = W["ssm_in_w"][j]
    nzx = D_INNER + CONV_DIM
    wdt = jnp.pad(w_in[:, nzx:], ((0, 0), (0, LANE - N_HEADS)))
    prm = jnp.zeros((8, LANE), F32)
    prm = prm.at[0, :N_HEADS].set(W["ssm_dt_bias"][j]).at[1, :N_HEADS].set(W["ssm_A_log"][j])
    prm = prm.at[2, :N_HEADS].set(W["ssm_D"][j])
    return dict(wdt=wdt, cw=W["ssm_conv_w"][j], cb=W["ssm_conv_b"][j].reshape(1, CONV_DIM), prm=prm,
                ng=W["ssm_norm_g"][j].reshape(1, D_INNER))


def ssd_layer_fwd(h, W, j, tag):
    p = _ssd_weights(W, j)
    zx = mm(h, W["ssm_in_w"], layer=j, b_n=D_INNER + CONV_DIM, out_dtype=BF16, name=f"{tag}_zx")
    dtr = mm(h, p["wdt"], name=f"{tag}_dt")
    xs, pre_x = ssm_conv_fwd(zx, p["cw"], p["cb"], col0=D_INNER, ncols=D_INNER, wcol0=0, out_dtype=F32,
                             name=f"{tag}_convx")
    bc, pre_bc = ssm_conv_fwd(zx, p["cw"], p["cb"], col0=2 * D_INNER, ncols=2 * GN, wcol0=D_INNER, out_dtype=BF16,
                              name=f"{tag}_convbc")
    y, yn, st = ssd_fwd(xs, bc, dtr, zx, p["prm"], p["ng"], name=f"{tag}_scan")
    f = mm(yn, W["ssm_out_w"], layer=j, out_dtype=BF16, name=f"{tag}_out")
    return f, dict(h=h, zx=zx, dtr=dtr, xs=xs, bc=bc, pre_x=pre_x, pre_bc=pre_bc, y=y, yn=yn, st=st, p=p)


def ssd_layer_bwd(df, ctx, W, GB, j, tag):
    p = ctx["p"]
    h = ctx["h"]
    dyn = mm(df, W["ssm_out_w"], layer=j, tb=True, name=f"{tag}_b_dyn")
    GB["ssm_out_w"] = mm(ctx["yn"], df, ta=True, into=(GB["ssm_out_w"], j, 0), name=f"{tag}_b_gwo")
    dxbc, dz, ddtr, dng, dprm = ssd_bwd(dyn, ctx["y"], ctx["zx"], ctx["xs"], ctx["bc"], ctx["dtr"], ctx["st"],
                                        p["prm"], p["ng"], name=f"{tag}_b_scan")
    dx1, dcw1, dcb1 = ssm_conv_bwd(ctx["zx"], ctx["pre_x"], dxbc, p["cw"], col0=D_INNER, dcol0=0, ncols=D_INNER,
                                   name=f"{tag}_b_convx")
    dx2, dcw2, dcb2 = ssm_conv_bwd(ctx["zx"], ctx["pre_bc"], dxbc, p["cw"], col0=2 * D_INNER, dcol0=D_INNER,
                                   ncols=2 * GN, name=f"{tag}_b_convbc")
    dh = mm(dz, W["ssm_in_w"], layer=j, tb=True, b_k0=0, name=f"{tag}_b_dh1")
    dh = mm(dx1, W["ssm_in_w"], layer=j, tb=True, b_k0=D_INNER, acc=dh, name=f"{tag}_b_dh2")
    dh = mm(dx2, W["ssm_in_w"], layer=j, tb=True, b_k0=2 * D_INNER, acc=dh, name=f"{tag}_b_dh3")
    dh = mm(ddtr, p["wdt"], tb=True, acc=dh, out_dtype=BF16, name=f"{tag}_b_dh4")
    g_in = jnp.concatenate([mm(h, dz, ta=True, out_dtype=BF16, name=f"{tag}_b_gz"),
                            mm(h, dx1, ta=True, out_dtype=BF16, name=f"{tag}_b_gx"),
                            mm(h, dx2, ta=True, out_dtype=BF16, name=f"{tag}_b_gbc"),
                            mm(h, ddtr, ta=True, out_dtype=BF16, name=f"{tag}_b_gdt")[:, :N_HEADS]], axis=1)
    return dh, dict(ssm_in_w=g_in, ssm_conv_w=jnp.concatenate([dcw1, dcw2], axis=1),
                    ssm_conv_b=jnp.concatenate([dcb1, dcb2], axis=1)[0], ssm_dt_bias=dprm[0, :N_HEADS],
                    ssm_A_log=dprm[1, :N_HEADS], ssm_D=dprm[2, :N_HEADS], ssm_norm_g=dng[0])


def cf_layer_fwd(h, W, j, tag):
    u = mm(h, W["cf_pw1_w"], layer=j, bias=W["cf_pw1_b"][j], out_dtype=BF16, name=f"{tag}_pw1")
    c, s = cf_fwd(u, W["cf_dw_w"][j], W["cf_dw_b"][j].reshape(1, -1), W["cf_ln_g"][j].reshape(1, -1),
                  W["cf_ln_b"][j].reshape(1, -1), name=f"{tag}_conv")
    f = mm(s, W["cf_pw2_w"], layer=j, bias=W["cf_pw2_b"][j], out_dtype=BF16, name=f"{tag}_pw2")
    return f, dict(h=h, u=u, c=c, s=s)


def cf_layer_bwd(df, ctx, W, GB, j, tag):
    h = ctx["h"]
    ds = mm(df, W["cf_pw2_w"], layer=j, tb=True, name=f"{tag}_b_ds")
    GB["cf_pw2_w"] = mm(ctx["s"], df, ta=True, into=(GB["cf_pw2_w"], j, 0), name=f"{tag}_b_gpw2")
    g_b2 = colsum(df, name=f"{tag}_b_gb2")
    dc, dlg, dlb = cf_bwd_ln(ctx["c"], ds, W["cf_ln_g"][j].reshape(1, -1), W["cf_ln_b"][j].reshape(1, -1),
                             name=f"{tag}_b_ln")
    du, ddw, ddb = cf_bwd_conv(ctx["u"], dc, W["cf_dw_w"][j], name=f"{tag}_b_conv")
    dh = mm(du, W["cf_pw1_w"], layer=j, tb=True, out_dtype=BF16, name=f"{tag}_b_dh")
    GB["cf_pw1_w"] = mm(h, du, ta=True, into=(GB["cf_pw1_w"], j, 0), name=f"{tag}_b_gpw1")
    g_b1 = colsum(du, name=f"{tag}_b_gb1")
    return dh, dict(cf_pw1_b=g_b1[0], cf_dw_w=ddw, cf_dw_b=ddb[0], cf_ln_g=dlg[0], cf_ln_b=dlb[0], cf_pw2_b=g_b2[0])


def xa_layer_fwd(h, mem, W, i, tag):
    m = norm_fwd(mem, W["xa_mem_g"][i], name=f"{tag}_memnorm")
    kv = mm(m, W["xa_kv_w"], layer=i, out_dtype=BF16, name=f"{tag}_kv")
    q = mm(h, W["xa_q_w"], layer=i, out_dtype=BF16, name=f"{tag}_q")
    o = attn_fwd(q, kv, name=f"{tag}_attn")
    f = mm(o, W["xa_o_w"], layer=i, out_dtype=BF16, name=f"{tag}_o")
    return f, dict(h=h, m=m, kv=kv, q=q, o=o)


def xa_layer_bwd(df, ctx, mem, W, GB, i, tag):
    h = ctx["h"]
    do = mm(df, W["xa_o_w"], layer=i, tb=True, out_dtype=BF16, name=f"{tag}_b_do")
    GB["xa_o_w"] = mm(ctx["o"], df, ta=True, into=(GB["xa_o_w"], i, 0), name=f"{tag}_b_go")
    dq, dkv = attn_bwd(ctx["q"], ctx["kv"], do, name=f"{tag}_b_attn")
    dh = mm(dq, W["xa_q_w"], layer=i, tb=True, out_dtype=BF16, name=f"{tag}_b_dh")
    GB["xa_q_w"] = mm(h, dq, ta=True, into=(GB["xa_q_w"], i, 0), name=f"{tag}_b_gq")
    GB["xa_kv_w"] = mm(ctx["m"], dkv, ta=True, into=(GB["xa_kv_w"], i, 0), name=f"{tag}_b_gkv")
    dm = mm(dkv, W["xa_kv_w"], layer=i, tb=True, name=f"{tag}_b_dm")
    g_mg = norm_dg(mem, dm, name=f"{tag}_b_gmem")
    return dh, dict(xa_mem_g=g_mg[0])


def ffn_layer_fwd(h, W, i, tag):
    cw, cb = W["ffn_conv_w"][i], W["ffn_conv_b"][i].reshape(1, -1)
    u = mm(h, W["ffn_in_w"], layer=i, out_dtype=BF16, name=f"{tag}_in")
    act, c = ffn_act_fwd(u, cw, cb, name=f"{tag}_act")
    f = mm(act, W["ffn_out_w"], layer=i, out_dtype=BF16, name=f"{tag}_out")
    return f, dict(h=h, u=u, c=c, act=act)


def ffn_layer_bwd(df, ctx, W, GB, i, tag):
    h = ctx["h"]
    dact = mm(df, W["ffn_out_w"], layer=i, tb=True, out_dtype=BF16, name=f"{tag}_b_dact")
    GB["ffn_out_w"] = mm(ctx["act"], df, ta=True, into=(GB["ffn_out_w"], i, 0), name=f"{tag}_b_gout")
    du, dcw, dcb = ffn_act_bwd(ctx["u"], ctx["c"], dact, W["ffn_conv_w"][i], name=f"{tag}_b_act")
    dh = None
    for half in range(2):
        dh = mm(du, W["ffn_in_w"], a_idx=half, layer=i, tb=True, b_k0=half * D_FF, acc=dh,
                out_dtype=BF16 if half else F32, name=f"{tag}_b_dh{half}")
        GB["ffn_in_w"] = mm(h, du, ta=True, layer=half, into=(GB["ffn_in_w"], i, half * D_FF), name=f"{tag}_b_gin{half}")
    cat = lambda a: jnp.concatenate([a[0], a[1]], axis=-1)
    return dh, dict(ffn_conv_w=cat(dcw), ffn_conv_b=cat(dcb)[0])


def _sublayer_weights(i, s):
    if s == 0:
        return [("ssm_in_w", i // 2), ("ssm_out_w", i // 2)] if i % 2 == 0 else [("cf_pw1_w", i // 2), ("cf_pw2_w", i // 2)]
    return [(n, i) for n in (("xa_q_w", "xa_kv_w", "xa_o_w") if s == 1 else ("ffn_in_w", "ffn_out_w"))]


def local_step(x, mem, target, W, fetch=None, layer_done=None):
    subs = [(i, s) for i in range(DEPTH) for s in range(3)]
    ng = W["norm_g"]

    def fwd(i, s, h):
        tag = f"l{i}s{s}"
        if s == 0:
            return ssd_layer_fwd(h, W, i // 2, tag) if i % 2 == 0 else cf_layer_fwd(h, W, i // 2, tag)
        if s == 1:
            return xa_layer_fwd(h, mem, W, i, tag)
        return ffn_layer_fwd(h, W, i, tag)

    GB = {}

    def bwd(i, s, df, ctx):
        tag = f"l{i}s{s}"
        if s == 0:
            return (ssd_layer_bwd if i % 2 == 0 else cf_layer_bwd)(df, ctx, W, GB, i // 2, tag)
        if s == 1:
            return xa_layer_bwd(df, ctx, mem, W, GB, i, tag)
        return ffn_layer_bwd(df, ctx, W, GB, i, tag)

    h = norm_fwd(x, ng[0, 0], name="norm0")
    saved = []
    dxp = loss = None
    for k, (i, s) in enumerate(subs):
        if fetch is not None:
            fetch(i, s, x)
        f, ctx = fwd(i, s, h)
        saved.append((x, f, ctx))
        if k + 1 < len(subs):
            ni, ns = subs[k + 1]
            x, h = bnd_fwd(x, f, ng[i, 2 * s + 1], ng[ni, 2 * ns], name=f"bnd{k}")
        else:
            dxp, loss = final_fwd(x, f, ng[i, 2 * s + 1], target, name="final")

    for n in BIG:
        if n != "ssm_in_w":
            GB[n] = jnp.zeros((len(W[n]), *W[n][0].shape), BF16)
    grads = {}

    def put(name, idx, val):
        grads.setdefault(name, {})[idx] = val

    i, s = subs[-1]
    top = bnd_bwd(dxp, post=(saved[-1][1], ng[i, 2 * s + 1]), name="bbnd_top")
    put("norm_g", (i, 2 * s + 1), top["dgpost"][0])
    df = top["df"]
    for k in range(len(subs) - 1, -1, -1):
        i, s = subs[k]
        xk, _, ctx = saved[k]
        dh, gw = bwd(i, s, df, ctx)
        for name, val in gw.items():
            put(name, i // 2 if name.startswith(("ssm_", "cf_")) else i, val)
        dep = None
        if layer_done is not None:
            dep = layer_done(i, s, GB, grads["ssm_in_w"].pop(i // 2) if (s == 0 and i % 2 == 0) else None)
        if k > 0:
            pi, ps = subs[k - 1]
            r = bnd_bwd(dxp, pre=(xk, ng[i, 2 * s], dh), post=(saved[k - 1][1], ng[pi, 2 * ps + 1]), dep=dep,
                        name=f"bbnd{k}")
            put("norm_g", (pi, 2 * ps + 1), r["dgpost"][0])
            df = r["df"]
        else:
            r = bnd_bwd(dxp, pre=(xk, ng[i, 2 * s], dh), dep=dep, name="bbnd0")
        put("norm_g", (i, 2 * s), r["dgpre"][0])
        dxp = r["dx"]

    out = {} if layer_done is not None else dict(GB)
    for name, d in grads.items():
        if name == "norm_g":
            out[name] = jnp.stack([jnp.stack([d[(i, t)] for t in range(6)]) for i in range(DEPTH)])
        elif d:
            out[name] = jnp.stack([d[j] for j in sorted(d)])
    return loss, dxp, out


ANY = pl.BlockSpec(memory_space=pl.ANY)


def _pos():
    return lax.axis_index("x"), lax.axis_index("y"), lax.axis_index("c")


def all_gather(shard, name):
    R, C = shard.shape

    def body(x_ref, out_ref, send_sems, recv_sems, local_sem):
        x, y, c = _pos()
        me, sibling = (x, y, c), (x, y, 1 - c)
        chips = [(1 - x, y), (x, 1 - y), (1 - x, 1 - y)]

        def slot(px, py, pc):
            return out_ref.at[4 * px + 2 * py + pc]

        def copy(k, block, to, src=None):
            return pltpu.make_async_remote_copy(
                src_ref=slot(*block) if src is None else src, dst_ref=slot(*block),
                send_sem=send_sems.at[k], recv_sem=recv_sems.at[k], device_id=to, device_id_type=MESH)

        mine = pltpu.make_async_copy(x_ref, slot(*me), local_sem)
        mine.start()
        first = [copy(0, me, sibling, src=x_ref)]
        first += [copy(1 + j, me, (*chip, c), src=x_ref) for j, chip in enumerate(chips)]
        for cp in first:
            cp.start()
        passed = [copy(4 + j, (*chip, c), sibling) for j, chip in enumerate(chips)]
        for j, chip in enumerate(chips):
            copy(1 + j, (*chip, c), me).wait_recv()
            passed[j].start()
        copy(0, sibling, me).wait_recv()
        for j, chip in enumerate(chips):
            copy(4 + j, (*chip, 1 - c), me).wait_recv()
        for cp in first + passed:
            cp.wait_send()
        mine.wait()

    return pl.pallas_call(
        body, name=name, out_shape=jax.ShapeDtypeStruct((N_DEV, R, C), shard.dtype),
        in_specs=[ANY], out_specs=ANY,
        scratch_shapes=[pltpu.SemaphoreType.DMA((7,)), pltpu.SemaphoreType.DMA((7,)), pltpu.SemaphoreType.DMA(())],
    )(shard)


def _win(ref, kind, k, a, b):
    if kind == "lead":
        return ref.at[k]
    if kind == "row":
        return ref.at[:, pl.ds(pl.multiple_of(k * a, 16), a), :]
    return ref.at[:, :, pl.ds(pl.multiple_of(k * b, LANE), b)]


def _full_shape(shard_shape, kind):
    n, a, b = shard_shape
    return {"lead": (N_DEV, n, a, b), "row": (n, N_DEV * a, b), "col": (n, a, N_DEV * b)}[kind]


HBM = pl.BlockSpec(memory_space=pltpu.HBM)
SEMS = pl.BlockSpec(memory_space=pltpu.SEMAPHORE)
DATAFLOW = pltpu.SideEffectType.DATAFLOW_SIDE_EFFECTING
N_PEER = N_DEV - 1


def _in_hbm(a):
    return pltpu.with_memory_space_constraint(a, pltpu.HBM)


def _peer(x, y, c, r):
    return ((1 - x) if r & 4 else x, (1 - y) if r & 2 else y, (1 - c) if r & 1 else c)


def _win2(ref, kind, k, a, b):
    if kind == "lead":
        return ref.at[k]
    if kind == "row":
        return ref.at[pl.ds(pl.multiple_of(k * a, 16), a), :]
    return ref.at[:, pl.ds(pl.multiple_of(k * b, LANE), b)]


def _zone_shape(kind, a, b):
    return {"lead": (N_DEV, a, b), "row": (N_DEV * a, b), "col": (a, N_DEV * b)}[kind]


def gather_start(shards, items, after, name):
    ns, nz, na = len(shards), len(items), len(after)
    zones = [lax.empty(_zone_shape(kind, a, b), shards[w].dtype) for w, l, kind, a, b in items]

    def body(*refs):
        x_refs = refs[:ns]
        send_sems, recv_sems, local_sems = refs[ns + nz + na:ns + nz + na + 3]
        z_refs = refs[ns + nz + na + 3 + ns:ns + nz + na + 3 + ns + nz]
        token = refs[-1]
        x, y, c = _pos()
        me = 4 * x + 2 * y + c
        for t, (w, l, kind, a, b) in enumerate(items):
            mine = _win2(z_refs[t], kind, me, a, b)
            pltpu.make_async_copy(x_refs[w].at[l], mine, local_sems.at[t]).start()
            for r in range(1, N_DEV):
                pltpu.make_async_remote_copy(
                    src_ref=x_refs[w].at[l], dst_ref=mine,
                    send_sem=send_sems.at[N_PEER * t + r - 1], recv_sem=recv_sems.at[N_PEER * t + r - 1],
                    device_id=_peer(x, y, c, r), device_id_type=MESH).start()
        token[...] = jnp.zeros_like(token)

    n_sem = N_PEER * nz
    outs = pl.pallas_call(
        body, name=name,
        out_shape=(pltpu.SemaphoreType.DMA((n_sem,)), pltpu.SemaphoreType.DMA((n_sem,)), pltpu.SemaphoreType.DMA((nz,)),
                   *[pltpu.HBM(s.shape, s.dtype) for s in shards], *[pltpu.HBM(z.shape, z.dtype) for z in zones],
                   jax.ShapeDtypeStruct((8, LANE), F32)),
        in_specs=[HBM] * (ns + nz) + [pl.BlockSpec(memory_space=pl.ANY)] * na,
        out_specs=(SEMS, SEMS, SEMS, *[HBM] * (ns + nz), pl.BlockSpec(memory_space=pltpu.VMEM)),
        input_output_aliases={i: 3 + i for i in range(ns + nz)},
        compiler_params=pltpu.CompilerParams(has_side_effects=DATAFLOW),
    )(*[_in_hbm(s) for s in shards], *[_in_hbm(z) for z in zones], *after)
    return outs[:3], list(outs[3:3 + ns]), list(outs[3 + ns:3 + ns + nz]), outs[-1]


def gather_wait(zones, idx, items, sems, after, keep, name):
    nz, nk = len(zones), len(keep)

    def body(*refs):
        z_refs = refs[:nz]
        send_sems, recv_sems, local_sems = refs[nz:nz + 3]
        x, y, c = _pos()
        me = 4 * x + 2 * y + c
        for z_ref, t in zip(z_refs, idx):
            w, l, kind, a, b = items[t]
            mine = _win2(z_ref, kind, me, a, b)
            pltpu.make_async_copy(mine, mine, local_sems.at[t]).wait()
            for r in range(1, N_DEV):
                peer = _peer(x, y, c, r)
                cp = pltpu.make_async_remote_copy(
                    src_ref=mine, dst_ref=_win2(z_ref, kind, 4 * peer[0] + 2 * peer[1] + peer[2], a, b),
                    send_sem=send_sems.at[N_PEER * t + r - 1], recv_sem=recv_sems.at[N_PEER * t + r - 1],
                    device_id=peer, device_id_type=MESH)
                cp.wait_send()
                cp.wait_recv()

    outs = pl.pallas_call(
        body, name=name, out_shape=tuple(pltpu.HBM(z.shape, z.dtype) for z in zones),
        in_specs=[HBM] * nz + [SEMS] * 3 + [pl.BlockSpec(memory_space=pl.ANY)] * (1 + nk),
        out_specs=tuple([HBM] * nz), input_output_aliases={i: i for i in range(nz)},
        compiler_params=pltpu.CompilerParams(has_side_effects=DATAFLOW),
    )(*zones, *sems, after, *keep)
    return list(outs)


def gather_now(shards, kinds, name):
    nw = len(shards)
    geo = [s.shape[1:] for s in shards]

    def body(*refs):
        x_refs, o_refs = refs[:nw], refs[nw:2 * nw]
        send_sems, recv_sems, local_sems = refs[2 * nw:]
        x, y, c = _pos()
        me, sibling = (x, y, c), (x, y, 1 - c)
        chips = [(1 - x, y), (x, 1 - y), (1 - x, 1 - y)]

        def slot(w, px, py, pc):
            return _win(o_refs[w], kinds[w], 4 * px + 2 * py + pc, *geo[w])

        def copy(w, k, block, to, src=None):
            return pltpu.make_async_remote_copy(
                src_ref=slot(w, *block) if src is None else src, dst_ref=slot(w, *block),
                send_sem=send_sems.at[7 * w + k], recv_sem=recv_sems.at[7 * w + k], device_id=to, device_id_type=MESH)

        mine = [pltpu.make_async_copy(x_refs[w], slot(w, *me), local_sems.at[w]) for w in range(nw)]
        for cp in mine:
            cp.start()
        first = []
        for w in range(nw):
            first.append(copy(w, 0, me, sibling, src=x_refs[w]))
            first += [copy(w, 1 + j, me, (*chip, c), src=x_refs[w]) for j, chip in enumerate(chips)]
        for cp in first:
            cp.start()
        passed = []
        for w in range(nw):
            for j, chip in enumerate(chips):
                copy(w, 1 + j, (*chip, c), me).wait_recv()
                cp = copy(w, 4 + j, (*chip, c), sibling)
                cp.start()
                passed.append(cp)
        for w in range(nw):
            copy(w, 0, sibling, me).wait_recv()
            for j, chip in enumerate(chips):
                copy(w, 4 + j, (*chip, 1 - c), me).wait_recv()
        for cp in first + passed:
            cp.wait_send()
        for cp in mine:
            cp.wait()

    return pl.pallas_call(
        body, name=name,
        out_shape=[jax.ShapeDtypeStruct(_full_shape(s.shape, k), s.dtype) for s, k in zip(shards, kinds)],
        in_specs=[ANY] * nw, out_specs=[ANY] * nw,
        scratch_shapes=[pltpu.SemaphoreType.DMA((7 * nw,)), pltpu.SemaphoreType.DMA((7 * nw,)),
                        pltpu.SemaphoreType.DMA((nw,))],
    )(*shards)


def _src_win(ref, l, kind, k, a, b):
    return _win2(ref if l is None else ref.at[l], kind, k, a, b)


def rs_start(srcs, items, name):
    ns, nz = len(srcs), len(items)
    zones = [lax.empty((N_PEER, a, b), srcs[w].dtype) for w, l, kind, a, b in items]

    def body(*refs):
        s_refs = refs[:ns]
        send_sems, recv_sems = refs[ns + nz], refs[ns + nz + 1]
        z_refs = refs[ns + nz + 2 + ns:ns + nz + 2 + ns + nz]
        token = refs[-1]
        x, y, c = _pos()
        for t, (w, l, kind, a, b) in enumerate(items):
            for r in range(1, N_DEV):
                peer = _peer(x, y, c, r)
                pltpu.make_async_remote_copy(
                    src_ref=_src_win(s_refs[w], l, kind, 4 * peer[0] + 2 * peer[1] + peer[2], a, b),
                    dst_ref=z_refs[t].at[r - 1],
                    send_sem=send_sems.at[N_PEER * t + r - 1], recv_sem=recv_sems.at[N_PEER * t + r - 1],
                    device_id=peer, device_id_type=MESH).start()
        token[...] = jnp.zeros_like(token)

    n_sem = N_PEER * nz
    outs = pl.pallas_call(
        body, name=name,
        out_shape=(pltpu.SemaphoreType.DMA((n_sem,)), pltpu.SemaphoreType.DMA((n_sem,)),
                   *[pltpu.HBM(s.shape, s.dtype) for s in srcs], *[pltpu.HBM(z.shape, z.dtype) for z in zones],
                   jax.ShapeDtypeStruct((8, LANE), F32)),
        in_specs=[HBM] * (ns + nz), out_specs=(SEMS, SEMS, *[HBM] * (ns + nz), pl.BlockSpec(memory_space=pltpu.VMEM)),
        input_output_aliases={i: 2 + i for i in range(ns + nz)},
        compiler_params=pltpu.CompilerParams(has_side_effects=DATAFLOW),
    )(*[_in_hbm(s) for s in srcs], *[_in_hbm(z) for z in zones])
    return outs[:2], list(outs[2:2 + ns]), list(outs[2 + ns:2 + ns + nz]), outs[-1]


def rs_wait(zones, items, sems, after, keep, name):
    nz, nk = len(zones), len(keep)

    def body(*refs):
        z_refs = refs[:nz]
        send_sems, recv_sems = refs[nz], refs[nz + 1]
        x, y, c = _pos()
        for t, z_ref in enumerate(z_refs):
            for r in range(1, N_DEV):
                cp = pltpu.make_async_remote_copy(
                    src_ref=z_ref.at[r - 1], dst_ref=z_ref.at[r - 1],
                    send_sem=send_sems.at[N_PEER * t + r - 1], recv_sem=recv_sems.at[N_PEER * t + r - 1],
                    device_id=_peer(x, y, c, r), device_id_type=MESH)
                cp.wait_send()
                cp.wait_recv()

    outs = pl.pallas_call(
        body, name=name, out_shape=tuple(pltpu.HBM(z.shape, z.dtype) for z in zones),
        in_specs=[HBM] * nz + [SEMS] * 2 + [pl.BlockSpec(memory_space=pl.ANY)] * (1 + nk),
        out_specs=tuple([HBM] * nz), input_output_aliases={i: i for i in range(nz)},
        compiler_params=pltpu.CompilerParams(has_side_effects=DATAFLOW),
    )(*zones, *sems, after, *keep)
    return list(outs)


def adam_rs(w, m, v, l, own, kind, zone, outs, name):
    n, a, b = w.shape
    ta = max(t for t in range(16, min(a, 256) + 1, 16) if a % t == 0)
    per = a // ta
    me = (4 * lax.axis_index("x") + 2 * lax.axis_index("y") + lax.axis_index("c")).astype(jnp.int32).reshape(1)

    def body(me_ref, w_ref, m_ref, v_ref, own_ref, z_ref, i0, i1, i2, i3, g_ref, d_ref, m2_ref, v2_ref):
        gv = own_ref[...].astype(F32)
        for k in range(N_PEER):
            gv = gv + z_ref[k].astype(F32)
        m2 = ADAM_B1 * m_ref[...] + (1.0 - ADAM_B1) * gv
        v2 = ADAM_B2 * v_ref[...] + (1.0 - ADAM_B2) * (gv * gv)
        m_hat = m2 / (1.0 - ADAM_B1 ** ADAM_STEP)
        v_hat = v2 / (1.0 - ADAM_B2 ** ADAM_STEP)
        g_ref[...] = gv
        d_ref[...] = -ADAM_LR * (m_hat / (jnp.sqrt(v_hat) + ADAM_EPS) + ADAM_WD * w_ref[...])
        m2_ref[...] = m2
        v2_ref[...] = v2

    spec = pl.BlockSpec((None, ta, b), lambda r, me_ref: (l, r, 0))
    if kind == "lead":
        own_spec = pl.BlockSpec((None, ta, b), lambda r, me_ref: (me_ref[0], r, 0))
    elif kind == "row":
        own_spec = pl.BlockSpec((None, ta, b), lambda r, me_ref: (l, me_ref[0] * per + r, 0))
    else:
        own_spec = pl.BlockSpec((None, ta, b), lambda r, me_ref: (l, r, me_ref[0]))
    return pl.pallas_call(
        body, name=name, out_shape=[jax.ShapeDtypeStruct((n, a, b), F32)] * 4,
        grid_spec=pltpu.PrefetchScalarGridSpec(
            num_scalar_prefetch=1, grid=(per,),
            in_specs=[spec] * 3 + [own_spec, pl.BlockSpec((N_PEER, ta, b), lambda r, me_ref: (0, r, 0))] + [ANY] * 4,
            out_specs=[spec] * 4),
        input_output_aliases={6 + k: k for k in range(4)},
        compiler_params=_cp("parallel"),
    )(me, w, m, v, own, zone, *outs)


def small_exchange(sh, rep, name):
    _, Rs, C = sh.shape
    Rr = rep.shape[0]

    def body(sh_ref, rep_ref, sh_out, rep_out, send_sems, recv_sems, local_sems):
        x, y, c = _pos()
        me = 4 * x + 2 * y + c
        l1 = pltpu.make_async_copy(sh_ref.at[me], sh_out.at[me], local_sems.at[0])
        l2 = pltpu.make_async_copy(rep_ref, rep_out.at[me], local_sems.at[1])
        l1.start()
        l2.start()

        def flip(v, bit):
            return 1 - v if bit else v

        sends, recvs = [], []
        for r in range(1, N_DEV):
            peer = (flip(x, r & 4), flip(y, r & 2), flip(c, r & 1))
            pid = 4 * peer[0] + 2 * peer[1] + peer[2]
            k = 2 * (r - 1)
            mk = lambda src, dst, kk: pltpu.make_async_remote_copy(
                src_ref=src, dst_ref=dst, send_sem=send_sems.at[kk], recv_sem=recv_sems.at[kk],
                device_id=peer, device_id_type=MESH)
            sends += [mk(sh_ref.at[pid], sh_out.at[me], k), mk(rep_ref, rep_out.at[me], k + 1)]
            recvs += [mk(sh_ref.at[me], sh_out.at[pid], k), mk(rep_ref, rep_out.at[pid], k + 1)]
        for cp in sends:
            cp.start()
        for cp in recvs:
            cp.wait_recv()
        for cp in sends:
            cp.wait_send()
        l1.wait()
        l2.wait()

    n = 2 * (N_DEV - 1)
    return pl.pallas_call(
        body, name=name,
        out_shape=[jax.ShapeDtypeStruct((N_DEV, Rs, C), sh.dtype), jax.ShapeDtypeStruct((N_DEV, *rep.shape), rep.dtype)],
        in_specs=[ANY, ANY], out_specs=[ANY, ANY],
        scratch_shapes=[pltpu.SemaphoreType.DMA((n,)), pltpu.SemaphoreType.DMA((n,)), pltpu.SemaphoreType.DMA((2,))],
    )(sh, rep)


def adam_slots(w, m, v, slots, name):
    S, n, a, b = slots.shape
    ta = max(t for t in range(16, min(a, 512) + 1, 8)
             if a % t == 0 and t * S * b * slots.dtype.itemsize <= 4 * 1024 * 1024)

    def body(w_ref, m_ref, v_ref, s_ref, g_ref, d_ref, m2_ref, v2_ref):
        gv = s_ref[0].astype(F32)
        for k in range(1, S):
            gv = gv + s_ref[k].astype(F32)
        m2 = ADAM_B1 * m_ref[...] + (1.0 - ADAM_B1) * gv
        v2 = ADAM_B2 * v_ref[...] + (1.0 - ADAM_B2) * (gv * gv)
        m_hat = m2 / (1.0 - ADAM_B1 ** ADAM_STEP)
        v_hat = v2 / (1.0 - ADAM_B2 ** ADAM_STEP)
        g_ref[...] = gv
        d_ref[...] = -ADAM_LR * (m_hat / (jnp.sqrt(v_hat) + ADAM_EPS) + ADAM_WD * w_ref[...])
        m2_ref[...] = m2
        v2_ref[...] = v2

    spec = pl.BlockSpec((None, ta, b), lambda l, r: (l, r, 0))
    return pl.pallas_call(
        body, name=name, grid=(n, a // ta),
        in_specs=[spec] * 3 + [pl.BlockSpec((S, None, ta, b), lambda l, r: (0, l, r, 0))], out_specs=[spec] * 4,
        out_shape=[jax.ShapeDtypeStruct((n, a, b), F32)] * 4, compiler_params=_cp("parallel", "parallel"),
    )(w, m, v, slots)


WEIGHTS = ["norm_g", "ssm_in_w", "ssm_conv_w", "ssm_conv_b", "ssm_dt_bias", "ssm_A_log", "ssm_D", "ssm_norm_g",
           "ssm_out_w", "cf_pw1_w", "cf_pw1_b", "cf_dw_w", "cf_dw_b", "cf_ln_g", "cf_ln_b", "cf_pw2_w", "cf_pw2_b",
           "xa_mem_g", "xa_q_w", "xa_kv_w", "xa_o_w", "ffn_in_w", "ffn_conv_w", "ffn_conv_b", "ffn_out_w"]
ARGS = ["x", "mem"] + WEIGHTS + ["loss_target"] + ["m_" + n for n in WEIGHTS] + ["v_" + n for n in WEIGHTS]
BIG = {"ssm_in_w": "col", "ssm_out_w": "row", "cf_pw1_w": "col", "cf_pw2_w": "row", "xa_q_w": "row",
       "xa_kv_w": "col", "xa_o_w": "row", "ffn_in_w": "col", "ffn_out_w": "row"}
SMALL = ["norm_g", "ssm_conv_w", "cf_pw1_b", "cf_dw_w", "cf_dw_b", "cf_ln_g", "cf_ln_b", "cf_pw2_b", "ffn_conv_w"]
REP = ["ssm_conv_b", "ssm_dt_bias", "ssm_A_log", "ssm_D", "ssm_norm_g", "xa_mem_g", "ffn_conv_b"]
SMALL_W = 768
REP_W = 512


def _r8(n):
    return -(-n // 8) * 8


def _stack2d(arrs, wid):
    parts = []
    for a in arrs:
        r, c = a.shape[-2:]
        parts.append(jnp.pad(a, [(0, 0)] * (a.ndim - 2) + [(0, _r8(r) - r), (0, wid - c)]))
    return jnp.concatenate(parts, axis=-2)


def _unstack2d(buf, shapes2d):
    out, o = [], 0
    for r, c in shapes2d:
        out.append(buf[..., o:o + r, :c])
        o += _r8(r)
    return out


def _gathered_to_full(g):
    lead = g.shape[1:-1]
    return jnp.moveaxis(g, 0, -2).reshape(*lead, N_DEV * g.shape[-1])


def _full_to_slots(w):
    lead = w.shape[:-1]
    return jnp.moveaxis(w.reshape(*lead, N_DEV, w.shape[-1] // N_DEV), -2, 0)


def kernel(x, mem, norm_g, ssm_in_w, ssm_conv_w, ssm_conv_b, ssm_dt_bias, ssm_A_log, ssm_D, ssm_norm_g, ssm_out_w, cf_pw1_w, cf_pw1_b, cf_dw_w, cf_dw_b, cf_ln_g, cf_ln_b, cf_pw2_w, cf_pw2_b, xa_mem_g, xa_q_w, xa_kv_w, xa_o_w, ffn_in_w, ffn_conv_w, ffn_conv_b, ffn_out_w, loss_target, m_norm_g, m_ssm_in_w, m_ssm_conv_w, m_ssm_conv_b, m_ssm_dt_bias, m_ssm_A_log, m_ssm_D, m_ssm_norm_g, m_ssm_out_w, m_cf_pw1_w, m_cf_pw1_b, m_cf_dw_w, m_cf_dw_b, m_cf_ln_g, m_cf_ln_b, m_cf_pw2_w, m_cf_pw2_b, m_xa_mem_g, m_xa_q_w, m_xa_kv_w, m_xa_o_w, m_ffn_in_w, m_ffn_conv_w, m_ffn_conv_b, m_ffn_out_w, v_norm_g, v_ssm_in_w, v_ssm_conv_w, v_ssm_conv_b, v_ssm_dt_bias, v_ssm_A_log, v_ssm_D, v_ssm_norm_g, v_ssm_out_w, v_cf_pw1_w, v_cf_pw1_b, v_cf_dw_w, v_cf_dw_b, v_cf_ln_g, v_cf_ln_b, v_cf_pw2_w, v_cf_pw2_b, v_xa_mem_g, v_xa_q_w, v_xa_kv_w, v_xa_o_w, v_ffn_in_w, v_ffn_conv_w, v_ffn_conv_b, v_ffn_out_w):
    return _step(x, mem, norm_g, ssm_in_w, ssm_conv_w, ssm_conv_b, ssm_dt_bias, ssm_A_log, ssm_D, ssm_norm_g, ssm_out_w, cf_pw1_w, cf_pw1_b, cf_dw_w, cf_dw_b, cf_ln_g, cf_ln_b, cf_pw2_w, cf_pw2_b, xa_mem_g, xa_q_w, xa_kv_w, xa_o_w, ffn_in_w, ffn_conv_w, ffn_conv_b, ffn_out_w, loss_target, m_norm_g, m_ssm_in_w, m_ssm_conv_w, m_ssm_conv_b, m_ssm_dt_bias, m_ssm_A_log, m_ssm_D, m_ssm_norm_g, m_ssm_out_w, m_cf_pw1_w, m_cf_pw1_b, m_cf_dw_w, m_cf_dw_b, m_cf_ln_g, m_cf_ln_b, m_cf_pw2_w, m_cf_pw2_b, m_xa_mem_g, m_xa_q_w, m_xa_kv_w, m_xa_o_w, m_ffn_in_w, m_ffn_conv_w, m_ffn_conv_b, m_ffn_out_w, v_norm_g, v_ssm_in_w, v_ssm_conv_w, v_ssm_conv_b, v_ssm_dt_bias, v_ssm_A_log, v_ssm_D, v_ssm_norm_g, v_ssm_out_w, v_cf_pw1_w, v_cf_pw1_b, v_cf_dw_w, v_cf_dw_b, v_cf_ln_g, v_cf_ln_b, v_cf_pw2_w, v_cf_pw2_b, v_xa_mem_g, v_xa_q_w, v_xa_kv_w, v_xa_o_w, v_ffn_in_w, v_ffn_conv_w, v_ffn_conv_b, v_ffn_out_w)


def _step(*args):
    A = dict(zip(ARGS, args, strict=True))
    x, mem, target = A["x"][0], A["mem"][0], A["loss_target"][0]

    big = list(BIG)
    geo = [A[n].shape for n in big]
    kinds = ["row" if BIG[n] == "row" else ("col" if A[n].shape[-1] % LANE == 0 else "lead") for n in big]
    W = {n: A[n] for n in REP}
    small2d = [(A[n].size // A[n].shape[-1], A[n].shape[-1]) for n in SMALL]
    rep2d = [(A[n].size // REP_W, REP_W) if A[n].shape[-1] % REP_W == 0 else A[n].shape for n in REP] + [(1, 1)]
    stack_small = lambda pre: _stack2d([A[pre + n].reshape(rc) for n, rc in zip(SMALL, small2d)], SMALL_W)
    stack_rep = lambda pre: _stack2d([A[pre + n].reshape(rc) for n, rc in zip(REP, rep2d)] + [jnp.zeros((1, 1), F32)],
                                     REP_W)
    small_g = all_gather(stack_small(""), name="gather_small")
    for n, g in zip(SMALL, _unstack2d(small_g, small2d)):
        W[n] = _gathered_to_full(g.reshape(N_DEV, *A[n].shape))

    shards = [A[n].astype(BF16) for n in big]
    for n in big:
        W[n] = [None] * A[n].shape[0]
    first = _sublayer_weights(0, 0)
    got0 = gather_now([shards[big.index(n)][l:l + 1] for n, l in first], [kinds[big.index(n)] for n, l in first],
                      name="gather_first")
    for (n, l), g in zip(first, got0):
        W[n][l] = _gathered_to_full(g)[0] if kinds[big.index(n)] == "lead" else g[0]
    items, sub_items = [], {}
    for i in range(DEPTH):
        for s in range(3):
            sub_items[i, s] = []
            for n, l in _sublayer_weights(i, s) if (i, s) != (0, 0) else []:
                w = big.index(n)
                sub_items[i, s].append(len(items))
                items.append((w, l, kinds[w], *geo[w][1:]))
    sems, shards_thru, zones, token = gather_start(shards, items, [small_g, got0[0]], name="gather_start")
    x = x + token[0, 0]

    def fetch(i, s, x_in):
        ids = sub_items[i, s]
        if not ids:
            return
        got = gather_wait([zones[t] for t in ids], ids, items, sems, x_in, shards_thru if (i, s) == (DEPTH - 1, 2) else [],
                          name=f"gather_wait{i}{s}")
        for t, z in zip(ids, got):
            w, l, kind = items[t][:3]
            W[big[w]][l] = _gathered_to_full(z) if kind == "lead" else z

    sent = []
    final = {}

    def layer_done(i, s, GB, g_in):
        srcs, its = [], []
        for n, l in _sublayer_weights(i, s):
            w = big.index(n)
            if kinds[w] == "lead":
                srcs.append(_full_to_slots(g_in if n == "ssm_in_w" else GB[n][l]))
                its.append((len(srcs) - 1, None, "lead", *geo[w][1:], n, l))
            else:
                srcs.append(GB[n])
                its.append((len(srcs) - 1, l, kinds[w], *geo[w][1:], n, l))
        sems_i, thru, zones_i, token_i = rs_start(srcs, [it[:5] for it in its], name=f"rs_start{i}{s}")
        for it, s in zip(its, thru):
            if it[2] != "lead":
                GB[it[5]] = s
        sent.append((its, sems_i, [s for it, s in zip(its, thru) if it[2] == "lead"], zones_i))
        final["GB"] = GB
        return token_i

    loss, grad_x, G = local_step(x, mem, target, W, fetch, layer_done)

    sh = _stack2d([_full_to_slots(G[n]).reshape(N_DEV, *rc) for n, rc in zip(SMALL, small2d)], SMALL_W)
    rep = _stack2d([G[n].reshape(rc) for n, rc in zip(REP, rep2d)] + [loss[:, :1]], REP_W)
    sh_got, rep_got = small_exchange(sh, rep, name="small_exchange")

    res = {}
    GBf = final["GB"]
    bufs = {n: [lax.empty(A[n].shape, F32) for _ in range(4)] for n in big}
    for i, (its, sems_i, lead_srcs, zones_i) in enumerate(sent):
        keep = lead_srcs + [GBf[it[5]] for it in its if it[2] != "lead"]
        zones_i = rs_wait(zones_i, [it[:5] for it in its], sems_i, sh_got, keep, name=f"rs_wait{i}")
        lead_it = iter(lead_srcs)
        for it, z in zip(its, zones_i):
            n, l = it[5], it[6]
            own = next(lead_it) if it[2] == "lead" else GBf[n]
            bufs[n] = adam_rs(A[n], A["m_" + n], A["v_" + n], l, own, it[2], z, bufs[n], name=f"adam_{n}{l}")
    for n in big:
        res[n] = tuple(bufs[n])
    for names, shapes2d, stack, slots, tag in ((SMALL, small2d, stack_small, sh_got, "small"),
                                               (REP, rep2d, stack_rep, rep_got, "rep")):
        outs4 = adam_slots(stack("")[None], stack("m_")[None], stack("v_")[None], slots[:, None], name=f"adam_{tag}")
        parts = [_unstack2d(o[0], shapes2d) for o in outs4]
        for k, n in enumerate(names):
            res[n] = tuple(q[k].reshape(A[n].shape) for q in parts)
        if tag == "rep":
            total_loss = parts[0][-1][0, 0]

    outs = [total_loss, grad_x[None]]
    for k in range(4):
        outs += [res[n][k] for n in WEIGHTS]
    return tuple(outs)
```

```python
import jax
import jax.numpy as jnp
from jax import lax
from jax.experimental import pallas as pl
from jax.experimental.pallas import tpu as pltpu

F32 = jnp.float32
BF16 = jnp.bfloat16

D_MODEL = 1024
D_INNER = 2048
N_HEADS = 32
HEAD_DIM = 64
N_GROUPS = 4
D_STATE = 128
CHUNK = 128
CONV_DIM = 3072
SSM_K = 4
CF_K = 31
N_MEM = 256
XA_HEADS = 4
XA_HD = 256
D_FF = 2816
FFN_K = 3
EPS = 1e-6
DEPTH = 4
N_DEV = 8

ADAM_LR = 0.001
ADAM_B1 = 0.9
ADAM_B2 = 0.999
ADAM_EPS = 1e-08
ADAM_WD = 0.01
ADAM_STEP = 10

LANE = 128
VMEM_LIMIT = 56 * 1024 * 1024
NEG = -1e30
MESH = pl.DeviceIdType.MESH


def _cp(*sem):
    return pltpu.CompilerParams(dimension_semantics=sem if sem else None, vmem_limit_bytes=VMEM_LIMIT)


def _tile(n, cap):
    if n <= cap:
        return n
    best = 0
    for t in range(LANE, cap + 1, LANE):
        if n % t == 0:
            best = t
    assert best, (n, cap)
    return best


def _sig(x):
    return 1.0 / (1.0 + jnp.exp(-x))


def _split3(v):
    v0 = v.astype(BF16)
    r1 = v - v0.astype(F32)
    v1 = r1.astype(BF16)
    v2 = (r1 - v1.astype(F32)).astype(BF16)
    return v0, v1, v2


def _dot(a, b, ca=1, cb=0):
    return lax.dot_general(a, b, (((ca,), (cb,)), ((), ())), preferred_element_type=F32)


def _dot3(v, m, ca=1, cb=0):
    v0, v1, v2 = _split3(v)
    return _dot(v0, m, ca, cb) + _dot(v1, m, ca, cb) + _dot(v2, m, ca, cb)


def mm(a, b, *, ta=False, tb=False, bias=None, acc=None, out_dtype=F32, a_idx=None, layer=None, b_k0=0, b_n=None,
       into=None, dep=None, name):
    if isinstance(b, (list, tuple)):
        b, layer = b[layer], None
    if ta:
        K, M = a.shape[-2:]
    else:
        M, K = a.shape[-2:]
    N = b_n if b_n is not None else (b.shape[-2] if tb else b.shape[-1])
    assert (b.ndim == 3) == (layer is not None) and (a.ndim == 3) == (a_idx is not None)
    tm = _tile(M, 1536)
    tn = _tile(N, 1536)
    tk = _tile(K, 2048)
    nk = K // tk
    assert b_k0 % tk == 0 and b_k0 + K <= (b.shape[-1] if tb else b.shape[-2])
    kb = b_k0 // tk
    has_bias, has_acc = bias is not None, acc is not None
    if into is not None:
        out_dtype = into[0].dtype
        assert into[0].shape[1] == M and into[2] % tn == 0 and into[2] + N <= into[0].shape[2] and not has_acc

    def body(*refs):
        a_ref, b_ref = refs[0], refs[1]
        pos = 2
        bias_ref = acc_ref = None
        if has_bias:
            bias_ref = refs[pos]
            pos += 1
        if has_acc:
            acc_ref = refs[pos]
            pos += 1
        if into is not None:
            pos += 1
        if dep is not None:
            pos += 1
        o_ref = refs[pos]
        s_ref = refs[pos + 1] if nk > 1 else None
        p = _dot(a_ref[...].astype(BF16), b_ref[...].astype(BF16), 0 if ta else 1, 1 if tb else 0)

        def extras(v):
            if has_bias:
                v = v + bias_ref[...]
            if has_acc:
                v = v + acc_ref[...]
            return v

        if nk == 1:
            o_ref[...] = extras(p).astype(out_dtype)
        else:
            k = pl.program_id(2)

            @pl.when(k == 0)
            def _():
                s_ref[...] = extras(p)

            @pl.when(k > 0)
            def _():
                s_ref[...] += p

            @pl.when(k == nk - 1)
            def _():
                o_ref[...] = s_ref[...].astype(out_dtype)

    lead_a = () if a_idx is None else (a_idx,)
    lead_b = () if layer is None else (layer,)
    sq = lambda lead: (None,) * len(lead)
    if ta:
        a_spec = pl.BlockSpec((*sq(lead_a), tk, tm), lambda i, j, k: (*lead_a, k, i))
    else:
        a_spec = pl.BlockSpec((*sq(lead_a), tm, tk), lambda i, j, k: (*lead_a, i, k))
    if tb:
        b_spec = pl.BlockSpec((*sq(lead_b), tn, tk), lambda i, j, k: (*lead_b, j, k + kb))
    else:
        b_spec = pl.BlockSpec((*sq(lead_b), tk, tn), lambda i, j, k: (*lead_b, k + kb, j))
    in_specs, args = [a_spec, b_spec], [a, b]
    if has_bias:
        in_specs.append(pl.BlockSpec((1, tn), lambda i, j, k: (0, j)))
        args.append(bias.reshape(1, N).astype(F32))
    if has_acc:
        in_specs.append(pl.BlockSpec((tm, tn), lambda i, j, k: (i, j)))
        args.append(acc)
    if into is None:
        out_spec = pl.BlockSpec((tm, tn), lambda i, j, k: (i, j))
        out_shape = jax.ShapeDtypeStruct((M, N), out_dtype)
        aliases = {}
    else:
        buf, l, col0 = into
        cb = col0 // tn
        in_specs.append(pl.BlockSpec(memory_space=pl.ANY))
        args.append(buf)
        out_spec = pl.BlockSpec((None, tm, tn), lambda i, j, k: (l, i, j + cb))
        out_shape = jax.ShapeDtypeStruct(buf.shape, buf.dtype)
        aliases = {len(args) - 1: 0}
    if dep is not None:
        in_specs.append(pl.BlockSpec(memory_space=pl.ANY))
        args.append(dep)
    return pl.pallas_call(
        body, name=name, grid=(M // tm, N // tn, nk),
        in_specs=in_specs, out_specs=out_spec, out_shape=out_shape, input_output_aliases=aliases,
        scratch_shapes=[pltpu.VMEM((tm, tn), F32)] if nk > 1 else [],
        compiler_params=_cp("parallel", "parallel", "arbitrary"),
    )(*args)


def colsum(x, name):
    L, C = x.shape
    tr = _tile(L, 512)
    tc = _tile(C, 1024)

    def body(x_ref, o_ref):
        @pl.when(pl.program_id(1) == 0)
        def _():
            o_ref[...] = jnp.zeros_like(o_ref)

        o_ref[...] += jnp.sum(x_ref[...].astype(F32), axis=0, keepdims=True)

    return pl.pallas_call(
        body, name=name, grid=(C // tc, L // tr),
        in_specs=[pl.BlockSpec((tr, tc), lambda j, i: (i, j))],
        out_specs=pl.BlockSpec((1, tc), lambda j, i: (0, j)),
        out_shape=jax.ShapeDtypeStruct((1, C), F32),
        compiler_params=_cp("parallel", "arbitrary"),
    )(x)


TR = 256


def _row_spec(tr, w):
    return pl.BlockSpec((tr, w), lambda i: (i, 0))


def _vec_spec(w):
    return pl.BlockSpec((1, w), lambda i: (0, 0))


def _rms(v):
    return lax.rsqrt(jnp.mean(v * v, axis=-1, keepdims=True) + EPS)


def norm_fwd(x, g, name):
    L, D = x.shape
    tr = min(TR, L)

    def body(x_ref, g_ref, h_ref):
        xv = x_ref[...]
        h_ref[...] = (xv * _rms(xv) * g_ref[...]).astype(BF16)

    return pl.pallas_call(
        body, name=name, grid=(L // tr,),
        in_specs=[_row_spec(tr, D), _vec_spec(D)], out_specs=_row_spec(tr, D),
        out_shape=jax.ShapeDtypeStruct((L, D), BF16), compiler_params=_cp("parallel"),
    )(x, g.reshape(1, D))


def bnd_fwd(x, f, gpost, gpre, name):
    L, D = x.shape
    tr = min(TR, L)

    def body(x_ref, f_ref, gp_ref, gn_ref, xo_ref, h_ref):
        fv = f_ref[...].astype(F32)
        xn = x_ref[...] + fv * _rms(fv) * gp_ref[...]
        xo_ref[...] = xn
        h_ref[...] = (xn * _rms(xn) * gn_ref[...]).astype(BF16)

    return pl.pallas_call(
        body, name=name, grid=(L // tr,),
        in_specs=[_row_spec(tr, D), _row_spec(tr, D), _vec_spec(D), _vec_spec(D)],
        out_specs=[_row_spec(tr, D), _row_spec(tr, D)],
        out_shape=[jax.ShapeDtypeStruct((L, D), F32), jax.ShapeDtypeStruct((L, D), BF16)],
        compiler_params=_cp("parallel"),
    )(x, f, gpost.reshape(1, D), gpre.reshape(1, D))


def final_fwd(x, f, gpost, target, name):
    L, D = x.shape
    tr = min(TR, L)
    n = L // tr

    def body(x_ref, f_ref, gp_ref, t_ref, dy_ref, loss_ref, acc_ref):
        i = pl.program_id(0)

        @pl.when(i == 0)
        def _():
            acc_ref[...] = jnp.zeros_like(acc_ref)

        fv = f_ref[...].astype(F32)
        e = x_ref[...] + fv * _rms(fv) * gp_ref[...] - t_ref[...]
        dy_ref[...] = e * (1.0 / D)
        acc_ref[...] += jnp.sum(e * e, axis=0, keepdims=True)

        @pl.when(i == n - 1)
        def _():
            loss_ref[...] = jnp.full((1, LANE), 0.5 / D, F32) * jnp.sum(acc_ref[...])

    return pl.pallas_call(
        body, name=name, grid=(n,),
        in_specs=[_row_spec(tr, D), _row_spec(tr, D), _vec_spec(D), _row_spec(tr, D)],
        out_specs=[_row_spec(tr, D), _vec_spec(LANE)],
        out_shape=[jax.ShapeDtypeStruct((L, D), F32), jax.ShapeDtypeStruct((1, LANE), F32)],
        scratch_shapes=[pltpu.VMEM((1, D), F32)],
        compiler_params=_cp("arbitrary"),
    )(x, f, gpost.reshape(1, D), target)


def _rms_bwd(v, g, dy):
    r = _rms(v)
    vn = v * r
    dg = jnp.sum(dy * vn, axis=0, keepdims=True)
    dvn = dy * g
    dv = r * (dvn - vn * jnp.mean(dvn * vn, axis=-1, keepdims=True))
    return dv, dg


def bnd_bwd(dxp, *, pre=None, post=None, dep=None, name):
    L, D = dxp.shape
    tr = min(TR, L)
    has_pre, has_post = pre is not None, post is not None

    def body(*refs):
        pos = 0
        dxp_ref = refs[pos]; pos += 1
        if has_pre:
            x_ref, gpre_ref, dh_ref = refs[pos:pos + 3]; pos += 3
        if has_post:
            f_ref, gpost_ref = refs[pos:pos + 2]; pos += 2
        if dep is not None:
            pos += 1
        if has_pre:
            dx_ref, dgpre_ref = refs[pos:pos + 2]; pos += 2
        if has_post:
            df_ref, dgpost_ref = refs[pos:pos + 2]; pos += 2
        i = pl.program_id(0)
        dx = dxp_ref[...]
        if has_pre:
            d, dg = _rms_bwd(x_ref[...], gpre_ref[...], dh_ref[...].astype(F32))
            dx = dx + d
            dx_ref[...] = dx

            @pl.when(i == 0)
            def _():
                dgpre_ref[...] = jnp.zeros_like(dgpre_ref)

            dgpre_ref[...] += dg
        if has_post:
            d, dg = _rms_bwd(f_ref[...].astype(F32), gpost_ref[...], dx)
            df_ref[...] = d.astype(BF16)

            @pl.when(i == 0)
            def _():
                dgpost_ref[...] = jnp.zeros_like(dgpost_ref)

            dgpost_ref[...] += dg

    in_specs, args = [_row_spec(tr, D)], [dxp]
    out_specs, out_shape, names = [], [], []
    if has_pre:
        x, gpre, dh = pre
        in_specs += [_row_spec(tr, D), _vec_spec(D), _row_spec(tr, D)]
        args += [x, gpre.reshape(1, D), dh]
        out_specs += [_row_spec(tr, D), _vec_spec(D)]
        out_shape += [jax.ShapeDtypeStruct((L, D), F32), jax.ShapeDtypeStruct((1, D), F32)]
        names += ["dx", "dgpre"]
    if has_post:
        f, gpost = post
        in_specs += [_row_spec(tr, D), _vec_spec(D)]
        args += [f, gpost.reshape(1, D)]
        out_specs += [_row_spec(tr, D), _vec_spec(D)]
        out_shape += [jax.ShapeDtypeStruct((L, D), BF16), jax.ShapeDtypeStruct((1, D), F32)]
        names += ["df", "dgpost"]
    if dep is not None:
        in_specs.append(pl.BlockSpec(memory_space=pl.ANY))
        args.append(dep)
    outs = pl.pallas_call(
        body, name=name, grid=(L // tr,), in_specs=in_specs, out_specs=out_specs, out_shape=out_shape,
        compiler_params=_cp("arbitrary"),
    )(*args)
    return dict(zip(names, outs))


def norm_dg(x, dy, name):
    L, D = x.shape
    tr = min(TR, L)

    def body(x_ref, dy_ref, o_ref):
        @pl.when(pl.program_id(0) == 0)
        def _():
            o_ref[...] = jnp.zeros_like(o_ref)

        xv = x_ref[...]
        o_ref[...] += jnp.sum(dy_ref[...] * xv * _rms(xv), axis=0, keepdims=True)

    return pl.pallas_call(
        body, name=name, grid=(L // tr,),
        in_specs=[_row_spec(tr, D), _row_spec(tr, D)], out_specs=_vec_spec(D),
        out_shape=jax.ShapeDtypeStruct((1, D), F32), compiler_params=_cp("arbitrary"),
    )(x, dy)


HALO = 32


def _prev_halo_spec(tr, tc, col):
    per = tr // HALO
    return pl.BlockSpec((HALO, tc), lambda *g: (jnp.maximum(g[-1] * per - 1, 0), col(*g)))


def _fill_prev(scr, halo_val, blk_val, i, tr):
    scr[pl.ds(0, HALO), :] = jnp.where(i == 0, 0.0, halo_val)
    scr[pl.ds(HALO, tr), :] = blk_val


def _conv(scr, w_ref, K, tr):
    acc = None
    for k in range(K):
        term = scr[pl.ds(HALO - (K - 1) + k, tr), :] * w_ref[k:k + 1, :]
        acc = term if acc is None else acc + term
    return acc


def _shift_copies(scr, sh, rows):
    n = rows - 8
    for r in range(1, 8):
        sh[r - 1, pl.ds(0, n), :] = scr[pl.ds(r, n), :]


def _tap(scr, sh, off, tr):
    q, r = divmod(off, 8)
    return scr[pl.ds(off, tr), :] if r == 0 else sh[r - 1, pl.ds(8 * q, tr), :]


def ssm_conv_fwd(zx, w, b, *, col0, ncols, wcol0, out_dtype, name):
    L = zx.shape[0]
    tr = min(TR, L)
    tc = 1024
    cb, wb = col0 // tc, wcol0 // tc

    def body(x_ref, h_ref, w_ref, b_ref, o_ref, p_ref, scr):
        i = pl.program_id(1)
        _fill_prev(scr, h_ref[...].astype(F32), x_ref[...].astype(F32), i, tr)
        pre = _conv(scr, w_ref, SSM_K, tr) + b_ref[...]
        p_ref[...] = pre.astype(BF16)
        o_ref[...] = (pre * _sig(pre)).astype(out_dtype)

    out = pl.BlockSpec((tr, tc), lambda j, i: (i, j))
    return pl.pallas_call(
        body, name=name, grid=(ncols // tc, L // tr),
        in_specs=[pl.BlockSpec((tr, tc), lambda j, i: (i, j + cb)),
                  _prev_halo_spec(tr, tc, lambda j, i: j + cb),
                  pl.BlockSpec((SSM_K, tc), lambda j, i: (0, j + wb)),
                  pl.BlockSpec((1, tc), lambda j, i: (0, j + wb))],
        out_specs=[out, out],
        out_shape=[jax.ShapeDtypeStruct((L, ncols), out_dtype), jax.ShapeDtypeStruct((L, ncols), BF16)],
        scratch_shapes=[pltpu.VMEM((HALO + tr, tc), F32)],
        compiler_params=_cp("parallel", "parallel"),
    )(zx, zx, w, b)


def ssm_conv_bwd(zx, pre, d, w, *, col0, dcol0, ncols, name):
    L = zx.shape[0]
    tr = min(TR, L)
    tc = 1024
    cb, db_ = col0 // tc, dcol0 // tc
    n = L // tr
    per = tr // HALO
    last = L // HALO - 1

    def body(x_ref, p_ref, np_ref, d_ref, nd_ref, w_ref, dx_ref, dw_ref, db_ref, sd):
        i = pl.program_id(1)

        def dpre(p, dv):
            s = _sig(p)
            return dv * s * (1.0 + p * (1.0 - s))

        dp = dpre(p_ref[...].astype(F32), d_ref[...])
        sd[pl.ds(0, tr), :] = dp
        sd[pl.ds(tr, HALO), :] = jnp.where(i == n - 1, 0.0, dpre(np_ref[...].astype(F32), nd_ref[...]))

        @pl.when(i == 0)
        def _():
            dw_ref[...] = jnp.zeros_like(dw_ref)
            db_ref[...] = jnp.zeros_like(db_ref)

        xv = x_ref[...].astype(F32)
        acc = None
        for k in range(SSM_K):
            tk = sd[pl.ds(SSM_K - 1 - k, tr), :]
            term = tk * w_ref[k:k + 1, :]
            acc = term if acc is None else acc + term
            dw_ref[k:k + 1, :] += jnp.sum(xv * tk, axis=0, keepdims=True)
        dx_ref[...] = acc.astype(BF16)
        db_ref[...] += jnp.sum(dp, axis=0, keepdims=True)

    nxt = lambda i: jnp.minimum((i + 1) * per, last)
    return pl.pallas_call(
        body, name=name, grid=(ncols // tc, n),
        in_specs=[pl.BlockSpec((tr, tc), lambda j, i: (i, j + cb)),
                  pl.BlockSpec((tr, tc), lambda j, i: (i, j)),
                  pl.BlockSpec((HALO, tc), lambda j, i: (nxt(i), j)),
                  pl.BlockSpec((tr, tc), lambda j, i: (i, j + db_)),
                  pl.BlockSpec((HALO, tc), lambda j, i: (nxt(i), j + db_)),
                  pl.BlockSpec((SSM_K, tc), lambda j, i: (0, j + db_))],
        out_specs=[pl.BlockSpec((tr, tc), lambda j, i: (i, j)),
                   pl.BlockSpec((SSM_K, tc), lambda j, i: (0, j)),
                   pl.BlockSpec((1, tc), lambda j, i: (0, j))],
        out_shape=[jax.ShapeDtypeStruct((L, ncols), BF16), jax.ShapeDtypeStruct((SSM_K, ncols), F32),
                   jax.ShapeDtypeStruct((1, ncols), F32)],
        scratch_shapes=[pltpu.VMEM((tr + HALO, tc), F32)],
        compiler_params=_cp("parallel", "arbitrary"),
    )(zx, pre, pre, d, d, w)


FFN_TC = 1408


def ffn_act_fwd(u, w, b, name):
    L = u.shape[0]
    tr = min(TR, L)
    tc = FFN_TC
    nb = D_FF // tc

    def body(g_ref, hg_ref, v_ref, hv_ref, wg_ref, wv_ref, bg_ref, bv_ref, o_ref, c_ref, sg, sv):
        i = pl.program_id(1)
        _fill_prev(sg, hg_ref[...].astype(F32), g_ref[...].astype(F32), i, tr)
        _fill_prev(sv, hv_ref[...].astype(F32), v_ref[...].astype(F32), i, tr)
        ug = _conv(sg, wg_ref, FFN_K, tr) + bg_ref[...]
        uv = _conv(sv, wv_ref, FFN_K, tr) + bv_ref[...]
        c_ref[0] = ug.astype(BF16)
        c_ref[1] = uv.astype(BF16)
        o_ref[...] = (ug * _sig(ug) * uv).astype(BF16)

    blk = lambda off: pl.BlockSpec((tr, tc), lambda j, i: (i, j + off))
    wsp = lambda off: pl.BlockSpec((FFN_K, tc), lambda j, i: (0, j + off))
    bsp = lambda off: pl.BlockSpec((1, tc), lambda j, i: (0, j + off))
    return pl.pallas_call(
        body, name=name, grid=(nb, L // tr),
        in_specs=[blk(0), _prev_halo_spec(tr, tc, lambda j, i: j),
                  blk(nb), _prev_halo_spec(tr, tc, lambda j, i: j + nb),
                  wsp(0), wsp(nb), bsp(0), bsp(nb)],
        out_specs=[pl.BlockSpec((tr, tc), lambda j, i: (i, j)), pl.BlockSpec((2, tr, tc), lambda j, i: (0, i, j))],
        out_shape=[jax.ShapeDtypeStruct((L, D_FF), BF16), jax.ShapeDtypeStruct((2, L, D_FF), BF16)],
        scratch_shapes=[pltpu.VMEM((HALO + tr, tc), F32), pltpu.VMEM((HALO + tr, tc), F32)],
        compiler_params=_cp("parallel", "parallel"),
    )(u, u, u, u, w, w, b, b)


def ffn_act_bwd(u, c, dact, w, name):
    L = u.shape[0]
    tr = min(TR, L)
    tc = FFN_TC
    nb = D_FF // tc
    n = L // tr
    per = tr // HALO
    last = L // HALO - 1

    def body(g_ref, v_ref, c_ref, nc_ref, da_ref, nda_ref, wg_ref, wv_ref, du_ref, dw_ref, db_ref, dg_s, dv_s):
        i = pl.program_id(1)

        def grads(cg, cv, da):
            s = _sig(cg)
            return da * cv * s * (1.0 + cg * (1.0 - s)), da * cg * s

        dg, dv = grads(c_ref[0].astype(F32), c_ref[1].astype(F32), da_ref[...].astype(F32))
        ndg, ndv = grads(nc_ref[0].astype(F32), nc_ref[1].astype(F32), nda_ref[...].astype(F32))
        at_end = i == n - 1
        for half, (scr, d, nd, x_ref, w_ref) in enumerate(((dg_s, dg, ndg, g_ref, wg_ref), (dv_s, dv, ndv, v_ref, wv_ref))):
            scr[pl.ds(0, tr), :] = d
            scr[pl.ds(tr, HALO), :] = jnp.where(at_end, 0.0, nd)

            @pl.when(i == 0)
            def _():
                dw_ref[half] = jnp.zeros((FFN_K, tc), F32)
                db_ref[half] = jnp.zeros((1, tc), F32)

            xv = x_ref[...].astype(F32)
            acc = None
            for k in range(FFN_K):
                tk = scr[pl.ds(FFN_K - 1 - k, tr), :]
                term = tk * w_ref[k:k + 1, :]
                acc = term if acc is None else acc + term
                dw_ref[half, k:k + 1, :] += jnp.sum(xv * tk, axis=0, keepdims=True)
            du_ref[half] = acc.astype(BF16)
            db_ref[half] += jnp.sum(d, axis=0, keepdims=True)

    blk = lambda off: pl.BlockSpec((tr, tc), lambda j, i: (i, j + off))
    wsp = lambda off: pl.BlockSpec((FFN_K, tc), lambda j, i: (0, j + off))
    nxt = lambda i: jnp.minimum((i + 1) * per, last)
    return pl.pallas_call(
        body, name=name, grid=(nb, n),
        in_specs=[blk(0), blk(nb),
                  pl.BlockSpec((2, tr, tc), lambda j, i: (0, i, j)),
                  pl.BlockSpec((2, HALO, tc), lambda j, i: (0, nxt(i), j)),
                  pl.BlockSpec((tr, tc), lambda j, i: (i, j)),
                  pl.BlockSpec((HALO, tc), lambda j, i: (nxt(i), j)),
                  wsp(0), wsp(nb)],
        out_specs=[pl.BlockSpec((2, tr, tc), lambda j, i: (0, i, j)),
                   pl.BlockSpec((2, FFN_K, tc), lambda j, i: (0, 0, j)),
                   pl.BlockSpec((2, 1, tc), lambda j, i: (0, 0, j))],
        out_shape=[jax.ShapeDtypeStruct((2, L, D_FF), BF16), jax.ShapeDtypeStruct((2, FFN_K, D_FF), F32),
                   jax.ShapeDtypeStruct((2, 1, D_FF), F32)],
        scratch_shapes=[pltpu.VMEM((tr + HALO, tc), F32), pltpu.VMEM((tr + HALO, tc), F32)],
        compiler_params=_cp("parallel", "arbitrary"),
    )(u, u, c, c, dact, dact, w, w)


def _ln_stats(c):
    mu = jnp.mean(c, axis=-1, keepdims=True)
    cc = c - mu
    rstd = lax.rsqrt(jnp.mean(cc * cc, axis=-1, keepdims=True) + EPS)
    return cc * rstd, rstd


def cf_fwd(u, dw_w, dw_b, ln_g, ln_b, name):
    L = u.shape[0]
    D = D_MODEL
    tr = min(TR, L)

    def body(a_ref, ha_ref, g_ref, hg_ref, w_ref, b_ref, lg_ref, lb_ref, c_ref, s_ref, scr, sh):
        i = pl.program_id(0)
        glu_h = ha_ref[...].astype(F32) * _sig(hg_ref[...].astype(F32))
        glu = a_ref[...].astype(F32) * _sig(g_ref[...].astype(F32))
        _fill_prev(scr, glu_h, glu, i, tr)
        _shift_copies(scr, sh, HALO + tr)
        c = b_ref[...]
        for k in range(CF_K):
            c = c + _tap(scr, sh, HALO - (CF_K - 1) + k, tr) * w_ref[k:k + 1, :]
        c_ref[...] = c
        xhat, _ = _ln_stats(c)
        ln = xhat * lg_ref[...] + lb_ref[...]
        s_ref[...] = (ln * _sig(ln)).astype(BF16)

    per = tr // HALO
    halo = lambda col: pl.BlockSpec((HALO, D), lambda i: (jnp.maximum(i * per - 1, 0), col))
    return pl.pallas_call(
        body, name=name, grid=(L // tr,),
        in_specs=[pl.BlockSpec((tr, D), lambda i: (i, 0)), halo(0),
                  pl.BlockSpec((tr, D), lambda i: (i, 1)), halo(1),
                  pl.BlockSpec((CF_K, D), lambda i: (0, 0)), _vec_spec(D), _vec_spec(D), _vec_spec(D)],
        out_specs=[_row_spec(tr, D), _row_spec(tr, D)],
        out_shape=[jax.ShapeDtypeStruct((L, D), F32), jax.ShapeDtypeStruct((L, D), BF16)],
        scratch_shapes=[pltpu.VMEM((HALO + tr, D), F32), pltpu.VMEM((7, HALO + tr, D), F32)],
        compiler_params=_cp("parallel"),
    )(u, u, u, u, dw_w, dw_b, ln_g, ln_b)


def cf_bwd_ln(c, ds, ln_g, ln_b, name):
    L, D = c.shape
    tr = min(TR, L)

    def body(c_ref, ds_ref, lg_ref, lb_ref, dc_ref, dg_ref, db_ref):
        xhat, rstd = _ln_stats(c_ref[...])
        ln = xhat * lg_ref[...] + lb_ref[...]
        sg = _sig(ln)
        dln = ds_ref[...].astype(F32) * sg * (1.0 + ln * (1.0 - sg))

        @pl.when(pl.program_id(0) == 0)
        def _():
            dg_ref[...] = jnp.zeros_like(dg_ref)
            db_ref[...] = jnp.zeros_like(db_ref)

        dg_ref[...] += jnp.sum(dln * xhat, axis=0, keepdims=True)
        db_ref[...] += jnp.sum(dln, axis=0, keepdims=True)
        dxh = dln * lg_ref[...]
        dc_ref[...] = rstd * (dxh - jnp.mean(dxh, axis=-1, keepdims=True)
                              - xhat * jnp.mean(dxh * xhat, axis=-1, keepdims=True))

    return pl.pallas_call(
        body, name=name, grid=(L // tr,),
        in_specs=[_row_spec(tr, D), _row_spec(tr, D), _vec_spec(D), _vec_spec(D)],
        out_specs=[_row_spec(tr, D), _vec_spec(D), _vec_spec(D)],
        out_shape=[jax.ShapeDtypeStruct((L, D), F32), jax.ShapeDtypeStruct((1, D), F32),
                   jax.ShapeDtypeStruct((1, D), F32)],
        compiler_params=_cp("arbitrary"),
    )(c, ds, ln_g, ln_b)


def cf_bwd_conv(u, dc, dw_w, name):
    L = u.shape[0]
    D = D_MODEL
    tr = min(TR, L)
    n = L // tr

    def body(a_ref, g_ref, dc_ref, nx_ref, w_ref, du_ref, dw_ref, db_ref, sd, shd):
        i = pl.program_id(0)
        a = a_ref[...].astype(F32)
        sg = _sig(g_ref[...].astype(F32))
        glu = a * sg
        dcv = dc_ref[...]
        sd[pl.ds(0, tr), :] = dcv
        sd[pl.ds(tr, HALO), :] = jnp.where(i == n - 1, 0.0, nx_ref[...])
        _shift_copies(sd, shd, tr + HALO)

        @pl.when(i == 0)
        def _():
            dw_ref[...] = jnp.zeros_like(dw_ref)
            db_ref[...] = jnp.zeros_like(db_ref)

        dglu = None
        for k in range(CF_K):
            tk = _tap(sd, shd, CF_K - 1 - k, tr)
            term = tk * w_ref[k:k + 1, :]
            dglu = term if dglu is None else dglu + term
            dw_ref[k:k + 1, :] += jnp.sum(glu * tk, axis=0, keepdims=True)
        du_ref[:, 0:D] = (dglu * sg).astype(BF16)
        du_ref[:, D:2 * D] = (dglu * a * sg * (1.0 - sg)).astype(BF16)
        db_ref[...] += jnp.sum(dcv, axis=0, keepdims=True)

    per = tr // HALO
    last = L // HALO - 1
    return pl.pallas_call(
        body, name=name, grid=(n,),
        in_specs=[pl.BlockSpec((tr, D), lambda i: (i, 0)),
                  pl.BlockSpec((tr, D), lambda i: (i, 1)),
                  _row_spec(tr, D),
                  pl.BlockSpec((HALO, D), lambda i: (jnp.minimum((i + 1) * per, last), 0)),
                  pl.BlockSpec((CF_K, D), lambda i: (0, 0))],
        out_specs=[pl.BlockSpec((tr, 2 * D), lambda i: (i, 0)),
                   pl.BlockSpec((CF_K, D), lambda i: (0, 0)), _vec_spec(D)],
        out_shape=[jax.ShapeDtypeStruct((L, 2 * D), BF16), jax.ShapeDtypeStruct((CF_K, D), F32),
                   jax.ShapeDtypeStruct((1, D), F32)],
        scratch_shapes=[pltpu.VMEM((tr + HALO, D), F32), pltpu.VMEM((7, tr + HALO, D), F32)],
        compiler_params=_cp("arbitrary"),
    )(u, u, dc, dc, dw_w)


XA_TR = 512
XA_SCALE = XA_HD ** -0.5


def _xa_probs(qh, kh):
    s = _dot(qh, kh, 1, 1) * XA_SCALE
    p = jnp.exp(s - jnp.max(s, axis=-1, keepdims=True))
    return p / jnp.sum(p, axis=-1, keepdims=True)


def attn_fwd(q, kv, name):
    L, D = q.shape
    tr = min(XA_TR, L)

    def body(q_ref, kv_ref, o_ref):
        for hd in range(XA_HEADS):
            c = slice(hd * XA_HD, (hd + 1) * XA_HD)
            p = _xa_probs(q_ref[:, c], kv_ref[:, c])
            vh = kv_ref[:, D + hd * XA_HD:D + (hd + 1) * XA_HD]
            o_ref[:, c] = _dot(p.astype(BF16), vh).astype(BF16)

    return pl.pallas_call(
        body, name=name, grid=(L // tr,),
        in_specs=[_row_spec(tr, D), pl.BlockSpec((N_MEM, 2 * D), lambda i: (0, 0))],
        out_specs=_row_spec(tr, D), out_shape=jax.ShapeDtypeStruct((L, D), BF16),
        compiler_params=_cp("parallel"),
    )(q, kv)


def attn_bwd(q, kv, do, name):
    L, D = q.shape
    tr = min(XA_TR, L)

    def body(q_ref, kv_ref, do_ref, dq_ref, dkv_ref):
        @pl.when(pl.program_id(0) == 0)
        def _():
            dkv_ref[...] = jnp.zeros_like(dkv_ref)

        for hd in range(XA_HEADS):
            c = slice(hd * XA_HD, (hd + 1) * XA_HD)
            cv = slice(D + hd * XA_HD, D + (hd + 1) * XA_HD)
            qh, kh, vh, doh = q_ref[:, c], kv_ref[:, c], kv_ref[:, cv], do_ref[:, c]
            p = _xa_probs(qh, kh)
            dp = _dot(doh, vh, 1, 1)
            dkv_ref[:, cv] += _dot(p.astype(BF16), doh, 0, 0)
            ds = (p * (dp - jnp.sum(dp * p, axis=-1, keepdims=True)) * XA_SCALE).astype(BF16)
            dq_ref[:, c] = _dot(ds, kh).astype(BF16)
            dkv_ref[:, c] += _dot(ds, qh, 0, 0)

    return pl.pallas_call(
        body, name=name, grid=(L // tr,),
        in_specs=[_row_spec(tr, D), pl.BlockSpec((N_MEM, 2 * D), lambda i: (0, 0)), _row_spec(tr, D)],
        out_specs=[_row_spec(tr, D), pl.BlockSpec((N_MEM, 2 * D), lambda i: (0, 0))],
        out_shape=[jax.ShapeDtypeStruct((L, D), BF16), jax.ShapeDtypeStruct((N_MEM, 2 * D), F32)],
        compiler_params=_cp("arbitrary"),
    )(q, kv, do)


N_PAIRS = N_HEADS // 2
PAIRS_PER_GROUP = N_PAIRS // N_GROUPS
GN = N_GROUPS * D_STATE


def _softplus(x):
    t = jnp.exp(-jnp.abs(x))
    return jnp.maximum(x, 0.0) + jnp.where(t < 1e-4, t * (1.0 - 0.5 * t), jnp.log(1.0 + t))


def _dot3b(m, v, ca=1, cb=0):
    v0, v1, v2 = _split3(v)
    return _dot(m, v0, ca, cb) + _dot(m, v1, ca, cb) + _dot(m, v2, ca, cb)


def ssd_consts():
    h = lax.broadcasted_iota(jnp.int32, (LANE, D_INNER), 0)
    c = lax.broadcasted_iota(jnp.int32, (LANE, D_INNER), 1)
    expand = (c // HEAD_DIM == h).astype(BF16)
    r = lax.broadcasted_iota(jnp.int32, (CHUNK, CHUNK), 0)
    k = lax.broadcasted_iota(jnp.int32, (CHUNK, CHUNK), 1)
    tri = (k <= r).astype(BF16)
    return expand, tri


def _ssd_common(dtr_ref, prm_ref, e_ref, tri_ref):
    lane = lax.broadcasted_iota(jnp.int32, (CHUNK, LANE), 1)
    valid = lane < N_HEADS
    A = -jnp.exp(prm_ref[1:2, :])
    pre = dtr_ref[...] + prm_ref[0:1, :]
    dt = jnp.where(valid, _softplus(pre), 0.0)
    cs = _dot3b(tri_ref[...], dt * A)
    E = e_ref[...]
    dt_x = _dot3(dt, E)
    cs_x = _dot3(cs, E)
    csl_x = cs_x[CHUNK - 1:CHUNK, :]
    return dict(valid=valid, A=A, pre=pre, dt=dt, cs=cs, csT=cs.T, dt_x=dt_x, ecs_x=jnp.exp(cs_x),
                dend_x=jnp.exp(csl_x - cs_x), cd_x=jnp.exp(csl_x), D_x=_dot3(prm_ref[...], E)[2:3, :])


def ssd_fwd(xs, bc, dtr, zx, prm, ng, name):
    L = xs.shape[0]
    nc = L // CHUNK
    expand, tri = ssd_consts()

    def body(xs_ref, bc_ref, dtr_ref, z_ref, prm_ref, ng_ref, e_ref, tri_ref, y_ref, yn_ref, st_ref, state):
        @pl.when(pl.program_id(0) == 0)
        def _():
            state[...] = jnp.zeros_like(state)

        q = _ssd_common(dtr_ref, prm_ref, e_ref, tri_ref)
        cs, csT = q["cs"], q["csT"]
        xs_v = xs_ref[...]
        X = xs_v * q["dt_x"]
        Xb = X.astype(BF16)
        Xd = (X * q["dend_x"]).astype(BF16)
        ii = lax.broadcasted_iota(jnp.int32, (CHUNK, CHUNK), 0)
        jj = lax.broadcasted_iota(jnp.int32, (CHUNK, CHUNK), 1)
        tril = jj <= ii
        first = jj < HEAD_DIM
        for g in range(N_GROUPS):
            Bg = bc_ref[:, g * D_STATE:(g + 1) * D_STATE]
            Cg = bc_ref[:, GN + g * D_STATE:GN + (g + 1) * D_STATE]
            S = _dot(Cg, Bg, 1, 1)
            for pr in range(PAIRS_PER_GROUP):
                pair = g * PAIRS_PER_GROUP + pr
                cols = slice(pair * LANE, (pair + 1) * LANE)
                Xp = Xb[:, cols]
                ys = []
                for h in (2 * pair, 2 * pair + 1):
                    seg = cs[:, h:h + 1] - csT[h:h + 1, :]
                    M = (S * jnp.exp(jnp.where(tril, seg, NEG))).astype(BF16)
                    ys.append(_dot(M, Xp))
                prevT = state[pair]
                st_ref[0, pair] = prevT
                yoff = _dot(Cg, prevT.astype(BF16)) * q["ecs_x"][:, cols]
                y_ref[:, cols] = jnp.where(first, ys[0], ys[1]) + yoff + xs_v[:, cols] * q["D_x"][:, cols]
                state[pair] = prevT * q["cd_x"][:, cols] + _dot(Bg, Xd[:, cols], 0, 0)
        z = z_ref[...].astype(F32)
        gt = y_ref[...] * z * _sig(z)
        yn_ref[...] = (gt * _rms(gt) * ng_ref[...]).astype(BF16)

    row = lambda w: pl.BlockSpec((CHUNK, w), lambda c: (c, 0))
    const = lambda a: pl.BlockSpec(a.shape, lambda c: (0,) * a.ndim)
    return pl.pallas_call(
        body, name=name, grid=(nc,),
        in_specs=[row(D_INNER), row(2 * GN), row(LANE), row(D_INNER), const(prm), const(ng), const(expand), const(tri)],
        out_specs=[row(D_INNER), row(D_INNER), pl.BlockSpec((1, N_PAIRS, D_STATE, LANE), lambda c: (c, 0, 0, 0))],
        out_shape=[jax.ShapeDtypeStruct((L, D_INNER), F32), jax.ShapeDtypeStruct((L, D_INNER), BF16),
                   jax.ShapeDtypeStruct((nc, N_PAIRS, D_STATE, LANE), F32)],
        scratch_shapes=[pltpu.VMEM((N_PAIRS, D_STATE, LANE), F32)],
        compiler_params=_cp("arbitrary"),
    )(xs, bc, dtr, zx, prm, ng, expand, tri)


def ssd_bwd(dyn, y, zx, xs, bc, dtr, st, prm, ng, name):
    L = xs.shape[0]
    nc = L // CHUNK
    expand, tri = ssd_consts()

    def body(dyn_ref, y_ref, z_ref, xs_ref, bc_ref, dtr_ref, st_ref, prm_ref, ng_ref, e_ref, tri_ref,
             dxbc_ref, dz_ref, ddtr_ref, dng_ref, dprm_ref, dstate, g_cs, g_q, dX, g_row):
        step = pl.program_id(0)

        @pl.when(step == 0)
        def _():
            dstate[...] = jnp.zeros_like(dstate)
            dng_ref[...] = jnp.zeros_like(dng_ref)
            dprm_ref[...] = jnp.zeros_like(dprm_ref)
            g_row[...] = jnp.zeros_like(g_row)

        q = _ssd_common(dtr_ref, prm_ref, e_ref, tri_ref)
        cs, csT, E = q["cs"], q["csT"], e_ref[...]
        xs_v = xs_ref[...]
        X = xs_v * q["dt_x"]
        Xb = X.astype(BF16)
        Xd_f = X * q["dend_x"]
        Xd = Xd_f.astype(BF16)

        yv = y_ref[...]
        z = z_ref[...].astype(F32)
        sz = _sig(z)
        silu = z * sz
        gt = yv * silu
        r = _rms(gt)
        gn = gt * r
        dyn_v = dyn_ref[...]
        dng_ref[...] += jnp.sum(dyn_v * gn, axis=0, keepdims=True)
        dgn = dyn_v * ng_ref[...]
        dgt = r * (dgn - gn * jnp.mean(dgn * gn, axis=-1, keepdims=True))
        dY = dgt * silu
        dz_ref[...] = (dgt * yv * sz * (1.0 + z * (1.0 - sz))).astype(BF16)
        dYb = dY.astype(BF16)
        g_row[1:2, :] += jnp.sum(dY * xs_v, axis=0, keepdims=True)

        ii = lax.broadcasted_iota(jnp.int32, (CHUNK, CHUNK), 0)
        jj = lax.broadcasted_iota(jnp.int32, (CHUNK, CHUNK), 1)
        tril = jj <= ii
        triu = jj >= ii
        first = jj < HEAD_DIM
        lane_row = lax.broadcasted_iota(jnp.int32, (1, LANE), 1)
        sub_col = lax.broadcasted_iota(jnp.int32, (CHUNK, 1), 0)
        dcs_col = jnp.zeros((CHUNK, LANE), F32)
        dcs_rowT = jnp.zeros((LANE, CHUNK), F32)
        for g in range(N_GROUPS):
            Bg = bc_ref[:, g * D_STATE:(g + 1) * D_STATE]
            Cg = bc_ref[:, GN + g * D_STATE:GN + (g + 1) * D_STATE]
            S = _dot(Cg, Bg, 1, 1)
            ST = _dot(Bg, Cg, 1, 1)
            dS = jnp.zeros((CHUNK, CHUNK), F32)
            dCg = jnp.zeros((CHUNK, D_STATE), F32)
            dBg = jnp.zeros((CHUNK, D_STATE), F32)
            for pr in range(PAIRS_PER_GROUP):
                pair = g * PAIRS_PER_GROUP + pr
                cols = slice(pair * LANE, (pair + 1) * LANE)
                Xp = Xb[:, cols]
                dYp_f = dY[:, cols]
                dYp = dYb[:, cols]
                prevT = st_ref[0, pair]
                prevTb = prevT.astype(BF16)
                dst = dstate[pair]
                dstb = dst.astype(BF16)
                ecs_p = q["ecs_x"][:, cols]
                g_cs[:, cols] = dYp_f * (_dot(Cg, prevTb) * ecs_p)
                dWb = (dYp_f * ecs_p).astype(BF16)
                dprev = dst * q["cd_x"][:, cols] + _dot(Cg, dWb, 0, 0)
                dCg = dCg + _dot(dWb, prevTb, 1, 1)
                g_row[0:1, cols] = jnp.sum(dst * prevT, axis=0, keepdims=True)
                dXp = None
                for hh, h in enumerate((2 * pair, 2 * pair + 1)):
                    mine = first if hh == 0 else jnp.logical_not(first)
                    seg = cs[:, h:h + 1] - csT[h:h + 1, :]
                    lam = jnp.exp(jnp.where(tril, seg, NEG))
                    dM = _dot(jnp.where(mine, dYp, jnp.zeros_like(dYp)), Xp, 1, 1)
                    dS = dS + dM * lam
                    Gm = dM * (S * lam)
                    dcs_col = dcs_col + jnp.sum(Gm, axis=1, keepdims=True) * (lane_row == h).astype(F32)
                    dcs_rowT = dcs_rowT + (sub_col == h).astype(F32) * jnp.sum(Gm, axis=0, keepdims=True)
                    MT = (ST * jnp.exp(jnp.where(triu, -seg, NEG))).astype(BF16)
                    t = _dot(MT, dYp)
                    dXp = t if dXp is None else jnp.where(first, dXp, t)
                dXd = _dot(Bg, dstb)
                dBg = dBg + _dot(Xd[:, cols], dstb, 1, 1)
                g_q[:, cols] = dXd * Xd_f[:, cols]
                dX[:, cols] = dXp + dXd * q["dend_x"][:, cols]
                dstate[pair] = dprev
            dSb = dS.astype(BF16)
            dxbc_ref[:, D_INNER + g * D_STATE:D_INNER + (g + 1) * D_STATE] = dBg + _dot(dSb, Cg, 0, 0)
            dxbc_ref[:, D_INNER + GN + g * D_STATE:D_INNER + GN + (g + 1) * D_STATE] = dCg + _dot(dSb, Bg)
        dXv = dX[...]
        dxbc_ref[:, 0:D_INNER] = q["D_x"] * dY + dXv * q["dt_x"]
        r_dt = _dot3(dXv * xs_v, E, 1, 1)
        r_cs = _dot3(g_cs[...], E, 1, 1)
        r_q = _dot3(g_q[...], E, 1, 1)
        r_row = _dot3(g_row[...], E, 1, 1)
        cd = jnp.exp(cs[CHUNK - 1:CHUNK, :])
        dcs_last = jnp.sum(r_q, axis=0, keepdims=True) + r_row[0:1, :] * cd
        dcs = r_cs - r_q + dcs_col - dcs_rowT.T + jnp.where(sub_col == CHUNK - 1, dcs_last, 0.0)
        da = _dot3b(tri_ref[...], dcs, 0, 0)
        dpre = jnp.where(q["valid"], (r_dt + da * q["A"]) * _sig(q["pre"]), 0.0)
        ddtr_ref[...] = dpre
        dprm_ref[0:1, :] += jnp.sum(dpre, axis=0, keepdims=True)
        dprm_ref[1:2, :] += jnp.sum(da * q["dt"], axis=0, keepdims=True) * q["A"]
        dprm_ref[2:3, :] = r_row[1:2, :]

    rev = lambda w: pl.BlockSpec((CHUNK, w), lambda c: (nc - 1 - c, 0))
    const = lambda a: pl.BlockSpec(a.shape, lambda c: (0,) * a.ndim)
    return pl.pallas_call(
        body, name=name, grid=(nc,),
        in_specs=[rev(D_INNER), rev(D_INNER), rev(D_INNER), rev(D_INNER), rev(2 * GN), rev(LANE),
                  pl.BlockSpec((1, N_PAIRS, D_STATE, LANE), lambda c: (nc - 1 - c, 0, 0, 0)),
                  const(prm), const(ng), const(expand), const(tri)],
        out_specs=[rev(CONV_DIM), rev(D_INNER), rev(LANE),
                   pl.BlockSpec((1, D_INNER), lambda c: (0, 0)), pl.BlockSpec((8, LANE), lambda c: (0, 0))],
        out_shape=[jax.ShapeDtypeStruct((L, CONV_DIM), F32), jax.ShapeDtypeStruct((L, D_INNER), BF16),
                   jax.ShapeDtypeStruct((L, LANE), F32), jax.ShapeDtypeStruct((1, D_INNER), F32),
                   jax.ShapeDtypeStruct((8, LANE), F32)],
        scratch_shapes=[pltpu.VMEM((N_PAIRS, D_STATE, LANE), F32), pltpu.VMEM((CHUNK, D_INNER), F32),
                        pltpu.VMEM((CHUNK, D_INNER), F32), pltpu.VMEM((CHUNK, D_INNER), F32),
                        pltpu.VMEM((8, D_INNER), F32)],
        compiler_params=_cp("arbitrary"),
    )(dyn, y, zx, xs, bc, dtr, st, prm, ng, expand, tri)


def _ssd_weights(W, j):
    w_in = W["ssm_in_w"][j]
    nzx = D_INNER + CONV_DIM
    wdt = jnp.pad(w_in[:, nzx:], ((0, 0), (0, LANE - N_HEADS)))
    prm = jnp.zeros((8, LANE), F32)
    prm = prm.at[0, :N_HEADS].set(W["ssm_dt_bias"][j]).at[1, :N_HEADS].set(W["ssm_A_log"][j])
    prm = prm.at[2, :N_HEADS].set(W["ssm_D"][j])
    return dict(wdt=wdt, cw=W["ssm_conv_w"][j], cb=W["ssm_conv_b"][j].reshape(1, CONV_DIM), prm=prm,
                ng=W["ssm_norm_g"][j].reshape(1, D_INNER))


def ssd_layer_fwd(h, W, j, tag):
    p = _ssd_weights(W, j)
    zx = mm(h, W["ssm_in_w"], layer=j, b_n=D_INNER + CONV_DIM, out_dtype=BF16, name=f"{tag}_zx")
    dtr = mm(h, p["wdt"], name=f"{tag}_dt")
    xs, pre_x = ssm_conv_fwd(zx, p["cw"], p["cb"], col0=D_INNER, ncols=D_INNER, wcol0=0, out_dtype=F32,
                             name=f"{tag}_convx")
    bc, pre_bc = ssm_conv_fwd(zx, p["cw"], p["cb"], col0=2 * D_INNER, ncols=2 * GN, wcol0=D_INNER, out_dtype=BF16,
                              name=f"{tag}_convbc")
    y, yn, st = ssd_fwd(xs, bc, dtr, zx, p["prm"], p["ng"], name=f"{tag}_scan")
    f = mm(yn, W["ssm_out_w"], layer=j, out_dtype=BF16, name=f"{tag}_out")
    return f, dict(h=h, zx=zx, dtr=dtr, xs=xs, bc=bc, pre_x=pre_x, pre_bc=pre_bc, y=y, yn=yn, st=st, p=p)


def ssd_layer_bwd(df, ctx, W, GB, j, tag, done=None):
    p = ctx["p"]
    h = ctx["h"]
    dyn = mm(df, W["ssm_out_w"], layer=j, tb=True, name=f"{tag}_b_dyn")
    GB["ssm_out_w"] = mm(ctx["yn"], df, ta=True, into=(GB["ssm_out_w"], j, 0), name=f"{tag}_b_gwo")
    dxbc, dz, ddtr, dng, dprm = ssd_bwd(dyn, ctx["y"], ctx["zx"], ctx["xs"], ctx["bc"], ctx["dtr"], ctx["st"],
                                        p["prm"], p["ng"], name=f"{tag}_b_scan")
    dx1, dcw1, dcb1 = ssm_conv_bwd(ctx["zx"], ctx["pre_x"], dxbc, p["cw"], col0=D_INNER, dcol0=0, ncols=D_INNER,
                                   name=f"{tag}_b_convx")
    dx2, dcw2, dcb2 = ssm_conv_bwd(ctx["zx"], ctx["pre_bc"], dxbc, p["cw"], col0=2 * D_INNER, dcol0=D_INNER,
                                   ncols=2 * GN, name=f"{tag}_b_convbc")
    g_in = jnp.concatenate([mm(h, dz, ta=True, out_dtype=BF16, name=f"{tag}_b_gz"),
                            mm(h, dx1, ta=True, out_dtype=BF16, name=f"{tag}_b_gx"),
                            mm(h, dx2, ta=True, out_dtype=BF16, name=f"{tag}_b_gbc"),
                            mm(h, ddtr, ta=True, out_dtype=BF16, name=f"{tag}_b_gdt")[:, :N_HEADS]], axis=1)
    gw = dict(ssm_conv_w=jnp.concatenate([dcw1, dcw2], axis=1), ssm_conv_b=jnp.concatenate([dcb1, dcb2], axis=1)[0],
              ssm_dt_bias=dprm[0, :N_HEADS], ssm_A_log=dprm[1, :N_HEADS], ssm_D=dprm[2, :N_HEADS], ssm_norm_g=dng[0])
    if done is None:
        gw["ssm_in_w"] = g_in
        token = None
    else:
        token = done(g_in)
    dh = mm(dz, W["ssm_in_w"], layer=j, tb=True, b_k0=0, dep=token, name=f"{tag}_b_dh1")
    dh = mm(dx1, W["ssm_in_w"], layer=j, tb=True, b_k0=D_INNER, acc=dh, name=f"{tag}_b_dh2")
    dh = mm(dx2, W["ssm_in_w"], layer=j, tb=True, b_k0=2 * D_INNER, acc=dh, name=f"{tag}_b_dh3")
    dh = mm(ddtr, p["wdt"], tb=True, acc=dh, out_dtype=BF16, name=f"{tag}_b_dh4")
    return dh, gw


def cf_layer_fwd(h, W, j, tag):
    u = mm(h, W["cf_pw1_w"], layer=j, bias=W["cf_pw1_b"][j], out_dtype=BF16, name=f"{tag}_pw1")
    c, s = cf_fwd(u, W["cf_dw_w"][j], W["cf_dw_b"][j].reshape(1, -1), W["cf_ln_g"][j].reshape(1, -1),
                  W["cf_ln_b"][j].reshape(1, -1), name=f"{tag}_conv")
    f = mm(s, W["cf_pw2_w"], layer=j, bias=W["cf_pw2_b"][j], out_dtype=BF16, name=f"{tag}_pw2")
    return f, dict(h=h, u=u, c=c, s=s)


def cf_layer_bwd(df, ctx, W, GB, j, tag):
    h = ctx["h"]
    ds = mm(df, W["cf_pw2_w"], layer=j, tb=True, name=f"{tag}_b_ds")
    GB["cf_pw2_w"] = mm(ctx["s"], df, ta=True, into=(GB["cf_pw2_w"], j, 0), name=f"{tag}_b_gpw2")
    g_b2 = colsum(df, name=f"{tag}_b_gb2")
    dc, dlg, dlb = cf_bwd_ln(ctx["c"], ds, W["cf_ln_g"][j].reshape(1, -1), W["cf_ln_b"][j].reshape(1, -1),
                             name=f"{tag}_b_ln")
    du, ddw, ddb = cf_bwd_conv(ctx["u"], dc, W["cf_dw_w"][j], name=f"{tag}_b_conv")
    dh = mm(du, W["cf_pw1_w"], layer=j, tb=True, out_dtype=BF16, name=f"{tag}_b_dh")
    GB["cf_pw1_w"] = mm(h, du, ta=True, into=(GB["cf_pw1_w"], j, 0), name=f"{tag}_b_gpw1")
    g_b1 = colsum(du, name=f"{tag}_b_gb1")
    return dh, dict(cf_pw1_b=g_b1[0], cf_dw_w=ddw, cf_dw_b=ddb[0], cf_ln_g=dlg[0], cf_ln_b=dlb[0], cf_pw2_b=g_b2[0])


def xa_layer_fwd(h, mem, W, i, tag):
    m = norm_fwd(mem, W["xa_mem_g"][i], name=f"{tag}_memnorm")
    kv = mm(m, W["xa_kv_w"], layer=i, out_dtype=BF16, name=f"{tag}_kv")
    q = mm(h, W["xa_q_w"], layer=i, out_dtype=BF16, name=f"{tag}_q")
    o = attn_fwd(q, kv, name=f"{tag}_attn")
    f = mm(o, W["xa_o_w"], layer=i, out_dtype=BF16, name=f"{tag}_o")
    return f, dict(h=h, m=m, kv=kv, q=q, o=o)


def xa_layer_bwd(df, ctx, mem, W, GB, i, tag):
    h = ctx["h"]
    do = mm(df, W["xa_o_w"], layer=i, tb=True, out_dtype=BF16, name=f"{tag}_b_do")
    GB["xa_o_w"] = mm(ctx["o"], df, ta=True, into=(GB["xa_o_w"], i, 0), name=f"{tag}_b_go")
    dq, dkv = attn_bwd(ctx["q"], ctx["kv"], do, name=f"{tag}_b_attn")
    dh = mm(dq, W["xa_q_w"], layer=i, tb=True, out_dtype=BF16, name=f"{tag}_b_dh")
    GB["xa_q_w"] = mm(h, dq, ta=True, into=(GB["xa_q_w"], i, 0), name=f"{tag}_b_gq")
    GB["xa_kv_w"] = mm(ctx["m"], dkv, ta=True, into=(GB["xa_kv_w"], i, 0), name=f"{tag}_b_gkv")
    dm = mm(dkv, W["xa_kv_w"], layer=i, tb=True, name=f"{tag}_b_dm")
    g_mg = norm_dg(mem, dm, name=f"{tag}_b_gmem")
    return dh, dict(xa_mem_g=g_mg[0])


def ffn_layer_fwd(h, W, i, tag):
    cw, cb = W["ffn_conv_w"][i], W["ffn_conv_b"][i].reshape(1, -1)
    u = mm(h, W["ffn_in_w"], layer=i, out_dtype=BF16, name=f"{tag}_in")
    act, c = ffn_act_fwd(u, cw, cb, name=f"{tag}_act")
    f = mm(act, W["ffn_out_w"], layer=i, out_dtype=BF16, name=f"{tag}_out")
    return f, dict(h=h, u=u, c=c, act=act)


def ffn_layer_bwd(df, ctx, W, GB, i, tag):
    h = ctx["h"]
    dact = mm(df, W["ffn_out_w"], layer=i, tb=True, out_dtype=BF16, name=f"{tag}_b_dact")
    GB["ffn_out_w"] = mm(ctx["act"], df, ta=True, into=(GB["ffn_out_w"], i, 0), name=f"{tag}_b_gout")
    du, dcw, dcb = ffn_act_bwd(ctx["u"], ctx["c"], dact, W["ffn_conv_w"][i], name=f"{tag}_b_act")
    dh = None
    for half in range(2):
        dh = mm(du, W["ffn_in_w"], a_idx=half, layer=i, tb=True, b_k0=half * D_FF, acc=dh,
                out_dtype=BF16 if half else F32, name=f"{tag}_b_dh{half}")
        GB["ffn_in_w"] = mm(h, du, ta=True, layer=half, into=(GB["ffn_in_w"], i, half * D_FF), name=f"{tag}_b_gin{half}")
    cat = lambda a: jnp.concatenate([a[0], a[1]], axis=-1)
    return dh, dict(ffn_conv_w=cat(dcw), ffn_conv_b=cat(dcb)[0])


def _sublayer_weights(i, s):
    if s == 0:
        return [("ssm_in_w", i // 2), ("ssm_out_w", i // 2)] if i % 2 == 0 else [("cf_pw1_w", i // 2), ("cf_pw2_w", i // 2)]
    return [(n, i) for n in (("xa_q_w", "xa_kv_w", "xa_o_w") if s == 1 else ("ffn_in_w", "ffn_out_w"))]


def local_step(x, mem, target, W, fetch=None, layer_done=None):
    subs = [(i, s) for i in range(DEPTH) for s in range(3)]
    ng = W["norm_g"]

    def fwd(i, s, h):
        tag = f"l{i}s{s}"
        if s == 0:
            return ssd_layer_fwd(h, W, i // 2, tag) if i % 2 == 0 else cf_layer_fwd(h, W, i // 2, tag)
        if s == 1:
            return xa_layer_fwd(h, mem, W, i, tag)
        return ffn_layer_fwd(h, W, i, tag)

    GB = {}

    early = {}

    def bwd(i, s, df, ctx):
        tag = f"l{i}s{s}"
        if s == 0 and i % 2 == 0:
            def done(g_in):
                early[i, s] = layer_done(i, s, GB, g_in)
                return early[i, s]
            return ssd_layer_bwd(df, ctx, W, GB, i // 2, tag, done if layer_done is not None else None)
        if s == 0:
            return cf_layer_bwd(df, ctx, W, GB, i // 2, tag)
        if s == 1:
            return xa_layer_bwd(df, ctx, mem, W, GB, i, tag)
        return ffn_layer_bwd(df, ctx, W, GB, i, tag)

    h = norm_fwd(x, ng[0, 0], name="norm0")
    saved = []
    dxp = loss = None
    for k, (i, s) in enumerate(subs):
        if fetch is not None:
            fetch(i, s, x)
        f, ctx = fwd(i, s, h)
        saved.append((x, f, ctx))
        if k + 1 < len(subs):
            ni, ns = subs[k + 1]
            x, h = bnd_fwd(x, f, ng[i, 2 * s + 1], ng[ni, 2 * ns], name=f"bnd{k}")
        else:
            dxp, loss = final_fwd(x, f, ng[i, 2 * s + 1], target, name="final")

    for n in BIG:
        if n != "ssm_in_w":
            GB[n] = jnp.zeros((len(W[n]), *W[n][0].shape), BF16)
    grads = {}

    def put(name, idx, val):
        grads.setdefault(name, {})[idx] = val

    i, s = subs[-1]
    top = bnd_bwd(dxp, post=(saved[-1][1], ng[i, 2 * s + 1]), name="bbnd_top")
    put("norm_g", (i, 2 * s + 1), top["dgpost"][0])
    df = top["df"]
    for k in range(len(subs) - 1, -1, -1):
        i, s = subs[k]
        xk, _, ctx = saved[k]
        dh, gw = bwd(i, s, df, ctx)
        for name, val in gw.items():
            put(name, i // 2 if name.startswith(("ssm_", "cf_")) else i, val)
        dep = early.get((i, s))
        if layer_done is not None and dep is None:
            dep = layer_done(i, s, GB, None)
        if k > 0:
            pi, ps = subs[k - 1]
            r = bnd_bwd(dxp, pre=(xk, ng[i, 2 * s], dh), post=(saved[k - 1][1], ng[pi, 2 * ps + 1]), dep=dep,
                        name=f"bbnd{k}")
            put("norm_g", (pi, 2 * ps + 1), r["dgpost"][0])
            df = r["df"]
        else:
            r = bnd_bwd(dxp, pre=(xk, ng[i, 2 * s], dh), dep=dep, name="bbnd0")
        put("norm_g", (i, 2 * s), r["dgpre"][0])
        dxp = r["dx"]

    out = {} if layer_done is not None else dict(GB)
    for name, d in grads.items():
        if name == "norm_g":
            out[name] = jnp.stack([jnp.stack([d[(i, t)] for t in range(6)]) for i in range(DEPTH)])
        elif d:
            out[name] = jnp.stack([d[j] for j in sorted(d)])
    return loss, dxp, out


ANY = pl.BlockSpec(memory_space=pl.ANY)


def _pos():
    return lax.axis_index("x"), lax.axis_index("y"), lax.axis_index("c")


def all_gather(shard, name):
    R, C = shard.shape

    def body(x_ref, out_ref, send_sems, recv_sems, local_sem):
        x, y, c = _pos()
        me, sibling = (x, y, c), (x, y, 1 - c)
        chips = [(1 - x, y), (x, 1 - y), (1 - x, 1 - y)]

        def slot(px, py, pc):
            return out_ref.at[4 * px + 2 * py + pc]

        def copy(k, block, to, src=None):
            return pltpu.make_async_remote_copy(
                src_ref=slot(*block) if src is None else src, dst_ref=slot(*block),
                send_sem=send_sems.at[k], recv_sem=recv_sems.at[k], device_id=to, device_id_type=MESH)

        mine = pltpu.make_async_copy(x_ref, slot(*me), local_sem)
        mine.start()
        first = [copy(0, me, sibling, src=x_ref)]
        first += [copy(1 + j, me, (*chip, c), src=x_ref) for j, chip in enumerate(chips)]
        for cp in first:
            cp.start()
        passed = [copy(4 + j, (*chip, c), sibling) for j, chip in enumerate(chips)]
        for j, chip in enumerate(chips):
            copy(1 + j, (*chip, c), me).wait_recv()
            passed[j].start()
        copy(0, sibling, me).wait_recv()
        for j, chip in enumerate(chips):
            copy(4 + j, (*chip, 1 - c), me).wait_recv()
        for cp in first + passed:
            cp.wait_send()
        mine.wait()

    return pl.pallas_call(
        body, name=name, out_shape=jax.ShapeDtypeStruct((N_DEV, R, C), shard.dtype),
        in_specs=[ANY], out_specs=ANY,
        scratch_shapes=[pltpu.SemaphoreType.DMA((7,)), pltpu.SemaphoreType.DMA((7,)), pltpu.SemaphoreType.DMA(())],
    )(shard)


def _win(ref, kind, k, a, b):
    if kind == "lead":
        return ref.at[k]
    if kind == "row":
        return ref.at[:, pl.ds(pl.multiple_of(k * a, 16), a), :]
    return ref.at[:, :, pl.ds(pl.multiple_of(k * b, LANE), b)]


def _full_shape(shard_shape, kind):
    n, a, b = shard_shape
    return {"lead": (N_DEV, n, a, b), "row": (n, N_DEV * a, b), "col": (n, a, N_DEV * b)}[kind]


HBM = pl.BlockSpec(memory_space=pltpu.HBM)
SEMS = pl.BlockSpec(memory_space=pltpu.SEMAPHORE)
DATAFLOW = pltpu.SideEffectType.DATAFLOW_SIDE_EFFECTING
N_PEER = N_DEV - 1


def _in_hbm(a):
    return pltpu.with_memory_space_constraint(a, pltpu.HBM)


def _peer(x, y, c, r):
    return ((1 - x) if r & 4 else x, (1 - y) if r & 2 else y, (1 - c) if r & 1 else c)


def _win2(ref, kind, k, a, b):
    if kind == "lead":
        return ref.at[k]
    if kind == "row":
        return ref.at[pl.ds(pl.multiple_of(k * a, 16), a), :]
    return ref.at[:, pl.ds(pl.multiple_of(k * b, LANE), b)]


def _zone_shape(kind, a, b):
    return {"lead": (N_DEV, a, b), "row": (N_DEV * a, b), "col": (a, N_DEV * b)}[kind]


def gather_start(shards, items, after, name):
    ns, nz, na = len(shards), len(items), len(after)
    zones = [lax.empty(_zone_shape(kind, a, b), shards[w].dtype) for w, l, kind, a, b in items]

    def body(*refs):
        x_refs = refs[:ns]
        send_sems, recv_sems, local_sems = refs[ns + nz + na:ns + nz + na + 3]
        z_refs = refs[ns + nz + na + 3 + ns:ns + nz + na + 3 + ns + nz]
        token = refs[-1]
        x, y, c = _pos()
        me = 4 * x + 2 * y + c
        for t, (w, l, kind, a, b) in enumerate(items):
            mine = _win2(z_refs[t], kind, me, a, b)
            pltpu.make_async_copy(x_refs[w].at[l], mine, local_sems.at[t]).start()
            for r in range(1, N_DEV):
                pltpu.make_async_remote_copy(
                    src_ref=x_refs[w].at[l], dst_ref=mine,
                    send_sem=send_sems.at[N_PEER * t + r - 1], recv_sem=recv_sems.at[N_PEER * t + r - 1],
                    device_id=_peer(x, y, c, r), device_id_type=MESH).start()
        token[...] = jnp.zeros_like(token)

    n_sem = N_PEER * nz
    outs = pl.pallas_call(
        body, name=name,
        out_shape=(pltpu.SemaphoreType.DMA((n_sem,)), pltpu.SemaphoreType.DMA((n_sem,)), pltpu.SemaphoreType.DMA((nz,)),
                   *[pltpu.HBM(s.shape, s.dtype) for s in shards], *[pltpu.HBM(z.shape, z.dtype) for z in zones],
                   jax.ShapeDtypeStruct((8, LANE), F32)),
        in_specs=[HBM] * (ns + nz) + [pl.BlockSpec(memory_space=pl.ANY)] * na,
        out_specs=(SEMS, SEMS, SEMS, *[HBM] * (ns + nz), pl.BlockSpec(memory_space=pltpu.VMEM)),
        input_output_aliases={i: 3 + i for i in range(ns + nz)},
        compiler_params=pltpu.CompilerParams(has_side_effects=DATAFLOW),
    )(*[_in_hbm(s) for s in shards], *[_in_hbm(z) for z in zones], *after)
    return outs[:3], list(outs[3:3 + ns]), list(outs[3 + ns:3 + ns + nz]), outs[-1]


def gather_wait(zones, idx, items, sems, after, keep, name):
    nz, nk = len(zones), len(keep)

    def body(*refs):
        z_refs = refs[:nz]
        send_sems, recv_sems, local_sems = refs[nz:nz + 3]
        x, y, c = _pos()
        me = 4 * x + 2 * y + c
        for z_ref, t in zip(z_refs, idx):
            w, l, kind, a, b = items[t]
            mine = _win2(z_ref, kind, me, a, b)
            pltpu.make_async_copy(mine, mine, local_sems.at[t]).wait()
            for r in range(1, N_DEV):
                peer = _peer(x, y, c, r)
                cp = pltpu.make_async_remote_copy(
                    src_ref=mine, dst_ref=_win2(z_ref, kind, 4 * peer[0] + 2 * peer[1] + peer[2], a, b),
                    send_sem=send_sems.at[N_PEER * t + r - 1], recv_sem=recv_sems.at[N_PEER * t + r - 1],
                    device_id=peer, device_id_type=MESH)
                cp.wait_send()
                cp.wait_recv()

    outs = pl.pallas_call(
        body, name=name, out_shape=tuple(pltpu.HBM(z.shape, z.dtype) for z in zones),
        in_specs=[HBM] * nz + [SEMS] * 3 + [pl.BlockSpec(memory_space=pl.ANY)] * (1 + nk),
        out_specs=tuple([HBM] * nz), input_output_aliases={i: i for i in range(nz)},
        compiler_params=pltpu.CompilerParams(has_side_effects=DATAFLOW),
    )(*zones, *sems, after, *keep)
    return list(outs)


def gather_now(shards, kinds, name):
    nw = len(shards)
    geo = [s.shape[1:] for s in shards]

    def body(*refs):
        x_refs, o_refs = refs[:nw], refs[nw:2 * nw]
        send_sems, recv_sems, local_sems = refs[2 * nw:]
        x, y, c = _pos()
        me, sibling = (x, y, c), (x, y, 1 - c)
        chips = [(1 - x, y), (x, 1 - y), (1 - x, 1 - y)]

        def slot(w, px, py, pc):
            return _win(o_refs[w], kinds[w], 4 * px + 2 * py + pc, *geo[w])

        def copy(w, k, block, to, src=None):
            return pltpu.make_async_remote_copy(
                src_ref=slot(w, *block) if src is None else src, dst_ref=slot(w, *block),
                send_sem=send_sems.at[7 * w + k], recv_sem=recv_sems.at[7 * w + k], device_id=to, device_id_type=MESH)

        mine = [pltpu.make_async_copy(x_refs[w], slot(w, *me), local_sems.at[w]) for w in range(nw)]
        for cp in mine:
            cp.start()
        first = []
        for w in range(nw):
            first.append(copy(w, 0, me, sibling, src=x_refs[w]))
            first += [copy(w, 1 + j, me, (*chip, c), src=x_refs[w]) for j, chip in enumerate(chips)]
        for cp in first:
            cp.start()
        passed = []
        for w in range(nw):
            for j, chip in enumerate(chips):
                copy(w, 1 + j, (*chip, c), me).wait_recv()
                cp = copy(w, 4 + j, (*chip, c), sibling)
                cp.start()
                passed.append(cp)
        for w in range(nw):
            copy(w, 0, sibling, me).wait_recv()
            for j, chip in enumerate(chips):
                copy(w, 4 + j, (*chip, 1 - c), me).wait_recv()
        for cp in first + passed:
            cp.wait_send()
        for cp in mine:
            cp.wait()

    return pl.pallas_call(
        body, name=name,
        out_shape=[jax.ShapeDtypeStruct(_full_shape(s.shape, k), s.dtype) for s, k in zip(shards, kinds)],
        in_specs=[ANY] * nw, out_specs=[ANY] * nw,
        scratch_shapes=[pltpu.SemaphoreType.DMA((7 * nw,)), pltpu.SemaphoreType.DMA((7 * nw,)),
                        pltpu.SemaphoreType.DMA((nw,))],
    )(*shards)


def _src_win(ref, l, kind, k, a, b):
    return _win2(ref if l is None else ref.at[l], kind, k, a, b)


def rs_start(srcs, items, name):
    ns, nz = len(srcs), len(items)
    zones = [lax.empty((N_PEER, a, b), srcs[w].dtype) for w, l, kind, a, b in items]

    def body(*refs):
        s_refs = refs[:ns]
        send_sems, recv_sems = refs[ns + nz], refs[ns + nz + 1]
        z_refs = refs[ns + nz + 2 + ns:ns + nz + 2 + ns + nz]
        token = refs[-1]
        x, y, c = _pos()
        for t, (w, l, kind, a, b) in enumerate(items):
            for r in range(1, N_DEV):
                peer = _peer(x, y, c, r)
                pltpu.make_async_remote_copy(
                    src_ref=_src_win(s_refs[w], l, kind, 4 * peer[0] + 2 * peer[1] + peer[2], a, b),
                    dst_ref=z_refs[t].at[r - 1],
                    send_sem=send_sems.at[N_PEER * t + r - 1], recv_sem=recv_sems.at[N_PEER * t + r - 1],
                    device_id=peer, device_id_type=MESH).start()
        token[...] = jnp.zeros_like(token)

    n_sem = N_PEER * nz
    outs = pl.pallas_call(
        body, name=name,
        out_shape=(pltpu.SemaphoreType.DMA((n_sem,)), pltpu.SemaphoreType.DMA((n_sem,)),
                   *[pltpu.HBM(s.shape, s.dtype) for s in srcs], *[pltpu.HBM(z.shape, z.dtype) for z in zones],
                   jax.ShapeDtypeStruct((8, LANE), F32)),
        in_specs=[HBM] * (ns + nz), out_specs=(SEMS, SEMS, *[HBM] * (ns + nz), pl.BlockSpec(memory_space=pltpu.VMEM)),
        input_output_aliases={i: 2 + i for i in range(ns + nz)},
        compiler_params=pltpu.CompilerParams(has_side_effects=DATAFLOW),
    )(*[_in_hbm(s) for s in srcs], *[_in_hbm(z) for z in zones])
    return outs[:2], list(outs[2:2 + ns]), list(outs[2 + ns:2 + ns + nz]), outs[-1]


def rs_wait(zones, items, sems, after, keep, name):
    nz, nk = len(zones), len(keep)

    def body(*refs):
        z_refs = refs[:nz]
        send_sems, recv_sems = refs[nz], refs[nz + 1]
        x, y, c = _pos()
        for t, z_ref in enumerate(z_refs):
            for r in range(1, N_DEV):
                cp = pltpu.make_async_remote_copy(
                    src_ref=z_ref.at[r - 1], dst_ref=z_ref.at[r - 1],
                    send_sem=send_sems.at[N_PEER * t + r - 1], recv_sem=recv_sems.at[N_PEER * t + r - 1],
                    device_id=_peer(x, y, c, r), device_id_type=MESH)
                cp.wait_send()
                cp.wait_recv()

    outs = pl.pallas_call(
        body, name=name, out_shape=tuple(pltpu.HBM(z.shape, z.dtype) for z in zones),
        in_specs=[HBM] * nz + [SEMS] * 2 + [pl.BlockSpec(memory_space=pl.ANY)] * (1 + nk),
        out_specs=tuple([HBM] * nz), input_output_aliases={i: i for i in range(nz)},
        compiler_params=pltpu.CompilerParams(has_side_effects=DATAFLOW),
    )(*zones, *sems, after, *keep)
    return list(outs)


def adam_rs(w, m, v, l, own, kind, zone, outs, name):
    n, a, b = w.shape
    ta = max(t for t in range(16, min(a, 256) + 1, 16) if a % t == 0)
    per = a // ta
    me = (4 * lax.axis_index("x") + 2 * lax.axis_index("y") + lax.axis_index("c")).astype(jnp.int32).reshape(1)

    def body(me_ref, w_ref, m_ref, v_ref, own_ref, z_ref, i0, i1, i2, i3, g_ref, d_ref, m2_ref, v2_ref):
        gv = own_ref[...].astype(F32)
        for k in range(N_PEER):
            gv = gv + z_ref[k].astype(F32)
        m2 = ADAM_B1 * m_ref[...] + (1.0 - ADAM_B1) * gv
        v2 = ADAM_B2 * v_ref[...] + (1.0 - ADAM_B2) * (gv * gv)
        m_hat = m2 / (1.0 - ADAM_B1 ** ADAM_STEP)
        v_hat = v2 / (1.0 - ADAM_B2 ** ADAM_STEP)
        g_ref[...] = gv
        d_ref[...] = -ADAM_LR * (m_hat / (jnp.sqrt(v_hat) + ADAM_EPS) + ADAM_WD * w_ref[...])
        m2_ref[...] = m2
        v2_ref[...] = v2

    spec = pl.BlockSpec((None, ta, b), lambda r, me_ref: (l, r, 0))
    if kind == "lead":
        own_spec = pl.BlockSpec((None, ta, b), lambda r, me_ref: (me_ref[0], r, 0))
    elif kind == "row":
        own_spec = pl.BlockSpec((None, ta, b), lambda r, me_ref: (l, me_ref[0] * per + r, 0))
    else:
        own_spec = pl.BlockSpec((None, ta, b), lambda r, me_ref: (l, r, me_ref[0]))
    return pl.pallas_call(
        body, name=name, out_shape=[jax.ShapeDtypeStruct((n, a, b), F32)] * 4,
        grid_spec=pltpu.PrefetchScalarGridSpec(
            num_scalar_prefetch=1, grid=(per,),
            in_specs=[spec] * 3 + [own_spec, pl.BlockSpec((N_PEER, ta, b), lambda r, me_ref: (0, r, 0))] + [ANY] * 4,
            out_specs=[spec] * 4),
        input_output_aliases={6 + k: k for k in range(4)},
        compiler_params=_cp("parallel"),
    )(me, w, m, v, own, zone, *outs)


def small_exchange(sh, rep, name):
    _, Rs, C = sh.shape
    Rr = rep.shape[0]

    def body(sh_ref, rep_ref, sh_out, rep_out, send_sems, recv_sems, local_sems):
        x, y, c = _pos()
        me = 4 * x + 2 * y + c
        l1 = pltpu.make_async_copy(sh_ref.at[me], sh_out.at[me], local_sems.at[0])
        l2 = pltpu.make_async_copy(rep_ref, rep_out.at[me], local_sems.at[1])
        l1.start()
        l2.start()

        def flip(v, bit):
            return 1 - v if bit else v

        sends, recvs = [], []
        for r in range(1, N_DEV):
            peer = (flip(x, r & 4), flip(y, r & 2), flip(c, r & 1))
            pid = 4 * peer[0] + 2 * peer[1] + peer[2]
            k = 2 * (r - 1)
            mk = lambda src, dst, kk: pltpu.make_async_remote_copy(
                src_ref=src, dst_ref=dst, send_sem=send_sems.at[kk], recv_sem=recv_sems.at[kk],
                device_id=peer, device_id_type=MESH)
            sends += [mk(sh_ref.at[pid], sh_out.at[me], k), mk(rep_ref, rep_out.at[me], k + 1)]
            recvs += [mk(sh_ref.at[me], sh_out.at[pid], k), mk(rep_ref, rep_out.at[pid], k + 1)]
        for cp in sends:
            cp.start()
        for cp in recvs:
            cp.wait_recv()
        for cp in sends:
            cp.wait_send()
        l1.wait()
        l2.wait()

    n = 2 * (N_DEV - 1)
    return pl.pallas_call(
        body, name=name,
        out_shape=[jax.ShapeDtypeStruct((N_DEV, Rs, C), sh.dtype), jax.ShapeDtypeStruct((N_DEV, *rep.shape), rep.dtype)],
        in_specs=[ANY, ANY], out_specs=[ANY, ANY],
        scratch_shapes=[pltpu.SemaphoreType.DMA((n,)), pltpu.SemaphoreType.DMA((n,)), pltpu.SemaphoreType.DMA((2,))],
    )(sh, rep)


def adam_slots(w, m, v, slots, name):
    S, n, a, b = slots.shape
    ta = max(t for t in range(16, min(a, 512) + 1, 8)
             if a % t == 0 and t * S * b * slots.dtype.itemsize <= 4 * 1024 * 1024)

    def body(w_ref, m_ref, v_ref, s_ref, g_ref, d_ref, m2_ref, v2_ref):
        gv = s_ref[0].astype(F32)
        for k in range(1, S):
            gv = gv + s_ref[k].astype(F32)
        m2 = ADAM_B1 * m_ref[...] + (1.0 - ADAM_B1) * gv
        v2 = ADAM_B2 * v_ref[...] + (1.0 - ADAM_B2) * (gv * gv)
        m_hat = m2 / (1.0 - ADAM_B1 ** ADAM_STEP)
        v_hat = v2 / (1.0 - ADAM_B2 ** ADAM_STEP)
        g_ref[...] = gv
        d_ref[...] = -ADAM_LR * (m_hat / (jnp.sqrt(v_hat) + ADAM_EPS) + ADAM_WD * w_ref[...])
        m2_ref[...] = m2
        v2_ref[...] = v2

    spec = pl.BlockSpec((None, ta, b), lambda l, r: (l, r, 0))
    return pl.pallas_call(
        body, name=name, grid=(n, a // ta),
        in_specs=[spec] * 3 + [pl.BlockSpec((S, None, ta, b), lambda l, r: (0, l, r, 0))], out_specs=[spec] * 4,
        out_shape=[jax.ShapeDtypeStruct((n, a, b), F32)] * 4, compiler_params=_cp("parallel", "parallel"),
    )(w, m, v, slots)


WEIGHTS = ["norm_g", "ssm_in_w", "ssm_conv_w", "ssm_conv_b", "ssm_dt_bias", "ssm_A_log", "ssm_D", "ssm_norm_g",
           "ssm_out_w", "cf_pw1_w", "cf_pw1_b", "cf_dw_w", "cf_dw_b", "cf_ln_g", "cf_ln_b", "cf_pw2_w", "cf_pw2_b",
           "xa_mem_g", "xa_q_w", "xa_kv_w", "xa_o_w", "ffn_in_w", "ffn_conv_w", "ffn_conv_b", "ffn_out_w"]
ARGS = ["x", "mem"] + WEIGHTS + ["loss_target"] + ["m_" + n for n in WEIGHTS] + ["v_" + n for n in WEIGHTS]
BIG = {"ssm_in_w": "col", "ssm_out_w": "row", "cf_pw1_w": "col", "cf_pw2_w": "row", "xa_q_w": "row",
       "xa_kv_w": "col", "xa_o_w": "row", "ffn_in_w": "col", "ffn_out_w": "row"}
SMALL = ["norm_g", "ssm_conv_w", "cf_pw1_b", "cf_dw_w", "cf_dw_b", "cf_ln_g", "cf_ln_b", "cf_pw2_b", "ffn_conv_w"]
REP = ["ssm_conv_b", "ssm_dt_bias", "ssm_A_log", "ssm_D", "ssm_norm_g", "xa_mem_g", "ffn_conv_b"]
SMALL_W = 768
REP_W = 512


def _r8(n):
    return -(-n // 8) * 8


def _stack2d(arrs, wid):
    parts = []
    for a in arrs:
        r, c = a.shape[-2:]
        parts.append(jnp.pad(a, [(0, 0)] * (a.ndim - 2) + [(0, _r8(r) - r), (0, wid - c)]))
    return jnp.concatenate(parts, axis=-2)


def _unstack2d(buf, shapes2d):
    out, o = [], 0
    for r, c in shapes2d:
        out.append(buf[..., o:o + r, :c])
        o += _r8(r)
    return out


def _gathered_to_full(g):
    lead = g.shape[1:-1]
    return jnp.moveaxis(g, 0, -2).reshape(*lead, N_DEV * g.shape[-1])


def _full_to_slots(w):
    lead = w.shape[:-1]
    return jnp.moveaxis(w.reshape(*lead, N_DEV, w.shape[-1] // N_DEV), -2, 0)


def kernel(x, mem, norm_g, ssm_in_w, ssm_conv_w, ssm_conv_b, ssm_dt_bias, ssm_A_log, ssm_D, ssm_norm_g, ssm_out_w, cf_pw1_w, cf_pw1_b, cf_dw_w, cf_dw_b, cf_ln_g, cf_ln_b, cf_pw2_w, cf_pw2_b, xa_mem_g, xa_q_w, xa_kv_w, xa_o_w, ffn_in_w, ffn_conv_w, ffn_conv_b, ffn_out_w, loss_target, m_norm_g, m_ssm_in_w, m_ssm_conv_w, m_ssm_conv_b, m_ssm_dt_bias, m_ssm_A_log, m_ssm_D, m_ssm_norm_g, m_ssm_out_w, m_cf_pw1_w, m_cf_pw1_b, m_cf_dw_w, m_cf_dw_b, m_cf_ln_g, m_cf_ln_b, m_cf_pw2_w, m_cf_pw2_b, m_xa_mem_g, m_xa_q_w, m_xa_kv_w, m_xa_o_w, m_ffn_in_w, m_ffn_conv_w, m_ffn_conv_b, m_ffn_out_w, v_norm_g, v_ssm_in_w, v_ssm_conv_w, v_ssm_conv_b, v_ssm_dt_bias, v_ssm_A_log, v_ssm_D, v_ssm_norm_g, v_ssm_out_w, v_cf_pw1_w, v_cf_pw1_b, v_cf_dw_w, v_cf_dw_b, v_cf_ln_g, v_cf_ln_b, v_cf_pw2_w, v_cf_pw2_b, v_xa_mem_g, v_xa_q_w, v_xa_kv_w, v_xa_o_w, v_ffn_in_w, v_ffn_conv_w, v_ffn_conv_b, v_ffn_out_w):
    return _step(x, mem, norm_g, ssm_in_w, ssm_conv_w, ssm_conv_b, ssm_dt_bias, ssm_A_log, ssm_D, ssm_norm_g, ssm_out_w, cf_pw1_w, cf_pw1_b, cf_dw_w, cf_dw_b, cf_ln_g, cf_ln_b, cf_pw2_w, cf_pw2_b, xa_mem_g, xa_q_w, xa_kv_w, xa_o_w, ffn_in_w, ffn_conv_w, ffn_conv_b, ffn_out_w, loss_target, m_norm_g, m_ssm_in_w, m_ssm_conv_w, m_ssm_conv_b, m_ssm_dt_bias, m_ssm_A_log, m_ssm_D, m_ssm_norm_g, m_ssm_out_w, m_cf_pw1_w, m_cf_pw1_b, m_cf_dw_w, m_cf_dw_b, m_cf_ln_g, m_cf_ln_b, m_cf_pw2_w, m_cf_pw2_b, m_xa_mem_g, m_xa_q_w, m_xa_kv_w, m_xa_o_w, m_ffn_in_w, m_ffn_conv_w, m_ffn_conv_b, m_ffn_out_w, v_norm_g, v_ssm_in_w, v_ssm_conv_w, v_ssm_conv_b, v_ssm_dt_bias, v_ssm_A_log, v_ssm_D, v_ssm_norm_g, v_ssm_out_w, v_cf_pw1_w, v_cf_pw1_b, v_cf_dw_w, v_cf_dw_b, v_cf_ln_g, v_cf_ln_b, v_cf_pw2_w, v_cf_pw2_b, v_xa_mem_g, v_xa_q_w, v_xa_kv_w, v_xa_o_w, v_ffn_in_w, v_ffn_conv_w, v_ffn_conv_b, v_ffn_out_w)


def _step(*args):
    A = dict(zip(ARGS, args, strict=True))
    x, mem, target = A["x"][0], A["mem"][0], A["loss_target"][0]

    big = list(BIG)
    geo = [A[n].shape for n in big]
    kinds = ["row" if BIG[n] == "row" else ("col" if A[n].shape[-1] % LANE == 0 else "lead") for n in big]
    W = {n: A[n] for n in REP}
    small2d = [(A[n].size // A[n].shape[-1], A[n].shape[-1]) for n in SMALL]
    rep2d = [(A[n].size // REP_W, REP_W) if A[n].shape[-1] % REP_W == 0 else A[n].shape for n in REP] + [(1, 1)]
    stack_small = lambda pre: _stack2d([A[pre + n].reshape(rc) for n, rc in zip(SMALL, small2d)], SMALL_W)
    stack_rep = lambda pre: _stack2d([A[pre + n].reshape(rc) for n, rc in zip(REP, rep2d)] + [jnp.zeros((1, 1), F32)],
                                     REP_W)
    small_g = all_gather(stack_small(""), name="gather_small")
    for n, g in zip(SMALL, _unstack2d(small_g, small2d)):
        W[n] = _gathered_to_full(g.reshape(N_DEV, *A[n].shape))

    shards = [A[n].astype(BF16) for n in big]
    for n in big:
        W[n] = [None] * A[n].shape[0]
    first = _sublayer_weights(0, 0)
    got0 = gather_now([shards[big.index(n)][l:l + 1] for n, l in first], [kinds[big.index(n)] for n, l in first],
                      name="gather_first")
    for (n, l), g in zip(first, got0):
        W[n][l] = _gathered_to_full(g)[0] if kinds[big.index(n)] == "lead" else g[0]
    items, sub_items = [], {}
    for i in range(DEPTH):
        for s in range(3):
            sub_items[i, s] = []
            for n, l in _sublayer_weights(i, s) if (i, s) != (0, 0) else []:
                w = big.index(n)
                sub_items[i, s].append(len(items))
                items.append((w, l, kinds[w], *geo[w][1:]))
    sems, shards_thru, zones, token = gather_start(shards, items, [small_g, got0[0]], name="gather_start")
    x = x + token[0, 0]

    def fetch(i, s, x_in):
        ids = sub_items[i, s]
        if not ids:
            return
        got = gather_wait([zones[t] for t in ids], ids, items, sems, x_in, shards_thru if (i, s) == (DEPTH - 1, 2) else [],
                          name=f"gather_wait{i}{s}")
        for t, z in zip(ids, got):
            w, l, kind = items[t][:3]
            W[big[w]][l] = _gathered_to_full(z) if kind == "lead" else z

    sent = []
    final = {}

    def layer_done(i, s, GB, g_in):
        srcs, its = [], []
        for n, l in _sublayer_weights(i, s):
            w = big.index(n)
            if kinds[w] == "lead":
                srcs.append(_full_to_slots(g_in if n == "ssm_in_w" else GB[n][l]))
                its.append((len(srcs) - 1, None, "lead", *geo[w][1:], n, l))
            else:
                srcs.append(GB[n])
                its.append((len(srcs) - 1, l, kinds[w], *geo[w][1:], n, l))
        sems_i, thru, zones_i, token_i = rs_start(srcs, [it[:5] for it in its], name=f"rs_start{i}{s}")
        for it, s in zip(its, thru):
            if it[2] != "lead":
                GB[it[5]] = s
        sent.append((its, sems_i, [s for it, s in zip(its, thru) if it[2] == "lead"], zones_i))
        final["GB"] = GB
        return token_i

    loss, grad_x, G = local_step(x, mem, target, W, fetch, layer_done)

    sh = _stack2d([_full_to_slots(G[n]).reshape(N_DEV, *rc) for n, rc in zip(SMALL, small2d)], SMALL_W)
    rep = _stack2d([G[n].reshape(rc) for n, rc in zip(REP, rep2d)] + [loss[:, :1]], REP_W)
    sh_got, rep_got = small_exchange(sh, rep, name="small_exchange")

    res = {}
    GBf = final["GB"]
    bufs = {n: [lax.empty(A[n].shape, F32) for _ in range(4)] for n in big}
    for i, (its, sems_i, lead_srcs, zones_i) in enumerate(sent):
        keep = lead_srcs + [GBf[it[5]] for it in its if it[2] != "lead"]
        zones_i = rs_wait(zones_i, [it[:5] for it in its], sems_i, sh_got, keep, name=f"rs_wait{i}")
        lead_it = iter(lead_srcs)
        for it, z in zip(its, zones_i):
            n, l = it[5], it[6]
            own = next(lead_it) if it[2] == "lead" else GBf[n]
            bufs[n] = adam_rs(A[n], A["m_" + n], A["v_" + n], l, own, it[2], z, bufs[n], name=f"adam_{n}{l}")
    for n in big:
        res[n] = tuple(bufs[n])
    for names, shapes2d, stack, slots, tag in ((SMALL, small2d, stack_small, sh_got, "small"),
                                               (REP, rep2d, stack_rep, rep_got, "rep")):
        outs4 = adam_slots(stack("")[None], stack("m_")[None], stack("v_")[None], slots[:, None], name=f"adam_{tag}")
        parts = [_unstack2d(o[0], shapes2d) for o in outs4]
        for k, n in enumerate(names):
            res[n] = tuple(q[k].reshape(A[n].shape) for q in parts)
        if tag == "rep":
            total_loss = parts[0][-1][0, 0]

    outs = [total_loss, grad_x[None]]
    for k in range(4):
        outs += [res[n][k] for n in WEIGHTS]
    return tuple(outs)
```

```python
import jax
import jax.numpy as jnp
from jax import lax
from jax.experimental import pallas as pl
from jax.experimental.pallas import tpu as pltpu

F32 = jnp.float32
BF16 = jnp.bfloat16

D_MODEL = 1024
D_INNER = 2048
N_HEADS = 32
HEAD_DIM = 64
N_GROUPS = 4
D_STATE = 128
CHUNK = 128
CONV_DIM = 3072
SSM_K = 4
CF_K = 31
N_MEM = 256
XA_HEADS = 4
XA_HD = 256
D_FF = 2816
FFN_K = 3
EPS = 1e-6
DEPTH = 4
N_DEV = 8

ADAM_LR = 0.001
ADAM_B1 = 0.9
ADAM_B2 = 0.999
ADAM_EPS = 1e-08
ADAM_WD = 0.01
ADAM_STEP = 10

LANE = 128
VMEM_LIMIT = 56 * 1024 * 1024
NEG = -1e30
MESH = pl.DeviceIdType.MESH


def _cp(*sem):
    return pltpu.CompilerParams(dimension_semantics=sem if sem else None, vmem_limit_bytes=VMEM_LIMIT)


def _tile(n, cap):
    if n <= cap:
        return n
    best = 0
    for t in range(LANE, cap + 1, LANE):
        if n % t == 0:
            best = t
    assert best, (n, cap)
    return best


def _sig(x):
    return 1.0 / (1.0 + jnp.exp(-x))


def _split3(v):
    v0 = v.astype(BF16)
    r1 = v - v0.astype(F32)
    v1 = r1.astype(BF16)
    v2 = (r1 - v1.astype(F32)).astype(BF16)
    return v0, v1, v2


def _dot(a, b, ca=1, cb=0):
    return lax.dot_general(a, b, (((ca,), (cb,)), ((), ())), preferred_element_type=F32)


def _dot2(v, m, ca=1, cb=0):
    v0 = v.astype(BF16)
    v1 = (v - v0.astype(F32)).astype(BF16)
    return _dot(v0, m, ca, cb) + _dot(v1, m, ca, cb)


def _dot3(v, m, ca=1, cb=0):
    v0, v1, v2 = _split3(v)
    return _dot(v0, m, ca, cb) + _dot(v1, m, ca, cb) + _dot(v2, m, ca, cb)


def mm(a, b, *, ta=False, tb=False, bias=None, acc=None, out_dtype=F32, a_idx=None, layer=None, b_k0=0, b_n=None,
       into=None, dep=None, name):
    if isinstance(b, (list, tuple)):
        b, layer = b[layer], None
    if ta:
        K, M = a.shape[-2:]
    else:
        M, K = a.shape[-2:]
    N = b_n if b_n is not None else (b.shape[-2] if tb else b.shape[-1])
    assert (b.ndim == 3) == (layer is not None) and (a.ndim == 3) == (a_idx is not None)
    tm = _tile(M, 1536)
    tn = _tile(N, 1536)
    tk = _tile(K, 2048)
    nk = K // tk
    assert b_k0 % tk == 0 and b_k0 + K <= (b.shape[-1] if tb else b.shape[-2])
    kb = b_k0 // tk
    has_bias, has_acc = bias is not None, acc is not None
    if into is not None:
        out_dtype = into[0].dtype
        assert into[0].shape[1] == M and into[2] % tn == 0 and into[2] + N <= into[0].shape[2] and not has_acc

    def body(*refs):
        a_ref, b_ref = refs[0], refs[1]
        pos = 2
        bias_ref = acc_ref = None
        if has_bias:
            bias_ref = refs[pos]
            pos += 1
        if has_acc:
            acc_ref = refs[pos]
            pos += 1
        if into is not None:
            pos += 1
        if dep is not None:
            pos += 1
        o_ref = refs[pos]
        s_ref = refs[pos + 1] if nk > 1 else None
        p = _dot(a_ref[...].astype(BF16), b_ref[...].astype(BF16), 0 if ta else 1, 1 if tb else 0)

        def extras(v):
            if has_bias:
                v = v + bias_ref[...]
            if has_acc:
                v = v + acc_ref[...]
            return v

        if nk == 1:
            o_ref[...] = extras(p).astype(out_dtype)
        else:
            k = pl.program_id(2)

            @pl.when(k == 0)
            def _():
                s_ref[...] = extras(p)

            @pl.when(k > 0)
            def _():
                s_ref[...] += p

            @pl.when(k == nk - 1)
            def _():
                o_ref[...] = s_ref[...].astype(out_dtype)

    lead_a = () if a_idx is None else (a_idx,)
    lead_b = () if layer is None else (layer,)
    sq = lambda lead: (None,) * len(lead)
    if ta:
        a_spec = pl.BlockSpec((*sq(lead_a), tk, tm), lambda i, j, k: (*lead_a, k, i))
    else:
        a_spec = pl.BlockSpec((*sq(lead_a), tm, tk), lambda i, j, k: (*lead_a, i, k))
    if tb:
        b_spec = pl.BlockSpec((*sq(lead_b), tn, tk), lambda i, j, k: (*lead_b, j, k + kb))
    else:
        b_spec = pl.BlockSpec((*sq(lead_b), tk, tn), lambda i, j, k: (*lead_b, k + kb, j))
    in_specs, args = [a_spec, b_spec], [a, b]
    if has_bias:
        in_specs.append(pl.BlockSpec((1, tn), lambda i, j, k: (0, j)))
        args.append(bias.reshape(1, N).astype(F32))
    if has_acc:
        in_specs.append(pl.BlockSpec((tm, tn), lambda i, j, k: (i, j)))
        args.append(acc)
    if into is None:
        out_spec = pl.BlockSpec((tm, tn), lambda i, j, k: (i, j))
        out_shape = jax.ShapeDtypeStruct((M, N), out_dtype)
        aliases = {}
    else:
        buf, l, col0 = into
        cb = col0 // tn
        in_specs.append(pl.BlockSpec(memory_space=pl.ANY))
        args.append(buf)
        out_spec = pl.BlockSpec((None, tm, tn), lambda i, j, k: (l, i, j + cb))
        out_shape = jax.ShapeDtypeStruct(buf.shape, buf.dtype)
        aliases = {len(args) - 1: 0}
    if dep is not None:
        in_specs.append(pl.BlockSpec(memory_space=pl.ANY))
        args.append(dep)
    return pl.pallas_call(
        body, name=name, grid=(M // tm, N // tn, nk),
        in_specs=in_specs, out_specs=out_spec, out_shape=out_shape, input_output_aliases=aliases,
        scratch_shapes=[pltpu.VMEM((tm, tn), F32)] if nk > 1 else [],
        compiler_params=_cp("parallel", "parallel", "arbitrary"),
    )(*args)


def colsum(x, name):
    L, C = x.shape
    tr = _tile(L, 512)
    tc = _tile(C, 1024)

    def body(x_ref, o_ref):
        @pl.when(pl.program_id(1) == 0)
        def _():
            o_ref[...] = jnp.zeros_like(o_ref)

        o_ref[...] += jnp.sum(x_ref[...].astype(F32), axis=0, keepdims=True)

    return pl.pallas_call(
        body, name=name, grid=(C // tc, L // tr),
        in_specs=[pl.BlockSpec((tr, tc), lambda j, i: (i, j))],
        out_specs=pl.BlockSpec((1, tc), lambda j, i: (0, j)),
        out_shape=jax.ShapeDtypeStruct((1, C), F32),
        compiler_params=_cp("parallel", "arbitrary"),
    )(x)


TR = 512


def _row_spec(tr, w):
    return pl.BlockSpec((tr, w), lambda i: (i, 0))


def _vec_spec(w):
    return pl.BlockSpec((1, w), lambda i: (0, 0))


def _rms(v):
    return lax.rsqrt(jnp.mean(v * v, axis=-1, keepdims=True) + EPS)


def norm_fwd(x, g, name):
    L, D = x.shape
    tr = min(TR, L)

    def body(x_ref, g_ref, h_ref):
        xv = x_ref[...]
        h_ref[...] = (xv * _rms(xv) * g_ref[...]).astype(BF16)

    return pl.pallas_call(
        body, name=name, grid=(L // tr,),
        in_specs=[_row_spec(tr, D), _vec_spec(D)], out_specs=_row_spec(tr, D),
        out_shape=jax.ShapeDtypeStruct((L, D), BF16), compiler_params=_cp("parallel"),
    )(x, g.reshape(1, D))


def bnd_fwd(x, f, gpost, gpre, name):
    L, D = x.shape
    tr = min(TR, L)

    def body(x_ref, f_ref, gp_ref, gn_ref, xo_ref, h_ref):
        fv = f_ref[...].astype(F32)
        xn = x_ref[...] + fv * _rms(fv) * gp_ref[...]
        xo_ref[...] = xn
        h_ref[...] = (xn * _rms(xn) * gn_ref[...]).astype(BF16)

    return pl.pallas_call(
        body, name=name, grid=(L // tr,),
        in_specs=[_row_spec(tr, D), _row_spec(tr, D), _vec_spec(D), _vec_spec(D)],
        out_specs=[_row_spec(tr, D), _row_spec(tr, D)],
        out_shape=[jax.ShapeDtypeStruct((L, D), F32), jax.ShapeDtypeStruct((L, D), BF16)],
        compiler_params=_cp("parallel"),
    )(x, f, gpost.reshape(1, D), gpre.reshape(1, D))


def final_fwd(x, f, gpost, target, name):
    L, D = x.shape
    tr = min(TR, L)
    n = L // tr

    def body(x_ref, f_ref, gp_ref, t_ref, dy_ref, loss_ref, acc_ref):
        i = pl.program_id(0)

        @pl.when(i == 0)
        def _():
            acc_ref[...] = jnp.zeros_like(acc_ref)

        fv = f_ref[...].astype(F32)
        e = x_ref[...] + fv * _rms(fv) * gp_ref[...] - t_ref[...]
        dy_ref[...] = e * (1.0 / D)
        acc_ref[...] += jnp.sum(e * e, axis=0, keepdims=True)

        @pl.when(i == n - 1)
        def _():
            loss_ref[...] = jnp.full((1, LANE), 0.5 / D, F32) * jnp.sum(acc_ref[...])

    return pl.pallas_call(
        body, name=name, grid=(n,),
        in_specs=[_row_spec(tr, D), _row_spec(tr, D), _vec_spec(D), _row_spec(tr, D)],
        out_specs=[_row_spec(tr, D), _vec_spec(LANE)],
        out_shape=[jax.ShapeDtypeStruct((L, D), F32), jax.ShapeDtypeStruct((1, LANE), F32)],
        scratch_shapes=[pltpu.VMEM((1, D), F32)],
        compiler_params=_cp("arbitrary"),
    )(x, f, gpost.reshape(1, D), target)


def _rms_bwd(v, g, dy):
    r = _rms(v)
    vn = v * r
    dg = jnp.sum(dy * vn, axis=0, keepdims=True)
    dvn = dy * g
    dv = r * (dvn - vn * jnp.mean(dvn * vn, axis=-1, keepdims=True))
    return dv, dg


def bnd_bwd(dxp, *, pre=None, post=None, dep=None, name):
    L, D = dxp.shape
    tr = min(TR, L)
    has_pre, has_post = pre is not None, post is not None

    def body(*refs):
        pos = 0
        dxp_ref = refs[pos]; pos += 1
        if has_pre:
            x_ref, gpre_ref, dh_ref = refs[pos:pos + 3]; pos += 3
        if has_post:
            f_ref, gpost_ref = refs[pos:pos + 2]; pos += 2
        if dep is not None:
            pos += 1
        if has_pre:
            dx_ref, dgpre_ref = refs[pos:pos + 2]; pos += 2
        if has_post:
            df_ref, dgpost_ref = refs[pos:pos + 2]; pos += 2
        i = pl.program_id(0)
        dx = dxp_ref[...]
        if has_pre:
            d, dg = _rms_bwd(x_ref[...], gpre_ref[...], dh_ref[...].astype(F32))
            dx = dx + d
            dx_ref[...] = dx

            @pl.when(i == 0)
            def _():
                dgpre_ref[...] = jnp.zeros_like(dgpre_ref)

            dgpre_ref[...] += dg
        if has_post:
            d, dg = _rms_bwd(f_ref[...].astype(F32), gpost_ref[...], dx)
            df_ref[...] = d.astype(BF16)

            @pl.when(i == 0)
            def _():
                dgpost_ref[...] = jnp.zeros_like(dgpost_ref)

            dgpost_ref[...] += dg

    in_specs, args = [_row_spec(tr, D)], [dxp]
    out_specs, out_shape, names = [], [], []
    if has_pre:
        x, gpre, dh = pre
        in_specs += [_row_spec(tr, D), _vec_spec(D), _row_spec(tr, D)]
        args += [x, gpre.reshape(1, D), dh]
        out_specs += [_row_spec(tr, D), _vec_spec(D)]
        out_shape += [jax.ShapeDtypeStruct((L, D), F32), jax.ShapeDtypeStruct((1, D), F32)]
        names += ["dx", "dgpre"]
    if has_post:
        f, gpost = post
        in_specs += [_row_spec(tr, D), _vec_spec(D)]
        args += [f, gpost.reshape(1, D)]
        out_specs += [_row_spec(tr, D), _vec_spec(D)]
        out_shape += [jax.ShapeDtypeStruct((L, D), BF16), jax.ShapeDtypeStruct((1, D), F32)]
        names += ["df", "dgpost"]
    if dep is not None:
        in_specs.append(pl.BlockSpec(memory_space=pl.ANY))
        args.append(dep)
    outs = pl.pallas_call(
        body, name=name, grid=(L // tr,), in_specs=in_specs, out_specs=out_specs, out_shape=out_shape,
        compiler_params=_cp("arbitrary"),
    )(*args)
    return dict(zip(names, outs))


def norm_dg(x, dy, name):
    L, D = x.shape
    tr = min(TR, L)

    def body(x_ref, dy_ref, o_ref):
        @pl.when(pl.program_id(0) == 0)
        def _():
            o_ref[...] = jnp.zeros_like(o_ref)

        xv = x_ref[...]
        o_ref[...] += jnp.sum(dy_ref[...] * xv * _rms(xv), axis=0, keepdims=True)

    return pl.pallas_call(
        body, name=name, grid=(L // tr,),
        in_specs=[_row_spec(tr, D), _row_spec(tr, D)], out_specs=_vec_spec(D),
        out_shape=jax.ShapeDtypeStruct((1, D), F32), compiler_params=_cp("arbitrary"),
    )(x, dy)


HALO = 32


def _prev_halo_spec(tr, tc, col):
    per = tr // HALO
    return pl.BlockSpec((HALO, tc), lambda *g: (jnp.maximum(g[-1] * per - 1, 0), col(*g)))


def _fill_prev(scr, halo_val, blk_val, i, tr):
    scr[pl.ds(0, HALO), :] = jnp.where(i == 0, 0.0, halo_val)
    scr[pl.ds(HALO, tr), :] = blk_val


def _conv(scr, w_ref, K, tr):
    acc = None
    for k in range(K):
        term = scr[pl.ds(HALO - (K - 1) + k, tr), :] * w_ref[k:k + 1, :]
        acc = term if acc is None else acc + term
    return acc


def _shift_copies(scr, sh, rows):
    n = rows - 8
    for r in range(1, 8):
        sh[r - 1, pl.ds(0, n), :] = scr[pl.ds(r, n), :]


def _tap(scr, sh, off, tr):
    q, r = divmod(off, 8)
    return scr[pl.ds(off, tr), :] if r == 0 else sh[r - 1, pl.ds(8 * q, tr), :]


def ssm_conv_fwd(zx, w, b, *, col0, ncols, wcol0, out_dtype, name):
    L = zx.shape[0]
    tr = min(TR, L)
    tc = 1024
    cb, wb = col0 // tc, wcol0 // tc

    def body(x_ref, h_ref, w_ref, b_ref, o_ref, p_ref, scr):
        i = pl.program_id(1)
        _fill_prev(scr, h_ref[...].astype(F32), x_ref[...].astype(F32), i, tr)
        pre = _conv(scr, w_ref, SSM_K, tr) + b_ref[...]
        p_ref[...] = pre.astype(BF16)
        o_ref[...] = (pre * _sig(pre)).astype(out_dtype)

    out = pl.BlockSpec((tr, tc), lambda j, i: (i, j))
    return pl.pallas_call(
        body, name=name, grid=(ncols // tc, L // tr),
        in_specs=[pl.BlockSpec((tr, tc), lambda j, i: (i, j + cb)),
                  _prev_halo_spec(tr, tc, lambda j, i: j + cb),
                  pl.BlockSpec((SSM_K, tc), lambda j, i: (0, j + wb)),
                  pl.BlockSpec((1, tc), lambda j, i: (0, j + wb))],
        out_specs=[out, out],
        out_shape=[jax.ShapeDtypeStruct((L, ncols), out_dtype), jax.ShapeDtypeStruct((L, ncols), BF16)],
        scratch_shapes=[pltpu.VMEM((HALO + tr, tc), F32)],
        compiler_params=_cp("parallel", "parallel"),
    )(zx, zx, w, b)


def ssm_conv_bwd(zx, pre, d, w, *, col0, dcol0, ncols, name):
    L = zx.shape[0]
    tr = min(TR, L)
    tc = 1024
    cb, db_ = col0 // tc, dcol0 // tc
    n = L // tr
    per = tr // HALO
    last = L // HALO - 1

    def body(x_ref, p_ref, np_ref, d_ref, nd_ref, w_ref, dx_ref, dw_ref, db_ref, sd):
        i = pl.program_id(1)

        def dpre(p, dv):
            s = _sig(p)
            return dv * s * (1.0 + p * (1.0 - s))

        dp = dpre(p_ref[...].astype(F32), d_ref[...])
        sd[pl.ds(0, tr), :] = dp
        sd[pl.ds(tr, HALO), :] = jnp.where(i == n - 1, 0.0, dpre(np_ref[...].astype(F32), nd_ref[...]))

        @pl.when(i == 0)
        def _():
            dw_ref[...] = jnp.zeros_like(dw_ref)
            db_ref[...] = jnp.zeros_like(db_ref)

        xv = x_ref[...].astype(F32)
        acc = None
        for k in range(SSM_K):
            tk = sd[pl.ds(SSM_K - 1 - k, tr), :]
            term = tk * w_ref[k:k + 1, :]
            acc = term if acc is None else acc + term
            dw_ref[k:k + 1, :] += jnp.sum(xv * tk, axis=0, keepdims=True)
        dx_ref[...] = acc.astype(BF16)
        db_ref[...] += jnp.sum(dp, axis=0, keepdims=True)

    nxt = lambda i: jnp.minimum((i + 1) * per, last)
    return pl.pallas_call(
        body, name=name, grid=(ncols // tc, n),
        in_specs=[pl.BlockSpec((tr, tc), lambda j, i: (i, j + cb)),
                  pl.BlockSpec((tr, tc), lambda j, i: (i, j)),
                  pl.BlockSpec((HALO, tc), lambda j, i: (nxt(i), j)),
                  pl.BlockSpec((tr, tc), lambda j, i: (i, j + db_)),
                  pl.BlockSpec((HALO, tc), lambda j, i: (nxt(i), j + db_)),
                  pl.BlockSpec((SSM_K, tc), lambda j, i: (0, j + db_))],
        out_specs=[pl.BlockSpec((tr, tc), lambda j, i: (i, j)),
                   pl.BlockSpec((SSM_K, tc), lambda j, i: (0, j)),
                   pl.BlockSpec((1, tc), lambda j, i: (0, j))],
        out_shape=[jax.ShapeDtypeStruct((L, ncols), BF16), jax.ShapeDtypeStruct((SSM_K, ncols), F32),
                   jax.ShapeDtypeStruct((1, ncols), F32)],
        scratch_shapes=[pltpu.VMEM((tr + HALO, tc), F32)],
        compiler_params=_cp("parallel", "arbitrary"),
    )(zx, pre, pre, d, d, w)


FFN_TC = 1408
FFN_TR = 256


def ffn_act_fwd(u, w, b, name):
    L = u.shape[0]
    tr = min(FFN_TR, L)
    tc = FFN_TC
    nb = D_FF // tc

    def body(g_ref, hg_ref, v_ref, hv_ref, wg_ref, wv_ref, bg_ref, bv_ref, o_ref, c_ref, sg, sv):
        i = pl.program_id(1)
        _fill_prev(sg, hg_ref[...].astype(F32), g_ref[...].astype(F32), i, tr)
        _fill_prev(sv, hv_ref[...].astype(F32), v_ref[...].astype(F32), i, tr)
        ug = _conv(sg, wg_ref, FFN_K, tr) + bg_ref[...]
        uv = _conv(sv, wv_ref, FFN_K, tr) + bv_ref[...]
        c_ref[0] = ug.astype(BF16)
        c_ref[1] = uv.astype(BF16)
        o_ref[...] = (ug * _sig(ug) * uv).astype(BF16)

    blk = lambda off: pl.BlockSpec((tr, tc), lambda j, i: (i, j + off))
    wsp = lambda off: pl.BlockSpec((FFN_K, tc), lambda j, i: (0, j + off))
    bsp = lambda off: pl.BlockSpec((1, tc), lambda j, i: (0, j + off))
    return pl.pallas_call(
        body, name=name, grid=(nb, L // tr),
        in_specs=[blk(0), _prev_halo_spec(tr, tc, lambda j, i: j),
                  blk(nb), _prev_halo_spec(tr, tc, lambda j, i: j + nb),
                  wsp(0), wsp(nb), bsp(0), bsp(nb)],
        out_specs=[pl.BlockSpec((tr, tc), lambda j, i: (i, j)), pl.BlockSpec((2, tr, tc), lambda j, i: (0, i, j))],
        out_shape=[jax.ShapeDtypeStruct((L, D_FF), BF16), jax.ShapeDtypeStruct((2, L, D_FF), BF16)],
        scratch_shapes=[pltpu.VMEM((HALO + tr, tc), F32), pltpu.VMEM((HALO + tr, tc), F32)],
        compiler_params=_cp("parallel", "parallel"),
    )(u, u, u, u, w, w, b, b)


def ffn_act_bwd(u, c, dact, w, name):
    L = u.shape[0]
    tr = min(FFN_TR, L)
    tc = FFN_TC
    nb = D_FF // tc
    n = L // tr
    per = tr // HALO
    last = L // HALO - 1

    def body(g_ref, v_ref, c_ref, nc_ref, da_ref, nda_ref, wg_ref, wv_ref, du_ref, dw_ref, db_ref, dg_s, dv_s):
        i = pl.program_id(1)

        def grads(cg, cv, da):
            s = _sig(cg)
            return da * cv * s * (1.0 + cg * (1.0 - s)), da * cg * s

        dg, dv = grads(c_ref[0].astype(F32), c_ref[1].astype(F32), da_ref[...].astype(F32))
        ndg, ndv = grads(nc_ref[0].astype(F32), nc_ref[1].astype(F32), nda_ref[...].astype(F32))
        at_end = i == n - 1
        for half, (scr, d, nd, x_ref, w_ref) in enumerate(((dg_s, dg, ndg, g_ref, wg_ref), (dv_s, dv, ndv, v_ref, wv_ref))):
            scr[pl.ds(0, tr), :] = d
            scr[pl.ds(tr, HALO), :] = jnp.where(at_end, 0.0, nd)

            @pl.when(i == 0)
            def _():
                dw_ref[half] = jnp.zeros((FFN_K, tc), F32)
                db_ref[half] = jnp.zeros((1, tc), F32)

            xv = x_ref[...].astype(F32)
            acc = None
            for k in range(FFN_K):
                tk = scr[pl.ds(FFN_K - 1 - k, tr), :]
                term = tk * w_ref[k:k + 1, :]
                acc = term if acc is None else acc + term
                dw_ref[half, k:k + 1, :] += jnp.sum(xv * tk, axis=0, keepdims=True)
            du_ref[half] = acc.astype(BF16)
            db_ref[half] += jnp.sum(d, axis=0, keepdims=True)

    blk = lambda off: pl.BlockSpec((tr, tc), lambda j, i: (i, j + off))
    wsp = lambda off: pl.BlockSpec((FFN_K, tc), lambda j, i: (0, j + off))
    nxt = lambda i: jnp.minimum((i + 1) * per, last)
    return pl.pallas_call(
        body, name=name, grid=(nb, n),
        in_specs=[blk(0), blk(nb),
                  pl.BlockSpec((2, tr, tc), lambda j, i: (0, i, j)),
                  pl.BlockSpec((2, HALO, tc), lambda j, i: (0, nxt(i), j)),
                  pl.BlockSpec((tr, tc), lambda j, i: (i, j)),
                  pl.BlockSpec((HALO, tc), lambda j, i: (nxt(i), j)),
                  wsp(0), wsp(nb)],
        out_specs=[pl.BlockSpec((2, tr, tc), lambda j, i: (0, i, j)),
                   pl.BlockSpec((2, FFN_K, tc), lambda j, i: (0, 0, j)),
                   pl.BlockSpec((2, 1, tc), lambda j, i: (0, 0, j))],
        out_shape=[jax.ShapeDtypeStruct((2, L, D_FF), BF16), jax.ShapeDtypeStruct((2, FFN_K, D_FF), F32),
                   jax.ShapeDtypeStruct((2, 1, D_FF), F32)],
        scratch_shapes=[pltpu.VMEM((tr + HALO, tc), F32), pltpu.VMEM((tr + HALO, tc), F32)],
        compiler_params=_cp("parallel", "arbitrary"),
    )(u, u, c, c, dact, dact, w, w)


def _ln_stats(c):
    mu = jnp.mean(c, axis=-1, keepdims=True)
    cc = c - mu
    rstd = lax.rsqrt(jnp.mean(cc * cc, axis=-1, keepdims=True) + EPS)
    return cc * rstd, rstd


def cf_fwd(u, dw_w, dw_b, ln_g, ln_b, name):
    L = u.shape[0]
    D = D_MODEL
    tr = min(TR, L)

    def body(a_ref, ha_ref, g_ref, hg_ref, w_ref, b_ref, lg_ref, lb_ref, c_ref, s_ref, scr, sh):
        i = pl.program_id(0)
        glu_h = ha_ref[...].astype(F32) * _sig(hg_ref[...].astype(F32))
        glu = a_ref[...].astype(F32) * _sig(g_ref[...].astype(F32))
        _fill_prev(scr, glu_h, glu, i, tr)
        _shift_copies(scr, sh, HALO + tr)
        c = b_ref[...]
        for k in range(CF_K):
            c = c + _tap(scr, sh, HALO - (CF_K - 1) + k, tr) * w_ref[k:k + 1, :]
        c_ref[...] = c
        xhat, _ = _ln_stats(c)
        ln = xhat * lg_ref[...] + lb_ref[...]
        s_ref[...] = (ln * _sig(ln)).astype(BF16)

    per = tr // HALO
    halo = lambda col: pl.BlockSpec((HALO, D), lambda i: (jnp.maximum(i * per - 1, 0), col))
    return pl.pallas_call(
        body, name=name, grid=(L // tr,),
        in_specs=[pl.BlockSpec((tr, D), lambda i: (i, 0)), halo(0),
                  pl.BlockSpec((tr, D), lambda i: (i, 1)), halo(1),
                  pl.BlockSpec((CF_K, D), lambda i: (0, 0)), _vec_spec(D), _vec_spec(D), _vec_spec(D)],
        out_specs=[_row_spec(tr, D), _row_spec(tr, D)],
        out_shape=[jax.ShapeDtypeStruct((L, D), F32), jax.ShapeDtypeStruct((L, D), BF16)],
        scratch_shapes=[pltpu.VMEM((HALO + tr, D), F32), pltpu.VMEM((7, HALO + tr, D), F32)],
        compiler_params=_cp("parallel"),
    )(u, u, u, u, dw_w, dw_b, ln_g, ln_b)


def cf_bwd_ln(c, ds, ln_g, ln_b, name):
    L, D = c.shape
    tr = min(TR, L)

    def body(c_ref, ds_ref, lg_ref, lb_ref, dc_ref, dg_ref, db_ref):
        xhat, rstd = _ln_stats(c_ref[...])
        ln = xhat * lg_ref[...] + lb_ref[...]
        sg = _sig(ln)
        dln = ds_ref[...].astype(F32) * sg * (1.0 + ln * (1.0 - sg))

        @pl.when(pl.program_id(0) == 0)
        def _():
            dg_ref[...] = jnp.zeros_like(dg_ref)
            db_ref[...] = jnp.zeros_like(db_ref)

        dg_ref[...] += jnp.sum(dln * xhat, axis=0, keepdims=True)
        db_ref[...] += jnp.sum(dln, axis=0, keepdims=True)
        dxh = dln * lg_ref[...]
        dc_ref[...] = rstd * (dxh - jnp.mean(dxh, axis=-1, keepdims=True)
                              - xhat * jnp.mean(dxh * xhat, axis=-1, keepdims=True))

    return pl.pallas_call(
        body, name=name, grid=(L // tr,),
        in_specs=[_row_spec(tr, D), _row_spec(tr, D), _vec_spec(D), _vec_spec(D)],
        out_specs=[_row_spec(tr, D), _vec_spec(D), _vec_spec(D)],
        out_shape=[jax.ShapeDtypeStruct((L, D), F32), jax.ShapeDtypeStruct((1, D), F32),
                   jax.ShapeDtypeStruct((1, D), F32)],
        compiler_params=_cp("arbitrary"),
    )(c, ds, ln_g, ln_b)


def cf_bwd_conv(u, dc, dw_w, name):
    L = u.shape[0]
    D = D_MODEL
    tr = min(TR, L)
    n = L // tr

    def body(a_ref, g_ref, dc_ref, nx_ref, w_ref, du_ref, dw_ref, db_ref, sd, shd):
        i = pl.program_id(0)
        a = a_ref[...].astype(F32)
        sg = _sig(g_ref[...].astype(F32))
        glu = a * sg
        dcv = dc_ref[...]
        sd[pl.ds(0, tr), :] = dcv
        sd[pl.ds(tr, HALO), :] = jnp.where(i == n - 1, 0.0, nx_ref[...])
        _shift_copies(sd, shd, tr + HALO)

        @pl.when(i == 0)
        def _():
            dw_ref[...] = jnp.zeros_like(dw_ref)
            db_ref[...] = jnp.zeros_like(db_ref)

        dglu = None
        for k in range(CF_K):
            tk = _tap(sd, shd, CF_K - 1 - k, tr)
            term = tk * w_ref[k:k + 1, :]
            dglu = term if dglu is None else dglu + term
            dw_ref[k:k + 1, :] += jnp.sum(glu * tk, axis=0, keepdims=True)
        du_ref[:, 0:D] = (dglu * sg).astype(BF16)
        du_ref[:, D:2 * D] = (dglu * a * sg * (1.0 - sg)).astype(BF16)
        db_ref[...] += jnp.sum(dcv, axis=0, keepdims=True)

    per = tr // HALO
    last = L // HALO - 1
    return pl.pallas_call(
        body, name=name, grid=(n,),
        in_specs=[pl.BlockSpec((tr, D), lambda i: (i, 0)),
                  pl.BlockSpec((tr, D), lambda i: (i, 1)),
                  _row_spec(tr, D),
                  pl.BlockSpec((HALO, D), lambda i: (jnp.minimum((i + 1) * per, last), 0)),
                  pl.BlockSpec((CF_K, D), lambda i: (0, 0))],
        out_specs=[pl.BlockSpec((tr, 2 * D), lambda i: (i, 0)),
                   pl.BlockSpec((CF_K, D), lambda i: (0, 0)), _vec_spec(D)],
        out_shape=[jax.ShapeDtypeStruct((L, 2 * D), BF16), jax.ShapeDtypeStruct((CF_K, D), F32),
                   jax.ShapeDtypeStruct((1, D), F32)],
        scratch_shapes=[pltpu.VMEM((tr + HALO, D), F32), pltpu.VMEM((7, tr + HALO, D), F32)],
        compiler_params=_cp("arbitrary"),
    )(u, u, dc, dc, dw_w)


XA_TR = 512
XA_SCALE = XA_HD ** -0.5


def _xa_probs(qh, kh):
    s = _dot(qh, kh, 1, 1) * XA_SCALE
    p = jnp.exp(s - jnp.max(s, axis=-1, keepdims=True))
    return p / jnp.sum(p, axis=-1, keepdims=True)


def attn_fwd(q, kv, name):
    L, D = q.shape
    tr = min(XA_TR, L)

    def body(q_ref, kv_ref, o_ref):
        for hd in range(XA_HEADS):
            c = slice(hd * XA_HD, (hd + 1) * XA_HD)
            p = _xa_probs(q_ref[:, c], kv_ref[:, c])
            vh = kv_ref[:, D + hd * XA_HD:D + (hd + 1) * XA_HD]
            o_ref[:, c] = _dot(p.astype(BF16), vh).astype(BF16)

    return pl.pallas_call(
        body, name=name, grid=(L // tr,),
        in_specs=[_row_spec(tr, D), pl.BlockSpec((N_MEM, 2 * D), lambda i: (0, 0))],
        out_specs=_row_spec(tr, D), out_shape=jax.ShapeDtypeStruct((L, D), BF16),
        compiler_params=_cp("parallel"),
    )(q, kv)


def attn_bwd(q, kv, do, name):
    L, D = q.shape
    tr = min(XA_TR, L)

    def body(q_ref, kv_ref, do_ref, dq_ref, dkv_ref):
        @pl.when(pl.program_id(0) == 0)
        def _():
            dkv_ref[...] = jnp.zeros_like(dkv_ref)

        for hd in range(XA_HEADS):
            c = slice(hd * XA_HD, (hd + 1) * XA_HD)
            cv = slice(D + hd * XA_HD, D + (hd + 1) * XA_HD)
            qh, kh, vh, doh = q_ref[:, c], kv_ref[:, c], kv_ref[:, cv], do_ref[:, c]
            p = _xa_probs(qh, kh)
            dp = _dot(doh, vh, 1, 1)
            dkv_ref[:, cv] += _dot(p.astype(BF16), doh, 0, 0)
            ds = (p * (dp - jnp.sum(dp * p, axis=-1, keepdims=True)) * XA_SCALE).astype(BF16)
            dq_ref[:, c] = _dot(ds, kh).astype(BF16)
            dkv_ref[:, c] += _dot(ds, qh, 0, 0)

    return pl.pallas_call(
        body, name=name, grid=(L // tr,),
        in_specs=[_row_spec(tr, D), pl.BlockSpec((N_MEM, 2 * D), lambda i: (0, 0)), _row_spec(tr, D)],
        out_specs=[_row_spec(tr, D), pl.BlockSpec((N_MEM, 2 * D), lambda i: (0, 0))],
        out_shape=[jax.ShapeDtypeStruct((L, D), BF16), jax.ShapeDtypeStruct((N_MEM, 2 * D), F32)],
        compiler_params=_cp("arbitrary"),
    )(q, kv, do)


N_PAIRS = N_HEADS // 2
PAIRS_PER_GROUP = N_PAIRS // N_GROUPS
GN = N_GROUPS * D_STATE


def _softplus(x):
    t = jnp.exp(-jnp.abs(x))
    return jnp.maximum(x, 0.0) + jnp.where(t < 1e-4, t * (1.0 - 0.5 * t), jnp.log(1.0 + t))


def _dot3b(m, v, ca=1, cb=0):
    v0, v1, v2 = _split3(v)
    return _dot(m, v0, ca, cb) + _dot(m, v1, ca, cb) + _dot(m, v2, ca, cb)


def ssd_consts():
    h = lax.broadcasted_iota(jnp.int32, (LANE, D_INNER), 0)
    c = lax.broadcasted_iota(jnp.int32, (LANE, D_INNER), 1)
    expand = (c // HEAD_DIM == h).astype(BF16)
    r = lax.broadcasted_iota(jnp.int32, (CHUNK, CHUNK), 0)
    k = lax.broadcasted_iota(jnp.int32, (CHUNK, CHUNK), 1)
    tri = (k <= r).astype(BF16)
    return expand, tri


def _ssd_common(dtr_ref, prm_ref, e_ref, tri_ref):
    lane = lax.broadcasted_iota(jnp.int32, (CHUNK, LANE), 1)
    valid = lane < N_HEADS
    A = -jnp.exp(prm_ref[1:2, :])
    pre = dtr_ref[...] + prm_ref[0:1, :]
    dt = jnp.where(valid, _softplus(pre), 0.0)
    cs = _dot3b(tri_ref[...], dt * A)
    E = e_ref[...]
    dt_x = _dot2(dt, E)
    cs_x = _dot3(cs, E)
    csl_x = cs_x[CHUNK - 1:CHUNK, :]
    return dict(valid=valid, A=A, pre=pre, dt=dt, cs=cs, csT=cs.T, dt_x=dt_x, ecs_x=jnp.exp(cs_x),
                dend_x=jnp.exp(csl_x - cs_x), cd_x=jnp.exp(csl_x), D_x=_dot3(prm_ref[...], E)[2:3, :])


def ssd_fwd(xs, bc, dtr, zx, prm, ng, name):
    L = xs.shape[0]
    nc = L // CHUNK
    expand, tri = ssd_consts()

    def body(xs_ref, bc_ref, dtr_ref, z_ref, prm_ref, ng_ref, e_ref, tri_ref, y_ref, yn_ref, st_ref, state):
        @pl.when(pl.program_id(0) == 0)
        def _():
            state[...] = jnp.zeros_like(state)

        q = _ssd_common(dtr_ref, prm_ref, e_ref, tri_ref)
        cs, csT = q["cs"], q["csT"]
        xs_v = xs_ref[...]
        X = xs_v * q["dt_x"]
        Xb = X.astype(BF16)
        Xd = (X * q["dend_x"]).astype(BF16)
        ii = lax.broadcasted_iota(jnp.int32, (CHUNK, CHUNK), 0)
        jj = lax.broadcasted_iota(jnp.int32, (CHUNK, CHUNK), 1)
        tril = jj <= ii
        first = jj < HEAD_DIM
        for g in range(N_GROUPS):
            Bg = bc_ref[:, g * D_STATE:(g + 1) * D_STATE]
            Cg = bc_ref[:, GN + g * D_STATE:GN + (g + 1) * D_STATE]
            S = _dot(Cg, Bg, 1, 1)
            for pr in range(PAIRS_PER_GROUP):
                pair = g * PAIRS_PER_GROUP + pr
                cols = slice(pair * LANE, (pair + 1) * LANE)
                Xp = Xb[:, cols]
                ys = []
                for h in (2 * pair, 2 * pair + 1):
                    seg = cs[:, h:h + 1] - csT[h:h + 1, :]
                    M = (S * jnp.exp(jnp.where(tril, seg, NEG))).astype(BF16)
                    ys.append(_dot(M, Xp))
                prevT = state[pair]
                st_ref[0, pair] = prevT
                yoff = _dot(Cg, prevT.astype(BF16)) * q["ecs_x"][:, cols]
                y_ref[:, cols] = jnp.where(first, ys[0], ys[1]) + yoff + xs_v[:, cols] * q["D_x"][:, cols]
                state[pair] = prevT * q["cd_x"][:, cols] + _dot(Bg, Xd[:, cols], 0, 0)
        z = z_ref[...].astype(F32)
        gt = y_ref[...] * z * _sig(z)
        yn_ref[...] = (gt * _rms(gt) * ng_ref[...]).astype(BF16)

    row = lambda w: pl.BlockSpec((CHUNK, w), lambda c: (c, 0))
    const = lambda a: pl.BlockSpec(a.shape, lambda c: (0,) * a.ndim)
    return pl.pallas_call(
        body, name=name, grid=(nc,),
        in_specs=[row(D_INNER), row(2 * GN), row(LANE), row(D_INNER), const(prm), const(ng), const(expand), const(tri)],
        out_specs=[row(D_INNER), row(D_INNER), pl.BlockSpec((1, N_PAIRS, D_STATE, LANE), lambda c: (c, 0, 0, 0))],
        out_shape=[jax.ShapeDtypeStruct((L, D_INNER), F32), jax.ShapeDtypeStruct((L, D_INNER), BF16),
                   jax.ShapeDtypeStruct((nc, N_PAIRS, D_STATE, LANE), F32)],
        scratch_shapes=[pltpu.VMEM((N_PAIRS, D_STATE, LANE), F32)],
        compiler_params=_cp("arbitrary"),
    )(xs, bc, dtr, zx, prm, ng, expand, tri)


def ssd_bwd(dyn, y, zx, xs, bc, dtr, st, prm, ng, name):
    L = xs.shape[0]
    nc = L // CHUNK
    expand, tri = ssd_consts()

    def body(dyn_ref, y_ref, z_ref, xs_ref, bc_ref, dtr_ref, st_ref, prm_ref, ng_ref, e_ref, tri_ref,
             dxbc_ref, dz_ref, ddtr_ref, dng_ref, dprm_ref, dstate, g_cs, g_q, dX, g_row):
        step = pl.program_id(0)

        @pl.when(step == 0)
        def _():
            dstate[...] = jnp.zeros_like(dstate)
            dng_ref[...] = jnp.zeros_like(dng_ref)
            dprm_ref[...] = jnp.zeros_like(dprm_ref)
            g_row[...] = jnp.zeros_like(g_row)

        q = _ssd_common(dtr_ref, prm_ref, e_ref, tri_ref)
        cs, csT, E = q["cs"], q["csT"], e_ref[...]
        xs_v = xs_ref[...]
        X = xs_v * q["dt_x"]
        Xb = X.astype(BF16)
        Xd_f = X * q["dend_x"]
        Xd = Xd_f.astype(BF16)

        yv = y_ref[...]
        z = z_ref[...].astype(F32)
        sz = _sig(z)
        silu = z * sz
        gt = yv * silu
        r = _rms(gt)
        gn = gt * r
        dyn_v = dyn_ref[...]
        dng_ref[...] += jnp.sum(dyn_v * gn, axis=0, keepdims=True)
        dgn = dyn_v * ng_ref[...]
        dgt = r * (dgn - gn * jnp.mean(dgn * gn, axis=-1, keepdims=True))
        dY = dgt * silu
        dz_ref[...] = (dgt * yv * sz * (1.0 + z * (1.0 - sz))).astype(BF16)
        dYb = dY.astype(BF16)
        g_row[1:2, :] += jnp.sum(dY * xs_v, axis=0, keepdims=True)

        ii = lax.broadcasted_iota(jnp.int32, (CHUNK, CHUNK), 0)
        jj = lax.broadcasted_iota(jnp.int32, (CHUNK, CHUNK), 1)
        tril = jj <= ii
        triu = jj >= ii
        first = jj < HEAD_DIM
        lane_row = lax.broadcasted_iota(jnp.int32, (1, LANE), 1)
        sub_col = lax.broadcasted_iota(jnp.int32, (CHUNK, 1), 0)
        dcs_col = jnp.zeros((CHUNK, LANE), F32)
        dcs_rowT = jnp.zeros((LANE, CHUNK), F32)
        for g in range(N_GROUPS):
            Bg = bc_ref[:, g * D_STATE:(g + 1) * D_STATE]
            Cg = bc_ref[:, GN + g * D_STATE:GN + (g + 1) * D_STATE]
            S = _dot(Cg, Bg, 1, 1)
            ST = _dot(Bg, Cg, 1, 1)
            dS = jnp.zeros((CHUNK, CHUNK), F32)
            dCg = jnp.zeros((CHUNK, D_STATE), F32)
            dBg = jnp.zeros((CHUNK, D_STATE), F32)
            for pr in range(PAIRS_PER_GROUP):
                pair = g * PAIRS_PER_GROUP + pr
                cols = slice(pair * LANE, (pair + 1) * LANE)
                Xp = Xb[:, cols]
                dYp_f = dY[:, cols]
                dYp = dYb[:, cols]
                prevT = st_ref[0, pair]
                prevTb = prevT.astype(BF16)
                dst = dstate[pair]
                dstb = dst.astype(BF16)
                ecs_p = q["ecs_x"][:, cols]
                g_cs[:, cols] = dYp_f * (_dot(Cg, prevTb) * ecs_p)
                dWb = (dYp_f * ecs_p).astype(BF16)
                dprev = dst * q["cd_x"][:, cols] + _dot(Cg, dWb, 0, 0)
                dCg = dCg + _dot(dWb, prevTb, 1, 1)
                g_row[0:1, cols] = jnp.sum(dst * prevT, axis=0, keepdims=True)
                dXp = None
                for hh, h in enumerate((2 * pair, 2 * pair + 1)):
                    mine = first if hh == 0 else jnp.logical_not(first)
                    seg = cs[:, h:h + 1] - csT[h:h + 1, :]
                    lam = jnp.exp(jnp.where(tril, seg, NEG))
                    dM = _dot(jnp.where(mine, dYp, jnp.zeros_like(dYp)), Xp, 1, 1)
                    dS = dS + dM * lam
                    Gm = dM * (S * lam)
                    dcs_col = dcs_col + jnp.sum(Gm, axis=1, keepdims=True) * (lane_row == h).astype(F32)
                    dcs_rowT = dcs_rowT + (sub_col == h).astype(F32) * jnp.sum(Gm, axis=0, keepdims=True)
                    MT = (ST * jnp.exp(jnp.where(triu, -seg, NEG))).astype(BF16)
                    t = _dot(MT, dYp)
                    dXp = t if dXp is None else jnp.where(first, dXp, t)
                dXd = _dot(Bg, dstb)
                dBg = dBg + _dot(Xd[:, cols], dstb, 1, 1)
                g_q[:, cols] = dXd * Xd_f[:, cols]
                dX[:, cols] = dXp + dXd * q["dend_x"][:, cols]
                dstate[pair] = dprev
            dSb = dS.astype(BF16)
            dxbc_ref[:, D_INNER + g * D_STATE:D_INNER + (g + 1) * D_STATE] = dBg + _dot(dSb, Cg, 0, 0)
            dxbc_ref[:, D_INNER + GN + g * D_STATE:D_INNER + GN + (g + 1) * D_STATE] = dCg + _dot(dSb, Bg)
        dXv = dX[...]
        dxbc_ref[:, 0:D_INNER] = q["D_x"] * dY + dXv * q["dt_x"]
        r_dt = _dot2(dXv * xs_v, E, 1, 1)
        r_cs = _dot2(g_cs[...], E, 1, 1)
        r_q = _dot2(g_q[...], E, 1, 1)
        r_row = _dot2(g_row[...], E, 1, 1)
        cd = jnp.exp(cs[CHUNK - 1:CHUNK, :])
        dcs_last = jnp.sum(r_q, axis=0, keepdims=True) + r_row[0:1, :] * cd
        dcs = r_cs - r_q + dcs_col - dcs_rowT.T + jnp.where(sub_col == CHUNK - 1, dcs_last, 0.0)
        da = _dot3b(tri_ref[...], dcs, 0, 0)
        dpre = jnp.where(q["valid"], (r_dt + da * q["A"]) * _sig(q["pre"]), 0.0)
        ddtr_ref[...] = dpre
        dprm_ref[0:1, :] += jnp.sum(dpre, axis=0, keepdims=True)
        dprm_ref[1:2, :] += jnp.sum(da * q["dt"], axis=0, keepdims=True) * q["A"]
        dprm_ref[2:3, :] = r_row[1:2, :]

    rev = lambda w: pl.BlockSpec((CHUNK, w), lambda c: (nc - 1 - c, 0))
    const = lambda a: pl.BlockSpec(a.shape, lambda c: (0,) * a.ndim)
    return pl.pallas_call(
        body, name=name, grid=(nc,),
        in_specs=[rev(D_INNER), rev(D_INNER), rev(D_INNER), rev(D_INNER), rev(2 * GN), rev(LANE),
                  pl.BlockSpec((1, N_PAIRS, D_STATE, LANE), lambda c: (nc - 1 - c, 0, 0, 0)),
                  const(prm), const(ng), const(expand), const(tri)],
        out_specs=[rev(CONV_DIM), rev(D_INNER), rev(LANE),
                   pl.BlockSpec((1, D_INNER), lambda c: (0, 0)), pl.BlockSpec((8, LANE), lambda c: (0, 0))],
        out_shape=[jax.ShapeDtypeStruct((L, CONV_DIM), F32), jax.ShapeDtypeStruct((L, D_INNER), BF16),
                   jax.ShapeDtypeStruct((L, LANE), F32), jax.ShapeDtypeStruct((1, D_INNER), F32),
                   jax.ShapeDtypeStruct((8, LANE), F32)],
        scratch_shapes=[pltpu.VMEM((N_PAIRS, D_STATE, LANE), F32), pltpu.VMEM((CHUNK, D_INNER), F32),
                        pltpu.VMEM((CHUNK, D_INNER), F32), pltpu.VMEM((CHUNK, D_INNER), F32),
                        pltpu.VMEM((8, D_INNER), F32)],
        compiler_params=_cp("arbitrary"),
    )(dyn, y, zx, xs, bc, dtr, st, prm, ng, expand, tri)


def _ssd_weights(W, j):
    w_in = W["ssm_in_w"][j]
    nzx = D_INNER + CONV_DIM
    wdt = jnp.pad(w_in[:, nzx:], ((0, 0), (0, LANE - N_HEADS)))
    prm = jnp.zeros((8, LANE), F32)
    prm = prm.at[0, :N_HEADS].set(W["ssm_dt_bias"][j]).at[1, :N_HEADS].set(W["ssm_A_log"][j])
    prm = prm.at[2, :N_HEADS].set(W["ssm_D"][j])
    return dict(wdt=wdt, cw=W["ssm_conv_w"][j], cb=W["ssm_conv_b"][j].reshape(1, CONV_DIM), prm=prm,
                ng=W["ssm_norm_g"][j].reshape(1, D_INNER))


def ssd_layer_fwd(h, W, j, tag):
    p = _ssd_weights(W, j)
    zx = mm(h, W["ssm_in_w"], layer=j, b_n=D_INNER + CONV_DIM, out_dtype=BF16, name=f"{tag}_zx")
    dtr = mm(h, p["wdt"], name=f"{tag}_dt")
    xs, pre_x = ssm_conv_fwd(zx, p["cw"], p["cb"], col0=D_INNER, ncols=D_INNER, wcol0=0, out_dtype=F32,
                             name=f"{tag}_convx")
    bc, pre_bc = ssm_conv_fwd(zx, p["cw"], p["cb"], col0=2 * D_INNER, ncols=2 * GN, wcol0=D_INNER, out_dtype=BF16,
                              name=f"{tag}_convbc")
    y, yn, st = ssd_fwd(xs, bc, dtr, zx, p["prm"], p["ng"], name=f"{tag}_scan")
    f = mm(yn, W["ssm_out_w"], layer=j, out_dtype=BF16, name=f"{tag}_out")
    return f, dict(h=h, zx=zx, dtr=dtr, xs=xs, bc=bc, pre_x=pre_x, pre_bc=pre_bc, y=y, yn=yn, st=st, p=p)


def ssd_layer_bwd(df, ctx, W, GB, j, tag, done=None):
    p = ctx["p"]
    h = ctx["h"]
    dyn = mm(df, W["ssm_out_w"], layer=j, tb=True, name=f"{tag}_b_dyn")
    GB["ssm_out_w"] = mm(ctx["yn"], df, ta=True, into=(GB["ssm_out_w"], j, 0), name=f"{tag}_b_gwo")
    dxbc, dz, ddtr, dng, dprm = ssd_bwd(dyn, ctx["y"], ctx["zx"], ctx["xs"], ctx["bc"], ctx["dtr"], ctx["st"],
                                        p["prm"], p["ng"], name=f"{tag}_b_scan")
    dx1, dcw1, dcb1 = ssm_conv_bwd(ctx["zx"], ctx["pre_x"], dxbc, p["cw"], col0=D_INNER, dcol0=0, ncols=D_INNER,
                                   name=f"{tag}_b_convx")
    dx2, dcw2, dcb2 = ssm_conv_bwd(ctx["zx"], ctx["pre_bc"], dxbc, p["cw"], col0=2 * D_INNER, dcol0=D_INNER,
                                   ncols=2 * GN, name=f"{tag}_b_convbc")
    g_in = jnp.concatenate([mm(h, dz, ta=True, out_dtype=BF16, name=f"{tag}_b_gz"),
                            mm(h, dx1, ta=True, out_dtype=BF16, name=f"{tag}_b_gx"),
                            mm(h, dx2, ta=True, out_dtype=BF16, name=f"{tag}_b_gbc"),
                            mm(h, ddtr, ta=True, out_dtype=BF16, name=f"{tag}_b_gdt")[:, :N_HEADS]], axis=1)
    gw = dict(ssm_conv_w=jnp.concatenate([dcw1, dcw2], axis=1), ssm_conv_b=jnp.concatenate([dcb1, dcb2], axis=1)[0],
              ssm_dt_bias=dprm[0, :N_HEADS], ssm_A_log=dprm[1, :N_HEADS], ssm_D=dprm[2, :N_HEADS], ssm_norm_g=dng[0])
    if done is None:
        gw["ssm_in_w"] = g_in
        token = None
    else:
        token = done(g_in)
    dh = mm(dz, W["ssm_in_w"], layer=j, tb=True, b_k0=0, dep=token, name=f"{tag}_b_dh1")
    dh = mm(dx1, W["ssm_in_w"], layer=j, tb=True, b_k0=D_INNER, acc=dh, name=f"{tag}_b_dh2")
    dh = mm(dx2, W["ssm_in_w"], layer=j, tb=True, b_k0=2 * D_INNER, acc=dh, name=f"{tag}_b_dh3")
    dh = mm(ddtr, p["wdt"], tb=True, acc=dh, out_dtype=BF16, name=f"{tag}_b_dh4")
    return dh, gw


def cf_layer_fwd(h, W, j, tag):
    u = mm(h, W["cf_pw1_w"], layer=j, bias=W["cf_pw1_b"][j], out_dtype=BF16, name=f"{tag}_pw1")
    c, s = cf_fwd(u, W["cf_dw_w"][j], W["cf_dw_b"][j].reshape(1, -1), W["cf_ln_g"][j].reshape(1, -1),
                  W["cf_ln_b"][j].reshape(1, -1), name=f"{tag}_conv")
    f = mm(s, W["cf_pw2_w"], layer=j, bias=W["cf_pw2_b"][j], out_dtype=BF16, name=f"{tag}_pw2")
    return f, dict(h=h, u=u, c=c, s=s)


def cf_layer_bwd(df, ctx, W, GB, j, tag):
    h = ctx["h"]
    ds = mm(df, W["cf_pw2_w"], layer=j, tb=True, name=f"{tag}_b_ds")
    GB["cf_pw2_w"] = mm(ctx["s"], df, ta=True, into=(GB["cf_pw2_w"], j, 0), name=f"{tag}_b_gpw2")
    g_b2 = colsum(df, name=f"{tag}_b_gb2")
    dc, dlg, dlb = cf_bwd_ln(ctx["c"], ds, W["cf_ln_g"][j].reshape(1, -1), W["cf_ln_b"][j].reshape(1, -1),
                             name=f"{tag}_b_ln")
    du, ddw, ddb = cf_bwd_conv(ctx["u"], dc, W["cf_dw_w"][j], name=f"{tag}_b_conv")
    dh = mm(du, W["cf_pw1_w"], layer=j, tb=True, out_dtype=BF16, name=f"{tag}_b_dh")
    GB["cf_pw1_w"] = mm(h, du, ta=True, into=(GB["cf_pw1_w"], j, 0), name=f"{tag}_b_gpw1")
    g_b1 = colsum(du, name=f"{tag}_b_gb1")
    return dh, dict(cf_pw1_b=g_b1[0], cf_dw_w=ddw, cf_dw_b=ddb[0], cf_ln_g=dlg[0], cf_ln_b=dlb[0], cf_pw2_b=g_b2[0])


def xa_layer_fwd(h, mem, W, i, tag):
    m = norm_fwd(mem, W["xa_mem_g"][i], name=f"{tag}_memnorm")
    kv = mm(m, W["xa_kv_w"], layer=i, out_dtype=BF16, name=f"{tag}_kv")
    q = mm(h, W["xa_q_w"], layer=i, out_dtype=BF16, name=f"{tag}_q")
    o = attn_fwd(q, kv, name=f"{tag}_attn")
    f = mm(o, W["xa_o_w"], layer=i, out_dtype=BF16, name=f"{tag}_o")
    return f, dict(h=h, m=m, kv=kv, q=q, o=o)


def xa_layer_bwd(df, ctx, mem, W, GB, i, tag):
    h = ctx["h"]
    do = mm(df, W["xa_o_w"], layer=i, tb=True, out_dtype=BF16, name=f"{tag}_b_do")
    GB["xa_o_w"] = mm(ctx["o"], df, ta=True, into=(GB["xa_o_w"], i, 0), name=f"{tag}_b_go")
    dq, dkv = attn_bwd(ctx["q"], ctx["kv"], do, name=f"{tag}_b_attn")
    dh = mm(dq, W["xa_q_w"], layer=i, tb=True, out_dtype=BF16, name=f"{tag}_b_dh")
    GB["xa_q_w"] = mm(h, dq, ta=True, into=(GB["xa_q_w"], i, 0), name=f"{tag}_b_gq")
    GB["xa_kv_w"] = mm(ctx["m"], dkv, ta=True, into=(GB["xa_kv_w"], i, 0), name=f"{tag}_b_gkv")
    dm = mm(dkv, W["xa_kv_w"], layer=i, tb=True, name=f"{tag}_b_dm")
    g_mg = norm_dg(mem, dm, name=f"{tag}_b_gmem")
    return dh, dict(xa_mem_g=g_mg[0])


def ffn_layer_fwd(h, W, i, tag):
    cw, cb = W["ffn_conv_w"][i], W["ffn_conv_b"][i].reshape(1, -1)
    u = mm(h, W["ffn_in_w"], layer=i, out_dtype=BF16, name=f"{tag}_in")
    act, c = ffn_act_fwd(u, cw, cb, name=f"{tag}_act")
    f = mm(act, W["ffn_out_w"], layer=i, out_dtype=BF16, name=f"{tag}_out")
    return f, dict(h=h, u=u, c=c, act=act)


def ffn_layer_bwd(df, ctx, W, GB, i, tag):
    h = ctx["h"]
    dact = mm(df, W["ffn_out_w"], layer=i, tb=True, out_dtype=BF16, name=f"{tag}_b_dact")
    GB["ffn_out_w"] = mm(ctx["act"], df, ta=True, into=(GB["ffn_out_w"], i, 0), name=f"{tag}_b_gout")
    du, dcw, dcb = ffn_act_bwd(ctx["u"], ctx["c"], dact, W["ffn_conv_w"][i], name=f"{tag}_b_act")
    dh = None
    for half in range(2):
        dh = mm(du, W["ffn_in_w"], a_idx=half, layer=i, tb=True, b_k0=half * D_FF, acc=dh,
                out_dtype=BF16 if half else F32, name=f"{tag}_b_dh{half}")
        GB["ffn_in_w"] = mm(h, du, ta=True, layer=half, into=(GB["ffn_in_w"], i, half * D_FF), name=f"{tag}_b_gin{half}")
    cat = lambda a: jnp.concatenate([a[0], a[1]], axis=-1)
    return dh, dict(ffn_conv_w=cat(dcw), ffn_conv_b=cat(dcb)[0])


def _sublayer_weights(i, s):
    if s == 0:
        return [("ssm_in_w", i // 2), ("ssm_out_w", i // 2)] if i % 2 == 0 else [("cf_pw1_w", i // 2), ("cf_pw2_w", i // 2)]
    return [(n, i) for n in (("xa_q_w", "xa_kv_w", "xa_o_w") if s == 1 else ("ffn_in_w", "ffn_out_w"))]


def local_step(x, mem, target, W, fetch=None, layer_done=None):
    subs = [(i, s) for i in range(DEPTH) for s in range(3)]
    ng = W["norm_g"]

    def fwd(i, s, h):
        tag = f"l{i}s{s}"
        if s == 0:
            return ssd_layer_fwd(h, W, i // 2, tag) if i % 2 == 0 else cf_layer_fwd(h, W, i // 2, tag)
        if s == 1:
            return xa_layer_fwd(h, mem, W, i, tag)
        return ffn_layer_fwd(h, W, i, tag)

    GB = {}

    early = {}

    def bwd(i, s, df, ctx):
        tag = f"l{i}s{s}"
        if s == 0 and i % 2 == 0:
            def done(g_in):
                early[i, s] = layer_done(i, s, GB, g_in)
                return early[i, s]
            return ssd_layer_bwd(df, ctx, W, GB, i // 2, tag, done if layer_done is not None else None)
        if s == 0:
            return cf_layer_bwd(df, ctx, W, GB, i // 2, tag)
        if s == 1:
            return xa_layer_bwd(df, ctx, mem, W, GB, i, tag)
        return ffn_layer_bwd(df, ctx, W, GB, i, tag)

    h = norm_fwd(x, ng[0, 0], name="norm0")
    saved = []
    dxp = loss = None
    for k, (i, s) in enumerate(subs):
        if fetch is not None:
            fetch(i, s, x)
        f, ctx = fwd(i, s, h)
        saved.append((x, f, ctx))
        if k + 1 < len(subs):
            ni, ns = subs[k + 1]
            x, h = bnd_fwd(x, f, ng[i, 2 * s + 1], ng[ni, 2 * ns], name=f"bnd{k}")
        else:
            dxp, loss = final_fwd(x, f, ng[i, 2 * s + 1], target, name="final")

    for n in BIG:
        if n != "ssm_in_w":
            GB[n] = jnp.zeros((len(W[n]), *W[n][0].shape), BF16)
    grads = {}

    def put(name, idx, val):
        grads.setdefault(name, {})[idx] = val

    i, s = subs[-1]
    top = bnd_bwd(dxp, post=(saved[-1][1], ng[i, 2 * s + 1]), name="bbnd_top")
    put("norm_g", (i, 2 * s + 1), top["dgpost"][0])
    df = top["df"]
    for k in range(len(subs) - 1, -1, -1):
        i, s = subs[k]
        xk, _, ctx = saved[k]
        dh, gw = bwd(i, s, df, ctx)
        for name, val in gw.items():
            put(name, i // 2 if name.startswith(("ssm_", "cf_")) else i, val)
        dep = early.get((i, s))
        if layer_done is not None and dep is None:
            dep = layer_done(i, s, GB, None)
        if k > 0:
            pi, ps = subs[k - 1]
            r = bnd_bwd(dxp, pre=(xk, ng[i, 2 * s], dh), post=(saved[k - 1][1], ng[pi, 2 * ps + 1]), dep=dep,
                        name=f"bbnd{k}")
            put("norm_g", (pi, 2 * ps + 1), r["dgpost"][0])
            df = r["df"]
        else:
            r = bnd_bwd(dxp, pre=(xk, ng[i, 2 * s], dh), dep=dep, name="bbnd0")
        put("norm_g", (i, 2 * s), r["dgpre"][0])
        dxp = r["dx"]

    out = {} if layer_done is not None else dict(GB)
    for name, d in grads.items():
        if name == "norm_g":
            out[name] = jnp.stack([jnp.stack([d[(i, t)] for t in range(6)]) for i in range(DEPTH)])
        elif d:
            out[name] = jnp.stack([d[j] for j in sorted(d)])
    return loss, dxp, out


ANY = pl.BlockSpec(memory_space=pl.ANY)


def _pos():
    return lax.axis_index("x"), lax.axis_index("y"), lax.axis_index("c")


def all_gather(shard, name):
    R, C = shard.shape

    def body(x_ref, out_ref, send_sems, recv_sems, local_sem):
        x, y, c = _pos()
        me, sibling = (x, y, c), (x, y, 1 - c)
        chips = [(1 - x, y), (x, 1 - y), (1 - x, 1 - y)]

        def slot(px, py, pc):
            return out_ref.at[4 * px + 2 * py + pc]

        def copy(k, block, to, src=None):
            return pltpu.make_async_remote_copy(
                src_ref=slot(*block) if src is None else src, dst_ref=slot(*block),
                send_sem=send_sems.at[k], recv_sem=recv_sems.at[k], device_id=to, device_id_type=MESH)

        mine = pltpu.make_async_copy(x_ref, slot(*me), local_sem)
        mine.start()
        first = [copy(0, me, sibling, src=x_ref)]
        first += [copy(1 + j, me, (*chip, c), src=x_ref) for j, chip in enumerate(chips)]
        for cp in first:
            cp.start()
        passed = [copy(4 + j, (*chip, c), sibling) for j, chip in enumerate(chips)]
        for j, chip in enumerate(chips):
            copy(1 + j, (*chip, c), me).wait_recv()
            passed[j].start()
        copy(0, sibling, me).wait_recv()
        for j, chip in enumerate(chips):
            copy(4 + j, (*chip, 1 - c), me).wait_recv()
        for cp in first + passed:
            cp.wait_send()
        mine.wait()

    return pl.pallas_call(
        body, name=name, out_shape=jax.ShapeDtypeStruct((N_DEV, R, C), shard.dtype),
        in_specs=[ANY], out_specs=ANY,
        scratch_shapes=[pltpu.SemaphoreType.DMA((7,)), pltpu.SemaphoreType.DMA((7,)), pltpu.SemaphoreType.DMA(())],
    )(shard)


def _win(ref, kind, k, a, b):
    if kind == "lead":
        return ref.at[k]
    if kind == "row":
        return ref.at[:, pl.ds(pl.multiple_of(k * a, 16), a), :]
    return ref.at[:, :, pl.ds(pl.multiple_of(k * b, LANE), b)]


def _full_shape(shard_shape, kind):
    n, a, b = shard_shape
    return {"lead": (N_DEV, n, a, b), "row": (n, N_DEV * a, b), "col": (n, a, N_DEV * b)}[kind]


HBM = pl.BlockSpec(memory_space=pltpu.HBM)
SEMS = pl.BlockSpec(memory_space=pltpu.SEMAPHORE)
DATAFLOW = pltpu.SideEffectType.DATAFLOW_SIDE_EFFECTING
N_PEER = N_DEV - 1


def _in_hbm(a):
    return pltpu.with_memory_space_constraint(a, pltpu.HBM)


def _peer(x, y, c, r):
    return ((1 - x) if r & 4 else x, (1 - y) if r & 2 else y, (1 - c) if r & 1 else c)


def _win2(ref, kind, k, a, b):
    if kind == "lead":
        return ref.at[k]
    if kind == "row":
        return ref.at[pl.ds(pl.multiple_of(k * a, 16), a), :]
    return ref.at[:, pl.ds(pl.multiple_of(k * b, LANE), b)]


def _zone_shape(kind, a, b):
    return {"lead": (N_DEV, a, b), "row": (N_DEV * a, b), "col": (a, N_DEV * b)}[kind]


def gather_start(shards, items, after, name):
    ns, nz, na = len(shards), len(items), len(after)
    zones = [lax.empty(_zone_shape(kind, a, b), shards[w].dtype) for w, l, kind, a, b in items]

    def body(*refs):
        x_refs = refs[:ns]
        send_sems, recv_sems, local_sems = refs[ns + nz + na:ns + nz + na + 3]
        z_refs = refs[ns + nz + na + 3 + ns:ns + nz + na + 3 + ns + nz]
        token = refs[-1]
        x, y, c = _pos()
        me = 4 * x + 2 * y + c
        for t, (w, l, kind, a, b) in enumerate(items):
            mine = _win2(z_refs[t], kind, me, a, b)
            pltpu.make_async_copy(x_refs[w].at[l], mine, local_sems.at[t]).start()
            for r in range(1, N_DEV):
                pltpu.make_async_remote_copy(
                    src_ref=x_refs[w].at[l], dst_ref=mine,
                    send_sem=send_sems.at[N_PEER * t + r - 1], recv_sem=recv_sems.at[N_PEER * t + r - 1],
                    device_id=_peer(x, y, c, r), device_id_type=MESH).start()
        token[...] = jnp.zeros_like(token)

    n_sem = N_PEER * nz
    outs = pl.pallas_call(
        body, name=name,
        out_shape=(pltpu.SemaphoreType.DMA((n_sem,)), pltpu.SemaphoreType.DMA((n_sem,)), pltpu.SemaphoreType.DMA((nz,)),
                   *[pltpu.HBM(s.shape, s.dtype) for s in shards], *[pltpu.HBM(z.shape, z.dtype) for z in zones],
                   jax.ShapeDtypeStruct((8, LANE), F32)),
        in_specs=[HBM] * (ns + nz) + [pl.BlockSpec(memory_space=pl.ANY)] * na,
        out_specs=(SEMS, SEMS, SEMS, *[HBM] * (ns + nz), pl.BlockSpec(memory_space=pltpu.VMEM)),
        input_output_aliases={i: 3 + i for i in range(ns + nz)},
        compiler_params=pltpu.CompilerParams(has_side_effects=DATAFLOW),
    )(*[_in_hbm(s) for s in shards], *[_in_hbm(z) for z in zones], *after)
    return outs[:3], list(outs[3:3 + ns]), list(outs[3 + ns:3 + ns + nz]), outs[-1]


def gather_wait(zones, idx, items, sems, after, keep, name):
    nz, nk = len(zones), len(keep)

    def body(*refs):
        z_refs = refs[:nz]
        send_sems, recv_sems, local_sems = refs[nz:nz + 3]
        x, y, c = _pos()
        me = 4 * x + 2 * y + c
        for z_ref, t in zip(z_refs, idx):
            w, l, kind, a, b = items[t]
            mine = _win2(z_ref, kind, me, a, b)
            pltpu.make_async_copy(mine, mine, local_sems.at[t]).wait()
            for r in range(1, N_DEV):
                peer = _peer(x, y, c, r)
                cp = pltpu.make_async_remote_copy(
                    src_ref=mine, dst_ref=_win2(z_ref, kind, 4 * peer[0] + 2 * peer[1] + peer[2], a, b),
                    send_sem=send_sems.at[N_PEER * t + r - 1], recv_sem=recv_sems.at[N_PEER * t + r - 1],
                    device_id=peer, device_id_type=MESH)
                cp.wait_send()
                cp.wait_recv()

    outs = pl.pallas_call(
        body, name=name, out_shape=tuple(pltpu.HBM(z.shape, z.dtype) for z in zones),
        in_specs=[HBM] * nz + [SEMS] * 3 + [pl.BlockSpec(memory_space=pl.ANY)] * (1 + nk),
        out_specs=tuple([HBM] * nz), input_output_aliases={i: i for i in range(nz)},
        compiler_params=pltpu.CompilerParams(has_side_effects=DATAFLOW),
    )(*zones, *sems, after, *keep)
    return list(outs)


def gather_now(shards, kinds, name):
    nw = len(shards)
    geo = [s.shape[1:] for s in shards]

    def body(*refs):
        x_refs, o_refs = refs[:nw], refs[nw:2 * nw]
        send_sems, recv_sems, local_sems = refs[2 * nw:]
        x, y, c = _pos()
        me, sibling = (x, y, c), (x, y, 1 - c)
        chips = [(1 - x, y), (x, 1 - y), (1 - x, 1 - y)]

        def slot(w, px, py, pc):
            return _win(o_refs[w], kinds[w], 4 * px + 2 * py + pc, *geo[w])

        def copy(w, k, block, to, src=None):
            return pltpu.make_async_remote_copy(
                src_ref=slot(w, *block) if src is None else src, dst_ref=slot(w, *block),
                send_sem=send_sems.at[7 * w + k], recv_sem=recv_sems.at[7 * w + k], device_id=to, device_id_type=MESH)

        mine = [pltpu.make_async_copy(x_refs[w], slot(w, *me), local_sems.at[w]) for w in range(nw)]
        for cp in mine:
            cp.start()
        first = []
        for w in range(nw):
            first.append(copy(w, 0, me, sibling, src=x_refs[w]))
            first += [copy(w, 1 + j, me, (*chip, c), src=x_refs[w]) for j, chip in enumerate(chips)]
        for cp in first:
            cp.start()
        passed = []
        for w in range(nw):
            for j, chip in enumerate(chips):
                copy(w, 1 + j, (*chip, c), me).wait_recv()
                cp = copy(w, 4 + j, (*chip, c), sibling)
                cp.start()
                passed.append(cp)
        for w in range(nw):
            copy(w, 0, sibling, me).wait_recv()
            for j, chip in enumerate(chips):
                copy(w, 4 + j, (*chip, 1 - c), me).wait_recv()
        for cp in first + passed:
            cp.wait_send()
        for cp in mine:
            cp.wait()

    return pl.pallas_call(
        body, name=name,
        out_shape=[jax.ShapeDtypeStruct(_full_shape(s.shape, k), s.dtype) for s, k in zip(shards, kinds)],
        in_specs=[ANY] * nw, out_specs=[ANY] * nw,
        scratch_shapes=[pltpu.SemaphoreType.DMA((7 * nw,)), pltpu.SemaphoreType.DMA((7 * nw,)),
                        pltpu.SemaphoreType.DMA((nw,))],
    )(*shards)


def _src_win(ref, l, kind, k, a, b):
    return _win2(ref if l is None else ref.at[l], kind, k, a, b)


def rs_start(srcs, items, name):
    ns, nz = len(srcs), len(items)
    zones = [lax.empty((N_PEER, a, b), srcs[w].dtype) for w, l, kind, a, b in items]

    def body(*refs):
        s_refs = refs[:ns]
        send_sems, recv_sems = refs[ns + nz], refs[ns + nz + 1]
        z_refs = refs[ns + nz + 2 + ns:ns + nz + 2 + ns + nz]
        token = refs[-1]
        x, y, c = _pos()
        for t, (w, l, kind, a, b) in enumerate(items):
            for r in range(1, N_DEV):
                peer = _peer(x, y, c, r)
                pltpu.make_async_remote_copy(
                    src_ref=_src_win(s_refs[w], l, kind, 4 * peer[0] + 2 * peer[1] + peer[2], a, b),
                    dst_ref=z_refs[t].at[r - 1],
                    send_sem=send_sems.at[N_PEER * t + r - 1], recv_sem=recv_sems.at[N_PEER * t + r - 1],
                    device_id=peer, device_id_type=MESH).start()
        token[...] = jnp.zeros_like(token)

    n_sem = N_PEER * nz
    outs = pl.pallas_call(
        body, name=name,
        out_shape=(pltpu.SemaphoreType.DMA((n_sem,)), pltpu.SemaphoreType.DMA((n_sem,)),
                   *[pltpu.HBM(s.shape, s.dtype) for s in srcs], *[pltpu.HBM(z.shape, z.dtype) for z in zones],
                   jax.ShapeDtypeStruct((8, LANE), F32)),
        in_specs=[HBM] * (ns + nz), out_specs=(SEMS, SEMS, *[HBM] * (ns + nz), pl.BlockSpec(memory_space=pltpu.VMEM)),
        input_output_aliases={i: 2 + i for i in range(ns + nz)},
        compiler_params=pltpu.CompilerParams(has_side_effects=DATAFLOW),
    )(*[_in_hbm(s) for s in srcs], *[_in_hbm(z) for z in zones])
    return outs[:2], list(outs[2:2 + ns]), list(outs[2 + ns:2 + ns + nz]), outs[-1]


def rs_wait(zones, items, sems, after, keep, name):
    nz, nk = len(zones), len(keep)

    def body(*refs):
        z_refs = refs[:nz]
        send_sems, recv_sems = refs[nz], refs[nz + 1]
        x, y, c = _pos()
        for t, z_ref in enumerate(z_refs):
            for r in range(1, N_DEV):
                cp = pltpu.make_async_remote_copy(
                    src_ref=z_ref.at[r - 1], dst_ref=z_ref.at[r - 1],
                    send_sem=send_sems.at[N_PEER * t + r - 1], recv_sem=recv_sems.at[N_PEER * t + r - 1],
                    device_id=_peer(x, y, c, r), device_id_type=MESH)
                cp.wait_send()
                cp.wait_recv()

    outs = pl.pallas_call(
        body, name=name, out_shape=tuple(pltpu.HBM(z.shape, z.dtype) for z in zones),
        in_specs=[HBM] * nz + [SEMS] * 2 + [pl.BlockSpec(memory_space=pl.ANY)] * (1 + nk),
        out_specs=tuple([HBM] * nz), input_output_aliases={i: i for i in range(nz)},
        compiler_params=pltpu.CompilerParams(has_side_effects=DATAFLOW),
    )(*zones, *sems, after, *keep)
    return list(outs)


def adam_rs(w, m, v, l, own, kind, zone, outs, name):
    n, a, b = w.shape
    ta = max(t for t in range(16, min(a, 256) + 1, 16) if a % t == 0)
    per = a // ta
    me = (4 * lax.axis_index("x") + 2 * lax.axis_index("y") + lax.axis_index("c")).astype(jnp.int32).reshape(1)

    def body(me_ref, w_ref, m_ref, v_ref, own_ref, z_ref, i0, i1, i2, i3, g_ref, d_ref, m2_ref, v2_ref):
        gv = own_ref[...].astype(F32)
        for k in range(N_PEER):
            gv = gv + z_ref[k].astype(F32)
        m2 = ADAM_B1 * m_ref[...] + (1.0 - ADAM_B1) * gv
        v2 = ADAM_B2 * v_ref[...] + (1.0 - ADAM_B2) * (gv * gv)
        m_hat = m2 / (1.0 - ADAM_B1 ** ADAM_STEP)
        v_hat = v2 / (1.0 - ADAM_B2 ** ADAM_STEP)
        g_ref[...] = gv
        d_ref[...] = -ADAM_LR * (m_hat / (jnp.sqrt(v_hat) + ADAM_EPS) + ADAM_WD * w_ref[...])
        m2_ref[...] = m2
        v2_ref[...] = v2

    spec = pl.BlockSpec((None, ta, b), lambda r, me_ref: (l, r, 0))
    if kind == "lead":
        own_spec = pl.BlockSpec((None, ta, b), lambda r, me_ref: (me_ref[0], r, 0))
    elif kind == "row":
        own_spec = pl.BlockSpec((None, ta, b), lambda r, me_ref: (l, me_ref[0] * per + r, 0))
    else:
        own_spec = pl.BlockSpec((None, ta, b), lambda r, me_ref: (l, r, me_ref[0]))
    return pl.pallas_call(
        body, name=name, out_shape=[jax.ShapeDtypeStruct((n, a, b), F32)] * 4,
        grid_spec=pltpu.PrefetchScalarGridSpec(
            num_scalar_prefetch=1, grid=(per,),
            in_specs=[spec] * 3 + [own_spec, pl.BlockSpec((N_PEER, ta, b), lambda r, me_ref: (0, r, 0))] + [ANY] * 4,
            out_specs=[spec] * 4),
        input_output_aliases={6 + k: k for k in range(4)},
        compiler_params=_cp("parallel"),
    )(me, w, m, v, own, zone, *outs)


def small_exchange(sh, rep, name):
    _, Rs, C = sh.shape
    Rr = rep.shape[0]

    def body(sh_ref, rep_ref, sh_out, rep_out, send_sems, recv_sems, local_sems):
        x, y, c = _pos()
        me = 4 * x + 2 * y + c
        l1 = pltpu.make_async_copy(sh_ref.at[me], sh_out.at[me], local_sems.at[0])
        l2 = pltpu.make_async_copy(rep_ref, rep_out.at[me], local_sems.at[1])
        l1.start()
        l2.start()

        def flip(v, bit):
            return 1 - v if bit else v

        sends, recvs = [], []
        for r in range(1, N_DEV):
            peer = (flip(x, r & 4), flip(y, r & 2), flip(c, r & 1))
            pid = 4 * peer[0] + 2 * peer[1] + peer[2]
            k = 2 * (r - 1)
            mk = lambda src, dst, kk: pltpu.make_async_remote_copy(
                src_ref=src, dst_ref=dst, send_sem=send_sems.at[kk], recv_sem=recv_sems.at[kk],
                device_id=peer, device_id_type=MESH)
            sends += [mk(sh_ref.at[pid], sh_out.at[me], k), mk(rep_ref, rep_out.at[me], k + 1)]
            recvs += [mk(sh_ref.at[me], sh_out.at[pid], k), mk(rep_ref, rep_out.at[pid], k + 1)]
        for cp in sends:
            cp.start()
        for cp in recvs:
            cp.wait_recv()
        for cp in sends:
            cp.wait_send()
        l1.wait()
        l2.wait()

    n = 2 * (N_DEV - 1)
    return pl.pallas_call(
        body, name=name,
        out_shape=[jax.ShapeDtypeStruct((N_DEV, Rs, C), sh.dtype), jax.ShapeDtypeStruct((N_DEV, *rep.shape), rep.dtype)],
        in_specs=[ANY, ANY], out_specs=[ANY, ANY],
        scratch_shapes=[pltpu.SemaphoreType.DMA((n,)), pltpu.SemaphoreType.DMA((n,)), pltpu.SemaphoreType.DMA((2,))],
    )(sh, rep)


def adam_slots(w, m, v, slots, name):
    S, n, a, b = slots.shape
    ta = max(t for t in range(16, min(a, 512) + 1, 8)
             if a % t == 0 and t * S * b * slots.dtype.itemsize <= 4 * 1024 * 1024)

    def body(w_ref, m_ref, v_ref, s_ref, g_ref, d_ref, m2_ref, v2_ref):
        gv = s_ref[0].astype(F32)
        for k in range(1, S):
            gv = gv + s_ref[k].astype(F32)
        m2 = ADAM_B1 * m_ref[...] + (1.0 - ADAM_B1) * gv
        v2 = ADAM_B2 * v_ref[...] + (1.0 - ADAM_B2) * (gv * gv)
        m_hat = m2 / (1.0 - ADAM_B1 ** ADAM_STEP)
        v_hat = v2 / (1.0 - ADAM_B2 ** ADAM_STEP)
        g_ref[...] = gv
        d_ref[...] = -ADAM_LR * (m_hat / (jnp.sqrt(v_hat) + ADAM_EPS) + ADAM_WD * w_ref[...])
        m2_ref[...] = m2
        v2_ref[...] = v2

    spec = pl.BlockSpec((None, ta, b), lambda l, r: (l, r, 0))
    return pl.pallas_call(
        body, name=name, grid=(n, a // ta),
        in_specs=[spec] * 3 + [pl.BlockSpec((S, None, ta, b), lambda l, r: (0, l, r, 0))], out_specs=[spec] * 4,
        out_shape=[jax.ShapeDtypeStruct((n, a, b), F32)] * 4, compiler_params=_cp("parallel", "parallel"),
    )(w, m, v, slots)


WEIGHTS = ["norm_g", "ssm_in_w", "ssm_conv_w", "ssm_conv_b", "ssm_dt_bias", "ssm_A_log", "ssm_D", "ssm_norm_g",
           "ssm_out_w", "cf_pw1_w", "cf_pw1_b", "cf_dw_w", "cf_dw_b", "cf_ln_g", "cf_ln_b", "cf_pw2_w", "cf_pw2_b",
           "xa_mem_g", "xa_q_w", "xa_kv_w", "xa_o_w", "ffn_in_w", "ffn_conv_w", "ffn_conv_b", "ffn_out_w"]
ARGS = ["x", "mem"] + WEIGHTS + ["loss_target"] + ["m_" + n for n in WEIGHTS] + ["v_" + n for n in WEIGHTS]
BIG = {"ssm_in_w": "col", "ssm_out_w": "row", "cf_pw1_w": "col", "cf_pw2_w": "row", "xa_q_w": "row",
       "xa_kv_w": "col", "xa_o_w": "row", "ffn_in_w": "col", "ffn_out_w": "row"}
SMALL = ["norm_g", "ssm_conv_w", "cf_pw1_b", "cf_dw_w", "cf_dw_b", "cf_ln_g", "cf_ln_b", "cf_pw2_b", "ffn_conv_w"]
REP = ["ssm_conv_b", "ssm_dt_bias", "ssm_A_log", "ssm_D", "ssm_norm_g", "xa_mem_g", "ffn_conv_b"]
SMALL_W = 768
REP_W = 512


def _r8(n):
    return -(-n // 8) * 8


def _stack2d(arrs, wid):
    parts = []
    for a in arrs:
        r, c = a.shape[-2:]
        parts.append(jnp.pad(a, [(0, 0)] * (a.ndim - 2) + [(0, _r8(r) - r), (0, wid - c)]))
    return jnp.concatenate(parts, axis=-2)


def _unstack2d(buf, shapes2d):
    out, o = [], 0
    for r, c in shapes2d:
        out.append(buf[..., o:o + r, :c])
        o += _r8(r)
    return out


def _gathered_to_full(g):
    lead = g.shape[1:-1]
    return jnp.moveaxis(g, 0, -2).reshape(*lead, N_DEV * g.shape[-1])


def _full_to_slots(w):
    lead = w.shape[:-1]
    return jnp.moveaxis(w.reshape(*lead, N_DEV, w.shape[-1] // N_DEV), -2, 0)


def kernel(x, mem, norm_g, ssm_in_w, ssm_conv_w, ssm_conv_b, ssm_dt_bias, ssm_A_log, ssm_D, ssm_norm_g, ssm_out_w, cf_pw1_w, cf_pw1_b, cf_dw_w, cf_dw_b, cf_ln_g, cf_ln_b, cf_pw2_w, cf_pw2_b, xa_mem_g, xa_q_w, xa_kv_w, xa_o_w, ffn_in_w, ffn_conv_w, ffn_conv_b, ffn_out_w, loss_target, m_norm_g, m_ssm_in_w, m_ssm_conv_w, m_ssm_conv_b, m_ssm_dt_bias, m_ssm_A_log, m_ssm_D, m_ssm_norm_g, m_ssm_out_w, m_cf_pw1_w, m_cf_pw1_b, m_cf_dw_w, m_cf_dw_b, m_cf_ln_g, m_cf_ln_b, m_cf_pw2_w, m_cf_pw2_b, m_xa_mem_g, m_xa_q_w, m_xa_kv_w, m_xa_o_w, m_ffn_in_w, m_ffn_conv_w, m_ffn_conv_b, m_ffn_out_w, v_norm_g, v_ssm_in_w, v_ssm_conv_w, v_ssm_conv_b, v_ssm_dt_bias, v_ssm_A_log, v_ssm_D, v_ssm_norm_g, v_ssm_out_w, v_cf_pw1_w, v_cf_pw1_b, v_cf_dw_w, v_cf_dw_b, v_cf_ln_g, v_cf_ln_b, v_cf_pw2_w, v_cf_pw2_b, v_xa_mem_g, v_xa_q_w, v_xa_kv_w, v_xa_o_w, v_ffn_in_w, v_ffn_conv_w, v_ffn_conv_b, v_ffn_out_w):
    return _step(x, mem, norm_g, ssm_in_w, ssm_conv_w, ssm_conv_b, ssm_dt_bias, ssm_A_log, ssm_D, ssm_norm_g, ssm_out_w, cf_pw1_w, cf_pw1_b, cf_dw_w, cf_dw_b, cf_ln_g, cf_ln_b, cf_pw2_w, cf_pw2_b, xa_mem_g, xa_q_w, xa_kv_w, xa_o_w, ffn_in_w, ffn_conv_w, ffn_conv_b, ffn_out_w, loss_target, m_norm_g, m_ssm_in_w, m_ssm_conv_w, m_ssm_conv_b, m_ssm_dt_bias, m_ssm_A_log, m_ssm_D, m_ssm_norm_g, m_ssm_out_w, m_cf_pw1_w, m_cf_pw1_b, m_cf_dw_w, m_cf_dw_b, m_cf_ln_g, m_cf_ln_b, m_cf_pw2_w, m_cf_pw2_b, m_xa_mem_g, m_xa_q_w, m_xa_kv_w, m_xa_o_w, m_ffn_in_w, m_ffn_conv_w, m_ffn_conv_b, m_ffn_out_w, v_norm_g, v_ssm_in_w, v_ssm_conv_w, v_ssm_conv_b, v_ssm_dt_bias, v_ssm_A_log, v_ssm_D, v_ssm_norm_g, v_ssm_out_w, v_cf_pw1_w, v_cf_pw1_b, v_cf_dw_w, v_cf_dw_b, v_cf_ln_g, v_cf_ln_b, v_cf_pw2_w, v_cf_pw2_b, v_xa_mem_g, v_xa_q_w, v_xa_kv_w, v_xa_o_w, v_ffn_in_w, v_ffn_conv_w, v_ffn_conv_b, v_ffn_out_w)


def _step(*args):
    A = dict(zip(ARGS, args, strict=True))
    x, mem, target = A["x"][0], A["mem"][0], A["loss_target"][0]

    big = list(BIG)
    geo = [A[n].shape for n in big]
    kinds = ["row" if BIG[n] == "row" else ("col" if A[n].shape[-1] % LANE == 0 else "lead") for n in big]
    W = {n: A[n] for n in REP}
    small2d = [(A[n].size // A[n].shape[-1], A[n].shape[-1]) for n in SMALL]
    rep2d = [(A[n].size // REP_W, REP_W) if A[n].shape[-1] % REP_W == 0 else A[n].shape for n in REP] + [(1, 1)]
    stack_small = lambda pre: _stack2d([A[pre + n].reshape(rc) for n, rc in zip(SMALL, small2d)], SMALL_W)
    stack_rep = lambda pre: _stack2d([A[pre + n].reshape(rc) for n, rc in zip(REP, rep2d)] + [jnp.zeros((1, 1), F32)],
                                     REP_W)
    small_g = all_gather(stack_small(""), name="gather_small")
    for n, g in zip(SMALL, _unstack2d(small_g, small2d)):
        W[n] = _gathered_to_full(g.reshape(N_DEV, *A[n].shape))

    shards = [A[n].astype(BF16) for n in big]
    for n in big:
        W[n] = [None] * A[n].shape[0]
    first = _sublayer_weights(0, 0)
    got0 = gather_now([shards[big.index(n)][l:l + 1] for n, l in first], [kinds[big.index(n)] for n, l in first],
                      name="gather_first")
    for (n, l), g in zip(first, got0):
        W[n][l] = _gathered_to_full(g)[0] if kinds[big.index(n)] == "lead" else g[0]
    items, sub_items = [], {}
    for i in range(DEPTH):
        for s in range(3):
            sub_items[i, s] = []
            for n, l in _sublayer_weights(i, s) if (i, s) != (0, 0) else []:
                w = big.index(n)
                sub_items[i, s].append(len(items))
                items.append((w, l, kinds[w], *geo[w][1:]))
    sems, shards_thru, zones, token = gather_start(shards, items, [small_g, got0[0]], name="gather_start")
    x = x + token[0, 0]

    def fetch(i, s, x_in):
        ids = sub_items[i, s]
        if not ids:
            return
        got = gather_wait([zones[t] for t in ids], ids, items, sems, x_in, shards_thru if (i, s) == (DEPTH - 1, 2) else [],
                          name=f"gather_wait{i}{s}")
        for t, z in zip(ids, got):
            w, l, kind = items[t][:3]
            W[big[w]][l] = _gathered_to_full(z) if kind == "lead" else z

    sent = []
    final = {}

    def layer_done(i, s, GB, g_in):
        srcs, its = [], []
        for n, l in _sublayer_weights(i, s):
            w = big.index(n)
            if kinds[w] == "lead":
                srcs.append(_full_to_slots(g_in if n == "ssm_in_w" else GB[n][l]))
                its.append((len(srcs) - 1, None, "lead", *geo[w][1:], n, l))
            else:
                srcs.append(GB[n])
                its.append((len(srcs) - 1, l, kinds[w], *geo[w][1:], n, l))
        sems_i, thru, zones_i, token_i = rs_start(srcs, [it[:5] for it in its], name=f"rs_start{i}{s}")
        for it, s in zip(its, thru):
            if it[2] != "lead":
                GB[it[5]] = s
        sent.append((its, sems_i, [s for it, s in zip(its, thru) if it[2] == "lead"], zones_i))
        final["GB"] = GB
        return token_i

    loss, grad_x, G = local_step(x, mem, target, W, fetch, layer_done)

    sh = _stack2d([_full_to_slots(G[n]).reshape(N_DEV, *rc) for n, rc in zip(SMALL, small2d)], SMALL_W)
    rep = _stack2d([G[n].reshape(rc) for n, rc in zip(REP, rep2d)] + [loss[:, :1]], REP_W)
    sh_got, rep_got = small_exchange(sh, rep, name="small_exchange")

    res = {}
    GBf = final["GB"]
    bufs = {n: [lax.empty(A[n].shape, F32) for _ in range(4)] for n in big}
    for i, (its, sems_i, lead_srcs, zones_i) in enumerate(sent):
        keep = lead_srcs + [GBf[it[5]] for it in its if it[2] != "lead"]
        zones_i = rs_wait(zones_i, [it[:5] for it in its], sems_i, sh_got, keep, name=f"rs_wait{i}")
        lead_it = iter(lead_srcs)
        for it, z in zip(its, zones_i):
            n, l = it[5], it[6]
            own = next(lead_it) if it[2] == "lead" else GBf[n]
            bufs[n] = adam_rs(A[n], A["m_" + n], A["v_" + n], l, own, it[2], z, bufs[n], name=f"adam_{n}{l}")
    for n in big:
        res[n] = tuple(bufs[n])
    for names, shapes2d, stack, slots, tag in ((SMALL, small2d, stack_small, sh_got, "small"),
                                               (REP, rep2d, stack_rep, rep_got, "rep")):
        outs4 = adam_slots(stack("")[None], stack("m_")[None], stack("v_")[None], slots[:, None], name=f"adam_{tag}")
        parts = [_unstack2d(o[0], shapes2d) for o in outs4]
        for k, n in enumerate(names):
            res[n] = tuple(q[k].reshape(A[n].shape) for q in parts)
        if tag == "rep":
            total_loss = parts[0][-1][0, 0]

    outs = [total_loss, grad_x[None]]
    for k in range(4):
        outs += [res[n][k] for n in WEIGHTS]
    return tuple(outs)
```

```python
import jax
import jax.numpy as jnp
from jax import lax
from jax.experimental import pallas as pl
from jax.experimental.pallas import tpu as pltpu

F32 = jnp.float32
BF16 = jnp.bfloat16

D_MODEL = 1024
D_INNER = 2048
N_HEADS = 32
HEAD_DIM = 64
N_GROUPS = 4
D_STATE = 128
CHUNK = 128
CONV_DIM = 3072
SSM_K = 4
CF_K = 31
N_MEM = 256
XA_HEADS = 4
XA_HD = 256
D_FF = 2816
FFN_K = 3
EPS = 1e-6
DEPTH = 4
N_DEV = 8

ADAM_LR = 0.001
ADAM_B1 = 0.9
ADAM_B2 = 0.999
ADAM_EPS = 1e-08
ADAM_WD = 0.01
ADAM_STEP = 10

LANE = 128
VMEM_LIMIT = 56 * 1024 * 1024
NEG = -1e30
MESH = pl.DeviceIdType.MESH


def _cp(*sem):
    return pltpu.CompilerParams(dimension_semantics=sem if sem else None, vmem_limit_bytes=VMEM_LIMIT)


def _tile(n, cap):
    if n <= cap:
        return n
    best = 0
    for t in range(LANE, cap + 1, LANE):
        if n % t == 0:
            best = t
    assert best, (n, cap)
    return best


def _sig(x):
    return 1.0 / (1.0 + jnp.exp(-x))


def _split3(v):
    v0 = v.astype(BF16)
    r1 = v - v0.astype(F32)
    v1 = r1.astype(BF16)
    v2 = (r1 - v1.astype(F32)).astype(BF16)
    return v0, v1, v2


def _dot(a, b, ca=1, cb=0):
    return lax.dot_general(a, b, (((ca,), (cb,)), ((), ())), preferred_element_type=F32)


def _dot2(v, m, ca=1, cb=0):
    v0 = v.astype(BF16)
    v1 = (v - v0.astype(F32)).astype(BF16)
    return _dot(v0, m, ca, cb) + _dot(v1, m, ca, cb)


def _dot3(v, m, ca=1, cb=0):
    v0, v1, v2 = _split3(v)
    return _dot(v0, m, ca, cb) + _dot(v1, m, ca, cb) + _dot(v2, m, ca, cb)


def mm(a, b, *, ta=False, tb=False, bias=None, acc=None, out_dtype=F32, a_idx=None, layer=None, b_k0=0, b_n=None,
       into=None, dep=None, name):
    if isinstance(b, (list, tuple)):
        b, layer = b[layer], None
    if ta:
        K, M = a.shape[-2:]
    else:
        M, K = a.shape[-2:]
    N = b_n if b_n is not None else (b.shape[-2] if tb else b.shape[-1])
    assert (b.ndim == 3) == (layer is not None) and (a.ndim == 3) == (a_idx is not None)
    tm = _tile(M, 1536)
    tn = _tile(N, 1536)
    tk = _tile(K, 2048)
    nk = K // tk
    assert b_k0 % tk == 0 and b_k0 + K <= (b.shape[-1] if tb else b.shape[-2])
    kb = b_k0 // tk
    has_bias, has_acc = bias is not None, acc is not None
    if into is not None:
        out_dtype = into[0].dtype
        assert into[0].shape[1] == M and into[2] % tn == 0 and into[2] + N <= into[0].shape[2] and not has_acc

    def body(*refs):
        a_ref, b_ref = refs[0], refs[1]
        pos = 2
        bias_ref = acc_ref = None
        if has_bias:
            bias_ref = refs[pos]
            pos += 1
        if has_acc:
            acc_ref = refs[pos]
            pos += 1
        if into is not None:
            pos += 1
        if dep is not None:
            pos += 1
        o_ref = refs[pos]
        s_ref = refs[pos + 1] if nk > 1 else None
        p = _dot(a_ref[...].astype(BF16), b_ref[...].astype(BF16), 0 if ta else 1, 1 if tb else 0)

        def extras(v):
            if has_bias:
                v = v + bias_ref[...]
            if has_acc:
                v = v + acc_ref[...]
            return v

        if nk == 1:
            o_ref[...] = extras(p).astype(out_dtype)
        else:
            k = pl.program_id(2)

            @pl.when(k == 0)
            def _():
                s_ref[...] = extras(p)

            @pl.when(k > 0)
            def _():
                s_ref[...] += p

            @pl.when(k == nk - 1)
            def _():
                o_ref[...] = s_ref[...].astype(out_dtype)

    lead_a = () if a_idx is None else (a_idx,)
    lead_b = () if layer is None else (layer,)
    sq = lambda lead: (None,) * len(lead)
    if ta:
        a_spec = pl.BlockSpec((*sq(lead_a), tk, tm), lambda i, j, k: (*lead_a, k, i))
    else:
        a_spec = pl.BlockSpec((*sq(lead_a), tm, tk), lambda i, j, k: (*lead_a, i, k))
    if tb:
        b_spec = pl.BlockSpec((*sq(lead_b), tn, tk), lambda i, j, k: (*lead_b, j, k + kb))
    else:
        b_spec = pl.BlockSpec((*sq(lead_b), tk, tn), lambda i, j, k: (*lead_b, k + kb, j))
    in_specs, args = [a_spec, b_spec], [a, b]
    if has_bias:
        in_specs.append(pl.BlockSpec((1, tn), lambda i, j, k: (0, j)))
        args.append(bias.reshape(1, N).astype(F32))
    if has_acc:
        in_specs.append(pl.BlockSpec((tm, tn), lambda i, j, k: (i, j)))
        args.append(acc)
    if into is None:
        out_spec = pl.BlockSpec((tm, tn), lambda i, j, k: (i, j))
        out_shape = jax.ShapeDtypeStruct((M, N), out_dtype)
        aliases = {}
    else:
        buf, l, col0 = into
        cb = col0 // tn
        in_specs.append(pl.BlockSpec(memory_space=pl.ANY))
        args.append(buf)
        out_spec = pl.BlockSpec((None, tm, tn), lambda i, j, k: (l, i, j + cb))
        out_shape = jax.ShapeDtypeStruct(buf.shape, buf.dtype)
        aliases = {len(args) - 1: 0}
    if dep is not None:
        in_specs.append(pl.BlockSpec(memory_space=pl.ANY))
        args.append(dep)
    return pl.pallas_call(
        body, name=name, grid=(M // tm, N // tn, nk),
        in_specs=in_specs, out_specs=out_spec, out_shape=out_shape, input_output_aliases=aliases,
        scratch_shapes=[pltpu.VMEM((tm, tn), F32)] if nk > 1 else [],
        compiler_params=_cp("parallel", "parallel", "arbitrary"),
    )(*args)


def colsum(x, name):
    L, C = x.shape
    tr = _tile(L, 512)
    tc = _tile(C, 1024)

    def body(x_ref, o_ref):
        @pl.when(pl.program_id(1) == 0)
        def _():
            o_ref[...] = jnp.zeros_like(o_ref)

        o_ref[...] += jnp.sum(x_ref[...].astype(F32), axis=0, keepdims=True)

    return pl.pallas_call(
        body, name=name, grid=(C // tc, L // tr),
        in_specs=[pl.BlockSpec((tr, tc), lambda j, i: (i, j))],
        out_specs=pl.BlockSpec((1, tc), lambda j, i: (0, j)),
        out_shape=jax.ShapeDtypeStruct((1, C), F32),
        compiler_params=_cp("parallel", "arbitrary"),
    )(x)


TR = 512


def _row_spec(tr, w):
    return pl.BlockSpec((tr, w), lambda i: (i, 0))


def _vec_spec(w):
    return pl.BlockSpec((1, w), lambda i: (0, 0))


def _rms(v):
    return lax.rsqrt(jnp.mean(v * v, axis=-1, keepdims=True) + EPS)


def norm_fwd(x, g, name):
    L, D = x.shape
    tr = min(TR, L)

    def body(x_ref, g_ref, h_ref):
        xv = x_ref[...]
        h_ref[...] = (xv * _rms(xv) * g_ref[...]).astype(BF16)

    return pl.pallas_call(
        body, name=name, grid=(L // tr,),
        in_specs=[_row_spec(tr, D), _vec_spec(D)], out_specs=_row_spec(tr, D),
        out_shape=jax.ShapeDtypeStruct((L, D), BF16), compiler_params=_cp("parallel"),
    )(x, g.reshape(1, D))


def bnd_fwd(x, f, gpost, gpre, name):
    L, D = x.shape
    tr = min(TR, L)

    def body(x_ref, f_ref, gp_ref, gn_ref, xo_ref, h_ref):
        fv = f_ref[...].astype(F32)
        xn = x_ref[...] + fv * _rms(fv) * gp_ref[...]
        xo_ref[...] = xn
        h_ref[...] = (xn * _rms(xn) * gn_ref[...]).astype(BF16)

    return pl.pallas_call(
        body, name=name, grid=(L // tr,),
        in_specs=[_row_spec(tr, D), _row_spec(tr, D), _vec_spec(D), _vec_spec(D)],
        out_specs=[_row_spec(tr, D), _row_spec(tr, D)],
        out_shape=[jax.ShapeDtypeStruct((L, D), F32), jax.ShapeDtypeStruct((L, D), BF16)],
        compiler_params=_cp("parallel"),
    )(x, f, gpost.reshape(1, D), gpre.reshape(1, D))


def final_fwd(x, f, gpost, target, name):
    L, D = x.shape
    tr = min(TR, L)
    n = L // tr

    def body(x_ref, f_ref, gp_ref, t_ref, dy_ref, loss_ref, acc_ref):
        i = pl.program_id(0)

        @pl.when(i == 0)
        def _():
            acc_ref[...] = jnp.zeros_like(acc_ref)

        fv = f_ref[...].astype(F32)
        e = x_ref[...] + fv * _rms(fv) * gp_ref[...] - t_ref[...]
        dy_ref[...] = e * (1.0 / D)
        acc_ref[...] += jnp.sum(e * e, axis=0, keepdims=True)

        @pl.when(i == n - 1)
        def _():
            loss_ref[...] = jnp.full((1, LANE), 0.5 / D, F32) * jnp.sum(acc_ref[...])

    return pl.pallas_call(
        body, name=name, grid=(n,),
        in_specs=[_row_spec(tr, D), _row_spec(tr, D), _vec_spec(D), _row_spec(tr, D)],
        out_specs=[_row_spec(tr, D), _vec_spec(LANE)],
        out_shape=[jax.ShapeDtypeStruct((L, D), F32), jax.ShapeDtypeStruct((1, LANE), F32)],
        scratch_shapes=[pltpu.VMEM((1, D), F32)],
        compiler_params=_cp("arbitrary"),
    )(x, f, gpost.reshape(1, D), target)


def _rms_bwd(v, g, dy):
    r = _rms(v)
    vn = v * r
    dg = jnp.sum(dy * vn, axis=0, keepdims=True)
    dvn = dy * g
    dv = r * (dvn - vn * jnp.mean(dvn * vn, axis=-1, keepdims=True))
    return dv, dg


def bnd_bwd(dxp, *, pre=None, post=None, dep=None, name):
    L, D = dxp.shape
    tr = min(TR, L)
    has_pre, has_post = pre is not None, post is not None

    def body(*refs):
        pos = 0
        dxp_ref = refs[pos]; pos += 1
        if has_pre:
            x_ref, gpre_ref, dh_ref = refs[pos:pos + 3]; pos += 3
        if has_post:
            f_ref, gpost_ref = refs[pos:pos + 2]; pos += 2
        if dep is not None:
            pos += 1
        if has_pre:
            dx_ref, dgpre_ref = refs[pos:pos + 2]; pos += 2
        if has_post:
            df_ref, dgpost_ref = refs[pos:pos + 2]; pos += 2
        i = pl.program_id(0)
        dx = dxp_ref[...]
        if has_pre:
            d, dg = _rms_bwd(x_ref[...], gpre_ref[...], dh_ref[...].astype(F32))
            dx = dx + d
            dx_ref[...] = dx

            @pl.when(i == 0)
            def _():
                dgpre_ref[...] = jnp.zeros_like(dgpre_ref)

            dgpre_ref[...] += dg
        if has_post:
            d, dg = _rms_bwd(f_ref[...].astype(F32), gpost_ref[...], dx)
            df_ref[...] = d.astype(BF16)

            @pl.when(i == 0)
            def _():
                dgpost_ref[...] = jnp.zeros_like(dgpost_ref)

            dgpost_ref[...] += dg

    in_specs, args = [_row_spec(tr, D)], [dxp]
    out_specs, out_shape, names = [], [], []
    if has_pre:
        x, gpre, dh = pre
        in_specs += [_row_spec(tr, D), _vec_spec(D), _row_spec(tr, D)]
        args += [x, gpre.reshape(1, D), dh]
        out_specs += [_row_spec(tr, D), _vec_spec(D)]
        out_shape += [jax.ShapeDtypeStruct((L, D), F32), jax.ShapeDtypeStruct((1, D), F32)]
        names += ["dx", "dgpre"]
    if has_post:
        f, gpost = post
        in_specs += [_row_spec(tr, D), _vec_spec(D)]
        args += [f, gpost.reshape(1, D)]
        out_specs += [_row_spec(tr, D), _vec_spec(D)]
        out_shape += [jax.ShapeDtypeStruct((L, D), BF16), jax.ShapeDtypeStruct((1, D), F32)]
        names += ["df", "dgpost"]
    if dep is not None:
        in_specs.append(pl.BlockSpec(memory_space=pl.ANY))
        args.append(dep)
    outs = pl.pallas_call(
        body, name=name, grid=(L // tr,), in_specs=in_specs, out_specs=out_specs, out_shape=out_shape,
        compiler_params=_cp("arbitrary"),
    )(*args)
    return dict(zip(names, outs))


def norm_dg(x, dy, name):
    L, D = x.shape
    tr = min(TR, L)

    def body(x_ref, dy_ref, o_ref):
        @pl.when(pl.program_id(0) == 0)
        def _():
            o_ref[...] = jnp.zeros_like(o_ref)

        xv = x_ref[...]
        o_ref[...] += jnp.sum(dy_ref[...] * xv * _rms(xv), axis=0, keepdims=True)

    return pl.pallas_call(
        body, name=name, grid=(L // tr,),
        in_specs=[_row_spec(tr, D), _row_spec(tr, D)], out_specs=_vec_spec(D),
        out_shape=jax.ShapeDtypeStruct((1, D), F32), compiler_params=_cp("arbitrary"),
    )(x, dy)


HALO = 32


def _prev_halo_spec(tr, tc, col):
    per = tr // HALO
    return pl.BlockSpec((HALO, tc), lambda *g: (jnp.maximum(g[-1] * per - 1, 0), col(*g)))


def _fill_prev(scr, halo_val, blk_val, i, tr):
    scr[pl.ds(0, HALO), :] = jnp.where(i == 0, 0.0, halo_val)
    scr[pl.ds(HALO, tr), :] = blk_val


def _conv(scr, w_ref, K, tr):
    acc = None
    for k in range(K):
        term = scr[pl.ds(HALO - (K - 1) + k, tr), :] * w_ref[k:k + 1, :]
        acc = term if acc is None else acc + term
    return acc


def _shift_copies(scr, sh, rows):
    n = rows - 8
    for r in range(1, 8):
        sh[r - 1, pl.ds(0, n), :] = scr[pl.ds(r, n), :]


def _tap(scr, sh, off, tr):
    q, r = divmod(off, 8)
    return scr[pl.ds(off, tr), :] if r == 0 else sh[r - 1, pl.ds(8 * q, tr), :]


def ssm_conv_fwd(zx, w, b, *, col0, ncols, wcol0, out_dtype, name):
    L = zx.shape[0]
    tr = min(TR, L)
    tc = 1024
    cb, wb = col0 // tc, wcol0 // tc

    def body(x_ref, h_ref, w_ref, b_ref, o_ref, p_ref, scr):
        i = pl.program_id(1)
        _fill_prev(scr, h_ref[...].astype(F32), x_ref[...].astype(F32), i, tr)
        pre = _conv(scr, w_ref, SSM_K, tr) + b_ref[...]
        p_ref[...] = pre.astype(BF16)
        o_ref[...] = (pre * _sig(pre)).astype(out_dtype)

    out = pl.BlockSpec((tr, tc), lambda j, i: (i, j))
    return pl.pallas_call(
        body, name=name, grid=(ncols // tc, L // tr),
        in_specs=[pl.BlockSpec((tr, tc), lambda j, i: (i, j + cb)),
                  _prev_halo_spec(tr, tc, lambda j, i: j + cb),
                  pl.BlockSpec((SSM_K, tc), lambda j, i: (0, j + wb)),
                  pl.BlockSpec((1, tc), lambda j, i: (0, j + wb))],
        out_specs=[out, out],
        out_shape=[jax.ShapeDtypeStruct((L, ncols), out_dtype), jax.ShapeDtypeStruct((L, ncols), BF16)],
        scratch_shapes=[pltpu.VMEM((HALO + tr, tc), F32)],
        compiler_params=_cp("parallel", "parallel"),
    )(zx, zx, w, b)


def ssm_conv_bwd(zx, pre, d, w, *, col0, dcol0, ncols, name):
    L = zx.shape[0]
    tr = min(TR, L)
    tc = 1024
    cb, db_ = col0 // tc, dcol0 // tc
    n = L // tr
    per = tr // HALO
    last = L // HALO - 1

    def body(x_ref, p_ref, np_ref, d_ref, nd_ref, w_ref, dx_ref, dw_ref, db_ref, sd):
        i = pl.program_id(1)

        def dpre(p, dv):
            s = _sig(p)
            return dv * s * (1.0 + p * (1.0 - s))

        dp = dpre(p_ref[...].astype(F32), d_ref[...])
        sd[pl.ds(0, tr), :] = dp
        sd[pl.ds(tr, HALO), :] = jnp.where(i == n - 1, 0.0, dpre(np_ref[...].astype(F32), nd_ref[...]))

        @pl.when(i == 0)
        def _():
            dw_ref[...] = jnp.zeros_like(dw_ref)
            db_ref[...] = jnp.zeros_like(db_ref)

        xv = x_ref[...].astype(F32)
        acc = None
        for k in range(SSM_K):
            tk = sd[pl.ds(SSM_K - 1 - k, tr), :]
            term = tk * w_ref[k:k + 1, :]
            acc = term if acc is None else acc + term
            dw_ref[k:k + 1, :] += jnp.sum(xv * tk, axis=0, keepdims=True)
        dx_ref[...] = acc.astype(BF16)
        db_ref[...] += jnp.sum(dp, axis=0, keepdims=True)

    nxt = lambda i: jnp.minimum((i + 1) * per, last)
    return pl.pallas_call(
        body, name=name, grid=(ncols // tc, n),
        in_specs=[pl.BlockSpec((tr, tc), lambda j, i: (i, j + cb)),
                  pl.BlockSpec((tr, tc), lambda j, i: (i, j)),
                  pl.BlockSpec((HALO, tc), lambda j, i: (nxt(i), j)),
                  pl.BlockSpec((tr, tc), lambda j, i: (i, j + db_)),
                  pl.BlockSpec((HALO, tc), lambda j, i: (nxt(i), j + db_)),
                  pl.BlockSpec((SSM_K, tc), lambda j, i: (0, j + db_))],
        out_specs=[pl.BlockSpec((tr, tc), lambda j, i: (i, j)),
                   pl.BlockSpec((SSM_K, tc), lambda j, i: (0, j)),
                   pl.BlockSpec((1, tc), lambda j, i: (0, j))],
        out_shape=[jax.ShapeDtypeStruct((L, ncols), BF16), jax.ShapeDtypeStruct((SSM_K, ncols), F32),
                   jax.ShapeDtypeStruct((1, ncols), F32)],
        scratch_shapes=[pltpu.VMEM((tr + HALO, tc), F32)],
        compiler_params=_cp("parallel", "arbitrary"),
    )(zx, pre, pre, d, d, w)


FFN_TC = 1408
FFN_TR = 256


def ffn_act_fwd(u, w, b, name):
    L = u.shape[0]
    tr = min(FFN_TR, L)
    tc = FFN_TC
    nb = D_FF // tc

    def body(g_ref, hg_ref, v_ref, hv_ref, wg_ref, wv_ref, bg_ref, bv_ref, o_ref, c_ref, sg, sv):
        i = pl.program_id(1)
        _fill_prev(sg, hg_ref[...].astype(F32), g_ref[...].astype(F32), i, tr)
        _fill_prev(sv, hv_ref[...].astype(F32), v_ref[...].astype(F32), i, tr)
        ug = _conv(sg, wg_ref, FFN_K, tr) + bg_ref[...]
        uv = _conv(sv, wv_ref, FFN_K, tr) + bv_ref[...]
        c_ref[0] = ug.astype(BF16)
        c_ref[1] = uv.astype(BF16)
        o_ref[...] = (ug * _sig(ug) * uv).astype(BF16)

    blk = lambda off: pl.BlockSpec((tr, tc), lambda j, i: (i, j + off))
    wsp = lambda off: pl.BlockSpec((FFN_K, tc), lambda j, i: (0, j + off))
    bsp = lambda off: pl.BlockSpec((1, tc), lambda j, i: (0, j + off))
    return pl.pallas_call(
        body, name=name, grid=(nb, L // tr),
        in_specs=[blk(0), _prev_halo_spec(tr, tc, lambda j, i: j),
                  blk(nb), _prev_halo_spec(tr, tc, lambda j, i: j + nb),
                  wsp(0), wsp(nb), bsp(0), bsp(nb)],
        out_specs=[pl.BlockSpec((tr, tc), lambda j, i: (i, j)), pl.BlockSpec((2, tr, tc), lambda j, i: (0, i, j))],
        out_shape=[jax.ShapeDtypeStruct((L, D_FF), BF16), jax.ShapeDtypeStruct((2, L, D_FF), BF16)],
        scratch_shapes=[pltpu.VMEM((HALO + tr, tc), F32), pltpu.VMEM((HALO + tr, tc), F32)],
        compiler_params=_cp("parallel", "parallel"),
    )(u, u, u, u, w, w, b, b)


def ffn_act_bwd(u, c, dact, w, name):
    L = u.shape[0]
    tr = min(FFN_TR, L)
    tc = FFN_TC
    nb = D_FF // tc
    n = L // tr
    per = tr // HALO
    last = L // HALO - 1

    def body(g_ref, v_ref, c_ref, nc_ref, da_ref, nda_ref, wg_ref, wv_ref, du_ref, dw_ref, db_ref, dg_s, dv_s):
        i = pl.program_id(1)

        def grads(cg, cv, da):
            s = _sig(cg)
            return da * cv * s * (1.0 + cg * (1.0 - s)), da * cg * s

        dg, dv = grads(c_ref[0].astype(F32), c_ref[1].astype(F32), da_ref[...].astype(F32))
        ndg, ndv = grads(nc_ref[0].astype(F32), nc_ref[1].astype(F32), nda_ref[...].astype(F32))
        at_end = i == n - 1
        for half, (scr, d, nd, x_ref, w_ref) in enumerate(((dg_s, dg, ndg, g_ref, wg_ref), (dv_s, dv, ndv, v_ref, wv_ref))):
            scr[pl.ds(0, tr), :] = d
            scr[pl.ds(tr, HALO), :] = jnp.where(at_end, 0.0, nd)

            @pl.when(i == 0)
            def _():
                dw_ref[half] = jnp.zeros((FFN_K, tc), F32)
                db_ref[half] = jnp.zeros((1, tc), F32)

            xv = x_ref[...].astype(F32)
            acc = None
            for k in range(FFN_K):
                tk = scr[pl.ds(FFN_K - 1 - k, tr), :]
                term = tk * w_ref[k:k + 1, :]
                acc = term if acc is None else acc + term
                dw_ref[half, k:k + 1, :] += jnp.sum(xv * tk, axis=0, keepdims=True)
            du_ref[half] = acc.astype(BF16)
            db_ref[half] += jnp.sum(d, axis=0, keepdims=True)

    blk = lambda off: pl.BlockSpec((tr, tc), lambda j, i: (i, j + off))
    wsp = lambda off: pl.BlockSpec((FFN_K, tc), lambda j, i: (0, j + off))
    nxt = lambda i: jnp.minimum((i + 1) * per, last)
    return pl.pallas_call(
        body, name=name, grid=(nb, n),
        in_specs=[blk(0), blk(nb),
                  pl.BlockSpec((2, tr, tc), lambda j, i: (0, i, j)),
                  pl.BlockSpec((2, HALO, tc), lambda j, i: (0, nxt(i), j)),
                  pl.BlockSpec((tr, tc), lambda j, i: (i, j)),
                  pl.BlockSpec((HALO, tc), lambda j, i: (nxt(i), j)),
                  wsp(0), wsp(nb)],
        out_specs=[pl.BlockSpec((2, tr, tc), lambda j, i: (0, i, j)),
                   pl.BlockSpec((2, FFN_K, tc), lambda j, i: (0, 0, j)),
                   pl.BlockSpec((2, 1, tc), lambda j, i: (0, 0, j))],
        out_shape=[jax.ShapeDtypeStruct((2, L, D_FF), BF16), jax.ShapeDtypeStruct((2, FFN_K, D_FF), F32),
                   jax.ShapeDtypeStruct((2, 1, D_FF), F32)],
        scratch_shapes=[pltpu.VMEM((tr + HALO, tc), F32), pltpu.VMEM((tr + HALO, tc), F32)],
        compiler_params=_cp("parallel", "arbitrary"),
    )(u, u, c, c, dact, dact, w, w)


def _ln_stats(c):
    mu = jnp.mean(c, axis=-1, keepdims=True)
    cc = c - mu
    rstd = lax.rsqrt(jnp.mean(cc * cc, axis=-1, keepdims=True) + EPS)
    return cc * rstd, rstd


def cf_fwd(u, dw_w, dw_b, ln_g, ln_b, name):
    L = u.shape[0]
    D = D_MODEL
    tr = min(TR, L)

    def body(a_ref, ha_ref, g_ref, hg_ref, w_ref, b_ref, lg_ref, lb_ref, c_ref, s_ref, scr, sh):
        i = pl.program_id(0)
        glu_h = ha_ref[...].astype(F32) * _sig(hg_ref[...].astype(F32))
        glu = a_ref[...].astype(F32) * _sig(g_ref[...].astype(F32))
        _fill_prev(scr, glu_h, glu, i, tr)
        _shift_copies(scr, sh, HALO + tr)
        c = b_ref[...]
        for k in range(CF_K):
            c = c + _tap(scr, sh, HALO - (CF_K - 1) + k, tr) * w_ref[k:k + 1, :]
        c_ref[...] = c
        xhat, _ = _ln_stats(c)
        ln = xhat * lg_ref[...] + lb_ref[...]
        s_ref[...] = (ln * _sig(ln)).astype(BF16)

    per = tr // HALO
    halo = lambda col: pl.BlockSpec((HALO, D), lambda i: (jnp.maximum(i * per - 1, 0), col))
    return pl.pallas_call(
        body, name=name, grid=(L // tr,),
        in_specs=[pl.BlockSpec((tr, D), lambda i: (i, 0)), halo(0),
                  pl.BlockSpec((tr, D), lambda i: (i, 1)), halo(1),
                  pl.BlockSpec((CF_K, D), lambda i: (0, 0)), _vec_spec(D), _vec_spec(D), _vec_spec(D)],
        out_specs=[_row_spec(tr, D), _row_spec(tr, D)],
        out_shape=[jax.ShapeDtypeStruct((L, D), F32), jax.ShapeDtypeStruct((L, D), BF16)],
        scratch_shapes=[pltpu.VMEM((HALO + tr, D), F32), pltpu.VMEM((7, HALO + tr, D), F32)],
        compiler_params=_cp("parallel"),
    )(u, u, u, u, dw_w, dw_b, ln_g, ln_b)


def cf_bwd_ln(c, ds, ln_g, ln_b, name):
    L, D = c.shape
    tr = min(TR, L)

    def body(c_ref, ds_ref, lg_ref, lb_ref, dc_ref, dg_ref, db_ref):
        xhat, rstd = _ln_stats(c_ref[...])
        ln = xhat * lg_ref[...] + lb_ref[...]
        sg = _sig(ln)
        dln = ds_ref[...].astype(F32) * sg * (1.0 + ln * (1.0 - sg))

        @pl.when(pl.program_id(0) == 0)
        def _():
            dg_ref[...] = jnp.zeros_like(dg_ref)
            db_ref[...] = jnp.zeros_like(db_ref)

        dg_ref[...] += jnp.sum(dln * xhat, axis=0, keepdims=True)
        db_ref[...] += jnp.sum(dln, axis=0, keepdims=True)
        dxh = dln * lg_ref[...]
        dc_ref[...] = rstd * (dxh - jnp.mean(dxh, axis=-1, keepdims=True)
                              - xhat * jnp.mean(dxh * xhat, axis=-1, keepdims=True))

    return pl.pallas_call(
        body, name=name, grid=(L // tr,),
        in_specs=[_row_spec(tr, D), _row_spec(tr, D), _vec_spec(D), _vec_spec(D)],
        out_specs=[_row_spec(tr, D), _vec_spec(D), _vec_spec(D)],
        out_shape=[jax.ShapeDtypeStruct((L, D), F32), jax.ShapeDtypeStruct((1, D), F32),
                   jax.ShapeDtypeStruct((1, D), F32)],
        compiler_params=_cp("arbitrary"),
    )(c, ds, ln_g, ln_b)


def cf_bwd_conv(u, dc, dw_w, name):
    L = u.shape[0]
    D = D_MODEL
    tr = min(TR, L)
    n = L // tr

    def body(a_ref, g_ref, dc_ref, nx_ref, w_ref, du_ref, dw_ref, db_ref, sd, shd):
        i = pl.program_id(0)
        a = a_ref[...].astype(F32)
        sg = _sig(g_ref[...].astype(F32))
        glu = a * sg
        dcv = dc_ref[...]
        sd[pl.ds(0, tr), :] = dcv
        sd[pl.ds(tr, HALO), :] = jnp.where(i == n - 1, 0.0, nx_ref[...])
        _shift_copies(sd, shd, tr + HALO)

        @pl.when(i == 0)
        def _():
            dw_ref[...] = jnp.zeros_like(dw_ref)
            db_ref[...] = jnp.zeros_like(db_ref)

        dglu = None
        for k in range(CF_K):
            tk = _tap(sd, shd, CF_K - 1 - k, tr)
            term = tk * w_ref[k:k + 1, :]
            dglu = term if dglu is None else dglu + term
            dw_ref[k:k + 1, :] += jnp.sum(glu * tk, axis=0, keepdims=True)
        du_ref[:, 0:D] = (dglu * sg).astype(BF16)
        du_ref[:, D:2 * D] = (dglu * a * sg * (1.0 - sg)).astype(BF16)
        db_ref[...] += jnp.sum(dcv, axis=0, keepdims=True)

    per = tr // HALO
    last = L // HALO - 1
    return pl.pallas_call(
        body, name=name, grid=(n,),
        in_specs=[pl.BlockSpec((tr, D), lambda i: (i, 0)),
                  pl.BlockSpec((tr, D), lambda i: (i, 1)),
                  _row_spec(tr, D),
                  pl.BlockSpec((HALO, D), lambda i: (jnp.minimum((i + 1) * per, last), 0)),
                  pl.BlockSpec((CF_K, D), lambda i: (0, 0))],
        out_specs=[pl.BlockSpec((tr, 2 * D), lambda i: (i, 0)),
                   pl.BlockSpec((CF_K, D), lambda i: (0, 0)), _vec_spec(D)],
        out_shape=[jax.ShapeDtypeStruct((L, 2 * D), BF16), jax.ShapeDtypeStruct((CF_K, D), F32),
                   jax.ShapeDtypeStruct((1, D), F32)],
        scratch_shapes=[pltpu.VMEM((tr + HALO, D), F32), pltpu.VMEM((7, tr + HALO, D), F32)],
        compiler_params=_cp("arbitrary"),
    )(u, u, dc, dc, dw_w)


XA_TR = 512
XA_SCALE = XA_HD ** -0.5


def _xa_probs(qh, kh):
    s = _dot(qh, kh, 1, 1) * XA_SCALE
    p = jnp.exp(s - jnp.max(s, axis=-1, keepdims=True))
    return p / jnp.sum(p, axis=-1, keepdims=True)


def attn_fwd(q, kv, name):
    L, D = q.shape
    tr = min(XA_TR, L)

    def body(q_ref, kv_ref, o_ref):
        for hd in range(XA_HEADS):
            c = slice(hd * XA_HD, (hd + 1) * XA_HD)
            p = _xa_probs(q_ref[:, c], kv_ref[:, c])
            vh = kv_ref[:, D + hd * XA_HD:D + (hd + 1) * XA_HD]
            o_ref[:, c] = _dot(p.astype(BF16), vh).astype(BF16)

    return pl.pallas_call(
        body, name=name, grid=(L // tr,),
        in_specs=[_row_spec(tr, D), pl.BlockSpec((N_MEM, 2 * D), lambda i: (0, 0))],
        out_specs=_row_spec(tr, D), out_shape=jax.ShapeDtypeStruct((L, D), BF16),
        compiler_params=_cp("parallel"),
    )(q, kv)


def attn_bwd(q, kv, do, name):
    L, D = q.shape
    tr = min(XA_TR, L)

    def body(q_ref, kv_ref, do_ref, dq_ref, dkv_ref):
        @pl.when(pl.program_id(0) == 0)
        def _():
            dkv_ref[...] = jnp.zeros_like(dkv_ref)

        for hd in range(XA_HEADS):
            c = slice(hd * XA_HD, (hd + 1) * XA_HD)
            cv = slice(D + hd * XA_HD, D + (hd + 1) * XA_HD)
            qh, kh, vh, doh = q_ref[:, c], kv_ref[:, c], kv_ref[:, cv], do_ref[:, c]
            p = _xa_probs(qh, kh)
            dp = _dot(doh, vh, 1, 1)
            dkv_ref[:, cv] += _dot(p.astype(BF16), doh, 0, 0)
            ds = (p * (dp - jnp.sum(dp * p, axis=-1, keepdims=True)) * XA_SCALE).astype(BF16)
            dq_ref[:, c] = _dot(ds, kh).astype(BF16)
            dkv_ref[:, c] += _dot(ds, qh, 0, 0)

    return pl.pallas_call(
        body, name=name, grid=(L // tr,),
        in_specs=[_row_spec(tr, D), pl.BlockSpec((N_MEM, 2 * D), lambda i: (0, 0)), _row_spec(tr, D)],
        out_specs=[_row_spec(tr, D), pl.BlockSpec((N_MEM, 2 * D), lambda i: (0, 0))],
        out_shape=[jax.ShapeDtypeStruct((L, D), BF16), jax.ShapeDtypeStruct((N_MEM, 2 * D), F32)],
        compiler_params=_cp("arbitrary"),
    )(q, kv, do)


N_PAIRS = N_HEADS // 2
PAIRS_PER_GROUP = N_PAIRS // N_GROUPS
GN = N_GROUPS * D_STATE


def _softplus(x):
    t = jnp.exp(-jnp.abs(x))
    return jnp.maximum(x, 0.0) + jnp.where(t < 1e-4, t * (1.0 - 0.5 * t), jnp.log(1.0 + t))


def _dot3b(m, v, ca=1, cb=0):
    v0, v1, v2 = _split3(v)
    return _dot(m, v0, ca, cb) + _dot(m, v1, ca, cb) + _dot(m, v2, ca, cb)


def ssd_consts():
    h = lax.broadcasted_iota(jnp.int32, (LANE, D_INNER), 0)
    c = lax.broadcasted_iota(jnp.int32, (LANE, D_INNER), 1)
    expand = (c // HEAD_DIM == h).astype(BF16)
    r = lax.broadcasted_iota(jnp.int32, (CHUNK, CHUNK), 0)
    k = lax.broadcasted_iota(jnp.int32, (CHUNK, CHUNK), 1)
    tri = (k <= r).astype(BF16)
    return expand, tri


def _ssd_common(dtr_ref, prm_ref, e_ref, tri_ref):
    lane = lax.broadcasted_iota(jnp.int32, (CHUNK, LANE), 1)
    valid = lane < N_HEADS
    A = -jnp.exp(prm_ref[1:2, :])
    pre = dtr_ref[...] + prm_ref[0:1, :]
    dt = jnp.where(valid, _softplus(pre), 0.0)
    cs = _dot3b(tri_ref[...], dt * A)
    E = e_ref[...]
    dt_x = _dot2(dt, E)
    cs_x = _dot3(cs, E)
    csl_x = cs_x[CHUNK - 1:CHUNK, :]
    return dict(valid=valid, A=A, pre=pre, dt=dt, cs=cs, csT=cs.T, dt_x=dt_x, ecs_x=jnp.exp(cs_x),
                dend_x=jnp.exp(csl_x - cs_x), cd_x=jnp.exp(csl_x), D_x=_dot3(prm_ref[...], E)[2:3, :])


def ssd_fwd(xs, bc, dtr, zx, prm, ng, name):
    L = xs.shape[0]
    nc = L // CHUNK
    expand, tri = ssd_consts()

    def body(xs_ref, bc_ref, dtr_ref, z_ref, prm_ref, ng_ref, e_ref, tri_ref, y_ref, yn_ref, st_ref, state):
        @pl.when(pl.program_id(0) == 0)
        def _():
            state[...] = jnp.zeros_like(state)

        q = _ssd_common(dtr_ref, prm_ref, e_ref, tri_ref)
        cs, csT = q["cs"], q["csT"]
        xs_v = xs_ref[...]
        X = xs_v * q["dt_x"]
        Xb = X.astype(BF16)
        Xd = (X * q["dend_x"]).astype(BF16)
        ii = lax.broadcasted_iota(jnp.int32, (CHUNK, CHUNK), 0)
        jj = lax.broadcasted_iota(jnp.int32, (CHUNK, CHUNK), 1)
        tril = jj <= ii
        first = jj < HEAD_DIM
        for g in range(N_GROUPS):
            Bg = bc_ref[:, g * D_STATE:(g + 1) * D_STATE]
            Cg = bc_ref[:, GN + g * D_STATE:GN + (g + 1) * D_STATE]
            S = _dot(Cg, Bg, 1, 1)
            for pr in range(PAIRS_PER_GROUP):
                pair = g * PAIRS_PER_GROUP + pr
                cols = slice(pair * LANE, (pair + 1) * LANE)
                Xp = Xb[:, cols]
                ys = []
                for h in (2 * pair, 2 * pair + 1):
                    seg = cs[:, h:h + 1] - csT[h:h + 1, :]
                    M = (S * jnp.exp(jnp.where(tril, seg, NEG))).astype(BF16)
                    ys.append(_dot(M, Xp))
                prevT = state[pair]
                st_ref[0, pair] = prevT
                yoff = _dot(Cg, prevT.astype(BF16)) * q["ecs_x"][:, cols]
                y_ref[:, cols] = jnp.where(first, ys[0], ys[1]) + yoff + xs_v[:, cols] * q["D_x"][:, cols]
                state[pair] = prevT * q["cd_x"][:, cols] + _dot(Bg, Xd[:, cols], 0, 0)
        z = z_ref[...].astype(F32)
        gt = y_ref[...] * z * _sig(z)
        yn_ref[...] = (gt * _rms(gt) * ng_ref[...]).astype(BF16)

    row = lambda w: pl.BlockSpec((CHUNK, w), lambda c: (c, 0))
    const = lambda a: pl.BlockSpec(a.shape, lambda c: (0,) * a.ndim)
    return pl.pallas_call(
        body, name=name, grid=(nc,),
        in_specs=[row(D_INNER), row(2 * GN), row(LANE), row(D_INNER), const(prm), const(ng), const(expand), const(tri)],
        out_specs=[row(D_INNER), row(D_INNER), pl.BlockSpec((1, N_PAIRS, D_STATE, LANE), lambda c: (c, 0, 0, 0))],
        out_shape=[jax.ShapeDtypeStruct((L, D_INNER), F32), jax.ShapeDtypeStruct((L, D_INNER), BF16),
                   jax.ShapeDtypeStruct((nc, N_PAIRS, D_STATE, LANE), F32)],
        scratch_shapes=[pltpu.VMEM((N_PAIRS, D_STATE, LANE), F32)],
        compiler_params=_cp("arbitrary"),
    )(xs, bc, dtr, zx, prm, ng, expand, tri)


def ssd_bwd(dyn, y, zx, xs, bc, dtr, st, prm, ng, name):
    L = xs.shape[0]
    nc = L // CHUNK
    expand, tri = ssd_consts()

    def body(dyn_ref, y_ref, z_ref, xs_ref, bc_ref, dtr_ref, st_ref, prm_ref, ng_ref, e_ref, tri_ref,
             dxbc_ref, dz_ref, ddtr_ref, dng_ref, dprm_ref, dstate, g_cs, g_q, dX, g_row):
        step = pl.program_id(0)

        @pl.when(step == 0)
        def _():
            dstate[...] = jnp.zeros_like(dstate)
            dng_ref[...] = jnp.zeros_like(dng_ref)
            dprm_ref[...] = jnp.zeros_like(dprm_ref)
            g_row[...] = jnp.zeros_like(g_row)

        q = _ssd_common(dtr_ref, prm_ref, e_ref, tri_ref)
        cs, csT, E = q["cs"], q["csT"], e_ref[...]
        xs_v = xs_ref[...]
        X = xs_v * q["dt_x"]
        Xb = X.astype(BF16)
        Xd_f = X * q["dend_x"]
        Xd = Xd_f.astype(BF16)

        yv = y_ref[...]
        z = z_ref[...].astype(F32)
        sz = _sig(z)
        silu = z * sz
        gt = yv * silu
        r = _rms(gt)
        gn = gt * r
        dyn_v = dyn_ref[...]
        dng_ref[...] += jnp.sum(dyn_v * gn, axis=0, keepdims=True)
        dgn = dyn_v * ng_ref[...]
        dgt = r * (dgn - gn * jnp.mean(dgn * gn, axis=-1, keepdims=True))
        dY = dgt * silu
        dz_ref[...] = (dgt * yv * sz * (1.0 + z * (1.0 - sz))).astype(BF16)
        dYb = dY.astype(BF16)
        g_row[1:2, :] += jnp.sum(dY * xs_v, axis=0, keepdims=True)

        ii = lax.broadcasted_iota(jnp.int32, (CHUNK, CHUNK), 0)
        jj = lax.broadcasted_iota(jnp.int32, (CHUNK, CHUNK), 1)
        tril = jj <= ii
        triu = jj >= ii
        first = jj < HEAD_DIM
        lane_row = lax.broadcasted_iota(jnp.int32, (1, LANE), 1)
        sub_col = lax.broadcasted_iota(jnp.int32, (CHUNK, 1), 0)
        dcs_col = jnp.zeros((CHUNK, LANE), F32)
        dcs_rowT = jnp.zeros((LANE, CHUNK), F32)
        for g in range(N_GROUPS):
            Bg = bc_ref[:, g * D_STATE:(g + 1) * D_STATE]
            Cg = bc_ref[:, GN + g * D_STATE:GN + (g + 1) * D_STATE]
            S = _dot(Cg, Bg, 1, 1)
            ST = _dot(Bg, Cg, 1, 1)
            dS = jnp.zeros((CHUNK, CHUNK), F32)
            dCg = jnp.zeros((CHUNK, D_STATE), F32)
            dBg = jnp.zeros((CHUNK, D_STATE), F32)
            for pr in range(PAIRS_PER_GROUP):
                pair = g * PAIRS_PER_GROUP + pr
                cols = slice(pair * LANE, (pair + 1) * LANE)
                Xp = Xb[:, cols]
                dYp_f = dY[:, cols]
                dYp = dYb[:, cols]
                prevT = st_ref[0, pair]
                prevTb = prevT.astype(BF16)
                dst = dstate[pair]
                dstb = dst.astype(BF16)
                ecs_p = q["ecs_x"][:, cols]
                g_cs[:, cols] = dYp_f * (_dot(Cg, prevTb) * ecs_p)
                dWb = (dYp_f * ecs_p).astype(BF16)
                dprev = dst * q["cd_x"][:, cols] + _dot(Cg, dWb, 0, 0)
                dCg = dCg + _dot(dWb, prevTb, 1, 1)
                g_row[0:1, cols] = jnp.sum(dst * prevT, axis=0, keepdims=True)
                dXp = None
                for hh, h in enumerate((2 * pair, 2 * pair + 1)):
                    mine = first if hh == 0 else jnp.logical_not(first)
                    seg = cs[:, h:h + 1] - csT[h:h + 1, :]
                    lam = jnp.exp(jnp.where(tril, seg, NEG))
                    dM = _dot(jnp.where(mine, dYp, jnp.zeros_like(dYp)), Xp, 1, 1)
                    dS = dS + dM * lam
                    Gm = dM * (S * lam)
                    dcs_col = dcs_col + jnp.sum(Gm, axis=1, keepdims=True) * (lane_row == h).astype(F32)
                    dcs_rowT = dcs_rowT + (sub_col == h).astype(F32) * jnp.sum(Gm, axis=0, keepdims=True)
                    MT = (ST * jnp.exp(jnp.where(triu, -seg, NEG))).astype(BF16)
                    t = _dot(MT, dYp)
                    dXp = t if dXp is None else jnp.where(first, dXp, t)
                dXd = _dot(Bg, dstb)
                dBg = dBg + _dot(Xd[:, cols], dstb, 1, 1)
                g_q[:, cols] = dXd * Xd_f[:, cols]
                dX[:, cols] = dXp + dXd * q["dend_x"][:, cols]
                dstate[pair] = dprev
            dSb = dS.astype(BF16)
            dxbc_ref[:, D_INNER + g * D_STATE:D_INNER + (g + 1) * D_STATE] = dBg + _dot(dSb, Cg, 0, 0)
            dxbc_ref[:, D_INNER + GN + g * D_STATE:D_INNER + GN + (g + 1) * D_STATE] = dCg + _dot(dSb, Bg)
        dXv = dX[...]
        dxbc_ref[:, 0:D_INNER] = q["D_x"] * dY + dXv * q["dt_x"]
        r_dt = _dot2(dXv * xs_v, E, 1, 1)
        r_cs = _dot2(g_cs[...], E, 1, 1)
        r_q = _dot2(g_q[...], E, 1, 1)
        r_row = _dot2(g_row[...], E, 1, 1)
        cd = jnp.exp(cs[CHUNK - 1:CHUNK, :])
        dcs_last = jnp.sum(r_q, axis=0, keepdims=True) + r_row[0:1, :] * cd
        dcs = r_cs - r_q + dcs_col - dcs_rowT.T + jnp.where(sub_col == CHUNK - 1, dcs_last, 0.0)
        da = _dot3b(tri_ref[...], dcs, 0, 0)
        dpre = jnp.where(q["valid"], (r_dt + da * q["A"]) * _sig(q["pre"]), 0.0)
        ddtr_ref[...] = dpre
        dprm_ref[0:1, :] += jnp.sum(dpre, axis=0, keepdims=True)
        dprm_ref[1:2, :] += jnp.sum(da * q["dt"], axis=0, keepdims=True) * q["A"]
        dprm_ref[2:3, :] = r_row[1:2, :]

    rev = lambda w: pl.BlockSpec((CHUNK, w), lambda c: (nc - 1 - c, 0))
    const = lambda a: pl.BlockSpec(a.shape, lambda c: (0,) * a.ndim)
    return pl.pallas_call(
        body, name=name, grid=(nc,),
        in_specs=[rev(D_INNER), rev(D_INNER), rev(D_INNER), rev(D_INNER), rev(2 * GN), rev(LANE),
                  pl.BlockSpec((1, N_PAIRS, D_STATE, LANE), lambda c: (nc - 1 - c, 0, 0, 0)),
                  const(prm), const(ng), const(expand), const(tri)],
        out_specs=[rev(CONV_DIM), rev(D_INNER), rev(LANE),
                   pl.BlockSpec((1, D_INNER), lambda c: (0, 0)), pl.BlockSpec((8, LANE), lambda c: (0, 0))],
        out_shape=[jax.ShapeDtypeStruct((L, CONV_DIM), F32), jax.ShapeDtypeStruct((L, D_INNER), BF16),
                   jax.ShapeDtypeStruct((L, LANE), F32), jax.ShapeDtypeStruct((1, D_INNER), F32),
                   jax.ShapeDtypeStruct((8, LANE), F32)],
        scratch_shapes=[pltpu.VMEM((N_PAIRS, D_STATE, LANE), F32), pltpu.VMEM((CHUNK, D_INNER), F32),
                        pltpu.VMEM((CHUNK, D_INNER), F32), pltpu.VMEM((CHUNK, D_INNER), F32),
                        pltpu.VMEM((8, D_INNER), F32)],
        compiler_params=_cp("arbitrary"),
    )(dyn, y, zx, xs, bc, dtr, st, prm, ng, expand, tri)


def _ssd_weights(W, j):
    w_in = W["ssm_in_w"][j]
    nzx = D_INNER + CONV_DIM
    wdt = jnp.pad(w_in[:, nzx:], ((0, 0), (0, LANE - N_HEADS)))
    prm = jnp.zeros((8, LANE), F32)
    prm = prm.at[0, :N_HEADS].set(W["ssm_dt_bias"][j]).at[1, :N_HEADS].set(W["ssm_A_log"][j])
    prm = prm.at[2, :N_HEADS].set(W["ssm_D"][j])
    return dict(wdt=wdt, cw=W["ssm_conv_w"][j], cb=W["ssm_conv_b"][j].reshape(1, CONV_DIM), prm=prm,
                ng=W["ssm_norm_g"][j].reshape(1, D_INNER))


def ssd_layer_fwd(h, W, j, tag):
    p = _ssd_weights(W, j)
    zx = mm(h, W["ssm_in_w"], layer=j, b_n=D_INNER + CONV_DIM, out_dtype=BF16, name=f"{tag}_zx")
    dtr = mm(h, p["wdt"], name=f"{tag}_dt")
    xs, pre_x = ssm_conv_fwd(zx, p["cw"], p["cb"], col0=D_INNER, ncols=D_INNER, wcol0=0, out_dtype=F32,
                             name=f"{tag}_convx")
    bc, pre_bc = ssm_conv_fwd(zx, p["cw"], p["cb"], col0=2 * D_INNER, ncols=2 * GN, wcol0=D_INNER, out_dtype=BF16,
                              name=f"{tag}_convbc")
    y, yn, st = ssd_fwd(xs, bc, dtr, zx, p["prm"], p["ng"], name=f"{tag}_scan")
    f = mm(yn, W["ssm_out_w"], layer=j, out_dtype=BF16, name=f"{tag}_out")
    return f, dict(h=h, zx=zx, dtr=dtr, xs=xs, bc=bc, pre_x=pre_x, pre_bc=pre_bc, y=y, yn=yn, st=st, p=p)


def ssd_layer_bwd(df, ctx, W, GB, j, tag, done=None):
    p = ctx["p"]
    h = ctx["h"]
    dyn = mm(df, W["ssm_out_w"], layer=j, tb=True, name=f"{tag}_b_dyn")
    GB["ssm_out_w"] = mm(ctx["yn"], df, ta=True, into=(GB["ssm_out_w"], j, 0), name=f"{tag}_b_gwo")
    dxbc, dz, ddtr, dng, dprm = ssd_bwd(dyn, ctx["y"], ctx["zx"], ctx["xs"], ctx["bc"], ctx["dtr"], ctx["st"],
                                        p["prm"], p["ng"], name=f"{tag}_b_scan")
    dx1, dcw1, dcb1 = ssm_conv_bwd(ctx["zx"], ctx["pre_x"], dxbc, p["cw"], col0=D_INNER, dcol0=0, ncols=D_INNER,
                                   name=f"{tag}_b_convx")
    dx2, dcw2, dcb2 = ssm_conv_bwd(ctx["zx"], ctx["pre_bc"], dxbc, p["cw"], col0=2 * D_INNER, dcol0=D_INNER,
                                   ncols=2 * GN, name=f"{tag}_b_convbc")
    g_in = jnp.concatenate([mm(h, dz, ta=True, out_dtype=BF16, name=f"{tag}_b_gz"),
                            mm(h, dx1, ta=True, out_dtype=BF16, name=f"{tag}_b_gx"),
                            mm(h, dx2, ta=True, out_dtype=BF16, name=f"{tag}_b_gbc"),
                            mm(h, ddtr, ta=True, out_dtype=BF16, name=f"{tag}_b_gdt")[:, :N_HEADS]], axis=1)
    gw = dict(ssm_conv_w=jnp.concatenate([dcw1, dcw2], axis=1), ssm_conv_b=jnp.concatenate([dcb1, dcb2], axis=1)[0],
              ssm_dt_bias=dprm[0, :N_HEADS], ssm_A_log=dprm[1, :N_HEADS], ssm_D=dprm[2, :N_HEADS], ssm_norm_g=dng[0])
    if done is None:
        gw["ssm_in_w"] = g_in
        token = None
    else:
        token = done(g_in)
    dh = mm(dz, W["ssm_in_w"], layer=j, tb=True, b_k0=0, dep=token, name=f"{tag}_b_dh1")
    dh = mm(dx1, W["ssm_in_w"], layer=j, tb=True, b_k0=D_INNER, acc=dh, name=f"{tag}_b_dh2")
    dh = mm(dx2, W["ssm_in_w"], layer=j, tb=True, b_k0=2 * D_INNER, acc=dh, name=f"{tag}_b_dh3")
    dh = mm(ddtr, p["wdt"], tb=True, acc=dh, out_dtype=BF16, name=f"{tag}_b_dh4")
    return dh, gw


def cf_layer_fwd(h, W, j, tag):
    u = mm(h, W["cf_pw1_w"], layer=j, bias=W["cf_pw1_b"][j], out_dtype=BF16, name=f"{tag}_pw1")
    c, s = cf_fwd(u, W["cf_dw_w"][j], W["cf_dw_b"][j].reshape(1, -1), W["cf_ln_g"][j].reshape(1, -1),
                  W["cf_ln_b"][j].reshape(1, -1), name=f"{tag}_conv")
    f = mm(s, W["cf_pw2_w"], layer=j, bias=W["cf_pw2_b"][j], out_dtype=BF16, name=f"{tag}_pw2")
    return f, dict(h=h, u=u, c=c, s=s)


def cf_layer_bwd(df, ctx, W, GB, j, tag):
    h = ctx["h"]
    ds = mm(df, W["cf_pw2_w"], layer=j, tb=True, name=f"{tag}_b_ds")
    GB["cf_pw2_w"] = mm(ctx["s"], df, ta=True, into=(GB["cf_pw2_w"], j, 0), name=f"{tag}_b_gpw2")
    g_b2 = colsum(df, name=f"{tag}_b_gb2")
    dc, dlg, dlb = cf_bwd_ln(ctx["c"], ds, W["cf_ln_g"][j].reshape(1, -1), W["cf_ln_b"][j].reshape(1, -1),
                             name=f"{tag}_b_ln")
    du, ddw, ddb = cf_bwd_conv(ctx["u"], dc, W["cf_dw_w"][j], name=f"{tag}_b_conv")
    dh = mm(du, W["cf_pw1_w"], layer=j, tb=True, out_dtype=BF16, name=f"{tag}_b_dh")
    GB["cf_pw1_w"] = mm(h, du, ta=True, into=(GB["cf_pw1_w"], j, 0), name=f"{tag}_b_gpw1")
    g_b1 = colsum(du, name=f"{tag}_b_gb1")
    return dh, dict(cf_pw1_b=g_b1[0], cf_dw_w=ddw, cf_dw_b=ddb[0], cf_ln_g=dlg[0], cf_ln_b=dlb[0], cf_pw2_b=g_b2[0])


def xa_layer_fwd(h, mem, W, i, tag):
    m = norm_fwd(mem, W["xa_mem_g"][i], name=f"{tag}_memnorm")
    kv = mm(m, W["xa_kv_w"], layer=i, out_dtype=BF16, name=f"{tag}_kv")
    q = mm(h, W["xa_q_w"], layer=i, out_dtype=BF16, name=f"{tag}_q")
    o = attn_fwd(q, kv, name=f"{tag}_attn")
    f = mm(o, W["xa_o_w"], layer=i, out_dtype=BF16, name=f"{tag}_o")
    return f, dict(h=h, m=m, kv=kv, q=q, o=o)


def xa_layer_bwd(df, ctx, mem, W, GB, i, tag):
    h = ctx["h"]
    do = mm(df, W["xa_o_w"], layer=i, tb=True, out_dtype=BF16, name=f"{tag}_b_do")
    GB["xa_o_w"] = mm(ctx["o"], df, ta=True, into=(GB["xa_o_w"], i, 0), name=f"{tag}_b_go")
    dq, dkv = attn_bwd(ctx["q"], ctx["kv"], do, name=f"{tag}_b_attn")
    dh = mm(dq, W["xa_q_w"], layer=i, tb=True, out_dtype=BF16, name=f"{tag}_b_dh")
    GB["xa_q_w"] = mm(h, dq, ta=True, into=(GB["xa_q_w"], i, 0), name=f"{tag}_b_gq")
    GB["xa_kv_w"] = mm(ctx["m"], dkv, ta=True, into=(GB["xa_kv_w"], i, 0), name=f"{tag}_b_gkv")
    dm = mm(dkv, W["xa_kv_w"], layer=i, tb=True, name=f"{tag}_b_dm")
    g_mg = norm_dg(mem, dm, name=f"{tag}_b_gmem")
    return dh, dict(xa_mem_g=g_mg[0])


def ffn_layer_fwd(h, W, i, tag):
    cw, cb = W["ffn_conv_w"][i], W["ffn_conv_b"][i].reshape(1, -1)
    u = mm(h, W["ffn_in_w"], layer=i, out_dtype=BF16, name=f"{tag}_in")
    act, c = ffn_act_fwd(u, cw, cb, name=f"{tag}_act")
    f = mm(act, W["ffn_out_w"], layer=i, out_dtype=BF16, name=f"{tag}_out")
    return f, dict(h=h, u=u, c=c, act=act)


def ffn_layer_bwd(df, ctx, W, GB, i, tag):
    h = ctx["h"]
    dact = mm(df, W["ffn_out_w"], layer=i, tb=True, out_dtype=BF16, name=f"{tag}_b_dact")
    GB["ffn_out_w"] = mm(ctx["act"], df, ta=True, into=(GB["ffn_out_w"], i, 0), name=f"{tag}_b_gout")
    du, dcw, dcb = ffn_act_bwd(ctx["u"], ctx["c"], dact, W["ffn_conv_w"][i], name=f"{tag}_b_act")
    dh = None
    for half in range(2):
        dh = mm(du, W["ffn_in_w"], a_idx=half, layer=i, tb=True, b_k0=half * D_FF, acc=dh,
                out_dtype=BF16 if half else F32, name=f"{tag}_b_dh{half}")
        GB["ffn_in_w"] = mm(h, du, ta=True, layer=half, into=(GB["ffn_in_w"], i, half * D_FF), name=f"{tag}_b_gin{half}")
    cat = lambda a: jnp.concatenate([a[0], a[1]], axis=-1)
    return dh, dict(ffn_conv_w=cat(dcw), ffn_conv_b=cat(dcb)[0])


def _sublayer_weights(i, s):
    if s == 0:
        return [("ssm_in_w", i // 2), ("ssm_out_w", i // 2)] if i % 2 == 0 else [("cf_pw1_w", i // 2), ("cf_pw2_w", i // 2)]
    return [(n, i) for n in (("xa_q_w", "xa_kv_w", "xa_o_w") if s == 1 else ("ffn_in_w", "ffn_out_w"))]


def local_step(x, mem, target, W, fetch=None, layer_done=None):
    subs = [(i, s) for i in range(DEPTH) for s in range(3)]
    ng = W["norm_g"]

    def fwd(i, s, h):
        tag = f"l{i}s{s}"
        if s == 0:
            return ssd_layer_fwd(h, W, i // 2, tag) if i % 2 == 0 else cf_layer_fwd(h, W, i // 2, tag)
        if s == 1:
            return xa_layer_fwd(h, mem, W, i, tag)
        return ffn_layer_fwd(h, W, i, tag)

    GB = {}

    early = {}

    def bwd(i, s, df, ctx):
        tag = f"l{i}s{s}"
        if s == 0 and i % 2 == 0:
            def done(g_in):
                early[i, s] = layer_done(i, s, GB, g_in)
                return early[i, s]
            return ssd_layer_bwd(df, ctx, W, GB, i // 2, tag, done if layer_done is not None else None)
        if s == 0:
            return cf_layer_bwd(df, ctx, W, GB, i // 2, tag)
        if s == 1:
            return xa_layer_bwd(df, ctx, mem, W, GB, i, tag)
        return ffn_layer_bwd(df, ctx, W, GB, i, tag)

    h = norm_fwd(x, ng[0, 0], name="norm0")
    saved = []
    dxp = loss = None
    for k, (i, s) in enumerate(subs):
        if fetch is not None:
            fetch(i, s, x)
        f, ctx = fwd(i, s, h)
        saved.append((x, f, ctx))
        if k + 1 < len(subs):
            ni, ns = subs[k + 1]
            x, h = bnd_fwd(x, f, ng[i, 2 * s + 1], ng[ni, 2 * ns], name=f"bnd{k}")
        else:
            dxp, loss = final_fwd(x, f, ng[i, 2 * s + 1], target, name="final")

    for n in BIG:
        if n != "ssm_in_w":
            GB[n] = jnp.zeros((len(W[n]), *W[n][0].shape), BF16)
    grads = {}

    def put(name, idx, val):
        grads.setdefault(name, {})[idx] = val

    i, s = subs[-1]
    top = bnd_bwd(dxp, post=(saved[-1][1], ng[i, 2 * s + 1]), name="bbnd_top")
    put("norm_g", (i, 2 * s + 1), top["dgpost"][0])
    df = top["df"]
    for k in range(len(subs) - 1, -1, -1):
        i, s = subs[k]
        xk, _, ctx = saved[k]
        dh, gw = bwd(i, s, df, ctx)
        for name, val in gw.items():
            put(name, i // 2 if name.startswith(("ssm_", "cf_")) else i, val)
        dep = early.get((i, s))
        if layer_done is not None and dep is None:
            dep = layer_done(i, s, GB, None)
        if k > 0:
            pi, ps = subs[k - 1]
            r = bnd_bwd(dxp, pre=(xk, ng[i, 2 * s], dh), post=(saved[k - 1][1], ng[pi, 2 * ps + 1]), dep=dep,
                        name=f"bbnd{k}")
            put("norm_g", (pi, 2 * ps + 1), r["dgpost"][0])
            df = r["df"]
        else:
            r = bnd_bwd(dxp, pre=(xk, ng[i, 2 * s], dh), dep=dep, name="bbnd0")
        put("norm_g", (i, 2 * s), r["dgpre"][0])
        dxp = r["dx"]

    out = {} if layer_done is not None else dict(GB)
    for name, d in grads.items():
        if name == "norm_g":
            out[name] = jnp.stack([jnp.stack([d[(i, t)] for t in range(6)]) for i in range(DEPTH)])
        elif d:
            out[name] = jnp.stack([d[j] for j in sorted(d)])
    return loss, dxp, out


ANY = pl.BlockSpec(memory_space=pl.ANY)


def _pos():
    return lax.axis_index("x"), lax.axis_index("y"), lax.axis_index("c")


def all_gather(shard, name):
    R, C = shard.shape

    def body(x_ref, out_ref, send_sems, recv_sems, local_sem):
        x, y, c = _pos()
        me, sibling = (x, y, c), (x, y, 1 - c)
        chips = [(1 - x, y), (x, 1 - y), (1 - x, 1 - y)]

        def slot(px, py, pc):
            return out_ref.at[4 * px + 2 * py + pc]

        def copy(k, block, to, src=None):
            return pltpu.make_async_remote_copy(
                src_ref=slot(*block) if src is None else src, dst_ref=slot(*block),
                send_sem=send_sems.at[k], recv_sem=recv_sems.at[k], device_id=to, device_id_type=MESH)

        mine = pltpu.make_async_copy(x_ref, slot(*me), local_sem)
        mine.start()
        first = [copy(0, me, sibling, src=x_ref)]
        first += [copy(1 + j, me, (*chip, c), src=x_ref) for j, chip in enumerate(chips)]
        for cp in first:
            cp.start()
        passed = [copy(4 + j, (*chip, c), sibling) for j, chip in enumerate(chips)]
        for j, chip in enumerate(chips):
            copy(1 + j, (*chip, c), me).wait_recv()
            passed[j].start()
        copy(0, sibling, me).wait_recv()
        for j, chip in enumerate(chips):
            copy(4 + j, (*chip, 1 - c), me).wait_recv()
        for cp in first + passed:
            cp.wait_send()
        mine.wait()

    return pl.pallas_call(
        body, name=name, out_shape=jax.ShapeDtypeStruct((N_DEV, R, C), shard.dtype),
        in_specs=[ANY], out_specs=ANY,
        scratch_shapes=[pltpu.SemaphoreType.DMA((7,)), pltpu.SemaphoreType.DMA((7,)), pltpu.SemaphoreType.DMA(())],
    )(shard)


def _win(ref, kind, k, a, b):
    if kind == "lead":
        return ref.at[k]
    if kind == "row":
        return ref.at[:, pl.ds(pl.multiple_of(k * a, 16), a), :]
    return ref.at[:, :, pl.ds(pl.multiple_of(k * b, LANE), b)]


def _full_shape(shard_shape, kind):
    n, a, b = shard_shape
    return {"lead": (N_DEV, n, a, b), "row": (n, N_DEV * a, b), "col": (n, a, N_DEV * b)}[kind]


HBM = pl.BlockSpec(memory_space=pltpu.HBM)
SEMS = pl.BlockSpec(memory_space=pltpu.SEMAPHORE)
DATAFLOW = pltpu.SideEffectType.DATAFLOW_SIDE_EFFECTING
N_PEER = N_DEV - 1


def _in_hbm(a):
    return pltpu.with_memory_space_constraint(a, pltpu.HBM)


def _peer(x, y, c, r):
    return ((1 - x) if r & 4 else x, (1 - y) if r & 2 else y, (1 - c) if r & 1 else c)


def _win2(ref, kind, k, a, b):
    if kind == "lead":
        return ref.at[k]
    if kind == "row":
        return ref.at[pl.ds(pl.multiple_of(k * a, 16), a), :]
    return ref.at[:, pl.ds(pl.multiple_of(k * b, LANE), b)]


def _zone_shape(kind, a, b):
    return {"lead": (N_DEV, a, b), "row": (N_DEV * a, b), "col": (a, N_DEV * b)}[kind]


def gather_start(shards, items, after, name):
    ns, nz, na = len(shards), len(items), len(after)
    zones = [lax.empty(_zone_shape(kind, a, b), shards[w].dtype) for w, l, kind, a, b in items]

    def body(*refs):
        x_refs = refs[:ns]
        send_sems, recv_sems, local_sems = refs[ns + nz + na:ns + nz + na + 3]
        z_refs = refs[ns + nz + na + 3 + ns:ns + nz + na + 3 + ns + nz]
        token = refs[-1]
        x, y, c = _pos()
        me = 4 * x + 2 * y + c
        for t, (w, l, kind, a, b) in enumerate(items):
            mine = _win2(z_refs[t], kind, me, a, b)
            pltpu.make_async_copy(x_refs[w].at[l], mine, local_sems.at[t]).start()
            for r in range(1, N_DEV):
                pltpu.make_async_remote_copy(
                    src_ref=x_refs[w].at[l], dst_ref=mine,
                    send_sem=send_sems.at[N_PEER * t + r - 1], recv_sem=recv_sems.at[N_PEER * t + r - 1],
                    device_id=_peer(x, y, c, r), device_id_type=MESH).start()
        token[...] = jnp.zeros_like(token)

    n_sem = N_PEER * nz
    outs = pl.pallas_call(
        body, name=name,
        out_shape=(pltpu.SemaphoreType.DMA((n_sem,)), pltpu.SemaphoreType.DMA((n_sem,)), pltpu.SemaphoreType.DMA((nz,)),
                   *[pltpu.HBM(s.shape, s.dtype) for s in shards], *[pltpu.HBM(z.shape, z.dtype) for z in zones],
                   jax.ShapeDtypeStruct((8, LANE), F32)),
        in_specs=[HBM] * (ns + nz) + [pl.BlockSpec(memory_space=pl.ANY)] * na,
        out_specs=(SEMS, SEMS, SEMS, *[HBM] * (ns + nz), pl.BlockSpec(memory_space=pltpu.VMEM)),
        input_output_aliases={i: 3 + i for i in range(ns + nz)},
        compiler_params=pltpu.CompilerParams(has_side_effects=DATAFLOW),
    )(*[_in_hbm(s) for s in shards], *[_in_hbm(z) for z in zones], *after)
    return outs[:3], list(outs[3:3 + ns]), list(outs[3 + ns:3 + ns + nz]), outs[-1]


def gather_wait(zones, idx, items, sems, after, keep, name):
    nz, nk = len(zones), len(keep)

    def body(*refs):
        z_refs = refs[:nz]
        send_sems, recv_sems, local_sems = refs[nz:nz + 3]
        x, y, c = _pos()
        me = 4 * x + 2 * y + c
        for z_ref, t in zip(z_refs, idx):
            w, l, kind, a, b = items[t]
            mine = _win2(z_ref, kind, me, a, b)
            pltpu.make_async_copy(mine, mine, local_sems.at[t]).wait()
            for r in range(1, N_DEV):
                peer = _peer(x, y, c, r)
                cp = pltpu.make_async_remote_copy(
                    src_ref=mine, dst_ref=_win2(z_ref, kind, 4 * peer[0] + 2 * peer[1] + peer[2], a, b),
                    send_sem=send_sems.at[N_PEER * t + r - 1], recv_sem=recv_sems.at[N_PEER * t + r - 1],
                    device_id=peer, device_id_type=MESH)
                cp.wait_send()
                cp.wait_recv()

    outs = pl.pallas_call(
        body, name=name, out_shape=tuple(pltpu.HBM(z.shape, z.dtype) for z in zones),
        in_specs=[HBM] * nz + [SEMS] * 3 + [pl.BlockSpec(memory_space=pl.ANY)] * (1 + nk),
        out_specs=tuple([HBM] * nz), input_output_aliases={i: i for i in range(nz)},
        compiler_params=pltpu.CompilerParams(has_side_effects=DATAFLOW),
    )(*zones, *sems, after, *keep)
    return list(outs)


def gather_now(shards, kinds, name):
    nw = len(shards)
    geo = [s.shape[1:] for s in shards]

    def body(*refs):
        x_refs, o_refs = refs[:nw], refs[nw:2 * nw]
        send_sems, recv_sems, local_sems = refs[2 * nw:]
        x, y, c = _pos()
        me, sibling = (x, y, c), (x, y, 1 - c)
        chips = [(1 - x, y), (x, 1 - y), (1 - x, 1 - y)]

        def slot(w, px, py, pc):
            return _win(o_refs[w], kinds[w], 4 * px + 2 * py + pc, *geo[w])

        def copy(w, k, block, to, src=None):
            return pltpu.make_async_remote_copy(
                src_ref=slot(w, *block) if src is None else src, dst_ref=slot(w, *block),
                send_sem=send_sems.at[7 * w + k], recv_sem=recv_sems.at[7 * w + k], device_id=to, device_id_type=MESH)

        mine = [pltpu.make_async_copy(x_refs[w], slot(w, *me), local_sems.at[w]) for w in range(nw)]
        for cp in mine:
            cp.start()
        first = []
        for w in range(nw):
            first.append(copy(w, 0, me, sibling, src=x_refs[w]))
            first += [copy(w, 1 + j, me, (*chip, c), src=x_refs[w]) for j, chip in enumerate(chips)]
        for cp in first:
            cp.start()
        passed = []
        for w in range(nw):
            for j, chip in enumerate(chips):
                copy(w, 1 + j, (*chip, c), me).wait_recv()
                cp = copy(w, 4 + j, (*chip, c), sibling)
                cp.start()
                passed.append(cp)
        for w in range(nw):
            copy(w, 0, sibling, me).wait_recv()
            for j, chip in enumerate(chips):
                copy(w, 4 + j, (*chip, 1 - c), me).wait_recv()
        for cp in first + passed:
            cp.wait_send()
        for cp in mine:
            cp.wait()

    return pl.pallas_call(
        body, name=name,
        out_shape=[jax.ShapeDtypeStruct(_full_shape(s.shape, k), s.dtype) for s, k in zip(shards, kinds)],
        in_specs=[ANY] * nw, out_specs=[ANY] * nw,
        scratch_shapes=[pltpu.SemaphoreType.DMA((7 * nw,)), pltpu.SemaphoreType.DMA((7 * nw,)),
                        pltpu.SemaphoreType.DMA((nw,))],
    )(*shards)


def _src_win(ref, l, kind, k, a, b):
    return _win2(ref if l is None else ref.at[l], kind, k, a, b)


def rs_start(srcs, items, name):
    ns, nz = len(srcs), len(items)
    zones = [lax.empty((N_PEER, a, b), srcs[w].dtype) for w, l, kind, a, b in items]

    def body(*refs):
        s_refs = refs[:ns]
        send_sems, recv_sems = refs[ns + nz], refs[ns + nz + 1]
        z_refs = refs[ns + nz + 2 + ns:ns + nz + 2 + ns + nz]
        token = refs[-1]
        x, y, c = _pos()
        for t, (w, l, kind, a, b) in enumerate(items):
            for r in range(1, N_DEV):
                peer = _peer(x, y, c, r)
                pltpu.make_async_remote_copy(
                    src_ref=_src_win(s_refs[w], l, kind, 4 * peer[0] + 2 * peer[1] + peer[2], a, b),
                    dst_ref=z_refs[t].at[r - 1],
                    send_sem=send_sems.at[N_PEER * t + r - 1], recv_sem=recv_sems.at[N_PEER * t + r - 1],
                    device_id=peer, device_id_type=MESH).start()
        token[...] = jnp.zeros_like(token)

    n_sem = N_PEER * nz
    outs = pl.pallas_call(
        body, name=name,
        out_shape=(pltpu.SemaphoreType.DMA((n_sem,)), pltpu.SemaphoreType.DMA((n_sem,)),
                   *[pltpu.HBM(s.shape, s.dtype) for s in srcs], *[pltpu.HBM(z.shape, z.dtype) for z in zones],
                   jax.ShapeDtypeStruct((8, LANE), F32)),
        in_specs=[HBM] * (ns + nz), out_specs=(SEMS, SEMS, *[HBM] * (ns + nz), pl.BlockSpec(memory_space=pltpu.VMEM)),
        input_output_aliases={i: 2 + i for i in range(ns + nz)},
        compiler_params=pltpu.CompilerParams(has_side_effects=DATAFLOW),
    )(*[_in_hbm(s) for s in srcs], *[_in_hbm(z) for z in zones])
    return outs[:2], list(outs[2:2 + ns]), list(outs[2 + ns:2 + ns + nz]), outs[-1]


def rs_wait(zones, items, sems, after, keep, name):
    nz, nk = len(zones), len(keep)

    def body(*refs):
        z_refs = refs[:nz]
        send_sems, recv_sems = refs[nz], refs[nz + 1]
        x, y, c = _pos()
        for t, z_ref in enumerate(z_refs):
            for r in range(1, N_DEV):
                cp = pltpu.make_async_remote_copy(
                    src_ref=z_ref.at[r - 1], dst_ref=z_ref.at[r - 1],
                    send_sem=send_sems.at[N_PEER * t + r - 1], recv_sem=recv_sems.at[N_PEER * t + r - 1],
                    device_id=_peer(x, y, c, r), device_id_type=MESH)
                cp.wait_send()
                cp.wait_recv()

    outs = pl.pallas_call(
        body, name=name, out_shape=tuple(pltpu.HBM(z.shape, z.dtype) for z in zones),
        in_specs=[HBM] * nz + [SEMS] * 2 + [pl.BlockSpec(memory_space=pl.ANY)] * (1 + nk),
        out_specs=tuple([HBM] * nz), input_output_aliases={i: i for i in range(nz)},
        compiler_params=pltpu.CompilerParams(has_side_effects=DATAFLOW),
    )(*zones, *sems, after, *keep)
    return list(outs)


def adam_rs(w, m, v, l, own, kind, zone, outs, name):
    n, a, b = w.shape
    ta = max(t for t in range(16, min(a, 256) + 1, 16) if a % t == 0)
    per = a // ta
    me = (4 * lax.axis_index("x") + 2 * lax.axis_index("y") + lax.axis_index("c")).astype(jnp.int32).reshape(1)

    def body(me_ref, w_ref, m_ref, v_ref, own_ref, z_ref, i0, i1, i2, i3, g_ref, d_ref, m2_ref, v2_ref):
        gv = own_ref[...].astype(F32)
        for k in range(N_PEER):
            gv = gv + z_ref[k].astype(F32)
        m2 = ADAM_B1 * m_ref[...] + (1.0 - ADAM_B1) * gv
        v2 = ADAM_B2 * v_ref[...] + (1.0 - ADAM_B2) * (gv * gv)
        m_hat = m2 / (1.0 - ADAM_B1 ** ADAM_STEP)
        v_hat = v2 / (1.0 - ADAM_B2 ** ADAM_STEP)
        g_ref[...] = gv
        d_ref[...] = -ADAM_LR * (m_hat / (jnp.sqrt(v_hat) + ADAM_EPS) + ADAM_WD * w_ref[...])
        m2_ref[...] = m2
        v2_ref[...] = v2

    spec = pl.BlockSpec((None, ta, b), lambda r, me_ref: (l, r, 0))
    if kind == "lead":
        own_spec = pl.BlockSpec((None, ta, b), lambda r, me_ref: (me_ref[0], r, 0))
    elif kind == "row":
        own_spec = pl.BlockSpec((None, ta, b), lambda r, me_ref: (l, me_ref[0] * per + r, 0))
    else:
        own_spec = pl.BlockSpec((None, ta, b), lambda r, me_ref: (l, r, me_ref[0]))
    return pl.pallas_call(
        body, name=name, out_shape=[jax.ShapeDtypeStruct((n, a, b), F32)] * 4,
        grid_spec=pltpu.PrefetchScalarGridSpec(
            num_scalar_prefetch=1, grid=(per,),
            in_specs=[spec] * 3 + [own_spec, pl.BlockSpec((N_PEER, ta, b), lambda r, me_ref: (0, r, 0))] + [ANY] * 4,
            out_specs=[spec] * 4),
        input_output_aliases={6 + k: k for k in range(4)},
        compiler_params=_cp("parallel"),
    )(me, w, m, v, own, zone, *outs)


def small_exchange(sh, rep, after, name):
    _, Rs, C = sh.shape
    na = len(after)

    def body(*refs):
        sh_ref, rep_ref = refs[:2]
        sh_out, rep_out, send_sems, recv_sems, local_sems = refs[2 + na:]
        x, y, c = _pos()
        me = 4 * x + 2 * y + c
        l1 = pltpu.make_async_copy(sh_ref.at[me], sh_out.at[me], local_sems.at[0])
        l2 = pltpu.make_async_copy(rep_ref, rep_out.at[me], local_sems.at[1])
        l1.start()
        l2.start()

        def flip(v, bit):
            return 1 - v if bit else v

        sends, recvs = [], []
        for r in range(1, N_DEV):
            peer = (flip(x, r & 4), flip(y, r & 2), flip(c, r & 1))
            pid = 4 * peer[0] + 2 * peer[1] + peer[2]
            k = 2 * (r - 1)
            mk = lambda src, dst, kk: pltpu.make_async_remote_copy(
                src_ref=src, dst_ref=dst, send_sem=send_sems.at[kk], recv_sem=recv_sems.at[kk],
                device_id=peer, device_id_type=MESH)
            sends += [mk(sh_ref.at[pid], sh_out.at[me], k), mk(rep_ref, rep_out.at[me], k + 1)]
            recvs += [mk(sh_ref.at[me], sh_out.at[pid], k), mk(rep_ref, rep_out.at[pid], k + 1)]
        for cp in sends:
            cp.start()
        for cp in recvs:
            cp.wait_recv()
        for cp in sends:
            cp.wait_send()
        l1.wait()
        l2.wait()

    n = 2 * (N_DEV - 1)
    return pl.pallas_call(
        body, name=name,
        out_shape=[jax.ShapeDtypeStruct((N_DEV, Rs, C), sh.dtype), jax.ShapeDtypeStruct((N_DEV, *rep.shape), rep.dtype)],
        in_specs=[ANY] * (2 + na), out_specs=[ANY, ANY],
        scratch_shapes=[pltpu.SemaphoreType.DMA((n,)), pltpu.SemaphoreType.DMA((n,)), pltpu.SemaphoreType.DMA((2,))],
    )(sh, rep, *after)


def adam_slots(w, m, v, slots, name):
    S, n, a, b = slots.shape
    ta = max(t for t in range(16, min(a, 512) + 1, 8)
             if a % t == 0 and t * S * b * slots.dtype.itemsize <= 4 * 1024 * 1024)

    def body(w_ref, m_ref, v_ref, s_ref, g_ref, d_ref, m2_ref, v2_ref):
        gv = s_ref[0].astype(F32)
        for k in range(1, S):
            gv = gv + s_ref[k].astype(F32)
        m2 = ADAM_B1 * m_ref[...] + (1.0 - ADAM_B1) * gv
        v2 = ADAM_B2 * v_ref[...] + (1.0 - ADAM_B2) * (gv * gv)
        m_hat = m2 / (1.0 - ADAM_B1 ** ADAM_STEP)
        v_hat = v2 / (1.0 - ADAM_B2 ** ADAM_STEP)
        g_ref[...] = gv
        d_ref[...] = -ADAM_LR * (m_hat / (jnp.sqrt(v_hat) + ADAM_EPS) + ADAM_WD * w_ref[...])
        m2_ref[...] = m2
        v2_ref[...] = v2

    spec = pl.BlockSpec((None, ta, b), lambda l, r: (l, r, 0))
    return pl.pallas_call(
        body, name=name, grid=(n, a // ta),
        in_specs=[spec] * 3 + [pl.BlockSpec((S, None, ta, b), lambda l, r: (0, l, r, 0))], out_specs=[spec] * 4,
        out_shape=[jax.ShapeDtypeStruct((n, a, b), F32)] * 4, compiler_params=_cp("parallel", "parallel"),
    )(w, m, v, slots)


WEIGHTS = ["norm_g", "ssm_in_w", "ssm_conv_w", "ssm_conv_b", "ssm_dt_bias", "ssm_A_log", "ssm_D", "ssm_norm_g",
           "ssm_out_w", "cf_pw1_w", "cf_pw1_b", "cf_dw_w", "cf_dw_b", "cf_ln_g", "cf_ln_b", "cf_pw2_w", "cf_pw2_b",
           "xa_mem_g", "xa_q_w", "xa_kv_w", "xa_o_w", "ffn_in_w", "ffn_conv_w", "ffn_conv_b", "ffn_out_w"]
ARGS = ["x", "mem"] + WEIGHTS + ["loss_target"] + ["m_" + n for n in WEIGHTS] + ["v_" + n for n in WEIGHTS]
BIG = {"ssm_in_w": "col", "ssm_out_w": "row", "cf_pw1_w": "col", "cf_pw2_w": "row", "xa_q_w": "row",
       "xa_kv_w": "col", "xa_o_w": "row", "ffn_in_w": "col", "ffn_out_w": "row"}
SMALL = ["norm_g", "ssm_conv_w", "cf_pw1_b", "cf_dw_w", "cf_dw_b", "cf_ln_g", "cf_ln_b", "cf_pw2_b", "ffn_conv_w"]
REP = ["ssm_conv_b", "ssm_dt_bias", "ssm_A_log", "ssm_D", "ssm_norm_g", "xa_mem_g", "ffn_conv_b"]
SMALL_W = 768
REP_W = 512


def _r8(n):
    return -(-n // 8) * 8


def _stack2d(arrs, wid):
    parts = []
    for a in arrs:
        r, c = a.shape[-2:]
        parts.append(jnp.pad(a, [(0, 0)] * (a.ndim - 2) + [(0, _r8(r) - r), (0, wid - c)]))
    return jnp.concatenate(parts, axis=-2)


def _unstack2d(buf, shapes2d):
    out, o = [], 0
    for r, c in shapes2d:
        out.append(buf[..., o:o + r, :c])
        o += _r8(r)
    return out


def _gathered_to_full(g):
    lead = g.shape[1:-1]
    return jnp.moveaxis(g, 0, -2).reshape(*lead, N_DEV * g.shape[-1])


def _full_to_slots(w):
    lead = w.shape[:-1]
    return jnp.moveaxis(w.reshape(*lead, N_DEV, w.shape[-1] // N_DEV), -2, 0)


def kernel(x, mem, norm_g, ssm_in_w, ssm_conv_w, ssm_conv_b, ssm_dt_bias, ssm_A_log, ssm_D, ssm_norm_g, ssm_out_w, cf_pw1_w, cf_pw1_b, cf_dw_w, cf_dw_b, cf_ln_g, cf_ln_b, cf_pw2_w, cf_pw2_b, xa_mem_g, xa_q_w, xa_kv_w, xa_o_w, ffn_in_w, ffn_conv_w, ffn_conv_b, ffn_out_w, loss_target, m_norm_g, m_ssm_in_w, m_ssm_conv_w, m_ssm_conv_b, m_ssm_dt_bias, m_ssm_A_log, m_ssm_D, m_ssm_norm_g, m_ssm_out_w, m_cf_pw1_w, m_cf_pw1_b, m_cf_dw_w, m_cf_dw_b, m_cf_ln_g, m_cf_ln_b, m_cf_pw2_w, m_cf_pw2_b, m_xa_mem_g, m_xa_q_w, m_xa_kv_w, m_xa_o_w, m_ffn_in_w, m_ffn_conv_w, m_ffn_conv_b, m_ffn_out_w, v_norm_g, v_ssm_in_w, v_ssm_conv_w, v_ssm_conv_b, v_ssm_dt_bias, v_ssm_A_log, v_ssm_D, v_ssm_norm_g, v_ssm_out_w, v_cf_pw1_w, v_cf_pw1_b, v_cf_dw_w, v_cf_dw_b, v_cf_ln_g, v_cf_ln_b, v_cf_pw2_w, v_cf_pw2_b, v_xa_mem_g, v_xa_q_w, v_xa_kv_w, v_xa_o_w, v_ffn_in_w, v_ffn_conv_w, v_ffn_conv_b, v_ffn_out_w):
    return _step(x, mem, norm_g, ssm_in_w, ssm_conv_w, ssm_conv_b, ssm_dt_bias, ssm_A_log, ssm_D, ssm_norm_g, ssm_out_w, cf_pw1_w, cf_pw1_b, cf_dw_w, cf_dw_b, cf_ln_g, cf_ln_b, cf_pw2_w, cf_pw2_b, xa_mem_g, xa_q_w, xa_kv_w, xa_o_w, ffn_in_w, ffn_conv_w, ffn_conv_b, ffn_out_w, loss_target, m_norm_g, m_ssm_in_w, m_ssm_conv_w, m_ssm_conv_b, m_ssm_dt_bias, m_ssm_A_log, m_ssm_D, m_ssm_norm_g, m_ssm_out_w, m_cf_pw1_w, m_cf_pw1_b, m_cf_dw_w, m_cf_dw_b, m_cf_ln_g, m_cf_ln_b, m_cf_pw2_w, m_cf_pw2_b, m_xa_mem_g, m_xa_q_w, m_xa_kv_w, m_xa_o_w, m_ffn_in_w, m_ffn_conv_w, m_ffn_conv_b, m_ffn_out_w, v_norm_g, v_ssm_in_w, v_ssm_conv_w, v_ssm_conv_b, v_ssm_dt_bias, v_ssm_A_log, v_ssm_D, v_ssm_norm_g, v_ssm_out_w, v_cf_pw1_w, v_cf_pw1_b, v_cf_dw_w, v_cf_dw_b, v_cf_ln_g, v_cf_ln_b, v_cf_pw2_w, v_cf_pw2_b, v_xa_mem_g, v_xa_q_w, v_xa_kv_w, v_xa_o_w, v_ffn_in_w, v_ffn_conv_w, v_ffn_conv_b, v_ffn_out_w)


def _step(*args):
    A = dict(zip(ARGS, args, strict=True))
    x, mem, target = A["x"][0], A["mem"][0], A["loss_target"][0]

    big = list(BIG)
    geo = [A[n].shape for n in big]
    kinds = ["row" if BIG[n] == "row" else ("col" if A[n].shape[-1] % LANE == 0 else "lead") for n in big]
    W = {n: A[n] for n in REP}
    small2d = [(A[n].size // A[n].shape[-1], A[n].shape[-1]) for n in SMALL]
    rep2d = [(A[n].size // REP_W, REP_W) if A[n].shape[-1] % REP_W == 0 else A[n].shape for n in REP] + [(1, 1)]
    stack_small = lambda pre: _stack2d([A[pre + n].reshape(rc) for n, rc in zip(SMALL, small2d)], SMALL_W)
    stack_rep = lambda pre: _stack2d([A[pre + n].reshape(rc) for n, rc in zip(REP, rep2d)] + [jnp.zeros((1, 1), F32)],
                                     REP_W)
    small_g = all_gather(stack_small(""), name="gather_small")
    for n, g in zip(SMALL, _unstack2d(small_g, small2d)):
        W[n] = _gathered_to_full(g.reshape(N_DEV, *A[n].shape))

    shards = [A[n].astype(BF16) for n in big]
    for n in big:
        W[n] = [None] * A[n].shape[0]
    first = _sublayer_weights(0, 0)
    got0 = gather_now([shards[big.index(n)][l:l + 1] for n, l in first], [kinds[big.index(n)] for n, l in first],
                      name="gather_first")
    for (n, l), g in zip(first, got0):
        W[n][l] = _gathered_to_full(g)[0] if kinds[big.index(n)] == "lead" else g[0]
    items, sub_items = [], {}
    for i in range(DEPTH):
        for s in range(3):
            sub_items[i, s] = []
            for n, l in _sublayer_weights(i, s) if (i, s) != (0, 0) else []:
                w = big.index(n)
                sub_items[i, s].append(len(items))
                items.append((w, l, kinds[w], *geo[w][1:]))
    sems, shards_thru, zones, token = gather_start(shards, items, [small_g, got0[0]], name="gather_start")
    x = x + token[0, 0]

    def fetch(i, s, x_in):
        ids = sub_items[i, s]
        if not ids:
            return
        got = gather_wait([zones[t] for t in ids], ids, items, sems, x_in, shards_thru if (i, s) == (DEPTH - 1, 2) else [],
                          name=f"gather_wait{i}{s}")
        for t, z in zip(ids, got):
            w, l, kind = items[t][:3]
            W[big[w]][l] = _gathered_to_full(z) if kind == "lead" else z

    sent = []
    final = {}

    def layer_done(i, s, GB, g_in):
        srcs, its = [], []
        for n, l in _sublayer_weights(i, s):
            w = big.index(n)
            if kinds[w] == "lead":
                srcs.append(_full_to_slots(g_in if n == "ssm_in_w" else GB[n][l]))
                its.append((len(srcs) - 1, None, "lead", *geo[w][1:], n, l))
            else:
                srcs.append(GB[n])
                its.append((len(srcs) - 1, l, kinds[w], *geo[w][1:], n, l))
        sems_i, thru, zones_i, token_i = rs_start(srcs, [it[:5] for it in its], name=f"rs_start{i}{s}")
        for it, s in zip(its, thru):
            if it[2] != "lead":
                GB[it[5]] = s
        sent.append((its, sems_i, [s for it, s in zip(its, thru) if it[2] == "lead"], zones_i))
        final["GB"] = GB
        return token_i

    loss, grad_x, G = local_step(x, mem, target, W, fetch, layer_done)

    sh = _stack2d([_full_to_slots(G[n]).reshape(N_DEV, *rc) for n, rc in zip(SMALL, small2d)], SMALL_W)
    rep = _stack2d([G[n].reshape(rc) for n, rc in zip(REP, rep2d)] + [loss[:, :1]], REP_W)

    res = {}
    GBf = final["GB"]
    bufs = {n: [lax.empty(A[n].shape, F32) for _ in range(4)] for n in big}
    sh_got = rep_got = None
    for i, (its, sems_i, lead_srcs, zones_i) in enumerate(sent):
        if i == len(sent) - 1:
            sh_got, rep_got = small_exchange(sh, rep, [bufs[n][0] for n in big], name="small_exchange")
        keep = lead_srcs + [GBf[it[5]] for it in its if it[2] != "lead"]
        zones_i = rs_wait(zones_i, [it[:5] for it in its], sems_i, grad_x if sh_got is None else sh_got, keep,
                          name=f"rs_wait{i}")
        lead_it = iter(lead_srcs)
        for it, z in zip(its, zones_i):
            n, l = it[5], it[6]
            own = next(lead_it) if it[2] == "lead" else GBf[n]
            bufs[n] = adam_rs(A[n], A["m_" + n], A["v_" + n], l, own, it[2], z, bufs[n], name=f"adam_{n}{l}")
    for n in big:
        res[n] = tuple(bufs[n])
    for names, shapes2d, stack, slots, tag in ((SMALL, small2d, stack_small, sh_got, "small"),
                                               (REP, rep2d, stack_rep, rep_got, "rep")):
        outs4 = adam_slots(stack("")[None], stack("m_")[None], stack("v_")[None], slots[:, None], name=f"adam_{tag}")
        parts = [_unstack2d(o[0], shapes2d) for o in outs4]
        for k, n in enumerate(names):
            res[n] = tuple(q[k].reshape(A[n].shape) for q in parts)
        if tag == "rep":
            total_loss = parts[0][-1][0, 0]

    outs = [total_loss, grad_x[None]]
    for k in range(4):
        outs += [res[n][k] for n in WEIGHTS]
    return tuple(outs)
```

```python
import jax
import jax.numpy as jnp
from jax import lax
from jax.experimental import pallas as pl
from jax.experimental.pallas import tpu as pltpu

F32 = jnp.float32
BF16 = jnp.bfloat16

D_MODEL = 1024
D_INNER = 2048
N_HEADS = 32
HEAD_DIM = 64
N_GROUPS = 4
D_STATE = 128
CHUNK = 128
CONV_DIM = 3072
SSM_K = 4
CF_K = 31
N_MEM = 256
XA_HEADS = 4
XA_HD = 256
D_FF = 2816
FFN_K = 3
EPS = 1e-6
DEPTH = 4
N_DEV = 8

ADAM_LR = 0.001
ADAM_B1 = 0.9
ADAM_B2 = 0.999
ADAM_EPS = 1e-08
ADAM_WD = 0.01
ADAM_STEP = 10

LANE = 128
VMEM_LIMIT = 56 * 1024 * 1024
NEG = -1e30
MESH = pl.DeviceIdType.MESH


def _cp(*sem):
    return pltpu.CompilerParams(dimension_semantics=sem if sem else None, vmem_limit_bytes=VMEM_LIMIT)


def _tile(n, cap):
    if n <= cap:
        return n
    best = 0
    for t in range(LANE, cap + 1, LANE):
        if n % t == 0:
            best = t
    assert best, (n, cap)
    return best


def _sig(x):
    return 1.0 / (1.0 + jnp.exp(-x))


def _split3(v):
    v0 = v.astype(BF16)
    r1 = v - v0.astype(F32)
    v1 = r1.astype(BF16)
    v2 = (r1 - v1.astype(F32)).astype(BF16)
    return v0, v1, v2


def _dot(a, b, ca=1, cb=0):
    return lax.dot_general(a, b, (((ca,), (cb,)), ((), ())), preferred_element_type=F32)


def _dot2(v, m, ca=1, cb=0):
    v0 = v.astype(BF16)
    v1 = (v - v0.astype(F32)).astype(BF16)
    return _dot(v0, m, ca, cb) + _dot(v1, m, ca, cb)


def _dot3(v, m, ca=1, cb=0):
    v0, v1, v2 = _split3(v)
    return _dot(v0, m, ca, cb) + _dot(v1, m, ca, cb) + _dot(v2, m, ca, cb)


def mm(a, b, *, ta=False, tb=False, bias=None, acc=None, out_dtype=F32, a_idx=None, layer=None, b_k0=0, b_n=None,
       into=None, dep=None, name):
    if isinstance(b, (list, tuple)):
        b, layer = b[layer], None
    if ta:
        K, M = a.shape[-2:]
    else:
        M, K = a.shape[-2:]
    N = b_n if b_n is not None else (b.shape[-2] if tb else b.shape[-1])
    assert (b.ndim == 3) == (layer is not None) and (a.ndim == 3) == (a_idx is not None)
    tm = _tile(M, 1536)
    tn = _tile(N, 1536)
    tk = _tile(K, 2048)
    nk = K // tk
    assert b_k0 % tk == 0 and b_k0 + K <= (b.shape[-1] if tb else b.shape[-2])
    kb = b_k0 // tk
    has_bias, has_acc = bias is not None, acc is not None
    if into is not None:
        out_dtype = into[0].dtype
        assert into[0].shape[1] == M and into[2] % tn == 0 and into[2] + N <= into[0].shape[2] and not has_acc

    def body(*refs):
        a_ref, b_ref = refs[0], refs[1]
        pos = 2
        bias_ref = acc_ref = None
        if has_bias:
            bias_ref = refs[pos]
            pos += 1
        if has_acc:
            acc_ref = refs[pos]
            pos += 1
        if into is not None:
            pos += 1
        if dep is not None:
            pos += 1
        o_ref = refs[pos]
        s_ref = refs[pos + 1] if nk > 1 else None
        p = _dot(a_ref[...].astype(BF16), b_ref[...].astype(BF16), 0 if ta else 1, 1 if tb else 0)

        def extras(v):
            if has_bias:
                v = v + bias_ref[...]
            if has_acc:
                v = v + acc_ref[...]
            return v

        if nk == 1:
            o_ref[...] = extras(p).astype(out_dtype)
        else:
            k = pl.program_id(2)

            @pl.when(k == 0)
            def _():
                s_ref[...] = extras(p)

            @pl.when(k > 0)
            def _():
                s_ref[...] += p

            @pl.when(k == nk - 1)
            def _():
                o_ref[...] = s_ref[...].astype(out_dtype)

    lead_a = () if a_idx is None else (a_idx,)
    lead_b = () if layer is None else (layer,)
    sq = lambda lead: (None,) * len(lead)
    if ta:
        a_spec = pl.BlockSpec((*sq(lead_a), tk, tm), lambda i, j, k: (*lead_a, k, i))
    else:
        a_spec = pl.BlockSpec((*sq(lead_a), tm, tk), lambda i, j, k: (*lead_a, i, k))
    if tb:
        b_spec = pl.BlockSpec((*sq(lead_b), tn, tk), lambda i, j, k: (*lead_b, j, k + kb))
    else:
        b_spec = pl.BlockSpec((*sq(lead_b), tk, tn), lambda i, j, k: (*lead_b, k + kb, j))
    in_specs, args = [a_spec, b_spec], [a, b]
    if has_bias:
        in_specs.append(pl.BlockSpec((1, tn), lambda i, j, k: (0, j)))
        args.append(bias.reshape(1, N).astype(F32))
    if has_acc:
        in_specs.append(pl.BlockSpec((tm, tn), lambda i, j, k: (i, j)))
        args.append(acc)
    if into is None:
        out_spec = pl.BlockSpec((tm, tn), lambda i, j, k: (i, j))
        out_shape = jax.ShapeDtypeStruct((M, N), out_dtype)
        aliases = {}
    else:
        buf, l, col0 = into
        cb = col0 // tn
        in_specs.append(pl.BlockSpec(memory_space=pl.ANY))
        args.append(buf)
        out_spec = pl.BlockSpec((None, tm, tn), lambda i, j, k: (l, i, j + cb))
        out_shape = jax.ShapeDtypeStruct(buf.shape, buf.dtype)
        aliases = {len(args) - 1: 0}
    if dep is not None:
        in_specs.append(pl.BlockSpec(memory_space=pl.ANY))
        args.append(dep)
    return pl.pallas_call(
        body, name=name, grid=(M // tm, N // tn, nk),
        in_specs=in_specs, out_specs=out_spec, out_shape=out_shape, input_output_aliases=aliases,
        scratch_shapes=[pltpu.VMEM((tm, tn), F32)] if nk > 1 else [],
        compiler_params=_cp("parallel", "parallel", "arbitrary"),
    )(*args)


def colsum(x, name):
    L, C = x.shape
    tr = _tile(L, 512)
    tc = _tile(C, 1024)

    def body(x_ref, o_ref):
        @pl.when(pl.program_id(1) == 0)
        def _():
            o_ref[...] = jnp.zeros_like(o_ref)

        o_ref[...] += jnp.sum(x_ref[...].astype(F32), axis=0, keepdims=True)

    return pl.pallas_call(
        body, name=name, grid=(C // tc, L // tr),
        in_specs=[pl.BlockSpec((tr, tc), lambda j, i: (i, j))],
        out_specs=pl.BlockSpec((1, tc), lambda j, i: (0, j)),
        out_shape=jax.ShapeDtypeStruct((1, C), F32),
        compiler_params=_cp("parallel", "arbitrary"),
    )(x)


TR = 512


def _row_spec(tr, w):
    return pl.BlockSpec((tr, w), lambda i: (i, 0))


def _vec_spec(w):
    return pl.BlockSpec((1, w), lambda i: (0, 0))


def _rms(v):
    return lax.rsqrt(jnp.mean(v * v, axis=-1, keepdims=True) + EPS)


def norm_fwd(x, g, name):
    L, D = x.shape
    tr = min(TR, L)

    def body(x_ref, g_ref, h_ref):
        xv = x_ref[...]
        h_ref[...] = (xv * _rms(xv) * g_ref[...]).astype(BF16)

    return pl.pallas_call(
        body, name=name, grid=(L // tr,),
        in_specs=[_row_spec(tr, D), _vec_spec(D)], out_specs=_row_spec(tr, D),
        out_shape=jax.ShapeDtypeStruct((L, D), BF16), compiler_params=_cp("parallel"),
    )(x, g.reshape(1, D))


def bnd_fwd(x, f, gpost, gpre, name):
    L, D = x.shape
    tr = min(TR, L)

    def body(x_ref, f_ref, gp_ref, gn_ref, xo_ref, h_ref):
        fv = f_ref[...].astype(F32)
        xn = x_ref[...] + fv * _rms(fv) * gp_ref[...]
        xo_ref[...] = xn
        h_ref[...] = (xn * _rms(xn) * gn_ref[...]).astype(BF16)

    return pl.pallas_call(
        body, name=name, grid=(L // tr,),
        in_specs=[_row_spec(tr, D), _row_spec(tr, D), _vec_spec(D), _vec_spec(D)],
        out_specs=[_row_spec(tr, D), _row_spec(tr, D)],
        out_shape=[jax.ShapeDtypeStruct((L, D), F32), jax.ShapeDtypeStruct((L, D), BF16)],
        compiler_params=_cp("parallel"),
    )(x, f, gpost.reshape(1, D), gpre.reshape(1, D))


def final_fwd(x, f, gpost, target, name):
    L, D = x.shape
    tr = min(TR, L)
    n = L // tr

    def body(x_ref, f_ref, gp_ref, t_ref, dy_ref, loss_ref, acc_ref):
        i = pl.program_id(0)

        @pl.when(i == 0)
        def _():
            acc_ref[...] = jnp.zeros_like(acc_ref)

        fv = f_ref[...].astype(F32)
        e = x_ref[...] + fv * _rms(fv) * gp_ref[...] - t_ref[...]
        dy_ref[...] = e * (1.0 / D)
        acc_ref[...] += jnp.sum(e * e, axis=0, keepdims=True)

        @pl.when(i == n - 1)
        def _():
            loss_ref[...] = jnp.full((1, LANE), 0.5 / D, F32) * jnp.sum(acc_ref[...])

    return pl.pallas_call(
        body, name=name, grid=(n,),
        in_specs=[_row_spec(tr, D), _row_spec(tr, D), _vec_spec(D), _row_spec(tr, D)],
        out_specs=[_row_spec(tr, D), _vec_spec(LANE)],
        out_shape=[jax.ShapeDtypeStruct((L, D), F32), jax.ShapeDtypeStruct((1, LANE), F32)],
        scratch_shapes=[pltpu.VMEM((1, D), F32)],
        compiler_params=_cp("arbitrary"),
    )(x, f, gpost.reshape(1, D), target)


def _rms_bwd(v, g, dy):
    r = _rms(v)
    vn = v * r
    dg = jnp.sum(dy * vn, axis=0, keepdims=True)
    dvn = dy * g
    dv = r * (dvn - vn * jnp.mean(dvn * vn, axis=-1, keepdims=True))
    return dv, dg


def bnd_bwd(dxp, *, pre=None, post=None, dep=None, name):
    L, D = dxp.shape
    tr = min(TR, L)
    has_pre, has_post = pre is not None, post is not None

    def body(*refs):
        pos = 0
        dxp_ref = refs[pos]; pos += 1
        if has_pre:
            x_ref, gpre_ref, dh_ref = refs[pos:pos + 3]; pos += 3
        if has_post:
            f_ref, gpost_ref = refs[pos:pos + 2]; pos += 2
        if dep is not None:
            pos += 1
        if has_pre:
            dx_ref, dgpre_ref = refs[pos:pos + 2]; pos += 2
        if has_post:
            df_ref, dgpost_ref = refs[pos:pos + 2]; pos += 2
        i = pl.program_id(0)
        dx = dxp_ref[...]
        if has_pre:
            d, dg = _rms_bwd(x_ref[...], gpre_ref[...], dh_ref[...].astype(F32))
            dx = dx + d
            dx_ref[...] = dx

            @pl.when(i == 0)
            def _():
                dgpre_ref[...] = jnp.zeros_like(dgpre_ref)

            dgpre_ref[...] += dg
        if has_post:
            d, dg = _rms_bwd(f_ref[...].astype(F32), gpost_ref[...], dx)
            df_ref[...] = d.astype(BF16)

            @pl.when(i == 0)
            def _():
                dgpost_ref[...] = jnp.zeros_like(dgpost_ref)

            dgpost_ref[...] += dg

    in_specs, args = [_row_spec(tr, D)], [dxp]
    out_specs, out_shape, names = [], [], []
    if has_pre:
        x, gpre, dh = pre
        in_specs += [_row_spec(tr, D), _vec_spec(D), _row_spec(tr, D)]
        args += [x, gpre.reshape(1, D), dh]
        out_specs += [_row_spec(tr, D), _vec_spec(D)]
        out_shape += [jax.ShapeDtypeStruct((L, D), F32), jax.ShapeDtypeStruct((1, D), F32)]
        names += ["dx", "dgpre"]
    if has_post:
        f, gpost = post
        in_specs += [_row_spec(tr, D), _vec_spec(D)]
        args += [f, gpost.reshape(1, D)]
        out_specs += [_row_spec(tr, D), _vec_spec(D)]
        out_shape += [jax.ShapeDtypeStruct((L, D), BF16), jax.ShapeDtypeStruct((1, D), F32)]
        names += ["df", "dgpost"]
    if dep is not None:
        in_specs.append(pl.BlockSpec(memory_space=pl.ANY))
        args.append(dep)
    outs = pl.pallas_call(
        body, name=name, grid=(L // tr,), in_specs=in_specs, out_specs=out_specs, out_shape=out_shape,
        compiler_params=_cp("arbitrary"),
    )(*args)
    return dict(zip(names, outs))


def norm_dg(x, dy, name):
    L, D = x.shape
    tr = min(TR, L)

    def body(x_ref, dy_ref, o_ref):
        @pl.when(pl.program_id(0) == 0)
        def _():
            o_ref[...] = jnp.zeros_like(o_ref)

        xv = x_ref[...]
        o_ref[...] += jnp.sum(dy_ref[...] * xv * _rms(xv), axis=0, keepdims=True)

    return pl.pallas_call(
        body, name=name, grid=(L // tr,),
        in_specs=[_row_spec(tr, D), _row_spec(tr, D)], out_specs=_vec_spec(D),
        out_shape=jax.ShapeDtypeStruct((1, D), F32), compiler_params=_cp("arbitrary"),
    )(x, dy)


HALO = 32


def _prev_halo_spec(tr, tc, col):
    per = tr // HALO
    return pl.BlockSpec((HALO, tc), lambda *g: (jnp.maximum(g[-1] * per - 1, 0), col(*g)))


def _fill_prev(scr, halo_val, blk_val, i, tr):
    scr[pl.ds(0, HALO), :] = jnp.where(i == 0, 0.0, halo_val)
    scr[pl.ds(HALO, tr), :] = blk_val


def _conv(scr, w_ref, K, tr):
    acc = None
    for k in range(K):
        term = scr[pl.ds(HALO - (K - 1) + k, tr), :] * w_ref[k:k + 1, :]
        acc = term if acc is None else acc + term
    return acc


def _shift_copies(scr, sh, rows):
    n = rows - 8
    for r in range(1, 8):
        sh[r - 1, pl.ds(0, n), :] = scr[pl.ds(r, n), :]


def _tap(scr, sh, off, tr):
    q, r = divmod(off, 8)
    return scr[pl.ds(off, tr), :] if r == 0 else sh[r - 1, pl.ds(8 * q, tr), :]


def ssm_conv_fwd(zx, w, b, *, col0, ncols, wcol0, out_dtype, name):
    L = zx.shape[0]
    tr = min(TR, L)
    tc = 1024
    cb, wb = col0 // tc, wcol0 // tc

    def body(x_ref, h_ref, w_ref, b_ref, o_ref, p_ref, scr):
        i = pl.program_id(1)
        _fill_prev(scr, h_ref[...].astype(F32), x_ref[...].astype(F32), i, tr)
        pre = _conv(scr, w_ref, SSM_K, tr) + b_ref[...]
        p_ref[...] = pre.astype(BF16)
        o_ref[...] = (pre * _sig(pre)).astype(out_dtype)

    out = pl.BlockSpec((tr, tc), lambda j, i: (i, j))
    return pl.pallas_call(
        body, name=name, grid=(ncols // tc, L // tr),
        in_specs=[pl.BlockSpec((tr, tc), lambda j, i: (i, j + cb)),
                  _prev_halo_spec(tr, tc, lambda j, i: j + cb),
                  pl.BlockSpec((SSM_K, tc), lambda j, i: (0, j + wb)),
                  pl.BlockSpec((1, tc), lambda j, i: (0, j + wb))],
        out_specs=[out, out],
        out_shape=[jax.ShapeDtypeStruct((L, ncols), out_dtype), jax.ShapeDtypeStruct((L, ncols), BF16)],
        scratch_shapes=[pltpu.VMEM((HALO + tr, tc), F32)],
        compiler_params=_cp("parallel", "parallel"),
    )(zx, zx, w, b)


def ssm_conv_bwd(zx, pre, d, w, *, col0, dcol0, ncols, name):
    L = zx.shape[0]
    tr = min(TR, L)
    tc = 1024
    cb, db_ = col0 // tc, dcol0 // tc
    n = L // tr
    per = tr // HALO
    last = L // HALO - 1

    def body(x_ref, p_ref, np_ref, d_ref, nd_ref, w_ref, dx_ref, dw_ref, db_ref, sd):
        i = pl.program_id(1)

        def dpre(p, dv):
            s = _sig(p)
            return dv * s * (1.0 + p * (1.0 - s))

        dp = dpre(p_ref[...].astype(F32), d_ref[...])
        sd[pl.ds(0, tr), :] = dp
        sd[pl.ds(tr, HALO), :] = jnp.where(i == n - 1, 0.0, dpre(np_ref[...].astype(F32), nd_ref[...]))

        @pl.when(i == 0)
        def _():
            dw_ref[...] = jnp.zeros_like(dw_ref)
            db_ref[...] = jnp.zeros_like(db_ref)

        xv = x_ref[...].astype(F32)
        acc = None
        for k in range(SSM_K):
            tk = sd[pl.ds(SSM_K - 1 - k, tr), :]
            term = tk * w_ref[k:k + 1, :]
            acc = term if acc is None else acc + term
            dw_ref[k:k + 1, :] += jnp.sum(xv * tk, axis=0, keepdims=True)
        dx_ref[...] = acc.astype(BF16)
        db_ref[...] += jnp.sum(dp, axis=0, keepdims=True)

    nxt = lambda i: jnp.minimum((i + 1) * per, last)
    return pl.pallas_call(
        body, name=name, grid=(ncols // tc, n),
        in_specs=[pl.BlockSpec((tr, tc), lambda j, i: (i, j + cb)),
                  pl.BlockSpec((tr, tc), lambda j, i: (i, j)),
                  pl.BlockSpec((HALO, tc), lambda j, i: (nxt(i), j)),
                  pl.BlockSpec((tr, tc), lambda j, i: (i, j + db_)),
                  pl.BlockSpec((HALO, tc), lambda j, i: (nxt(i), j + db_)),
                  pl.BlockSpec((SSM_K, tc), lambda j, i: (0, j + db_))],
        out_specs=[pl.BlockSpec((tr, tc), lambda j, i: (i, j)),
                   pl.BlockSpec((SSM_K, tc), lambda j, i: (0, j)),
                   pl.BlockSpec((1, tc), lambda j, i: (0, j))],
        out_shape=[jax.ShapeDtypeStruct((L, ncols), BF16), jax.ShapeDtypeStruct((SSM_K, ncols), F32),
                   jax.ShapeDtypeStruct((1, ncols), F32)],
        scratch_shapes=[pltpu.VMEM((tr + HALO, tc), F32)],
        compiler_params=_cp("parallel", "arbitrary"),
    )(zx, pre, pre, d, d, w)


FFN_TC = 1408
FFN_TR = 512


def ffn_act_fwd(u, w, b, name):
    L = u.shape[0]
    tr = min(FFN_TR, L)
    tc = FFN_TC
    nb = D_FF // tc

    def body(g_ref, hg_ref, v_ref, hv_ref, wg_ref, wv_ref, bg_ref, bv_ref, o_ref, c_ref, sg, sv):
        i = pl.program_id(1)
        _fill_prev(sg, hg_ref[...].astype(F32), g_ref[...].astype(F32), i, tr)
        _fill_prev(sv, hv_ref[...].astype(F32), v_ref[...].astype(F32), i, tr)
        ug = _conv(sg, wg_ref, FFN_K, tr) + bg_ref[...]
        uv = _conv(sv, wv_ref, FFN_K, tr) + bv_ref[...]
        c_ref[0] = ug.astype(BF16)
        c_ref[1] = uv.astype(BF16)
        o_ref[...] = (ug * _sig(ug) * uv).astype(BF16)

    blk = lambda off: pl.BlockSpec((tr, tc), lambda j, i: (i, j + off))
    wsp = lambda off: pl.BlockSpec((FFN_K, tc), lambda j, i: (0, j + off))
    bsp = lambda off: pl.BlockSpec((1, tc), lambda j, i: (0, j + off))
    return pl.pallas_call(
        body, name=name, grid=(nb, L // tr),
        in_specs=[blk(0), _prev_halo_spec(tr, tc, lambda j, i: j),
                  blk(nb), _prev_halo_spec(tr, tc, lambda j, i: j + nb),
                  wsp(0), wsp(nb), bsp(0), bsp(nb)],
        out_specs=[pl.BlockSpec((tr, tc), lambda j, i: (i, j)), pl.BlockSpec((2, tr, tc), lambda j, i: (0, i, j))],
        out_shape=[jax.ShapeDtypeStruct((L, D_FF), BF16), jax.ShapeDtypeStruct((2, L, D_FF), BF16)],
        scratch_shapes=[pltpu.VMEM((HALO + tr, tc), F32), pltpu.VMEM((HALO + tr, tc), F32)],
        compiler_params=_cp("parallel", "parallel"),
    )(u, u, u, u, w, w, b, b)


def ffn_act_bwd(u, c, dact, w, name):
    L = u.shape[0]
    tr = min(FFN_TR, L)
    tc = FFN_TC
    nb = D_FF // tc
    n = L // tr
    per = tr // HALO
    last = L // HALO - 1

    def body(g_ref, v_ref, c_ref, nc_ref, da_ref, nda_ref, wg_ref, wv_ref, du_ref, dw_ref, db_ref, dg_s, dv_s):
        i = pl.program_id(1)

        def grads(cg, cv, da):
            s = _sig(cg)
            return da * cv * s * (1.0 + cg * (1.0 - s)), da * cg * s

        dg, dv = grads(c_ref[0].astype(F32), c_ref[1].astype(F32), da_ref[...].astype(F32))
        ndg, ndv = grads(nc_ref[0].astype(F32), nc_ref[1].astype(F32), nda_ref[...].astype(F32))
        at_end = i == n - 1
        for half, (scr, d, nd, x_ref, w_ref) in enumerate(((dg_s, dg, ndg, g_ref, wg_ref), (dv_s, dv, ndv, v_ref, wv_ref))):
            scr[pl.ds(0, tr), :] = d
            scr[pl.ds(tr, HALO), :] = jnp.where(at_end, 0.0, nd)

            @pl.when(i == 0)
            def _():
                dw_ref[half] = jnp.zeros((FFN_K, tc), F32)
                db_ref[half] = jnp.zeros((1, tc), F32)

            xv = x_ref[...].astype(F32)
            acc = None
            for k in range(FFN_K):
                tk = scr[pl.ds(FFN_K - 1 - k, tr), :]
                term = tk * w_ref[k:k + 1, :]
                acc = term if acc is None else acc + term
                dw_ref[half, k:k + 1, :] += jnp.sum(xv * tk, axis=0, keepdims=True)
            du_ref[half] = acc.astype(BF16)
            db_ref[half] += jnp.sum(d, axis=0, keepdims=True)

    blk = lambda off: pl.BlockSpec((tr, tc), lambda j, i: (i, j + off))
    wsp = lambda off: pl.BlockSpec((FFN_K, tc), lambda j, i: (0, j + off))
    nxt = lambda i: jnp.minimum((i + 1) * per, last)
    return pl.pallas_call(
        body, name=name, grid=(nb, n),
        in_specs=[blk(0), blk(nb),
                  pl.BlockSpec((2, tr, tc), lambda j, i: (0, i, j)),
                  pl.BlockSpec((2, HALO, tc), lambda j, i: (0, nxt(i), j)),
                  pl.BlockSpec((tr, tc), lambda j, i: (i, j)),
                  pl.BlockSpec((HALO, tc), lambda j, i: (nxt(i), j)),
                  wsp(0), wsp(nb)],
        out_specs=[pl.BlockSpec((2, tr, tc), lambda j, i: (0, i, j)),
                   pl.BlockSpec((2, FFN_K, tc), lambda j, i: (0, 0, j)),
                   pl.BlockSpec((2, 1, tc), lambda j, i: (0, 0, j))],
        out_shape=[jax.ShapeDtypeStruct((2, L, D_FF), BF16), jax.ShapeDtypeStruct((2, FFN_K, D_FF), F32),
                   jax.ShapeDtypeStruct((2, 1, D_FF), F32)],
        scratch_shapes=[pltpu.VMEM((tr + HALO, tc), F32), pltpu.VMEM((tr + HALO, tc), F32)],
        compiler_params=_cp("parallel", "arbitrary"),
    )(u, u, c, c, dact, dact, w, w)


def _ln_stats(c):
    mu = jnp.mean(c, axis=-1, keepdims=True)
    cc = c - mu
    rstd = lax.rsqrt(jnp.mean(cc * cc, axis=-1, keepdims=True) + EPS)
    return cc * rstd, rstd


def cf_fwd(u, dw_w, dw_b, ln_g, ln_b, name):
    L = u.shape[0]
    D = D_MODEL
    tr = min(TR, L)

    def body(a_ref, ha_ref, g_ref, hg_ref, w_ref, b_ref, lg_ref, lb_ref, c_ref, s_ref, scr, sh):
        i = pl.program_id(0)
        glu_h = ha_ref[...].astype(F32) * _sig(hg_ref[...].astype(F32))
        glu = a_ref[...].astype(F32) * _sig(g_ref[...].astype(F32))
        _fill_prev(scr, glu_h, glu, i, tr)
        _shift_copies(scr, sh, HALO + tr)
        c = b_ref[...]
        for k in range(CF_K):
            c = c + _tap(scr, sh, HALO - (CF_K - 1) + k, tr) * w_ref[k:k + 1, :]
        c_ref[...] = c
        xhat, _ = _ln_stats(c)
        ln = xhat * lg_ref[...] + lb_ref[...]
        s_ref[...] = (ln * _sig(ln)).astype(BF16)

    per = tr // HALO
    halo = lambda col: pl.BlockSpec((HALO, D), lambda i: (jnp.maximum(i * per - 1, 0), col))
    return pl.pallas_call(
        body, name=name, grid=(L // tr,),
        in_specs=[pl.BlockSpec((tr, D), lambda i: (i, 0)), halo(0),
                  pl.BlockSpec((tr, D), lambda i: (i, 1)), halo(1),
                  pl.BlockSpec((CF_K, D), lambda i: (0, 0)), _vec_spec(D), _vec_spec(D), _vec_spec(D)],
        out_specs=[_row_spec(tr, D), _row_spec(tr, D)],
        out_shape=[jax.ShapeDtypeStruct((L, D), F32), jax.ShapeDtypeStruct((L, D), BF16)],
        scratch_shapes=[pltpu.VMEM((HALO + tr, D), F32), pltpu.VMEM((7, HALO + tr, D), F32)],
        compiler_params=_cp("parallel"),
    )(u, u, u, u, dw_w, dw_b, ln_g, ln_b)


def cf_bwd_ln(c, ds, ln_g, ln_b, name):
    L, D = c.shape
    tr = min(TR, L)

    def body(c_ref, ds_ref, lg_ref, lb_ref, dc_ref, dg_ref, db_ref):
        xhat, rstd = _ln_stats(c_ref[...])
        ln = xhat * lg_ref[...] + lb_ref[...]
        sg = _sig(ln)
        dln = ds_ref[...].astype(F32) * sg * (1.0 + ln * (1.0 - sg))

        @pl.when(pl.program_id(0) == 0)
        def _():
            dg_ref[...] = jnp.zeros_like(dg_ref)
            db_ref[...] = jnp.zeros_like(db_ref)

        dg_ref[...] += jnp.sum(dln * xhat, axis=0, keepdims=True)
        db_ref[...] += jnp.sum(dln, axis=0, keepdims=True)
        dxh = dln * lg_ref[...]
        dc_ref[...] = rstd * (dxh - jnp.mean(dxh, axis=-1, keepdims=True)
                              - xhat * jnp.mean(dxh * xhat, axis=-1, keepdims=True))

    return pl.pallas_call(
        body, name=name, grid=(L // tr,),
        in_specs=[_row_spec(tr, D), _row_spec(tr, D), _vec_spec(D), _vec_spec(D)],
        out_specs=[_row_spec(tr, D), _vec_spec(D), _vec_spec(D)],
        out_shape=[jax.ShapeDtypeStruct((L, D), F32), jax.ShapeDtypeStruct((1, D), F32),
                   jax.ShapeDtypeStruct((1, D), F32)],
        compiler_params=_cp("arbitrary"),
    )(c, ds, ln_g, ln_b)


def cf_bwd_conv(u, dc, dw_w, name):
    L = u.shape[0]
    D = D_MODEL
    tr = min(TR, L)
    n = L // tr

    def body(a_ref, g_ref, dc_ref, nx_ref, w_ref, du_ref, dw_ref, db_ref, sd, shd):
        i = pl.program_id(0)
        a = a_ref[...].astype(F32)
        sg = _sig(g_ref[...].astype(F32))
        glu = a * sg
        dcv = dc_ref[...]
        sd[pl.ds(0, tr), :] = dcv
        sd[pl.ds(tr, HALO), :] = jnp.where(i == n - 1, 0.0, nx_ref[...])
        _shift_copies(sd, shd, tr + HALO)

        @pl.when(i == 0)
        def _():
            dw_ref[...] = jnp.zeros_like(dw_ref)
            db_ref[...] = jnp.zeros_like(db_ref)

        dglu = None
        for k in range(CF_K):
            tk = _tap(sd, shd, CF_K - 1 - k, tr)
            term = tk * w_ref[k:k + 1, :]
            dglu = term if dglu is None else dglu + term
            dw_ref[k:k + 1, :] += jnp.sum(glu * tk, axis=0, keepdims=True)
        du_ref[:, 0:D] = (dglu * sg).astype(BF16)
        du_ref[:, D:2 * D] = (dglu * a * sg * (1.0 - sg)).astype(BF16)
        db_ref[...] += jnp.sum(dcv, axis=0, keepdims=True)

    per = tr // HALO
    last = L // HALO - 1
    return pl.pallas_call(
        body, name=name, grid=(n,),
        in_specs=[pl.BlockSpec((tr, D), lambda i: (i, 0)),
                  pl.BlockSpec((tr, D), lambda i: (i, 1)),
                  _row_spec(tr, D),
                  pl.BlockSpec((HALO, D), lambda i: (jnp.minimum((i + 1) * per, last), 0)),
                  pl.BlockSpec((CF_K, D), lambda i: (0, 0))],
        out_specs=[pl.BlockSpec((tr, 2 * D), lambda i: (i, 0)),
                   pl.BlockSpec((CF_K, D), lambda i: (0, 0)), _vec_spec(D)],
        out_shape=[jax.ShapeDtypeStruct((L, 2 * D), BF16), jax.ShapeDtypeStruct((CF_K, D), F32),
                   jax.ShapeDtypeStruct((1, D), F32)],
        scratch_shapes=[pltpu.VMEM((tr + HALO, D), F32), pltpu.VMEM((7, tr + HALO, D), F32)],
        compiler_params=_cp("arbitrary"),
    )(u, u, dc, dc, dw_w)


XA_TR = 512
XA_SCALE = XA_HD ** -0.5


def _xa_probs(qh, kh):
    s = _dot(qh, kh, 1, 1) * XA_SCALE
    p = jnp.exp(s - jnp.max(s, axis=-1, keepdims=True))
    return p / jnp.sum(p, axis=-1, keepdims=True)


def attn_fwd(q, kv, name):
    L, D = q.shape
    tr = min(XA_TR, L)

    def body(q_ref, kv_ref, o_ref):
        for hd in range(XA_HEADS):
            c = slice(hd * XA_HD, (hd + 1) * XA_HD)
            p = _xa_probs(q_ref[:, c], kv_ref[:, c])
            vh = kv_ref[:, D + hd * XA_HD:D + (hd + 1) * XA_HD]
            o_ref[:, c] = _dot(p.astype(BF16), vh).astype(BF16)

    return pl.pallas_call(
        body, name=name, grid=(L // tr,),
        in_specs=[_row_spec(tr, D), pl.BlockSpec((N_MEM, 2 * D), lambda i: (0, 0))],
        out_specs=_row_spec(tr, D), out_shape=jax.ShapeDtypeStruct((L, D), BF16),
        compiler_params=_cp("parallel"),
    )(q, kv)


def attn_bwd(q, kv, do, name):
    L, D = q.shape
    tr = min(XA_TR, L)

    def body(q_ref, kv_ref, do_ref, dq_ref, dkv_ref):
        @pl.when(pl.program_id(0) == 0)
        def _():
            dkv_ref[...] = jnp.zeros_like(dkv_ref)

        for hd in range(XA_HEADS):
            c = slice(hd * XA_HD, (hd + 1) * XA_HD)
            cv = slice(D + hd * XA_HD, D + (hd + 1) * XA_HD)
            qh, kh, vh, doh = q_ref[:, c], kv_ref[:, c], kv_ref[:, cv], do_ref[:, c]
            p = _xa_probs(qh, kh)
            dp = _dot(doh, vh, 1, 1)
            dkv_ref[:, cv] += _dot(p.astype(BF16), doh, 0, 0)
            ds = (p * (dp - jnp.sum(dp * p, axis=-1, keepdims=True)) * XA_SCALE).astype(BF16)
            dq_ref[:, c] = _dot(ds, kh).astype(BF16)
            dkv_ref[:, c] += _dot(ds, qh, 0, 0)

    return pl.pallas_call(
        body, name=name, grid=(L // tr,),
        in_specs=[_row_spec(tr, D), pl.BlockSpec((N_MEM, 2 * D), lambda i: (0, 0)), _row_spec(tr, D)],
        out_specs=[_row_spec(tr, D), pl.BlockSpec((N_MEM, 2 * D), lambda i: (0, 0))],
        out_shape=[jax.ShapeDtypeStruct((L, D), BF16), jax.ShapeDtypeStruct((N_MEM, 2 * D), F32)],
        compiler_params=_cp("arbitrary"),
    )(q, kv, do)


N_PAIRS = N_HEADS // 2
PAIRS_PER_GROUP = N_PAIRS // N_GROUPS
GN = N_GROUPS * D_STATE


def _softplus(x):
    t = jnp.exp(-jnp.abs(x))
    return jnp.maximum(x, 0.0) + jnp.where(t < 1e-4, t * (1.0 - 0.5 * t), jnp.log(1.0 + t))


def _dot3b(m, v, ca=1, cb=0):
    v0, v1, v2 = _split3(v)
    return _dot(m, v0, ca, cb) + _dot(m, v1, ca, cb) + _dot(m, v2, ca, cb)


def ssd_consts():
    h = lax.broadcasted_iota(jnp.int32, (LANE, D_INNER), 0)
    c = lax.broadcasted_iota(jnp.int32, (LANE, D_INNER), 1)
    expand = (c // HEAD_DIM == h).astype(BF16)
    r = lax.broadcasted_iota(jnp.int32, (CHUNK, CHUNK), 0)
    k = lax.broadcasted_iota(jnp.int32, (CHUNK, CHUNK), 1)
    tri = (k <= r).astype(BF16)
    return expand, tri


def _ssd_common(dtr_ref, prm_ref, e_ref, tri_ref):
    lane = lax.broadcasted_iota(jnp.int32, (CHUNK, LANE), 1)
    valid = lane < N_HEADS
    A = -jnp.exp(prm_ref[1:2, :])
    pre = dtr_ref[...] + prm_ref[0:1, :]
    dt = jnp.where(valid, _softplus(pre), 0.0)
    cs = _dot3b(tri_ref[...], dt * A)
    E = e_ref[...]
    dt_x = _dot2(dt, E)
    cs_x = _dot3(cs, E)
    csl_x = cs_x[CHUNK - 1:CHUNK, :]
    return dict(valid=valid, A=A, pre=pre, dt=dt, cs=cs, csT=cs.T, dt_x=dt_x, ecs_x=jnp.exp(cs_x),
                dend_x=jnp.exp(csl_x - cs_x), cd_x=jnp.exp(csl_x), D_x=_dot3(prm_ref[...], E)[2:3, :])


def ssd_fwd(xs, bc, dtr, zx, prm, ng, name):
    L = xs.shape[0]
    nc = L // CHUNK
    expand, tri = ssd_consts()

    def body(xs_ref, bc_ref, dtr_ref, z_ref, prm_ref, ng_ref, e_ref, tri_ref, y_ref, yn_ref, st_ref, state):
        @pl.when(pl.program_id(0) == 0)
        def _():
            state[...] = jnp.zeros_like(state)

        q = _ssd_common(dtr_ref, prm_ref, e_ref, tri_ref)
        cs, csT = q["cs"], q["csT"]
        xs_v = xs_ref[...]
        X = xs_v * q["dt_x"]
        Xb = X.astype(BF16)
        Xd = (X * q["dend_x"]).astype(BF16)
        ii = lax.broadcasted_iota(jnp.int32, (CHUNK, CHUNK), 0)
        jj = lax.broadcasted_iota(jnp.int32, (CHUNK, CHUNK), 1)
        tril = jj <= ii
        first = jj < HEAD_DIM
        for g in range(N_GROUPS):
            Bg = bc_ref[:, g * D_STATE:(g + 1) * D_STATE]
            Cg = bc_ref[:, GN + g * D_STATE:GN + (g + 1) * D_STATE]
            S = _dot(Cg, Bg, 1, 1)
            for pr in range(PAIRS_PER_GROUP):
                pair = g * PAIRS_PER_GROUP + pr
                cols = slice(pair * LANE, (pair + 1) * LANE)
                Xp = Xb[:, cols]
                ys = []
                for h in (2 * pair, 2 * pair + 1):
                    seg = cs[:, h:h + 1] - csT[h:h + 1, :]
                    M = (S * jnp.exp(jnp.where(tril, seg, NEG))).astype(BF16)
                    ys.append(_dot(M, Xp))
                prevT = state[pair]
                st_ref[0, pair] = prevT
                yoff = _dot(Cg, prevT.astype(BF16)) * q["ecs_x"][:, cols]
                y_ref[:, cols] = jnp.where(first, ys[0], ys[1]) + yoff + xs_v[:, cols] * q["D_x"][:, cols]
                state[pair] = prevT * q["cd_x"][:, cols] + _dot(Bg, Xd[:, cols], 0, 0)
        z = z_ref[...].astype(F32)
        gt = y_ref[...] * z * _sig(z)
        yn_ref[...] = (gt * _rms(gt) * ng_ref[...]).astype(BF16)

    row = lambda w: pl.BlockSpec((CHUNK, w), lambda c: (c, 0))
    const = lambda a: pl.BlockSpec(a.shape, lambda c: (0,) * a.ndim)
    return pl.pallas_call(
        body, name=name, grid=(nc,),
        in_specs=[row(D_INNER), row(2 * GN), row(LANE), row(D_INNER), const(prm), const(ng), const(expand), const(tri)],
        out_specs=[row(D_INNER), row(D_INNER), pl.BlockSpec((1, N_PAIRS, D_STATE, LANE), lambda c: (c, 0, 0, 0))],
        out_shape=[jax.ShapeDtypeStruct((L, D_INNER), F32), jax.ShapeDtypeStruct((L, D_INNER), BF16),
                   jax.ShapeDtypeStruct((nc, N_PAIRS, D_STATE, LANE), F32)],
        scratch_shapes=[pltpu.VMEM((N_PAIRS, D_STATE, LANE), F32)],
        compiler_params=_cp("arbitrary"),
    )(xs, bc, dtr, zx, prm, ng, expand, tri)


def ssd_bwd(dyn, y, zx, xs, bc, dtr, st, prm, ng, name):
    L = xs.shape[0]
    nc = L // CHUNK
    expand, tri = ssd_consts()

    def body(dyn_ref, y_ref, z_ref, xs_ref, bc_ref, dtr_ref, st_ref, prm_ref, ng_ref, e_ref, tri_ref,
             dxbc_ref, dz_ref, ddtr_ref, dng_ref, dprm_ref, dstate, g_cs, g_q, dX, g_row):
        step = pl.program_id(0)

        @pl.when(step == 0)
        def _():
            dstate[...] = jnp.zeros_like(dstate)
            dng_ref[...] = jnp.zeros_like(dng_ref)
            dprm_ref[...] = jnp.zeros_like(dprm_ref)
            g_row[...] = jnp.zeros_like(g_row)

        q = _ssd_common(dtr_ref, prm_ref, e_ref, tri_ref)
        cs, csT, E = q["cs"], q["csT"], e_ref[...]
        xs_v = xs_ref[...]
        X = xs_v * q["dt_x"]
        Xb = X.astype(BF16)
        Xd_f = X * q["dend_x"]
        Xd = Xd_f.astype(BF16)

        yv = y_ref[...]
        z = z_ref[...].astype(F32)
        sz = _sig(z)
        silu = z * sz
        gt = yv * silu
        r = _rms(gt)
        gn = gt * r
        dyn_v = dyn_ref[...]
        dng_ref[...] += jnp.sum(dyn_v * gn, axis=0, keepdims=True)
        dgn = dyn_v * ng_ref[...]
        dgt = r * (dgn - gn * jnp.mean(dgn * gn, axis=-1, keepdims=True))
        dY = dgt * silu
        dz_ref[...] = (dgt * yv * sz * (1.0 + z * (1.0 - sz))).astype(BF16)
        dYb = dY.astype(BF16)
        g_row[1:2, :] += jnp.sum(dY * xs_v, axis=0, keepdims=True)

        ii = lax.broadcasted_iota(jnp.int32, (CHUNK, CHUNK), 0)
        jj = lax.broadcasted_iota(jnp.int32, (CHUNK, CHUNK), 1)
        tril = jj <= ii
        triu = jj >= ii
        first = jj < HEAD_DIM
        lane_row = lax.broadcasted_iota(jnp.int32, (1, LANE), 1)
        sub_col = lax.broadcasted_iota(jnp.int32, (CHUNK, 1), 0)
        dcs_col = jnp.zeros((CHUNK, LANE), F32)
        dcs_rowT = jnp.zeros((LANE, CHUNK), F32)
        for g in range(N_GROUPS):
            Bg = bc_ref[:, g * D_STATE:(g + 1) * D_STATE]
            Cg = bc_ref[:, GN + g * D_STATE:GN + (g + 1) * D_STATE]
            S = _dot(Cg, Bg, 1, 1)
            ST = _dot(Bg, Cg, 1, 1)
            dS = jnp.zeros((CHUNK, CHUNK), F32)
            dCg = jnp.zeros((CHUNK, D_STATE), F32)
            dBg = jnp.zeros((CHUNK, D_STATE), F32)
            for pr in range(PAIRS_PER_GROUP):
                pair = g * PAIRS_PER_GROUP + pr
                cols = slice(pair * LANE, (pair + 1) * LANE)
                Xp = Xb[:, cols]
                dYp_f = dY[:, cols]
                dYp = dYb[:, cols]
                prevT = st_ref[0, pair]
                prevTb = prevT.astype(BF16)
                dst = dstate[pair]
                dstb = dst.astype(BF16)
                ecs_p = q["ecs_x"][:, cols]
                g_cs[:, cols] = dYp_f * (_dot(Cg, prevTb) * ecs_p)
                dWb = (dYp_f * ecs_p).astype(BF16)
                dprev = dst * q["cd_x"][:, cols] + _dot(Cg, dWb, 0, 0)
                dCg = dCg + _dot(dWb, prevTb, 1, 1)
                g_row[0:1, cols] = jnp.sum(dst * prevT, axis=0, keepdims=True)
                dXp = None
                for hh, h in enumerate((2 * pair, 2 * pair + 1)):
                    mine = first if hh == 0 else jnp.logical_not(first)
                    seg = cs[:, h:h + 1] - csT[h:h + 1, :]
                    lam = jnp.exp(jnp.where(tril, seg, NEG))
                    dM = _dot(jnp.where(mine, dYp, jnp.zeros_like(dYp)), Xp, 1, 1)
                    dS = dS + dM * lam
                    Gm = dM * (S * lam)
                    dcs_col = dcs_col + jnp.sum(Gm, axis=1, keepdims=True) * (lane_row == h).astype(F32)
                    dcs_rowT = dcs_rowT + (sub_col == h).astype(F32) * jnp.sum(Gm, axis=0, keepdims=True)
                    MT = (ST * jnp.exp(jnp.where(triu, -seg, NEG))).astype(BF16)
                    t = _dot(MT, dYp)
                    dXp = t if dXp is None else jnp.where(first, dXp, t)
                dXd = _dot(Bg, dstb)
                dBg = dBg + _dot(Xd[:, cols], dstb, 1, 1)
                g_q[:, cols] = dXd * Xd_f[:, cols]
                dX[:, cols] = dXp + dXd * q["dend_x"][:, cols]
                dstate[pair] = dprev
            dSb = dS.astype(BF16)
            dxbc_ref[:, D_INNER + g * D_STATE:D_INNER + (g + 1) * D_STATE] = dBg + _dot(dSb, Cg, 0, 0)
            dxbc_ref[:, D_INNER + GN + g * D_STATE:D_INNER + GN + (g + 1) * D_STATE] = dCg + _dot(dSb, Bg)
        dXv = dX[...]
        dxbc_ref[:, 0:D_INNER] = q["D_x"] * dY + dXv * q["dt_x"]
        r_dt = _dot2(dXv * xs_v, E, 1, 1)
        r_cs = _dot2(g_cs[...], E, 1, 1)
        r_q = _dot2(g_q[...], E, 1, 1)
        r_row = _dot2(g_row[...], E, 1, 1)
        cd = jnp.exp(cs[CHUNK - 1:CHUNK, :])
        dcs_last = jnp.sum(r_q, axis=0, keepdims=True) + r_row[0:1, :] * cd
        dcs = r_cs - r_q + dcs_col - dcs_rowT.T + jnp.where(sub_col == CHUNK - 1, dcs_last, 0.0)
        da = _dot3b(tri_ref[...], dcs, 0, 0)
        dpre = jnp.where(q["valid"], (r_dt + da * q["A"]) * _sig(q["pre"]), 0.0)
        ddtr_ref[...] = dpre
        dprm_ref[0:1, :] += jnp.sum(dpre, axis=0, keepdims=True)
        dprm_ref[1:2, :] += jnp.sum(da * q["dt"], axis=0, keepdims=True) * q["A"]
        dprm_ref[2:3, :] = r_row[1:2, :]

    rev = lambda w: pl.BlockSpec((CHUNK, w), lambda c: (nc - 1 - c, 0))
    const = lambda a: pl.BlockSpec(a.shape, lambda c: (0,) * a.ndim)
    return pl.pallas_call(
        body, name=name, grid=(nc,),
        in_specs=[rev(D_INNER), rev(D_INNER), rev(D_INNER), rev(D_INNER), rev(2 * GN), rev(LANE),
                  pl.BlockSpec((1, N_PAIRS, D_STATE, LANE), lambda c: (nc - 1 - c, 0, 0, 0)),
                  const(prm), const(ng), const(expand), const(tri)],
        out_specs=[rev(CONV_DIM), rev(D_INNER), rev(LANE),
                   pl.BlockSpec((1, D_INNER), lambda c: (0, 0)), pl.BlockSpec((8, LANE), lambda c: (0, 0))],
        out_shape=[jax.ShapeDtypeStruct((L, CONV_DIM), F32), jax.ShapeDtypeStruct((L, D_INNER), BF16),
                   jax.ShapeDtypeStruct((L, LANE), F32), jax.ShapeDtypeStruct((1, D_INNER), F32),
                   jax.ShapeDtypeStruct((8, LANE), F32)],
        scratch_shapes=[pltpu.VMEM((N_PAIRS, D_STATE, LANE), F32), pltpu.VMEM((CHUNK, D_INNER), F32),
                        pltpu.VMEM((CHUNK, D_INNER), F32), pltpu.VMEM((CHUNK, D_INNER), F32),
                        pltpu.VMEM((8, D_INNER), F32)],
        compiler_params=_cp("arbitrary"),
    )(dyn, y, zx, xs, bc, dtr, st, prm, ng, expand, tri)


def _ssd_weights(W, j):
    w_in = W["ssm_in_w"][j]
    nzx = D_INNER + CONV_DIM
    wdt = jnp.pad(w_in[:, nzx:], ((0, 0), (0, LANE - N_HEADS)))
    prm = jnp.zeros((8, LANE), F32)
    prm = prm.at[0, :N_HEADS].set(W["ssm_dt_bias"][j]).at[1, :N_HEADS].set(W["ssm_A_log"][j])
    prm = prm.at[2, :N_HEADS].set(W["ssm_D"][j])
    return dict(wdt=wdt, cw=W["ssm_conv_w"][j], cb=W["ssm_conv_b"][j].reshape(1, CONV_DIM), prm=prm,
                ng=W["ssm_norm_g"][j].reshape(1, D_INNER))


def ssd_layer_fwd(h, W, j, tag):
    p = _ssd_weights(W, j)
    zx = mm(h, W["ssm_in_w"], layer=j, b_n=D_INNER + CONV_DIM, out_dtype=BF16, name=f"{tag}_zx")
    dtr = mm(h, p["wdt"], name=f"{tag}_dt")
    xs, pre_x = ssm_conv_fwd(zx, p["cw"], p["cb"], col0=D_INNER, ncols=D_INNER, wcol0=0, out_dtype=F32,
                             name=f"{tag}_convx")
    bc, pre_bc = ssm_conv_fwd(zx, p["cw"], p["cb"], col0=2 * D_INNER, ncols=2 * GN, wcol0=D_INNER, out_dtype=BF16,
                              name=f"{tag}_convbc")
    y, yn, st = ssd_fwd(xs, bc, dtr, zx, p["prm"], p["ng"], name=f"{tag}_scan")
    f = mm(yn, W["ssm_out_w"], layer=j, out_dtype=BF16, name=f"{tag}_out")
    return f, dict(h=h, zx=zx, dtr=dtr, xs=xs, bc=bc, pre_x=pre_x, pre_bc=pre_bc, y=y, yn=yn, st=st, p=p)


def ssd_layer_bwd(df, ctx, W, GB, j, tag, done=None):
    p = ctx["p"]
    h = ctx["h"]
    dyn = mm(df, W["ssm_out_w"], layer=j, tb=True, name=f"{tag}_b_dyn")
    GB["ssm_out_w"] = mm(ctx["yn"], df, ta=True, into=(GB["ssm_out_w"], j, 0), name=f"{tag}_b_gwo")
    dxbc, dz, ddtr, dng, dprm = ssd_bwd(dyn, ctx["y"], ctx["zx"], ctx["xs"], ctx["bc"], ctx["dtr"], ctx["st"],
                                        p["prm"], p["ng"], name=f"{tag}_b_scan")
    dx1, dcw1, dcb1 = ssm_conv_bwd(ctx["zx"], ctx["pre_x"], dxbc, p["cw"], col0=D_INNER, dcol0=0, ncols=D_INNER,
                                   name=f"{tag}_b_convx")
    dx2, dcw2, dcb2 = ssm_conv_bwd(ctx["zx"], ctx["pre_bc"], dxbc, p["cw"], col0=2 * D_INNER, dcol0=D_INNER,
                                   ncols=2 * GN, name=f"{tag}_b_convbc")
    g_in = jnp.concatenate([mm(h, dz, ta=True, out_dtype=BF16, name=f"{tag}_b_gz"),
                            mm(h, dx1, ta=True, out_dtype=BF16, name=f"{tag}_b_gx"),
                            mm(h, dx2, ta=True, out_dtype=BF16, name=f"{tag}_b_gbc"),
                            mm(h, ddtr, ta=True, out_dtype=BF16, name=f"{tag}_b_gdt")[:, :N_HEADS]], axis=1)
    gw = dict(ssm_conv_w=jnp.concatenate([dcw1, dcw2], axis=1), ssm_conv_b=jnp.concatenate([dcb1, dcb2], axis=1)[0],
              ssm_dt_bias=dprm[0, :N_HEADS], ssm_A_log=dprm[1, :N_HEADS], ssm_D=dprm[2, :N_HEADS], ssm_norm_g=dng[0])
    if done is None:
        gw["ssm_in_w"] = g_in
        token = None
    else:
        token = done(g_in)
    dh = mm(dz, W["ssm_in_w"], layer=j, tb=True, b_k0=0, dep=token, name=f"{tag}_b_dh1")
    dh = mm(dx1, W["ssm_in_w"], layer=j, tb=True, b_k0=D_INNER, acc=dh, name=f"{tag}_b_dh2")
    dh = mm(dx2, W["ssm_in_w"], layer=j, tb=True, b_k0=2 * D_INNER, acc=dh, name=f"{tag}_b_dh3")
    dh = mm(ddtr, p["wdt"], tb=True, acc=dh, out_dtype=BF16, name=f"{tag}_b_dh4")
    return dh, gw


def cf_layer_fwd(h, W, j, tag):
    u = mm(h, W["cf_pw1_w"], layer=j, bias=W["cf_pw1_b"][j], out_dtype=BF16, name=f"{tag}_pw1")
    c, s = cf_fwd(u, W["cf_dw_w"][j], W["cf_dw_b"][j].reshape(1, -1), W["cf_ln_g"][j].reshape(1, -1),
                  W["cf_ln_b"][j].reshape(1, -1), name=f"{tag}_conv")
    f = mm(s, W["cf_pw2_w"], layer=j, bias=W["cf_pw2_b"][j], out_dtype=BF16, name=f"{tag}_pw2")
    return f, dict(h=h, u=u, c=c, s=s)


def cf_layer_bwd(df, ctx, W, GB, j, tag):
    h = ctx["h"]
    ds = mm(df, W["cf_pw2_w"], layer=j, tb=True, name=f"{tag}_b_ds")
    GB["cf_pw2_w"] = mm(ctx["s"], df, ta=True, into=(GB["cf_pw2_w"], j, 0), name=f"{tag}_b_gpw2")
    g_b2 = colsum(df, name=f"{tag}_b_gb2")
    dc, dlg, dlb = cf_bwd_ln(ctx["c"], ds, W["cf_ln_g"][j].reshape(1, -1), W["cf_ln_b"][j].reshape(1, -1),
                             name=f"{tag}_b_ln")
    du, ddw, ddb = cf_bwd_conv(ctx["u"], dc, W["cf_dw_w"][j], name=f"{tag}_b_conv")
    dh = mm(du, W["cf_pw1_w"], layer=j, tb=True, out_dtype=BF16, name=f"{tag}_b_dh")
    GB["cf_pw1_w"] = mm(h, du, ta=True, into=(GB["cf_pw1_w"], j, 0), name=f"{tag}_b_gpw1")
    g_b1 = colsum(du, name=f"{tag}_b_gb1")
    return dh, dict(cf_pw1_b=g_b1[0], cf_dw_w=ddw, cf_dw_b=ddb[0], cf_ln_g=dlg[0], cf_ln_b=dlb[0], cf_pw2_b=g_b2[0])


def xa_layer_fwd(h, mem, W, i, tag):
    m = norm_fwd(mem, W["xa_mem_g"][i], name=f"{tag}_memnorm")
    kv = mm(m, W["xa_kv_w"], layer=i, out_dtype=BF16, name=f"{tag}_kv")
    q = mm(h, W["xa_q_w"], layer=i, out_dtype=BF16, name=f"{tag}_q")
    o = attn_fwd(q, kv, name=f"{tag}_attn")
    f = mm(o, W["xa_o_w"], layer=i, out_dtype=BF16, name=f"{tag}_o")
    return f, dict(h=h, m=m, kv=kv, q=q, o=o)


def xa_layer_bwd(df, ctx, mem, W, GB, i, tag):
    h = ctx["h"]
    do = mm(df, W["xa_o_w"], layer=i, tb=True, out_dtype=BF16, name=f"{tag}_b_do")
    GB["xa_o_w"] = mm(ctx["o"], df, ta=True, into=(GB["xa_o_w"], i, 0), name=f"{tag}_b_go")
    dq, dkv = attn_bwd(ctx["q"], ctx["kv"], do, name=f"{tag}_b_attn")
    dh = mm(dq, W["xa_q_w"], layer=i, tb=True, out_dtype=BF16, name=f"{tag}_b_dh")
    GB["xa_q_w"] = mm(h, dq, ta=True, into=(GB["xa_q_w"], i, 0), name=f"{tag}_b_gq")
    GB["xa_kv_w"] = mm(ctx["m"], dkv, ta=True, into=(GB["xa_kv_w"], i, 0), name=f"{tag}_b_gkv")
    dm = mm(dkv, W["xa_kv_w"], layer=i, tb=True, name=f"{tag}_b_dm")
    g_mg = norm_dg(mem, dm, name=f"{tag}_b_gmem")
    return dh, dict(xa_mem_g=g_mg[0])


def ffn_layer_fwd(h, W, i, tag):
    cw, cb = W["ffn_conv_w"][i], W["ffn_conv_b"][i].reshape(1, -1)
    u = mm(h, W["ffn_in_w"], layer=i, out_dtype=BF16, name=f"{tag}_in")
    act, c = ffn_act_fwd(u, cw, cb, name=f"{tag}_act")
    f = mm(act, W["ffn_out_w"], layer=i, out_dtype=BF16, name=f"{tag}_out")
    return f, dict(h=h, u=u, c=c, act=act)


def ffn_layer_bwd(df, ctx, W, GB, i, tag):
    h = ctx["h"]
    dact = mm(df, W["ffn_out_w"], layer=i, tb=True, out_dtype=BF16, name=f"{tag}_b_dact")
    GB["ffn_out_w"] = mm(ctx["act"], df, ta=True, into=(GB["ffn_out_w"], i, 0), name=f"{tag}_b_gout")
    du, dcw, dcb = ffn_act_bwd(ctx["u"], ctx["c"], dact, W["ffn_conv_w"][i], name=f"{tag}_b_act")
    dh = None
    for half in range(2):
        dh = mm(du, W["ffn_in_w"], a_idx=half, layer=i, tb=True, b_k0=half * D_FF, acc=dh,
                out_dtype=BF16 if half else F32, name=f"{tag}_b_dh{half}")
        GB["ffn_in_w"] = mm(h, du, ta=True, layer=half, into=(GB["ffn_in_w"], i, half * D_FF), name=f"{tag}_b_gin{half}")
    cat = lambda a: jnp.concatenate([a[0], a[1]], axis=-1)
    return dh, dict(ffn_conv_w=cat(dcw), ffn_conv_b=cat(dcb)[0])


def _sublayer_weights(i, s):
    if s == 0:
        return [("ssm_in_w", i // 2), ("ssm_out_w", i // 2)] if i % 2 == 0 else [("cf_pw1_w", i // 2), ("cf_pw2_w", i // 2)]
    return [(n, i) for n in (("xa_q_w", "xa_kv_w", "xa_o_w") if s == 1 else ("ffn_in_w", "ffn_out_w"))]


def local_step(x, mem, target, W, fetch=None, layer_done=None):
    subs = [(i, s) for i in range(DEPTH) for s in range(3)]
    ng = W["norm_g"]

    def fwd(i, s, h):
        tag = f"l{i}s{s}"
        if s == 0:
            return ssd_layer_fwd(h, W, i // 2, tag) if i % 2 == 0 else cf_layer_fwd(h, W, i // 2, tag)
        if s == 1:
            return xa_layer_fwd(h, mem, W, i, tag)
        return ffn_layer_fwd(h, W, i, tag)

    GB = {}

    early = {}

    def bwd(i, s, df, ctx):
        tag = f"l{i}s{s}"
        if s == 0 and i % 2 == 0:
            def done(g_in):
                early[i, s] = layer_done(i, s, GB, g_in)
                return early[i, s]
            return ssd_layer_bwd(df, ctx, W, GB, i // 2, tag, done if layer_done is not None else None)
        if s == 0:
            return cf_layer_bwd(df, ctx, W, GB, i // 2, tag)
        if s == 1:
            return xa_layer_bwd(df, ctx, mem, W, GB, i, tag)
        return ffn_layer_bwd(df, ctx, W, GB, i, tag)

    h = norm_fwd(x, ng[0, 0], name="norm0")
    saved = []
    dxp = loss = None
    for k, (i, s) in enumerate(subs):
        if fetch is not None:
            fetch(i, s, x)
        f, ctx = fwd(i, s, h)
        saved.append((x, f, ctx))
        if k + 1 < len(subs):
            ni, ns = subs[k + 1]
            x, h = bnd_fwd(x, f, ng[i, 2 * s + 1], ng[ni, 2 * ns], name=f"bnd{k}")
        else:
            dxp, loss = final_fwd(x, f, ng[i, 2 * s + 1], target, name="final")

    for n in BIG:
        if n != "ssm_in_w":
            GB[n] = jnp.zeros((len(W[n]), *W[n][0].shape), BF16)
    grads = {}

    def put(name, idx, val):
        grads.setdefault(name, {})[idx] = val

    i, s = subs[-1]
    top = bnd_bwd(dxp, post=(saved[-1][1], ng[i, 2 * s + 1]), name="bbnd_top")
    put("norm_g", (i, 2 * s + 1), top["dgpost"][0])
    df = top["df"]
    for k in range(len(subs) - 1, -1, -1):
        i, s = subs[k]
        xk, _, ctx = saved[k]
        dh, gw = bwd(i, s, df, ctx)
        for name, val in gw.items():
            put(name, i // 2 if name.startswith(("ssm_", "cf_")) else i, val)
        dep = early.get((i, s))
        if layer_done is not None and dep is None:
            dep = layer_done(i, s, GB, None)
        if k > 0:
            pi, ps = subs[k - 1]
            r = bnd_bwd(dxp, pre=(xk, ng[i, 2 * s], dh), post=(saved[k - 1][1], ng[pi, 2 * ps + 1]), dep=dep,
                        name=f"bbnd{k}")
            put("norm_g", (pi, 2 * ps + 1), r["dgpost"][0])
            df = r["df"]
        else:
            r = bnd_bwd(dxp, pre=(xk, ng[i, 2 * s], dh), dep=dep, name="bbnd0")
        put("norm_g", (i, 2 * s), r["dgpre"][0])
        dxp = r["dx"]

    out = {} if layer_done is not None else dict(GB)
    for name, d in grads.items():
        if name == "norm_g":
            out[name] = jnp.stack([jnp.stack([d[(i, t)] for t in range(6)]) for i in range(DEPTH)])
        elif d:
            out[name] = jnp.stack([d[j] for j in sorted(d)])
    return loss, dxp, out


ANY = pl.BlockSpec(memory_space=pl.ANY)


def _pos():
    return lax.axis_index("x"), lax.axis_index("y"), lax.axis_index("c")


def all_gather(shard, name):
    R, C = shard.shape

    def body(x_ref, out_ref, send_sems, recv_sems, local_sem):
        x, y, c = _pos()
        me, sibling = (x, y, c), (x, y, 1 - c)
        chips = [(1 - x, y), (x, 1 - y), (1 - x, 1 - y)]

        def slot(px, py, pc):
            return out_ref.at[4 * px + 2 * py + pc]

        def copy(k, block, to, src=None):
            return pltpu.make_async_remote_copy(
                src_ref=slot(*block) if src is None else src, dst_ref=slot(*block),
                send_sem=send_sems.at[k], recv_sem=recv_sems.at[k], device_id=to, device_id_type=MESH)

        mine = pltpu.make_async_copy(x_ref, slot(*me), local_sem)
        mine.start()
        first = [copy(0, me, sibling, src=x_ref)]
        first += [copy(1 + j, me, (*chip, c), src=x_ref) for j, chip in enumerate(chips)]
        for cp in first:
            cp.start()
        passed = [copy(4 + j, (*chip, c), sibling) for j, chip in enumerate(chips)]
        for j, chip in enumerate(chips):
            copy(1 + j, (*chip, c), me).wait_recv()
            passed[j].start()
        copy(0, sibling, me).wait_recv()
        for j, chip in enumerate(chips):
            copy(4 + j, (*chip, 1 - c), me).wait_recv()
        for cp in first + passed:
            cp.wait_send()
        mine.wait()

    return pl.pallas_call(
        body, name=name, out_shape=jax.ShapeDtypeStruct((N_DEV, R, C), shard.dtype),
        in_specs=[ANY], out_specs=ANY,
        scratch_shapes=[pltpu.SemaphoreType.DMA((7,)), pltpu.SemaphoreType.DMA((7,)), pltpu.SemaphoreType.DMA(())],
    )(shard)


def _win(ref, kind, k, a, b):
    if kind == "lead":
        return ref.at[k]
    if kind == "row":
        return ref.at[:, pl.ds(pl.multiple_of(k * a, 16), a), :]
    return ref.at[:, :, pl.ds(pl.multiple_of(k * b, LANE), b)]


def _full_shape(shard_shape, kind):
    n, a, b = shard_shape
    return {"lead": (N_DEV, n, a, b), "row": (n, N_DEV * a, b), "col": (n, a, N_DEV * b)}[kind]


HBM = pl.BlockSpec(memory_space=pltpu.HBM)
SEMS = pl.BlockSpec(memory_space=pltpu.SEMAPHORE)
DATAFLOW = pltpu.SideEffectType.DATAFLOW_SIDE_EFFECTING
N_PEER = N_DEV - 1


def _in_hbm(a):
    return pltpu.with_memory_space_constraint(a, pltpu.HBM)


def _peer(x, y, c, r):
    return ((1 - x) if r & 4 else x, (1 - y) if r & 2 else y, (1 - c) if r & 1 else c)


def _win2(ref, kind, k, a, b):
    if kind == "lead":
        return ref.at[k]
    if kind == "row":
        return ref.at[pl.ds(pl.multiple_of(k * a, 16), a), :]
    return ref.at[:, pl.ds(pl.multiple_of(k * b, LANE), b)]


def _zone_shape(kind, a, b):
    return {"lead": (N_DEV, a, b), "row": (N_DEV * a, b), "col": (a, N_DEV * b)}[kind]


def gather_start(shards, items, after, name):
    ns, nz, na = len(shards), len(items), len(after)
    zones = [lax.empty(_zone_shape(kind, a, b), shards[w].dtype) for w, l, kind, a, b in items]

    def body(*refs):
        x_refs = refs[:ns]
        send_sems, recv_sems, local_sems = refs[ns + nz + na:ns + nz + na + 3]
        z_refs = refs[ns + nz + na + 3 + ns:ns + nz + na + 3 + ns + nz]
        token = refs[-1]
        x, y, c = _pos()
        me = 4 * x + 2 * y + c
        for t, (w, l, kind, a, b) in enumerate(items):
            mine = _win2(z_refs[t], kind, me, a, b)
            pltpu.make_async_copy(x_refs[w].at[l], mine, local_sems.at[t]).start()
            for r in range(1, N_DEV):
                pltpu.make_async_remote_copy(
                    src_ref=x_refs[w].at[l], dst_ref=mine,
                    send_sem=send_sems.at[N_PEER * t + r - 1], recv_sem=recv_sems.at[N_PEER * t + r - 1],
                    device_id=_peer(x, y, c, r), device_id_type=MESH).start()
        token[...] = jnp.zeros_like(token)

    n_sem = N_PEER * nz
    outs = pl.pallas_call(
        body, name=name,
        out_shape=(pltpu.SemaphoreType.DMA((n_sem,)), pltpu.SemaphoreType.DMA((n_sem,)), pltpu.SemaphoreType.DMA((nz,)),
                   *[pltpu.HBM(s.shape, s.dtype) for s in shards], *[pltpu.HBM(z.shape, z.dtype) for z in zones],
                   jax.ShapeDtypeStruct((8, LANE), F32)),
        in_specs=[HBM] * (ns + nz) + [pl.BlockSpec(memory_space=pl.ANY)] * na,
        out_specs=(SEMS, SEMS, SEMS, *[HBM] * (ns + nz), pl.BlockSpec(memory_space=pltpu.VMEM)),
        input_output_aliases={i: 3 + i for i in range(ns + nz)},
        compiler_params=pltpu.CompilerParams(has_side_effects=DATAFLOW),
    )(*[_in_hbm(s) for s in shards], *[_in_hbm(z) for z in zones], *after)
    return outs[:3], list(outs[3:3 + ns]), list(outs[3 + ns:3 + ns + nz]), outs[-1]


def gather_wait(zones, idx, items, sems, after, keep, name):
    nz, nk = len(zones), len(keep)

    def body(*refs):
        z_refs = refs[:nz]
        send_sems, recv_sems, local_sems = refs[nz:nz + 3]
        x, y, c = _pos()
        me = 4 * x + 2 * y + c
        for z_ref, t in zip(z_refs, idx):
            w, l, kind, a, b = items[t]
            mine = _win2(z_ref, kind, me, a, b)
            pltpu.make_async_copy(mine, mine, local_sems.at[t]).wait()
            for r in range(1, N_DEV):
                peer = _peer(x, y, c, r)
                cp = pltpu.make_async_remote_copy(
                    src_ref=mine, dst_ref=_win2(z_ref, kind, 4 * peer[0] + 2 * peer[1] + peer[2], a, b),
                    send_sem=send_sems.at[N_PEER * t + r - 1], recv_sem=recv_sems.at[N_PEER * t + r - 1],
                    device_id=peer, device_id_type=MESH)
                cp.wait_send()
                cp.wait_recv()

    outs = pl.pallas_call(
        body, name=name, out_shape=tuple(pltpu.HBM(z.shape, z.dtype) for z in zones),
        in_specs=[HBM] * nz + [SEMS] * 3 + [pl.BlockSpec(memory_space=pl.ANY)] * (1 + nk),
        out_specs=tuple([HBM] * nz), input_output_aliases={i: i for i in range(nz)},
        compiler_params=pltpu.CompilerParams(has_side_effects=DATAFLOW),
    )(*zones, *sems, after, *keep)
    return list(outs)


def gather_now(shards, kinds, name):
    nw = len(shards)
    geo = [s.shape[1:] for s in shards]

    def body(*refs):
        x_refs, o_refs = refs[:nw], refs[nw:2 * nw]
        send_sems, recv_sems, local_sems = refs[2 * nw:]
        x, y, c = _pos()
        me, sibling = (x, y, c), (x, y, 1 - c)
        chips = [(1 - x, y), (x, 1 - y), (1 - x, 1 - y)]

        def slot(w, px, py, pc):
            return _win(o_refs[w], kinds[w], 4 * px + 2 * py + pc, *geo[w])

        def copy(w, k, block, to, src=None):
            return pltpu.make_async_remote_copy(
                src_ref=slot(w, *block) if src is None else src, dst_ref=slot(w, *block),
                send_sem=send_sems.at[7 * w + k], recv_sem=recv_sems.at[7 * w + k], device_id=to, device_id_type=MESH)

        mine = [pltpu.make_async_copy(x_refs[w], slot(w, *me), local_sems.at[w]) for w in range(nw)]
        for cp in mine:
            cp.start()
        first = []
        for w in range(nw):
            first.append(copy(w, 0, me, sibling, src=x_refs[w]))
            first += [copy(w, 1 + j, me, (*chip, c), src=x_refs[w]) for j, chip in enumerate(chips)]
        for cp in first:
            cp.start()
        passed = []
        for w in range(nw):
            for j, chip in enumerate(chips):
                copy(w, 1 + j, (*chip, c), me).wait_recv()
                cp = copy(w, 4 + j, (*chip, c), sibling)
                cp.start()
                passed.append(cp)
        for w in range(nw):
            copy(w, 0, sibling, me).wait_recv()
            for j, chip in enumerate(chips):
                copy(w, 4 + j, (*chip, 1 - c), me).wait_recv()
        for cp in first + passed:
            cp.wait_send()
        for cp in mine:
            cp.wait()

    return pl.pallas_call(
        body, name=name,
        out_shape=[jax.ShapeDtypeStruct(_full_shape(s.shape, k), s.dtype) for s, k in zip(shards, kinds)],
        in_specs=[ANY] * nw, out_specs=[ANY] * nw,
        scratch_shapes=[pltpu.SemaphoreType.DMA((7 * nw,)), pltpu.SemaphoreType.DMA((7 * nw,)),
                        pltpu.SemaphoreType.DMA((nw,))],
    )(*shards)


def _src_win(ref, l, kind, k, a, b):
    return _win2(ref if l is None else ref.at[l], kind, k, a, b)


def rs_start(srcs, items, name):
    ns, nz = len(srcs), len(items)
    zones = [lax.empty((N_PEER, a, b), srcs[w].dtype) for w, l, kind, a, b in items]

    def body(*refs):
        s_refs = refs[:ns]
        send_sems, recv_sems = refs[ns + nz], refs[ns + nz + 1]
        z_refs = refs[ns + nz + 2 + ns:ns + nz + 2 + ns + nz]
        token = refs[-1]
        x, y, c = _pos()
        for t, (w, l, kind, a, b) in enumerate(items):
            for r in range(1, N_DEV):
                peer = _peer(x, y, c, r)
                pltpu.make_async_remote_copy(
                    src_ref=_src_win(s_refs[w], l, kind, 4 * peer[0] + 2 * peer[1] + peer[2], a, b),
                    dst_ref=z_refs[t].at[r - 1],
                    send_sem=send_sems.at[N_PEER * t + r - 1], recv_sem=recv_sems.at[N_PEER * t + r - 1],
                    device_id=peer, device_id_type=MESH).start()
        token[...] = jnp.zeros_like(token)

    n_sem = N_PEER * nz
    outs = pl.pallas_call(
        body, name=name,
        out_shape=(pltpu.SemaphoreType.DMA((n_sem,)), pltpu.SemaphoreType.DMA((n_sem,)),
                   *[pltpu.HBM(s.shape, s.dtype) for s in srcs], *[pltpu.HBM(z.shape, z.dtype) for z in zones],
                   jax.ShapeDtypeStruct((8, LANE), F32)),
        in_specs=[HBM] * (ns + nz), out_specs=(SEMS, SEMS, *[HBM] * (ns + nz), pl.BlockSpec(memory_space=pltpu.VMEM)),
        input_output_aliases={i: 2 + i for i in range(ns + nz)},
        compiler_params=pltpu.CompilerParams(has_side_effects=DATAFLOW),
    )(*[_in_hbm(s) for s in srcs], *[_in_hbm(z) for z in zones])
    return outs[:2], list(outs[2:2 + ns]), list(outs[2 + ns:2 + ns + nz]), outs[-1]


def rs_wait(zones, items, sems, after, keep, name):
    nz, nk = len(zones), len(keep)

    def body(*refs):
        z_refs = refs[:nz]
        send_sems, recv_sems = refs[nz], refs[nz + 1]
        x, y, c = _pos()
        for t, z_ref in enumerate(z_refs):
            for r in range(1, N_DEV):
                cp = pltpu.make_async_remote_copy(
                    src_ref=z_ref.at[r - 1], dst_ref=z_ref.at[r - 1],
                    send_sem=send_sems.at[N_PEER * t + r - 1], recv_sem=recv_sems.at[N_PEER * t + r - 1],
                    device_id=_peer(x, y, c, r), device_id_type=MESH)
                cp.wait_send()
                cp.wait_recv()

    outs = pl.pallas_call(
        body, name=name, out_shape=tuple(pltpu.HBM(z.shape, z.dtype) for z in zones),
        in_specs=[HBM] * nz + [SEMS] * 2 + [pl.BlockSpec(memory_space=pl.ANY)] * (1 + nk),
        out_specs=tuple([HBM] * nz), input_output_aliases={i: i for i in range(nz)},
        compiler_params=pltpu.CompilerParams(has_side_effects=DATAFLOW),
    )(*zones, *sems, after, *keep)
    return list(outs)


def adam_rs(w, m, v, l, own, kind, zone, outs, name):
    n, a, b = w.shape
    ta = max(t for t in range(16, min(a, 256) + 1, 16) if a % t == 0)
    per = a // ta
    me = (4 * lax.axis_index("x") + 2 * lax.axis_index("y") + lax.axis_index("c")).astype(jnp.int32).reshape(1)

    def body(me_ref, w_ref, m_ref, v_ref, own_ref, z_ref, i0, i1, i2, i3, g_ref, d_ref, m2_ref, v2_ref):
        gv = own_ref[...].astype(F32)
        for k in range(N_PEER):
            gv = gv + z_ref[k].astype(F32)
        m2 = ADAM_B1 * m_ref[...] + (1.0 - ADAM_B1) * gv
        v2 = ADAM_B2 * v_ref[...] + (1.0 - ADAM_B2) * (gv * gv)
        m_hat = m2 / (1.0 - ADAM_B1 ** ADAM_STEP)
        v_hat = v2 / (1.0 - ADAM_B2 ** ADAM_STEP)
        g_ref[...] = gv
        d_ref[...] = -ADAM_LR * (m_hat / (jnp.sqrt(v_hat) + ADAM_EPS) + ADAM_WD * w_ref[...])
        m2_ref[...] = m2
        v2_ref[...] = v2

    spec = pl.BlockSpec((None, ta, b), lambda r, me_ref: (l, r, 0))
    if kind == "lead":
        own_spec = pl.BlockSpec((None, ta, b), lambda r, me_ref: (me_ref[0], r, 0))
    elif kind == "row":
        own_spec = pl.BlockSpec((None, ta, b), lambda r, me_ref: (l, me_ref[0] * per + r, 0))
    else:
        own_spec = pl.BlockSpec((None, ta, b), lambda r, me_ref: (l, r, me_ref[0]))
    return pl.pallas_call(
        body, name=name, out_shape=[jax.ShapeDtypeStruct((n, a, b), F32)] * 4,
        grid_spec=pltpu.PrefetchScalarGridSpec(
            num_scalar_prefetch=1, grid=(per,),
            in_specs=[spec] * 3 + [own_spec, pl.BlockSpec((N_PEER, ta, b), lambda r, me_ref: (0, r, 0))] + [ANY] * 4,
            out_specs=[spec] * 4),
        input_output_aliases={6 + k: k for k in range(4)},
        compiler_params=_cp("parallel"),
    )(me, w, m, v, own, zone, *outs)


def small_exchange(sh, rep, after, name):
    _, Rs, C = sh.shape
    na = len(after)

    def body(*refs):
        sh_ref, rep_ref = refs[:2]
        sh_out, rep_out, send_sems, recv_sems, local_sems = refs[2 + na:]
        x, y, c = _pos()
        me = 4 * x + 2 * y + c
        l1 = pltpu.make_async_copy(sh_ref.at[me], sh_out.at[me], local_sems.at[0])
        l2 = pltpu.make_async_copy(rep_ref, rep_out.at[me], local_sems.at[1])
        l1.start()
        l2.start()

        def flip(v, bit):
            return 1 - v if bit else v

        sends, recvs = [], []
        for r in range(1, N_DEV):
            peer = (flip(x, r & 4), flip(y, r & 2), flip(c, r & 1))
            pid = 4 * peer[0] + 2 * peer[1] + peer[2]
            k = 2 * (r - 1)
            mk = lambda src, dst, kk: pltpu.make_async_remote_copy(
                src_ref=src, dst_ref=dst, send_sem=send_sems.at[kk], recv_sem=recv_sems.at[kk],
                device_id=peer, device_id_type=MESH)
            sends += [mk(sh_ref.at[pid], sh_out.at[me], k), mk(rep_ref, rep_out.at[me], k + 1)]
            recvs += [mk(sh_ref.at[me], sh_out.at[pid], k), mk(rep_ref, rep_out.at[pid], k + 1)]
        for cp in sends:
            cp.start()
        for cp in recvs:
            cp.wait_recv()
        for cp in sends:
            cp.wait_send()
        l1.wait()
        l2.wait()

    n = 2 * (N_DEV - 1)
    return pl.pallas_call(
        body, name=name,
        out_shape=[jax.ShapeDtypeStruct((N_DEV, Rs, C), sh.dtype), jax.ShapeDtypeStruct((N_DEV, *rep.shape), rep.dtype)],
        in_specs=[ANY] * (2 + na), out_specs=[ANY, ANY],
        scratch_shapes=[pltpu.SemaphoreType.DMA((n,)), pltpu.SemaphoreType.DMA((n,)), pltpu.SemaphoreType.DMA((2,))],
    )(sh, rep, *after)


def adam_slots(w, m, v, slots, name):
    S, n, a, b = slots.shape
    ta = max(t for t in range(16, min(a, 512) + 1, 8)
             if a % t == 0 and t * S * b * slots.dtype.itemsize <= 4 * 1024 * 1024)

    def body(w_ref, m_ref, v_ref, s_ref, g_ref, d_ref, m2_ref, v2_ref):
        gv = s_ref[0].astype(F32)
        for k in range(1, S):
            gv = gv + s_ref[k].astype(F32)
        m2 = ADAM_B1 * m_ref[...] + (1.0 - ADAM_B1) * gv
        v2 = ADAM_B2 * v_ref[...] + (1.0 - ADAM_B2) * (gv * gv)
        m_hat = m2 / (1.0 - ADAM_B1 ** ADAM_STEP)
        v_hat = v2 / (1.0 - ADAM_B2 ** ADAM_STEP)
        g_ref[...] = gv
        d_ref[...] = -ADAM_LR * (m_hat / (jnp.sqrt(v_hat) + ADAM_EPS) + ADAM_WD * w_ref[...])
        m2_ref[...] = m2
        v2_ref[...] = v2

    spec = pl.BlockSpec((None, ta, b), lambda l, r: (l, r, 0))
    return pl.pallas_call(
        body, name=name, grid=(n, a // ta),
        in_specs=[spec] * 3 + [pl.BlockSpec((S, None, ta, b), lambda l, r: (0, l, r, 0))], out_specs=[spec] * 4,
        out_shape=[jax.ShapeDtypeStruct((n, a, b), F32)] * 4, compiler_params=_cp("parallel", "parallel"),
    )(w, m, v, slots)


WEIGHTS = ["norm_g", "ssm_in_w", "ssm_conv_w", "ssm_conv_b", "ssm_dt_bias", "ssm_A_log", "ssm_D", "ssm_norm_g",
           "ssm_out_w", "cf_pw1_w", "cf_pw1_b", "cf_dw_w", "cf_dw_b", "cf_ln_g", "cf_ln_b", "cf_pw2_w", "cf_pw2_b",
           "xa_mem_g", "xa_q_w", "xa_kv_w", "xa_o_w", "ffn_in_w", "ffn_conv_w", "ffn_conv_b", "ffn_out_w"]
ARGS = ["x", "mem"] + WEIGHTS + ["loss_target"] + ["m_" + n for n in WEIGHTS] + ["v_" + n for n in WEIGHTS]
BIG = {"ssm_in_w": "col", "ssm_out_w": "row", "cf_pw1_w": "col", "cf_pw2_w": "row", "xa_q_w": "row",
       "xa_kv_w": "col", "xa_o_w": "row", "ffn_in_w": "col", "ffn_out_w": "row"}
SMALL = ["norm_g", "ssm_conv_w", "cf_pw1_b", "cf_dw_w", "cf_dw_b", "cf_ln_g", "cf_ln_b", "cf_pw2_b", "ffn_conv_w"]
REP = ["ssm_conv_b", "ssm_dt_bias", "ssm_A_log", "ssm_D", "ssm_norm_g", "xa_mem_g", "ffn_conv_b"]
SMALL_W = 768
REP_W = 512


def _r8(n):
    return -(-n // 8) * 8


def _stack2d(arrs, wid):
    parts = []
    for a in arrs:
        r, c = a.shape[-2:]
        parts.append(jnp.pad(a, [(0, 0)] * (a.ndim - 2) + [(0, _r8(r) - r), (0, wid - c)]))
    return jnp.concatenate(parts, axis=-2)


def _unstack2d(buf, shapes2d):
    out, o = [], 0
    for r, c in shapes2d:
        out.append(buf[..., o:o + r, :c])
        o += _r8(r)
    return out


def _gathered_to_full(g):
    lead = g.shape[1:-1]
    return jnp.moveaxis(g, 0, -2).reshape(*lead, N_DEV * g.shape[-1])


def _full_to_slots(w):
    lead = w.shape[:-1]
    return jnp.moveaxis(w.reshape(*lead, N_DEV, w.shape[-1] // N_DEV), -2, 0)


def kernel(x, mem, norm_g, ssm_in_w, ssm_conv_w, ssm_conv_b, ssm_dt_bias, ssm_A_log, ssm_D, ssm_norm_g, ssm_out_w, cf_pw1_w, cf_pw1_b, cf_dw_w, cf_dw_b, cf_ln_g, cf_ln_b, cf_pw2_w, cf_pw2_b, xa_mem_g, xa_q_w, xa_kv_w, xa_o_w, ffn_in_w, ffn_conv_w, ffn_conv_b, ffn_out_w, loss_target, m_norm_g, m_ssm_in_w, m_ssm_conv_w, m_ssm_conv_b, m_ssm_dt_bias, m_ssm_A_log, m_ssm_D, m_ssm_norm_g, m_ssm_out_w, m_cf_pw1_w, m_cf_pw1_b, m_cf_dw_w, m_cf_dw_b, m_cf_ln_g, m_cf_ln_b, m_cf_pw2_w, m_cf_pw2_b, m_xa_mem_g, m_xa_q_w, m_xa_kv_w, m_xa_o_w, m_ffn_in_w, m_ffn_conv_w, m_ffn_conv_b, m_ffn_out_w, v_norm_g, v_ssm_in_w, v_ssm_conv_w, v_ssm_conv_b, v_ssm_dt_bias, v_ssm_A_log, v_ssm_D, v_ssm_norm_g, v_ssm_out_w, v_cf_pw1_w, v_cf_pw1_b, v_cf_dw_w, v_cf_dw_b, v_cf_ln_g, v_cf_ln_b, v_cf_pw2_w, v_cf_pw2_b, v_xa_mem_g, v_xa_q_w, v_xa_kv_w, v_xa_o_w, v_ffn_in_w, v_ffn_conv_w, v_ffn_conv_b, v_ffn_out_w):
    return _step(x, mem, norm_g, ssm_in_w, ssm_conv_w, ssm_conv_b, ssm_dt_bias, ssm_A_log, ssm_D, ssm_norm_g, ssm_out_w, cf_pw1_w, cf_pw1_b, cf_dw_w, cf_dw_b, cf_ln_g, cf_ln_b, cf_pw2_w, cf_pw2_b, xa_mem_g, xa_q_w, xa_kv_w, xa_o_w, ffn_in_w, ffn_conv_w, ffn_conv_b, ffn_out_w, loss_target, m_norm_g, m_ssm_in_w, m_ssm_conv_w, m_ssm_conv_b, m_ssm_dt_bias, m_ssm_A_log, m_ssm_D, m_ssm_norm_g, m_ssm_out_w, m_cf_pw1_w, m_cf_pw1_b, m_cf_dw_w, m_cf_dw_b, m_cf_ln_g, m_cf_ln_b, m_cf_pw2_w, m_cf_pw2_b, m_xa_mem_g, m_xa_q_w, m_xa_kv_w, m_xa_o_w, m_ffn_in_w, m_ffn_conv_w, m_ffn_conv_b, m_ffn_out_w, v_norm_g, v_ssm_in_w, v_ssm_conv_w, v_ssm_conv_b, v_ssm_dt_bias, v_ssm_A_log, v_ssm_D, v_ssm_norm_g, v_ssm_out_w, v_cf_pw1_w, v_cf_pw1_b, v_cf_dw_w, v_cf_dw_b, v_cf_ln_g, v_cf_ln_b, v_cf_pw2_w, v_cf_pw2_b, v_xa_mem_g, v_xa_q_w, v_xa_kv_w, v_xa_o_w, v_ffn_in_w, v_ffn_conv_w, v_ffn_conv_b, v_ffn_out_w)


def _step(*args):
    A = dict(zip(ARGS, args, strict=True))
    x, mem, target = A["x"][0], A["mem"][0], A["loss_target"][0]

    big = list(BIG)
    geo = [A[n].shape for n in big]
    kinds = ["row" if BIG[n] == "row" else ("col" if A[n].shape[-1] % LANE == 0 else "lead") for n in big]
    W = {n: A[n] for n in REP}
    small2d = [(A[n].size // A[n].shape[-1], A[n].shape[-1]) for n in SMALL]
    rep2d = [(A[n].size // REP_W, REP_W) if A[n].shape[-1] % REP_W == 0 else A[n].shape for n in REP] + [(1, 1)]
    stack_small = lambda pre: _stack2d([A[pre + n].reshape(rc) for n, rc in zip(SMALL, small2d)], SMALL_W)
    stack_rep = lambda pre: _stack2d([A[pre + n].reshape(rc) for n, rc in zip(REP, rep2d)] + [jnp.zeros((1, 1), F32)],
                                     REP_W)
    small_g = all_gather(stack_small(""), name="gather_small")
    for n, g in zip(SMALL, _unstack2d(small_g, small2d)):
        W[n] = _gathered_to_full(g.reshape(N_DEV, *A[n].shape))

    shards = [A[n].astype(BF16) for n in big]
    for n in big:
        W[n] = [None] * A[n].shape[0]
    first = _sublayer_weights(0, 0)
    got0 = gather_now([shards[big.index(n)][l:l + 1] for n, l in first], [kinds[big.index(n)] for n, l in first],
                      name="gather_first")
    for (n, l), g in zip(first, got0):
        W[n][l] = _gathered_to_full(g)[0] if kinds[big.index(n)] == "lead" else g[0]
    items, sub_items = [], {}
    for i in range(DEPTH):
        for s in range(3):
            sub_items[i, s] = []
            for n, l in _sublayer_weights(i, s) if (i, s) != (0, 0) else []:
                w = big.index(n)
                sub_items[i, s].append(len(items))
                items.append((w, l, kinds[w], *geo[w][1:]))
    sems, shards_thru, zones, token = gather_start(shards, items, [small_g, got0[0]], name="gather_start")
    x = x + token[0, 0]

    def fetch(i, s, x_in):
        ids = sub_items[i, s]
        if not ids:
            return
        got = gather_wait([zones[t] for t in ids], ids, items, sems, x_in, shards_thru if (i, s) == (DEPTH - 1, 2) else [],
                          name=f"gather_wait{i}{s}")
        for t, z in zip(ids, got):
            w, l, kind = items[t][:3]
            W[big[w]][l] = _gathered_to_full(z) if kind == "lead" else z

    sent = []
    final = {}

    def layer_done(i, s, GB, g_in):
        srcs, its = [], []
        for n, l in _sublayer_weights(i, s):
            w = big.index(n)
            if kinds[w] == "lead":
                srcs.append(_full_to_slots(g_in if n == "ssm_in_w" else GB[n][l]))
                its.append((len(srcs) - 1, None, "lead", *geo[w][1:], n, l))
            else:
                srcs.append(GB[n])
                its.append((len(srcs) - 1, l, kinds[w], *geo[w][1:], n, l))
        sems_i, thru, zones_i, token_i = rs_start(srcs, [it[:5] for it in its], name=f"rs_start{i}{s}")
        for it, s in zip(its, thru):
            if it[2] != "lead":
                GB[it[5]] = s
        sent.append((its, sems_i, [s for it, s in zip(its, thru) if it[2] == "lead"], zones_i))
        final["GB"] = GB
        return token_i

    loss, grad_x, G = local_step(x, mem, target, W, fetch, layer_done)

    sh = _stack2d([_full_to_slots(G[n]).reshape(N_DEV, *rc) for n, rc in zip(SMALL, small2d)], SMALL_W)
    rep = _stack2d([G[n].reshape(rc) for n, rc in zip(REP, rep2d)] + [loss[:, :1]], REP_W)

    res = {}
    GBf = final["GB"]
    bufs = {n: [lax.empty(A[n].shape, F32) for _ in range(4)] for n in big}
    sh_got = rep_got = None
    for i, (its, sems_i, lead_srcs, zones_i) in enumerate(sent):
        if i == len(sent) - 1:
            sh_got, rep_got = small_exchange(sh, rep, [bufs[n][0] for n in big], name="small_exchange")
        keep = lead_srcs + [GBf[it[5]] for it in its if it[2] != "lead"]
        zones_i = rs_wait(zones_i, [it[:5] for it in its], sems_i, grad_x if sh_got is None else sh_got, keep,
                          name=f"rs_wait{i}")
        lead_it = iter(lead_srcs)
        for it, z in zip(its, zones_i):
            n, l = it[5], it[6]
            own = next(lead_it) if it[2] == "lead" else GBf[n]
            bufs[n] = adam_rs(A[n], A["m_" + n], A["v_" + n], l, own, it[2], z, bufs[n], name=f"adam_{n}{l}")
    for n in big:
        res[n] = tuple(bufs[n])
    for names, shapes2d, stack, slots, tag in ((SMALL, small2d, stack_small, sh_got, "small"),
                                               (REP, rep2d, stack_rep, rep_got, "rep")):
        outs4 = adam_slots(stack("")[None], stack("m_")[None], stack("v_")[None], slots[:, None], name=f"adam_{tag}")
        parts = [_unstack2d(o[0], shapes2d) for o in outs4]
        for k, n in enumerate(names):
            res[n] = tuple(q[k].reshape(A[n].shape) for q in parts)
        if tag == "rep":
            total_loss = parts[0][-1][0, 0]

    outs = [total_loss, grad_x[None]]
    for k in range(4):
        outs += [res[n][k] for n in WEIGHTS]
    return tuple(outs)
```
